```python
import math
import jax, jax.numpy as jnp
from jax import lax
import numpy as np

D_MODEL = 1024
BATCH = 8
SEQ = 2048
DEPTH = 2

ATTN_HEADS = 8
ATTN_HEAD_DIM = 64
ATTN_WIDTH = ATTN_HEADS * ATTN_HEAD_DIM
DILATED_PATTERNS = ((128, 1), (512, 4), (2048, 16))
BLOCK = 128
NUM_BUCKETS = 32
MAX_DISTANCE = 2048
RET_HEADS = 4
RET_KEY_DIM = 64
RET_VALUE_DIM = 128
RET_WIDTH = RET_HEADS * RET_VALUE_DIM
RET_CHUNK = 128
ROPE_BASE = 10000.0
MIX_WIDTH = ATTN_WIDTH + RET_WIDTH
IN_SPLITS = (ATTN_WIDTH, ATTN_WIDTH, ATTN_WIDTH, RET_HEADS * RET_KEY_DIM,
             RET_HEADS * RET_KEY_DIM, RET_WIDTH, RET_WIDTH)
IN_WIDTH = sum(IN_SPLITS)
D_FF = 2816
N_EXPERTS = 8
TOP_K = 2
D_FF_EXPERT = 3584
N_DENSE = (DEPTH + 1) // 2
N_MOE = DEPTH // 2
EPS = 1e-6
NEG_INF = -1e30

kernel_name = 'hybrid_dilated_retention_moe_block'


def rms_norm(x, gain):
    xf = x.astype(jnp.float32)
    y = xf * lax.rsqrt(jnp.mean(xf * xf, axis=-1, keepdims=True) + EPS)
    return y * gain.astype(jnp.float32)


def t5_bucket(distance):
    max_exact = NUM_BUCKETS // 2
    n = jnp.maximum(distance, 0)
    nf = jnp.maximum(n.astype(jnp.float32), float(max_exact))
    large = max_exact + (jnp.log(nf / max_exact) / math.log(MAX_DISTANCE / max_exact)
                         * (NUM_BUCKETS - max_exact)).astype(jnp.int32)
    large = jnp.minimum(large, NUM_BUCKETS - 1)
    return jnp.where(n < max_exact, n, large)


def dilated_branch(q, k, v, bias_table, window, dilation):
    B, S, H, E = q.shape
    L = S // dilation
    w_sub = window // dilation
    nb = -(-L // BLOCK)
    pad_r = nb * BLOCK - L

    def to_sub(t):
        return t.reshape(B, L, dilation, H, E).transpose(0, 2, 3, 1, 4)

    qb = jnp.pad(to_sub(q), ((0, 0), (0, 0), (0, 0), (0, pad_r), (0, 0)))
    qb = qb.reshape(B, dilation, H, nb, BLOCK, E)

    def kv_windows(t):
        t = jnp.pad(to_sub(t), ((0, 0), (0, 0), (0, 0), (BLOCK, pad_r), (0, 0)))
        t = t.reshape(B, dilation, H, nb + 1, BLOCK, E)
        return jnp.concatenate([t[:, :, :, :-1], t[:, :, :, 1:]], axis=4)

    kw = kv_windows(k)
    vw = kv_windows(v)
    i = jnp.arange(BLOCK)[:, None]
    j = jnp.arange(2 * BLOCK)[None, :]
    rel = i - j + BLOCK
    kpos = jnp.arange(nb)[:, None, None] * BLOCK + j - BLOCK
    allowed = (rel >= 0) & (rel <= w_sub) & (kpos >= 0)
    bias = jnp.transpose(bias_table[t5_bucket(rel * dilation)], (2, 0, 1)).astype(jnp.float32)

    s = jnp.einsum('bdhnqe,bdhnke->bdhnqk', qb, kw) * (E ** -0.5) + bias[None, None, :, None]
    s = jnp.where(allowed[None, None, None], s, NEG_INF)
    m = jnp.max(s, axis=-1, keepdims=True)
    p = jnp.exp(s - m)
    den = jnp.sum(p, axis=-1, keepdims=True)
    o = jnp.einsum('bdhnqk,bdhnke->bdhnqe', p, vw) / den
    lse = (m + jnp.log(den))[..., 0]

    def from_sub(t):
        rest = t.shape[5:]
        t = t.reshape((B, dilation, H, nb * BLOCK) + rest)[:, :, :, :L]
        t = jnp.moveaxis(t, 3, 1)
        return t.reshape((B, S, H) + rest)

    return from_sub(o), from_sub(lse)


def dilated_attention(q, k, v, bias_table):
    outs, lses = [], []
    for window, dilation in DILATED_PATTERNS:
        o, l = dilated_branch(q, k, v, bias_table, window, dilation)
        outs.append(o)
        lses.append(l)
    outs = jnp.stack(outs, axis=0)
    weights = jax.nn.softmax(jnp.stack(lses, axis=0), axis=0)
    return jnp.sum(weights[..., None] * outs, axis=0)


def rotary(t, pos):
    half = t.shape[-1] // 2
    inv = ROPE_BASE ** (-jnp.arange(half, dtype=jnp.float32) / half)
    ang = pos[:, None] * inv[None, :]
    cos = jnp.cos(ang)[None, :, None, :]
    sin = jnp.sin(ang)[None, :, None, :]
    t1, t2 = t[..., :half], t[..., half:]
    return jnp.concatenate([t1 * cos - t2 * sin, t1 * sin + t2 * cos], axis=-1)


def retention_chunkwise(q, k, v):
    B, S, H, Dk = q.shape
    Dv = v.shape[-1]
    nc = S // RET_CHUNK
    log_g = jnp.log(1.0 - 2.0 ** (-5.0 - jnp.arange(H, dtype=jnp.float32)))
    idx = jnp.arange(RET_CHUNK, dtype=jnp.float32)
    diff = idx[:, None] - idx[None, :]
    decay_mask = jnp.where(diff >= 0, jnp.exp(jnp.maximum(diff, 0.0)[None] * log_g[:, None, None]), 0.0)
    q_decay = jnp.exp((idx + 1.0)[None, :] * log_g[:, None])[..., None]
    k_decay = jnp.exp((RET_CHUNK - 1.0 - idx)[None, :] * log_g[:, None])[..., None]
    chunk_decay = jnp.exp(RET_CHUNK * log_g)[:, None, None]

    def chunks(t):
        return t.reshape(B, nc, RET_CHUNK, H, t.shape[-1]).transpose(1, 0, 3, 2, 4)

    def step(state, inp):
        qi, ki, vi = inp
        inner = jnp.einsum('bhqd,bhkd->bhqk', qi, ki) * decay_mask
        inner = jnp.einsum('bhqk,bhkv->bhqv', inner, vi)
        cross = jnp.einsum('bhqd,bhdv->bhqv', qi * q_decay, state)
        new_state = state * chunk_decay + jnp.einsum('bhkd,bhkv->bhdv', ki * k_decay, vi)
        return new_state, inner + cross

    state0 = jnp.zeros((B, H, Dk, Dv), jnp.float32)
    _, ys = lax.scan(step, state0, (chunks(q), chunks(k), chunks(v)))
    return ys.transpose(1, 0, 3, 2, 4).reshape(B, S, H, Dv)


def token_mixer(h, w_in, q_gain, k_gain, ret_gain, w_out, bias_table):
    B, S, _ = h.shape
    proj = jnp.einsum('bsd,de->bse', h, w_in).astype(jnp.float32)
    q, k, v, rq, rk, rv, rg = jnp.split(proj, np.cumsum(IN_SPLITS)[:-1].tolist(), axis=-1)
    q = rms_norm(q.reshape(B, S, ATTN_HEADS, ATTN_HEAD_DIM), q_gain)
    k = rms_norm(k.reshape(B, S, ATTN_HEADS, ATTN_HEAD_DIM), k_gain)
    v = v.reshape(B, S, ATTN_HEADS, ATTN_HEAD_DIM)
    attn = dilated_attention(q, k, v, bias_table).reshape(B, S, ATTN_WIDTH)
    pos = jnp.arange(S, dtype=jnp.float32)
    rq = rotary(rq.reshape(B, S, RET_HEADS, RET_KEY_DIM), pos) * (RET_KEY_DIM ** -0.5)
    rk = rotary(rk.reshape(B, S, RET_HEADS, RET_KEY_DIM), pos)
    rv = rv.reshape(B, S, RET_HEADS, RET_VALUE_DIM)
    ret = retention_chunkwise(rq, rk, rv)
    mu = jnp.mean(ret, axis=-1, keepdims=True)
    var = jnp.mean(jnp.square(ret - mu), axis=-1, keepdims=True)
    ret = ((ret - mu) * lax.rsqrt(var + EPS)).reshape(B, S, RET_WIDTH) * ret_gain.astype(jnp.float32)
    ret = jax.nn.silu(rg) * ret
    mixed = jnp.concatenate([attn, ret], axis=-1).astype(h.dtype)
    return jnp.einsum('bsm,md->bsd', mixed, w_out)


def swiglu(h, w1, w3, w2):
    a = jnp.einsum('bsd,df->bsf', h, w1)
    b = jnp.einsum('bsd,df->bsf', h, w3)
    return jnp.einsum('bsf,fd->bsd', jax.nn.silu(a) * b, w2)


def moe_swiglu(h, router, w1, w3, w2):
    logits = jnp.einsum('bsd,de->bse', h, router).astype(jnp.float32)
    top_vals, top_idx = lax.top_k(logits, TOP_K)
    top_w = jax.nn.softmax(top_vals, axis=-1)
    gates = jnp.sum(jax.nn.one_hot(top_idx, N_EXPERTS, dtype=jnp.float32) * top_w[..., None], axis=-2)
    out = jnp.zeros(h.shape, jnp.float32)
    for e in range(N_EXPERTS):
        out = out + gates[..., e:e + 1] * swiglu(h, w1[e], w3[e], w2[e]).astype(jnp.float32)
    return out


def setup_inputs(seed: int = 0) -> dict:
    key = jax.random.key(seed)
    ks = jax.random.split(key, 20)
    f32 = jnp.float32
    nrm = lambda k, shape, scale: jax.random.normal(k, shape, f32) * scale
    return {
        'x': nrm(ks[0], (BATCH, SEQ, D_MODEL), 1.0),
        'c': nrm(ks[1], (BATCH, D_MODEL), 1.0),
        'rel_bias_table': nrm(ks[2], (NUM_BUCKETS, ATTN_HEADS), 0.3),
        'norm_mix': 1.0 + nrm(ks[3], (DEPTH, D_MODEL), 0.02),
        'norm_ffn': 1.0 + nrm(ks[4], (DEPTH, D_MODEL), 0.02),
        'w_mod': nrm(ks[5], (DEPTH, D_MODEL, 6 * D_MODEL), 0.3 * D_MODEL ** -0.5),
        'b_mod': nrm(ks[6], (DEPTH, 6 * D_MODEL), 0.02),
        'w_in': nrm(ks[7], (DEPTH, D_MODEL, IN_WIDTH), D_MODEL ** -0.5),
        'q_gain': 1.0 + nrm(ks[8], (DEPTH, ATTN_HEAD_DIM), 0.02),
        'k_gain': 1.0 + nrm(ks[9], (DEPTH, ATTN_HEAD_DIM), 0.02),
        'ret_gain': 1.0 + nrm(ks[10], (DEPTH, RET_WIDTH), 0.02),
        'w_out': nrm(ks[11], (DEPTH, MIX_WIDTH, D_MODEL), MIX_WIDTH ** -0.5),
        'ffn_w1': nrm(ks[12], (N_DENSE, D_MODEL, D_FF), D_MODEL ** -0.5),
        'ffn_w3': nrm(ks[13], (N_DENSE, D_MODEL, D_FF), D_MODEL ** -0.5),
        'ffn_w2': nrm(ks[14], (N_DENSE, D_FF, D_MODEL), D_FF ** -0.5),
        'moe_router': nrm(ks[15], (N_MOE, D_MODEL, N_EXPERTS), D_MODEL ** -0.5),
        'moe_w1': nrm(ks[16], (N_MOE, N_EXPERTS, D_MODEL, D_FF_EXPERT), D_MODEL ** -0.5),
        'moe_w3': nrm(ks[17], (N_MOE, N_EXPERTS, D_MODEL, D_FF_EXPERT), D_MODEL ** -0.5),
        'moe_w2': nrm(ks[18], (N_MOE, N_EXPERTS, D_FF_EXPERT, D_MODEL), D_FF_EXPERT ** -0.5),
    }


def reference(x, c, rel_bias_table, norm_mix, norm_ffn, w_mod, b_mod, w_in, q_gain, k_gain,
              ret_gain, w_out, ffn_w1, ffn_w3, ffn_w2, moe_router, moe_w1, moe_w3, moe_w2):
    dt = x.dtype
    c_act = jax.nn.silu(c.astype(jnp.float32))
    for layer in range(DEPTH):
        mod = jnp.einsum('bd,de->be', c_act, w_mod[layer].astype(jnp.float32)) + b_mod[layer].astype(jnp.float32)
        shift_m, scale_m, gate_m, shift_f, scale_f, gate_f = jnp.split(mod[:, None, :], 6, axis=-1)
        h = (rms_norm(x, norm_mix[layer]) * (1.0 + scale_m) + shift_m).astype(dt)
        mix = token_mixer(h, w_in[layer], q_gain[layer], k_gain[layer], ret_gain[layer],
                          w_out[layer], rel_bias_table)
        x = x + (gate_m * mix.astype(jnp.float32)).astype(dt)
        h = (rms_norm(x, norm_ffn[layer]) * (1.0 + scale_f) + shift_f).astype(dt)
        if layer % 2 == 0:
            i = layer // 2
            f = swiglu(h, ffn_w1[i], ffn_w3[i], ffn_w2[i])
        else:
            i = layer // 2
            f = moe_swiglu(h, moe_router[i], moe_w1[i], moe_w3[i], moe_w2[i])
        x = x + (gate_f * f.astype(jnp.float32)).astype(dt)
    return x
```

```python
import functools
import math

import jax
import jax.numpy as jnp
import numpy as np
from jax import lax
from jax.experimental import pallas as pl
from jax.experimental.pallas import tpu as pltpu

D_MODEL = 1024
SEQ = 2048
ATTN_HEADS = 8
ATTN_HEAD_DIM = 64
ATTN_WIDTH = ATTN_HEADS * ATTN_HEAD_DIM
DILATED_PATTERNS = ((128, 1), (512, 4), (2048, 16))
BLOCK = 128
NUM_BUCKETS = 32
MAX_DISTANCE = 2048
RET_HEADS = 4
RET_KEY_DIM = 64
RET_VALUE_DIM = 128
RET_WIDTH = RET_HEADS * RET_VALUE_DIM
RET_QK_WIDTH = RET_HEADS * RET_KEY_DIM
RET_CHUNK = 128
ROPE_BASE = 10000.0
IN_WIDTH = 3 * ATTN_WIDTH + 2 * RET_QK_WIDTH + 2 * RET_WIDTH
RET_IN_WIDTH = IN_WIDTH - 3 * ATTN_WIDTH
N_EXPERTS = 8
EPS = 1e-6
NEG_INF = -1e30

LANES = 128
VMEM_LIMIT = 56 * 1024 * 1024

BF16 = jnp.bfloat16
F32 = jnp.float32

TM_PROJ = 256
TM_FFN = 512
TF_FFN = 1408
TR_MOE = 256
TF_MOE = 512
RET_ROWS = 512
RES16 = 4


def _cparams(sem):
    return pltpu.CompilerParams(dimension_semantics=sem, vmem_limit_bytes=VMEM_LIMIT)


def _dot(a, b):
    return jnp.dot(a, b, preferred_element_type=F32)


def _dot_nt(a, b):
    return lax.dot_general(a, b, (((1,), (1,)), ((), ())), preferred_element_type=F32)


def _dot_tn(a, b):
    return lax.dot_general(a, b, (((0,), (0,)), ((), ())), preferred_element_type=F32)


def _split_bf16(v):
    hi = v.astype(BF16)
    lo = (v - hi.astype(F32)).astype(BF16)
    return hi, lo


def _silu(v):
    return v * (1.0 / (1.0 + jnp.exp(-v)))


def _modulated_norm(x, gain, scale, shift):
    ms = jnp.mean(x * x, axis=-1, keepdims=True)
    y = x * lax.rsqrt(ms + EPS) * gain
    return y * (1.0 + scale) + shift


def _mod_kernel(c_ref, w_ref, b_ref, o_ref):
    ca = _silu(c_ref[...]).astype(BF16)
    o_ref[...] = _dot(ca, w_ref[...].astype(BF16)) + b_ref[...]


def _modulation(c, w_mod, b_mod):
    depth, _, width = w_mod.shape
    batch = c.shape[0]
    tn = 1536
    return pl.pallas_call(
        _mod_kernel,
        out_shape=jax.ShapeDtypeStruct((depth, batch, width), F32),
        grid=(depth, width // tn),
        in_specs=[
            pl.BlockSpec((batch, D_MODEL), lambda l, n: (0, 0)),
            pl.BlockSpec((None, D_MODEL, tn), lambda l, n: (l, 0, n)),
            pl.BlockSpec((None, 1, tn), lambda l, n: (l, 0, n)),
        ],
        out_specs=pl.BlockSpec((None, batch, tn), lambda l, n: (l, 0, n)),
        compiler_params=_cparams(("arbitrary", "arbitrary")),
        name="adaln_modulation",
    )(c, w_mod, b_mod.reshape(depth, 1, width))


def _bias_kernel(table_ref, bucket_ref, o_ref):
    h = pl.program_id(1)
    bucket = bucket_ref[...]
    acc = jnp.full(bucket.shape, NEG_INF, F32)
    for b in range(NUM_BUCKETS):
        acc = jnp.where(bucket == b, table_ref[b, h], acc)
    o_ref[...] = acc


def _bias_masks(rel_bias_table):
    i = jnp.arange(BLOCK)[:, None]
    j = jnp.arange(2 * BLOCK)[None, :]
    rel = i - j + BLOCK
    buckets = []
    for window, dilation in DILATED_PATTERNS:
        w_sub = window // dilation
        dist = rel * dilation
        max_exact = NUM_BUCKETS // 2
        n = jnp.maximum(dist, 0)
        nf = jnp.maximum(n.astype(F32), float(max_exact))
        large = max_exact + (jnp.log(nf / max_exact) / math.log(MAX_DISTANCE / max_exact)
                             * (NUM_BUCKETS - max_exact)).astype(jnp.int32)
        large = jnp.minimum(large, NUM_BUCKETS - 1)
        bucket = jnp.where(n < max_exact, n, large)
        allowed = (rel >= 0) & (rel <= w_sub)
        buckets.append(jnp.where(allowed, bucket, -1))
    buckets = jnp.stack(buckets).astype(jnp.int32)
    n_pat = len(DILATED_PATTERNS)
    return pl.pallas_call(
        _bias_kernel,
        out_shape=jax.ShapeDtypeStruct((n_pat, ATTN_HEADS, BLOCK, 2 * BLOCK), F32),
        grid=(n_pat, ATTN_HEADS),
        in_specs=[
            pl.BlockSpec(memory_space=pltpu.SMEM),
            pl.BlockSpec((None, BLOCK, 2 * BLOCK), lambda p, h: (p, 0, 0)),
        ],
        out_specs=pl.BlockSpec((None, None, BLOCK, 2 * BLOCK), lambda p, h: (p, h, 0, 0)),
        compiler_params=_cparams(("arbitrary", "arbitrary")),
        name="relative_bias_masks",
    )(rel_bias_table, buckets)


def _in_proj_kernel(x_ref, mod_ref, gain_ref, w_ref, qg_ref, kg_ref, grp_ref,
                    q_ref, k_ref, v_ref, r_ref):
    mod = mod_ref[...]
    h = _modulated_norm(x_ref[...], gain_ref[...], mod[1:2], mod[0:1]).astype(BF16)
    proj = _dot(h, w_ref[...])
    grp = grp_ref[...]

    def head_norm(t, gain):
        hi, lo = _split_bf16(t * t)
        ss = _dot(hi, grp) + _dot(lo, grp)
        return t * lax.rsqrt(ss * (1.0 / ATTN_HEAD_DIM) + EPS) * gain

    q = head_norm(proj[:, :ATTN_WIDTH], qg_ref[...])
    q_ref[...] = (q * (ATTN_HEAD_DIM ** -0.5)).astype(BF16)
    k_ref[...] = head_norm(proj[:, ATTN_WIDTH:2 * ATTN_WIDTH], kg_ref[...]).astype(BF16)
    v_ref[...] = proj[:, 2 * ATTN_WIDTH:3 * ATTN_WIDTH].astype(BF16)
    r_ref[...] = proj[:, 3 * ATTN_WIDTH:]


def _in_proj(x, mod, gain, w_in, q_gain, k_gain):
    tokens = x.shape[0]
    tm = TM_PROJ
    per_seq = SEQ // tm
    grp = np.kron(np.eye(ATTN_HEADS), np.ones((ATTN_HEAD_DIM, ATTN_HEAD_DIM))).astype(np.float32)
    row = lambda i: (i, 0)
    const = lambda i: (0, 0)
    return pl.pallas_call(
        _in_proj_kernel,
        out_shape=(
            jax.ShapeDtypeStruct((tokens, ATTN_WIDTH), BF16),
            jax.ShapeDtypeStruct((tokens, ATTN_WIDTH), BF16),
            jax.ShapeDtypeStruct((tokens, ATTN_WIDTH), BF16),
            jax.ShapeDtypeStruct((tokens, RET_IN_WIDTH), F32),
        ),
        grid=(tokens // tm,),
        in_specs=[
            pl.BlockSpec((tm, D_MODEL), row),
            pl.BlockSpec((None, 6, D_MODEL), lambda i: (i // per_seq, 0, 0)),
            pl.BlockSpec((1, D_MODEL), const),
            pl.BlockSpec((D_MODEL, IN_WIDTH), const),
            pl.BlockSpec((1, ATTN_WIDTH), const),
            pl.BlockSpec((1, ATTN_WIDTH), const),
            pl.BlockSpec((ATTN_WIDTH, ATTN_WIDTH), const),
        ],
        out_specs=(
            pl.BlockSpec((tm, ATTN_WIDTH), row),
            pl.BlockSpec((tm, ATTN_WIDTH), row),
            pl.BlockSpec((tm, ATTN_WIDTH), row),
            pl.BlockSpec((tm, RET_IN_WIDTH), row),
        ),
        compiler_params=_cparams(("arbitrary",)),
        name="in_projection",
    )(x, mod, gain.reshape(1, D_MODEL), w_in.astype(BF16),
      jnp.tile(q_gain, ATTN_HEADS).reshape(1, ATTN_WIDTH),
      jnp.tile(k_gain, ATTN_HEADS).reshape(1, ATTN_WIDTH),
      jnp.asarray(grp, BF16))


def _attn_block(q_blk, k_win, v_win, bm_ref, col0, o_ref, l_ref, row0, c0, l0):
    width = k_win.shape[0]
    lane = lax.broadcasted_iota(jnp.int32, (BLOCK, LANES), 1)
    low = lane < ATTN_HEAD_DIM
    head_masks = (jnp.where(low, 1.0, 0.0).astype(BF16), jnp.where(low, 0.0, 1.0).astype(BF16))
    lse_tile = jnp.zeros((BLOCK, LANES), F32)
    for hp in range(ATTN_HEADS // 2):
        sl = slice(hp * LANES, (hp + 1) * LANES)
        qp, kp, vp = q_blk[:, sl], k_win[:, sl], v_win[:, sl]
        outs = []
        for hh in range(2):
            head = 2 * hp + hh
            qm = qp * head_masks[hh]
            s = _dot_nt(qm, kp) + bm_ref[head, :, col0:col0 + width]
            m = jnp.max(s, axis=-1, keepdims=True)
            p = jnp.exp(s - m)
            den = jnp.sum(p, axis=-1, keepdims=True)
            outs.append(_dot(p.astype(BF16), vp) / den)
            lse_tile = jnp.where(lane == head, m + jnp.log(den), lse_tile)
        o_ref[pl.ds(row0, BLOCK), c0 + hp * LANES:c0 + (hp + 1) * LANES] = jnp.where(low, outs[0], outs[1])
    l_ref[pl.ds(row0, BLOCK), l0:l0 + LANES] = lse_tile


def _attn_kernel(q_ref, k_ref, v_ref, bm_ref, o_ref, l_ref, *, n_res, n_blocks):
    for r in range(n_res):
        c0 = r * ATTN_WIDTH
        cs = slice(c0, c0 + ATTN_WIDTH)
        _attn_block(q_ref[0:BLOCK, cs], k_ref[0:BLOCK, cs], v_ref[0:BLOCK, cs],
                    bm_ref, BLOCK, o_ref, l_ref, 0, c0, r * LANES)
        if n_blocks > 1:
            def body(n, carry):
                q0 = pl.multiple_of(n * BLOCK, BLOCK)
                w0 = pl.multiple_of((n - 1) * BLOCK, BLOCK)
                _attn_block(q_ref[pl.ds(q0, BLOCK), cs], k_ref[pl.ds(w0, 2 * BLOCK), cs],
                            v_ref[pl.ds(w0, 2 * BLOCK), cs], bm_ref, 0, o_ref, l_ref, q0, c0, r * LANES)
                return carry
            lax.fori_loop(1, n_blocks, body, 0)


def _dilated_branch(q, k, v, bm, dilation, n_res):
    batch = q.shape[0]
    sub_len = SEQ // dilation
    n_blocks = sub_len // BLOCK
    view = lambda t: t.reshape(batch, sub_len, dilation * ATTN_WIDTH)
    blk = pl.BlockSpec((None, sub_len, n_res * ATTN_WIDTH), lambda b, r: (b, 0, r))
    out, lse = pl.pallas_call(
        functools.partial(_attn_kernel, n_res=n_res, n_blocks=n_blocks),
        out_shape=(
            jax.ShapeDtypeStruct((batch, sub_len, dilation * ATTN_WIDTH), F32),
            jax.ShapeDtypeStruct((batch, sub_len, dilation * LANES), F32),
        ),
        grid=(batch, dilation // n_res),
        in_specs=[blk, blk, blk,
                  pl.BlockSpec((ATTN_HEADS, BLOCK, 2 * BLOCK), lambda b, r: (0, 0, 0))],
        out_specs=(
            pl.BlockSpec((None, sub_len, n_res * ATTN_WIDTH), lambda b, r: (b, 0, r)),
            pl.BlockSpec((None, sub_len, n_res * LANES), lambda b, r: (b, 0, r)),
        ),
        compiler_params=_cparams(("arbitrary", "arbitrary")),
        name=f"dilated_attention_d{dilation}",
    )(view(q), view(k), view(v), bm)
    return out.reshape(batch * SEQ, ATTN_WIDTH), lse.reshape(batch * SEQ, LANES)


def _retention_kernel(r_ref, cos_ref, sin_ref, dmask_ref, qdec_ref, kdec_ref, cdec_ref, gain_ref,
                      o_ref, state_ref):
    @pl.when(pl.program_id(1) == 0)
    def _():
        state_ref[...] = jnp.zeros_like(state_ref)

    lane = lax.broadcasted_iota(jnp.int32, (RET_CHUNK, LANES), 1)
    low = lane < RET_KEY_DIM
    first_half = (lane % RET_KEY_DIM) < (RET_KEY_DIM // 2)

    def rotate(t, cos, sin):
        partner = jnp.where(first_half, pltpu.roll(t, LANES - RET_KEY_DIM // 2, 1),
                            pltpu.roll(t, RET_KEY_DIM // 2, 1))
        return t * cos + partner * sin

    for c in range(RET_ROWS // RET_CHUNK):
        rows = slice(c * RET_CHUNK, (c + 1) * RET_CHUNK)
        for hp in range(RET_HEADS // 2):
            qs = slice(hp * LANES, (hp + 1) * LANES)
            ks = slice(RET_QK_WIDTH + hp * LANES, RET_QK_WIDTH + (hp + 1) * LANES)
            cos, sin = cos_ref[rows, qs], sin_ref[rows, qs]
            q_pair = rotate(r_ref[rows, qs], cos, sin) * (RET_KEY_DIM ** -0.5)
            k_pair = rotate(r_ref[rows, ks], cos, sin)
            for hh in range(2):
                head = 2 * hp + hh
                vs = slice(2 * RET_QK_WIDTH + head * LANES, 2 * RET_QK_WIDTH + (head + 1) * LANES)
                gs = slice(2 * RET_QK_WIDTH + RET_WIDTH + head * LANES,
                           2 * RET_QK_WIDTH + RET_WIDTH + (head + 1) * LANES)
                keep = low if hh == 0 else jnp.logical_not(low)
                qm = jnp.where(keep, q_pair, 0.0)
                vb = r_ref[rows, vs].astype(BF16)
                state = state_ref[head]
                inner = _dot_nt(qm.astype(BF16), k_pair.astype(BF16)) * dmask_ref[head]
                y = _dot(inner.astype(BF16), vb)
                y = y + _dot((qm * qdec_ref[head]).astype(BF16), state.astype(BF16))
                state_ref[head] = state * cdec_ref[head] + _dot_tn((k_pair * kdec_ref[head]).astype(BF16), vb)
                mu = jnp.mean(y, axis=-1, keepdims=True)
                yc = y - mu
                var = jnp.mean(yc * yc, axis=-1, keepdims=True)
                yn = yc * lax.rsqrt(var + EPS) * gain_ref[:, head * LANES:(head + 1) * LANES]
                o_ref[rows, head * LANES:(head + 1) * LANES] = (_silu(r_ref[rows, gs]) * yn).astype(BF16)


def _retention_tables():
    half = RET_KEY_DIM // 2
    pos = jnp.arange(SEQ, dtype=F32)
    inv = ROPE_BASE ** (-jnp.arange(half, dtype=F32) / half)
    ang = pos[:, None] * inv[None, :]
    cos, sin = jnp.cos(ang), jnp.sin(ang)
    cos_full = jnp.tile(jnp.concatenate([cos, cos], axis=-1), (1, RET_HEADS))
    sin_signed = jnp.tile(jnp.concatenate([-sin, sin], axis=-1), (1, RET_HEADS))
    log_g = jnp.log(1.0 - 2.0 ** (-5.0 - jnp.arange(RET_HEADS, dtype=F32)))
    idx = jnp.arange(RET_CHUNK, dtype=F32)
    diff = idx[:, None] - idx[None, :]
    dmask = jnp.where(diff >= 0, jnp.exp(jnp.maximum(diff, 0.0)[None] * log_g[:, None, None]), 0.0)
    q_decay = jnp.exp((idx + 1.0)[None, :] * log_g[:, None])[..., None]
    k_decay = jnp.exp((RET_CHUNK - 1.0 - idx)[None, :] * log_g[:, None])[..., None]
    chunk_decay = jnp.exp(RET_CHUNK * log_g)[:, None, None]
    full = (RET_HEADS, RET_CHUNK, LANES)
    return (cos_full, sin_signed, dmask, jnp.broadcast_to(q_decay, full),
            jnp.broadcast_to(k_decay, full), jnp.broadcast_to(chunk_decay, full))


def _retention(ret_in, ret_gain):
    tokens = ret_in.shape[0]
    batch = tokens // SEQ
    per_seq = SEQ // RET_ROWS
    cos, sin, dmask, qdec, kdec, cdec = _retention_tables()
    tab = pl.BlockSpec((RET_ROWS, RET_QK_WIDTH), lambda b, j: (j, 0))
    const3 = pl.BlockSpec((RET_HEADS, RET_CHUNK, LANES), lambda b, j: (0, 0, 0))
    return pl.pallas_call(
        _retention_kernel,
        out_shape=jax.ShapeDtypeStruct((tokens, RET_WIDTH), BF16),
        grid=(batch, per_seq),
        in_specs=[
            pl.BlockSpec((RET_ROWS, RET_IN_WIDTH), lambda b, j: (b * per_seq + j, 0)),
            tab, tab, const3, const3, const3, const3,
            pl.BlockSpec((1, RET_WIDTH), lambda b, j: (0, 0)),
        ],
        out_specs=pl.BlockSpec((RET_ROWS, RET_WIDTH), lambda b, j: (b * per_seq + j, 0)),
        scratch_shapes=[pltpu.VMEM((RET_HEADS, LANES, RET_VALUE_DIM), F32)],
        compiler_params=_cparams(("arbitrary", "arbitrary")),
        name="retention",
    )(ret_in, cos, sin, dmask, qdec, kdec, cdec, ret_gain.reshape(1, RET_WIDTH))


def _out_proj_kernel(o1_ref, o2_ref, o3_ref, l1_ref, l2_ref, l3_ref, ret_ref, x_ref, mod_ref,
                     w_ref, exp_ref, out_ref):
    l1, l2, l3 = l1_ref[...], l2_ref[...], l3_ref[...]
    m = jnp.maximum(jnp.maximum(l1, l2), l3)
    e1, e2, e3 = jnp.exp(l1 - m), jnp.exp(l2 - m), jnp.exp(l3 - m)
    den = e1 + e2 + e3
    expand = exp_ref[...]

    def widen(w):
        hi, lo = _split_bf16(w)
        return _dot(hi, expand) + _dot(lo, expand)

    attn = widen(e1 / den) * o1_ref[...] + widen(e2 / den) * o2_ref[...] + widen(e3 / den) * o3_ref[...]
    mix = _dot(attn.astype(BF16), w_ref[:ATTN_WIDTH, :]) + _dot(ret_ref[...], w_ref[ATTN_WIDTH:, :])
    out_ref[...] = x_ref[...] + mod_ref[2:3, :] * mix


def _out_proj(outs, lses, ret, x, mod, w_out):
    tokens = x.shape[0]
    tm = TM_PROJ
    per_seq = SEQ // tm
    expand = np.zeros((LANES, ATTN_WIDTH), np.float32)
    for h in range(ATTN_HEADS):
        expand[h, h * ATTN_HEAD_DIM:(h + 1) * ATTN_HEAD_DIM] = 1.0
    row = lambda i: (i, 0)
    const = lambda i: (0, 0)
    o_spec = pl.BlockSpec((tm, ATTN_WIDTH), row)
    l_spec = pl.BlockSpec((tm, LANES), row)
    return pl.pallas_call(
        _out_proj_kernel,
        out_shape=jax.ShapeDtypeStruct((tokens, D_MODEL), F32),
        grid=(tokens // tm,),
        in_specs=[o_spec, o_spec, o_spec, l_spec, l_spec, l_spec,
                  pl.BlockSpec((tm, RET_WIDTH), row),
                  pl.BlockSpec((tm, D_MODEL), row),
                  pl.BlockSpec((None, 6, D_MODEL), lambda i: (i // per_seq, 0, 0)),
                  pl.BlockSpec((D_MODEL, D_MODEL), const),
                  pl.BlockSpec((LANES, ATTN_WIDTH), const)],
        out_specs=pl.BlockSpec((tm, D_MODEL), row),
        compiler_params=_cparams(("arbitrary",)),
        name="out_projection",
    )(*outs, *lses, ret, x, mod, w_out.astype(BF16), jnp.asarray(expand, BF16))


def _ffn_kernel(x_ref, mod_ref, gain_ref, w1_ref, w3_ref, w2_ref, out_ref, h_ref, acc_ref):
    f = pl.program_id(1)

    @pl.when(f == 0)
    def _():
        mod = mod_ref[...]
        h_ref[...] = _modulated_norm(x_ref[...], gain_ref[...], mod[4:5], mod[3:4]).astype(BF16)
        acc_ref[...] = jnp.zeros_like(acc_ref)

    h = h_ref[...]
    z = (_silu(_dot(h, w1_ref[...])) * _dot(h, w3_ref[...])).astype(BF16)
    acc_ref[...] += _dot(z, w2_ref[...])

    @pl.when(f == pl.num_programs(1) - 1)
    def _():
        out_ref[...] = x_ref[...] + mod_ref[5:6, :] * acc_ref[...]


def _dense_ffn(x, mod, gain, w1, w3, w2):
    tokens = x.shape[0]
    d_ff = w1.shape[1]
    tm, tf = TM_FFN, TF_FFN
    per_seq = SEQ // tm
    return pl.pallas_call(
        _ffn_kernel,
        out_shape=jax.ShapeDtypeStruct((tokens, D_MODEL), F32),
        grid=(tokens // tm, d_ff // tf),
        in_specs=[
            pl.BlockSpec((tm, D_MODEL), lambda i, f: (i, 0)),
            pl.BlockSpec((None, 6, D_MODEL), lambda i, f: (i // per_seq, 0, 0)),
            pl.BlockSpec((1, D_MODEL), lambda i, f: (0, 0)),
            pl.BlockSpec((D_MODEL, tf), lambda i, f: (0, f)),
            pl.BlockSpec((D_MODEL, tf), lambda i, f: (0, f)),
            pl.BlockSpec((tf, D_MODEL), lambda i, f: (f, 0)),
        ],
        out_specs=pl.BlockSpec((tm, D_MODEL), lambda i, f: (i, 0)),
        scratch_shapes=[pltpu.VMEM((tm, D_MODEL), BF16), pltpu.VMEM((tm, D_MODEL), F32)],
        compiler_params=_cparams(("arbitrary", "arbitrary")),
        name="dense_swiglu",
    )(x, mod, gain.reshape(1, D_MODEL), w1.astype(BF16), w3.astype(BF16), w2.astype(BF16))


def _router_kernel(x_ref, mod_ref, gain_ref, wr_ref, h_ref, gates_ref, pos_ref, post_ref, cnt_ref,
                   carry_ref, *, tiles_per_seq):
    i = pl.program_id(0)

    @pl.when(i % tiles_per_seq == 0)
    def _():
        carry_ref[...] = jnp.zeros_like(carry_ref)

    mod = mod_ref[...]
    h = _modulated_norm(x_ref[...], gain_ref[...], mod[4:5], mod[3:4]).astype(BF16)
    h_ref[...] = h
    tm = h.shape[0]
    lane = lax.broadcasted_iota(jnp.int32, (tm, LANES), 1).astype(F32)
    logits = jnp.where(lane < N_EXPERTS, _dot(h, wr_ref[...]), -jnp.inf)
    m1 = jnp.max(logits, axis=-1, keepdims=True)
    i1 = jnp.min(jnp.where(logits == m1, lane, float(LANES)), axis=-1, keepdims=True)
    rest = jnp.where(lane == i1, -jnp.inf, logits)
    m2 = jnp.max(rest, axis=-1, keepdims=True)
    i2 = jnp.min(jnp.where(rest == m2, lane, float(LANES)), axis=-1, keepdims=True)
    e2 = jnp.exp(m2 - m1)
    g1 = 1.0 / (1.0 + e2)
    g2 = e2 / (1.0 + e2)
    gates_ref[...] = jnp.where(lane == i1, g1, 0.0) + jnp.where(lane == i2, g2, 0.0)
    chosen = (lane == i1) | (lane == i2)
    onehot = jnp.where(chosen, 1.0, 0.0)
    r = lax.broadcasted_iota(jnp.int32, (tm, tm), 0)
    c = lax.broadcasted_iota(jnp.int32, (tm, tm), 1)
    tril = jnp.where(c <= r, 1.0, 0.0).astype(BF16)
    incl = _dot(tril, onehot.astype(BF16))
    carry = carry_ref[0:1, :]
    pos = jnp.where(chosen, incl - 1.0 + carry, -1.0)
    pos_ref[...] = pos
    post_ref[...] = pos.T[:N_EXPERTS, :]
    total = carry + incl[tm - 1:tm, :]
    carry_ref[...] = jnp.broadcast_to(total, carry_ref.shape)
    cnt_ref[...] = jnp.broadcast_to(total, cnt_ref.shape)


def _router(x, mod, gain, w_router):
    tokens = x.shape[0]
    batch = tokens // SEQ
    tm = TM_PROJ
    per_seq = SEQ // tm
    wr = jnp.zeros((D_MODEL, LANES), BF16).at[:, :N_EXPERTS].set(w_router.astype(BF16))
    row = lambda i: (i, 0)
    return pl.pallas_call(
        functools.partial(_router_kernel, tiles_per_seq=per_seq),
        out_shape=(
            jax.ShapeDtypeStruct((tokens, D_MODEL), BF16),
            jax.ShapeDtypeStruct((tokens, LANES), F32),
            jax.ShapeDtypeStruct((tokens, LANES), F32),
            jax.ShapeDtypeStruct((N_EXPERTS, tokens), F32),
            jax.ShapeDtypeStruct((batch, 8, LANES), F32),
        ),
        grid=(tokens // tm,),
        in_specs=[
            pl.BlockSpec((tm, D_MODEL), row),
            pl.BlockSpec((None, 6, D_MODEL), lambda i: (i // per_seq, 0, 0)),
            pl.BlockSpec((1, D_MODEL), lambda i: (0, 0)),
            pl.BlockSpec((D_MODEL, LANES), lambda i: (0, 0)),
        ],
        out_specs=(
            pl.BlockSpec((tm, D_MODEL), row),
            pl.BlockSpec((tm, LANES), row),
            pl.BlockSpec((tm, LANES), row),
            pl.BlockSpec((N_EXPERTS, tm), lambda i: (0, i)),
            pl.BlockSpec((None, 8, LANES), lambda i: (i // per_seq, 0, 0)),
        ),
        scratch_shapes=[pltpu.VMEM((8, LANES), F32)],
        compiler_params=_cparams(("arbitrary",)),
        name="expert_router",
    )(x, mod, gain.reshape(1, D_MODEL), wr)


def _moe_kernel(nt_ref, h_ref, post_ref, pos_ref, gates_ref, w1_ref, w3_ref, w2_ref, out_ref,
                hs_ref, acc_ref):
    b, e, f = pl.program_id(0), pl.program_id(1), pl.program_id(2)
    n_tiles = nt_ref[b * N_EXPERTS + e]
    tr = TR_MOE

    @pl.when((e == 0) & (f == 0))
    def _():
        out_ref[...] = jnp.zeros_like(out_ref)

    @pl.when(f == 0)
    def _():
        pos_row = post_ref[pl.ds(e, 1), :]

        def gather(r, carry):
            r0 = pl.multiple_of(r * tr, tr)
            slot = (lax.broadcasted_iota(jnp.int32, (tr, SEQ), 0) + r0).astype(F32)
            sel = jnp.where(pos_row == slot, 1.0, 0.0).astype(BF16)
            hs_ref[pl.ds(r0, tr), :] = _dot(sel, h_ref[...]).astype(BF16)
            acc_ref[pl.ds(r0, tr), :] = jnp.zeros((tr, D_MODEL), F32)
            return carry
        lax.fori_loop(0, n_tiles, gather, 0)

    def swiglu(r, carry):
        r0 = pl.multiple_of(r * tr, tr)
        hr = hs_ref[pl.ds(r0, tr), :]
        z = (_silu(_dot(hr, w1_ref[...])) * _dot(hr, w3_ref[...])).astype(BF16)
        acc_ref[pl.ds(r0, tr), :] += _dot(z, w2_ref[...])
        return carry
    lax.fori_loop(0, n_tiles, swiglu, 0)

    @pl.when(f == pl.num_programs(2) - 1)
    def _():
        lane = lax.broadcasted_iota(jnp.int32, (tr, LANES), 1)

        def scatter(r, carry):
            r0 = pl.multiple_of(r * tr, tr)
            rows = acc_ref[pl.ds(r0, tr), :].astype(BF16)
            slot = (lax.broadcasted_iota(jnp.int32, (tr, tr), 1) + r0).astype(F32)
            for tb in range(SEQ // tr):
                ts = slice(tb * tr, (tb + 1) * tr)
                pos_col = jnp.sum(jnp.where(lane == e, pos_ref[ts, :], 0.0), axis=-1, keepdims=True)
                gate_col = jnp.sum(jnp.where(lane == e, gates_ref[ts, :], 0.0), axis=-1, keepdims=True)
                sel = jnp.where(pos_col == slot, 1.0, 0.0).astype(BF16)
                out_ref[ts, :] += gate_col * _dot(sel, rows)
            return carry
        lax.fori_loop(0, n_tiles, scatter, 0)


def _moe(h, post, pos, gates, counts, w1, w3, w2):
    tokens = h.shape[0]
    batch = tokens // SEQ
    d_ff = w1.shape[2]
    tf = TF_MOE
    n_tiles = ((counts + (TR_MOE - 1)) // TR_MOE).astype(jnp.int32).reshape(batch * N_EXPERTS)
    grid_spec = pltpu.PrefetchScalarGridSpec(
        num_scalar_prefetch=1,
        grid=(batch, N_EXPERTS, d_ff // tf),
        in_specs=[
            pl.BlockSpec((SEQ, D_MODEL), lambda b, e, f, nt: (b, 0)),
            pl.BlockSpec((N_EXPERTS, SEQ), lambda b, e, f, nt: (0, b)),
            pl.BlockSpec((SEQ, LANES), lambda b, e, f, nt: (b, 0)),
            pl.BlockSpec((SEQ, LANES), lambda b, e, f, nt: (b, 0)),
            pl.BlockSpec((None, D_MODEL, tf), lambda b, e, f, nt: (e, 0, f)),
            pl.BlockSpec((None, D_MODEL, tf), lambda b, e, f, nt: (e, 0, f)),
            pl.BlockSpec((None, tf, D_MODEL), lambda b, e, f, nt: (e, f, 0)),
        ],
        out_specs=pl.BlockSpec((SEQ, D_MODEL), lambda b, e, f, nt: (b, 0)),
        scratch_shapes=[pltpu.VMEM((SEQ, D_MODEL), BF16), pltpu.VMEM((SEQ, D_MODEL), F32)],
    )
    return pl.pallas_call(
        _moe_kernel,
        out_shape=jax.ShapeDtypeStruct((tokens, D_MODEL), F32),
        grid_spec=grid_spec,
        compiler_params=_cparams(("arbitrary", "arbitrary", "arbitrary")),
        name="expert_swiglu",
    )(n_tiles, h, post, pos, gates, w1.astype(BF16), w3.astype(BF16), w2.astype(BF16))


def _residual_kernel(x_ref, y_ref, mod_ref, out_ref):
    out_ref[...] = x_ref[...] + mod_ref[5:6, :] * y_ref[...]


def _gated_residual(x, y, mod):
    tokens = x.shape[0]
    tm = 1024
    per_seq = SEQ // tm
    row = lambda i: (i, 0)
    return pl.pallas_call(
        _residual_kernel,
        out_shape=jax.ShapeDtypeStruct((tokens, D_MODEL), F32),
        grid=(tokens // tm,),
        in_specs=[pl.BlockSpec((tm, D_MODEL), row), pl.BlockSpec((tm, D_MODEL), row),
                  pl.BlockSpec((None, 6, D_MODEL), lambda i: (i // per_seq, 0, 0))],
        out_specs=pl.BlockSpec((tm, D_MODEL), row),
        compiler_params=_cparams(("arbitrary",)),
        name="gated_residual",
    )(x, y, mod)


def _token_mixer(x, mod, gain, w_in, q_gain, k_gain, ret_gain, w_out, bias_masks):
    batch = x.shape[0] // SEQ
    q, k, v, ret_in = _in_proj(x, mod, gain, w_in, q_gain, k_gain)
    shape3 = (batch, SEQ, ATTN_WIDTH)
    q, k, v = q.reshape(shape3), k.reshape(shape3), v.reshape(shape3)
    outs, lses = [], []
    for p, (_, dilation) in enumerate(DILATED_PATTERNS):
        n_res = RES16 if dilation == 16 else 1
        o, l = _dilated_branch(q, k, v, bias_masks[p], dilation, n_res)
        outs.append(o)
        lses.append(l)
    ret = _retention(ret_in, ret_gain)
    return _out_proj(outs, lses, ret, x, mod, w_out)


def _moe_ffn(x, mod, gain, w_router, w1, w3, w2):
    h, gates, pos, post, cnt = _router(x, mod, gain, w_router)
    counts = cnt[:, 0, :N_EXPERTS].astype(jnp.int32)
    y = _moe(h, post, pos, gates, counts, w1, w3, w2)
    return _gated_residual(x, y, mod)


def kernel(x, c, rel_bias_table, norm_mix, norm_ffn, w_mod, b_mod, w_in, q_gain, k_gain, ret_gain, w_out,
           ffn_w1, ffn_w3, ffn_w2, moe_router, moe_w1, moe_w3, moe_w2):
    batch, seq, d_model = x.shape
    assert (seq, d_model) == (SEQ, D_MODEL)
    depth = w_mod.shape[0]
    mods = _modulation(c, w_mod, b_mod).reshape(depth, batch, 6, D_MODEL)
    bias_masks = _bias_masks(rel_bias_table)
    xt = x.reshape(batch * seq, d_model)
    for layer in range(depth):
        mod = mods[layer]
        xt = _token_mixer(xt, mod, norm_mix[layer], w_in[layer], q_gain[layer], k_gain[layer],
                          ret_gain[layer], w_out[layer], bias_masks)
        i = layer // 2
        if layer % 2 == 0:
            xt = _dense_ffn(xt, mod, norm_ffn[layer], ffn_w1[i], ffn_w3[i], ffn_w2[i])
        else:
            xt = _moe_ffn(xt, mod, norm_ffn[layer], moe_router[i], moe_w1[i], moe_w3[i], moe_w2[i])
    return xt.reshape(batch, seq, d_model)
```

```python
import functools
import math

import jax
import jax.numpy as jnp
import numpy as np
from jax import lax
from jax.experimental import pallas as pl
from jax.experimental.pallas import tpu as pltpu

D_MODEL = 1024
SEQ = 2048
ATTN_HEADS = 8
ATTN_HEAD_DIM = 64
ATTN_WIDTH = ATTN_HEADS * ATTN_HEAD_DIM
DILATED_PATTERNS = ((128, 1), (512, 4), (2048, 16))
BLOCK = 128
NUM_BUCKETS = 32
MAX_DISTANCE = 2048
RET_HEADS = 4
RET_KEY_DIM = 64
RET_VALUE_DIM = 128
RET_WIDTH = RET_HEADS * RET_VALUE_DIM
RET_QK_WIDTH = RET_HEADS * RET_KEY_DIM
RET_CHUNK = 128
ROPE_BASE = 10000.0
IN_WIDTH = 3 * ATTN_WIDTH + 2 * RET_QK_WIDTH + 2 * RET_WIDTH
RET_IN_WIDTH = IN_WIDTH - 3 * ATTN_WIDTH
N_EXPERTS = 8
EPS = 1e-6
NEG_INF = -1e30

LANES = 128
VMEM_LIMIT = 56 * 1024 * 1024

BF16 = jnp.bfloat16
F32 = jnp.float32

TM_PROJ = 256
TM_FFN = 512
TF_FFN = 1408
TR_MOE = 256
TF_MOE = 512
RET_ROWS = 512


def _cparams(sem):
    return pltpu.CompilerParams(dimension_semantics=sem, vmem_limit_bytes=VMEM_LIMIT)


def _dot(a, b):
    return jnp.dot(a, b, preferred_element_type=F32)


def _dot_nt(a, b):
    return lax.dot_general(a, b, (((1,), (1,)), ((), ())), preferred_element_type=F32)


def _dot_tn(a, b):
    return lax.dot_general(a, b, (((0,), (0,)), ((), ())), preferred_element_type=F32)


def _split_bf16(v):
    hi = v.astype(BF16)
    lo = (v - hi.astype(F32)).astype(BF16)
    return hi, lo


def _silu(v):
    return v * (1.0 / (1.0 + jnp.exp(-v)))


def _modulated_norm(x, gain, scale, shift):
    ms = jnp.mean(x * x, axis=-1, keepdims=True)
    y = x * lax.rsqrt(ms + EPS) * gain
    return y * (1.0 + scale) + shift


def _mod_kernel(c_ref, w_ref, b_ref, o_ref):
    ca = _silu(c_ref[...]).astype(BF16)
    o_ref[...] = _dot(ca, w_ref[...].astype(BF16)) + b_ref[...]


def _modulation(c, w_mod, b_mod):
    depth, _, width = w_mod.shape
    batch = c.shape[0]
    tn = 1536
    return pl.pallas_call(
        _mod_kernel,
        out_shape=jax.ShapeDtypeStruct((depth, batch, width), F32),
        grid=(depth, width // tn),
        in_specs=[
            pl.BlockSpec((batch, D_MODEL), lambda l, n: (0, 0)),
            pl.BlockSpec((None, D_MODEL, tn), lambda l, n: (l, 0, n)),
            pl.BlockSpec((None, 1, tn), lambda l, n: (l, 0, n)),
        ],
        out_specs=pl.BlockSpec((None, batch, tn), lambda l, n: (l, 0, n)),
        compiler_params=_cparams(("arbitrary", "arbitrary")),
        name="adaln_modulation",
    )(c, w_mod, b_mod.reshape(depth, 1, width))


def _bias_kernel(table_ref, bucket_ref, o_ref):
    h = pl.program_id(1)
    bucket = bucket_ref[...]
    acc = jnp.full(bucket.shape, NEG_INF, F32)
    for b in range(NUM_BUCKETS):
        acc = jnp.where(bucket == b, table_ref[b, h], acc)
    o_ref[...] = acc


BIAS_FULL = {1: 0, 4: 2}
BIAS_FIRST = {1: 1, 4: 3, 16: 4}
N_BIAS_SETS = 5


def _bias_masks(rel_bias_table):
    i = jnp.arange(BLOCK)[:, None]
    j = jnp.arange(2 * BLOCK)[None, :]
    max_exact = NUM_BUCKETS // 2

    def bucket_of(rel, dilation, w_sub, exists):
        n = jnp.maximum(rel * dilation, 0)
        nf = jnp.maximum(n.astype(F32), float(max_exact))
        large = max_exact + (jnp.log(nf / max_exact) / math.log(MAX_DISTANCE / max_exact)
                             * (NUM_BUCKETS - max_exact)).astype(jnp.int32)
        large = jnp.minimum(large, NUM_BUCKETS - 1)
        bucket = jnp.where(n < max_exact, n, large)
        allowed = (rel >= 0) & (rel <= w_sub) & exists
        return jnp.where(allowed, bucket, -1)

    sets = [None] * N_BIAS_SETS
    for window, dilation in DILATED_PATTERNS:
        w_sub = window // dilation
        if dilation in BIAS_FULL:
            sets[BIAS_FULL[dilation]] = bucket_of(i - j + BLOCK, dilation, w_sub, j >= 0)
        sets[BIAS_FIRST[dilation]] = bucket_of(i - j, dilation, w_sub, j < BLOCK)
    buckets = jnp.stack(sets).astype(jnp.int32)
    return pl.pallas_call(
        _bias_kernel,
        out_shape=jax.ShapeDtypeStruct((N_BIAS_SETS, ATTN_HEADS // 2, 2 * BLOCK, 2 * BLOCK), F32),
        grid=(N_BIAS_SETS, ATTN_HEADS),
        in_specs=[
            pl.BlockSpec(memory_space=pltpu.SMEM),
            pl.BlockSpec((None, BLOCK, 2 * BLOCK), lambda p, h: (p, 0, 0)),
        ],
        out_specs=pl.BlockSpec((None, None, BLOCK, 2 * BLOCK), lambda p, h: (p, h // 2, h % 2, 0)),
        compiler_params=_cparams(("arbitrary", "arbitrary")),
        name="relative_bias_masks",
    )(rel_bias_table, buckets)


HALF = ATTN_WIDTH // 2


def _in_proj_kernel(x_ref, mod_ref, gain_ref, w_ref, qg_ref, kg_ref, grp_ref,
                    q1_ref, k1_ref, v1_ref, q4_ref, k4_ref, v4_ref, q16_ref, k16_ref, v16_ref, r_ref,
                    perm_ref):
    mod = mod_ref[...]
    h = _modulated_norm(x_ref[...], gain_ref[...], mod[1:2], mod[0:1]).astype(BF16)
    proj = _dot(h, w_ref[...])
    grp = grp_ref[...]
    tm = proj.shape[0]

    def head_norm(t, gain):
        hi, lo = _split_bf16(t * t)
        ss = _dot(hi, grp) + _dot(lo, grp)
        return t * lax.rsqrt(ss * (1.0 / ATTN_HEAD_DIM) + EPS) * gain

    def emit(t, o1_ref, o4_ref, o16_ref):
        o1_ref[...] = t.astype(BF16)
        for j in range(ATTN_WIDTH // LANES):
            perm_ref[j] = t[:, j * LANES:(j + 1) * LANES]
        for dil, o_ref in ((4, o4_ref), (16, o16_ref)):
            for hf in range(2):
                for r in range(dil):
                    for jj in range(HALF // LANES):
                        c0 = (hf * dil + r) * HALF + jj * LANES
                        o_ref[:, c0:c0 + LANES] = perm_ref[hf * (HALF // LANES) + jj,
                                                           pl.ds(r, tm // dil, stride=dil), :].astype(BF16)

    emit(head_norm(proj[:, :ATTN_WIDTH], qg_ref[...]) * (ATTN_HEAD_DIM ** -0.5), q1_ref, q4_ref, q16_ref)
    emit(head_norm(proj[:, ATTN_WIDTH:2 * ATTN_WIDTH], kg_ref[...]), k1_ref, k4_ref, k16_ref)
    emit(proj[:, 2 * ATTN_WIDTH:3 * ATTN_WIDTH], v1_ref, v4_ref, v16_ref)
    r_ref[...] = proj[:, 3 * ATTN_WIDTH:]


def _in_proj(x, mod, gain, w_in, q_gain, k_gain):
    tokens = x.shape[0]
    tm = TM_PROJ
    per_seq = SEQ // tm
    grp = np.kron(np.eye(ATTN_HEADS), np.ones((ATTN_HEAD_DIM, ATTN_HEAD_DIM))).astype(np.float32)
    row = lambda i: (i, 0)
    const = lambda i: (0, 0)
    layouts = []
    for dil in (1, 4, 16):
        shape = jax.ShapeDtypeStruct((tokens // dil, dil * ATTN_WIDTH), BF16)
        spec = pl.BlockSpec((tm // dil, dil * ATTN_WIDTH), row)
        layouts.append(((shape,) * 3, (spec,) * 3))
    out_shape = sum((s for s, _ in layouts), ()) + (jax.ShapeDtypeStruct((tokens, RET_IN_WIDTH), F32),)
    out_specs = sum((s for _, s in layouts), ()) + (pl.BlockSpec((tm, RET_IN_WIDTH), row),)
    return pl.pallas_call(
        _in_proj_kernel,
        out_shape=out_shape,
        grid=(tokens // tm,),
        in_specs=[
            pl.BlockSpec((tm, D_MODEL), row),
            pl.BlockSpec((None, 6, D_MODEL), lambda i: (i // per_seq, 0, 0)),
            pl.BlockSpec((1, D_MODEL), const),
            pl.BlockSpec((D_MODEL, IN_WIDTH), const),
            pl.BlockSpec((1, ATTN_WIDTH), const),
            pl.BlockSpec((1, ATTN_WIDTH), const),
            pl.BlockSpec((ATTN_WIDTH, ATTN_WIDTH), const),
        ],
        out_specs=out_specs,
        scratch_shapes=[pltpu.VMEM((ATTN_WIDTH // LANES, tm, LANES), F32)],
        compiler_params=_cparams(("arbitrary",)),
        name="in_projection",
    )(x, mod, gain.reshape(1, D_MODEL), w_in.astype(BF16),
      jnp.tile(q_gain, ATTN_HEADS).reshape(1, ATTN_WIDTH),
      jnp.tile(k_gain, ATTN_HEADS).reshape(1, ATTN_WIDTH),
      jnp.asarray(grp, BF16))


PAIRS_PER_HALF = ATTN_HEADS // 4
GROUP = 4


def _pair_scores(qp, kp, vp, bias2, masks, low):
    q2 = jnp.concatenate([qp * masks[0], qp * masks[1]], axis=0)
    s = _dot_nt(q2, kp) + bias2
    m = jnp.max(s, axis=-1, keepdims=True)
    p = jnp.exp(s - m)
    den = jnp.sum(p, axis=-1, keepdims=True)
    pv = _dot(p.astype(BF16), vp)
    pick = lambda t: jnp.where(low, t[:BLOCK], t[BLOCK:])
    return pick(pv), pick(m), pick(den)


def _attn_kernel(q1_ref, k1_ref, v1_ref, q4_ref, k4_ref, v4_ref, q16_ref, k16_ref, v16_ref, bm_ref,
                 o_ref, acc_ref, max_ref, den_ref):
    lane = lax.broadcasted_iota(jnp.int32, (BLOCK, LANES), 1)
    low = lane < ATTN_HEAD_DIM
    masks = (jnp.where(low, 1.0, 0.0).astype(BF16), jnp.where(low, 0.0, 1.0).astype(BF16))

    def block(q_ref, k_ref, v_ref, c0, q0, w0, width, bias_set, rows, first):
        for p in range(PAIRS_PER_HALF):
            cs = slice(c0 + p * LANES, c0 + (p + 1) * LANES)
            acc, m, den = _pair_scores(q_ref[pl.ds(q0, BLOCK), cs], k_ref[pl.ds(w0, width), cs],
                                       v_ref[pl.ds(w0, width), cs], bm_ref[bias_set, p, :, 0:width],
                                       masks, low)
            if not first:
                m_old = max_ref[p, rows, :]
                m_new = jnp.maximum(m_old, m)
                a, b = jnp.exp(m_old - m_new), jnp.exp(m - m_new)
                den = den_ref[p, rows, :] * a + den * b
                acc = acc_ref[p, rows, :] * a + acc * b
                m = m_new
            max_ref[p, rows, :] = m
            den_ref[p, rows, :] = den
            acc_ref[p, rows, :] = acc

    def d1_group(g, carry):
        for u in range(GROUP):
            n = g * GROUP + u
            q0 = pl.multiple_of(n * BLOCK, BLOCK)
            w0 = pl.multiple_of(jnp.maximum(n - 1, 0) * BLOCK, BLOCK)
            bias_set = jnp.where(n == 0, BIAS_FIRST[1], BIAS_FULL[1])
            block(q1_ref, k1_ref, v1_ref, 0, q0, w0, 2 * BLOCK, bias_set, pl.ds(q0, BLOCK), True)
        return carry
    lax.fori_loop(0, SEQ // BLOCK // GROUP, d1_group, 0)

    for r in range(4):
        for n in range(SEQ // 4 // BLOCK):
            block(q4_ref, k4_ref, v4_ref, r * HALF, n * BLOCK, max(n - 1, 0) * BLOCK, 2 * BLOCK,
                  BIAS_FIRST[4] if n == 0 else BIAS_FULL[4],
                  pl.ds(r + 4 * BLOCK * n, BLOCK, stride=4), False)

    for r in range(16):
        block(q16_ref, k16_ref, v16_ref, r * HALF, 0, 0, BLOCK, BIAS_FIRST[16],
              pl.ds(r, BLOCK, stride=16), False)

    for n in range(SEQ // BLOCK):
        rows = slice(n * BLOCK, (n + 1) * BLOCK)
        for p in range(PAIRS_PER_HALF):
            o_ref[rows, p * LANES:(p + 1) * LANES] = (acc_ref[p, rows, :] / den_ref[p, rows, :]).astype(BF16)


def _dilated_attention(qkv, bias_masks):
    tokens = qkv[0].shape[0]
    batch = tokens // SEQ
    specs = []
    for dil in (1, 4, 16):
        specs += [pl.BlockSpec((SEQ // dil, dil * HALF), lambda b, hf: (b, hf))] * 3
    state = pltpu.VMEM((PAIRS_PER_HALF, SEQ, LANES), F32)
    return pl.pallas_call(
        _attn_kernel,
        out_shape=jax.ShapeDtypeStruct((tokens, ATTN_WIDTH), BF16),
        grid=(batch, 2),
        in_specs=specs + [pl.BlockSpec((N_BIAS_SETS, PAIRS_PER_HALF, 2 * BLOCK, 2 * BLOCK),
                                       lambda b, hf: (0, hf, 0, 0))],
        out_specs=pl.BlockSpec((SEQ, HALF), lambda b, hf: (b, hf)),
        scratch_shapes=[state, state, state],
        compiler_params=_cparams(("arbitrary", "arbitrary")),
        name="dilated_attention",
    )(*qkv, bias_masks)


def _retention_kernel(r_ref, cos_ref, sin_ref, dmask_ref, qdec_ref, kdec_ref, cdec_ref, gain_ref,
                      o_ref, state_ref):
    @pl.when(pl.program_id(1) == 0)
    def _():
        state_ref[...] = jnp.zeros_like(state_ref)

    lane = lax.broadcasted_iota(jnp.int32, (RET_CHUNK, LANES), 1)
    low = lane < RET_KEY_DIM
    first_half = (lane % RET_KEY_DIM) < (RET_KEY_DIM // 2)

    def rotate(t, cos, sin):
        partner = jnp.where(first_half, pltpu.roll(t, LANES - RET_KEY_DIM // 2, 1),
                            pltpu.roll(t, RET_KEY_DIM // 2, 1))
        return t * cos + partner * sin

    for c in range(RET_ROWS // RET_CHUNK):
        rows = slice(c * RET_CHUNK, (c + 1) * RET_CHUNK)
        for hp in range(RET_HEADS // 2):
            qs = slice(hp * LANES, (hp + 1) * LANES)
            ks = slice(RET_QK_WIDTH + hp * LANES, RET_QK_WIDTH + (hp + 1) * LANES)
            cos, sin = cos_ref[rows, qs], sin_ref[rows, qs]
            q_pair = rotate(r_ref[rows, qs], cos, sin) * (RET_KEY_DIM ** -0.5)
            k_pair = rotate(r_ref[rows, ks], cos, sin)
            for hh in range(2):
                head = 2 * hp + hh
                vs = slice(2 * RET_QK_WIDTH + head * LANES, 2 * RET_QK_WIDTH + (head + 1) * LANES)
                gs = slice(2 * RET_QK_WIDTH + RET_WIDTH + head * LANES,
                           2 * RET_QK_WIDTH + RET_WIDTH + (head + 1) * LANES)
                keep = low if hh == 0 else jnp.logical_not(low)
                qm = jnp.where(keep, q_pair, 0.0)
                vb = r_ref[rows, vs].astype(BF16)
                state = state_ref[head]
                inner = _dot_nt(qm.astype(BF16), k_pair.astype(BF16)) * dmask_ref[head]
                y = _dot(inner.astype(BF16), vb)
                y = y + _dot((qm * qdec_ref[head]).astype(BF16), state.astype(BF16))
                state_ref[head] = state * cdec_ref[head] + _dot_tn((k_pair * kdec_ref[head]).astype(BF16), vb)
                mu = jnp.mean(y, axis=-1, keepdims=True)
                yc = y - mu
                var = jnp.mean(yc * yc, axis=-1, keepdims=True)
                yn = yc * lax.rsqrt(var + EPS) * gain_ref[:, head * LANES:(head + 1) * LANES]
                o_ref[rows, head * LANES:(head + 1) * LANES] = (_silu(r_ref[rows, gs]) * yn).astype(BF16)


def _retention_tables():
    half = RET_KEY_DIM // 2
    pos = jnp.arange(SEQ, dtype=F32)
    inv = ROPE_BASE ** (-jnp.arange(half, dtype=F32) / half)
    ang = pos[:, None] * inv[None, :]
    cos, sin = jnp.cos(ang), jnp.sin(ang)
    cos_full = jnp.tile(jnp.concatenate([cos, cos], axis=-1), (1, RET_HEADS))
    sin_signed = jnp.tile(jnp.concatenate([-sin, sin], axis=-1), (1, RET_HEADS))
    log_g = jnp.log(1.0 - 2.0 ** (-5.0 - jnp.arange(RET_HEADS, dtype=F32)))
    idx = jnp.arange(RET_CHUNK, dtype=F32)
    diff = idx[:, None] - idx[None, :]
    dmask = jnp.where(diff >= 0, jnp.exp(jnp.maximum(diff, 0.0)[None] * log_g[:, None, None]), 0.0)
    q_decay = jnp.exp((idx + 1.0)[None, :] * log_g[:, None])[..., None]
    k_decay = jnp.exp((RET_CHUNK - 1.0 - idx)[None, :] * log_g[:, None])[..., None]
    chunk_decay = jnp.exp(RET_CHUNK * log_g)[:, None, None]
    full = (RET_HEADS, RET_CHUNK, LANES)
    return (cos_full, sin_signed, dmask, jnp.broadcast_to(q_decay, full),
            jnp.broadcast_to(k_decay, full), jnp.broadcast_to(chunk_decay, full))


def _retention(ret_in, ret_gain):
    tokens = ret_in.shape[0]
    batch = tokens // SEQ
    per_seq = SEQ // RET_ROWS
    cos, sin, dmask, qdec, kdec, cdec = _retention_tables()
    tab = pl.BlockSpec((RET_ROWS, RET_QK_WIDTH), lambda b, j: (j, 0))
    const3 = pl.BlockSpec((RET_HEADS, RET_CHUNK, LANES), lambda b, j: (0, 0, 0))
    return pl.pallas_call(
        _retention_kernel,
        out_shape=jax.ShapeDtypeStruct((tokens, RET_WIDTH), BF16),
        grid=(batch, per_seq),
        in_specs=[
            pl.BlockSpec((RET_ROWS, RET_IN_WIDTH), lambda b, j: (b * per_seq + j, 0)),
            tab, tab, const3, const3, const3, const3,
            pl.BlockSpec((1, RET_WIDTH), lambda b, j: (0, 0)),
        ],
        out_specs=pl.BlockSpec((RET_ROWS, RET_WIDTH), lambda b, j: (b * per_seq + j, 0)),
        scratch_shapes=[pltpu.VMEM((RET_HEADS, LANES, RET_VALUE_DIM), F32)],
        compiler_params=_cparams(("arbitrary", "arbitrary")),
        name="retention",
    )(ret_in, cos, sin, dmask, qdec, kdec, cdec, ret_gain.reshape(1, RET_WIDTH))


def _out_proj_kernel(attn_ref, ret_ref, x_ref, mod_ref, w_ref, out_ref):
    mix = _dot(attn_ref[...], w_ref[:ATTN_WIDTH, :]) + _dot(ret_ref[...], w_ref[ATTN_WIDTH:, :])
    out_ref[...] = x_ref[...] + mod_ref[2:3, :] * mix


def _out_proj(attn, ret, x, mod, w_out):
    tokens = x.shape[0]
    tm = TM_FFN
    per_seq = SEQ // tm
    row = lambda i: (i, 0)
    return pl.pallas_call(
        _out_proj_kernel,
        out_shape=jax.ShapeDtypeStruct((tokens, D_MODEL), F32),
        grid=(tokens // tm,),
        in_specs=[pl.BlockSpec((tm, ATTN_WIDTH), row),
                  pl.BlockSpec((tm, RET_WIDTH), row),
                  pl.BlockSpec((tm, D_MODEL), row),
                  pl.BlockSpec((None, 6, D_MODEL), lambda i: (i // per_seq, 0, 0)),
                  pl.BlockSpec((D_MODEL, D_MODEL), lambda i: (0, 0))],
        out_specs=pl.BlockSpec((tm, D_MODEL), row),
        compiler_params=_cparams(("arbitrary",)),
        name="out_projection",
    )(attn, ret, x, mod, w_out.astype(BF16))


def _ffn_kernel(x_ref, mod_ref, gain_ref, w1_ref, w3_ref, w2_ref, out_ref, h_ref, acc_ref):
    f = pl.program_id(1)

    @pl.when(f == 0)
    def _():
        mod = mod_ref[...]
        h_ref[...] = _modulated_norm(x_ref[...], gain_ref[...], mod[4:5], mod[3:4]).astype(BF16)
        acc_ref[...] = jnp.zeros_like(acc_ref)

    h = h_ref[...]
    z = (_silu(_dot(h, w1_ref[...])) * _dot(h, w3_ref[...])).astype(BF16)
    acc_ref[...] += _dot(z, w2_ref[...])

    @pl.when(f == pl.num_programs(1) - 1)
    def _():
        out_ref[...] = x_ref[...] + mod_ref[5:6, :] * acc_ref[...]


def _dense_ffn(x, mod, gain, w1, w3, w2):
    tokens = x.shape[0]
    d_ff = w1.shape[1]
    tm, tf = TM_FFN, TF_FFN
    per_seq = SEQ // tm
    return pl.pallas_call(
        _ffn_kernel,
        out_shape=jax.ShapeDtypeStruct((tokens, D_MODEL), F32),
        grid=(tokens // tm, d_ff // tf),
        in_specs=[
            pl.BlockSpec((tm, D_MODEL), lambda i, f: (i, 0)),
            pl.BlockSpec((None, 6, D_MODEL), lambda i, f: (i // per_seq, 0, 0)),
            pl.BlockSpec((1, D_MODEL), lambda i, f: (0, 0)),
            pl.BlockSpec((D_MODEL, tf), lambda i, f: (0, f)),
            pl.BlockSpec((D_MODEL, tf), lambda i, f: (0, f)),
            pl.BlockSpec((tf, D_MODEL), lambda i, f: (f, 0)),
        ],
        out_specs=pl.BlockSpec((tm, D_MODEL), lambda i, f: (i, 0)),
        scratch_shapes=[pltpu.VMEM((tm, D_MODEL), BF16), pltpu.VMEM((tm, D_MODEL), F32)],
        compiler_params=_cparams(("arbitrary", "arbitrary")),
        name="dense_swiglu",
    )(x, mod, gain.reshape(1, D_MODEL), w1.astype(BF16), w3.astype(BF16), w2.astype(BF16))


def _router_kernel(x_ref, mod_ref, gain_ref, wr_ref, h_ref, gates_ref, pos_ref, post_ref, cnt_ref,
                   carry_ref, *, tiles_per_seq):
    i = pl.program_id(0)

    @pl.when(i % tiles_per_seq == 0)
    def _():
        carry_ref[...] = jnp.zeros_like(carry_ref)

    mod = mod_ref[...]
    h = _modulated_norm(x_ref[...], gain_ref[...], mod[4:5], mod[3:4]).astype(BF16)
    h_ref[...] = h
    tm = h.shape[0]
    lane = lax.broadcasted_iota(jnp.int32, (tm, LANES), 1).astype(F32)
    logits = jnp.where(lane < N_EXPERTS, _dot(h, wr_ref[...]), -jnp.inf)
    m1 = jnp.max(logits, axis=-1, keepdims=True)
    i1 = jnp.min(jnp.where(logits == m1, lane, float(LANES)), axis=-1, keepdims=True)
    rest = jnp.where(lane == i1, -jnp.inf, logits)
    m2 = jnp.max(rest, axis=-1, keepdims=True)
    i2 = jnp.min(jnp.where(rest == m2, lane, float(LANES)), axis=-1, keepdims=True)
    e2 = jnp.exp(m2 - m1)
    g1 = 1.0 / (1.0 + e2)
    g2 = e2 / (1.0 + e2)
    gates_ref[...] = jnp.where(lane == i1, g1, 0.0) + jnp.where(lane == i2, g2, 0.0)
    chosen = (lane == i1) | (lane == i2)
    onehot = jnp.where(chosen, 1.0, 0.0)
    r = lax.broadcasted_iota(jnp.int32, (tm, tm), 0)
    c = lax.broadcasted_iota(jnp.int32, (tm, tm), 1)
    tril = jnp.where(c <= r, 1.0, 0.0).astype(BF16)
    incl = _dot(tril, onehot.astype(BF16))
    carry = carry_ref[0:1, :]
    pos = jnp.where(chosen, incl - 1.0 + carry, -1.0)
    pos_ref[...] = pos
    post_ref[...] = pos.T[:N_EXPERTS, :]
    total = carry + incl[tm - 1:tm, :]
    carry_ref[...] = jnp.broadcast_to(total, carry_ref.shape)
    cnt_ref[...] = jnp.broadcast_to(total, cnt_ref.shape)


def _router(x, mod, gain, w_router):
    tokens = x.shape[0]
    batch = tokens // SEQ
    tm = TM_PROJ
    per_seq = SEQ // tm
    wr = jnp.zeros((D_MODEL, LANES), BF16).at[:, :N_EXPERTS].set(w_router.astype(BF16))
    row = lambda i: (i, 0)
    return pl.pallas_call(
        functools.partial(_router_kernel, tiles_per_seq=per_seq),
        out_shape=(
            jax.ShapeDtypeStruct((tokens, D_MODEL), BF16),
            jax.ShapeDtypeStruct((tokens, LANES), F32),
            jax.ShapeDtypeStruct((tokens, LANES), F32),
            jax.ShapeDtypeStruct((N_EXPERTS, tokens), F32),
            jax.ShapeDtypeStruct((batch, 8, LANES), F32),
        ),
        grid=(tokens // tm,),
        in_specs=[
            pl.BlockSpec((tm, D_MODEL), row),
            pl.BlockSpec((None, 6, D_MODEL), lambda i: (i // per_seq, 0, 0)),
            pl.BlockSpec((1, D_MODEL), lambda i: (0, 0)),
            pl.BlockSpec((D_MODEL, LANES), lambda i: (0, 0)),
        ],
        out_specs=(
            pl.BlockSpec((tm, D_MODEL), row),
            pl.BlockSpec((tm, LANES), row),
            pl.BlockSpec((tm, LANES), row),
            pl.BlockSpec((N_EXPERTS, tm), lambda i: (0, i)),
            pl.BlockSpec((None, 8, LANES), lambda i: (i // per_seq, 0, 0)),
        ),
        scratch_shapes=[pltpu.VMEM((8, LANES), F32)],
        compiler_params=_cparams(("arbitrary",)),
        name="expert_router",
    )(x, mod, gain.reshape(1, D_MODEL), wr)


def _moe_kernel(nt_ref, h_ref, post_ref, pos_ref, gates_ref, w1_ref, w3_ref, w2_ref, out_ref,
                hs_ref, acc_ref):
    b, e, f = pl.program_id(0), pl.program_id(1), pl.program_id(2)
    n_tiles = nt_ref[b * N_EXPERTS + e]
    tr = TR_MOE

    @pl.when((e == 0) & (f == 0))
    def _():
        out_ref[...] = jnp.zeros_like(out_ref)

    @pl.when(f == 0)
    def _():
        pos_row = post_ref[pl.ds(e, 1), :]

        def gather(r, carry):
            r0 = pl.multiple_of(r * tr, tr)
            slot = (lax.broadcasted_iota(jnp.int32, (tr, SEQ), 0) + r0).astype(F32)
            sel = jnp.where(pos_row == slot, 1.0, 0.0).astype(BF16)
            hs_ref[pl.ds(r0, tr), :] = _dot(sel, h_ref[...]).astype(BF16)
            acc_ref[pl.ds(r0, tr), :] = jnp.zeros((tr, D_MODEL), F32)
            return carry
        lax.fori_loop(0, n_tiles, gather, 0)

    def swiglu(r, carry):
        r0 = pl.multiple_of(r * tr, tr)
        hr = hs_ref[pl.ds(r0, tr), :]
        z = (_silu(_dot(hr, w1_ref[...])) * _dot(hr, w3_ref[...])).astype(BF16)
        acc_ref[pl.ds(r0, tr), :] += _dot(z, w2_ref[...])
        return carry
    lax.fori_loop(0, n_tiles, swiglu, 0)

    @pl.when(f == pl.num_programs(2) - 1)
    def _():
        lane = lax.broadcasted_iota(jnp.int32, (tr, LANES), 1)

        def scatter(r, carry):
            r0 = pl.multiple_of(r * tr, tr)
            rows = acc_ref[pl.ds(r0, tr), :].astype(BF16)
            slot = (lax.broadcasted_iota(jnp.int32, (tr, tr), 1) + r0).astype(F32)
            for tb in range(SEQ // tr):
                ts = slice(tb * tr, (tb + 1) * tr)
                pos_col = jnp.sum(jnp.where(lane == e, pos_ref[ts, :], 0.0), axis=-1, keepdims=True)
                gate_col = jnp.sum(jnp.where(lane == e, gates_ref[ts, :], 0.0), axis=-1, keepdims=True)
                sel = jnp.where(pos_col == slot, 1.0, 0.0).astype(BF16)
                out_ref[ts, :] += gate_col * _dot(sel, rows)
            return carry
        lax.fori_loop(0, n_tiles, scatter, 0)


def _moe(h, post, pos, gates, counts, w1, w3, w2):
    tokens = h.shape[0]
    batch = tokens // SEQ
    d_ff = w1.shape[2]
    tf = TF_MOE
    n_tiles = ((counts + (TR_MOE - 1)) // TR_MOE).astype(jnp.int32).reshape(batch * N_EXPERTS)
    grid_spec = pltpu.PrefetchScalarGridSpec(
        num_scalar_prefetch=1,
        grid=(batch, N_EXPERTS, d_ff // tf),
        in_specs=[
            pl.BlockSpec((SEQ, D_MODEL), lambda b, e, f, nt: (b, 0)),
            pl.BlockSpec((N_EXPERTS, SEQ), lambda b, e, f, nt: (0, b)),
            pl.BlockSpec((SEQ, LANES), lambda b, e, f, nt: (b, 0)),
            pl.BlockSpec((SEQ, LANES), lambda b, e, f, nt: (b, 0)),
            pl.BlockSpec((None, D_MODEL, tf), lambda b, e, f, nt: (e, 0, f)),
            pl.BlockSpec((None, D_MODEL, tf), lambda b, e, f, nt: (e, 0, f)),
            pl.BlockSpec((None, tf, D_MODEL), lambda b, e, f, nt: (e, f, 0)),
        ],
        out_specs=pl.BlockSpec((SEQ, D_MODEL), lambda b, e, f, nt: (b, 0)),
        scratch_shapes=[pltpu.VMEM((SEQ, D_MODEL), BF16), pltpu.VMEM((SEQ, D_MODEL), F32)],
    )
    return pl.pallas_call(
        _moe_kernel,
        out_shape=jax.ShapeDtypeStruct((tokens, D_MODEL), F32),
        grid_spec=grid_spec,
        compiler_params=_cparams(("arbitrary", "arbitrary", "arbitrary")),
        name="expert_swiglu",
    )(n_tiles, h, post, pos, gates, w1.astype(BF16), w3.astype(BF16), w2.astype(BF16))


def _residual_kernel(x_ref, y_ref, mod_ref, out_ref):
    out_ref[...] = x_ref[...] + mod_ref[5:6, :] * y_ref[...]


def _gated_residual(x, y, mod):
    tokens = x.shape[0]
    tm = 1024
    per_seq = SEQ // tm
    row = lambda i: (i, 0)
    return pl.pallas_call(
        _residual_kernel,
        out_shape=jax.ShapeDtypeStruct((tokens, D_MODEL), F32),
        grid=(tokens // tm,),
        in_specs=[pl.BlockSpec((tm, D_MODEL), row), pl.BlockSpec((tm, D_MODEL), row),
                  pl.BlockSpec((None, 6, D_MODEL), lambda i: (i // per_seq, 0, 0))],
        out_specs=pl.BlockSpec((tm, D_MODEL), row),
        compiler_params=_cparams(("arbitrary",)),
        name="gated_residual",
    )(x, y, mod)


def _token_mixer(x, mod, gain, w_in, q_gain, k_gain, ret_gain, w_out, bias_masks):
    *qkv, ret_in = _in_proj(x, mod, gain, w_in, q_gain, k_gain)
    attn = _dilated_attention(qkv, bias_masks)
    ret = _retention(ret_in, ret_gain)
    return _out_proj(attn, ret, x, mod, w_out)


def _moe_ffn(x, mod, gain, w_router, w1, w3, w2):
    h, gates, pos, post, cnt = _router(x, mod, gain, w_router)
    counts = cnt[:, 0, :N_EXPERTS].astype(jnp.int32)
    y = _moe(h, post, pos, gates, counts, w1, w3, w2)
    return _gated_residual(x, y, mod)


def kernel(x, c, rel_bias_table, norm_mix, norm_ffn, w_mod, b_mod, w_in, q_gain, k_gain, ret_gain, w_out,
           ffn_w1, ffn_w3, ffn_w2, moe_router, moe_w1, moe_w3, moe_w2):
    batch, seq, d_model = x.shape
    assert (seq, d_model) == (SEQ, D_MODEL)
    depth = w_mod.shape[0]
    mods = _modulation(c, w_mod, b_mod).reshape(depth, batch, 6, D_MODEL)
    bias_masks = _bias_masks(rel_bias_table)
    xt = x.reshape(batch * seq, d_model)
    for layer in range(depth):
        mod = mods[layer]
        xt = _token_mixer(xt, mod, norm_mix[layer], w_in[layer], q_gain[layer], k_gain[layer],
                          ret_gain[layer], w_out[layer], bias_masks)
        i = layer // 2
        if layer % 2 == 0:
            xt = _dense_ffn(xt, mod, norm_ffn[layer], ffn_w1[i], ffn_w3[i], ffn_w2[i])
        else:
            xt = _moe_ffn(xt, mod, norm_ffn[layer], moe_router[i], moe_w1[i], moe_w3[i], moe_w2[i])
    return xt.reshape(batch, seq, d_model)
```

```python
import functools
import math

import jax
import jax.numpy as jnp
import numpy as np
from jax import lax
from jax.experimental import pallas as pl
from jax.experimental.pallas import tpu as pltpu

D_MODEL = 1024
SEQ = 2048
ATTN_HEADS = 8
ATTN_HEAD_DIM = 64
ATTN_WIDTH = ATTN_HEADS * ATTN_HEAD_DIM
DILATED_PATTERNS = ((128, 1), (512, 4), (2048, 16))
BLOCK = 128
NUM_BUCKETS = 32
MAX_DISTANCE = 2048
RET_HEADS = 4
RET_KEY_DIM = 64
RET_VALUE_DIM = 128
RET_WIDTH = RET_HEADS * RET_VALUE_DIM
RET_QK_WIDTH = RET_HEADS * RET_KEY_DIM
RET_CHUNK = 128
ROPE_BASE = 10000.0
IN_WIDTH = 3 * ATTN_WIDTH + 2 * RET_QK_WIDTH + 2 * RET_WIDTH
RET_IN_WIDTH = IN_WIDTH - 3 * ATTN_WIDTH
N_EXPERTS = 8
EPS = 1e-6
NEG_INF = -1e30

LANES = 128
VMEM_LIMIT = 56 * 1024 * 1024

BF16 = jnp.bfloat16
F32 = jnp.float32

TM_PROJ = 256
TM_FFN = 512
TF_FFN = 1408
TR_MOE = 256
TF_MOE = 896
assert TM_PROJ == TR_MOE
RET_ROWS = 512


def _cparams(sem):
    return pltpu.CompilerParams(dimension_semantics=sem, vmem_limit_bytes=VMEM_LIMIT)


def _dot(a, b):
    return jnp.dot(a, b, preferred_element_type=F32)


def _dot_nt(a, b):
    return lax.dot_general(a, b, (((1,), (1,)), ((), ())), preferred_element_type=F32)


def _dot_tn(a, b):
    return lax.dot_general(a, b, (((0,), (0,)), ((), ())), preferred_element_type=F32)


def _split_bf16(v):
    hi = v.astype(BF16)
    lo = (v - hi.astype(F32)).astype(BF16)
    return hi, lo


def _silu(v):
    return v * (1.0 / (1.0 + jnp.exp(-v)))


def _modulated_norm(x, gain, scale, shift):
    ms = jnp.mean(x * x, axis=-1, keepdims=True)
    y = x * lax.rsqrt(ms + EPS) * gain
    return y * (1.0 + scale) + shift


def _mod_kernel(c_ref, w_ref, b_ref, o_ref):
    ca = _silu(c_ref[...]).astype(BF16)
    o_ref[...] = _dot(ca, w_ref[...].astype(BF16)) + b_ref[...]


def _modulation(c, w_mod, b_mod):
    depth, _, width = w_mod.shape
    batch = c.shape[0]
    tn = 1536
    return pl.pallas_call(
        _mod_kernel,
        out_shape=jax.ShapeDtypeStruct((depth, batch, width), F32),
        grid=(depth, width // tn),
        in_specs=[
            pl.BlockSpec((batch, D_MODEL), lambda l, n: (0, 0)),
            pl.BlockSpec((None, D_MODEL, tn), lambda l, n: (l, 0, n)),
            pl.BlockSpec((None, 1, tn), lambda l, n: (l, 0, n)),
        ],
        out_specs=pl.BlockSpec((None, batch, tn), lambda l, n: (l, 0, n)),
        compiler_params=_cparams(("arbitrary", "arbitrary")),
        name="adaln_modulation",
    )(c, w_mod, b_mod.reshape(depth, 1, width))


def _bias_kernel(table_ref, bucket_ref, o_ref):
    h = pl.program_id(1)
    bucket = bucket_ref[...]
    acc = jnp.full(bucket.shape, NEG_INF, F32)
    for b in range(NUM_BUCKETS):
        acc = jnp.where(bucket == b, table_ref[b, h], acc)
    o_ref[...] = acc


BIAS_FULL = {1: 0, 4: 2}
BIAS_FIRST = {1: 1, 4: 3, 16: 4}
N_BIAS_SETS = 5


def _bias_masks(rel_bias_table):
    i = jnp.arange(BLOCK)[:, None]
    j = jnp.arange(2 * BLOCK)[None, :]
    max_exact = NUM_BUCKETS // 2

    def bucket_of(rel, dilation, w_sub, exists):
        n = jnp.maximum(rel * dilation, 0)
        nf = jnp.maximum(n.astype(F32), float(max_exact))
        large = max_exact + (jnp.log(nf / max_exact) / math.log(MAX_DISTANCE / max_exact)
                             * (NUM_BUCKETS - max_exact)).astype(jnp.int32)
        large = jnp.minimum(large, NUM_BUCKETS - 1)
        bucket = jnp.where(n < max_exact, n, large)
        allowed = (rel >= 0) & (rel <= w_sub) & exists
        return jnp.where(allowed, bucket, -1)

    sets = [None] * N_BIAS_SETS
    for window, dilation in DILATED_PATTERNS:
        w_sub = window // dilation
        if dilation in BIAS_FULL:
            sets[BIAS_FULL[dilation]] = bucket_of(i - j + BLOCK, dilation, w_sub, j >= 0)
        sets[BIAS_FIRST[dilation]] = bucket_of(i - j, dilation, w_sub, j < BLOCK)
    buckets = jnp.stack(sets).astype(jnp.int32)
    return pl.pallas_call(
        _bias_kernel,
        out_shape=jax.ShapeDtypeStruct((N_BIAS_SETS, ATTN_HEADS // 2, 2 * BLOCK, 2 * BLOCK), F32),
        grid=(N_BIAS_SETS, ATTN_HEADS),
        in_specs=[
            pl.BlockSpec(memory_space=pltpu.SMEM),
            pl.BlockSpec((None, BLOCK, 2 * BLOCK), lambda p, h: (p, 0, 0)),
        ],
        out_specs=pl.BlockSpec((None, None, BLOCK, 2 * BLOCK), lambda p, h: (p, h // 2, h % 2, 0)),
        compiler_params=_cparams(("arbitrary", "arbitrary")),
        name="relative_bias_masks",
    )(rel_bias_table, buckets)


HALF = ATTN_WIDTH // 2


def _in_proj_kernel(x_ref, mod_ref, gain_ref, w_ref, qg_ref, kg_ref, grp_ref,
                    q1_ref, k1_ref, v1_ref, q4_ref, k4_ref, v4_ref, q16_ref, k16_ref, v16_ref, r_ref,
                    perm_ref):
    mod = mod_ref[...]
    h = _modulated_norm(x_ref[...], gain_ref[...], mod[1:2], mod[0:1]).astype(BF16)
    proj = _dot(h, w_ref[...])
    grp = grp_ref[...]
    tm = proj.shape[0]

    def head_norm(t, gain):
        hi, lo = _split_bf16(t * t)
        ss = _dot(hi, grp) + _dot(lo, grp)
        return t * lax.rsqrt(ss * (1.0 / ATTN_HEAD_DIM) + EPS) * gain

    def emit(t, o1_ref, o4_ref, o16_ref):
        o1_ref[...] = t.astype(BF16)
        for j in range(ATTN_WIDTH // LANES):
            perm_ref[j] = t[:, j * LANES:(j + 1) * LANES]
        for dil, o_ref in ((4, o4_ref), (16, o16_ref)):
            for hf in range(2):
                for r in range(dil):
                    for jj in range(HALF // LANES):
                        c0 = (hf * dil + r) * HALF + jj * LANES
                        o_ref[:, c0:c0 + LANES] = perm_ref[hf * (HALF // LANES) + jj,
                                                           pl.ds(r, tm // dil, stride=dil), :].astype(BF16)

    emit(head_norm(proj[:, :ATTN_WIDTH], qg_ref[...]) * (ATTN_HEAD_DIM ** -0.5), q1_ref, q4_ref, q16_ref)
    emit(head_norm(proj[:, ATTN_WIDTH:2 * ATTN_WIDTH], kg_ref[...]), k1_ref, k4_ref, k16_ref)
    emit(proj[:, 2 * ATTN_WIDTH:3 * ATTN_WIDTH], v1_ref, v4_ref, v16_ref)
    r_ref[...] = proj[:, 3 * ATTN_WIDTH:]


def _in_proj(x, mod, gain, w_in, q_gain, k_gain):
    tokens = x.shape[0]
    tm = TM_PROJ
    per_seq = SEQ // tm
    grp = np.kron(np.eye(ATTN_HEADS), np.ones((ATTN_HEAD_DIM, ATTN_HEAD_DIM))).astype(np.float32)
    row = lambda i: (i, 0)
    const = lambda i: (0, 0)
    layouts = []
    for dil in (1, 4, 16):
        shape = jax.ShapeDtypeStruct((tokens // dil, dil * ATTN_WIDTH), BF16)
        spec = pl.BlockSpec((tm // dil, dil * ATTN_WIDTH), row)
        layouts.append(((shape,) * 3, (spec,) * 3))
    out_shape = sum((s for s, _ in layouts), ()) + (jax.ShapeDtypeStruct((tokens, RET_IN_WIDTH), F32),)
    out_specs = sum((s for _, s in layouts), ()) + (pl.BlockSpec((tm, RET_IN_WIDTH), row),)
    return pl.pallas_call(
        _in_proj_kernel,
        out_shape=out_shape,
        grid=(tokens // tm,),
        in_specs=[
            pl.BlockSpec((tm, D_MODEL), row),
            pl.BlockSpec((None, 6, D_MODEL), lambda i: (i // per_seq, 0, 0)),
            pl.BlockSpec((1, D_MODEL), const),
            pl.BlockSpec((D_MODEL, IN_WIDTH), const),
            pl.BlockSpec((1, ATTN_WIDTH), const),
            pl.BlockSpec((1, ATTN_WIDTH), const),
            pl.BlockSpec((ATTN_WIDTH, ATTN_WIDTH), const),
        ],
        out_specs=out_specs,
        scratch_shapes=[pltpu.VMEM((ATTN_WIDTH // LANES, tm, LANES), F32)],
        compiler_params=_cparams(("arbitrary",)),
        name="in_projection",
    )(x, mod, gain.reshape(1, D_MODEL), w_in.astype(BF16),
      jnp.tile(q_gain, ATTN_HEADS).reshape(1, ATTN_WIDTH),
      jnp.tile(k_gain, ATTN_HEADS).reshape(1, ATTN_WIDTH),
      jnp.asarray(grp, BF16))


PAIRS_PER_HALF = ATTN_HEADS // 4
GROUP = 4


def _pair_scores(qp, kp, vp, bias2, masks, low):
    q2 = jnp.concatenate([qp * masks[0], qp * masks[1]], axis=0)
    s = _dot_nt(q2, kp) + bias2
    m = jnp.max(s, axis=-1, keepdims=True)
    p = jnp.exp(s - m)
    den = jnp.sum(p, axis=-1, keepdims=True)
    pv = _dot(p.astype(BF16), vp)
    pick = lambda t: jnp.where(low, t[:BLOCK], t[BLOCK:])
    return pick(pv), pick(m), pick(den)


def _attn_kernel(q1_ref, k1_ref, v1_ref, q4_ref, k4_ref, v4_ref, q16_ref, k16_ref, v16_ref, bm_ref,
                 o_ref, acc_ref, max_ref, den_ref):
    lane = lax.broadcasted_iota(jnp.int32, (BLOCK, LANES), 1)
    low = lane < ATTN_HEAD_DIM
    masks = (jnp.where(low, 1.0, 0.0).astype(BF16), jnp.where(low, 0.0, 1.0).astype(BF16))

    def block(q_ref, k_ref, v_ref, c0, q0, w0, width, bias_set, rows, first):
        for p in range(PAIRS_PER_HALF):
            cs = slice(c0 + p * LANES, c0 + (p + 1) * LANES)
            acc, m, den = _pair_scores(q_ref[pl.ds(q0, BLOCK), cs], k_ref[pl.ds(w0, width), cs],
                                       v_ref[pl.ds(w0, width), cs], bm_ref[bias_set, p, :, 0:width],
                                       masks, low)
            if not first:
                m_old = max_ref[p, rows, :]
                m_new = jnp.maximum(m_old, m)
                a, b = jnp.exp(m_old - m_new), jnp.exp(m - m_new)
                den = den_ref[p, rows, :] * a + den * b
                acc = acc_ref[p, rows, :] * a + acc * b
                m = m_new
            max_ref[p, rows, :] = m
            den_ref[p, rows, :] = den
            acc_ref[p, rows, :] = acc

    def d1_group(g, carry):
        for u in range(GROUP):
            n = g * GROUP + u
            q0 = pl.multiple_of(n * BLOCK, BLOCK)
            w0 = pl.multiple_of(jnp.maximum(n - 1, 0) * BLOCK, BLOCK)
            bias_set = jnp.where(n == 0, BIAS_FIRST[1], BIAS_FULL[1])
            block(q1_ref, k1_ref, v1_ref, 0, q0, w0, 2 * BLOCK, bias_set, pl.ds(q0, BLOCK), True)
        return carry
    lax.fori_loop(0, SEQ // BLOCK // GROUP, d1_group, 0)

    for r in range(4):
        for n in range(SEQ // 4 // BLOCK):
            block(q4_ref, k4_ref, v4_ref, r * HALF, n * BLOCK, max(n - 1, 0) * BLOCK, 2 * BLOCK,
                  BIAS_FIRST[4] if n == 0 else BIAS_FULL[4],
                  pl.ds(r + 4 * BLOCK * n, BLOCK, stride=4), False)

    for r in range(16):
        block(q16_ref, k16_ref, v16_ref, r * HALF, 0, 0, BLOCK, BIAS_FIRST[16],
              pl.ds(r, BLOCK, stride=16), False)

    for n in range(SEQ // BLOCK):
        rows = slice(n * BLOCK, (n + 1) * BLOCK)
        for p in range(PAIRS_PER_HALF):
            o_ref[rows, p * LANES:(p + 1) * LANES] = (acc_ref[p, rows, :] / den_ref[p, rows, :]).astype(BF16)


def _dilated_attention(qkv, bias_masks):
    tokens = qkv[0].shape[0]
    batch = tokens // SEQ
    specs = []
    for dil in (1, 4, 16):
        specs += [pl.BlockSpec((SEQ // dil, dil * HALF), lambda b, hf: (b, hf))] * 3
    state = pltpu.VMEM((PAIRS_PER_HALF, SEQ, LANES), F32)
    return pl.pallas_call(
        _attn_kernel,
        out_shape=jax.ShapeDtypeStruct((tokens, ATTN_WIDTH), BF16),
        grid=(batch, 2),
        in_specs=specs + [pl.BlockSpec((N_BIAS_SETS, PAIRS_PER_HALF, 2 * BLOCK, 2 * BLOCK),
                                       lambda b, hf: (0, hf, 0, 0))],
        out_specs=pl.BlockSpec((SEQ, HALF), lambda b, hf: (b, hf)),
        scratch_shapes=[state, state, state],
        compiler_params=_cparams(("arbitrary", "arbitrary")),
        name="dilated_attention",
    )(*qkv, bias_masks)


def _retention_kernel(r_ref, cos_ref, sin_ref, dmask_ref, qdec_ref, kdec_ref, cdec_ref, gain_ref,
                      o_ref, state_ref):
    @pl.when(pl.program_id(1) == 0)
    def _():
        state_ref[...] = jnp.zeros_like(state_ref)

    lane = lax.broadcasted_iota(jnp.int32, (RET_CHUNK, LANES), 1)
    low = lane < RET_KEY_DIM
    first_half = (lane % RET_KEY_DIM) < (RET_KEY_DIM // 2)

    def rotate(t, cos, sin):
        partner = jnp.where(first_half, pltpu.roll(t, LANES - RET_KEY_DIM // 2, 1),
                            pltpu.roll(t, RET_KEY_DIM // 2, 1))
        return t * cos + partner * sin

    for c in range(RET_ROWS // RET_CHUNK):
        rows = slice(c * RET_CHUNK, (c + 1) * RET_CHUNK)
        for hp in range(RET_HEADS // 2):
            qs = slice(hp * LANES, (hp + 1) * LANES)
            ks = slice(RET_QK_WIDTH + hp * LANES, RET_QK_WIDTH + (hp + 1) * LANES)
            cos, sin = cos_ref[rows, qs], sin_ref[rows, qs]
            q_pair = rotate(r_ref[rows, qs], cos, sin) * (RET_KEY_DIM ** -0.5)
            k_pair = rotate(r_ref[rows, ks], cos, sin)
            for hh in range(2):
                head = 2 * hp + hh
                vs = slice(2 * RET_QK_WIDTH + head * LANES, 2 * RET_QK_WIDTH + (head + 1) * LANES)
                gs = slice(2 * RET_QK_WIDTH + RET_WIDTH + head * LANES,
                           2 * RET_QK_WIDTH + RET_WIDTH + (head + 1) * LANES)
                keep = low if hh == 0 else jnp.logical_not(low)
                qm = jnp.where(keep, q_pair, 0.0)
                vb = r_ref[rows, vs].astype(BF16)
                state = state_ref[head]
                inner = _dot_nt(qm.astype(BF16), k_pair.astype(BF16)) * dmask_ref[head]
                y = _dot(inner.astype(BF16), vb)
                y = y + _dot((qm * qdec_ref[head]).astype(BF16), state.astype(BF16))
                state_ref[head] = state * cdec_ref[head] + _dot_tn((k_pair * kdec_ref[head]).astype(BF16), vb)
                mu = jnp.mean(y, axis=-1, keepdims=True)
                yc = y - mu
                var = jnp.mean(yc * yc, axis=-1, keepdims=True)
                yn = yc * lax.rsqrt(var + EPS) * gain_ref[:, head * LANES:(head + 1) * LANES]
                o_ref[rows, head * LANES:(head + 1) * LANES] = (_silu(r_ref[rows, gs]) * yn).astype(BF16)


def _retention_tables():
    half = RET_KEY_DIM // 2
    pos = jnp.arange(SEQ, dtype=F32)
    inv = ROPE_BASE ** (-jnp.arange(half, dtype=F32) / half)
    ang = pos[:, None] * inv[None, :]
    cos, sin = jnp.cos(ang), jnp.sin(ang)
    cos_full = jnp.tile(jnp.concatenate([cos, cos], axis=-1), (1, RET_HEADS))
    sin_signed = jnp.tile(jnp.concatenate([-sin, sin], axis=-1), (1, RET_HEADS))
    log_g = jnp.log(1.0 - 2.0 ** (-5.0 - jnp.arange(RET_HEADS, dtype=F32)))
    idx = jnp.arange(RET_CHUNK, dtype=F32)
    diff = idx[:, None] - idx[None, :]
    dmask = jnp.where(diff >= 0, jnp.exp(jnp.maximum(diff, 0.0)[None] * log_g[:, None, None]), 0.0)
    q_decay = jnp.exp((idx + 1.0)[None, :] * log_g[:, None])[..., None]
    k_decay = jnp.exp((RET_CHUNK - 1.0 - idx)[None, :] * log_g[:, None])[..., None]
    chunk_decay = jnp.exp(RET_CHUNK * log_g)[:, None, None]
    full = (RET_HEADS, RET_CHUNK, LANES)
    return (cos_full, sin_signed, dmask, jnp.broadcast_to(q_decay, full),
            jnp.broadcast_to(k_decay, full), jnp.broadcast_to(chunk_decay, full))


def _retention(ret_in, ret_gain):
    tokens = ret_in.shape[0]
    batch = tokens // SEQ
    per_seq = SEQ // RET_ROWS
    cos, sin, dmask, qdec, kdec, cdec = _retention_tables()
    tab = pl.BlockSpec((RET_ROWS, RET_QK_WIDTH), lambda b, j: (j, 0))
    const3 = pl.BlockSpec((RET_HEADS, RET_CHUNK, LANES), lambda b, j: (0, 0, 0))
    return pl.pallas_call(
        _retention_kernel,
        out_shape=jax.ShapeDtypeStruct((tokens, RET_WIDTH), BF16),
        grid=(batch, per_seq),
        in_specs=[
            pl.BlockSpec((RET_ROWS, RET_IN_WIDTH), lambda b, j: (b * per_seq + j, 0)),
            tab, tab, const3, const3, const3, const3,
            pl.BlockSpec((1, RET_WIDTH), lambda b, j: (0, 0)),
        ],
        out_specs=pl.BlockSpec((RET_ROWS, RET_WIDTH), lambda b, j: (b * per_seq + j, 0)),
        scratch_shapes=[pltpu.VMEM((RET_HEADS, LANES, RET_VALUE_DIM), F32)],
        compiler_params=_cparams(("arbitrary", "arbitrary")),
        name="retention",
    )(ret_in, cos, sin, dmask, qdec, kdec, cdec, ret_gain.reshape(1, RET_WIDTH))


def _out_proj_kernel(attn_ref, ret_ref, x_ref, mod_ref, w_ref, out_ref):
    mix = _dot(attn_ref[...], w_ref[:ATTN_WIDTH, :]) + _dot(ret_ref[...], w_ref[ATTN_WIDTH:, :])
    out_ref[...] = x_ref[...] + mod_ref[2:3, :] * mix


def _out_proj(attn, ret, x, mod, w_out):
    tokens = x.shape[0]
    tm = TM_FFN
    per_seq = SEQ // tm
    row = lambda i: (i, 0)
    return pl.pallas_call(
        _out_proj_kernel,
        out_shape=jax.ShapeDtypeStruct((tokens, D_MODEL), F32),
        grid=(tokens // tm,),
        in_specs=[pl.BlockSpec((tm, ATTN_WIDTH), row),
                  pl.BlockSpec((tm, RET_WIDTH), row),
                  pl.BlockSpec((tm, D_MODEL), row),
                  pl.BlockSpec((None, 6, D_MODEL), lambda i: (i // per_seq, 0, 0)),
                  pl.BlockSpec((D_MODEL, D_MODEL), lambda i: (0, 0))],
        out_specs=pl.BlockSpec((tm, D_MODEL), row),
        compiler_params=_cparams(("arbitrary",)),
        name="out_projection",
    )(attn, ret, x, mod, w_out.astype(BF16))


def _ffn_kernel(x_ref, mod_ref, gain_ref, w1_ref, w3_ref, w2_ref, out_ref, h_ref, acc_ref):
    f = pl.program_id(1)

    @pl.when(f == 0)
    def _():
        mod = mod_ref[...]
        h_ref[...] = _modulated_norm(x_ref[...], gain_ref[...], mod[4:5], mod[3:4]).astype(BF16)
        acc_ref[...] = jnp.zeros_like(acc_ref)

    h = h_ref[...]
    z = (_silu(_dot(h, w1_ref[...])) * _dot(h, w3_ref[...])).astype(BF16)
    acc_ref[...] += _dot(z, w2_ref[...])

    @pl.when(f == pl.num_programs(1) - 1)
    def _():
        out_ref[...] = x_ref[...] + mod_ref[5:6, :] * acc_ref[...]


def _dense_ffn(x, mod, gain, w1, w3, w2):
    tokens = x.shape[0]
    d_ff = w1.shape[1]
    tm, tf = TM_FFN, TF_FFN
    per_seq = SEQ // tm
    return pl.pallas_call(
        _ffn_kernel,
        out_shape=jax.ShapeDtypeStruct((tokens, D_MODEL), F32),
        grid=(tokens // tm, d_ff // tf),
        in_specs=[
            pl.BlockSpec((tm, D_MODEL), lambda i, f: (i, 0)),
            pl.BlockSpec((None, 6, D_MODEL), lambda i, f: (i // per_seq, 0, 0)),
            pl.BlockSpec((1, D_MODEL), lambda i, f: (0, 0)),
            pl.BlockSpec((D_MODEL, tf), lambda i, f: (0, f)),
            pl.BlockSpec((D_MODEL, tf), lambda i, f: (0, f)),
            pl.BlockSpec((tf, D_MODEL), lambda i, f: (f, 0)),
        ],
        out_specs=pl.BlockSpec((tm, D_MODEL), lambda i, f: (i, 0)),
        scratch_shapes=[pltpu.VMEM((tm, D_MODEL), BF16), pltpu.VMEM((tm, D_MODEL), F32)],
        compiler_params=_cparams(("arbitrary", "arbitrary")),
        name="dense_swiglu",
    )(x, mod, gain.reshape(1, D_MODEL), w1.astype(BF16), w3.astype(BF16), w2.astype(BF16))


def _router_kernel(x_ref, mod_ref, gain_ref, wr_ref, h_ref, gates_ref, pos_ref, post_ref, start_ref,
                   cnt_ref, carry_ref, *, tiles_per_seq):
    i = pl.program_id(0)

    @pl.when(i % tiles_per_seq == 0)
    def _():
        carry_ref[...] = jnp.zeros_like(carry_ref)

    mod = mod_ref[...]
    h = _modulated_norm(x_ref[...], gain_ref[...], mod[4:5], mod[3:4]).astype(BF16)
    h_ref[...] = h
    tm = h.shape[0]
    lane = lax.broadcasted_iota(jnp.int32, (tm, LANES), 1).astype(F32)
    logits = jnp.where(lane < N_EXPERTS, _dot(h, wr_ref[...]), -jnp.inf)
    m1 = jnp.max(logits, axis=-1, keepdims=True)
    i1 = jnp.min(jnp.where(logits == m1, lane, float(LANES)), axis=-1, keepdims=True)
    rest = jnp.where(lane == i1, -jnp.inf, logits)
    m2 = jnp.max(rest, axis=-1, keepdims=True)
    i2 = jnp.min(jnp.where(rest == m2, lane, float(LANES)), axis=-1, keepdims=True)
    e2 = jnp.exp(m2 - m1)
    g1 = 1.0 / (1.0 + e2)
    g2 = e2 / (1.0 + e2)
    gates_ref[...] = jnp.where(lane == i1, g1, 0.0) + jnp.where(lane == i2, g2, 0.0)
    chosen = (lane == i1) | (lane == i2)
    onehot = jnp.where(chosen, 1.0, 0.0)
    r = lax.broadcasted_iota(jnp.int32, (tm, tm), 0)
    c = lax.broadcasted_iota(jnp.int32, (tm, tm), 1)
    tril = jnp.where(c <= r, 1.0, 0.0).astype(BF16)
    incl = _dot(tril, onehot.astype(BF16))
    carry = carry_ref[0:1, :]
    start_ref[...] = carry_ref[...]
    pos = jnp.where(chosen, incl - 1.0 + carry, -1.0)
    pos_ref[...] = pos
    post_ref[...] = pos.T[:N_EXPERTS, :]
    total = carry + incl[tm - 1:tm, :]
    carry_ref[...] = jnp.broadcast_to(total, carry_ref.shape)
    cnt_ref[...] = jnp.broadcast_to(total, cnt_ref.shape)


def _router(x, mod, gain, w_router):
    tokens = x.shape[0]
    batch = tokens // SEQ
    tm = TM_PROJ
    per_seq = SEQ // tm
    wr = jnp.zeros((D_MODEL, LANES), BF16).at[:, :N_EXPERTS].set(w_router.astype(BF16))
    row = lambda i: (i, 0)
    return pl.pallas_call(
        functools.partial(_router_kernel, tiles_per_seq=per_seq),
        out_shape=(
            jax.ShapeDtypeStruct((tokens, D_MODEL), BF16),
            jax.ShapeDtypeStruct((tokens, LANES), F32),
            jax.ShapeDtypeStruct((tokens, LANES), F32),
            jax.ShapeDtypeStruct((N_EXPERTS, tokens), F32),
            jax.ShapeDtypeStruct((tokens // tm, 8, LANES), F32),
            jax.ShapeDtypeStruct((batch, 8, LANES), F32),
        ),
        grid=(tokens // tm,),
        in_specs=[
            pl.BlockSpec((tm, D_MODEL), row),
            pl.BlockSpec((None, 6, D_MODEL), lambda i: (i // per_seq, 0, 0)),
            pl.BlockSpec((1, D_MODEL), lambda i: (0, 0)),
            pl.BlockSpec((D_MODEL, LANES), lambda i: (0, 0)),
        ],
        out_specs=(
            pl.BlockSpec((tm, D_MODEL), row),
            pl.BlockSpec((tm, LANES), row),
            pl.BlockSpec((tm, LANES), row),
            pl.BlockSpec((N_EXPERTS, tm), lambda i: (0, i)),
            pl.BlockSpec((None, 8, LANES), lambda i: (i, 0, 0)),
            pl.BlockSpec((None, 8, LANES), lambda i: (i // per_seq, 0, 0)),
        ),
        scratch_shapes=[pltpu.VMEM((8, LANES), F32)],
        compiler_params=_cparams(("arbitrary",)),
        name="expert_router",
    )(x, mod, gain.reshape(1, D_MODEL), wr)


TOK_BLOCKS = SEQ // TR_MOE


def _moe_kernel(tbl_ref, h_ref, post_ref, pos_ref, gates_ref, x_ref, mod_ref, w1_ref, w3_ref, w2_ref,
                out_ref, hs_ref, acc_ref):
    b, e, f = pl.program_id(0), pl.program_id(1), pl.program_id(2)
    last_f = pl.num_programs(2) - 1
    tr = TR_MOE
    base = b * (TOK_BLOCKS + 1) * N_EXPERTS + e
    bounds = [tbl_ref[base + tb * N_EXPERTS] for tb in range(TOK_BLOCKS + 1)]
    n_tiles = lax.shift_right_logical(bounds[-1] + (tr - 1), tr.bit_length() - 1)

    def overlaps(r0, tb):
        return (bounds[tb] < r0 + tr) & (bounds[tb + 1] > r0)

    @pl.when((e == 0) & (f == 0))
    def _():
        out_ref[...] = jnp.zeros_like(out_ref)

    @pl.when(f == 0)
    def _():
        def gather(r, carry):
            r0 = pl.multiple_of(r * tr, tr)
            rows = pl.ds(r0, tr)
            acc_ref[rows, :] = jnp.zeros((tr, D_MODEL), F32)
            slot = (lax.broadcasted_iota(jnp.int32, (tr, tr), 0) + r0).astype(F32)
            for tb in range(TOK_BLOCKS):
                @pl.when(overlaps(r0, tb))
                def _():
                    ts = slice(tb * tr, (tb + 1) * tr)
                    sel = jnp.where(post_ref[pl.ds(e, 1), ts] == slot, 1.0, 0.0).astype(BF16)
                    acc_ref[rows, :] += _dot(sel, h_ref[ts, :])
            hs_ref[rows, :] = acc_ref[rows, :].astype(BF16)
            acc_ref[rows, :] = jnp.zeros((tr, D_MODEL), F32)
            return carry
        lax.fori_loop(0, n_tiles, gather, 0)

    def swiglu(r, carry):
        r0 = pl.multiple_of(r * tr, tr)
        hr = hs_ref[pl.ds(r0, tr), :]
        z = (_silu(_dot(hr, w1_ref[...])) * _dot(hr, w3_ref[...])).astype(BF16)
        acc_ref[pl.ds(r0, tr), :] += _dot(z, w2_ref[...])
        return carry
    lax.fori_loop(0, n_tiles, swiglu, 0)

    @pl.when(f == last_f)
    def _():
        lane = lax.broadcasted_iota(jnp.int32, (tr, LANES), 1)

        def scatter(r, carry):
            r0 = pl.multiple_of(r * tr, tr)
            rows = acc_ref[pl.ds(r0, tr), :].astype(BF16)
            slot = (lax.broadcasted_iota(jnp.int32, (tr, tr), 1) + r0).astype(F32)
            for tb in range(TOK_BLOCKS):
                @pl.when(overlaps(r0, tb))
                def _():
                    ts = slice(tb * tr, (tb + 1) * tr)
                    pos_col = jnp.sum(jnp.where(lane == e, pos_ref[ts, :], 0.0), axis=-1, keepdims=True)
                    gate_col = jnp.sum(jnp.where(lane == e, gates_ref[ts, :], 0.0), axis=-1, keepdims=True)
                    sel = jnp.where(pos_col == slot, 1.0, 0.0).astype(BF16)
                    out_ref[ts, :] += gate_col * _dot(sel, rows)
            return carry
        lax.fori_loop(0, n_tiles, scatter, 0)

    @pl.when((e == N_EXPERTS - 1) & (f == last_f))
    def _():
        out_ref[...] = x_ref[...] + mod_ref[5:6, :] * out_ref[...]


def _moe(x, mod, h, post, pos, gates, table, w1, w3, w2):
    tokens = h.shape[0]
    batch = tokens // SEQ
    d_ff = w1.shape[2]
    tf = TF_MOE
    n_f = d_ff // tf
    chunked = lambda w: w.reshape(N_EXPERTS, D_MODEL, n_f, tf).transpose(0, 2, 1, 3).astype(BF16)
    once = pl.Buffered(1)
    grid_spec = pltpu.PrefetchScalarGridSpec(
        num_scalar_prefetch=1,
        grid=(batch, N_EXPERTS, n_f),
        in_specs=[
            pl.BlockSpec((SEQ, D_MODEL), lambda b, e, f, t: (b, 0), pipeline_mode=once),
            pl.BlockSpec((N_EXPERTS, SEQ), lambda b, e, f, t: (0, b), pipeline_mode=once),
            pl.BlockSpec((SEQ, LANES), lambda b, e, f, t: (b, 0), pipeline_mode=once),
            pl.BlockSpec((SEQ, LANES), lambda b, e, f, t: (b, 0), pipeline_mode=once),
            pl.BlockSpec((SEQ, D_MODEL), lambda b, e, f, t: (b, 0), pipeline_mode=once),
            pl.BlockSpec((None, 6, D_MODEL), lambda b, e, f, t: (b, 0, 0)),
            pl.BlockSpec((None, None, D_MODEL, tf), lambda b, e, f, t: (e, f, 0, 0)),
            pl.BlockSpec((None, None, D_MODEL, tf), lambda b, e, f, t: (e, f, 0, 0)),
            pl.BlockSpec((None, tf, D_MODEL), lambda b, e, f, t: (e, f, 0)),
        ],
        out_specs=pl.BlockSpec((SEQ, D_MODEL), lambda b, e, f, t: (b, 0), pipeline_mode=once),
        scratch_shapes=[pltpu.VMEM((SEQ, D_MODEL), BF16), pltpu.VMEM((SEQ, D_MODEL), F32)],
    )
    return pl.pallas_call(
        _moe_kernel,
        out_shape=jax.ShapeDtypeStruct((tokens, D_MODEL), F32),
        grid_spec=grid_spec,
        compiler_params=_cparams(("arbitrary", "arbitrary", "arbitrary")),
        name="expert_swiglu",
    )(table, h, post, pos, gates, x, mod, chunked(w1), chunked(w3), w2.astype(BF16))


def _token_mixer(x, mod, gain, w_in, q_gain, k_gain, ret_gain, w_out, bias_masks):
    *qkv, ret_in = _in_proj(x, mod, gain, w_in, q_gain, k_gain)
    attn = _dilated_attention(qkv, bias_masks)
    ret = _retention(ret_in, ret_gain)
    return _out_proj(attn, ret, x, mod, w_out)


def _moe_ffn(x, mod, gain, w_router, w1, w3, w2):
    h, gates, pos, post, start, cnt = _router(x, mod, gain, w_router)
    batch = cnt.shape[0]
    table = jnp.concatenate([start[:, 0, :N_EXPERTS].reshape(batch, TOK_BLOCKS, N_EXPERTS),
                             cnt[:, :1, :N_EXPERTS]], axis=1).astype(jnp.int32).reshape(-1)
    return _moe(x, mod, h, post, pos, gates, table, w1, w3, w2)


def kernel(x, c, rel_bias_table, norm_mix, norm_ffn, w_mod, b_mod, w_in, q_gain, k_gain, ret_gain, w_out,
           ffn_w1, ffn_w3, ffn_w2, moe_router, moe_w1, moe_w3, moe_w2):
    batch, seq, d_model = x.shape
    assert (seq, d_model) == (SEQ, D_MODEL)
    depth = w_mod.shape[0]
    mods = _modulation(c, w_mod, b_mod).reshape(depth, batch, 6, D_MODEL)
    bias_masks = _bias_masks(rel_bias_table)
    xt = x.reshape(batch * seq, d_model)
    for layer in range(depth):
        mod = mods[layer]
        xt = _token_mixer(xt, mod, norm_mix[layer], w_in[layer], q_gain[layer], k_gain[layer],
                          ret_gain[layer], w_out[layer], bias_masks)
        i = layer // 2
        if layer % 2 == 0:
            xt = _dense_ffn(xt, mod, norm_ffn[layer], ffn_w1[i], ffn_w3[i], ffn_w2[i])
        else:
            xt = _moe_ffn(xt, mod, norm_ffn[layer], moe_router[i], moe_w1[i], moe_w3[i], moe_w2[i])
    return xt.reshape(batch, seq, d_model)
```

```python
import functools
import math

import jax
import jax.numpy as jnp
import numpy as np
from jax import lax
from jax.experimental import pallas as pl
from jax.experimental.pallas import tpu as pltpu

D_MODEL = 1024
SEQ = 2048
ATTN_HEADS = 8
ATTN_HEAD_DIM = 64
ATTN_WIDTH = ATTN_HEADS * ATTN_HEAD_DIM
DILATED_PATTERNS = ((128, 1), (512, 4), (2048, 16))
BLOCK = 128
NUM_BUCKETS = 32
MAX_DISTANCE = 2048
RET_HEADS = 4
RET_KEY_DIM = 64
RET_VALUE_DIM = 128
RET_WIDTH = RET_HEADS * RET_VALUE_DIM
RET_QK_WIDTH = RET_HEADS * RET_KEY_DIM
RET_CHUNK = 128
ROPE_BASE = 10000.0
IN_WIDTH = 3 * ATTN_WIDTH + 2 * RET_QK_WIDTH + 2 * RET_WIDTH
RET_IN_WIDTH = IN_WIDTH - 3 * ATTN_WIDTH
N_EXPERTS = 8
EPS = 1e-6
NEG_INF = -1e30

LANES = 128
VMEM_LIMIT = 56 * 1024 * 1024

BF16 = jnp.bfloat16
F32 = jnp.float32

TM_PROJ = 256
TM_FFN = 512
TF_FFN = 1408
TR_MOE = 256
TF_MOE = 896
assert TM_PROJ == TR_MOE
RET_ROWS = 512
ROW_ALIGN = 16


def _cparams(sem):
    return pltpu.CompilerParams(dimension_semantics=sem, vmem_limit_bytes=VMEM_LIMIT)


def _dot(a, b):
    return jnp.dot(a, b, preferred_element_type=F32)


def _dot_nt(a, b):
    return lax.dot_general(a, b, (((1,), (1,)), ((), ())), preferred_element_type=F32)


def _dot_tn(a, b):
    return lax.dot_general(a, b, (((0,), (0,)), ((), ())), preferred_element_type=F32)


def _split_bf16(v):
    hi = v.astype(BF16)
    lo = (v - hi.astype(F32)).astype(BF16)
    return hi, lo


def _silu(v):
    return v * (1.0 / (1.0 + jnp.exp(-v)))


def _modulated_norm(x, gain, scale, shift):
    ms = jnp.mean(x * x, axis=-1, keepdims=True)
    y = x * lax.rsqrt(ms + EPS) * gain
    return y * (1.0 + scale) + shift


def _mod_kernel(c_ref, w_ref, b_ref, o_ref):
    ca = _silu(c_ref[...]).astype(BF16)
    o_ref[...] = _dot(ca, w_ref[...].astype(BF16)) + b_ref[...]


def _modulation(c, w_mod, b_mod):
    depth, _, width = w_mod.shape
    batch = c.shape[0]
    tn = 1536
    return pl.pallas_call(
        _mod_kernel,
        out_shape=jax.ShapeDtypeStruct((depth, batch, width), F32),
        grid=(depth, width // tn),
        in_specs=[
            pl.BlockSpec((batch, D_MODEL), lambda l, n: (0, 0)),
            pl.BlockSpec((None, D_MODEL, tn), lambda l, n: (l, 0, n)),
            pl.BlockSpec((None, 1, tn), lambda l, n: (l, 0, n)),
        ],
        out_specs=pl.BlockSpec((None, batch, tn), lambda l, n: (l, 0, n)),
        compiler_params=_cparams(("arbitrary", "arbitrary")),
        name="adaln_modulation",
    )(c, w_mod, b_mod.reshape(depth, 1, width))


def _bias_kernel(table_ref, bucket_ref, o_ref):
    h = pl.program_id(1)
    bucket = bucket_ref[...]
    acc = jnp.full(bucket.shape, NEG_INF, F32)
    for b in range(NUM_BUCKETS):
        acc = jnp.where(bucket == b, table_ref[b, h], acc)
    o_ref[...] = acc


BIAS_FULL = {1: 0, 4: 2}
BIAS_FIRST = {1: 1, 4: 3, 16: 4}
N_BIAS_SETS = 5


def _bias_masks(rel_bias_table):
    i = jnp.arange(BLOCK)[:, None]
    j = jnp.arange(2 * BLOCK)[None, :]
    max_exact = NUM_BUCKETS // 2

    def bucket_of(rel, dilation, w_sub, exists):
        n = jnp.maximum(rel * dilation, 0)
        nf = jnp.maximum(n.astype(F32), float(max_exact))
        large = max_exact + (jnp.log(nf / max_exact) / math.log(MAX_DISTANCE / max_exact)
                             * (NUM_BUCKETS - max_exact)).astype(jnp.int32)
        large = jnp.minimum(large, NUM_BUCKETS - 1)
        bucket = jnp.where(n < max_exact, n, large)
        allowed = (rel >= 0) & (rel <= w_sub) & exists
        return jnp.where(allowed, bucket, -1)

    sets = [None] * N_BIAS_SETS
    for window, dilation in DILATED_PATTERNS:
        w_sub = window // dilation
        if dilation in BIAS_FULL:
            sets[BIAS_FULL[dilation]] = bucket_of(i - j + BLOCK, dilation, w_sub, j >= 0)
        sets[BIAS_FIRST[dilation]] = bucket_of(i - j, dilation, w_sub, j < BLOCK)
    buckets = jnp.stack(sets).astype(jnp.int32)
    return pl.pallas_call(
        _bias_kernel,
        out_shape=jax.ShapeDtypeStruct((N_BIAS_SETS, ATTN_HEADS // 2, 2 * BLOCK, 2 * BLOCK), F32),
        grid=(N_BIAS_SETS, ATTN_HEADS),
        in_specs=[
            pl.BlockSpec(memory_space=pltpu.SMEM),
            pl.BlockSpec((None, BLOCK, 2 * BLOCK), lambda p, h: (p, 0, 0)),
        ],
        out_specs=pl.BlockSpec((None, None, BLOCK, 2 * BLOCK), lambda p, h: (p, h // 2, h % 2, 0)),
        compiler_params=_cparams(("arbitrary", "arbitrary")),
        name="relative_bias_masks",
    )(rel_bias_table, buckets)


HALF = ATTN_WIDTH // 2


def _in_proj_kernel(x_ref, mod_ref, gain_ref, w_ref, qg_ref, kg_ref, grp_ref,
                    q1_ref, k1_ref, v1_ref, q4_ref, k4_ref, v4_ref, q16_ref, k16_ref, v16_ref, r_ref,
                    perm_ref):
    mod = mod_ref[...]
    h = _modulated_norm(x_ref[...], gain_ref[...], mod[1:2], mod[0:1]).astype(BF16)
    proj = _dot(h, w_ref[...])
    grp = grp_ref[...]
    tm = proj.shape[0]

    def head_norm(t, gain):
        hi, lo = _split_bf16(t * t)
        ss = _dot(hi, grp) + _dot(lo, grp)
        return t * lax.rsqrt(ss * (1.0 / ATTN_HEAD_DIM) + EPS) * gain

    def emit(t, o1_ref, o4_ref, o16_ref):
        o1_ref[...] = t.astype(BF16)
        for j in range(ATTN_WIDTH // LANES):
            perm_ref[j] = t[:, j * LANES:(j + 1) * LANES]
        for dil, o_ref in ((4, o4_ref), (16, o16_ref)):
            for hf in range(2):
                for r in range(dil):
                    for jj in range(HALF // LANES):
                        c0 = (hf * dil + r) * HALF + jj * LANES
                        o_ref[:, c0:c0 + LANES] = perm_ref[hf * (HALF // LANES) + jj,
                                                           pl.ds(r, tm // dil, stride=dil), :].astype(BF16)

    emit(head_norm(proj[:, :ATTN_WIDTH], qg_ref[...]) * (ATTN_HEAD_DIM ** -0.5), q1_ref, q4_ref, q16_ref)
    emit(head_norm(proj[:, ATTN_WIDTH:2 * ATTN_WIDTH], kg_ref[...]), k1_ref, k4_ref, k16_ref)
    emit(proj[:, 2 * ATTN_WIDTH:3 * ATTN_WIDTH], v1_ref, v4_ref, v16_ref)
    r_ref[...] = proj[:, 3 * ATTN_WIDTH:]


def _in_proj(x, mod, gain, w_in, q_gain, k_gain):
    tokens = x.shape[0]
    tm = TM_PROJ
    per_seq = SEQ // tm
    grp = np.kron(np.eye(ATTN_HEADS), np.ones((ATTN_HEAD_DIM, ATTN_HEAD_DIM))).astype(np.float32)
    row = lambda i: (i, 0)
    const = lambda i: (0, 0)
    layouts = []
    for dil in (1, 4, 16):
        shape = jax.ShapeDtypeStruct((tokens // dil, dil * ATTN_WIDTH), BF16)
        spec = pl.BlockSpec((tm // dil, dil * ATTN_WIDTH), row)
        layouts.append(((shape,) * 3, (spec,) * 3))
    out_shape = sum((s for s, _ in layouts), ()) + (jax.ShapeDtypeStruct((tokens, RET_IN_WIDTH), F32),)
    out_specs = sum((s for _, s in layouts), ()) + (pl.BlockSpec((tm, RET_IN_WIDTH), row),)
    return pl.pallas_call(
        _in_proj_kernel,
        out_shape=out_shape,
        grid=(tokens // tm,),
        in_specs=[
            pl.BlockSpec((tm, D_MODEL), row),
            pl.BlockSpec((None, 6, D_MODEL), lambda i: (i // per_seq, 0, 0)),
            pl.BlockSpec((1, D_MODEL), const),
            pl.BlockSpec((D_MODEL, IN_WIDTH), const),
            pl.BlockSpec((1, ATTN_WIDTH), const),
            pl.BlockSpec((1, ATTN_WIDTH), const),
            pl.BlockSpec((ATTN_WIDTH, ATTN_WIDTH), const),
        ],
        out_specs=out_specs,
        scratch_shapes=[pltpu.VMEM((ATTN_WIDTH // LANES, tm, LANES), F32)],
        compiler_params=_cparams(("arbitrary",)),
        name="in_projection",
    )(x, mod, gain.reshape(1, D_MODEL), w_in.astype(BF16),
      jnp.tile(q_gain, ATTN_HEADS).reshape(1, ATTN_WIDTH),
      jnp.tile(k_gain, ATTN_HEADS).reshape(1, ATTN_WIDTH),
      jnp.asarray(grp, BF16))


PAIRS_PER_HALF = ATTN_HEADS // 4
GROUP = 4


def _pair_scores(qp, kp, vp, bias2, masks, low):
    q2 = jnp.concatenate([qp * masks[0], qp * masks[1]], axis=0)
    s = _dot_nt(q2, kp) + bias2
    m = jnp.max(s, axis=-1, keepdims=True)
    p = jnp.exp(s - m)
    den = jnp.sum(p, axis=-1, keepdims=True)
    pv = _dot(p.astype(BF16), vp)
    pick = lambda t: jnp.where(low, t[:BLOCK], t[BLOCK:])
    return pick(pv), pick(m), pick(den)


def _attn_kernel(q1_ref, k1_ref, v1_ref, q4_ref, k4_ref, v4_ref, q16_ref, k16_ref, v16_ref, bm_ref,
                 o_ref, acc_ref, max_ref, den_ref):
    lane = lax.broadcasted_iota(jnp.int32, (BLOCK, LANES), 1)
    low = lane < ATTN_HEAD_DIM
    masks = (jnp.where(low, 1.0, 0.0).astype(BF16), jnp.where(low, 0.0, 1.0).astype(BF16))

    def block(q_ref, k_ref, v_ref, c0, q0, w0, width, bias_set, rows, first):
        for p in range(PAIRS_PER_HALF):
            cs = slice(c0 + p * LANES, c0 + (p + 1) * LANES)
            acc, m, den = _pair_scores(q_ref[pl.ds(q0, BLOCK), cs], k_ref[pl.ds(w0, width), cs],
                                       v_ref[pl.ds(w0, width), cs], bm_ref[bias_set, p, :, 0:width],
                                       masks, low)
            if not first:
                m_old = max_ref[p, rows, :]
                m_new = jnp.maximum(m_old, m)
                a, b = jnp.exp(m_old - m_new), jnp.exp(m - m_new)
                den = den_ref[p, rows, :] * a + den * b
                acc = acc_ref[p, rows, :] * a + acc * b
                m = m_new
            max_ref[p, rows, :] = m
            den_ref[p, rows, :] = den
            acc_ref[p, rows, :] = acc

    def d1_group(g, carry):
        for u in range(GROUP):
            n = g * GROUP + u
            q0 = pl.multiple_of(n * BLOCK, BLOCK)
            w0 = pl.multiple_of(jnp.maximum(n - 1, 0) * BLOCK, BLOCK)
            bias_set = jnp.where(n == 0, BIAS_FIRST[1], BIAS_FULL[1])
            block(q1_ref, k1_ref, v1_ref, 0, q0, w0, 2 * BLOCK, bias_set, pl.ds(q0, BLOCK), True)
        return carry
    lax.fori_loop(0, SEQ // BLOCK // GROUP, d1_group, 0)

    for r in range(4):
        for n in range(SEQ // 4 // BLOCK):
            block(q4_ref, k4_ref, v4_ref, r * HALF, n * BLOCK, max(n - 1, 0) * BLOCK, 2 * BLOCK,
                  BIAS_FIRST[4] if n == 0 else BIAS_FULL[4],
                  pl.ds(r + 4 * BLOCK * n, BLOCK, stride=4), False)

    for r in range(16):
        block(q16_ref, k16_ref, v16_ref, r * HALF, 0, 0, BLOCK, BIAS_FIRST[16],
              pl.ds(r, BLOCK, stride=16), False)

    for n in range(SEQ // BLOCK):
        rows = slice(n * BLOCK, (n + 1) * BLOCK)
        for p in range(PAIRS_PER_HALF):
            o_ref[rows, p * LANES:(p + 1) * LANES] = (acc_ref[p, rows, :] / den_ref[p, rows, :]).astype(BF16)


def _dilated_attention(qkv, bias_masks):
    tokens = qkv[0].shape[0]
    batch = tokens // SEQ
    specs = []
    for dil in (1, 4, 16):
        specs += [pl.BlockSpec((SEQ // dil, dil * HALF), lambda b, hf: (b, hf))] * 3
    state = pltpu.VMEM((PAIRS_PER_HALF, SEQ, LANES), F32)
    return pl.pallas_call(
        _attn_kernel,
        out_shape=jax.ShapeDtypeStruct((tokens, ATTN_WIDTH), BF16),
        grid=(batch, 2),
        in_specs=specs + [pl.BlockSpec((N_BIAS_SETS, PAIRS_PER_HALF, 2 * BLOCK, 2 * BLOCK),
                                       lambda b, hf: (0, hf, 0, 0))],
        out_specs=pl.BlockSpec((SEQ, HALF), lambda b, hf: (b, hf)),
        scratch_shapes=[state, state, state],
        compiler_params=_cparams(("arbitrary", "arbitrary")),
        name="dilated_attention",
    )(*qkv, bias_masks)


def _retention_kernel(r_ref, cos_ref, sin_ref, dmask_ref, qdec_ref, kdec_ref, cdec_ref, gain_ref,
                      o_ref, state_ref):
    @pl.when(pl.program_id(1) == 0)
    def _():
        state_ref[...] = jnp.zeros_like(state_ref)

    lane = lax.broadcasted_iota(jnp.int32, (RET_CHUNK, LANES), 1)
    low = lane < RET_KEY_DIM
    first_half = (lane % RET_KEY_DIM) < (RET_KEY_DIM // 2)

    def rotate(t, cos, sin):
        partner = jnp.where(first_half, pltpu.roll(t, LANES - RET_KEY_DIM // 2, 1),
                            pltpu.roll(t, RET_KEY_DIM // 2, 1))
        return t * cos + partner * sin

    for c in range(RET_ROWS // RET_CHUNK):
        rows = slice(c * RET_CHUNK, (c + 1) * RET_CHUNK)
        for hp in range(RET_HEADS // 2):
            qs = slice(hp * LANES, (hp + 1) * LANES)
            ks = slice(RET_QK_WIDTH + hp * LANES, RET_QK_WIDTH + (hp + 1) * LANES)
            cos, sin = cos_ref[rows, qs], sin_ref[rows, qs]
            q_pair = rotate(r_ref[rows, qs], cos, sin) * (RET_KEY_DIM ** -0.5)
            k_pair = rotate(r_ref[rows, ks], cos, sin)
            for hh in range(2):
                head = 2 * hp + hh
                vs = slice(2 * RET_QK_WIDTH + head * LANES, 2 * RET_QK_WIDTH + (head + 1) * LANES)
                gs = slice(2 * RET_QK_WIDTH + RET_WIDTH + head * LANES,
                           2 * RET_QK_WIDTH + RET_WIDTH + (head + 1) * LANES)
                keep = low if hh == 0 else jnp.logical_not(low)
                qm = jnp.where(keep, q_pair, 0.0)
                vb = r_ref[rows, vs].astype(BF16)
                state = state_ref[head]
                inner = _dot_nt(qm.astype(BF16), k_pair.astype(BF16)) * dmask_ref[head]
                y = _dot(inner.astype(BF16), vb)
                y = y + _dot((qm * qdec_ref[head]).astype(BF16), state.astype(BF16))
                state_ref[head] = state * cdec_ref[head] + _dot_tn((k_pair * kdec_ref[head]).astype(BF16), vb)
                mu = jnp.mean(y, axis=-1, keepdims=True)
                yc = y - mu
                var = jnp.mean(yc * yc, axis=-1, keepdims=True)
                yn = yc * lax.rsqrt(var + EPS) * gain_ref[:, head * LANES:(head + 1) * LANES]
                o_ref[rows, head * LANES:(head + 1) * LANES] = (_silu(r_ref[rows, gs]) * yn).astype(BF16)


def _retention_tables():
    half = RET_KEY_DIM // 2
    pos = jnp.arange(SEQ, dtype=F32)
    inv = ROPE_BASE ** (-jnp.arange(half, dtype=F32) / half)
    ang = pos[:, None] * inv[None, :]
    cos, sin = jnp.cos(ang), jnp.sin(ang)
    cos_full = jnp.tile(jnp.concatenate([cos, cos], axis=-1), (1, RET_HEADS))
    sin_signed = jnp.tile(jnp.concatenate([-sin, sin], axis=-1), (1, RET_HEADS))
    log_g = jnp.log(1.0 - 2.0 ** (-5.0 - jnp.arange(RET_HEADS, dtype=F32)))
    idx = jnp.arange(RET_CHUNK, dtype=F32)
    diff = idx[:, None] - idx[None, :]
    dmask = jnp.where(diff >= 0, jnp.exp(jnp.maximum(diff, 0.0)[None] * log_g[:, None, None]), 0.0)
    q_decay = jnp.exp((idx + 1.0)[None, :] * log_g[:, None])[..., None]
    k_decay = jnp.exp((RET_CHUNK - 1.0 - idx)[None, :] * log_g[:, None])[..., None]
    chunk_decay = jnp.exp(RET_CHUNK * log_g)[:, None, None]
    full = (RET_HEADS, RET_CHUNK, LANES)
    return (cos_full, sin_signed, dmask, jnp.broadcast_to(q_decay, full),
            jnp.broadcast_to(k_decay, full), jnp.broadcast_to(chunk_decay, full))


def _retention(ret_in, ret_gain):
    tokens = ret_in.shape[0]
    batch = tokens // SEQ
    per_seq = SEQ // RET_ROWS
    cos, sin, dmask, qdec, kdec, cdec = _retention_tables()
    tab = pl.BlockSpec((RET_ROWS, RET_QK_WIDTH), lambda b, j: (j, 0))
    const3 = pl.BlockSpec((RET_HEADS, RET_CHUNK, LANES), lambda b, j: (0, 0, 0))
    return pl.pallas_call(
        _retention_kernel,
        out_shape=jax.ShapeDtypeStruct((tokens, RET_WIDTH), BF16),
        grid=(batch, per_seq),
        in_specs=[
            pl.BlockSpec((RET_ROWS, RET_IN_WIDTH), lambda b, j: (b * per_seq + j, 0)),
            tab, tab, const3, const3, const3, const3,
            pl.BlockSpec((1, RET_WIDTH), lambda b, j: (0, 0)),
        ],
        out_specs=pl.BlockSpec((RET_ROWS, RET_WIDTH), lambda b, j: (b * per_seq + j, 0)),
        scratch_shapes=[pltpu.VMEM((RET_HEADS, LANES, RET_VALUE_DIM), F32)],
        compiler_params=_cparams(("arbitrary", "arbitrary")),
        name="retention",
    )(ret_in, cos, sin, dmask, qdec, kdec, cdec, ret_gain.reshape(1, RET_WIDTH))


def _out_proj_kernel(attn_ref, ret_ref, x_ref, mod_ref, w_ref, out_ref):
    mix = _dot(attn_ref[...], w_ref[:ATTN_WIDTH, :]) + _dot(ret_ref[...], w_ref[ATTN_WIDTH:, :])
    out_ref[...] = x_ref[...] + mod_ref[2:3, :] * mix


def _out_proj(attn, ret, x, mod, w_out):
    tokens = x.shape[0]
    tm = TM_FFN
    per_seq = SEQ // tm
    row = lambda i: (i, 0)
    return pl.pallas_call(
        _out_proj_kernel,
        out_shape=jax.ShapeDtypeStruct((tokens, D_MODEL), F32),
        grid=(tokens // tm,),
        in_specs=[pl.BlockSpec((tm, ATTN_WIDTH), row),
                  pl.BlockSpec((tm, RET_WIDTH), row),
                  pl.BlockSpec((tm, D_MODEL), row),
                  pl.BlockSpec((None, 6, D_MODEL), lambda i: (i // per_seq, 0, 0)),
                  pl.BlockSpec((D_MODEL, D_MODEL), lambda i: (0, 0))],
        out_specs=pl.BlockSpec((tm, D_MODEL), row),
        compiler_params=_cparams(("arbitrary",)),
        name="out_projection",
    )(attn, ret, x, mod, w_out.astype(BF16))


def _ffn_kernel(x_ref, mod_ref, gain_ref, w1_ref, w3_ref, w2_ref, out_ref, h_ref, acc_ref):
    f = pl.program_id(1)

    @pl.when(f == 0)
    def _():
        mod = mod_ref[...]
        h_ref[...] = _modulated_norm(x_ref[...], gain_ref[...], mod[4:5], mod[3:4]).astype(BF16)
        acc_ref[...] = jnp.zeros_like(acc_ref)

    h = h_ref[...]
    z = (_silu(_dot(h, w1_ref[...])) * _dot(h, w3_ref[...])).astype(BF16)
    acc_ref[...] += _dot(z, w2_ref[...])

    @pl.when(f == pl.num_programs(1) - 1)
    def _():
        out_ref[...] = x_ref[...] + mod_ref[5:6, :] * acc_ref[...]


def _dense_ffn(x, mod, gain, w1, w3, w2):
    tokens = x.shape[0]
    d_ff = w1.shape[1]
    tm, tf = TM_FFN, TF_FFN
    per_seq = SEQ // tm
    return pl.pallas_call(
        _ffn_kernel,
        out_shape=jax.ShapeDtypeStruct((tokens, D_MODEL), F32),
        grid=(tokens // tm, d_ff // tf),
        in_specs=[
            pl.BlockSpec((tm, D_MODEL), lambda i, f: (i, 0)),
            pl.BlockSpec((None, 6, D_MODEL), lambda i, f: (i // per_seq, 0, 0)),
            pl.BlockSpec((1, D_MODEL), lambda i, f: (0, 0)),
            pl.BlockSpec((D_MODEL, tf), lambda i, f: (0, f)),
            pl.BlockSpec((D_MODEL, tf), lambda i, f: (0, f)),
            pl.BlockSpec((tf, D_MODEL), lambda i, f: (f, 0)),
        ],
        out_specs=pl.BlockSpec((tm, D_MODEL), lambda i, f: (i, 0)),
        scratch_shapes=[pltpu.VMEM((tm, D_MODEL), BF16), pltpu.VMEM((tm, D_MODEL), F32)],
        compiler_params=_cparams(("arbitrary", "arbitrary")),
        name="dense_swiglu",
    )(x, mod, gain.reshape(1, D_MODEL), w1.astype(BF16), w3.astype(BF16), w2.astype(BF16))


def _router_kernel(x_ref, mod_ref, gain_ref, wr_ref, h_ref, gates_ref, pos_ref, post_ref, start_ref,
                   cnt_ref, carry_ref, *, tiles_per_seq):
    i = pl.program_id(0)

    @pl.when(i % tiles_per_seq == 0)
    def _():
        carry_ref[...] = jnp.zeros_like(carry_ref)

    mod = mod_ref[...]
    h = _modulated_norm(x_ref[...], gain_ref[...], mod[4:5], mod[3:4]).astype(BF16)
    h_ref[...] = h
    tm = h.shape[0]
    lane = lax.broadcasted_iota(jnp.int32, (tm, LANES), 1).astype(F32)
    logits = jnp.where(lane < N_EXPERTS, _dot(h, wr_ref[...]), -jnp.inf)
    m1 = jnp.max(logits, axis=-1, keepdims=True)
    i1 = jnp.min(jnp.where(logits == m1, lane, float(LANES)), axis=-1, keepdims=True)
    rest = jnp.where(lane == i1, -jnp.inf, logits)
    m2 = jnp.max(rest, axis=-1, keepdims=True)
    i2 = jnp.min(jnp.where(rest == m2, lane, float(LANES)), axis=-1, keepdims=True)
    e2 = jnp.exp(m2 - m1)
    g1 = 1.0 / (1.0 + e2)
    g2 = e2 / (1.0 + e2)
    gates_ref[...] = jnp.where(lane == i1, g1, 0.0) + jnp.where(lane == i2, g2, 0.0)
    chosen = (lane == i1) | (lane == i2)
    onehot = jnp.where(chosen, 1.0, 0.0)
    r = lax.broadcasted_iota(jnp.int32, (tm, tm), 0)
    c = lax.broadcasted_iota(jnp.int32, (tm, tm), 1)
    tril = jnp.where(c <= r, 1.0, 0.0).astype(BF16)
    incl = _dot(tril, onehot.astype(BF16))
    carry = carry_ref[0:1, :]
    start_ref[...] = carry_ref[...]
    pos = jnp.where(chosen, incl - 1.0 + carry, -1.0)
    pos_ref[...] = pos
    post_ref[...] = pos.T[:N_EXPERTS, :]
    total = carry + jnp.ceil(incl[tm - 1:tm, :] * (1.0 / ROW_ALIGN)) * ROW_ALIGN
    carry_ref[...] = jnp.broadcast_to(total, carry_ref.shape)
    cnt_ref[...] = jnp.broadcast_to(total, cnt_ref.shape)


def _router(x, mod, gain, w_router):
    tokens = x.shape[0]
    batch = tokens // SEQ
    tm = TM_PROJ
    per_seq = SEQ // tm
    wr = jnp.zeros((D_MODEL, LANES), BF16).at[:, :N_EXPERTS].set(w_router.astype(BF16))
    row = lambda i: (i, 0)
    return pl.pallas_call(
        functools.partial(_router_kernel, tiles_per_seq=per_seq),
        out_shape=(
            jax.ShapeDtypeStruct((tokens, D_MODEL), BF16),
            jax.ShapeDtypeStruct((tokens, LANES), F32),
            jax.ShapeDtypeStruct((tokens, LANES), F32),
            jax.ShapeDtypeStruct((N_EXPERTS, tokens), F32),
            jax.ShapeDtypeStruct((tokens // tm, 8, LANES), F32),
            jax.ShapeDtypeStruct((batch, 8, LANES), F32),
        ),
        grid=(tokens // tm,),
        in_specs=[
            pl.BlockSpec((tm, D_MODEL), row),
            pl.BlockSpec((None, 6, D_MODEL), lambda i: (i // per_seq, 0, 0)),
            pl.BlockSpec((1, D_MODEL), lambda i: (0, 0)),
            pl.BlockSpec((D_MODEL, LANES), lambda i: (0, 0)),
        ],
        out_specs=(
            pl.BlockSpec((tm, D_MODEL), row),
            pl.BlockSpec((tm, LANES), row),
            pl.BlockSpec((tm, LANES), row),
            pl.BlockSpec((N_EXPERTS, tm), lambda i: (0, i)),
            pl.BlockSpec((None, 8, LANES), lambda i: (i, 0, 0)),
            pl.BlockSpec((None, 8, LANES), lambda i: (i // per_seq, 0, 0)),
        ),
        scratch_shapes=[pltpu.VMEM((8, LANES), F32)],
        compiler_params=_cparams(("arbitrary",)),
        name="expert_router",
    )(x, mod, gain.reshape(1, D_MODEL), wr)


TOK_BLOCKS = SEQ // TR_MOE
assert TOK_BLOCKS * (ROW_ALIGN - 1) <= TR_MOE


def _moe_kernel(tbl_ref, h_ref, post_ref, pos_ref, gates_ref, x_ref, mod_ref, w1_ref, w3_ref, w2_ref,
                out_ref, hs_ref, acc_ref):
    b, e, f = pl.program_id(0), pl.program_id(1), pl.program_id(2)
    last_f = pl.num_programs(2) - 1
    tr = TR_MOE
    base = b * (TOK_BLOCKS + 1) * N_EXPERTS + e
    starts = [pl.multiple_of(tbl_ref[base + tb * N_EXPERTS], ROW_ALIGN) for tb in range(TOK_BLOCKS)]
    n_tiles = lax.shift_right_logical(tbl_ref[base + TOK_BLOCKS * N_EXPERTS] + (tr - 1), tr.bit_length() - 1)

    @pl.when((e == 0) & (f == 0))
    def _():
        out_ref[...] = jnp.zeros_like(out_ref)

    @pl.when(f == 0)
    def _():
        slot = lax.broadcasted_iota(jnp.int32, (tr, tr), 0).astype(F32)
        for tb in range(TOK_BLOCKS):
            ts = slice(tb * tr, (tb + 1) * tr)
            local = post_ref[pl.ds(e, 1), ts] - starts[tb].astype(F32)
            sel = jnp.where(local == slot, 1.0, 0.0).astype(BF16)
            hs_ref[pl.ds(starts[tb], tr), :] = _dot(sel, h_ref[ts, :]).astype(BF16)

        def clear(r, carry):
            acc_ref[pl.ds(pl.multiple_of(r * tr, tr), tr), :] = jnp.zeros((tr, D_MODEL), F32)
            return carry
        lax.fori_loop(0, n_tiles + 1, clear, 0)

    def swiglu(r, carry):
        r0 = pl.multiple_of(r * tr, tr)
        hr = hs_ref[pl.ds(r0, tr), :]
        z = (_silu(_dot(hr, w1_ref[...])) * _dot(hr, w3_ref[...])).astype(BF16)
        acc_ref[pl.ds(r0, tr), :] += _dot(z, w2_ref[...])
        return carry
    lax.fori_loop(0, n_tiles, swiglu, 0)

    @pl.when(f == last_f)
    def _():
        lane = lax.broadcasted_iota(jnp.int32, (tr, LANES), 1)
        slot = lax.broadcasted_iota(jnp.int32, (tr, tr), 1).astype(F32)
        for tb in range(TOK_BLOCKS):
            ts = slice(tb * tr, (tb + 1) * tr)
            rows = acc_ref[pl.ds(starts[tb], tr), :].astype(BF16)
            pos_col = jnp.sum(jnp.where(lane == e, pos_ref[ts, :], 0.0), axis=-1, keepdims=True)
            gate_col = jnp.sum(jnp.where(lane == e, gates_ref[ts, :], 0.0), axis=-1, keepdims=True)
            sel = jnp.where(pos_col - starts[tb].astype(F32) == slot, 1.0, 0.0).astype(BF16)
            out_ref[ts, :] += gate_col * _dot(sel, rows)

    @pl.when((e == N_EXPERTS - 1) & (f == last_f))
    def _():
        out_ref[...] = x_ref[...] + mod_ref[5:6, :] * out_ref[...]


def _moe(x, mod, h, post, pos, gates, table, w1, w3, w2):
    tokens = h.shape[0]
    batch = tokens // SEQ
    d_ff = w1.shape[2]
    tf = TF_MOE
    n_f = d_ff // tf
    once = pl.Buffered(1)
    max_rows = SEQ + 2 * TR_MOE
    grid_spec = pltpu.PrefetchScalarGridSpec(
        num_scalar_prefetch=1,
        grid=(batch, N_EXPERTS, n_f),
        in_specs=[
            pl.BlockSpec((SEQ, D_MODEL), lambda b, e, f, t: (b, 0), pipeline_mode=once),
            pl.BlockSpec((N_EXPERTS, SEQ), lambda b, e, f, t: (0, b), pipeline_mode=once),
            pl.BlockSpec((SEQ, LANES), lambda b, e, f, t: (b, 0), pipeline_mode=once),
            pl.BlockSpec((SEQ, LANES), lambda b, e, f, t: (b, 0), pipeline_mode=once),
            pl.BlockSpec((SEQ, D_MODEL), lambda b, e, f, t: (b, 0), pipeline_mode=once),
            pl.BlockSpec((None, 6, D_MODEL), lambda b, e, f, t: (b, 0, 0)),
            pl.BlockSpec((None, D_MODEL, tf), lambda b, e, f, t: (e, 0, f)),
            pl.BlockSpec((None, D_MODEL, tf), lambda b, e, f, t: (e, 0, f)),
            pl.BlockSpec((None, tf, D_MODEL), lambda b, e, f, t: (e, f, 0)),
        ],
        out_specs=pl.BlockSpec((SEQ, D_MODEL), lambda b, e, f, t: (b, 0), pipeline_mode=once),
        scratch_shapes=[pltpu.VMEM((max_rows, D_MODEL), BF16), pltpu.VMEM((max_rows, D_MODEL), F32)],
    )
    return pl.pallas_call(
        _moe_kernel,
        out_shape=jax.ShapeDtypeStruct((tokens, D_MODEL), F32),
        grid_spec=grid_spec,
        compiler_params=_cparams(("arbitrary", "arbitrary", "arbitrary")),
        name="expert_swiglu",
    )(table, h, post, pos, gates, x, mod, w1.astype(BF16), w3.astype(BF16), w2.astype(BF16))


def _token_mixer(x, mod, gain, w_in, q_gain, k_gain, ret_gain, w_out, bias_masks):
    *qkv, ret_in = _in_proj(x, mod, gain, w_in, q_gain, k_gain)
    attn = _dilated_attention(qkv, bias_masks)
    ret = _retention(ret_in, ret_gain)
    return _out_proj(attn, ret, x, mod, w_out)


def _moe_ffn(x, mod, gain, w_router, w1, w3, w2):
    h, gates, pos, post, start, cnt = _router(x, mod, gain, w_router)
    batch = cnt.shape[0]
    table = jnp.concatenate([start[:, 0, :N_EXPERTS].reshape(batch, TOK_BLOCKS, N_EXPERTS),
                             cnt[:, :1, :N_EXPERTS]], axis=1).astype(jnp.int32).reshape(-1)
    return _moe(x, mod, h, post, pos, gates, table, w1, w3, w2)


def kernel(x, c, rel_bias_table, norm_mix, norm_ffn, w_mod, b_mod, w_in, q_gain, k_gain, ret_gain, w_out,
           ffn_w1, ffn_w3, ffn_w2, moe_router, moe_w1, moe_w3, moe_w2):
    batch, seq, d_model = x.shape
    assert (seq, d_model) == (SEQ, D_MODEL)
    depth = w_mod.shape[0]
    mods = _modulation(c, w_mod, b_mod).reshape(depth, batch, 6, D_MODEL)
    bias_masks = _bias_masks(rel_bias_table)
    xt = x.reshape(batch * seq, d_model)
    for layer in range(depth):
        mod = mods[layer]
        xt = _token_mixer(xt, mod, norm_mix[layer], w_in[layer], q_gain[layer], k_gain[layer],
                          ret_gain[layer], w_out[layer], bias_masks)
        i = layer // 2
        if layer % 2 == 0:
            xt = _dense_ffn(xt, mod, norm_ffn[layer], ffn_w1[i], ffn_w3[i], ffn_w2[i])
        else:
            xt = _moe_ffn(xt, mod, norm_ffn[layer], moe_router[i], moe_w1[i], moe_w3[i], moe_w2[i])
    return xt.reshape(batch, seq, d_model)
```

```python
import functools
import math

import jax
import jax.numpy as jnp
import numpy as np
from jax import lax
from jax.experimental import pallas as pl
from jax.experimental.pallas import tpu as pltpu

D_MODEL = 1024
SEQ = 2048
ATTN_HEADS = 8
ATTN_HEAD_DIM = 64
ATTN_WIDTH = ATTN_HEADS * ATTN_HEAD_DIM
DILATED_PATTERNS = ((128, 1), (512, 4), (2048, 16))
BLOCK = 128
NUM_BUCKETS = 32
MAX_DISTANCE = 2048
RET_HEADS = 4
RET_KEY_DIM = 64
RET_VALUE_DIM = 128
RET_WIDTH = RET_HEADS * RET_VALUE_DIM
RET_QK_WIDTH = RET_HEADS * RET_KEY_DIM
RET_CHUNK = 128
ROPE_BASE = 10000.0
IN_WIDTH = 3 * ATTN_WIDTH + 2 * RET_QK_WIDTH + 2 * RET_WIDTH
RET_IN_WIDTH = IN_WIDTH - 3 * ATTN_WIDTH
N_EXPERTS = 8
EPS = 1e-6
NEG_INF = -1e30

LANES = 128
VMEM_LIMIT = 56 * 1024 * 1024

BF16 = jnp.bfloat16
F32 = jnp.float32

TM_PROJ = 256
TM_FFN = 512
TF_FFN = 1408
TR_MOE = 256
TF_MOE = 896
assert TM_PROJ == TR_MOE
RET_ROWS = 512
ROW_ALIGN = 16


def _cparams(sem):
    return pltpu.CompilerParams(dimension_semantics=sem, vmem_limit_bytes=VMEM_LIMIT)


def _dot(a, b):
    return jnp.dot(a, b, preferred_element_type=F32)


def _dot_nt(a, b):
    return lax.dot_general(a, b, (((1,), (1,)), ((), ())), preferred_element_type=F32)


def _dot_tn(a, b):
    return lax.dot_general(a, b, (((0,), (0,)), ((), ())), preferred_element_type=F32)


def _split_bf16(v):
    hi = v.astype(BF16)
    lo = (v - hi.astype(F32)).astype(BF16)
    return hi, lo


def _silu(v):
    return v * (1.0 / (1.0 + jnp.exp(-v)))


def _modulated_norm(x, gain, scale, shift):
    ms = jnp.mean(x * x, axis=-1, keepdims=True)
    y = x * lax.rsqrt(ms + EPS) * gain
    return y * (1.0 + scale) + shift


def _mod_kernel(c_ref, w_ref, b_ref, o_ref):
    ca = _silu(c_ref[...]).astype(BF16)
    o_ref[...] = _dot(ca, w_ref[...].astype(BF16)) + b_ref[...]


def _modulation(c, w_mod, b_mod):
    depth, _, width = w_mod.shape
    batch = c.shape[0]
    tn = 1536
    return pl.pallas_call(
        _mod_kernel,
        out_shape=jax.ShapeDtypeStruct((depth, batch, width), F32),
        grid=(depth, width // tn),
        in_specs=[
            pl.BlockSpec((batch, D_MODEL), lambda l, n: (0, 0)),
            pl.BlockSpec((None, D_MODEL, tn), lambda l, n: (l, 0, n)),
            pl.BlockSpec((None, 1, tn), lambda l, n: (l, 0, n)),
        ],
        out_specs=pl.BlockSpec((None, batch, tn), lambda l, n: (l, 0, n)),
        compiler_params=_cparams(("arbitrary", "arbitrary")),
        name="adaln_modulation",
    )(c, w_mod, b_mod.reshape(depth, 1, width))


def _bias_kernel(table_ref, bucket_ref, o_ref):
    h = pl.program_id(1)
    bucket = bucket_ref[...]
    acc = jnp.full(bucket.shape, NEG_INF, F32)
    for b in range(NUM_BUCKETS):
        acc = jnp.where(bucket == b, table_ref[b, h], acc)
    o_ref[...] = acc


BIAS_FULL = {1: 0, 4: 2}
BIAS_FIRST = {1: 1, 4: 3, 16: 4}
N_BIAS_SETS = 5


def _bias_masks(rel_bias_table):
    i = jnp.arange(BLOCK)[:, None]
    j = jnp.arange(2 * BLOCK)[None, :]
    max_exact = NUM_BUCKETS // 2

    def bucket_of(rel, dilation, w_sub, exists):
        n = jnp.maximum(rel * dilation, 0)
        nf = jnp.maximum(n.astype(F32), float(max_exact))
        large = max_exact + (jnp.log(nf / max_exact) / math.log(MAX_DISTANCE / max_exact)
                             * (NUM_BUCKETS - max_exact)).astype(jnp.int32)
        large = jnp.minimum(large, NUM_BUCKETS - 1)
        bucket = jnp.where(n < max_exact, n, large)
        allowed = (rel >= 0) & (rel <= w_sub) & exists
        return jnp.where(allowed, bucket, -1)

    sets = [None] * N_BIAS_SETS
    for window, dilation in DILATED_PATTERNS:
        w_sub = window // dilation
        if dilation in BIAS_FULL:
            sets[BIAS_FULL[dilation]] = bucket_of(i - j + BLOCK, dilation, w_sub, j >= 0)
        sets[BIAS_FIRST[dilation]] = bucket_of(i - j, dilation, w_sub, j < BLOCK)
    buckets = jnp.stack(sets).astype(jnp.int32)
    return pl.pallas_call(
        _bias_kernel,
        out_shape=jax.ShapeDtypeStruct((N_BIAS_SETS, ATTN_HEADS // 2, 2 * BLOCK, 2 * BLOCK), F32),
        grid=(N_BIAS_SETS, ATTN_HEADS),
        in_specs=[
            pl.BlockSpec(memory_space=pltpu.SMEM),
            pl.BlockSpec((None, BLOCK, 2 * BLOCK), lambda p, h: (p, 0, 0)),
        ],
        out_specs=pl.BlockSpec((None, None, BLOCK, 2 * BLOCK), lambda p, h: (p, h // 2, h % 2, 0)),
        compiler_params=_cparams(("arbitrary", "arbitrary")),
        name="relative_bias_masks",
    )(rel_bias_table, buckets)


HALF = ATTN_WIDTH // 2


def _in_proj_kernel(x_ref, mod_ref, gain_ref, w_ref, qg_ref, kg_ref, grp_ref,
                    q1_ref, k1_ref, v1_ref, q4_ref, k4_ref, v4_ref, q16_ref, k16_ref, v16_ref, r_ref,
                    perm_ref):
    mod = mod_ref[...]
    h = _modulated_norm(x_ref[...], gain_ref[...], mod[1:2], mod[0:1]).astype(BF16)
    proj = _dot(h, w_ref[...])
    grp = grp_ref[...]
    tm = proj.shape[0]

    def head_norm(t, gain):
        hi, lo = _split_bf16(t * t)
        ss = _dot(hi, grp) + _dot(lo, grp)
        return t * lax.rsqrt(ss * (1.0 / ATTN_HEAD_DIM) + EPS) * gain

    def emit(t, o1_ref, o4_ref, o16_ref):
        o1_ref[...] = t.astype(BF16)
        for j in range(ATTN_WIDTH // LANES):
            perm_ref[j] = t[:, j * LANES:(j + 1) * LANES]
        for dil, o_ref in ((4, o4_ref), (16, o16_ref)):
            for hf in range(2):
                for r in range(dil):
                    for jj in range(HALF // LANES):
                        c0 = (hf * dil + r) * HALF + jj * LANES
                        o_ref[:, c0:c0 + LANES] = perm_ref[hf * (HALF // LANES) + jj,
                                                           pl.ds(r, tm // dil, stride=dil), :].astype(BF16)

    emit(head_norm(proj[:, :ATTN_WIDTH], qg_ref[...]) * (ATTN_HEAD_DIM ** -0.5), q1_ref, q4_ref, q16_ref)
    emit(head_norm(proj[:, ATTN_WIDTH:2 * ATTN_WIDTH], kg_ref[...]), k1_ref, k4_ref, k16_ref)
    emit(proj[:, 2 * ATTN_WIDTH:3 * ATTN_WIDTH], v1_ref, v4_ref, v16_ref)
    r_ref[...] = proj[:, 3 * ATTN_WIDTH:]


def _in_proj(x, mod, gain, w_in, q_gain, k_gain):
    tokens = x.shape[0]
    tm = TM_PROJ
    per_seq = SEQ // tm
    grp = np.kron(np.eye(ATTN_HEADS), np.ones((ATTN_HEAD_DIM, ATTN_HEAD_DIM))).astype(np.float32)
    row = lambda i: (i, 0)
    const = lambda i: (0, 0)
    layouts = []
    for dil in (1, 4, 16):
        shape = jax.ShapeDtypeStruct((tokens // dil, dil * ATTN_WIDTH), BF16)
        spec = pl.BlockSpec((tm // dil, dil * ATTN_WIDTH), row)
        layouts.append(((shape,) * 3, (spec,) * 3))
    out_shape = sum((s for s, _ in layouts), ()) + (jax.ShapeDtypeStruct((tokens, RET_IN_WIDTH), F32),)
    out_specs = sum((s for _, s in layouts), ()) + (pl.BlockSpec((tm, RET_IN_WIDTH), row),)
    return pl.pallas_call(
        _in_proj_kernel,
        out_shape=out_shape,
        grid=(tokens // tm,),
        in_specs=[
            pl.BlockSpec((tm, D_MODEL), row),
            pl.BlockSpec((None, 6, D_MODEL), lambda i: (i // per_seq, 0, 0)),
            pl.BlockSpec((1, D_MODEL), const),
            pl.BlockSpec((D_MODEL, IN_WIDTH), const),
            pl.BlockSpec((1, ATTN_WIDTH), const),
            pl.BlockSpec((1, ATTN_WIDTH), const),
            pl.BlockSpec((ATTN_WIDTH, ATTN_WIDTH), const),
        ],
        out_specs=out_specs,
        scratch_shapes=[pltpu.VMEM((ATTN_WIDTH // LANES, tm, LANES), F32)],
        compiler_params=_cparams(("arbitrary",)),
        name="in_projection",
    )(x, mod, gain.reshape(1, D_MODEL), w_in.astype(BF16),
      jnp.tile(q_gain, ATTN_HEADS).reshape(1, ATTN_WIDTH),
      jnp.tile(k_gain, ATTN_HEADS).reshape(1, ATTN_WIDTH),
      jnp.asarray(grp, BF16))


PAIRS_PER_HALF = ATTN_HEADS // 4
GROUP = 4


def _pair_scores(qp, kp, vp, bias2, masks, low):
    q2 = jnp.concatenate([qp * masks[0], qp * masks[1]], axis=0)
    s = _dot_nt(q2, kp) + bias2
    m = jnp.max(s, axis=-1, keepdims=True)
    p = jnp.exp(s - m)
    den = jnp.sum(p, axis=-1, keepdims=True)
    pv = _dot(p.astype(BF16), vp)
    pick = lambda t: jnp.where(low, t[:BLOCK], t[BLOCK:])
    return pick(pv), pick(m), pick(den)


def _attn_kernel(q1_ref, k1_ref, v1_ref, q4_ref, k4_ref, v4_ref, q16_ref, k16_ref, v16_ref, bm_ref,
                 o_ref, acc_ref, max_ref, den_ref):
    lane = lax.broadcasted_iota(jnp.int32, (BLOCK, LANES), 1)
    low = lane < ATTN_HEAD_DIM
    masks = (jnp.where(low, 1.0, 0.0).astype(BF16), jnp.where(low, 0.0, 1.0).astype(BF16))

    def block(q_ref, k_ref, v_ref, c0, q0, w0, width, bias_set, rows, first):
        for p in range(PAIRS_PER_HALF):
            cs = slice(c0 + p * LANES, c0 + (p + 1) * LANES)
            acc, m, den = _pair_scores(q_ref[pl.ds(q0, BLOCK), cs], k_ref[pl.ds(w0, width), cs],
                                       v_ref[pl.ds(w0, width), cs], bm_ref[bias_set, p, :, 0:width],
                                       masks, low)
            if not first:
                m_old = max_ref[p, rows, :]
                m_new = jnp.maximum(m_old, m)
                a, b = jnp.exp(m_old - m_new), jnp.exp(m - m_new)
                den = den_ref[p, rows, :] * a + den * b
                acc = acc_ref[p, rows, :] * a + acc * b
                m = m_new
            max_ref[p, rows, :] = m
            den_ref[p, rows, :] = den
            acc_ref[p, rows, :] = acc

    def d1_group(g, carry):
        for u in range(GROUP):
            n = g * GROUP + u
            q0 = pl.multiple_of(n * BLOCK, BLOCK)
            w0 = pl.multiple_of(jnp.maximum(n - 1, 0) * BLOCK, BLOCK)
            bias_set = jnp.where(n == 0, BIAS_FIRST[1], BIAS_FULL[1])
            block(q1_ref, k1_ref, v1_ref, 0, q0, w0, 2 * BLOCK, bias_set, pl.ds(q0, BLOCK), True)
        return carry
    lax.fori_loop(0, SEQ // BLOCK // GROUP, d1_group, 0)

    for r in range(4):
        for n in range(SEQ // 4 // BLOCK):
            block(q4_ref, k4_ref, v4_ref, r * HALF, n * BLOCK, max(n - 1, 0) * BLOCK, 2 * BLOCK,
                  BIAS_FIRST[4] if n == 0 else BIAS_FULL[4],
                  pl.ds(r + 4 * BLOCK * n, BLOCK, stride=4), False)

    for r in range(16):
        block(q16_ref, k16_ref, v16_ref, r * HALF, 0, 0, BLOCK, BIAS_FIRST[16],
              pl.ds(r, BLOCK, stride=16), False)

    for n in range(SEQ // BLOCK):
        rows = slice(n * BLOCK, (n + 1) * BLOCK)
        for p in range(PAIRS_PER_HALF):
            o_ref[rows, p * LANES:(p + 1) * LANES] = (acc_ref[p, rows, :] / den_ref[p, rows, :]).astype(BF16)


def _dilated_attention(qkv, bias_masks):
    tokens = qkv[0].shape[0]
    batch = tokens // SEQ
    specs = []
    for dil in (1, 4, 16):
        specs += [pl.BlockSpec((SEQ // dil, dil * HALF), lambda b, hf: (b, hf))] * 3
    state = pltpu.VMEM((PAIRS_PER_HALF, SEQ, LANES), F32)
    return pl.pallas_call(
        _attn_kernel,
        out_shape=jax.ShapeDtypeStruct((tokens, ATTN_WIDTH), BF16),
        grid=(batch, 2),
        in_specs=specs + [pl.BlockSpec((N_BIAS_SETS, PAIRS_PER_HALF, 2 * BLOCK, 2 * BLOCK),
                                       lambda b, hf: (0, hf, 0, 0))],
        out_specs=pl.BlockSpec((SEQ, HALF), lambda b, hf: (b, hf)),
        scratch_shapes=[state, state, state],
        compiler_params=_cparams(("arbitrary", "arbitrary")),
        name="dilated_attention",
    )(*qkv, bias_masks)


def _retention_kernel(r_ref, cos_ref, sin_ref, dmask_ref, qdec_ref, kdec_ref, cdec_ref, gain_ref,
                      o_ref, state_ref):
    @pl.when(pl.program_id(1) == 0)
    def _():
        state_ref[...] = jnp.zeros_like(state_ref)

    lane = lax.broadcasted_iota(jnp.int32, (RET_CHUNK, LANES), 1)
    low = lane < RET_KEY_DIM
    first_half = (lane % RET_KEY_DIM) < (RET_KEY_DIM // 2)

    def rotate(t, cos, sin):
        partner = jnp.where(first_half, pltpu.roll(t, LANES - RET_KEY_DIM // 2, 1),
                            pltpu.roll(t, RET_KEY_DIM // 2, 1))
        return t * cos + partner * sin

    for c in range(RET_ROWS // RET_CHUNK):
        rows = slice(c * RET_CHUNK, (c + 1) * RET_CHUNK)
        for hp in range(RET_HEADS // 2):
            qs = slice(hp * LANES, (hp + 1) * LANES)
            ks = slice(RET_QK_WIDTH + hp * LANES, RET_QK_WIDTH + (hp + 1) * LANES)
            cos, sin = cos_ref[rows, qs], sin_ref[rows, qs]
            q_pair = rotate(r_ref[rows, qs], cos, sin) * (RET_KEY_DIM ** -0.5)
            k_pair = rotate(r_ref[rows, ks], cos, sin)
            for hh in range(2):
                head = 2 * hp + hh
                vs = slice(2 * RET_QK_WIDTH + head * LANES, 2 * RET_QK_WIDTH + (head + 1) * LANES)
                gs = slice(2 * RET_QK_WIDTH + RET_WIDTH + head * LANES,
                           2 * RET_QK_WIDTH + RET_WIDTH + (head + 1) * LANES)
                keep = low if hh == 0 else jnp.logical_not(low)
                qm = jnp.where(keep, q_pair, 0.0)
                vb = r_ref[rows, vs].astype(BF16)
                state = state_ref[head]
                inner = _dot_nt(qm.astype(BF16), k_pair.astype(BF16)) * dmask_ref[head]
                y = _dot(inner.astype(BF16), vb)
                y = y + _dot((qm * qdec_ref[head]).astype(BF16), state.astype(BF16))
                state_ref[head] = state * cdec_ref[head] + _dot_tn((k_pair * kdec_ref[head]).astype(BF16), vb)
                mu = jnp.mean(y, axis=-1, keepdims=True)
                yc = y - mu
                var = jnp.mean(yc * yc, axis=-1, keepdims=True)
                yn = yc * lax.rsqrt(var + EPS) * gain_ref[:, head * LANES:(head + 1) * LANES]
                o_ref[rows, head * LANES:(head + 1) * LANES] = (_silu(r_ref[rows, gs]) * yn).astype(BF16)


def _retention_tables():
    half = RET_KEY_DIM // 2
    pos = jnp.arange(SEQ, dtype=F32)
    inv = ROPE_BASE ** (-jnp.arange(half, dtype=F32) / half)
    ang = pos[:, None] * inv[None, :]
    cos, sin = jnp.cos(ang), jnp.sin(ang)
    cos_full = jnp.tile(jnp.concatenate([cos, cos], axis=-1), (1, RET_HEADS))
    sin_signed = jnp.tile(jnp.concatenate([-sin, sin], axis=-1), (1, RET_HEADS))
    log_g = jnp.log(1.0 - 2.0 ** (-5.0 - jnp.arange(RET_HEADS, dtype=F32)))
    idx = jnp.arange(RET_CHUNK, dtype=F32)
    diff = idx[:, None] - idx[None, :]
    dmask = jnp.where(diff >= 0, jnp.exp(jnp.maximum(diff, 0.0)[None] * log_g[:, None, None]), 0.0)
    q_decay = jnp.exp((idx + 1.0)[None, :] * log_g[:, None])[..., None]
    k_decay = jnp.exp((RET_CHUNK - 1.0 - idx)[None, :] * log_g[:, None])[..., None]
    chunk_decay = jnp.exp(RET_CHUNK * log_g)[:, None, None]
    full = (RET_HEADS, RET_CHUNK, LANES)
    return (cos_full, sin_signed, dmask, jnp.broadcast_to(q_decay, full),
            jnp.broadcast_to(k_decay, full), jnp.broadcast_to(chunk_decay, full))


def _retention(ret_in, ret_gain):
    tokens = ret_in.shape[0]
    batch = tokens // SEQ
    per_seq = SEQ // RET_ROWS
    cos, sin, dmask, qdec, kdec, cdec = _retention_tables()
    tab = pl.BlockSpec((RET_ROWS, RET_QK_WIDTH), lambda b, j: (j, 0))
    const3 = pl.BlockSpec((RET_HEADS, RET_CHUNK, LANES), lambda b, j: (0, 0, 0))
    return pl.pallas_call(
        _retention_kernel,
        out_shape=jax.ShapeDtypeStruct((tokens, RET_WIDTH), BF16),
        grid=(batch, per_seq),
        in_specs=[
            pl.BlockSpec((RET_ROWS, RET_IN_WIDTH), lambda b, j: (b * per_seq + j, 0)),
            tab, tab, const3, const3, const3, const3,
            pl.BlockSpec((1, RET_WIDTH), lambda b, j: (0, 0)),
        ],
        out_specs=pl.BlockSpec((RET_ROWS, RET_WIDTH), lambda b, j: (b * per_seq + j, 0)),
        scratch_shapes=[pltpu.VMEM((RET_HEADS, LANES, RET_VALUE_DIM), F32)],
        compiler_params=_cparams(("arbitrary", "arbitrary")),
        name="retention",
    )(ret_in, cos, sin, dmask, qdec, kdec, cdec, ret_gain.reshape(1, RET_WIDTH))


def _out_proj_kernel(attn_ref, ret_ref, x_ref, mod_ref, w_ref, out_ref):
    mix = _dot(attn_ref[...], w_ref[:ATTN_WIDTH, :]) + _dot(ret_ref[...], w_ref[ATTN_WIDTH:, :])
    out_ref[...] = x_ref[...] + mod_ref[2:3, :] * mix


def _out_proj(attn, ret, x, mod, w_out):
    tokens = x.shape[0]
    tm = TM_FFN
    per_seq = SEQ // tm
    row = lambda i: (i, 0)
    return pl.pallas_call(
        _out_proj_kernel,
        out_shape=jax.ShapeDtypeStruct((tokens, D_MODEL), F32),
        grid=(tokens // tm,),
        in_specs=[pl.BlockSpec((tm, ATTN_WIDTH), row),
                  pl.BlockSpec((tm, RET_WIDTH), row),
                  pl.BlockSpec((tm, D_MODEL), row),
                  pl.BlockSpec((None, 6, D_MODEL), lambda i: (i // per_seq, 0, 0)),
                  pl.BlockSpec((D_MODEL, D_MODEL), lambda i: (0, 0))],
        out_specs=pl.BlockSpec((tm, D_MODEL), row),
        compiler_params=_cparams(("arbitrary",)),
        name="out_projection",
    )(attn, ret, x, mod, w_out.astype(BF16))


def _ffn_kernel(x_ref, mod_ref, gain_ref, w1_ref, w3_ref, w2_ref, out_ref, h_ref, acc_ref):
    f = pl.program_id(1)

    @pl.when(f == 0)
    def _():
        mod = mod_ref[...]
        h_ref[...] = _modulated_norm(x_ref[...], gain_ref[...], mod[4:5], mod[3:4]).astype(BF16)
        acc_ref[...] = jnp.zeros_like(acc_ref)

    h = h_ref[...]
    z = (_silu(_dot(h, w1_ref[...])) * _dot(h, w3_ref[...])).astype(BF16)
    acc_ref[...] += _dot(z, w2_ref[...])

    @pl.when(f == pl.num_programs(1) - 1)
    def _():
        out_ref[...] = x_ref[...] + mod_ref[5:6, :] * acc_ref[...]


def _dense_ffn(x, mod, gain, w1, w3, w2):
    tokens = x.shape[0]
    d_ff = w1.shape[1]
    tm, tf = TM_FFN, TF_FFN
    per_seq = SEQ // tm
    return pl.pallas_call(
        _ffn_kernel,
        out_shape=jax.ShapeDtypeStruct((tokens, D_MODEL), F32),
        grid=(tokens // tm, d_ff // tf),
        in_specs=[
            pl.BlockSpec((tm, D_MODEL), lambda i, f: (i, 0)),
            pl.BlockSpec((None, 6, D_MODEL), lambda i, f: (i // per_seq, 0, 0)),
            pl.BlockSpec((1, D_MODEL), lambda i, f: (0, 0)),
            pl.BlockSpec((D_MODEL, tf), lambda i, f: (0, f)),
            pl.BlockSpec((D_MODEL, tf), lambda i, f: (0, f)),
            pl.BlockSpec((tf, D_MODEL), lambda i, f: (f, 0)),
        ],
        out_specs=pl.BlockSpec((tm, D_MODEL), lambda i, f: (i, 0)),
        scratch_shapes=[pltpu.VMEM((tm, D_MODEL), BF16), pltpu.VMEM((tm, D_MODEL), F32)],
        compiler_params=_cparams(("arbitrary", "arbitrary")),
        name="dense_swiglu",
    )(x, mod, gain.reshape(1, D_MODEL), w1.astype(BF16), w3.astype(BF16), w2.astype(BF16))


def _router_kernel(x_ref, mod_ref, gain_ref, wr_ref, h_ref, gates_ref, pos_ref, post_ref, start_ref,
                   cnt_ref, carry_ref, *, tiles_per_seq):
    i = pl.program_id(0)

    @pl.when(i % tiles_per_seq == 0)
    def _():
        carry_ref[...] = jnp.zeros_like(carry_ref)

    mod = mod_ref[...]
    h = _modulated_norm(x_ref[...], gain_ref[...], mod[4:5], mod[3:4]).astype(BF16)
    h_ref[...] = h
    tm = h.shape[0]
    lane = lax.broadcasted_iota(jnp.int32, (tm, LANES), 1).astype(F32)
    logits = jnp.where(lane < N_EXPERTS, _dot(h, wr_ref[...]), -jnp.inf)
    m1 = jnp.max(logits, axis=-1, keepdims=True)
    i1 = jnp.min(jnp.where(logits == m1, lane, float(LANES)), axis=-1, keepdims=True)
    rest = jnp.where(lane == i1, -jnp.inf, logits)
    m2 = jnp.max(rest, axis=-1, keepdims=True)
    i2 = jnp.min(jnp.where(rest == m2, lane, float(LANES)), axis=-1, keepdims=True)
    e2 = jnp.exp(m2 - m1)
    g1 = 1.0 / (1.0 + e2)
    g2 = e2 / (1.0 + e2)
    gates_ref[...] = jnp.where(lane == i1, g1, 0.0) + jnp.where(lane == i2, g2, 0.0)
    chosen = (lane == i1) | (lane == i2)
    onehot = jnp.where(chosen, 1.0, 0.0)
    r = lax.broadcasted_iota(jnp.int32, (tm, tm), 0)
    c = lax.broadcasted_iota(jnp.int32, (tm, tm), 1)
    tril = jnp.where(c <= r, 1.0, 0.0).astype(BF16)
    incl = _dot(tril, onehot.astype(BF16))
    carry = carry_ref[0:1, :]
    start_ref[...] = carry_ref[...]
    pos = jnp.where(chosen, incl - 1.0 + carry, -1.0)
    pos_ref[...] = pos
    post_ref[...] = pos.T[:N_EXPERTS, :]
    total = carry + jnp.ceil(incl[tm - 1:tm, :] * (1.0 / ROW_ALIGN)) * ROW_ALIGN
    carry_ref[...] = jnp.broadcast_to(total, carry_ref.shape)
    cnt_ref[...] = jnp.broadcast_to(total, cnt_ref.shape)


def _router(x, mod, gain, w_router):
    tokens = x.shape[0]
    batch = tokens // SEQ
    tm = TM_PROJ
    per_seq = SEQ // tm
    wr = jnp.zeros((D_MODEL, LANES), BF16).at[:, :N_EXPERTS].set(w_router.astype(BF16))
    row = lambda i: (i, 0)
    return pl.pallas_call(
        functools.partial(_router_kernel, tiles_per_seq=per_seq),
        out_shape=(
            jax.ShapeDtypeStruct((tokens, D_MODEL), BF16),
            jax.ShapeDtypeStruct((tokens, LANES), F32),
            jax.ShapeDtypeStruct((tokens, LANES), F32),
            jax.ShapeDtypeStruct((N_EXPERTS, tokens), F32),
            jax.ShapeDtypeStruct((tokens // tm, 8, LANES), F32),
            jax.ShapeDtypeStruct((batch, 8, LANES), F32),
        ),
        grid=(tokens // tm,),
        in_specs=[
            pl.BlockSpec((tm, D_MODEL), row),
            pl.BlockSpec((None, 6, D_MODEL), lambda i: (i // per_seq, 0, 0)),
            pl.BlockSpec((1, D_MODEL), lambda i: (0, 0)),
            pl.BlockSpec((D_MODEL, LANES), lambda i: (0, 0)),
        ],
        out_specs=(
            pl.BlockSpec((tm, D_MODEL), row),
            pl.BlockSpec((tm, LANES), row),
            pl.BlockSpec((tm, LANES), row),
            pl.BlockSpec((N_EXPERTS, tm), lambda i: (0, i)),
            pl.BlockSpec((None, 8, LANES), lambda i: (i, 0, 0)),
            pl.BlockSpec((None, 8, LANES), lambda i: (i // per_seq, 0, 0)),
        ),
        scratch_shapes=[pltpu.VMEM((8, LANES), F32)],
        compiler_params=_cparams(("arbitrary",)),
        name="expert_router",
    )(x, mod, gain.reshape(1, D_MODEL), wr)


TOK_BLOCKS = SEQ // TR_MOE
assert TOK_BLOCKS * (ROW_ALIGN - 1) <= TR_MOE
TAIL_TILES = (64, 128, TR_MOE)


def _moe_kernel(tbl_ref, h_ref, post_ref, pos_ref, gates_ref, x_ref, mod_ref, w1_ref, w3_ref, w2_ref,
                out_ref, hs_ref, acc_ref):
    b, e, f = pl.program_id(0), pl.program_id(1), pl.program_id(2)
    last_f = pl.num_programs(2) - 1
    tr = TR_MOE
    base = b * (TOK_BLOCKS + 1) * N_EXPERTS + e
    starts = [pl.multiple_of(tbl_ref[base + tb * N_EXPERTS], ROW_ALIGN) for tb in range(TOK_BLOCKS)]
    total = tbl_ref[base + TOK_BLOCKS * N_EXPERTS]
    n_tiles = lax.shift_right_logical(total + (tr - 1), tr.bit_length() - 1)

    @pl.when((e == 0) & (f == 0))
    def _():
        out_ref[...] = jnp.zeros_like(out_ref)

    @pl.when(f == 0)
    def _():
        slot = lax.broadcasted_iota(jnp.int32, (tr, tr), 0).astype(F32)
        for tb in range(TOK_BLOCKS):
            ts = slice(tb * tr, (tb + 1) * tr)
            local = post_ref[pl.ds(e, 1), ts] - starts[tb].astype(F32)
            sel = jnp.where(local == slot, 1.0, 0.0).astype(BF16)
            hs_ref[pl.ds(starts[tb], tr), :] = _dot(sel, h_ref[ts, :]).astype(BF16)

        def clear(r, carry):
            acc_ref[pl.ds(pl.multiple_of(r * tr, tr), tr), :] = jnp.zeros((tr, D_MODEL), F32)
            return carry
        lax.fori_loop(0, n_tiles + 1, clear, 0)

    def swiglu(r0, m):
        rows = pl.ds(r0, m)
        hr = hs_ref[rows, :]
        z = (_silu(_dot(hr, w1_ref[...])) * _dot(hr, w3_ref[...])).astype(BF16)
        acc_ref[rows, :] += _dot(z, w2_ref[...])

    def full_tile(r, carry):
        swiglu(pl.multiple_of(r * tr, tr), tr)
        return carry
    n_full = lax.shift_right_logical(total, tr.bit_length() - 1)
    lax.fori_loop(0, n_full, full_tile, 0)
    rest = total - n_full * tr
    tail0 = pl.multiple_of(n_full * tr, tr)
    lo = 0
    for m in TAIL_TILES:
        @pl.when((rest > lo) & (rest <= m))
        def _():
            swiglu(tail0, m)
        lo = m

    @pl.when(f == last_f)
    def _():
        lane = lax.broadcasted_iota(jnp.int32, (tr, LANES), 1)
        slot = lax.broadcasted_iota(jnp.int32, (tr, tr), 1).astype(F32)
        for tb in range(TOK_BLOCKS):
            ts = slice(tb * tr, (tb + 1) * tr)
            rows = acc_ref[pl.ds(starts[tb], tr), :].astype(BF16)
            pos_col = jnp.sum(jnp.where(lane == e, pos_ref[ts, :], 0.0), axis=-1, keepdims=True)
            gate_col = jnp.sum(jnp.where(lane == e, gates_ref[ts, :], 0.0), axis=-1, keepdims=True)
            sel = jnp.where(pos_col - starts[tb].astype(F32) == slot, 1.0, 0.0).astype(BF16)
            out_ref[ts, :] += gate_col * _dot(sel, rows)

    @pl.when((e == N_EXPERTS - 1) & (f == last_f))
    def _():
        out_ref[...] = x_ref[...] + mod_ref[5:6, :] * out_ref[...]


def _moe(x, mod, h, post, pos, gates, table, w1, w3, w2):
    tokens = h.shape[0]
    batch = tokens // SEQ
    d_ff = w1.shape[2]
    tf = TF_MOE
    n_f = d_ff // tf
    once = pl.Buffered(1)
    max_rows = SEQ + 2 * TR_MOE
    grid_spec = pltpu.PrefetchScalarGridSpec(
        num_scalar_prefetch=1,
        grid=(batch, N_EXPERTS, n_f),
        in_specs=[
            pl.BlockSpec((SEQ, D_MODEL), lambda b, e, f, t: (b, 0), pipeline_mode=once),
            pl.BlockSpec((N_EXPERTS, SEQ), lambda b, e, f, t: (0, b), pipeline_mode=once),
            pl.BlockSpec((SEQ, LANES), lambda b, e, f, t: (b, 0), pipeline_mode=once),
            pl.BlockSpec((SEQ, LANES), lambda b, e, f, t: (b, 0), pipeline_mode=once),
            pl.BlockSpec((SEQ, D_MODEL), lambda b, e, f, t: (b, 0), pipeline_mode=once),
            pl.BlockSpec((None, 6, D_MODEL), lambda b, e, f, t: (b, 0, 0)),
            pl.BlockSpec((None, D_MODEL, tf), lambda b, e, f, t: (e, 0, f)),
            pl.BlockSpec((None, D_MODEL, tf), lambda b, e, f, t: (e, 0, f)),
            pl.BlockSpec((None, tf, D_MODEL), lambda b, e, f, t: (e, f, 0)),
        ],
        out_specs=pl.BlockSpec((SEQ, D_MODEL), lambda b, e, f, t: (b, 0), pipeline_mode=once),
        scratch_shapes=[pltpu.VMEM((max_rows, D_MODEL), BF16), pltpu.VMEM((max_rows, D_MODEL), F32)],
    )
    return pl.pallas_call(
        _moe_kernel,
        out_shape=jax.ShapeDtypeStruct((tokens, D_MODEL), F32),
        grid_spec=grid_spec,
        compiler_params=_cparams(("arbitrary", "arbitrary", "arbitrary")),
        name="expert_swiglu",
    )(table, h, post, pos, gates, x, mod, w1.astype(BF16), w3.astype(BF16), w2.astype(BF16))


def _token_mixer(x, mod, gain, w_in, q_gain, k_gain, ret_gain, w_out, bias_masks):
    *qkv, ret_in = _in_proj(x, mod, gain, w_in, q_gain, k_gain)
    attn = _dilated_attention(qkv, bias_masks)
    ret = _retention(ret_in, ret_gain)
    return _out_proj(attn, ret, x, mod, w_out)


def _moe_ffn(x, mod, gain, w_router, w1, w3, w2):
    h, gates, pos, post, start, cnt = _router(x, mod, gain, w_router)
    batch = cnt.shape[0]
    table = jnp.concatenate([start[:, 0, :N_EXPERTS].reshape(batch, TOK_BLOCKS, N_EXPERTS),
                             cnt[:, :1, :N_EXPERTS]], axis=1).astype(jnp.int32).reshape(-1)
    return _moe(x, mod, h, post, pos, gates, table, w1, w3, w2)


def kernel(x, c, rel_bias_table, norm_mix, norm_ffn, w_mod, b_mod, w_in, q_gain, k_gain, ret_gain, w_out,
           ffn_w1, ffn_w3, ffn_w2, moe_router, moe_w1, moe_w3, moe_w2):
    batch, seq, d_model = x.shape
    assert (seq, d_model) == (SEQ, D_MODEL)
    depth = w_mod.shape[0]
    mods = _modulation(c, w_mod, b_mod).reshape(depth, batch, 6, D_MODEL)
    bias_masks = _bias_masks(rel_bias_table)
    xt = x.reshape(batch * seq, d_model)
    for layer in range(depth):
        mod = mods[layer]
        xt = _token_mixer(xt, mod, norm_mix[layer], w_in[layer], q_gain[layer], k_gain[layer],
                          ret_gain[layer], w_out[layer], bias_masks)
        i = layer // 2
        if layer % 2 == 0:
            xt = _dense_ffn(xt, mod, norm_ffn[layer], ffn_w1[i], ffn_w3[i], ffn_w2[i])
        else:
            xt = _moe_ffn(xt, mod, norm_ffn[layer], moe_router[i], moe_w1[i], moe_w3[i], moe_w2[i])
    return xt.reshape(batch, seq, d_model)
```

```python
import functools
import math

import jax
import jax.numpy as jnp
import numpy as np
from jax import lax
from jax.experimental import pallas as pl
from jax.experimental.pallas import tpu as pltpu

D_MODEL = 1024
SEQ = 2048
ATTN_HEADS = 8
ATTN_HEAD_DIM = 64
ATTN_WIDTH = ATTN_HEADS * ATTN_HEAD_DIM
DILATED_PATTERNS = ((128, 1), (512, 4), (2048, 16))
BLOCK = 128
NUM_BUCKETS = 32
MAX_DISTANCE = 2048
RET_HEADS = 4
RET_KEY_DIM = 64
RET_VALUE_DIM = 128
RET_WIDTH = RET_HEADS * RET_VALUE_DIM
RET_QK_WIDTH = RET_HEADS * RET_KEY_DIM
RET_CHUNK = 128
ROPE_BASE = 10000.0
IN_WIDTH = 3 * ATTN_WIDTH + 2 * RET_QK_WIDTH + 2 * RET_WIDTH
RET_IN_WIDTH = IN_WIDTH - 3 * ATTN_WIDTH
N_EXPERTS = 8
EPS = 1e-6
NEG_INF = -1e30

LANES = 128
VMEM_LIMIT = 56 * 1024 * 1024

BF16 = jnp.bfloat16
F32 = jnp.float32

TM_PROJ = 256
TM_FFN = 512
TF_FFN = 1408
TR_MOE = 256
TF_MOE = 896
assert TM_PROJ == TR_MOE
RET_ROWS = 512
ROW_ALIGN = 16


def _cparams(sem):
    return pltpu.CompilerParams(dimension_semantics=sem, vmem_limit_bytes=VMEM_LIMIT)


def _dot(a, b):
    return jnp.dot(a, b, preferred_element_type=F32)


def _dot_nt(a, b):
    return lax.dot_general(a, b, (((1,), (1,)), ((), ())), preferred_element_type=F32)


def _dot_tn(a, b):
    return lax.dot_general(a, b, (((0,), (0,)), ((), ())), preferred_element_type=F32)


def _split_bf16(v):
    hi = v.astype(BF16)
    lo = (v - hi.astype(F32)).astype(BF16)
    return hi, lo


def _silu(v):
    return v * (1.0 / (1.0 + jnp.exp(-v)))


def _modulated_norm(x, gain, scale, shift):
    ms = jnp.mean(x * x, axis=-1, keepdims=True)
    y = x * lax.rsqrt(ms + EPS) * gain
    return y * (1.0 + scale) + shift


def _mod_kernel(c_ref, w_ref, b_ref, o_ref):
    ca = _silu(c_ref[...]).astype(BF16)
    o_ref[...] = _dot(ca, w_ref[...].astype(BF16)) + b_ref[...]


def _modulation(c, w_mod, b_mod):
    depth, _, width = w_mod.shape
    batch = c.shape[0]
    tn = 1536
    return pl.pallas_call(
        _mod_kernel,
        out_shape=jax.ShapeDtypeStruct((depth, batch, width), F32),
        grid=(depth, width // tn),
        in_specs=[
            pl.BlockSpec((batch, D_MODEL), lambda l, n: (0, 0)),
            pl.BlockSpec((None, D_MODEL, tn), lambda l, n: (l, 0, n)),
            pl.BlockSpec((None, 1, tn), lambda l, n: (l, 0, n)),
        ],
        out_specs=pl.BlockSpec((None, batch, tn), lambda l, n: (l, 0, n)),
        compiler_params=_cparams(("arbitrary", "arbitrary")),
        name="adaln_modulation",
    )(c, w_mod, b_mod.reshape(depth, 1, width))


def _bias_kernel(table_ref, bucket_ref, o_ref):
    h = pl.program_id(1)
    bucket = bucket_ref[...]
    acc = jnp.full(bucket.shape, NEG_INF, F32)
    for b in range(NUM_BUCKETS):
        acc = jnp.where(bucket == b, table_ref[b, h], acc)
    o_ref[...] = acc


BIAS_FULL = {1: 0, 4: 2}
BIAS_FIRST = {1: 1, 4: 3, 16: 4}
N_BIAS_SETS = 5


def _bias_masks(rel_bias_table):
    i = jnp.arange(BLOCK)[:, None]
    j = jnp.arange(2 * BLOCK)[None, :]
    max_exact = NUM_BUCKETS // 2

    def bucket_of(rel, dilation, w_sub, exists):
        n = jnp.maximum(rel * dilation, 0)
        nf = jnp.maximum(n.astype(F32), float(max_exact))
        large = max_exact + (jnp.log(nf / max_exact) / math.log(MAX_DISTANCE / max_exact)
                             * (NUM_BUCKETS - max_exact)).astype(jnp.int32)
        large = jnp.minimum(large, NUM_BUCKETS - 1)
        bucket = jnp.where(n < max_exact, n, large)
        allowed = (rel >= 0) & (rel <= w_sub) & exists
        return jnp.where(allowed, bucket, -1)

    sets = [None] * N_BIAS_SETS
    for window, dilation in DILATED_PATTERNS:
        w_sub = window // dilation
        if dilation in BIAS_FULL:
            sets[BIAS_FULL[dilation]] = bucket_of(i - j + BLOCK, dilation, w_sub, j >= 0)
        sets[BIAS_FIRST[dilation]] = bucket_of(i - j, dilation, w_sub, j < BLOCK)
    buckets = jnp.stack(sets).astype(jnp.int32)
    return pl.pallas_call(
        _bias_kernel,
        out_shape=jax.ShapeDtypeStruct((N_BIAS_SETS, ATTN_HEADS // 2, 2 * BLOCK, 2 * BLOCK), F32),
        grid=(N_BIAS_SETS, ATTN_HEADS),
        in_specs=[
            pl.BlockSpec(memory_space=pltpu.SMEM),
            pl.BlockSpec((None, BLOCK, 2 * BLOCK), lambda p, h: (p, 0, 0)),
        ],
        out_specs=pl.BlockSpec((None, None, BLOCK, 2 * BLOCK), lambda p, h: (p, h // 2, h % 2, 0)),
        compiler_params=_cparams(("arbitrary", "arbitrary")),
        name="relative_bias_masks",
    )(rel_bias_table, buckets)


HALF = ATTN_WIDTH // 2


def _in_proj_kernel(x_ref, mod_ref, gain_ref, w_ref, qg_ref, kg_ref, grp_ref,
                    q1_ref, k1_ref, v1_ref, q4_ref, k4_ref, v4_ref, q16_ref, k16_ref, v16_ref, r_ref,
                    perm_ref):
    mod = mod_ref[...]
    h = _modulated_norm(x_ref[...], gain_ref[...], mod[1:2], mod[0:1]).astype(BF16)
    proj = _dot(h, w_ref[...])
    grp = grp_ref[...]
    tm = proj.shape[0]

    def head_norm(t, gain):
        hi, lo = _split_bf16(t * t)
        ss = _dot(hi, grp) + _dot(lo, grp)
        return t * lax.rsqrt(ss * (1.0 / ATTN_HEAD_DIM) + EPS) * gain

    def emit(t, o1_ref, o4_ref, o16_ref):
        o1_ref[...] = t.astype(BF16)
        for j in range(ATTN_WIDTH // LANES):
            perm_ref[j] = t[:, j * LANES:(j + 1) * LANES]
        for dil, o_ref in ((4, o4_ref), (16, o16_ref)):
            for hf in range(2):
                for r in range(dil):
                    for jj in range(HALF // LANES):
                        c0 = (hf * dil + r) * HALF + jj * LANES
                        o_ref[:, c0:c0 + LANES] = perm_ref[hf * (HALF // LANES) + jj,
                                                           pl.ds(r, tm // dil, stride=dil), :].astype(BF16)

    emit(head_norm(proj[:, :ATTN_WIDTH], qg_ref[...]) * (ATTN_HEAD_DIM ** -0.5), q1_ref, q4_ref, q16_ref)
    emit(head_norm(proj[:, ATTN_WIDTH:2 * ATTN_WIDTH], kg_ref[...]), k1_ref, k4_ref, k16_ref)
    emit(proj[:, 2 * ATTN_WIDTH:3 * ATTN_WIDTH], v1_ref, v4_ref, v16_ref)
    r_ref[...] = proj[:, 3 * ATTN_WIDTH:]


def _in_proj(x, mod, gain, w_in, q_gain, k_gain):
    tokens = x.shape[0]
    tm = TM_PROJ
    per_seq = SEQ // tm
    grp = np.kron(np.eye(ATTN_HEADS), np.ones((ATTN_HEAD_DIM, ATTN_HEAD_DIM))).astype(np.float32)
    row = lambda i: (i, 0)
    const = lambda i: (0, 0)
    layouts = []
    for dil in (1, 4, 16):
        shape = jax.ShapeDtypeStruct((tokens // dil, dil * ATTN_WIDTH), BF16)
        spec = pl.BlockSpec((tm // dil, dil * ATTN_WIDTH), row)
        layouts.append(((shape,) * 3, (spec,) * 3))
    out_shape = sum((s for s, _ in layouts), ()) + (jax.ShapeDtypeStruct((tokens, RET_IN_WIDTH), F32),)
    out_specs = sum((s for _, s in layouts), ()) + (pl.BlockSpec((tm, RET_IN_WIDTH), row),)
    return pl.pallas_call(
        _in_proj_kernel,
        out_shape=out_shape,
        grid=(tokens // tm,),
        in_specs=[
            pl.BlockSpec((tm, D_MODEL), row),
            pl.BlockSpec((None, 6, D_MODEL), lambda i: (i // per_seq, 0, 0)),
            pl.BlockSpec((1, D_MODEL), const),
            pl.BlockSpec((D_MODEL, IN_WIDTH), const),
            pl.BlockSpec((1, ATTN_WIDTH), const),
            pl.BlockSpec((1, ATTN_WIDTH), const),
            pl.BlockSpec((ATTN_WIDTH, ATTN_WIDTH), const),
        ],
        out_specs=out_specs,
        scratch_shapes=[pltpu.VMEM((ATTN_WIDTH // LANES, tm, LANES), F32)],
        compiler_params=_cparams(("arbitrary",)),
        name="in_projection",
    )(x, mod, gain.reshape(1, D_MODEL), w_in.astype(BF16),
      jnp.tile(q_gain, ATTN_HEADS).reshape(1, ATTN_WIDTH),
      jnp.tile(k_gain, ATTN_HEADS).reshape(1, ATTN_WIDTH),
      jnp.asarray(grp, BF16))


PAIRS_PER_HALF = ATTN_HEADS // 4
GROUP = 4


def _pair_scores(qp, kp, vp, bias2, masks, low):
    q2 = jnp.concatenate([qp * masks[0], qp * masks[1]], axis=0)
    s = _dot_nt(q2, kp) + bias2
    m = jnp.max(s, axis=-1, keepdims=True)
    p = jnp.exp(s - m)
    den = jnp.sum(p, axis=-1, keepdims=True)
    pv = _dot(p.astype(BF16), vp)
    pick = lambda t: jnp.where(low, t[:BLOCK], t[BLOCK:])
    return pick(pv), pick(m), pick(den)


def _attn_kernel(q1_ref, k1_ref, v1_ref, q4_ref, k4_ref, v4_ref, q16_ref, k16_ref, v16_ref, bm_ref,
                 o_ref, acc_ref, max_ref, den_ref):
    lane = lax.broadcasted_iota(jnp.int32, (BLOCK, LANES), 1)
    low = lane < ATTN_HEAD_DIM
    masks = (jnp.where(low, 1.0, 0.0).astype(BF16), jnp.where(low, 0.0, 1.0).astype(BF16))

    def block(q_ref, k_ref, v_ref, c0, q0, w0, width, bias_set, rows, first):
        for p in range(PAIRS_PER_HALF):
            cs = slice(c0 + p * LANES, c0 + (p + 1) * LANES)
            acc, m, den = _pair_scores(q_ref[pl.ds(q0, BLOCK), cs], k_ref[pl.ds(w0, width), cs],
                                       v_ref[pl.ds(w0, width), cs], bm_ref[bias_set, p, :, 0:width],
                                       masks, low)
            if not first:
                m_old = max_ref[p, rows, :]
                m_new = jnp.maximum(m_old, m)
                a, b = jnp.exp(m_old - m_new), jnp.exp(m - m_new)
                den = den_ref[p, rows, :] * a + den * b
                acc = acc_ref[p, rows, :] * a + acc * b
                m = m_new
            max_ref[p, rows, :] = m
            den_ref[p, rows, :] = den
            acc_ref[p, rows, :] = acc

    def d1_group(g, carry):
        for u in range(GROUP):
            n = g * GROUP + u
            q0 = pl.multiple_of(n * BLOCK, BLOCK)
            w0 = pl.multiple_of(jnp.maximum(n - 1, 0) * BLOCK, BLOCK)
            bias_set = jnp.where(n == 0, BIAS_FIRST[1], BIAS_FULL[1])
            block(q1_ref, k1_ref, v1_ref, 0, q0, w0, 2 * BLOCK, bias_set, pl.ds(q0, BLOCK), True)
        return carry
    lax.fori_loop(0, SEQ // BLOCK // GROUP, d1_group, 0)

    for r in range(4):
        for n in range(SEQ // 4 // BLOCK):
            block(q4_ref, k4_ref, v4_ref, r * HALF, n * BLOCK, max(n - 1, 0) * BLOCK, 2 * BLOCK,
                  BIAS_FIRST[4] if n == 0 else BIAS_FULL[4],
                  pl.ds(r + 4 * BLOCK * n, BLOCK, stride=4), False)

    for r in range(16):
        block(q16_ref, k16_ref, v16_ref, r * HALF, 0, 0, BLOCK, BIAS_FIRST[16],
              pl.ds(r, BLOCK, stride=16), False)

    for n in range(SEQ // BLOCK):
        rows = slice(n * BLOCK, (n + 1) * BLOCK)
        for p in range(PAIRS_PER_HALF):
            o_ref[rows, p * LANES:(p + 1) * LANES] = (acc_ref[p, rows, :] / den_ref[p, rows, :]).astype(BF16)


def _dilated_attention(qkv, bias_masks):
    tokens = qkv[0].shape[0]
    batch = tokens // SEQ
    specs = []
    for dil in (1, 4, 16):
        specs += [pl.BlockSpec((SEQ // dil, dil * HALF), lambda b, hf: (b, hf))] * 3
    state = pltpu.VMEM((PAIRS_PER_HALF, SEQ, LANES), F32)
    return pl.pallas_call(
        _attn_kernel,
        out_shape=jax.ShapeDtypeStruct((tokens, ATTN_WIDTH), BF16),
        grid=(batch, 2),
        in_specs=specs + [pl.BlockSpec((N_BIAS_SETS, PAIRS_PER_HALF, 2 * BLOCK, 2 * BLOCK),
                                       lambda b, hf: (0, hf, 0, 0))],
        out_specs=pl.BlockSpec((SEQ, HALF), lambda b, hf: (b, hf)),
        scratch_shapes=[state, state, state],
        compiler_params=_cparams(("arbitrary", "arbitrary")),
        name="dilated_attention",
    )(*qkv, bias_masks)


def _retention_kernel(r_ref, cos_ref, sin_ref, dmask_ref, qdec_ref, kdec_ref, cdec_ref, gain_ref,
                      o_ref, state_ref):
    @pl.when(pl.program_id(1) == 0)
    def _():
        state_ref[...] = jnp.zeros_like(state_ref)

    lane = lax.broadcasted_iota(jnp.int32, (RET_CHUNK, LANES), 1)
    low = lane < RET_KEY_DIM
    first_half = (lane % RET_KEY_DIM) < (RET_KEY_DIM // 2)

    def rotate(t, cos, sin):
        partner = jnp.where(first_half, pltpu.roll(t, LANES - RET_KEY_DIM // 2, 1),
                            pltpu.roll(t, RET_KEY_DIM // 2, 1))
        return t * cos + partner * sin

    for c in range(RET_ROWS // RET_CHUNK):
        rows = slice(c * RET_CHUNK, (c + 1) * RET_CHUNK)
        for hp in range(RET_HEADS // 2):
            qs = slice(hp * LANES, (hp + 1) * LANES)
            ks = slice(RET_QK_WIDTH + hp * LANES, RET_QK_WIDTH + (hp + 1) * LANES)
            cos, sin = cos_ref[rows, qs], sin_ref[rows, qs]
            q_pair = rotate(r_ref[rows, qs], cos, sin) * (RET_KEY_DIM ** -0.5)
            k_pair = rotate(r_ref[rows, ks], cos, sin)
            for hh in range(2):
                head = 2 * hp + hh
                vs = slice(2 * RET_QK_WIDTH + head * LANES, 2 * RET_QK_WIDTH + (head + 1) * LANES)
                gs = slice(2 * RET_QK_WIDTH + RET_WIDTH + head * LANES,
                           2 * RET_QK_WIDTH + RET_WIDTH + (head + 1) * LANES)
                keep = low if hh == 0 else jnp.logical_not(low)
                qm = jnp.where(keep, q_pair, 0.0)
                vb = r_ref[rows, vs].astype(BF16)
                state = state_ref[head]
                inner = _dot_nt(qm.astype(BF16), k_pair.astype(BF16)) * dmask_ref[head]
                y = _dot(inner.astype(BF16), vb)
                y = y + _dot((qm * qdec_ref[head]).astype(BF16), state.astype(BF16))
                state_ref[head] = state * cdec_ref[head] + _dot_tn((k_pair * kdec_ref[head]).astype(BF16), vb)
                mu = jnp.mean(y, axis=-1, keepdims=True)
                yc = y - mu
                var = jnp.mean(yc * yc, axis=-1, keepdims=True)
                yn = yc * lax.rsqrt(var + EPS) * gain_ref[:, head * LANES:(head + 1) * LANES]
                o_ref[rows, head * LANES:(head + 1) * LANES] = (_silu(r_ref[rows, gs]) * yn).astype(BF16)


def _retention_tables():
    half = RET_KEY_DIM // 2
    pos = jnp.arange(SEQ, dtype=F32)
    inv = ROPE_BASE ** (-jnp.arange(half, dtype=F32) / half)
    ang = pos[:, None] * inv[None, :]
    cos, sin = jnp.cos(ang), jnp.sin(ang)
    cos_full = jnp.tile(jnp.concatenate([cos, cos], axis=-1), (1, RET_HEADS))
    sin_signed = jnp.tile(jnp.concatenate([-sin, sin], axis=-1), (1, RET_HEADS))
    log_g = jnp.log(1.0 - 2.0 ** (-5.0 - jnp.arange(RET_HEADS, dtype=F32)))
    idx = jnp.arange(RET_CHUNK, dtype=F32)
    diff = idx[:, None] - idx[None, :]
    dmask = jnp.where(diff >= 0, jnp.exp(jnp.maximum(diff, 0.0)[None] * log_g[:, None, None]), 0.0)
    q_decay = jnp.exp((idx + 1.0)[None, :] * log_g[:, None])[..., None]
    k_decay = jnp.exp((RET_CHUNK - 1.0 - idx)[None, :] * log_g[:, None])[..., None]
    chunk_decay = jnp.exp(RET_CHUNK * log_g)[:, None, None]
    full = (RET_HEADS, RET_CHUNK, LANES)
    return (cos_full, sin_signed, dmask, jnp.broadcast_to(q_decay, full),
            jnp.broadcast_to(k_decay, full), jnp.broadcast_to(chunk_decay, full))


def _retention(ret_in, ret_gain):
    tokens = ret_in.shape[0]
    batch = tokens // SEQ
    per_seq = SEQ // RET_ROWS
    cos, sin, dmask, qdec, kdec, cdec = _retention_tables()
    tab = pl.BlockSpec((RET_ROWS, RET_QK_WIDTH), lambda b, j: (j, 0))
    const3 = pl.BlockSpec((RET_HEADS, RET_CHUNK, LANES), lambda b, j: (0, 0, 0))
    return pl.pallas_call(
        _retention_kernel,
        out_shape=jax.ShapeDtypeStruct((tokens, RET_WIDTH), BF16),
        grid=(batch, per_seq),
        in_specs=[
            pl.BlockSpec((RET_ROWS, RET_IN_WIDTH), lambda b, j: (b * per_seq + j, 0)),
            tab, tab, const3, const3, const3, const3,
            pl.BlockSpec((1, RET_WIDTH), lambda b, j: (0, 0)),
        ],
        out_specs=pl.BlockSpec((RET_ROWS, RET_WIDTH), lambda b, j: (b * per_seq + j, 0)),
        scratch_shapes=[pltpu.VMEM((RET_HEADS, LANES, RET_VALUE_DIM), F32)],
        compiler_params=_cparams(("arbitrary", "arbitrary")),
        name="retention",
    )(ret_in, cos, sin, dmask, qdec, kdec, cdec, ret_gain.reshape(1, RET_WIDTH))


def _out_proj_kernel(attn_ref, ret_ref, x_ref, mod_ref, w_ref, out_ref):
    mix = _dot(attn_ref[...], w_ref[:ATTN_WIDTH, :]) + _dot(ret_ref[...], w_ref[ATTN_WIDTH:, :])
    out_ref[...] = x_ref[...] + mod_ref[2:3, :] * mix


def _out_proj(attn, ret, x, mod, w_out):
    tokens = x.shape[0]
    tm = TM_FFN
    per_seq = SEQ // tm
    row = lambda i: (i, 0)
    return pl.pallas_call(
        _out_proj_kernel,
        out_shape=jax.ShapeDtypeStruct((tokens, D_MODEL), F32),
        grid=(tokens // tm,),
        in_specs=[pl.BlockSpec((tm, ATTN_WIDTH), row),
                  pl.BlockSpec((tm, RET_WIDTH), row),
                  pl.BlockSpec((tm, D_MODEL), row),
                  pl.BlockSpec((None, 6, D_MODEL), lambda i: (i // per_seq, 0, 0)),
                  pl.BlockSpec((D_MODEL, D_MODEL), lambda i: (0, 0))],
        out_specs=pl.BlockSpec((tm, D_MODEL), row),
        compiler_params=_cparams(("arbitrary",)),
        name="out_projection",
    )(attn, ret, x, mod, w_out.astype(BF16))


def _ffn_kernel(x_ref, mod_ref, gain_ref, w1_ref, w3_ref, w2_ref, out_ref, h_ref, acc_ref):
    f = pl.program_id(1)

    @pl.when(f == 0)
    def _():
        mod = mod_ref[...]
        h_ref[...] = _modulated_norm(x_ref[...], gain_ref[...], mod[4:5], mod[3:4]).astype(BF16)
        acc_ref[...] = jnp.zeros_like(acc_ref)

    h = h_ref[...]
    z = (_silu(_dot(h, w1_ref[...])) * _dot(h, w3_ref[...])).astype(BF16)
    acc_ref[...] += _dot(z, w2_ref[...])

    @pl.when(f == pl.num_programs(1) - 1)
    def _():
        out_ref[...] = x_ref[...] + mod_ref[5:6, :] * acc_ref[...]


def _dense_ffn(x, mod, gain, w1, w3, w2):
    tokens = x.shape[0]
    d_ff = w1.shape[1]
    tm, tf = TM_FFN, TF_FFN
    per_seq = SEQ // tm
    return pl.pallas_call(
        _ffn_kernel,
        out_shape=jax.ShapeDtypeStruct((tokens, D_MODEL), F32),
        grid=(tokens // tm, d_ff // tf),
        in_specs=[
            pl.BlockSpec((tm, D_MODEL), lambda i, f: (i, 0)),
            pl.BlockSpec((None, 6, D_MODEL), lambda i, f: (i // per_seq, 0, 0)),
            pl.BlockSpec((1, D_MODEL), lambda i, f: (0, 0)),
            pl.BlockSpec((D_MODEL, tf), lambda i, f: (0, f)),
            pl.BlockSpec((D_MODEL, tf), lambda i, f: (0, f)),
            pl.BlockSpec((tf, D_MODEL), lambda i, f: (f, 0)),
        ],
        out_specs=pl.BlockSpec((tm, D_MODEL), lambda i, f: (i, 0)),
        scratch_shapes=[pltpu.VMEM((tm, D_MODEL), BF16), pltpu.VMEM((tm, D_MODEL), F32)],
        compiler_params=_cparams(("arbitrary", "arbitrary")),
        name="dense_swiglu",
    )(x, mod, gain.reshape(1, D_MODEL), w1.astype(BF16), w3.astype(BF16), w2.astype(BF16))


def _router_kernel(x_ref, mod_ref, gain_ref, wr_ref, h_ref, gates_ref, pos_ref, post_ref, start_ref,
                   cnt_ref, carry_ref, *, tiles_per_seq):
    i = pl.program_id(0)

    @pl.when(i % tiles_per_seq == 0)
    def _():
        carry_ref[...] = jnp.zeros_like(carry_ref)

    mod = mod_ref[...]
    h = _modulated_norm(x_ref[...], gain_ref[...], mod[4:5], mod[3:4]).astype(BF16)
    h_ref[...] = h
    tm = h.shape[0]
    lane = lax.broadcasted_iota(jnp.int32, (tm, LANES), 1).astype(F32)
    logits = jnp.where(lane < N_EXPERTS, _dot(h, wr_ref[...]), -jnp.inf)
    m1 = jnp.max(logits, axis=-1, keepdims=True)
    i1 = jnp.min(jnp.where(logits == m1, lane, float(LANES)), axis=-1, keepdims=True)
    rest = jnp.where(lane == i1, -jnp.inf, logits)
    m2 = jnp.max(rest, axis=-1, keepdims=True)
    i2 = jnp.min(jnp.where(rest == m2, lane, float(LANES)), axis=-1, keepdims=True)
    e2 = jnp.exp(m2 - m1)
    g1 = 1.0 / (1.0 + e2)
    g2 = e2 / (1.0 + e2)
    gates_ref[...] = jnp.where(lane == i1, g1, 0.0) + jnp.where(lane == i2, g2, 0.0)
    chosen = (lane == i1) | (lane == i2)
    onehot = jnp.where(chosen, 1.0, 0.0)
    r = lax.broadcasted_iota(jnp.int32, (tm, tm), 0)
    c = lax.broadcasted_iota(jnp.int32, (tm, tm), 1)
    tril = jnp.where(c <= r, 1.0, 0.0).astype(BF16)
    incl = _dot(tril, onehot.astype(BF16))
    carry = carry_ref[0:1, :]
    start_ref[...] = carry_ref[...]
    pos = jnp.where(chosen, incl - 1.0 + carry, -1.0)
    pos_ref[...] = pos
    post_ref[...] = pos.T[:N_EXPERTS, :]
    total = carry + jnp.ceil(incl[tm - 1:tm, :] * (1.0 / ROW_ALIGN)) * ROW_ALIGN
    carry_ref[...] = jnp.broadcast_to(total, carry_ref.shape)
    cnt_ref[...] = jnp.broadcast_to(total, cnt_ref.shape)


def _router(x, mod, gain, w_router):
    tokens = x.shape[0]
    batch = tokens // SEQ
    tm = TM_PROJ
    per_seq = SEQ // tm
    wr = jnp.zeros((D_MODEL, LANES), BF16).at[:, :N_EXPERTS].set(w_router.astype(BF16))
    row = lambda i: (i, 0)
    return pl.pallas_call(
        functools.partial(_router_kernel, tiles_per_seq=per_seq),
        out_shape=(
            jax.ShapeDtypeStruct((tokens, D_MODEL), BF16),
            jax.ShapeDtypeStruct((tokens, LANES), F32),
            jax.ShapeDtypeStruct((tokens, LANES), F32),
            jax.ShapeDtypeStruct((N_EXPERTS, tokens), F32),
            jax.ShapeDtypeStruct((tokens // tm, 8, LANES), F32),
            jax.ShapeDtypeStruct((batch, 8, LANES), F32),
        ),
        grid=(tokens // tm,),
        in_specs=[
            pl.BlockSpec((tm, D_MODEL), row),
            pl.BlockSpec((None, 6, D_MODEL), lambda i: (i // per_seq, 0, 0)),
            pl.BlockSpec((1, D_MODEL), lambda i: (0, 0)),
            pl.BlockSpec((D_MODEL, LANES), lambda i: (0, 0)),
        ],
        out_specs=(
            pl.BlockSpec((tm, D_MODEL), row),
            pl.BlockSpec((tm, LANES), row),
            pl.BlockSpec((tm, LANES), row),
            pl.BlockSpec((N_EXPERTS, tm), lambda i: (0, i)),
            pl.BlockSpec((None, 8, LANES), lambda i: (i, 0, 0)),
            pl.BlockSpec((None, 8, LANES), lambda i: (i // per_seq, 0, 0)),
        ),
        scratch_shapes=[pltpu.VMEM((8, LANES), F32)],
        compiler_params=_cparams(("arbitrary",)),
        name="expert_router",
    )(x, mod, gain.reshape(1, D_MODEL), wr)


TOK_BLOCKS = SEQ // TR_MOE
assert TOK_BLOCKS * (ROW_ALIGN - 1) <= TR_MOE
TAIL_TILES = (64, 128, TR_MOE)


def _moe_kernel(tbl_ref, h_ref, post_ref, pos_ref, gates_ref, x_ref, mod_ref, w1_ref, w3_ref, w2_ref,
                out_ref, hs_ref, acc_ref):
    b, e, f = pl.program_id(0), pl.program_id(1), pl.program_id(2)
    last_f = pl.num_programs(2) - 1
    tr = TR_MOE
    base = b * (TOK_BLOCKS + 1) * N_EXPERTS + e
    starts = [pl.multiple_of(tbl_ref[base + tb * N_EXPERTS], ROW_ALIGN) for tb in range(TOK_BLOCKS)]
    total = tbl_ref[base + TOK_BLOCKS * N_EXPERTS]
    n_tiles = lax.shift_right_logical(total + (tr - 1), tr.bit_length() - 1)

    @pl.when((e == 0) & (f == 0))
    def _():
        out_ref[...] = jnp.zeros_like(out_ref)

    @pl.when(f == 0)
    def _():
        slot = lax.broadcasted_iota(jnp.int32, (tr, tr), 0).astype(F32)
        for tb in range(TOK_BLOCKS):
            ts = slice(tb * tr, (tb + 1) * tr)
            local = post_ref[pl.ds(e, 1), ts] - starts[tb].astype(F32)
            sel = jnp.where(local == slot, 1.0, 0.0).astype(BF16)
            hs_ref[pl.ds(starts[tb], tr), :] = _dot(sel, h_ref[ts, :]).astype(BF16)

        def clear(r, carry):
            acc_ref[pl.ds(pl.multiple_of(r * tr, tr), tr), :] = jnp.zeros((tr, D_MODEL), F32)
            return carry
        lax.fori_loop(0, n_tiles + 1, clear, 0)

    def swiglu(r0, m):
        rows = pl.ds(r0, m)
        hr = hs_ref[rows, :]
        z = (_silu(_dot(hr, w1_ref[...])) * _dot(hr, w3_ref[...])).astype(BF16)
        acc_ref[rows, :] += _dot(z, w2_ref[...])

    def full_tile(r, carry):
        swiglu(pl.multiple_of(r * tr, tr), tr)
        return carry
    n_full = lax.shift_right_logical(total, tr.bit_length() - 1)
    lax.fori_loop(0, n_full, full_tile, 0)
    rest = total - n_full * tr
    tail0 = pl.multiple_of(n_full * tr, tr)
    lo = 0
    for m in TAIL_TILES:
        @pl.when((rest > lo) & (rest <= m))
        def _():
            swiglu(tail0, m)
        lo = m

    @pl.when(f == last_f)
    def _():
        lane = lax.broadcasted_iota(jnp.int32, (tr, LANES), 1)
        slot = lax.broadcasted_iota(jnp.int32, (tr, tr), 1).astype(F32)
        for tb in range(TOK_BLOCKS):
            ts = slice(tb * tr, (tb + 1) * tr)
            rows = acc_ref[pl.ds(starts[tb], tr), :].astype(BF16)
            pos_col = jnp.sum(jnp.where(lane == e, pos_ref[ts, :], 0.0), axis=-1, keepdims=True)
            gate_col = jnp.sum(jnp.where(lane == e, gates_ref[ts, :], 0.0), axis=-1, keepdims=True)
            sel = jnp.where(pos_col - starts[tb].astype(F32) == slot, 1.0, 0.0).astype(BF16)
            out_ref[ts, :] += gate_col * _dot(sel, rows)

    @pl.when((e == N_EXPERTS - 1) & (f == last_f))
    def _():
        out_ref[...] = x_ref[...] + mod_ref[5:6, :] * out_ref[...]


def _cast_kernel(w_ref, o_ref):
    o_ref[...] = w_ref[...].astype(BF16)


def _chunk_columns(w, tf):
    n_e, k, f = w.shape
    return pl.pallas_call(
        _cast_kernel,
        out_shape=jax.ShapeDtypeStruct((n_e, f // tf, k, tf), BF16),
        grid=(n_e, f // tf),
        in_specs=[pl.BlockSpec((None, k, tf), lambda e, c: (e, 0, c))],
        out_specs=pl.BlockSpec((None, None, k, tf), lambda e, c: (e, c, 0, 0)),
        compiler_params=_cparams(("arbitrary", "arbitrary")),
        name="chunk_expert_weights",
    )(w)


def _moe(x, mod, h, post, pos, gates, table, w1, w3, w2):
    tokens = h.shape[0]
    batch = tokens // SEQ
    d_ff = w1.shape[2]
    tf = TF_MOE
    n_f = d_ff // tf
    once = pl.Buffered(1)
    max_rows = SEQ + 2 * TR_MOE
    grid_spec = pltpu.PrefetchScalarGridSpec(
        num_scalar_prefetch=1,
        grid=(batch, N_EXPERTS, n_f),
        in_specs=[
            pl.BlockSpec((SEQ, D_MODEL), lambda b, e, f, t: (b, 0), pipeline_mode=once),
            pl.BlockSpec((N_EXPERTS, SEQ), lambda b, e, f, t: (0, b), pipeline_mode=once),
            pl.BlockSpec((SEQ, LANES), lambda b, e, f, t: (b, 0), pipeline_mode=once),
            pl.BlockSpec((SEQ, LANES), lambda b, e, f, t: (b, 0), pipeline_mode=once),
            pl.BlockSpec((SEQ, D_MODEL), lambda b, e, f, t: (b, 0), pipeline_mode=once),
            pl.BlockSpec((None, 6, D_MODEL), lambda b, e, f, t: (b, 0, 0)),
            pl.BlockSpec((None, None, D_MODEL, tf), lambda b, e, f, t: (e, f, 0, 0)),
            pl.BlockSpec((None, None, D_MODEL, tf), lambda b, e, f, t: (e, f, 0, 0)),
            pl.BlockSpec((None, tf, D_MODEL), lambda b, e, f, t: (e, f, 0)),
        ],
        out_specs=pl.BlockSpec((SEQ, D_MODEL), lambda b, e, f, t: (b, 0), pipeline_mode=once),
        scratch_shapes=[pltpu.VMEM((max_rows, D_MODEL), BF16), pltpu.VMEM((max_rows, D_MODEL), F32)],
    )
    return pl.pallas_call(
        _moe_kernel,
        out_shape=jax.ShapeDtypeStruct((tokens, D_MODEL), F32),
        grid_spec=grid_spec,
        compiler_params=_cparams(("arbitrary", "arbitrary", "arbitrary")),
        name="expert_swiglu",
    )(table, h, post, pos, gates, x, mod, _chunk_columns(w1, tf), _chunk_columns(w3, tf), w2.astype(BF16))


def _token_mixer(x, mod, gain, w_in, q_gain, k_gain, ret_gain, w_out, bias_masks):
    *qkv, ret_in = _in_proj(x, mod, gain, w_in, q_gain, k_gain)
    attn = _dilated_attention(qkv, bias_masks)
    ret = _retention(ret_in, ret_gain)
    return _out_proj(attn, ret, x, mod, w_out)


def _moe_ffn(x, mod, gain, w_router, w1, w3, w2):
    h, gates, pos, post, start, cnt = _router(x, mod, gain, w_router)
    batch = cnt.shape[0]
    table = jnp.concatenate([start[:, 0, :N_EXPERTS].reshape(batch, TOK_BLOCKS, N_EXPERTS),
                             cnt[:, :1, :N_EXPERTS]], axis=1).astype(jnp.int32).reshape(-1)
    return _moe(x, mod, h, post, pos, gates, table, w1, w3, w2)


def kernel(x, c, rel_bias_table, norm_mix, norm_ffn, w_mod, b_mod, w_in, q_gain, k_gain, ret_gain, w_out,
           ffn_w1, ffn_w3, ffn_w2, moe_router, moe_w1, moe_w3, moe_w2):
    batch, seq, d_model = x.shape
    assert (seq, d_model) == (SEQ, D_MODEL)
    depth = w_mod.shape[0]
    mods = _modulation(c, w_mod, b_mod).reshape(depth, batch, 6, D_MODEL)
    bias_masks = _bias_masks(rel_bias_table)
    xt = x.reshape(batch * seq, d_model)
    for layer in range(depth):
        mod = mods[layer]
        xt = _token_mixer(xt, mod, norm_mix[layer], w_in[layer], q_gain[layer], k_gain[layer],
                          ret_gain[layer], w_out[layer], bias_masks)
        i = layer // 2
        if layer % 2 == 0:
            xt = _dense_ffn(xt, mod, norm_ffn[layer], ffn_w1[i], ffn_w3[i], ffn_w2[i])
        else:
            xt = _moe_ffn(xt, mod, norm_ffn[layer], moe_router[i], moe_w1[i], moe_w3[i], moe_w2[i])
    return xt.reshape(batch, seq, d_model)
```

```python
import functools
import math

import jax
import jax.numpy as jnp
import numpy as np
from jax import lax
from jax.experimental import pallas as pl
from jax.experimental.pallas import tpu as pltpu

D_MODEL = 1024
SEQ = 2048
ATTN_HEADS = 8
ATTN_HEAD_DIM = 64
ATTN_WIDTH = ATTN_HEADS * ATTN_HEAD_DIM
DILATED_PATTERNS = ((128, 1), (512, 4), (2048, 16))
BLOCK = 128
NUM_BUCKETS = 32
MAX_DISTANCE = 2048
RET_HEADS = 4
RET_KEY_DIM = 64
RET_VALUE_DIM = 128
RET_WIDTH = RET_HEADS * RET_VALUE_DIM
RET_QK_WIDTH = RET_HEADS * RET_KEY_DIM
RET_CHUNK = 128
ROPE_BASE = 10000.0
IN_WIDTH = 3 * ATTN_WIDTH + 2 * RET_QK_WIDTH + 2 * RET_WIDTH
RET_IN_WIDTH = IN_WIDTH - 3 * ATTN_WIDTH
N_EXPERTS = 8
EPS = 1e-6
NEG_INF = -1e30

LANES = 128
VMEM_LIMIT = 56 * 1024 * 1024

BF16 = jnp.bfloat16
F32 = jnp.float32

TM_PROJ = 256
TM_FFN = 512
TF_FFN = 1408
TR_MOE = 256
TF_MOE = 896
assert TM_PROJ == TR_MOE
RET_ROWS = 512
SUBLANES = 8


def _cparams(sem):
    return pltpu.CompilerParams(dimension_semantics=sem, vmem_limit_bytes=VMEM_LIMIT)


def _dot(a, b):
    return jnp.dot(a, b, preferred_element_type=F32)


def _dot_nt(a, b):
    return lax.dot_general(a, b, (((1,), (1,)), ((), ())), preferred_element_type=F32)


def _dot_tn(a, b):
    return lax.dot_general(a, b, (((0,), (0,)), ((), ())), preferred_element_type=F32)


def _split_bf16(v):
    hi = v.astype(BF16)
    lo = (v - hi.astype(F32)).astype(BF16)
    return hi, lo


def _silu(v):
    return v * (1.0 / (1.0 + jnp.exp(-v)))


def _modulated_norm(x, gain, scale, shift):
    ms = jnp.mean(x * x, axis=-1, keepdims=True)
    y = x * lax.rsqrt(ms + EPS) * gain
    return y * (1.0 + scale) + shift


def _mod_kernel(c_ref, w_ref, b_ref, o_ref):
    ca = _silu(c_ref[...]).astype(BF16)
    o_ref[...] = _dot(ca, w_ref[...].astype(BF16)) + b_ref[...]


def _modulation(c, w_mod, b_mod):
    depth, _, width = w_mod.shape
    batch = c.shape[0]
    tn = 1536
    return pl.pallas_call(
        _mod_kernel,
        out_shape=jax.ShapeDtypeStruct((depth, batch, width), F32),
        grid=(depth, width // tn),
        in_specs=[
            pl.BlockSpec((batch, D_MODEL), lambda l, n: (0, 0)),
            pl.BlockSpec((None, D_MODEL, tn), lambda l, n: (l, 0, n)),
            pl.BlockSpec((None, 1, tn), lambda l, n: (l, 0, n)),
        ],
        out_specs=pl.BlockSpec((None, batch, tn), lambda l, n: (l, 0, n)),
        compiler_params=_cparams(("arbitrary", "arbitrary")),
        name="adaln_modulation",
    )(c, w_mod, b_mod.reshape(depth, 1, width))


def _bias_kernel(table_ref, bucket_ref, o_ref):
    h = pl.program_id(1)
    bucket = bucket_ref[...]
    acc = jnp.full(bucket.shape, NEG_INF, F32)
    for b in range(NUM_BUCKETS):
        acc = jnp.where(bucket == b, table_ref[b, h], acc)
    o_ref[...] = acc


BIAS_FULL = {1: 0, 4: 2}
BIAS_FIRST = {1: 1, 4: 3, 16: 4}
N_BIAS_SETS = 5


def _bias_masks(rel_bias_table):
    i = jnp.arange(BLOCK)[:, None]
    j = jnp.arange(2 * BLOCK)[None, :]
    max_exact = NUM_BUCKETS // 2

    def bucket_of(rel, dilation, w_sub, exists):
        n = jnp.maximum(rel * dilation, 0)
        nf = jnp.maximum(n.astype(F32), float(max_exact))
        large = max_exact + (jnp.log(nf / max_exact) / math.log(MAX_DISTANCE / max_exact)
                             * (NUM_BUCKETS - max_exact)).astype(jnp.int32)
        large = jnp.minimum(large, NUM_BUCKETS - 1)
        bucket = jnp.where(n < max_exact, n, large)
        allowed = (rel >= 0) & (rel <= w_sub) & exists
        return jnp.where(allowed, bucket, -1)

    sets = [None] * N_BIAS_SETS
    for window, dilation in DILATED_PATTERNS:
        w_sub = window // dilation
        if dilation in BIAS_FULL:
            sets[BIAS_FULL[dilation]] = bucket_of(i - j + BLOCK, dilation, w_sub, j >= 0)
        sets[BIAS_FIRST[dilation]] = bucket_of(i - j, dilation, w_sub, j < BLOCK)
    buckets = jnp.stack(sets).astype(jnp.int32)
    return pl.pallas_call(
        _bias_kernel,
        out_shape=jax.ShapeDtypeStruct((N_BIAS_SETS, ATTN_HEADS // 2, 2 * BLOCK, 2 * BLOCK), F32),
        grid=(N_BIAS_SETS, ATTN_HEADS),
        in_specs=[
            pl.BlockSpec(memory_space=pltpu.SMEM),
            pl.BlockSpec((None, BLOCK, 2 * BLOCK), lambda p, h: (p, 0, 0)),
        ],
        out_specs=pl.BlockSpec((None, None, BLOCK, 2 * BLOCK), lambda p, h: (p, h // 2, h % 2, 0)),
        compiler_params=_cparams(("arbitrary", "arbitrary")),
        name="relative_bias_masks",
    )(rel_bias_table, buckets)


HALF = ATTN_WIDTH // 2


def _in_proj_kernel(x_ref, mod_ref, gain_ref, w_ref, qg_ref, kg_ref, grp_ref,
                    q1_ref, k1_ref, v1_ref, q4_ref, k4_ref, v4_ref, q16_ref, k16_ref, v16_ref, r_ref,
                    perm_ref):
    mod = mod_ref[...]
    h = _modulated_norm(x_ref[...], gain_ref[...], mod[1:2], mod[0:1]).astype(BF16)
    proj = _dot(h, w_ref[...])
    grp = grp_ref[...]
    tm = proj.shape[0]

    def head_norm(t, gain):
        hi, lo = _split_bf16(t * t)
        ss = _dot(hi, grp) + _dot(lo, grp)
        return t * lax.rsqrt(ss * (1.0 / ATTN_HEAD_DIM) + EPS) * gain

    def emit(t, o1_ref, o4_ref, o16_ref):
        o1_ref[...] = t.astype(BF16)
        for j in range(ATTN_WIDTH // LANES):
            perm_ref[j] = t[:, j * LANES:(j + 1) * LANES]
        for dil, o_ref in ((4, o4_ref), (16, o16_ref)):
            for hf in range(2):
                for r in range(dil):
                    for jj in range(HALF // LANES):
                        c0 = (hf * dil + r) * HALF + jj * LANES
                        o_ref[:, c0:c0 + LANES] = perm_ref[hf * (HALF // LANES) + jj,
                                                           pl.ds(r, tm // dil, stride=dil), :].astype(BF16)

    emit(head_norm(proj[:, :ATTN_WIDTH], qg_ref[...]) * (ATTN_HEAD_DIM ** -0.5), q1_ref, q4_ref, q16_ref)
    emit(head_norm(proj[:, ATTN_WIDTH:2 * ATTN_WIDTH], kg_ref[...]), k1_ref, k4_ref, k16_ref)
    emit(proj[:, 2 * ATTN_WIDTH:3 * ATTN_WIDTH], v1_ref, v4_ref, v16_ref)
    r_ref[...] = proj[:, 3 * ATTN_WIDTH:]


def _in_proj(x, mod, gain, w_in, q_gain, k_gain):
    tokens = x.shape[0]
    tm = TM_PROJ
    per_seq = SEQ // tm
    grp = np.kron(np.eye(ATTN_HEADS), np.ones((ATTN_HEAD_DIM, ATTN_HEAD_DIM))).astype(np.float32)
    row = lambda i: (i, 0)
    const = lambda i: (0, 0)
    layouts = []
    for dil in (1, 4, 16):
        shape = jax.ShapeDtypeStruct((tokens // dil, dil * ATTN_WIDTH), BF16)
        spec = pl.BlockSpec((tm // dil, dil * ATTN_WIDTH), row)
        layouts.append(((shape,) * 3, (spec,) * 3))
    out_shape = sum((s for s, _ in layouts), ()) + (jax.ShapeDtypeStruct((tokens, RET_IN_WIDTH), F32),)
    out_specs = sum((s for _, s in layouts), ()) + (pl.BlockSpec((tm, RET_IN_WIDTH), row),)
    return pl.pallas_call(
        _in_proj_kernel,
        out_shape=out_shape,
        grid=(tokens // tm,),
        in_specs=[
            pl.BlockSpec((tm, D_MODEL), row),
            pl.BlockSpec((None, 6, D_MODEL), lambda i: (i // per_seq, 0, 0)),
            pl.BlockSpec((1, D_MODEL), const),
            pl.BlockSpec((D_MODEL, IN_WIDTH), const),
            pl.BlockSpec((1, ATTN_WIDTH), const),
            pl.BlockSpec((1, ATTN_WIDTH), const),
            pl.BlockSpec((ATTN_WIDTH, ATTN_WIDTH), const),
        ],
        out_specs=out_specs,
        scratch_shapes=[pltpu.VMEM((ATTN_WIDTH // LANES, tm, LANES), F32)],
        compiler_params=_cparams(("arbitrary",)),
        name="in_projection",
    )(x, mod, gain.reshape(1, D_MODEL), w_in.astype(BF16),
      jnp.tile(q_gain, ATTN_HEADS).reshape(1, ATTN_WIDTH),
      jnp.tile(k_gain, ATTN_HEADS).reshape(1, ATTN_WIDTH),
      jnp.asarray(grp, BF16))


PAIRS_PER_HALF = ATTN_HEADS // 4
GROUP = 4


def _pair_scores(qp, kp, vp, bias2, masks, low):
    q2 = jnp.concatenate([qp * masks[0], qp * masks[1]], axis=0)
    s = _dot_nt(q2, kp) + bias2
    m = jnp.max(s, axis=-1, keepdims=True)
    p = jnp.exp(s - m)
    den = jnp.sum(p, axis=-1, keepdims=True)
    pv = _dot(p.astype(BF16), vp)
    pick = lambda t: jnp.where(low, t[:BLOCK], t[BLOCK:])
    return pick(pv), pick(m), pick(den)


def _attn_kernel(q1_ref, k1_ref, v1_ref, q4_ref, k4_ref, v4_ref, q16_ref, k16_ref, v16_ref, bm_ref,
                 o_ref, acc_ref, max_ref, den_ref):
    lane = lax.broadcasted_iota(jnp.int32, (BLOCK, LANES), 1)
    low = lane < ATTN_HEAD_DIM
    masks = (jnp.where(low, 1.0, 0.0).astype(BF16), jnp.where(low, 0.0, 1.0).astype(BF16))

    def block(q_ref, k_ref, v_ref, c0, q0, w0, width, bias_set, rows, first):
        for p in range(PAIRS_PER_HALF):
            cs = slice(c0 + p * LANES, c0 + (p + 1) * LANES)
            acc, m, den = _pair_scores(q_ref[pl.ds(q0, BLOCK), cs], k_ref[pl.ds(w0, width), cs],
                                       v_ref[pl.ds(w0, width), cs], bm_ref[bias_set, p, :, 0:width],
                                       masks, low)
            if not first:
                m_old = max_ref[p, rows, :]
                m_new = jnp.maximum(m_old, m)
                a, b = jnp.exp(m_old - m_new), jnp.exp(m - m_new)
                den = den_ref[p, rows, :] * a + den * b
                acc = acc_ref[p, rows, :] * a + acc * b
                m = m_new
            max_ref[p, rows, :] = m
            den_ref[p, rows, :] = den
            acc_ref[p, rows, :] = acc

    def d1_group(g, carry):
        for u in range(GROUP):
            n = g * GROUP + u
            q0 = pl.multiple_of(n * BLOCK, BLOCK)
            w0 = pl.multiple_of(jnp.maximum(n - 1, 0) * BLOCK, BLOCK)
            bias_set = jnp.where(n == 0, BIAS_FIRST[1], BIAS_FULL[1])
            block(q1_ref, k1_ref, v1_ref, 0, q0, w0, 2 * BLOCK, bias_set, pl.ds(q0, BLOCK), True)
        return carry
    lax.fori_loop(0, SEQ // BLOCK // GROUP, d1_group, 0)

    for r in range(4):
        for n in range(SEQ // 4 // BLOCK):
            block(q4_ref, k4_ref, v4_ref, r * HALF, n * BLOCK, max(n - 1, 0) * BLOCK, 2 * BLOCK,
                  BIAS_FIRST[4] if n == 0 else BIAS_FULL[4],
                  pl.ds(r + 4 * BLOCK * n, BLOCK, stride=4), False)

    for r in range(16):
        block(q16_ref, k16_ref, v16_ref, r * HALF, 0, 0, BLOCK, BIAS_FIRST[16],
              pl.ds(r, BLOCK, stride=16), False)

    for n in range(SEQ // BLOCK):
        rows = slice(n * BLOCK, (n + 1) * BLOCK)
        for p in range(PAIRS_PER_HALF):
            o_ref[rows, p * LANES:(p + 1) * LANES] = (acc_ref[p, rows, :] / den_ref[p, rows, :]).astype(BF16)


def _dilated_attention(qkv, bias_masks):
    tokens = qkv[0].shape[0]
    batch = tokens // SEQ
    specs = []
    for dil in (1, 4, 16):
        specs += [pl.BlockSpec((SEQ // dil, dil * HALF), lambda b, hf: (b, hf))] * 3
    state = pltpu.VMEM((PAIRS_PER_HALF, SEQ, LANES), F32)
    return pl.pallas_call(
        _attn_kernel,
        out_shape=jax.ShapeDtypeStruct((tokens, ATTN_WIDTH), BF16),
        grid=(batch, 2),
        in_specs=specs + [pl.BlockSpec((N_BIAS_SETS, PAIRS_PER_HALF, 2 * BLOCK, 2 * BLOCK),
                                       lambda b, hf: (0, hf, 0, 0))],
        out_specs=pl.BlockSpec((SEQ, HALF), lambda b, hf: (b, hf)),
        scratch_shapes=[state, state, state],
        compiler_params=_cparams(("arbitrary", "arbitrary")),
        name="dilated_attention",
    )(*qkv, bias_masks)


def _retention_kernel(r_ref, cos_ref, sin_ref, dmask_ref, qdec_ref, kdec_ref, cdec_ref, gain_ref,
                      o_ref, state_ref):
    @pl.when(pl.program_id(1) == 0)
    def _():
        state_ref[...] = jnp.zeros_like(state_ref)

    lane = lax.broadcasted_iota(jnp.int32, (RET_CHUNK, LANES), 1)
    low = lane < RET_KEY_DIM
    first_half = (lane % RET_KEY_DIM) < (RET_KEY_DIM // 2)

    def rotate(t, cos, sin):
        partner = jnp.where(first_half, pltpu.roll(t, LANES - RET_KEY_DIM // 2, 1),
                            pltpu.roll(t, RET_KEY_DIM // 2, 1))
        return t * cos + partner * sin

    for c in range(RET_ROWS // RET_CHUNK):
        rows = slice(c * RET_CHUNK, (c + 1) * RET_CHUNK)
        for hp in range(RET_HEADS // 2):
            qs = slice(hp * LANES, (hp + 1) * LANES)
            ks = slice(RET_QK_WIDTH + hp * LANES, RET_QK_WIDTH + (hp + 1) * LANES)
            cos, sin = cos_ref[rows, qs], sin_ref[rows, qs]
            q_pair = rotate(r_ref[rows, qs], cos, sin) * (RET_KEY_DIM ** -0.5)
            k_pair = rotate(r_ref[rows, ks], cos, sin)
            for hh in range(2):
                head = 2 * hp + hh
                vs = slice(2 * RET_QK_WIDTH + head * LANES, 2 * RET_QK_WIDTH + (head + 1) * LANES)
                gs = slice(2 * RET_QK_WIDTH + RET_WIDTH + head * LANES,
                           2 * RET_QK_WIDTH + RET_WIDTH + (head + 1) * LANES)
                keep = low if hh == 0 else jnp.logical_not(low)
                qm = jnp.where(keep, q_pair, 0.0)
                vb = r_ref[rows, vs].astype(BF16)
                state = state_ref[head]
                inner = _dot_nt(qm.astype(BF16), k_pair.astype(BF16)) * dmask_ref[head]
                y = _dot(inner.astype(BF16), vb)
                y = y + _dot((qm * qdec_ref[head]).astype(BF16), state.astype(BF16))
                state_ref[head] = state * cdec_ref[head] + _dot_tn((k_pair * kdec_ref[head]).astype(BF16), vb)
                mu = jnp.mean(y, axis=-1, keepdims=True)
                yc = y - mu
                var = jnp.mean(yc * yc, axis=-1, keepdims=True)
                yn = yc * lax.rsqrt(var + EPS) * gain_ref[:, head * LANES:(head + 1) * LANES]
                o_ref[rows, head * LANES:(head + 1) * LANES] = (_silu(r_ref[rows, gs]) * yn).astype(BF16)


def _retention_tables():
    half = RET_KEY_DIM // 2
    pos = jnp.arange(SEQ, dtype=F32)
    inv = ROPE_BASE ** (-jnp.arange(half, dtype=F32) / half)
    ang = pos[:, None] * inv[None, :]
    cos, sin = jnp.cos(ang), jnp.sin(ang)
    cos_full = jnp.tile(jnp.concatenate([cos, cos], axis=-1), (1, RET_HEADS))
    sin_signed = jnp.tile(jnp.concatenate([-sin, sin], axis=-1), (1, RET_HEADS))
    log_g = jnp.log(1.0 - 2.0 ** (-5.0 - jnp.arange(RET_HEADS, dtype=F32)))
    idx = jnp.arange(RET_CHUNK, dtype=F32)
    diff = idx[:, None] - idx[None, :]
    dmask = jnp.where(diff >= 0, jnp.exp(jnp.maximum(diff, 0.0)[None] * log_g[:, None, None]), 0.0)
    q_decay = jnp.exp((idx + 1.0)[None, :] * log_g[:, None])[..., None]
    k_decay = jnp.exp((RET_CHUNK - 1.0 - idx)[None, :] * log_g[:, None])[..., None]
    chunk_decay = jnp.exp(RET_CHUNK * log_g)[:, None, None]
    full = (RET_HEADS, RET_CHUNK, LANES)
    return (cos_full, sin_signed, dmask, jnp.broadcast_to(q_decay, full),
            jnp.broadcast_to(k_decay, full), jnp.broadcast_to(chunk_decay, full))


def _retention(ret_in, ret_gain):
    tokens = ret_in.shape[0]
    batch = tokens // SEQ
    per_seq = SEQ // RET_ROWS
    cos, sin, dmask, qdec, kdec, cdec = _retention_tables()
    tab = pl.BlockSpec((RET_ROWS, RET_QK_WIDTH), lambda b, j: (j, 0))
    const3 = pl.BlockSpec((RET_HEADS, RET_CHUNK, LANES), lambda b, j: (0, 0, 0))
    return pl.pallas_call(
        _retention_kernel,
        out_shape=jax.ShapeDtypeStruct((tokens, RET_WIDTH), BF16),
        grid=(batch, per_seq),
        in_specs=[
            pl.BlockSpec((RET_ROWS, RET_IN_WIDTH), lambda b, j: (b * per_seq + j, 0)),
            tab, tab, const3, const3, const3, const3,
            pl.BlockSpec((1, RET_WIDTH), lambda b, j: (0, 0)),
        ],
        out_specs=pl.BlockSpec((RET_ROWS, RET_WIDTH), lambda b, j: (b * per_seq + j, 0)),
        scratch_shapes=[pltpu.VMEM((RET_HEADS, LANES, RET_VALUE_DIM), F32)],
        compiler_params=_cparams(("arbitrary", "arbitrary")),
        name="retention",
    )(ret_in, cos, sin, dmask, qdec, kdec, cdec, ret_gain.reshape(1, RET_WIDTH))


def _out_proj_kernel(attn_ref, ret_ref, x_ref, mod_ref, w_ref, out_ref):
    mix = _dot(attn_ref[...], w_ref[:ATTN_WIDTH, :]) + _dot(ret_ref[...], w_ref[ATTN_WIDTH:, :])
    out_ref[...] = x_ref[...] + mod_ref[2:3, :] * mix


def _out_proj(attn, ret, x, mod, w_out):
    tokens = x.shape[0]
    tm = TM_FFN
    per_seq = SEQ // tm
    row = lambda i: (i, 0)
    return pl.pallas_call(
        _out_proj_kernel,
        out_shape=jax.ShapeDtypeStruct((tokens, D_MODEL), F32),
        grid=(tokens // tm,),
        in_specs=[pl.BlockSpec((tm, ATTN_WIDTH), row),
                  pl.BlockSpec((tm, RET_WIDTH), row),
                  pl.BlockSpec((tm, D_MODEL), row),
                  pl.BlockSpec((None, 6, D_MODEL), lambda i: (i // per_seq, 0, 0)),
                  pl.BlockSpec((D_MODEL, D_MODEL), lambda i: (0, 0))],
        out_specs=pl.BlockSpec((tm, D_MODEL), row),
        compiler_params=_cparams(("arbitrary",)),
        name="out_projection",
    )(attn, ret, x, mod, w_out.astype(BF16))


def _ffn_kernel(x_ref, mod_ref, gain_ref, w1_ref, w3_ref, w2_ref, out_ref, h_ref, acc_ref):
    f = pl.program_id(1)

    @pl.when(f == 0)
    def _():
        mod = mod_ref[...]
        h_ref[...] = _modulated_norm(x_ref[...], gain_ref[...], mod[4:5], mod[3:4]).astype(BF16)
        acc_ref[...] = jnp.zeros_like(acc_ref)

    h = h_ref[...]
    z = (_silu(_dot(h, w1_ref[...])) * _dot(h, w3_ref[...])).astype(BF16)
    acc_ref[...] += _dot(z, w2_ref[...])

    @pl.when(f == pl.num_programs(1) - 1)
    def _():
        out_ref[...] = x_ref[...] + mod_ref[5:6, :] * acc_ref[...]


def _dense_ffn(x, mod, gain, w1, w3, w2):
    tokens = x.shape[0]
    d_ff = w1.shape[1]
    tm, tf = TM_FFN, TF_FFN
    per_seq = SEQ // tm
    return pl.pallas_call(
        _ffn_kernel,
        out_shape=jax.ShapeDtypeStruct((tokens, D_MODEL), F32),
        grid=(tokens // tm, d_ff // tf),
        in_specs=[
            pl.BlockSpec((tm, D_MODEL), lambda i, f: (i, 0)),
            pl.BlockSpec((None, 6, D_MODEL), lambda i, f: (i // per_seq, 0, 0)),
            pl.BlockSpec((1, D_MODEL), lambda i, f: (0, 0)),
            pl.BlockSpec((D_MODEL, tf), lambda i, f: (0, f)),
            pl.BlockSpec((D_MODEL, tf), lambda i, f: (0, f)),
            pl.BlockSpec((tf, D_MODEL), lambda i, f: (f, 0)),
        ],
        out_specs=pl.BlockSpec((tm, D_MODEL), lambda i, f: (i, 0)),
        scratch_shapes=[pltpu.VMEM((tm, D_MODEL), BF16), pltpu.VMEM((tm, D_MODEL), F32)],
        compiler_params=_cparams(("arbitrary", "arbitrary")),
        name="dense_swiglu",
    )(x, mod, gain.reshape(1, D_MODEL), w1.astype(BF16), w3.astype(BF16), w2.astype(BF16))


def _router_kernel(x_ref, mod_ref, gain_ref, wr_ref, h_ref, gates_ref, pos_ref, post_ref, start_ref,
                   cnt_ref, carry_ref, *, tiles_per_seq):
    i = pl.program_id(0)

    @pl.when(i % tiles_per_seq == 0)
    def _():
        carry_ref[...] = jnp.zeros_like(carry_ref)

    mod = mod_ref[...]
    h = _modulated_norm(x_ref[...], gain_ref[...], mod[4:5], mod[3:4]).astype(BF16)
    h_ref[...] = h
    tm = h.shape[0]
    lane = lax.broadcasted_iota(jnp.int32, (tm, LANES), 1).astype(F32)
    logits = jnp.where(lane < N_EXPERTS, _dot(h, wr_ref[...]), -jnp.inf)
    m1 = jnp.max(logits, axis=-1, keepdims=True)
    i1 = jnp.min(jnp.where(logits == m1, lane, float(LANES)), axis=-1, keepdims=True)
    rest = jnp.where(lane == i1, -jnp.inf, logits)
    m2 = jnp.max(rest, axis=-1, keepdims=True)
    i2 = jnp.min(jnp.where(rest == m2, lane, float(LANES)), axis=-1, keepdims=True)
    e2 = jnp.exp(m2 - m1)
    g1 = 1.0 / (1.0 + e2)
    g2 = e2 / (1.0 + e2)
    gates_ref[...] = jnp.where(lane == i1, g1, 0.0) + jnp.where(lane == i2, g2, 0.0)
    chosen = (lane == i1) | (lane == i2)
    onehot = jnp.where(chosen, 1.0, 0.0)
    r = lax.broadcasted_iota(jnp.int32, (tm, tm), 0)
    c = lax.broadcasted_iota(jnp.int32, (tm, tm), 1)
    tril = jnp.where(c <= r, 1.0, 0.0).astype(BF16)
    incl = _dot(tril, onehot.astype(BF16))
    carry = carry_ref[0:1, :]
    start_ref[...] = carry_ref[...]
    pos = jnp.where(chosen, incl - 1.0 + carry, -1.0)
    pos_ref[...] = pos
    post_ref[...] = pos.T[:N_EXPERTS, :]
    total = carry + incl[tm - 1:tm, :]
    carry_ref[...] = jnp.broadcast_to(total, carry_ref.shape)
    cnt_ref[...] = jnp.broadcast_to(total, cnt_ref.shape)


def _router(x, mod, gain, w_router):
    tokens = x.shape[0]
    batch = tokens // SEQ
    tm = TM_PROJ
    per_seq = SEQ // tm
    wr = jnp.zeros((D_MODEL, LANES), BF16).at[:, :N_EXPERTS].set(w_router.astype(BF16))
    row = lambda i: (i, 0)
    return pl.pallas_call(
        functools.partial(_router_kernel, tiles_per_seq=per_seq),
        out_shape=(
            jax.ShapeDtypeStruct((tokens, D_MODEL), BF16),
            jax.ShapeDtypeStruct((tokens, LANES), F32),
            jax.ShapeDtypeStruct((tokens, LANES), F32),
            jax.ShapeDtypeStruct((N_EXPERTS, tokens), F32),
            jax.ShapeDtypeStruct((tokens // tm, 8, LANES), F32),
            jax.ShapeDtypeStruct((batch, 8, LANES), F32),
        ),
        grid=(tokens // tm,),
        in_specs=[
            pl.BlockSpec((tm, D_MODEL), row),
            pl.BlockSpec((None, 6, D_MODEL), lambda i: (i // per_seq, 0, 0)),
            pl.BlockSpec((1, D_MODEL), lambda i: (0, 0)),
            pl.BlockSpec((D_MODEL, LANES), lambda i: (0, 0)),
        ],
        out_specs=(
            pl.BlockSpec((tm, D_MODEL), row),
            pl.BlockSpec((tm, LANES), row),
            pl.BlockSpec((tm, LANES), row),
            pl.BlockSpec((N_EXPERTS, tm), lambda i: (0, i)),
            pl.BlockSpec((None, 8, LANES), lambda i: (i, 0, 0)),
            pl.BlockSpec((None, 8, LANES), lambda i: (i // per_seq, 0, 0)),
        ),
        scratch_shapes=[pltpu.VMEM((8, LANES), F32)],
        compiler_params=_cparams(("arbitrary",)),
        name="expert_router",
    )(x, mod, gain.reshape(1, D_MODEL), wr)


TOK_BLOCKS = SEQ // TR_MOE
TAIL_TILES = (64, 128, TR_MOE)
SPILL_ROWS = 16


def _moe_kernel(tbl_ref, h_ref, post_ref, pos_ref, gates_ref, x_ref, mod_ref, w1_ref, w3_ref, w2_ref,
                out_ref, hs_ref, acc_ref):
    b, e, f = pl.program_id(0), pl.program_id(1), pl.program_id(2)
    last_f = pl.num_programs(2) - 1
    tr = TR_MOE
    base = b * (TOK_BLOCKS + 1) * N_EXPERTS + e
    bounds = [tbl_ref[base + tb * N_EXPERTS] for tb in range(TOK_BLOCKS + 1)]
    win = [pl.multiple_of(lax.shift_left(lax.shift_right_logical(s, 3), 3), SUBLANES) for s in bounds[:-1]]
    spills = [bounds[tb + 1] - win[tb] > tr for tb in range(TOK_BLOCKS)]
    total = bounds[-1]
    n_tiles = lax.shift_right_logical(total + (tr - 1), tr.bit_length() - 1)

    def clear(r, carry):
        acc_ref[pl.ds(pl.multiple_of(r * tr, tr), tr), :] = jnp.zeros((tr, D_MODEL), F32)
        return carry

    @pl.when((e == 0) & (f == 0))
    def _():
        out_ref[...] = jnp.zeros_like(out_ref)

    @pl.when(f == 0)
    def _():
        lax.fori_loop(0, n_tiles + 1, clear, 0)
        slot = lax.broadcasted_iota(jnp.int32, (tr, tr), 0).astype(F32)
        slot_spill = lax.broadcasted_iota(jnp.int32, (SPILL_ROWS, tr), 0).astype(F32) + float(tr)
        for tb in range(TOK_BLOCKS):
            ts = slice(tb * tr, (tb + 1) * tr)
            local = post_ref[pl.ds(e, 1), ts] - win[tb].astype(F32)
            sel = jnp.where(local == slot, 1.0, 0.0).astype(BF16)
            acc_ref[pl.ds(win[tb], tr), :] += _dot(sel, h_ref[ts, :])

            @pl.when(spills[tb])
            def _():
                sel2 = jnp.where(local == slot_spill, 1.0, 0.0).astype(BF16)
                acc_ref[pl.ds(win[tb] + tr, SPILL_ROWS), :] += _dot(sel2, h_ref[ts, :])

        def to_rows(r, carry):
            rows = pl.ds(pl.multiple_of(r * tr, tr), tr)
            hs_ref[rows, :] = acc_ref[rows, :].astype(BF16)
            acc_ref[rows, :] = jnp.zeros((tr, D_MODEL), F32)
            return carry
        lax.fori_loop(0, n_tiles + 1, to_rows, 0)

    def swiglu(r0, m):
        rows = pl.ds(r0, m)
        hr = hs_ref[rows, :]
        z = (_silu(_dot(hr, w1_ref[...])) * _dot(hr, w3_ref[...])).astype(BF16)
        acc_ref[rows, :] += _dot(z, w2_ref[...])

    def full_tile(r, carry):
        swiglu(pl.multiple_of(r * tr, tr), tr)
        return carry
    n_full = lax.shift_right_logical(total, tr.bit_length() - 1)
    lax.fori_loop(0, n_full, full_tile, 0)
    rest = total - n_full * tr
    tail0 = pl.multiple_of(n_full * tr, tr)
    lo = 0
    for m in TAIL_TILES:
        @pl.when((rest > lo) & (rest <= m))
        def _():
            swiglu(tail0, m)
        lo = m

    @pl.when(f == last_f)
    def _():
        lane = lax.broadcasted_iota(jnp.int32, (tr, LANES), 1)
        slot = lax.broadcasted_iota(jnp.int32, (tr, tr), 1).astype(F32)
        slot_spill = lax.broadcasted_iota(jnp.int32, (tr, SPILL_ROWS), 1).astype(F32) + float(tr)
        for tb in range(TOK_BLOCKS):
            ts = slice(tb * tr, (tb + 1) * tr)
            rows = acc_ref[pl.ds(win[tb], tr), :].astype(BF16)
            pos_col = jnp.sum(jnp.where(lane == e, pos_ref[ts, :], 0.0), axis=-1, keepdims=True)
            gate_col = jnp.sum(jnp.where(lane == e, gates_ref[ts, :], 0.0), axis=-1, keepdims=True)
            local = pos_col - win[tb].astype(F32)
            sel = jnp.where(local == slot, 1.0, 0.0).astype(BF16)
            out_ref[ts, :] += gate_col * _dot(sel, rows)

            @pl.when(spills[tb])
            def _():
                rows2 = acc_ref[pl.ds(win[tb] + tr, SPILL_ROWS), :].astype(BF16)
                sel2 = jnp.where(local == slot_spill, 1.0, 0.0).astype(BF16)
                out_ref[ts, :] += gate_col * _dot(sel2, rows2)

    @pl.when((e == N_EXPERTS - 1) & (f == last_f))
    def _():
        out_ref[...] = x_ref[...] + mod_ref[5:6, :] * out_ref[...]


def _cast_kernel(w_ref, o_ref):
    o_ref[...] = w_ref[...].astype(BF16)


def _chunk_columns(w, tf):
    n_e, k, f = w.shape
    return pl.pallas_call(
        _cast_kernel,
        out_shape=jax.ShapeDtypeStruct((n_e, f // tf, k, tf), BF16),
        grid=(n_e, f // tf),
        in_specs=[pl.BlockSpec((None, k, tf), lambda e, c: (e, 0, c))],
        out_specs=pl.BlockSpec((None, None, k, tf), lambda e, c: (e, c, 0, 0)),
        compiler_params=_cparams(("arbitrary", "arbitrary")),
        name="chunk_expert_weights",
    )(w)


def _moe(x, mod, h, post, pos, gates, table, w1, w3, w2):
    tokens = h.shape[0]
    batch = tokens // SEQ
    d_ff = w1.shape[2]
    tf = TF_MOE
    n_f = d_ff // tf
    once = pl.Buffered(1)
    max_rows = SEQ + 2 * TR_MOE
    grid_spec = pltpu.PrefetchScalarGridSpec(
        num_scalar_prefetch=1,
        grid=(batch, N_EXPERTS, n_f),
        in_specs=[
            pl.BlockSpec((SEQ, D_MODEL), lambda b, e, f, t: (b, 0), pipeline_mode=once),
            pl.BlockSpec((N_EXPERTS, SEQ), lambda b, e, f, t: (0, b), pipeline_mode=once),
            pl.BlockSpec((SEQ, LANES), lambda b, e, f, t: (b, 0), pipeline_mode=once),
            pl.BlockSpec((SEQ, LANES), lambda b, e, f, t: (b, 0), pipeline_mode=once),
            pl.BlockSpec((SEQ, D_MODEL), lambda b, e, f, t: (b, 0), pipeline_mode=once),
            pl.BlockSpec((None, 6, D_MODEL), lambda b, e, f, t: (b, 0, 0)),
            pl.BlockSpec((None, None, D_MODEL, tf), lambda b, e, f, t: (e, f, 0, 0)),
            pl.BlockSpec((None, None, D_MODEL, tf), lambda b, e, f, t: (e, f, 0, 0)),
            pl.BlockSpec((None, tf, D_MODEL), lambda b, e, f, t: (e, f, 0)),
        ],
        out_specs=pl.BlockSpec((SEQ, D_MODEL), lambda b, e, f, t: (b, 0), pipeline_mode=once),
        scratch_shapes=[pltpu.VMEM((max_rows, D_MODEL), BF16), pltpu.VMEM((max_rows, D_MODEL), F32)],
    )
    return pl.pallas_call(
        _moe_kernel,
        out_shape=jax.ShapeDtypeStruct((tokens, D_MODEL), F32),
        grid_spec=grid_spec,
        compiler_params=_cparams(("arbitrary", "arbitrary", "arbitrary")),
        name="expert_swiglu",
    )(table, h, post, pos, gates, x, mod, _chunk_columns(w1, tf), _chunk_columns(w3, tf), w2.astype(BF16))


def _token_mixer(x, mod, gain, w_in, q_gain, k_gain, ret_gain, w_out, bias_masks):
    *qkv, ret_in = _in_proj(x, mod, gain, w_in, q_gain, k_gain)
    attn = _dilated_attention(qkv, bias_masks)
    ret = _retention(ret_in, ret_gain)
    return _out_proj(attn, ret, x, mod, w_out)


def _moe_ffn(x, mod, gain, w_router, w1, w3, w2):
    h, gates, pos, post, start, cnt = _router(x, mod, gain, w_router)
    batch = cnt.shape[0]
    table = jnp.concatenate([start[:, 0, :N_EXPERTS].reshape(batch, TOK_BLOCKS, N_EXPERTS),
                             cnt[:, :1, :N_EXPERTS]], axis=1).astype(jnp.int32).reshape(-1)
    return _moe(x, mod, h, post, pos, gates, table, w1, w3, w2)


def kernel(x, c, rel_bias_table, norm_mix, norm_ffn, w_mod, b_mod, w_in, q_gain, k_gain, ret_gain, w_out,
           ffn_w1, ffn_w3, ffn_w2, moe_router, moe_w1, moe_w3, moe_w2):
    batch, seq, d_model = x.shape
    assert (seq, d_model) == (SEQ, D_MODEL)
    depth = w_mod.shape[0]
    mods = _modulation(c, w_mod, b_mod).reshape(depth, batch, 6, D_MODEL)
    bias_masks = _bias_masks(rel_bias_table)
    xt = x.reshape(batch * seq, d_model)
    for layer in range(depth):
        mod = mods[layer]
        xt = _token_mixer(xt, mod, norm_mix[layer], w_in[layer], q_gain[layer], k_gain[layer],
                          ret_gain[layer], w_out[layer], bias_masks)
        i = layer // 2
        if layer % 2 == 0:
            xt = _dense_ffn(xt, mod, norm_ffn[layer], ffn_w1[i], ffn_w3[i], ffn_w2[i])
        else:
            xt = _moe_ffn(xt, mod, norm_ffn[layer], moe_router[i], moe_w1[i], moe_w3[i], moe_w2[i])
    return xt.reshape(batch, seq, d_model)
```

```python
import functools
import math

import jax
import jax.numpy as jnp
import numpy as np
from jax import lax
from jax.experimental import pallas as pl
from jax.experimental.pallas import tpu as pltpu

D_MODEL = 1024
SEQ = 2048
ATTN_HEADS = 8
ATTN_HEAD_DIM = 64
ATTN_WIDTH = ATTN_HEADS * ATTN_HEAD_DIM
DILATED_PATTERNS = ((128, 1), (512, 4), (2048, 16))
BLOCK = 128
NUM_BUCKETS = 32
MAX_DISTANCE = 2048
RET_HEADS = 4
RET_KEY_DIM = 64
RET_VALUE_DIM = 128
RET_WIDTH = RET_HEADS * RET_VALUE_DIM
RET_QK_WIDTH = RET_HEADS * RET_KEY_DIM
RET_CHUNK = 128
ROPE_BASE = 10000.0
IN_WIDTH = 3 * ATTN_WIDTH + 2 * RET_QK_WIDTH + 2 * RET_WIDTH
RET_IN_WIDTH = IN_WIDTH - 3 * ATTN_WIDTH
N_EXPERTS = 8
EPS = 1e-6
NEG_INF = -1e30

LANES = 128
VMEM_LIMIT = 56 * 1024 * 1024

BF16 = jnp.bfloat16
F32 = jnp.float32

TM_PROJ = 256
TM_FFN = 512
TF_FFN = 1408
TR_MOE = 256
TF_MOE = 896
assert TM_PROJ == TR_MOE
RET_ROWS = 512
SUBLANES = 8


def _cparams(sem):
    return pltpu.CompilerParams(dimension_semantics=sem, vmem_limit_bytes=VMEM_LIMIT)


def _dot(a, b):
    return jnp.dot(a, b, preferred_element_type=F32)


def _dot_nt(a, b):
    return lax.dot_general(a, b, (((1,), (1,)), ((), ())), preferred_element_type=F32)


def _dot_tn(a, b):
    return lax.dot_general(a, b, (((0,), (0,)), ((), ())), preferred_element_type=F32)


def _split_bf16(v):
    hi = v.astype(BF16)
    lo = (v - hi.astype(F32)).astype(BF16)
    return hi, lo


def _silu(v):
    return v * (1.0 / (1.0 + jnp.exp(-v)))


def _modulated_norm(x, gain, scale, shift):
    ms = jnp.mean(x * x, axis=-1, keepdims=True)
    y = x * lax.rsqrt(ms + EPS) * gain
    return y * (1.0 + scale) + shift


def _mod_kernel(c_ref, w_ref, b_ref, o_ref):
    ca = _silu(c_ref[...]).astype(BF16)
    o_ref[...] = _dot(ca, w_ref[...].astype(BF16)) + b_ref[...]


def _modulation(c, w_mod, b_mod):
    depth, _, width = w_mod.shape
    batch = c.shape[0]
    tn = 1536
    return pl.pallas_call(
        _mod_kernel,
        out_shape=jax.ShapeDtypeStruct((depth, batch, width), F32),
        grid=(depth, width // tn),
        in_specs=[
            pl.BlockSpec((batch, D_MODEL), lambda l, n: (0, 0)),
            pl.BlockSpec((None, D_MODEL, tn), lambda l, n: (l, 0, n)),
            pl.BlockSpec((None, 1, tn), lambda l, n: (l, 0, n)),
        ],
        out_specs=pl.BlockSpec((None, batch, tn), lambda l, n: (l, 0, n)),
        compiler_params=_cparams(("arbitrary", "arbitrary")),
        name="adaln_modulation",
    )(c, w_mod, b_mod.reshape(depth, 1, width))


def _bias_kernel(table_ref, bucket_ref, o_ref):
    h = pl.program_id(1)
    bucket = bucket_ref[...]
    acc = jnp.full(bucket.shape, NEG_INF, F32)
    for b in range(NUM_BUCKETS):
        acc = jnp.where(bucket == b, table_ref[b, h], acc)
    o_ref[...] = acc


BIAS_FULL = {1: 0, 4: 2}
BIAS_FIRST = {1: 1, 4: 3, 16: 4}
N_BIAS_SETS = 5


def _bias_masks(rel_bias_table):
    i = jnp.arange(BLOCK)[:, None]
    j = jnp.arange(2 * BLOCK)[None, :]
    max_exact = NUM_BUCKETS // 2

    def bucket_of(rel, dilation, w_sub, exists):
        n = jnp.maximum(rel * dilation, 0)
        nf = jnp.maximum(n.astype(F32), float(max_exact))
        large = max_exact + (jnp.log(nf / max_exact) / math.log(MAX_DISTANCE / max_exact)
                             * (NUM_BUCKETS - max_exact)).astype(jnp.int32)
        large = jnp.minimum(large, NUM_BUCKETS - 1)
        bucket = jnp.where(n < max_exact, n, large)
        allowed = (rel >= 0) & (rel <= w_sub) & exists
        return jnp.where(allowed, bucket, -1)

    sets = [None] * N_BIAS_SETS
    for window, dilation in DILATED_PATTERNS:
        w_sub = window // dilation
        if dilation in BIAS_FULL:
            sets[BIAS_FULL[dilation]] = bucket_of(i - j + BLOCK, dilation, w_sub, j >= 0)
        sets[BIAS_FIRST[dilation]] = bucket_of(i - j, dilation, w_sub, j < BLOCK)
    buckets = jnp.stack(sets).astype(jnp.int32)
    return pl.pallas_call(
        _bias_kernel,
        out_shape=jax.ShapeDtypeStruct((N_BIAS_SETS, ATTN_HEADS // 2, 2 * BLOCK, 2 * BLOCK), F32),
        grid=(N_BIAS_SETS, ATTN_HEADS),
        in_specs=[
            pl.BlockSpec(memory_space=pltpu.SMEM),
            pl.BlockSpec((None, BLOCK, 2 * BLOCK), lambda p, h: (p, 0, 0)),
        ],
        out_specs=pl.BlockSpec((None, None, BLOCK, 2 * BLOCK), lambda p, h: (p, h // 2, h % 2, 0)),
        compiler_params=_cparams(("arbitrary", "arbitrary")),
        name="relative_bias_masks",
    )(rel_bias_table, buckets)


HALF = ATTN_WIDTH // 2


def _in_proj_kernel(x_ref, mod_ref, gain_ref, w_ref, qg_ref, kg_ref, grp_ref,
                    q1_ref, k1_ref, v1_ref, q4_ref, k4_ref, v4_ref, q16_ref, k16_ref, v16_ref, r_ref,
                    perm_ref):
    mod = mod_ref[...]
    h = _modulated_norm(x_ref[...], gain_ref[...], mod[1:2], mod[0:1]).astype(BF16)
    proj = _dot(h, w_ref[...])
    grp = grp_ref[...]
    tm = proj.shape[0]

    def head_norm(t, gain):
        hi, lo = _split_bf16(t * t)
        ss = _dot(hi, grp) + _dot(lo, grp)
        return t * lax.rsqrt(ss * (1.0 / ATTN_HEAD_DIM) + EPS) * gain

    def emit(t, o1_ref, o4_ref, o16_ref):
        o1_ref[...] = t.astype(BF16)
        for j in range(ATTN_WIDTH // LANES):
            perm_ref[j] = t[:, j * LANES:(j + 1) * LANES]
        for dil, o_ref in ((4, o4_ref), (16, o16_ref)):
            for hf in range(2):
                for r in range(dil):
                    for jj in range(HALF // LANES):
                        c0 = (hf * dil + r) * HALF + jj * LANES
                        o_ref[:, c0:c0 + LANES] = perm_ref[hf * (HALF // LANES) + jj,
                                                           pl.ds(r, tm // dil, stride=dil), :].astype(BF16)

    emit(head_norm(proj[:, :ATTN_WIDTH], qg_ref[...]) * (ATTN_HEAD_DIM ** -0.5), q1_ref, q4_ref, q16_ref)
    emit(head_norm(proj[:, ATTN_WIDTH:2 * ATTN_WIDTH], kg_ref[...]), k1_ref, k4_ref, k16_ref)
    emit(proj[:, 2 * ATTN_WIDTH:3 * ATTN_WIDTH], v1_ref, v4_ref, v16_ref)
    r_ref[...] = proj[:, 3 * ATTN_WIDTH:]


def _in_proj(x, mod, gain, w_in, q_gain, k_gain):
    tokens = x.shape[0]
    tm = TM_PROJ
    per_seq = SEQ // tm
    grp = np.kron(np.eye(ATTN_HEADS), np.ones((ATTN_HEAD_DIM, ATTN_HEAD_DIM))).astype(np.float32)
    row = lambda i: (i, 0)
    const = lambda i: (0, 0)
    layouts = []
    for dil in (1, 4, 16):
        shape = jax.ShapeDtypeStruct((tokens // dil, dil * ATTN_WIDTH), BF16)
        spec = pl.BlockSpec((tm // dil, dil * ATTN_WIDTH), row)
        layouts.append(((shape,) * 3, (spec,) * 3))
    out_shape = sum((s for s, _ in layouts), ()) + (jax.ShapeDtypeStruct((tokens, RET_IN_WIDTH), F32),)
    out_specs = sum((s for _, s in layouts), ()) + (pl.BlockSpec((tm, RET_IN_WIDTH), row),)
    return pl.pallas_call(
        _in_proj_kernel,
        out_shape=out_shape,
        grid=(tokens // tm,),
        in_specs=[
            pl.BlockSpec((tm, D_MODEL), row),
            pl.BlockSpec((None, 6, D_MODEL), lambda i: (i // per_seq, 0, 0)),
            pl.BlockSpec((1, D_MODEL), const),
            pl.BlockSpec((D_MODEL, IN_WIDTH), const),
            pl.BlockSpec((1, ATTN_WIDTH), const),
            pl.BlockSpec((1, ATTN_WIDTH), const),
            pl.BlockSpec((ATTN_WIDTH, ATTN_WIDTH), const),
        ],
        out_specs=out_specs,
        scratch_shapes=[pltpu.VMEM((ATTN_WIDTH // LANES, tm, LANES), F32)],
        compiler_params=_cparams(("arbitrary",)),
        name="in_projection",
    )(x, mod, gain.reshape(1, D_MODEL), w_in.astype(BF16),
      jnp.tile(q_gain, ATTN_HEADS).reshape(1, ATTN_WIDTH),
      jnp.tile(k_gain, ATTN_HEADS).reshape(1, ATTN_WIDTH),
      jnp.asarray(grp, BF16))


PAIRS_PER_HALF = ATTN_HEADS // 4
GROUP = 4


def _pair_scores(qp, kp, vp, bias2, masks, low):
    q2 = jnp.concatenate([qp * masks[0], qp * masks[1]], axis=0)
    s = _dot_nt(q2, kp) + bias2
    m = jnp.max(s, axis=-1, keepdims=True)
    p = jnp.exp(s - m)
    den = jnp.sum(p, axis=-1, keepdims=True)
    pv = _dot(p.astype(BF16), vp)
    pick = lambda t: jnp.where(low, t[:BLOCK], t[BLOCK:])
    return pick(pv), pick(m), pick(den)


def _attn_kernel(q1_ref, k1_ref, v1_ref, q4_ref, k4_ref, v4_ref, q16_ref, k16_ref, v16_ref, bm_ref,
                 o_ref, acc_ref, max_ref, den_ref):
    lane = lax.broadcasted_iota(jnp.int32, (BLOCK, LANES), 1)
    low = lane < ATTN_HEAD_DIM
    masks = (jnp.where(low, 1.0, 0.0).astype(BF16), jnp.where(low, 0.0, 1.0).astype(BF16))

    def block(q_ref, k_ref, v_ref, c0, q0, w0, width, bias_set, rows, first):
        for p in range(PAIRS_PER_HALF):
            cs = slice(c0 + p * LANES, c0 + (p + 1) * LANES)
            acc, m, den = _pair_scores(q_ref[pl.ds(q0, BLOCK), cs], k_ref[pl.ds(w0, width), cs],
                                       v_ref[pl.ds(w0, width), cs], bm_ref[bias_set, p, :, 0:width],
                                       masks, low)
            if not first:
                m_old = max_ref[p, rows, :]
                m_new = jnp.maximum(m_old, m)
                a, b = jnp.exp(m_old - m_new), jnp.exp(m - m_new)
                den = den_ref[p, rows, :] * a + den * b
                acc = acc_ref[p, rows, :] * a + acc * b
                m = m_new
            max_ref[p, rows, :] = m
            den_ref[p, rows, :] = den
            acc_ref[p, rows, :] = acc

    def d1_group(g, carry):
        for u in range(GROUP):
            n = g * GROUP + u
            q0 = pl.multiple_of(n * BLOCK, BLOCK)
            w0 = pl.multiple_of(jnp.maximum(n - 1, 0) * BLOCK, BLOCK)
            bias_set = jnp.where(n == 0, BIAS_FIRST[1], BIAS_FULL[1])
            block(q1_ref, k1_ref, v1_ref, 0, q0, w0, 2 * BLOCK, bias_set, pl.ds(q0, BLOCK), True)
        return carry
    lax.fori_loop(0, SEQ // BLOCK // GROUP, d1_group, 0)

    for r in range(4):
        for n in range(SEQ // 4 // BLOCK):
            block(q4_ref, k4_ref, v4_ref, r * HALF, n * BLOCK, max(n - 1, 0) * BLOCK, 2 * BLOCK,
                  BIAS_FIRST[4] if n == 0 else BIAS_FULL[4],
                  pl.ds(r + 4 * BLOCK * n, BLOCK, stride=4), False)

    for r in range(16):
        block(q16_ref, k16_ref, v16_ref, r * HALF, 0, 0, BLOCK, BIAS_FIRST[16],
              pl.ds(r, BLOCK, stride=16), False)

    for n in range(SEQ // BLOCK):
        rows = slice(n * BLOCK, (n + 1) * BLOCK)
        for p in range(PAIRS_PER_HALF):
            o_ref[rows, p * LANES:(p + 1) * LANES] = (acc_ref[p, rows, :] / den_ref[p, rows, :]).astype(BF16)


def _dilated_attention(qkv, bias_masks):
    tokens = qkv[0].shape[0]
    batch = tokens // SEQ
    specs = []
    for dil in (1, 4, 16):
        specs += [pl.BlockSpec((SEQ // dil, dil * HALF), lambda b, hf: (b, hf))] * 3
    state = pltpu.VMEM((PAIRS_PER_HALF, SEQ, LANES), F32)
    return pl.pallas_call(
        _attn_kernel,
        out_shape=jax.ShapeDtypeStruct((tokens, ATTN_WIDTH), BF16),
        grid=(batch, 2),
        in_specs=specs + [pl.BlockSpec((N_BIAS_SETS, PAIRS_PER_HALF, 2 * BLOCK, 2 * BLOCK),
                                       lambda b, hf: (0, hf, 0, 0))],
        out_specs=pl.BlockSpec((SEQ, HALF), lambda b, hf: (b, hf)),
        scratch_shapes=[state, state, state],
        compiler_params=_cparams(("arbitrary", "arbitrary")),
        name="dilated_attention",
    )(*qkv, bias_masks)


def _retention_kernel(r_ref, cos_ref, sin_ref, dmask_ref, qdec_ref, kdec_ref, cdec_ref, gain_ref,
                      o_ref, state_ref):
    @pl.when(pl.program_id(1) == 0)
    def _():
        state_ref[...] = jnp.zeros_like(state_ref)

    lane = lax.broadcasted_iota(jnp.int32, (RET_CHUNK, LANES), 1)
    low = lane < RET_KEY_DIM
    first_half = (lane % RET_KEY_DIM) < (RET_KEY_DIM // 2)

    def rotate(t, cos, sin):
        partner = jnp.where(first_half, pltpu.roll(t, LANES - RET_KEY_DIM // 2, 1),
                            pltpu.roll(t, RET_KEY_DIM // 2, 1))
        return t * cos + partner * sin

    for c in range(RET_ROWS // RET_CHUNK):
        rows = slice(c * RET_CHUNK, (c + 1) * RET_CHUNK)
        for hp in range(RET_HEADS // 2):
            qs = slice(hp * LANES, (hp + 1) * LANES)
            ks = slice(RET_QK_WIDTH + hp * LANES, RET_QK_WIDTH + (hp + 1) * LANES)
            cos, sin = cos_ref[rows, qs], sin_ref[rows, qs]
            q_pair = rotate(r_ref[rows, qs], cos, sin) * (RET_KEY_DIM ** -0.5)
            k_pair = rotate(r_ref[rows, ks], cos, sin)
            for hh in range(2):
                head = 2 * hp + hh
                vs = slice(2 * RET_QK_WIDTH + head * LANES, 2 * RET_QK_WIDTH + (head + 1) * LANES)
                gs = slice(2 * RET_QK_WIDTH + RET_WIDTH + head * LANES,
                           2 * RET_QK_WIDTH + RET_WIDTH + (head + 1) * LANES)
                keep = low if hh == 0 else jnp.logical_not(low)
                qm = jnp.where(keep, q_pair, 0.0)
                vb = r_ref[rows, vs].astype(BF16)
                state = state_ref[head]
                inner = _dot_nt(qm.astype(BF16), k_pair.astype(BF16)) * dmask_ref[head]
                y = _dot(inner.astype(BF16), vb)
                y = y + _dot((qm * qdec_ref[head]).astype(BF16), state.astype(BF16))
                state_ref[head] = state * cdec_ref[head] + _dot_tn((k_pair * kdec_ref[head]).astype(BF16), vb)
                mu = jnp.mean(y, axis=-1, keepdims=True)
                yc = y - mu
                var = jnp.mean(yc * yc, axis=-1, keepdims=True)
                yn = yc * lax.rsqrt(var + EPS) * gain_ref[:, head * LANES:(head + 1) * LANES]
                o_ref[rows, head * LANES:(head + 1) * LANES] = (_silu(r_ref[rows, gs]) * yn).astype(BF16)


def _retention_tables():
    half = RET_KEY_DIM // 2
    pos = jnp.arange(SEQ, dtype=F32)
    inv = ROPE_BASE ** (-jnp.arange(half, dtype=F32) / half)
    ang = pos[:, None] * inv[None, :]
    cos, sin = jnp.cos(ang), jnp.sin(ang)
    cos_full = jnp.tile(jnp.concatenate([cos, cos], axis=-1), (1, RET_HEADS))
    sin_signed = jnp.tile(jnp.concatenate([-sin, sin], axis=-1), (1, RET_HEADS))
    log_g = jnp.log(1.0 - 2.0 ** (-5.0 - jnp.arange(RET_HEADS, dtype=F32)))
    idx = jnp.arange(RET_CHUNK, dtype=F32)
    diff = idx[:, None] - idx[None, :]
    dmask = jnp.where(diff >= 0, jnp.exp(jnp.maximum(diff, 0.0)[None] * log_g[:, None, None]), 0.0)
    q_decay = jnp.exp((idx + 1.0)[None, :] * log_g[:, None])[..., None]
    k_decay = jnp.exp((RET_CHUNK - 1.0 - idx)[None, :] * log_g[:, None])[..., None]
    chunk_decay = jnp.exp(RET_CHUNK * log_g)[:, None, None]
    full = (RET_HEADS, RET_CHUNK, LANES)
    return (cos_full, sin_signed, dmask, jnp.broadcast_to(q_decay, full),
            jnp.broadcast_to(k_decay, full), jnp.broadcast_to(chunk_decay, full))


def _retention(ret_in, ret_gain):
    tokens = ret_in.shape[0]
    batch = tokens // SEQ
    per_seq = SEQ // RET_ROWS
    cos, sin, dmask, qdec, kdec, cdec = _retention_tables()
    tab = pl.BlockSpec((RET_ROWS, RET_QK_WIDTH), lambda b, j: (j, 0))
    const3 = pl.BlockSpec((RET_HEADS, RET_CHUNK, LANES), lambda b, j: (0, 0, 0))
    return pl.pallas_call(
        _retention_kernel,
        out_shape=jax.ShapeDtypeStruct((tokens, RET_WIDTH), BF16),
        grid=(batch, per_seq),
        in_specs=[
            pl.BlockSpec((RET_ROWS, RET_IN_WIDTH), lambda b, j: (b * per_seq + j, 0)),
            tab, tab, const3, const3, const3, const3,
            pl.BlockSpec((1, RET_WIDTH), lambda b, j: (0, 0)),
        ],
        out_specs=pl.BlockSpec((RET_ROWS, RET_WIDTH), lambda b, j: (b * per_seq + j, 0)),
        scratch_shapes=[pltpu.VMEM((RET_HEADS, LANES, RET_VALUE_DIM), F32)],
        compiler_params=_cparams(("arbitrary", "arbitrary")),
        name="retention",
    )(ret_in, cos, sin, dmask, qdec, kdec, cdec, ret_gain.reshape(1, RET_WIDTH))


def _out_proj_kernel(attn_ref, ret_ref, x_ref, mod_ref, w_ref, out_ref):
    mix = _dot(attn_ref[...], w_ref[:ATTN_WIDTH, :]) + _dot(ret_ref[...], w_ref[ATTN_WIDTH:, :])
    out_ref[...] = x_ref[...] + mod_ref[2:3, :] * mix


def _out_proj(attn, ret, x, mod, w_out):
    tokens = x.shape[0]
    tm = TM_FFN
    per_seq = SEQ // tm
    row = lambda i: (i, 0)
    return pl.pallas_call(
        _out_proj_kernel,
        out_shape=jax.ShapeDtypeStruct((tokens, D_MODEL), F32),
        grid=(tokens // tm,),
        in_specs=[pl.BlockSpec((tm, ATTN_WIDTH), row),
                  pl.BlockSpec((tm, RET_WIDTH), row),
                  pl.BlockSpec((tm, D_MODEL), row),
                  pl.BlockSpec((None, 6, D_MODEL), lambda i: (i // per_seq, 0, 0)),
                  pl.BlockSpec((D_MODEL, D_MODEL), lambda i: (0, 0))],
        out_specs=pl.BlockSpec((tm, D_MODEL), row),
        compiler_params=_cparams(("arbitrary",)),
        name="out_projection",
    )(attn, ret, x, mod, w_out.astype(BF16))


def _ffn_kernel(x_ref, mod_ref, gain_ref, w1_ref, w3_ref, w2_ref, out_ref, h_ref, acc_ref):
    f = pl.program_id(1)

    @pl.when(f == 0)
    def _():
        mod = mod_ref[...]
        h_ref[...] = _modulated_norm(x_ref[...], gain_ref[...], mod[4:5], mod[3:4]).astype(BF16)
        acc_ref[...] = jnp.zeros_like(acc_ref)

    h = h_ref[...]
    z = (_silu(_dot(h, w1_ref[...])) * _dot(h, w3_ref[...])).astype(BF16)
    acc_ref[...] += _dot(z, w2_ref[...])

    @pl.when(f == pl.num_programs(1) - 1)
    def _():
        out_ref[...] = x_ref[...] + mod_ref[5:6, :] * acc_ref[...]


def _dense_ffn(x, mod, gain, w1, w3, w2):
    tokens = x.shape[0]
    d_ff = w1.shape[1]
    tm, tf = TM_FFN, TF_FFN
    per_seq = SEQ // tm
    return pl.pallas_call(
        _ffn_kernel,
        out_shape=jax.ShapeDtypeStruct((tokens, D_MODEL), F32),
        grid=(tokens // tm, d_ff // tf),
        in_specs=[
            pl.BlockSpec((tm, D_MODEL), lambda i, f: (i, 0)),
            pl.BlockSpec((None, 6, D_MODEL), lambda i, f: (i // per_seq, 0, 0)),
            pl.BlockSpec((1, D_MODEL), lambda i, f: (0, 0)),
            pl.BlockSpec((D_MODEL, tf), lambda i, f: (0, f)),
            pl.BlockSpec((D_MODEL, tf), lambda i, f: (0, f)),
            pl.BlockSpec((tf, D_MODEL), lambda i, f: (f, 0)),
        ],
        out_specs=pl.BlockSpec((tm, D_MODEL), lambda i, f: (i, 0)),
        scratch_shapes=[pltpu.VMEM((tm, D_MODEL), BF16), pltpu.VMEM((tm, D_MODEL), F32)],
        compiler_params=_cparams(("arbitrary", "arbitrary")),
        name="dense_swiglu",
    )(x, mod, gain.reshape(1, D_MODEL), w1.astype(BF16), w3.astype(BF16), w2.astype(BF16))


def _router_kernel(x_ref, mod_ref, gain_ref, wr_ref, h_ref, gates_ref, pos_ref, post_ref, start_ref,
                   cnt_ref, carry_ref, *, tiles_per_seq):
    i = pl.program_id(0)

    @pl.when(i % tiles_per_seq == 0)
    def _():
        carry_ref[...] = jnp.zeros_like(carry_ref)

    mod = mod_ref[...]
    h = _modulated_norm(x_ref[...], gain_ref[...], mod[4:5], mod[3:4]).astype(BF16)
    h_ref[...] = h
    tm = h.shape[0]
    lane = lax.broadcasted_iota(jnp.int32, (tm, LANES), 1).astype(F32)
    logits = jnp.where(lane < N_EXPERTS, _dot(h, wr_ref[...]), -jnp.inf)
    m1 = jnp.max(logits, axis=-1, keepdims=True)
    i1 = jnp.min(jnp.where(logits == m1, lane, float(LANES)), axis=-1, keepdims=True)
    rest = jnp.where(lane == i1, -jnp.inf, logits)
    m2 = jnp.max(rest, axis=-1, keepdims=True)
    i2 = jnp.min(jnp.where(rest == m2, lane, float(LANES)), axis=-1, keepdims=True)
    e2 = jnp.exp(m2 - m1)
    g1 = 1.0 / (1.0 + e2)
    g2 = e2 / (1.0 + e2)
    gates_ref[...] = jnp.where(lane == i1, g1, 0.0) + jnp.where(lane == i2, g2, 0.0)
    chosen = (lane == i1) | (lane == i2)
    onehot = jnp.where(chosen, 1.0, 0.0)
    r = lax.broadcasted_iota(jnp.int32, (tm, tm), 0)
    c = lax.broadcasted_iota(jnp.int32, (tm, tm), 1)
    tril = jnp.where(c <= r, 1.0, 0.0).astype(BF16)
    incl = _dot(tril, onehot.astype(BF16))
    carry = carry_ref[0:1, :]
    start_ref[...] = carry_ref[...]
    pos = jnp.where(chosen, incl - 1.0 + carry, -1.0)
    pos_ref[...] = pos
    post_ref[...] = pos.T[:N_EXPERTS, :]
    total = carry + incl[tm - 1:tm, :]
    carry_ref[...] = jnp.broadcast_to(total, carry_ref.shape)
    cnt_ref[...] = jnp.broadcast_to(total, cnt_ref.shape)


def _router(x, mod, gain, w_router):
    tokens = x.shape[0]
    batch = tokens // SEQ
    tm = TM_PROJ
    per_seq = SEQ // tm
    wr = jnp.zeros((D_MODEL, LANES), BF16).at[:, :N_EXPERTS].set(w_router.astype(BF16))
    row = lambda i: (i, 0)
    return pl.pallas_call(
        functools.partial(_router_kernel, tiles_per_seq=per_seq),
        out_shape=(
            jax.ShapeDtypeStruct((tokens, D_MODEL), BF16),
            jax.ShapeDtypeStruct((tokens, LANES), F32),
            jax.ShapeDtypeStruct((tokens, LANES), F32),
            jax.ShapeDtypeStruct((N_EXPERTS, tokens), F32),
            jax.ShapeDtypeStruct((tokens // tm, 8, LANES), F32),
            jax.ShapeDtypeStruct((batch, 8, LANES), F32),
        ),
        grid=(tokens // tm,),
        in_specs=[
            pl.BlockSpec((tm, D_MODEL), row),
            pl.BlockSpec((None, 6, D_MODEL), lambda i: (i // per_seq, 0, 0)),
            pl.BlockSpec((1, D_MODEL), lambda i: (0, 0)),
            pl.BlockSpec((D_MODEL, LANES), lambda i: (0, 0)),
        ],
        out_specs=(
            pl.BlockSpec((tm, D_MODEL), row),
            pl.BlockSpec((tm, LANES), row),
            pl.BlockSpec((tm, LANES), row),
            pl.BlockSpec((N_EXPERTS, tm), lambda i: (0, i)),
            pl.BlockSpec((None, 8, LANES), lambda i: (i, 0, 0)),
            pl.BlockSpec((None, 8, LANES), lambda i: (i // per_seq, 0, 0)),
        ),
        scratch_shapes=[pltpu.VMEM((8, LANES), F32)],
        compiler_params=_cparams(("arbitrary",)),
        name="expert_router",
    )(x, mod, gain.reshape(1, D_MODEL), wr)


TOK_BLOCKS = SEQ // TR_MOE
TAIL_TILES = (64, 128, TR_MOE)
SPILL_ROWS = 16


def _moe_kernel(tbl_ref, h_ref, post_ref, pos_ref, gates_ref, x_ref, mod_ref, w1_ref, w3_ref, w2_ref,
                out_ref, hs_ref, acc_ref):
    b, e, f = pl.program_id(0), pl.program_id(1), pl.program_id(2)
    last_f = pl.num_programs(2) - 1
    tr = TR_MOE
    base = b * (TOK_BLOCKS + 1) * N_EXPERTS + e
    bounds = [tbl_ref[base + tb * N_EXPERTS] for tb in range(TOK_BLOCKS + 1)]
    win = [pl.multiple_of(lax.shift_left(lax.shift_right_logical(s, 3), 3), SUBLANES) for s in bounds[:-1]]
    spills = [bounds[tb + 1] - win[tb] > tr for tb in range(TOK_BLOCKS)]
    total = bounds[-1]
    n_tiles = lax.shift_right_logical(total + (tr - 1), tr.bit_length() - 1)

    def clear(r, carry):
        acc_ref[pl.ds(pl.multiple_of(r * tr, tr), tr), :] = jnp.zeros((tr, D_MODEL), F32)
        return carry

    @pl.when((e == 0) & (f == 0))
    def _():
        out_ref[...] = jnp.zeros_like(out_ref)

    @pl.when(f == 0)
    def _():
        lax.fori_loop(0, n_tiles + 2, clear, 0)
        slot = lax.broadcasted_iota(jnp.int32, (tr + SPILL_ROWS, tr), 0).astype(F32)
        for tb in range(TOK_BLOCKS):
            ts = slice(tb * tr, (tb + 1) * tr)
            local = post_ref[pl.ds(e, 1), ts] - win[tb].astype(F32)
            sel = jnp.where(local == slot, 1.0, 0.0).astype(BF16)
            acc_ref[pl.ds(win[tb], tr + SPILL_ROWS), :] += _dot(sel, h_ref[ts, :])

        def to_rows(r, carry):
            rows = pl.ds(pl.multiple_of(r * tr, tr), tr)
            hs_ref[rows, :] = acc_ref[rows, :].astype(BF16)
            acc_ref[rows, :] = jnp.zeros((tr, D_MODEL), F32)
            return carry
        lax.fori_loop(0, n_tiles + 2, to_rows, 0)

    def swiglu(r0, m):
        rows = pl.ds(r0, m)
        hr = hs_ref[rows, :]
        z = (_silu(_dot(hr, w1_ref[...])) * _dot(hr, w3_ref[...])).astype(BF16)
        acc_ref[rows, :] += _dot(z, w2_ref[...])

    def full_tile(r, carry):
        swiglu(pl.multiple_of(r * tr, tr), tr)
        return carry
    n_full = lax.shift_right_logical(total, tr.bit_length() - 1)
    lax.fori_loop(0, n_full, full_tile, 0)
    rest = total - n_full * tr
    tail0 = pl.multiple_of(n_full * tr, tr)
    lo = 0
    for m in TAIL_TILES:
        @pl.when((rest > lo) & (rest <= m))
        def _():
            swiglu(tail0, m)
        lo = m

    @pl.when(f == last_f)
    def _():
        lane = lax.broadcasted_iota(jnp.int32, (tr, LANES), 1)
        slot = lax.broadcasted_iota(jnp.int32, (tr, tr), 1).astype(F32)
        slot_spill = lax.broadcasted_iota(jnp.int32, (tr, SPILL_ROWS), 1).astype(F32) + float(tr)
        def columns(tb):
            ts = slice(tb * tr, (tb + 1) * tr)
            pos_col = jnp.sum(jnp.where(lane == e, pos_ref[ts, :], 0.0), axis=-1, keepdims=True)
            gate_col = jnp.sum(jnp.where(lane == e, gates_ref[ts, :], 0.0), axis=-1, keepdims=True)
            return ts, pos_col - win[tb].astype(F32), gate_col

        for tb in range(TOK_BLOCKS):
            ts, local, gate_col = columns(tb)
            rows = acc_ref[pl.ds(win[tb], tr), :].astype(BF16)
            sel = jnp.where(local == slot, 1.0, 0.0).astype(BF16)
            out_ref[ts, :] += gate_col * _dot(sel, rows)

        for tb in range(TOK_BLOCKS):
            @pl.when(spills[tb])
            def _():
                ts, local, gate_col = columns(tb)
                rows2 = acc_ref[pl.ds(win[tb] + tr, SPILL_ROWS), :].astype(BF16)
                sel2 = jnp.where(local == slot_spill, 1.0, 0.0).astype(BF16)
                out_ref[ts, :] += gate_col * _dot(sel2, rows2)

    @pl.when((e == N_EXPERTS - 1) & (f == last_f))
    def _():
        out_ref[...] = x_ref[...] + mod_ref[5:6, :] * out_ref[...]


def _cast_kernel(w_ref, o_ref):
    o_ref[...] = w_ref[...].astype(BF16)


def _chunk_columns(w, tf):
    n_e, k, f = w.shape
    return pl.pallas_call(
        _cast_kernel,
        out_shape=jax.ShapeDtypeStruct((n_e, f // tf, k, tf), BF16),
        grid=(n_e, f // tf),
        in_specs=[pl.BlockSpec((None, k, tf), lambda e, c: (e, 0, c))],
        out_specs=pl.BlockSpec((None, None, k, tf), lambda e, c: (e, c, 0, 0)),
        compiler_params=_cparams(("arbitrary", "arbitrary")),
        name="chunk_expert_weights",
    )(w)


def _moe(x, mod, h, post, pos, gates, table, w1, w3, w2):
    tokens = h.shape[0]
    batch = tokens // SEQ
    d_ff = w1.shape[2]
    tf = TF_MOE
    n_f = d_ff // tf
    once = pl.Buffered(1)
    max_rows = SEQ + 3 * TR_MOE
    grid_spec = pltpu.PrefetchScalarGridSpec(
        num_scalar_prefetch=1,
        grid=(batch, N_EXPERTS, n_f),
        in_specs=[
            pl.BlockSpec((SEQ, D_MODEL), lambda b, e, f, t: (b, 0), pipeline_mode=once),
            pl.BlockSpec((N_EXPERTS, SEQ), lambda b, e, f, t: (0, b), pipeline_mode=once),
            pl.BlockSpec((SEQ, LANES), lambda b, e, f, t: (b, 0), pipeline_mode=once),
            pl.BlockSpec((SEQ, LANES), lambda b, e, f, t: (b, 0), pipeline_mode=once),
            pl.BlockSpec((SEQ, D_MODEL), lambda b, e, f, t: (b, 0), pipeline_mode=once),
            pl.BlockSpec((None, 6, D_MODEL), lambda b, e, f, t: (b, 0, 0)),
            pl.BlockSpec((None, None, D_MODEL, tf), lambda b, e, f, t: (e, f, 0, 0)),
            pl.BlockSpec((None, None, D_MODEL, tf), lambda b, e, f, t: (e, f, 0, 0)),
            pl.BlockSpec((None, tf, D_MODEL), lambda b, e, f, t: (e, f, 0)),
        ],
        out_specs=pl.BlockSpec((SEQ, D_MODEL), lambda b, e, f, t: (b, 0), pipeline_mode=once),
        scratch_shapes=[pltpu.VMEM((max_rows, D_MODEL), BF16), pltpu.VMEM((max_rows, D_MODEL), F32)],
    )
    return pl.pallas_call(
        _moe_kernel,
        out_shape=jax.ShapeDtypeStruct((tokens, D_MODEL), F32),
        grid_spec=grid_spec,
        compiler_params=_cparams(("arbitrary", "arbitrary", "arbitrary")),
        name="expert_swiglu",
    )(table, h, post, pos, gates, x, mod, _chunk_columns(w1, tf), _chunk_columns(w3, tf), w2.astype(BF16))


def _token_mixer(x, mod, gain, w_in, q_gain, k_gain, ret_gain, w_out, bias_masks):
    *qkv, ret_in = _in_proj(x, mod, gain, w_in, q_gain, k_gain)
    attn = _dilated_attention(qkv, bias_masks)
    ret = _retention(ret_in, ret_gain)
    return _out_proj(attn, ret, x, mod, w_out)


def _moe_ffn(x, mod, gain, w_router, w1, w3, w2):
    h, gates, pos, post, start, cnt = _router(x, mod, gain, w_router)
    batch = cnt.shape[0]
    table = jnp.concatenate([start[:, 0, :N_EXPERTS].reshape(batch, TOK_BLOCKS, N_EXPERTS),
                             cnt[:, :1, :N_EXPERTS]], axis=1).astype(jnp.int32).reshape(-1)
    return _moe(x, mod, h, post, pos, gates, table, w1, w3, w2)


def kernel(x, c, rel_bias_table, norm_mix, norm_ffn, w_mod, b_mod, w_in, q_gain, k_gain, ret_gain, w_out,
           ffn_w1, ffn_w3, ffn_w2, moe_router, moe_w1, moe_w3, moe_w2):
    batch, seq, d_model = x.shape
    assert (seq, d_model) == (SEQ, D_MODEL)
    depth = w_mod.shape[0]
    mods = _modulation(c, w_mod, b_mod).reshape(depth, batch, 6, D_MODEL)
    bias_masks = _bias_masks(rel_bias_table)
    xt = x.reshape(batch * seq, d_model)
    for layer in range(depth):
        mod = mods[layer]
        xt = _token_mixer(xt, mod, norm_mix[layer], w_in[layer], q_gain[layer], k_gain[layer],
                          ret_gain[layer], w_out[layer], bias_masks)
        i = layer // 2
        if layer % 2 == 0:
            xt = _dense_ffn(xt, mod, norm_ffn[layer], ffn_w1[i], ffn_w3[i], ffn_w2[i])
        else:
            xt = _moe_ffn(xt, mod, norm_ffn[layer], moe_router[i], moe_w1[i], moe_w3[i], moe_w2[i])
    return xt.reshape(batch, seq, d_model)
```

```python
import functools
import math

import jax
import jax.numpy as jnp
import numpy as np
from jax import lax
from jax.experimental import pallas as pl
from jax.experimental.pallas import tpu as pltpu

D_MODEL = 1024
SEQ = 2048
ATTN_HEADS = 8
ATTN_HEAD_DIM = 64
ATTN_WIDTH = ATTN_HEADS * ATTN_HEAD_DIM
DILATED_PATTERNS = ((128, 1), (512, 4), (2048, 16))
BLOCK = 128
NUM_BUCKETS = 32
MAX_DISTANCE = 2048
RET_HEADS = 4
RET_KEY_DIM = 64
RET_VALUE_DIM = 128
RET_WIDTH = RET_HEADS * RET_VALUE_DIM
RET_QK_WIDTH = RET_HEADS * RET_KEY_DIM
RET_CHUNK = 128
ROPE_BASE = 10000.0
IN_WIDTH = 3 * ATTN_WIDTH + 2 * RET_QK_WIDTH + 2 * RET_WIDTH
RET_IN_WIDTH = IN_WIDTH - 3 * ATTN_WIDTH
N_EXPERTS = 8
EPS = 1e-6
NEG_INF = -1e30

LANES = 128
VMEM_LIMIT = 56 * 1024 * 1024

BF16 = jnp.bfloat16
F32 = jnp.float32

TM_PROJ = 256
TM_FFN = 512
MXU_COLS = 256
TR_MOE = 256
TF_MOE = 896
assert TM_PROJ == TR_MOE
RET_ROWS = 512
SUBLANES = 8


def _cparams(sem):
    return pltpu.CompilerParams(dimension_semantics=sem, vmem_limit_bytes=VMEM_LIMIT)


def _dot(a, b):
    return jnp.dot(a, b, preferred_element_type=F32)


def _dot_nt(a, b):
    return lax.dot_general(a, b, (((1,), (1,)), ((), ())), preferred_element_type=F32)


def _dot_tn(a, b):
    return lax.dot_general(a, b, (((0,), (0,)), ((), ())), preferred_element_type=F32)


def _split_bf16(v):
    hi = v.astype(BF16)
    lo = (v - hi.astype(F32)).astype(BF16)
    return hi, lo


def _silu(v):
    return v * (1.0 / (1.0 + jnp.exp(-v)))


def _modulated_norm(x, gain, scale, shift):
    ms = jnp.mean(x * x, axis=-1, keepdims=True)
    y = x * lax.rsqrt(ms + EPS) * gain
    return y * (1.0 + scale) + shift


def _mod_kernel(c_ref, w_ref, b_ref, o_ref):
    ca = _silu(c_ref[...]).astype(BF16)
    o_ref[...] = _dot(ca, w_ref[...].astype(BF16)) + b_ref[...]


def _modulation(c, w_mod, b_mod):
    depth, _, width = w_mod.shape
    batch = c.shape[0]
    tn = 1536
    return pl.pallas_call(
        _mod_kernel,
        out_shape=jax.ShapeDtypeStruct((depth, batch, width), F32),
        grid=(depth, width // tn),
        in_specs=[
            pl.BlockSpec((batch, D_MODEL), lambda l, n: (0, 0)),
            pl.BlockSpec((None, D_MODEL, tn), lambda l, n: (l, 0, n)),
            pl.BlockSpec((None, 1, tn), lambda l, n: (l, 0, n)),
        ],
        out_specs=pl.BlockSpec((None, batch, tn), lambda l, n: (l, 0, n)),
        compiler_params=_cparams(("arbitrary", "arbitrary")),
        name="adaln_modulation",
    )(c, w_mod, b_mod.reshape(depth, 1, width))


def _bias_kernel(table_ref, bucket_ref, o_ref):
    h = pl.program_id(1)
    bucket = bucket_ref[...]
    acc = jnp.full(bucket.shape, NEG_INF, F32)
    for b in range(NUM_BUCKETS):
        acc = jnp.where(bucket == b, table_ref[b, h], acc)
    o_ref[...] = acc


BIAS_FULL = {1: 0, 4: 2}
BIAS_FIRST = {1: 1, 4: 3, 16: 4}
N_BIAS_SETS = 5


def _bias_masks(rel_bias_table):
    i = jnp.arange(BLOCK)[:, None]
    j = jnp.arange(2 * BLOCK)[None, :]
    max_exact = NUM_BUCKETS // 2

    def bucket_of(rel, dilation, w_sub, exists):
        n = jnp.maximum(rel * dilation, 0)
        nf = jnp.maximum(n.astype(F32), float(max_exact))
        large = max_exact + (jnp.log(nf / max_exact) / math.log(MAX_DISTANCE / max_exact)
                             * (NUM_BUCKETS - max_exact)).astype(jnp.int32)
        large = jnp.minimum(large, NUM_BUCKETS - 1)
        bucket = jnp.where(n < max_exact, n, large)
        allowed = (rel >= 0) & (rel <= w_sub) & exists
        return jnp.where(allowed, bucket, -1)

    sets = [None] * N_BIAS_SETS
    for window, dilation in DILATED_PATTERNS:
        w_sub = window // dilation
        if dilation in BIAS_FULL:
            sets[BIAS_FULL[dilation]] = bucket_of(i - j + BLOCK, dilation, w_sub, j >= 0)
        sets[BIAS_FIRST[dilation]] = bucket_of(i - j, dilation, w_sub, j < BLOCK)
    buckets = jnp.stack(sets).astype(jnp.int32)
    return pl.pallas_call(
        _bias_kernel,
        out_shape=jax.ShapeDtypeStruct((N_BIAS_SETS, ATTN_HEADS // 2, 2 * BLOCK, 2 * BLOCK), F32),
        grid=(N_BIAS_SETS, ATTN_HEADS),
        in_specs=[
            pl.BlockSpec(memory_space=pltpu.SMEM),
            pl.BlockSpec((None, BLOCK, 2 * BLOCK), lambda p, h: (p, 0, 0)),
        ],
        out_specs=pl.BlockSpec((None, None, BLOCK, 2 * BLOCK), lambda p, h: (p, h // 2, h % 2, 0)),
        compiler_params=_cparams(("arbitrary", "arbitrary")),
        name="relative_bias_masks",
    )(rel_bias_table, buckets)


HALF = ATTN_WIDTH // 2


def _in_proj_kernel(x_ref, mod_ref, gain_ref, w_ref, qg_ref, kg_ref, grp_ref,
                    q1_ref, k1_ref, v1_ref, q4_ref, k4_ref, v4_ref, q16_ref, k16_ref, v16_ref, r_ref,
                    perm_ref):
    mod = mod_ref[...]
    h = _modulated_norm(x_ref[...], gain_ref[...], mod[1:2], mod[0:1]).astype(BF16)
    proj = _dot(h, w_ref[...])
    grp = grp_ref[...]
    tm = proj.shape[0]

    def head_norm(t, gain):
        hi, lo = _split_bf16(t * t)
        ss = _dot(hi, grp) + _dot(lo, grp)
        return t * lax.rsqrt(ss * (1.0 / ATTN_HEAD_DIM) + EPS) * gain

    def emit(t, o1_ref, o4_ref, o16_ref):
        o1_ref[...] = t.astype(BF16)
        for j in range(ATTN_WIDTH // LANES):
            perm_ref[j] = t[:, j * LANES:(j + 1) * LANES]
        for dil, o_ref in ((4, o4_ref), (16, o16_ref)):
            for hf in range(2):
                for r in range(dil):
                    for jj in range(HALF // LANES):
                        c0 = (hf * dil + r) * HALF + jj * LANES
                        o_ref[:, c0:c0 + LANES] = perm_ref[hf * (HALF // LANES) + jj,
                                                           pl.ds(r, tm // dil, stride=dil), :].astype(BF16)

    emit(head_norm(proj[:, :ATTN_WIDTH], qg_ref[...]) * (ATTN_HEAD_DIM ** -0.5), q1_ref, q4_ref, q16_ref)
    emit(head_norm(proj[:, ATTN_WIDTH:2 * ATTN_WIDTH], kg_ref[...]), k1_ref, k4_ref, k16_ref)
    emit(proj[:, 2 * ATTN_WIDTH:3 * ATTN_WIDTH], v1_ref, v4_ref, v16_ref)
    r_ref[...] = proj[:, 3 * ATTN_WIDTH:]


def _in_proj(x, mod, gain, w_in, q_gain, k_gain):
    tokens = x.shape[0]
    tm = TM_PROJ
    per_seq = SEQ // tm
    grp = np.kron(np.eye(ATTN_HEADS), np.ones((ATTN_HEAD_DIM, ATTN_HEAD_DIM))).astype(np.float32)
    row = lambda i: (i, 0)
    const = lambda i: (0, 0)
    layouts = []
    for dil in (1, 4, 16):
        shape = jax.ShapeDtypeStruct((tokens // dil, dil * ATTN_WIDTH), BF16)
        spec = pl.BlockSpec((tm // dil, dil * ATTN_WIDTH), row)
        layouts.append(((shape,) * 3, (spec,) * 3))
    out_shape = sum((s for s, _ in layouts), ()) + (jax.ShapeDtypeStruct((tokens, RET_IN_WIDTH), F32),)
    out_specs = sum((s for _, s in layouts), ()) + (pl.BlockSpec((tm, RET_IN_WIDTH), row),)
    return pl.pallas_call(
        _in_proj_kernel,
        out_shape=out_shape,
        grid=(tokens // tm,),
        in_specs=[
            pl.BlockSpec((tm, D_MODEL), row),
            pl.BlockSpec((None, 6, D_MODEL), lambda i: (i // per_seq, 0, 0)),
            pl.BlockSpec((1, D_MODEL), const),
            pl.BlockSpec((D_MODEL, IN_WIDTH), const),
            pl.BlockSpec((1, ATTN_WIDTH), const),
            pl.BlockSpec((1, ATTN_WIDTH), const),
            pl.BlockSpec((ATTN_WIDTH, ATTN_WIDTH), const),
        ],
        out_specs=out_specs,
        scratch_shapes=[pltpu.VMEM((ATTN_WIDTH // LANES, tm, LANES), F32)],
        compiler_params=_cparams(("arbitrary",)),
        name="in_projection",
    )(x, mod, gain.reshape(1, D_MODEL), w_in.astype(BF16),
      jnp.tile(q_gain, ATTN_HEADS).reshape(1, ATTN_WIDTH),
      jnp.tile(k_gain, ATTN_HEADS).reshape(1, ATTN_WIDTH),
      jnp.asarray(grp, BF16))


PAIRS_PER_HALF = ATTN_HEADS // 4
GROUP = 4


def _pair_scores(qp, kp, vp, bias2, masks, low):
    q2 = jnp.concatenate([qp * masks[0], qp * masks[1]], axis=0)
    s = _dot_nt(q2, kp) + bias2
    m = jnp.max(s, axis=-1, keepdims=True)
    p = jnp.exp(s - m)
    den = jnp.sum(p, axis=-1, keepdims=True)
    pv = _dot(p.astype(BF16), vp)
    pick = lambda t: jnp.where(low, t[:BLOCK], t[BLOCK:])
    return pick(pv), pick(m), pick(den)


def _attn_kernel(q1_ref, k1_ref, v1_ref, q4_ref, k4_ref, v4_ref, q16_ref, k16_ref, v16_ref, bm_ref,
                 o_ref, acc_ref, max_ref, den_ref):
    lane = lax.broadcasted_iota(jnp.int32, (BLOCK, LANES), 1)
    low = lane < ATTN_HEAD_DIM
    masks = (jnp.where(low, 1.0, 0.0).astype(BF16), jnp.where(low, 0.0, 1.0).astype(BF16))

    def block(q_ref, k_ref, v_ref, c0, q0, w0, width, bias_set, rows, first):
        for p in range(PAIRS_PER_HALF):
            cs = slice(c0 + p * LANES, c0 + (p + 1) * LANES)
            acc, m, den = _pair_scores(q_ref[pl.ds(q0, BLOCK), cs], k_ref[pl.ds(w0, width), cs],
                                       v_ref[pl.ds(w0, width), cs], bm_ref[bias_set, p, :, 0:width],
                                       masks, low)
            if not first:
                m_old = max_ref[p, rows, :]
                m_new = jnp.maximum(m_old, m)
                a, b = jnp.exp(m_old - m_new), jnp.exp(m - m_new)
                den = den_ref[p, rows, :] * a + den * b
                acc = acc_ref[p, rows, :] * a + acc * b
                m = m_new
            max_ref[p, rows, :] = m
            den_ref[p, rows, :] = den
            acc_ref[p, rows, :] = acc

    def d1_group(g, carry):
        for u in range(GROUP):
            n = g * GROUP + u
            q0 = pl.multiple_of(n * BLOCK, BLOCK)
            w0 = pl.multiple_of(jnp.maximum(n - 1, 0) * BLOCK, BLOCK)
            bias_set = jnp.where(n == 0, BIAS_FIRST[1], BIAS_FULL[1])
            block(q1_ref, k1_ref, v1_ref, 0, q0, w0, 2 * BLOCK, bias_set, pl.ds(q0, BLOCK), True)
        return carry
    lax.fori_loop(0, SEQ // BLOCK // GROUP, d1_group, 0)

    for r in range(4):
        for n in range(SEQ // 4 // BLOCK):
            block(q4_ref, k4_ref, v4_ref, r * HALF, n * BLOCK, max(n - 1, 0) * BLOCK, 2 * BLOCK,
                  BIAS_FIRST[4] if n == 0 else BIAS_FULL[4],
                  pl.ds(r + 4 * BLOCK * n, BLOCK, stride=4), False)

    for r in range(16):
        block(q16_ref, k16_ref, v16_ref, r * HALF, 0, 0, BLOCK, BIAS_FIRST[16],
              pl.ds(r, BLOCK, stride=16), False)

    for n in range(SEQ // BLOCK):
        rows = slice(n * BLOCK, (n + 1) * BLOCK)
        for p in range(PAIRS_PER_HALF):
            o_ref[rows, p * LANES:(p + 1) * LANES] = (acc_ref[p, rows, :] / den_ref[p, rows, :]).astype(BF16)


def _dilated_attention(qkv, bias_masks):
    tokens = qkv[0].shape[0]
    batch = tokens // SEQ
    specs = []
    for dil in (1, 4, 16):
        specs += [pl.BlockSpec((SEQ // dil, dil * HALF), lambda b, hf: (b, hf))] * 3
    state = pltpu.VMEM((PAIRS_PER_HALF, SEQ, LANES), F32)
    return pl.pallas_call(
        _attn_kernel,
        out_shape=jax.ShapeDtypeStruct((tokens, ATTN_WIDTH), BF16),
        grid=(batch, 2),
        in_specs=specs + [pl.BlockSpec((N_BIAS_SETS, PAIRS_PER_HALF, 2 * BLOCK, 2 * BLOCK),
                                       lambda b, hf: (0, hf, 0, 0))],
        out_specs=pl.BlockSpec((SEQ, HALF), lambda b, hf: (b, hf)),
        scratch_shapes=[state, state, state],
        compiler_params=_cparams(("arbitrary", "arbitrary")),
        name="dilated_attention",
    )(*qkv, bias_masks)


def _retention_kernel(r_ref, cos_ref, sin_ref, dmask_ref, qdec_ref, kdec_ref, cdec_ref, gain_ref,
                      o_ref, state_ref):
    @pl.when(pl.program_id(1) == 0)
    def _():
        state_ref[...] = jnp.zeros_like(state_ref)

    lane = lax.broadcasted_iota(jnp.int32, (RET_CHUNK, LANES), 1)
    low = lane < RET_KEY_DIM
    first_half = (lane % RET_KEY_DIM) < (RET_KEY_DIM // 2)

    def rotate(t, cos, sin):
        partner = jnp.where(first_half, pltpu.roll(t, LANES - RET_KEY_DIM // 2, 1),
                            pltpu.roll(t, RET_KEY_DIM // 2, 1))
        return t * cos + partner * sin

    for c in range(RET_ROWS // RET_CHUNK):
        rows = slice(c * RET_CHUNK, (c + 1) * RET_CHUNK)
        for hp in range(RET_HEADS // 2):
            qs = slice(hp * LANES, (hp + 1) * LANES)
            ks = slice(RET_QK_WIDTH + hp * LANES, RET_QK_WIDTH + (hp + 1) * LANES)
            cos, sin = cos_ref[rows, qs], sin_ref[rows, qs]
            q_pair = rotate(r_ref[rows, qs], cos, sin) * (RET_KEY_DIM ** -0.5)
            k_pair = rotate(r_ref[rows, ks], cos, sin)
            for hh in range(2):
                head = 2 * hp + hh
                vs = slice(2 * RET_QK_WIDTH + head * LANES, 2 * RET_QK_WIDTH + (head + 1) * LANES)
                gs = slice(2 * RET_QK_WIDTH + RET_WIDTH + head * LANES,
                           2 * RET_QK_WIDTH + RET_WIDTH + (head + 1) * LANES)
                keep = low if hh == 0 else jnp.logical_not(low)
                qm = jnp.where(keep, q_pair, 0.0)
                vb = r_ref[rows, vs].astype(BF16)
                state = state_ref[head]
                inner = _dot_nt(qm.astype(BF16), k_pair.astype(BF16)) * dmask_ref[head]
                y = _dot(inner.astype(BF16), vb)
                y = y + _dot((qm * qdec_ref[head]).astype(BF16), state.astype(BF16))
                state_ref[head] = state * cdec_ref[head] + _dot_tn((k_pair * kdec_ref[head]).astype(BF16), vb)
                mu = jnp.mean(y, axis=-1, keepdims=True)
                yc = y - mu
                var = jnp.mean(yc * yc, axis=-1, keepdims=True)
                yn = yc * lax.rsqrt(var + EPS) * gain_ref[:, head * LANES:(head + 1) * LANES]
                o_ref[rows, head * LANES:(head + 1) * LANES] = (_silu(r_ref[rows, gs]) * yn).astype(BF16)


def _retention_tables():
    half = RET_KEY_DIM // 2
    pos = jnp.arange(SEQ, dtype=F32)
    inv = ROPE_BASE ** (-jnp.arange(half, dtype=F32) / half)
    ang = pos[:, None] * inv[None, :]
    cos, sin = jnp.cos(ang), jnp.sin(ang)
    cos_full = jnp.tile(jnp.concatenate([cos, cos], axis=-1), (1, RET_HEADS))
    sin_signed = jnp.tile(jnp.concatenate([-sin, sin], axis=-1), (1, RET_HEADS))
    log_g = jnp.log(1.0 - 2.0 ** (-5.0 - jnp.arange(RET_HEADS, dtype=F32)))
    idx = jnp.arange(RET_CHUNK, dtype=F32)
    diff = idx[:, None] - idx[None, :]
    dmask = jnp.where(diff >= 0, jnp.exp(jnp.maximum(diff, 0.0)[None] * log_g[:, None, None]), 0.0)
    q_decay = jnp.exp((idx + 1.0)[None, :] * log_g[:, None])[..., None]
    k_decay = jnp.exp((RET_CHUNK - 1.0 - idx)[None, :] * log_g[:, None])[..., None]
    chunk_decay = jnp.exp(RET_CHUNK * log_g)[:, None, None]
    full = (RET_HEADS, RET_CHUNK, LANES)
    return (cos_full, sin_signed, dmask, jnp.broadcast_to(q_decay, full),
            jnp.broadcast_to(k_decay, full), jnp.broadcast_to(chunk_decay, full))


def _retention(ret_in, ret_gain):
    tokens = ret_in.shape[0]
    batch = tokens // SEQ
    per_seq = SEQ // RET_ROWS
    cos, sin, dmask, qdec, kdec, cdec = _retention_tables()
    tab = pl.BlockSpec((RET_ROWS, RET_QK_WIDTH), lambda b, j: (j, 0))
    const3 = pl.BlockSpec((RET_HEADS, RET_CHUNK, LANES), lambda b, j: (0, 0, 0))
    return pl.pallas_call(
        _retention_kernel,
        out_shape=jax.ShapeDtypeStruct((tokens, RET_WIDTH), BF16),
        grid=(batch, per_seq),
        in_specs=[
            pl.BlockSpec((RET_ROWS, RET_IN_WIDTH), lambda b, j: (b * per_seq + j, 0)),
            tab, tab, const3, const3, const3, const3,
            pl.BlockSpec((1, RET_WIDTH), lambda b, j: (0, 0)),
        ],
        out_specs=pl.BlockSpec((RET_ROWS, RET_WIDTH), lambda b, j: (b * per_seq + j, 0)),
        scratch_shapes=[pltpu.VMEM((RET_HEADS, LANES, RET_VALUE_DIM), F32)],
        compiler_params=_cparams(("arbitrary", "arbitrary")),
        name="retention",
    )(ret_in, cos, sin, dmask, qdec, kdec, cdec, ret_gain.reshape(1, RET_WIDTH))


def _out_proj_kernel(attn_ref, ret_ref, x_ref, mod_ref, w_ref, out_ref):
    mix = _dot(attn_ref[...], w_ref[:ATTN_WIDTH, :]) + _dot(ret_ref[...], w_ref[ATTN_WIDTH:, :])
    out_ref[...] = x_ref[...] + mod_ref[2:3, :] * mix


def _out_proj(attn, ret, x, mod, w_out):
    tokens = x.shape[0]
    tm = TM_FFN
    per_seq = SEQ // tm
    row = lambda i: (i, 0)
    return pl.pallas_call(
        _out_proj_kernel,
        out_shape=jax.ShapeDtypeStruct((tokens, D_MODEL), F32),
        grid=(tokens // tm,),
        in_specs=[pl.BlockSpec((tm, ATTN_WIDTH), row),
                  pl.BlockSpec((tm, RET_WIDTH), row),
                  pl.BlockSpec((tm, D_MODEL), row),
                  pl.BlockSpec((None, 6, D_MODEL), lambda i: (i // per_seq, 0, 0)),
                  pl.BlockSpec((D_MODEL, D_MODEL), lambda i: (0, 0))],
        out_specs=pl.BlockSpec((tm, D_MODEL), row),
        compiler_params=_cparams(("arbitrary",)),
        name="out_projection",
    )(attn, ret, x, mod, w_out.astype(BF16))


def _swiglu_chunks(h, w1_ref, w3_ref, w2_ref):
    d_ff = w1_ref.shape[-1]
    total = None
    for c0 in range(0, d_ff, MXU_COLS):
        c1 = min(c0 + MXU_COLS, d_ff)
        z = (_silu(_dot(h, w1_ref[:, c0:c1])) * _dot(h, w3_ref[:, c0:c1])).astype(BF16)
        part = _dot(z, w2_ref[c0:c1, :])
        total = part if total is None else total + part
    return total


def _ffn_kernel(x_ref, mod_ref, gain_ref, w1_ref, w3_ref, w2_ref, out_ref):
    mod = mod_ref[...]
    x = x_ref[...]
    h = _modulated_norm(x, gain_ref[...], mod[4:5], mod[3:4]).astype(BF16)
    out_ref[...] = x + mod[5:6, :] * _swiglu_chunks(h, w1_ref, w3_ref, w2_ref)


def _dense_ffn(x, mod, gain, w1, w3, w2):
    tokens = x.shape[0]
    d_ff = w1.shape[1]
    tm = TM_FFN
    per_seq = SEQ // tm
    resident = pl.Buffered(1)
    return pl.pallas_call(
        _ffn_kernel,
        out_shape=jax.ShapeDtypeStruct((tokens, D_MODEL), F32),
        grid=(tokens // tm,),
        in_specs=[
            pl.BlockSpec((tm, D_MODEL), lambda i: (i, 0)),
            pl.BlockSpec((None, 6, D_MODEL), lambda i: (i // per_seq, 0, 0)),
            pl.BlockSpec((1, D_MODEL), lambda i: (0, 0)),
            pl.BlockSpec((D_MODEL, d_ff), lambda i: (0, 0), pipeline_mode=resident),
            pl.BlockSpec((D_MODEL, d_ff), lambda i: (0, 0), pipeline_mode=resident),
            pl.BlockSpec((d_ff, D_MODEL), lambda i: (0, 0), pipeline_mode=resident),
        ],
        out_specs=pl.BlockSpec((tm, D_MODEL), lambda i: (i, 0)),
        compiler_params=_cparams(("arbitrary",)),
        name="dense_swiglu",
    )(x, mod, gain.reshape(1, D_MODEL), w1.astype(BF16), w3.astype(BF16), w2.astype(BF16))


def _router_kernel(x_ref, mod_ref, gain_ref, wr_ref, h_ref, gates_ref, pos_ref, post_ref, start_ref,
                   cnt_ref, carry_ref, *, tiles_per_seq):
    i = pl.program_id(0)

    @pl.when(i % tiles_per_seq == 0)
    def _():
        carry_ref[...] = jnp.zeros_like(carry_ref)

    mod = mod_ref[...]
    h = _modulated_norm(x_ref[...], gain_ref[...], mod[4:5], mod[3:4]).astype(BF16)
    h_ref[...] = h
    tm = h.shape[0]
    lane = lax.broadcasted_iota(jnp.int32, (tm, LANES), 1).astype(F32)
    logits = jnp.where(lane < N_EXPERTS, _dot(h, wr_ref[...]), -jnp.inf)
    m1 = jnp.max(logits, axis=-1, keepdims=True)
    i1 = jnp.min(jnp.where(logits == m1, lane, float(LANES)), axis=-1, keepdims=True)
    rest = jnp.where(lane == i1, -jnp.inf, logits)
    m2 = jnp.max(rest, axis=-1, keepdims=True)
    i2 = jnp.min(jnp.where(rest == m2, lane, float(LANES)), axis=-1, keepdims=True)
    e2 = jnp.exp(m2 - m1)
    g1 = 1.0 / (1.0 + e2)
    g2 = e2 / (1.0 + e2)
    gates_ref[...] = jnp.where(lane == i1, g1, 0.0) + jnp.where(lane == i2, g2, 0.0)
    chosen = (lane == i1) | (lane == i2)
    onehot = jnp.where(chosen, 1.0, 0.0)
    r = lax.broadcasted_iota(jnp.int32, (tm, tm), 0)
    c = lax.broadcasted_iota(jnp.int32, (tm, tm), 1)
    tril = jnp.where(c <= r, 1.0, 0.0).astype(BF16)
    incl = _dot(tril, onehot.astype(BF16))
    carry = carry_ref[0:1, :]
    start_ref[...] = carry_ref[...]
    pos = jnp.where(chosen, incl - 1.0 + carry, -1.0)
    pos_ref[...] = pos
    post_ref[...] = pos.T[:N_EXPERTS, :]
    total = carry + incl[tm - 1:tm, :]
    carry_ref[...] = jnp.broadcast_to(total, carry_ref.shape)
    cnt_ref[...] = jnp.broadcast_to(total, cnt_ref.shape)


def _router(x, mod, gain, w_router):
    tokens = x.shape[0]
    batch = tokens // SEQ
    tm = TM_PROJ
    per_seq = SEQ // tm
    wr = jnp.zeros((D_MODEL, LANES), BF16).at[:, :N_EXPERTS].set(w_router.astype(BF16))
    row = lambda i: (i, 0)
    return pl.pallas_call(
        functools.partial(_router_kernel, tiles_per_seq=per_seq),
        out_shape=(
            jax.ShapeDtypeStruct((tokens, D_MODEL), BF16),
            jax.ShapeDtypeStruct((tokens, LANES), F32),
            jax.ShapeDtypeStruct((tokens, LANES), F32),
            jax.ShapeDtypeStruct((N_EXPERTS, tokens), F32),
            jax.ShapeDtypeStruct((tokens // tm, 8, LANES), F32),
            jax.ShapeDtypeStruct((batch, 8, LANES), F32),
        ),
        grid=(tokens // tm,),
        in_specs=[
            pl.BlockSpec((tm, D_MODEL), row),
            pl.BlockSpec((None, 6, D_MODEL), lambda i: (i // per_seq, 0, 0)),
            pl.BlockSpec((1, D_MODEL), lambda i: (0, 0)),
            pl.BlockSpec((D_MODEL, LANES), lambda i: (0, 0)),
        ],
        out_specs=(
            pl.BlockSpec((tm, D_MODEL), row),
            pl.BlockSpec((tm, LANES), row),
            pl.BlockSpec((tm, LANES), row),
            pl.BlockSpec((N_EXPERTS, tm), lambda i: (0, i)),
            pl.BlockSpec((None, 8, LANES), lambda i: (i, 0, 0)),
            pl.BlockSpec((None, 8, LANES), lambda i: (i // per_seq, 0, 0)),
        ),
        scratch_shapes=[pltpu.VMEM((8, LANES), F32)],
        compiler_params=_cparams(("arbitrary",)),
        name="expert_router",
    )(x, mod, gain.reshape(1, D_MODEL), wr)


TOK_BLOCKS = SEQ // TR_MOE
TAIL_TILES = (64, 128, TR_MOE)
SPILL_ROWS = 16


def _moe_kernel(tbl_ref, h_ref, post_ref, pos_ref, gates_ref, x_ref, mod_ref, w1_ref, w3_ref, w2_ref,
                out_ref, hs_ref, acc_ref):
    b, e, f = pl.program_id(0), pl.program_id(1), pl.program_id(2)
    last_f = pl.num_programs(2) - 1
    tr = TR_MOE
    base = b * (TOK_BLOCKS + 1) * N_EXPERTS + e
    bounds = [tbl_ref[base + tb * N_EXPERTS] for tb in range(TOK_BLOCKS + 1)]
    win = [pl.multiple_of(lax.shift_left(lax.shift_right_logical(s, 3), 3), SUBLANES) for s in bounds[:-1]]
    spills = [bounds[tb + 1] - win[tb] > tr for tb in range(TOK_BLOCKS)]
    total = bounds[-1]
    n_tiles = lax.shift_right_logical(total + (tr - 1), tr.bit_length() - 1)

    def clear(r, carry):
        acc_ref[pl.ds(pl.multiple_of(r * tr, tr), tr), :] = jnp.zeros((tr, D_MODEL), F32)
        return carry

    @pl.when((e == 0) & (f == 0))
    def _():
        out_ref[...] = jnp.zeros_like(out_ref)

    @pl.when(f == 0)
    def _():
        lax.fori_loop(0, n_tiles + 2, clear, 0)
        slot = lax.broadcasted_iota(jnp.int32, (tr + SPILL_ROWS, tr), 0).astype(F32)
        for tb in range(TOK_BLOCKS):
            ts = slice(tb * tr, (tb + 1) * tr)
            local = post_ref[pl.ds(e, 1), ts] - win[tb].astype(F32)
            sel = jnp.where(local == slot, 1.0, 0.0).astype(BF16)
            acc_ref[pl.ds(win[tb], tr + SPILL_ROWS), :] += _dot(sel, h_ref[ts, :])

        def to_rows(r, carry):
            rows = pl.ds(pl.multiple_of(r * tr, tr), tr)
            hs_ref[rows, :] = acc_ref[rows, :].astype(BF16)
            acc_ref[rows, :] = jnp.zeros((tr, D_MODEL), F32)
            return carry
        lax.fori_loop(0, n_tiles + 2, to_rows, 0)

    def swiglu(r0, m):
        rows = pl.ds(r0, m)
        acc_ref[rows, :] += _swiglu_chunks(hs_ref[rows, :], w1_ref, w3_ref, w2_ref)

    def full_tile(r, carry):
        swiglu(pl.multiple_of(r * tr, tr), tr)
        return carry
    n_full = lax.shift_right_logical(total, tr.bit_length() - 1)
    lax.fori_loop(0, n_full, full_tile, 0)
    rest = total - n_full * tr
    tail0 = pl.multiple_of(n_full * tr, tr)
    lo = 0
    for m in TAIL_TILES:
        @pl.when((rest > lo) & (rest <= m))
        def _():
            swiglu(tail0, m)
        lo = m

    @pl.when(f == last_f)
    def _():
        lane = lax.broadcasted_iota(jnp.int32, (tr, LANES), 1)
        slot = lax.broadcasted_iota(jnp.int32, (tr, tr), 1).astype(F32)
        slot_spill = lax.broadcasted_iota(jnp.int32, (tr, SPILL_ROWS), 1).astype(F32) + float(tr)
        def columns(tb):
            ts = slice(tb * tr, (tb + 1) * tr)
            pos_col = jnp.sum(jnp.where(lane == e, pos_ref[ts, :], 0.0), axis=-1, keepdims=True)
            gate_col = jnp.sum(jnp.where(lane == e, gates_ref[ts, :], 0.0), axis=-1, keepdims=True)
            return ts, pos_col - win[tb].astype(F32), gate_col

        for tb in range(TOK_BLOCKS):
            ts, local, gate_col = columns(tb)
            rows = acc_ref[pl.ds(win[tb], tr), :].astype(BF16)
            sel = jnp.where(local == slot, 1.0, 0.0).astype(BF16)
            out_ref[ts, :] += gate_col * _dot(sel, rows)

        for tb in range(TOK_BLOCKS):
            @pl.when(spills[tb])
            def _():
                ts, local, gate_col = columns(tb)
                rows2 = acc_ref[pl.ds(win[tb] + tr, SPILL_ROWS), :].astype(BF16)
                sel2 = jnp.where(local == slot_spill, 1.0, 0.0).astype(BF16)
                out_ref[ts, :] += gate_col * _dot(sel2, rows2)

    @pl.when((e == N_EXPERTS - 1) & (f == last_f))
    def _():
        out_ref[...] = x_ref[...] + mod_ref[5:6, :] * out_ref[...]


def _cast_kernel(w_ref, o_ref):
    o_ref[...] = w_ref[...].astype(BF16)


def _chunk_columns(w, tf):
    n_e, k, f = w.shape
    return pl.pallas_call(
        _cast_kernel,
        out_shape=jax.ShapeDtypeStruct((n_e, f // tf, k, tf), BF16),
        grid=(n_e, f // tf),
        in_specs=[pl.BlockSpec((None, k, tf), lambda e, c: (e, 0, c))],
        out_specs=pl.BlockSpec((None, None, k, tf), lambda e, c: (e, c, 0, 0)),
        compiler_params=_cparams(("arbitrary", "arbitrary")),
        name="chunk_expert_weights",
    )(w)


def _moe(x, mod, h, post, pos, gates, table, w1, w3, w2):
    tokens = h.shape[0]
    batch = tokens // SEQ
    d_ff = w1.shape[2]
    tf = TF_MOE
    n_f = d_ff // tf
    once = pl.Buffered(1)
    max_rows = SEQ + 3 * TR_MOE
    grid_spec = pltpu.PrefetchScalarGridSpec(
        num_scalar_prefetch=1,
        grid=(batch, N_EXPERTS, n_f),
        in_specs=[
            pl.BlockSpec((SEQ, D_MODEL), lambda b, e, f, t: (b, 0), pipeline_mode=once),
            pl.BlockSpec((N_EXPERTS, SEQ), lambda b, e, f, t: (0, b), pipeline_mode=once),
            pl.BlockSpec((SEQ, LANES), lambda b, e, f, t: (b, 0), pipeline_mode=once),
            pl.BlockSpec((SEQ, LANES), lambda b, e, f, t: (b, 0), pipeline_mode=once),
            pl.BlockSpec((SEQ, D_MODEL), lambda b, e, f, t: (b, 0), pipeline_mode=once),
            pl.BlockSpec((None, 6, D_MODEL), lambda b, e, f, t: (b, 0, 0)),
            pl.BlockSpec((None, None, D_MODEL, tf), lambda b, e, f, t: (e, f, 0, 0)),
            pl.BlockSpec((None, None, D_MODEL, tf), lambda b, e, f, t: (e, f, 0, 0)),
            pl.BlockSpec((None, tf, D_MODEL), lambda b, e, f, t: (e, f, 0)),
        ],
        out_specs=pl.BlockSpec((SEQ, D_MODEL), lambda b, e, f, t: (b, 0), pipeline_mode=once),
        scratch_shapes=[pltpu.VMEM((max_rows, D_MODEL), BF16), pltpu.VMEM((max_rows, D_MODEL), F32)],
    )
    return pl.pallas_call(
        _moe_kernel,
        out_shape=jax.ShapeDtypeStruct((tokens, D_MODEL), F32),
        grid_spec=grid_spec,
        compiler_params=_cparams(("arbitrary", "arbitrary", "arbitrary")),
        name="expert_swiglu",
    )(table, h, post, pos, gates, x, mod, _chunk_columns(w1, tf), _chunk_columns(w3, tf), w2.astype(BF16))


def _token_mixer(x, mod, gain, w_in, q_gain, k_gain, ret_gain, w_out, bias_masks):
    *qkv, ret_in = _in_proj(x, mod, gain, w_in, q_gain, k_gain)
    attn = _dilated_attention(qkv, bias_masks)
    ret = _retention(ret_in, ret_gain)
    return _out_proj(attn, ret, x, mod, w_out)


def _moe_ffn(x, mod, gain, w_router, w1, w3, w2):
    h, gates, pos, post, start, cnt = _router(x, mod, gain, w_router)
    batch = cnt.shape[0]
    table = jnp.concatenate([start[:, 0, :N_EXPERTS].reshape(batch, TOK_BLOCKS, N_EXPERTS),
                             cnt[:, :1, :N_EXPERTS]], axis=1).astype(jnp.int32).reshape(-1)
    return _moe(x, mod, h, post, pos, gates, table, w1, w3, w2)


def kernel(x, c, rel_bias_table, norm_mix, norm_ffn, w_mod, b_mod, w_in, q_gain, k_gain, ret_gain, w_out,
           ffn_w1, ffn_w3, ffn_w2, moe_router, moe_w1, moe_w3, moe_w2):
    batch, seq, d_model = x.shape
    assert (seq, d_model) == (SEQ, D_MODEL)
    depth = w_mod.shape[0]
    mods = _modulation(c, w_mod, b_mod).reshape(depth, batch, 6, D_MODEL)
    bias_masks = _bias_masks(rel_bias_table)
    xt = x.reshape(batch * seq, d_model)
    for layer in range(depth):
        mod = mods[layer]
        xt = _token_mixer(xt, mod, norm_mix[layer], w_in[layer], q_gain[layer], k_gain[layer],
                          ret_gain[layer], w_out[layer], bias_masks)
        i = layer // 2
        if layer % 2 == 0:
            xt = _dense_ffn(xt, mod, norm_ffn[layer], ffn_w1[i], ffn_w3[i], ffn_w2[i])
        else:
            xt = _moe_ffn(xt, mod, norm_ffn[layer], moe_router[i], moe_w1[i], moe_w3[i], moe_w2[i])
    return xt.reshape(batch, seq, d_model)
```

```python
import functools
import math

import jax
import jax.numpy as jnp
import numpy as np
from jax import lax
from jax.experimental import pallas as pl
from jax.experimental.pallas import tpu as pltpu

D_MODEL = 1024
SEQ = 2048
ATTN_HEADS = 8
ATTN_HEAD_DIM = 64
ATTN_WIDTH = ATTN_HEADS * ATTN_HEAD_DIM
DILATED_PATTERNS = ((128, 1), (512, 4), (2048, 16))
BLOCK = 128
NUM_BUCKETS = 32
MAX_DISTANCE = 2048
RET_HEADS = 4
RET_KEY_DIM = 64
RET_VALUE_DIM = 128
RET_WIDTH = RET_HEADS * RET_VALUE_DIM
RET_QK_WIDTH = RET_HEADS * RET_KEY_DIM
RET_CHUNK = 128
ROPE_BASE = 10000.0
IN_WIDTH = 3 * ATTN_WIDTH + 2 * RET_QK_WIDTH + 2 * RET_WIDTH
RET_IN_WIDTH = IN_WIDTH - 3 * ATTN_WIDTH
N_EXPERTS = 8
EPS = 1e-6
NEG_INF = -1e30

LANES = 128
VMEM_LIMIT = 56 * 1024 * 1024

BF16 = jnp.bfloat16
F32 = jnp.float32

TM_PROJ = 256
TM_FFN = 512
MXU_COLS = 256
TR_MOE = 256
TF_MOE = 896
assert TM_PROJ == TR_MOE
RET_ROWS = 512
SUBLANES = 8


def _cparams(sem):
    return pltpu.CompilerParams(dimension_semantics=sem, vmem_limit_bytes=VMEM_LIMIT)


def _dot(a, b):
    return jnp.dot(a, b, preferred_element_type=F32)


def _dot_nt(a, b):
    return lax.dot_general(a, b, (((1,), (1,)), ((), ())), preferred_element_type=F32)


def _dot_tn(a, b):
    return lax.dot_general(a, b, (((0,), (0,)), ((), ())), preferred_element_type=F32)


def _split_bf16(v):
    hi = v.astype(BF16)
    lo = (v - hi.astype(F32)).astype(BF16)
    return hi, lo


def _silu(v):
    return v * (1.0 / (1.0 + jnp.exp(-v)))


def _modulated_norm(x, gain, scale, shift):
    ms = jnp.mean(x * x, axis=-1, keepdims=True)
    y = x * lax.rsqrt(ms + EPS) * gain
    return y * (1.0 + scale) + shift


def _mod_kernel(c_ref, w_ref, b_ref, o_ref):
    ca = _silu(c_ref[...]).astype(BF16)
    o_ref[...] = _dot(ca, w_ref[...].astype(BF16)) + b_ref[...]


def _modulation(c, w_mod, b_mod):
    depth, _, width = w_mod.shape
    batch = c.shape[0]
    tn = 1536
    return pl.pallas_call(
        _mod_kernel,
        out_shape=jax.ShapeDtypeStruct((depth, batch, width), F32),
        grid=(depth, width // tn),
        in_specs=[
            pl.BlockSpec((batch, D_MODEL), lambda l, n: (0, 0)),
            pl.BlockSpec((None, D_MODEL, tn), lambda l, n: (l, 0, n)),
            pl.BlockSpec((None, 1, tn), lambda l, n: (l, 0, n)),
        ],
        out_specs=pl.BlockSpec((None, batch, tn), lambda l, n: (l, 0, n)),
        compiler_params=_cparams(("arbitrary", "arbitrary")),
        name="adaln_modulation",
    )(c, w_mod, b_mod.reshape(depth, 1, width))


def _bias_kernel(table_ref, bucket_ref, o_ref):
    h = pl.program_id(1)
    bucket = bucket_ref[...]
    acc = jnp.full(bucket.shape, NEG_INF, F32)
    for b in range(NUM_BUCKETS):
        acc = jnp.where(bucket == b, table_ref[b, h], acc)
    o_ref[...] = acc


BIAS_FULL = {1: 0, 4: 2}
BIAS_FIRST = {1: 1, 4: 3, 16: 4}
N_BIAS_SETS = 5


def _bias_masks(rel_bias_table):
    i = jnp.arange(BLOCK)[:, None]
    j = jnp.arange(2 * BLOCK)[None, :]
    max_exact = NUM_BUCKETS // 2

    def bucket_of(rel, dilation, w_sub, exists):
        n = jnp.maximum(rel * dilation, 0)
        nf = jnp.maximum(n.astype(F32), float(max_exact))
        large = max_exact + (jnp.log(nf / max_exact) / math.log(MAX_DISTANCE / max_exact)
                             * (NUM_BUCKETS - max_exact)).astype(jnp.int32)
        large = jnp.minimum(large, NUM_BUCKETS - 1)
        bucket = jnp.where(n < max_exact, n, large)
        allowed = (rel >= 0) & (rel <= w_sub) & exists
        return jnp.where(allowed, bucket, -1)

    sets = [None] * N_BIAS_SETS
    for window, dilation in DILATED_PATTERNS:
        w_sub = window // dilation
        if dilation in BIAS_FULL:
            sets[BIAS_FULL[dilation]] = bucket_of(i - j + BLOCK, dilation, w_sub, j >= 0)
        sets[BIAS_FIRST[dilation]] = bucket_of(i - j, dilation, w_sub, j < BLOCK)
    buckets = jnp.stack(sets).astype(jnp.int32)
    return pl.pallas_call(
        _bias_kernel,
        out_shape=jax.ShapeDtypeStruct((N_BIAS_SETS, ATTN_HEADS // 2, 2 * BLOCK, 2 * BLOCK), F32),
        grid=(N_BIAS_SETS, ATTN_HEADS),
        in_specs=[
            pl.BlockSpec(memory_space=pltpu.SMEM),
            pl.BlockSpec((None, BLOCK, 2 * BLOCK), lambda p, h: (p, 0, 0)),
        ],
        out_specs=pl.BlockSpec((None, None, BLOCK, 2 * BLOCK), lambda p, h: (p, h // 2, h % 2, 0)),
        compiler_params=_cparams(("arbitrary", "arbitrary")),
        name="relative_bias_masks",
    )(rel_bias_table, buckets)


HALF = ATTN_WIDTH // 2


def _in_proj_kernel(x_ref, mod_ref, gain_ref, w_ref, qg_ref, kg_ref, grp_ref,
                    q1_ref, k1_ref, v1_ref, q4_ref, k4_ref, v4_ref, q16_ref, k16_ref, v16_ref, r_ref,
                    perm_ref):
    mod = mod_ref[...]
    h = _modulated_norm(x_ref[...], gain_ref[...], mod[1:2], mod[0:1]).astype(BF16)
    proj = _dot(h, w_ref[...])
    grp = grp_ref[...]
    tm = proj.shape[0]

    def head_norm(t, gain):
        hi, lo = _split_bf16(t * t)
        ss = _dot(hi, grp) + _dot(lo, grp)
        return t * lax.rsqrt(ss * (1.0 / ATTN_HEAD_DIM) + EPS) * gain

    def emit(t, o1_ref, o4_ref, o16_ref):
        o1_ref[...] = t.astype(BF16)
        for j in range(ATTN_WIDTH // LANES):
            perm_ref[j] = t[:, j * LANES:(j + 1) * LANES]
        for dil, o_ref in ((4, o4_ref), (16, o16_ref)):
            for hf in range(2):
                for r in range(dil):
                    for jj in range(HALF // LANES):
                        c0 = (hf * dil + r) * HALF + jj * LANES
                        o_ref[:, c0:c0 + LANES] = perm_ref[hf * (HALF // LANES) + jj,
                                                           pl.ds(r, tm // dil, stride=dil), :].astype(BF16)

    emit(head_norm(proj[:, :ATTN_WIDTH], qg_ref[...]) * (ATTN_HEAD_DIM ** -0.5), q1_ref, q4_ref, q16_ref)
    emit(head_norm(proj[:, ATTN_WIDTH:2 * ATTN_WIDTH], kg_ref[...]), k1_ref, k4_ref, k16_ref)
    emit(proj[:, 2 * ATTN_WIDTH:3 * ATTN_WIDTH], v1_ref, v4_ref, v16_ref)
    r_ref[...] = proj[:, 3 * ATTN_WIDTH:]


def _in_proj(x, mod, gain, w_in, q_gain, k_gain):
    tokens = x.shape[0]
    tm = TM_PROJ
    per_seq = SEQ // tm
    grp = np.kron(np.eye(ATTN_HEADS), np.ones((ATTN_HEAD_DIM, ATTN_HEAD_DIM))).astype(np.float32)
    row = lambda i: (i, 0)
    const = lambda i: (0, 0)
    layouts = []
    for dil in (1, 4, 16):
        shape = jax.ShapeDtypeStruct((tokens // dil, dil * ATTN_WIDTH), BF16)
        spec = pl.BlockSpec((tm // dil, dil * ATTN_WIDTH), row)
        layouts.append(((shape,) * 3, (spec,) * 3))
    out_shape = sum((s for s, _ in layouts), ()) + (jax.ShapeDtypeStruct((tokens, RET_IN_WIDTH), F32),)
    out_specs = sum((s for _, s in layouts), ()) + (pl.BlockSpec((tm, RET_IN_WIDTH), row),)
    return pl.pallas_call(
        _in_proj_kernel,
        out_shape=out_shape,
        grid=(tokens // tm,),
        in_specs=[
            pl.BlockSpec((tm, D_MODEL), row),
            pl.BlockSpec((None, 6, D_MODEL), lambda i: (i // per_seq, 0, 0)),
            pl.BlockSpec((1, D_MODEL), const),
            pl.BlockSpec((D_MODEL, IN_WIDTH), const),
            pl.BlockSpec((1, ATTN_WIDTH), const),
            pl.BlockSpec((1, ATTN_WIDTH), const),
            pl.BlockSpec((ATTN_WIDTH, ATTN_WIDTH), const),
        ],
        out_specs=out_specs,
        scratch_shapes=[pltpu.VMEM((ATTN_WIDTH // LANES, tm, LANES), F32)],
        compiler_params=_cparams(("arbitrary",)),
        name="in_projection",
    )(x, mod, gain.reshape(1, D_MODEL), w_in.astype(BF16),
      jnp.tile(q_gain, ATTN_HEADS).reshape(1, ATTN_WIDTH),
      jnp.tile(k_gain, ATTN_HEADS).reshape(1, ATTN_WIDTH),
      jnp.asarray(grp, BF16))


PAIRS_PER_HALF = ATTN_HEADS // 4
GROUP = 4


def _pair_scores(qp, kp, vp, bias2, masks, low):
    q2 = jnp.concatenate([qp * masks[0], qp * masks[1]], axis=0)
    s = _dot_nt(q2, kp) + bias2
    m = jnp.max(s, axis=-1, keepdims=True)
    p = jnp.exp(s - m)
    den = jnp.sum(p, axis=-1, keepdims=True)
    pv = _dot(p.astype(BF16), vp)
    pick = lambda t: jnp.where(low, t[:BLOCK], t[BLOCK:])
    return pick(pv), pick(m), pick(den)


def _attn_kernel(q1_ref, k1_ref, v1_ref, q4_ref, k4_ref, v4_ref, q16_ref, k16_ref, v16_ref, bm_ref,
                 o_ref, acc_ref, max_ref, den_ref):
    lane = lax.broadcasted_iota(jnp.int32, (BLOCK, LANES), 1)
    low = lane < ATTN_HEAD_DIM
    masks = (jnp.where(low, 1.0, 0.0).astype(BF16), jnp.where(low, 0.0, 1.0).astype(BF16))

    def block(q_ref, k_ref, v_ref, c0, q0, w0, width, bias_set, rows, first):
        for p in range(PAIRS_PER_HALF):
            cs = slice(c0 + p * LANES, c0 + (p + 1) * LANES)
            acc, m, den = _pair_scores(q_ref[pl.ds(q0, BLOCK), cs], k_ref[pl.ds(w0, width), cs],
                                       v_ref[pl.ds(w0, width), cs], bm_ref[bias_set, p, :, 0:width],
                                       masks, low)
            if not first:
                m_old = max_ref[p, rows, :]
                m_new = jnp.maximum(m_old, m)
                a, b = jnp.exp(m_old - m_new), jnp.exp(m - m_new)
                den = den_ref[p, rows, :] * a + den * b
                acc = acc_ref[p, rows, :] * a + acc * b
                m = m_new
            max_ref[p, rows, :] = m
            den_ref[p, rows, :] = den
            acc_ref[p, rows, :] = acc

    def d1_group(g, carry):
        for u in range(GROUP):
            n = g * GROUP + u
            q0 = pl.multiple_of(n * BLOCK, BLOCK)
            w0 = pl.multiple_of(jnp.maximum(n - 1, 0) * BLOCK, BLOCK)
            bias_set = jnp.where(n == 0, BIAS_FIRST[1], BIAS_FULL[1])
            block(q1_ref, k1_ref, v1_ref, 0, q0, w0, 2 * BLOCK, bias_set, pl.ds(q0, BLOCK), True)
        return carry
    lax.fori_loop(0, SEQ // BLOCK // GROUP, d1_group, 0)

    for r in range(4):
        for n in range(SEQ // 4 // BLOCK):
            block(q4_ref, k4_ref, v4_ref, r * HALF, n * BLOCK, max(n - 1, 0) * BLOCK, 2 * BLOCK,
                  BIAS_FIRST[4] if n == 0 else BIAS_FULL[4],
                  pl.ds(r + 4 * BLOCK * n, BLOCK, stride=4), False)

    for r in range(16):
        block(q16_ref, k16_ref, v16_ref, r * HALF, 0, 0, BLOCK, BIAS_FIRST[16],
              pl.ds(r, BLOCK, stride=16), False)

    for n in range(SEQ // BLOCK):
        rows = slice(n * BLOCK, (n + 1) * BLOCK)
        for p in range(PAIRS_PER_HALF):
            o_ref[rows, p * LANES:(p + 1) * LANES] = (acc_ref[p, rows, :] / den_ref[p, rows, :]).astype(BF16)


def _dilated_attention(qkv, bias_masks):
    tokens = qkv[0].shape[0]
    batch = tokens // SEQ
    specs = []
    for dil in (1, 4, 16):
        specs += [pl.BlockSpec((SEQ // dil, dil * HALF), lambda b, hf: (b, hf))] * 3
    state = pltpu.VMEM((PAIRS_PER_HALF, SEQ, LANES), F32)
    return pl.pallas_call(
        _attn_kernel,
        out_shape=jax.ShapeDtypeStruct((tokens, ATTN_WIDTH), BF16),
        grid=(batch, 2),
        in_specs=specs + [pl.BlockSpec((N_BIAS_SETS, PAIRS_PER_HALF, 2 * BLOCK, 2 * BLOCK),
                                       lambda b, hf: (0, hf, 0, 0))],
        out_specs=pl.BlockSpec((SEQ, HALF), lambda b, hf: (b, hf)),
        scratch_shapes=[state, state, state],
        compiler_params=_cparams(("arbitrary", "arbitrary")),
        name="dilated_attention",
    )(*qkv, bias_masks)


def _retention_kernel(r_ref, cos_ref, sin_ref, dmask_ref, qdec_ref, kdec_ref, cdec_ref, gain_ref,
                      o_ref, state_ref):
    @pl.when(pl.program_id(1) == 0)
    def _():
        state_ref[...] = jnp.zeros_like(state_ref)

    lane = lax.broadcasted_iota(jnp.int32, (RET_CHUNK, LANES), 1)
    low = lane < RET_KEY_DIM
    first_half = (lane % RET_KEY_DIM) < (RET_KEY_DIM // 2)

    def rotate(t, cos, sin):
        partner = jnp.where(first_half, pltpu.roll(t, LANES - RET_KEY_DIM // 2, 1),
                            pltpu.roll(t, RET_KEY_DIM // 2, 1))
        return t * cos + partner * sin

    for c in range(RET_ROWS // RET_CHUNK):
        rows = slice(c * RET_CHUNK, (c + 1) * RET_CHUNK)
        for hp in range(RET_HEADS // 2):
            qs = slice(hp * LANES, (hp + 1) * LANES)
            ks = slice(RET_QK_WIDTH + hp * LANES, RET_QK_WIDTH + (hp + 1) * LANES)
            cos, sin = cos_ref[rows, qs], sin_ref[rows, qs]
            q_pair = rotate(r_ref[rows, qs], cos, sin) * (RET_KEY_DIM ** -0.5)
            k_pair = rotate(r_ref[rows, ks], cos, sin)
            for hh in range(2):
                head = 2 * hp + hh
                vs = slice(2 * RET_QK_WIDTH + head * LANES, 2 * RET_QK_WIDTH + (head + 1) * LANES)
                gs = slice(2 * RET_QK_WIDTH + RET_WIDTH + head * LANES,
                           2 * RET_QK_WIDTH + RET_WIDTH + (head + 1) * LANES)
                keep = low if hh == 0 else jnp.logical_not(low)
                qm = jnp.where(keep, q_pair, 0.0)
                vb = r_ref[rows, vs].astype(BF16)
                state = state_ref[head]
                inner = _dot_nt(qm.astype(BF16), k_pair.astype(BF16)) * dmask_ref[head]
                y = _dot(inner.astype(BF16), vb)
                y = y + _dot((qm * qdec_ref[head]).astype(BF16), state.astype(BF16))
                state_ref[head] = state * cdec_ref[head] + _dot_tn((k_pair * kdec_ref[head]).astype(BF16), vb)
                mu = jnp.mean(y, axis=-1, keepdims=True)
                yc = y - mu
                var = jnp.mean(yc * yc, axis=-1, keepdims=True)
                yn = yc * lax.rsqrt(var + EPS) * gain_ref[:, head * LANES:(head + 1) * LANES]
                o_ref[rows, head * LANES:(head + 1) * LANES] = (_silu(r_ref[rows, gs]) * yn).astype(BF16)


def _retention_tables():
    half = RET_KEY_DIM // 2
    pos = jnp.arange(SEQ, dtype=F32)
    inv = ROPE_BASE ** (-jnp.arange(half, dtype=F32) / half)
    ang = pos[:, None] * inv[None, :]
    cos, sin = jnp.cos(ang), jnp.sin(ang)
    cos_full = jnp.tile(jnp.concatenate([cos, cos], axis=-1), (1, RET_HEADS))
    sin_signed = jnp.tile(jnp.concatenate([-sin, sin], axis=-1), (1, RET_HEADS))
    log_g = jnp.log(1.0 - 2.0 ** (-5.0 - jnp.arange(RET_HEADS, dtype=F32)))
    idx = jnp.arange(RET_CHUNK, dtype=F32)
    diff = idx[:, None] - idx[None, :]
    dmask = jnp.where(diff >= 0, jnp.exp(jnp.maximum(diff, 0.0)[None] * log_g[:, None, None]), 0.0)
    q_decay = jnp.exp((idx + 1.0)[None, :] * log_g[:, None])[..., None]
    k_decay = jnp.exp((RET_CHUNK - 1.0 - idx)[None, :] * log_g[:, None])[..., None]
    chunk_decay = jnp.exp(RET_CHUNK * log_g)[:, None, None]
    full = (RET_HEADS, RET_CHUNK, LANES)
    return (cos_full, sin_signed, dmask, jnp.broadcast_to(q_decay, full),
            jnp.broadcast_to(k_decay, full), jnp.broadcast_to(chunk_decay, full))


def _retention(ret_in, ret_gain):
    tokens = ret_in.shape[0]
    batch = tokens // SEQ
    per_seq = SEQ // RET_ROWS
    cos, sin, dmask, qdec, kdec, cdec = _retention_tables()
    tab = pl.BlockSpec((RET_ROWS, RET_QK_WIDTH), lambda b, j: (j, 0))
    const3 = pl.BlockSpec((RET_HEADS, RET_CHUNK, LANES), lambda b, j: (0, 0, 0))
    return pl.pallas_call(
        _retention_kernel,
        out_shape=jax.ShapeDtypeStruct((tokens, RET_WIDTH), BF16),
        grid=(batch, per_seq),
        in_specs=[
            pl.BlockSpec((RET_ROWS, RET_IN_WIDTH), lambda b, j: (b * per_seq + j, 0)),
            tab, tab, const3, const3, const3, const3,
            pl.BlockSpec((1, RET_WIDTH), lambda b, j: (0, 0)),
        ],
        out_specs=pl.BlockSpec((RET_ROWS, RET_WIDTH), lambda b, j: (b * per_seq + j, 0)),
        scratch_shapes=[pltpu.VMEM((RET_HEADS, LANES, RET_VALUE_DIM), F32)],
        compiler_params=_cparams(("arbitrary", "arbitrary")),
        name="retention",
    )(ret_in, cos, sin, dmask, qdec, kdec, cdec, ret_gain.reshape(1, RET_WIDTH))


def _out_proj_kernel(attn_ref, ret_ref, x_ref, mod_ref, w_ref, out_ref):
    mix = _dot(attn_ref[...], w_ref[:ATTN_WIDTH, :]) + _dot(ret_ref[...], w_ref[ATTN_WIDTH:, :])
    out_ref[...] = x_ref[...] + mod_ref[2:3, :] * mix


def _out_proj(attn, ret, x, mod, w_out):
    tokens = x.shape[0]
    tm = TM_FFN
    per_seq = SEQ // tm
    row = lambda i: (i, 0)
    return pl.pallas_call(
        _out_proj_kernel,
        out_shape=jax.ShapeDtypeStruct((tokens, D_MODEL), F32),
        grid=(tokens // tm,),
        in_specs=[pl.BlockSpec((tm, ATTN_WIDTH), row),
                  pl.BlockSpec((tm, RET_WIDTH), row),
                  pl.BlockSpec((tm, D_MODEL), row),
                  pl.BlockSpec((None, 6, D_MODEL), lambda i: (i // per_seq, 0, 0)),
                  pl.BlockSpec((D_MODEL, D_MODEL), lambda i: (0, 0))],
        out_specs=pl.BlockSpec((tm, D_MODEL), row),
        compiler_params=_cparams(("arbitrary",)),
        name="out_projection",
    )(attn, ret, x, mod, w_out.astype(BF16))


def _swiglu_chunks(h, w1_ref, w3_ref, w2_ref):
    d_ff = w1_ref.shape[-1]
    total = None
    for c0 in range(0, d_ff, MXU_COLS):
        c1 = min(c0 + MXU_COLS, d_ff)
        z = (_silu(_dot(h, w1_ref[:, c0:c1])) * _dot(h, w3_ref[:, c0:c1])).astype(BF16)
        part = _dot(z, w2_ref[c0:c1, :])
        total = part if total is None else total + part
    return total


def _ffn_kernel(x_ref, mod_ref, gain_ref, w1_ref, w3_ref, w2_ref, out_ref):
    mod = mod_ref[...]
    x = x_ref[...]
    h = _modulated_norm(x, gain_ref[...], mod[4:5], mod[3:4]).astype(BF16)
    out_ref[...] = x + mod[5:6, :] * _swiglu_chunks(h, w1_ref, w3_ref, w2_ref)


def _dense_ffn(x, mod, gain, w1, w3, w2):
    tokens = x.shape[0]
    d_ff = w1.shape[1]
    tm = TM_FFN
    per_seq = SEQ // tm
    resident = pl.Buffered(1)
    return pl.pallas_call(
        _ffn_kernel,
        out_shape=jax.ShapeDtypeStruct((tokens, D_MODEL), F32),
        grid=(tokens // tm,),
        in_specs=[
            pl.BlockSpec((tm, D_MODEL), lambda i: (i, 0)),
            pl.BlockSpec((None, 6, D_MODEL), lambda i: (i // per_seq, 0, 0)),
            pl.BlockSpec((1, D_MODEL), lambda i: (0, 0)),
            pl.BlockSpec((D_MODEL, d_ff), lambda i: (0, 0), pipeline_mode=resident),
            pl.BlockSpec((D_MODEL, d_ff), lambda i: (0, 0), pipeline_mode=resident),
            pl.BlockSpec((d_ff, D_MODEL), lambda i: (0, 0), pipeline_mode=resident),
        ],
        out_specs=pl.BlockSpec((tm, D_MODEL), lambda i: (i, 0)),
        compiler_params=_cparams(("arbitrary",)),
        name="dense_swiglu",
    )(x, mod, gain.reshape(1, D_MODEL), w1.astype(BF16), w3.astype(BF16), w2.astype(BF16))


def _router_kernel(x_ref, mod_ref, gain_ref, wr_ref, h_ref, gates_ref, pos_ref, post_ref, start_ref,
                   cnt_ref, carry_ref, *, tiles_per_seq):
    i = pl.program_id(0)

    @pl.when(i % tiles_per_seq == 0)
    def _():
        carry_ref[...] = jnp.zeros_like(carry_ref)

    mod = mod_ref[...]
    h = _modulated_norm(x_ref[...], gain_ref[...], mod[4:5], mod[3:4]).astype(BF16)
    h_ref[...] = h
    tm = h.shape[0]
    lane = lax.broadcasted_iota(jnp.int32, (tm, LANES), 1).astype(F32)
    logits = jnp.where(lane < N_EXPERTS, _dot(h, wr_ref[...]), -jnp.inf)
    m1 = jnp.max(logits, axis=-1, keepdims=True)
    i1 = jnp.min(jnp.where(logits == m1, lane, float(LANES)), axis=-1, keepdims=True)
    rest = jnp.where(lane == i1, -jnp.inf, logits)
    m2 = jnp.max(rest, axis=-1, keepdims=True)
    i2 = jnp.min(jnp.where(rest == m2, lane, float(LANES)), axis=-1, keepdims=True)
    e2 = jnp.exp(m2 - m1)
    g1 = 1.0 / (1.0 + e2)
    g2 = e2 / (1.0 + e2)
    gates_ref[...] = jnp.where(lane == i1, g1, 0.0) + jnp.where(lane == i2, g2, 0.0)
    chosen = (lane == i1) | (lane == i2)
    onehot = jnp.where(chosen, 1.0, 0.0)
    r = lax.broadcasted_iota(jnp.int32, (tm, tm), 0)
    c = lax.broadcasted_iota(jnp.int32, (tm, tm), 1)
    tril = jnp.where(c <= r, 1.0, 0.0).astype(BF16)
    incl = _dot(tril, onehot.astype(BF16))
    carry = carry_ref[0:1, :]
    start_ref[...] = carry_ref[...]
    pos = jnp.where(chosen, incl - 1.0 + carry, -1.0)
    pos_ref[...] = pos
    post_ref[...] = pos.T[:N_EXPERTS, :]
    total = carry + incl[tm - 1:tm, :]
    carry_ref[...] = jnp.broadcast_to(total, carry_ref.shape)
    cnt_ref[...] = jnp.broadcast_to(total, cnt_ref.shape)


def _router(x, mod, gain, w_router):
    tokens = x.shape[0]
    batch = tokens // SEQ
    tm = TM_PROJ
    per_seq = SEQ // tm
    wr = jnp.zeros((D_MODEL, LANES), BF16).at[:, :N_EXPERTS].set(w_router.astype(BF16))
    row = lambda i: (i, 0)
    return pl.pallas_call(
        functools.partial(_router_kernel, tiles_per_seq=per_seq),
        out_shape=(
            jax.ShapeDtypeStruct((tokens, D_MODEL), BF16),
            jax.ShapeDtypeStruct((tokens, LANES), F32),
            jax.ShapeDtypeStruct((tokens, LANES), F32),
            jax.ShapeDtypeStruct((N_EXPERTS, tokens), F32),
            jax.ShapeDtypeStruct((tokens // tm, 8, LANES), F32),
            jax.ShapeDtypeStruct((batch, 8, LANES), F32),
        ),
        grid=(tokens // tm,),
        in_specs=[
            pl.BlockSpec((tm, D_MODEL), row),
            pl.BlockSpec((None, 6, D_MODEL), lambda i: (i // per_seq, 0, 0)),
            pl.BlockSpec((1, D_MODEL), lambda i: (0, 0)),
            pl.BlockSpec((D_MODEL, LANES), lambda i: (0, 0)),
        ],
        out_specs=(
            pl.BlockSpec((tm, D_MODEL), row),
            pl.BlockSpec((tm, LANES), row),
            pl.BlockSpec((tm, LANES), row),
            pl.BlockSpec((N_EXPERTS, tm), lambda i: (0, i)),
            pl.BlockSpec((None, 8, LANES), lambda i: (i, 0, 0)),
            pl.BlockSpec((None, 8, LANES), lambda i: (i // per_seq, 0, 0)),
        ),
        scratch_shapes=[pltpu.VMEM((8, LANES), F32)],
        compiler_params=_cparams(("arbitrary",)),
        name="expert_router",
    )(x, mod, gain.reshape(1, D_MODEL), wr)


TOK_BLOCKS = SEQ // TR_MOE
TAIL_TILES = (64, 128, TR_MOE)
SPILL_ROWS = 16


def _moe_kernel(tbl_ref, h_ref, post_ref, pos_ref, gates_ref, x_ref, mod_ref, w1_ref, w3_ref, w2_ref,
                out_ref, hs_ref, acc_ref):
    b, e, f = pl.program_id(0), pl.program_id(1), pl.program_id(2)
    last_f = pl.num_programs(2) - 1
    tr = TR_MOE
    base = b * (TOK_BLOCKS + 1) * N_EXPERTS + e
    bounds = [tbl_ref[base + tb * N_EXPERTS] for tb in range(TOK_BLOCKS + 1)]
    win = [pl.multiple_of(lax.shift_left(lax.shift_right_logical(s, 3), 3), SUBLANES) for s in bounds[:-1]]
    spills = [bounds[tb + 1] - win[tb] > tr for tb in range(TOK_BLOCKS)]
    total = bounds[-1]
    n_tiles = lax.shift_right_logical(total + (tr - 1), tr.bit_length() - 1)

    def clear(r, carry):
        acc_ref[pl.ds(pl.multiple_of(r * tr, tr), tr), :] = jnp.zeros((tr, D_MODEL), F32)
        return carry

    @pl.when((e == 0) & (f == 0))
    def _():
        out_ref[...] = jnp.zeros_like(out_ref)

    @pl.when(f == 0)
    def _():
        out_ref[pl.ds(pl.multiple_of(e * tr, tr), tr), :] += x_ref[...]
        lax.fori_loop(0, n_tiles + 2, clear, 0)
        slot = lax.broadcasted_iota(jnp.int32, (tr + SPILL_ROWS, tr), 0).astype(F32)
        for tb in range(TOK_BLOCKS):
            ts = slice(tb * tr, (tb + 1) * tr)
            local = post_ref[pl.ds(e, 1), ts] - win[tb].astype(F32)
            sel = jnp.where(local == slot, 1.0, 0.0).astype(BF16)
            acc_ref[pl.ds(win[tb], tr + SPILL_ROWS), :] += _dot(sel, h_ref[ts, :])

        def to_rows(r, carry):
            rows = pl.ds(pl.multiple_of(r * tr, tr), tr)
            hs_ref[rows, :] = acc_ref[rows, :].astype(BF16)
            acc_ref[rows, :] = jnp.zeros((tr, D_MODEL), F32)
            return carry
        lax.fori_loop(0, n_tiles + 2, to_rows, 0)

    def swiglu(r0, m):
        rows = pl.ds(r0, m)
        hr = hs_ref[rows, :]
        z = (_silu(_dot(hr, w1_ref[...])) * _dot(hr, w3_ref[...])).astype(BF16)
        acc_ref[rows, :] += _dot(z, w2_ref[...])

    def full_tile(r, carry):
        swiglu(pl.multiple_of(r * tr, tr), tr)
        return carry
    n_full = lax.shift_right_logical(total, tr.bit_length() - 1)
    lax.fori_loop(0, n_full, full_tile, 0)
    rest = total - n_full * tr
    tail0 = pl.multiple_of(n_full * tr, tr)
    lo = 0
    for m in TAIL_TILES:
        @pl.when((rest > lo) & (rest <= m))
        def _():
            swiglu(tail0, m)
        lo = m

    @pl.when(f == last_f)
    def _():
        lane = lax.broadcasted_iota(jnp.int32, (tr, LANES), 1)
        slot = lax.broadcasted_iota(jnp.int32, (tr, tr), 1).astype(F32)
        slot_spill = lax.broadcasted_iota(jnp.int32, (tr, SPILL_ROWS), 1).astype(F32) + float(tr)
        layer_gate = mod_ref[5:6, :]

        def columns(tb):
            ts = slice(tb * tr, (tb + 1) * tr)
            pos_col = jnp.sum(jnp.where(lane == e, pos_ref[ts, :], 0.0), axis=-1, keepdims=True)
            gate_col = jnp.sum(jnp.where(lane == e, gates_ref[ts, :], 0.0), axis=-1, keepdims=True)
            return ts, pos_col - win[tb].astype(F32), gate_col

        for tb in range(TOK_BLOCKS):
            ts, local, gate_col = columns(tb)
            rows = acc_ref[pl.ds(win[tb], tr), :].astype(BF16)
            sel = jnp.where(local == slot, 1.0, 0.0).astype(BF16)
            out_ref[ts, :] += layer_gate * (gate_col * _dot(sel, rows))

        for tb in range(TOK_BLOCKS):
            @pl.when(spills[tb])
            def _():
                ts, local, gate_col = columns(tb)
                rows2 = acc_ref[pl.ds(win[tb] + tr, SPILL_ROWS), :].astype(BF16)
                sel2 = jnp.where(local == slot_spill, 1.0, 0.0).astype(BF16)
                out_ref[ts, :] += layer_gate * (gate_col * _dot(sel2, rows2))


def _cast_kernel(w_ref, o_ref):
    o_ref[...] = w_ref[...].astype(BF16)


def _chunk_columns(w, tf):
    n_e, k, f = w.shape
    return pl.pallas_call(
        _cast_kernel,
        out_shape=jax.ShapeDtypeStruct((n_e, f // tf, k, tf), BF16),
        grid=(n_e, f // tf),
        in_specs=[pl.BlockSpec((None, k, tf), lambda e, c: (e, 0, c))],
        out_specs=pl.BlockSpec((None, None, k, tf), lambda e, c: (e, c, 0, 0)),
        compiler_params=_cparams(("arbitrary", "arbitrary")),
        name="chunk_expert_weights",
    )(w)


def _moe(x, mod, h, post, pos, gates, table, w1, w3, w2):
    tokens = h.shape[0]
    batch = tokens // SEQ
    d_ff = w1.shape[2]
    tf = TF_MOE
    n_f = d_ff // tf
    assert N_EXPERTS == TOK_BLOCKS
    max_rows = SEQ + 3 * TR_MOE
    grid_spec = pltpu.PrefetchScalarGridSpec(
        num_scalar_prefetch=1,
        grid=(batch, N_EXPERTS, n_f),
        in_specs=[
            pl.BlockSpec((SEQ, D_MODEL), lambda b, e, f, t: (b, 0)),
            pl.BlockSpec((N_EXPERTS, SEQ), lambda b, e, f, t: (0, b)),
            pl.BlockSpec((SEQ, LANES), lambda b, e, f, t: (b, 0)),
            pl.BlockSpec((SEQ, LANES), lambda b, e, f, t: (b, 0)),
            pl.BlockSpec((TR_MOE, D_MODEL), lambda b, e, f, t: (b * TOK_BLOCKS + e, 0)),
            pl.BlockSpec((None, 6, D_MODEL), lambda b, e, f, t: (b, 0, 0)),
            pl.BlockSpec((None, None, D_MODEL, tf), lambda b, e, f, t: (e, f, 0, 0)),
            pl.BlockSpec((None, None, D_MODEL, tf), lambda b, e, f, t: (e, f, 0, 0)),
            pl.BlockSpec((None, tf, D_MODEL), lambda b, e, f, t: (e, f, 0)),
        ],
        out_specs=pl.BlockSpec((SEQ, D_MODEL), lambda b, e, f, t: (b, 0), pipeline_mode=pl.Buffered(1)),
        scratch_shapes=[pltpu.VMEM((max_rows, D_MODEL), BF16), pltpu.VMEM((max_rows, D_MODEL), F32)],
    )
    return pl.pallas_call(
        _moe_kernel,
        out_shape=jax.ShapeDtypeStruct((tokens, D_MODEL), F32),
        grid_spec=grid_spec,
        compiler_params=_cparams(("arbitrary", "arbitrary", "arbitrary")),
        name="expert_swiglu",
    )(table, h, post, pos, gates, x, mod, _chunk_columns(w1, tf), _chunk_columns(w3, tf), w2.astype(BF16))


def _token_mixer(x, mod, gain, w_in, q_gain, k_gain, ret_gain, w_out, bias_masks):
    *qkv, ret_in = _in_proj(x, mod, gain, w_in, q_gain, k_gain)
    attn = _dilated_attention(qkv, bias_masks)
    ret = _retention(ret_in, ret_gain)
    return _out_proj(attn, ret, x, mod, w_out)


def _moe_ffn(x, mod, gain, w_router, w1, w3, w2):
    h, gates, pos, post, start, cnt = _router(x, mod, gain, w_router)
    batch = cnt.shape[0]
    table = jnp.concatenate([start[:, 0, :N_EXPERTS].reshape(batch, TOK_BLOCKS, N_EXPERTS),
                             cnt[:, :1, :N_EXPERTS]], axis=1).astype(jnp.int32).reshape(-1)
    return _moe(x, mod, h, post, pos, gates, table, w1, w3, w2)


def kernel(x, c, rel_bias_table, norm_mix, norm_ffn, w_mod, b_mod, w_in, q_gain, k_gain, ret_gain, w_out,
           ffn_w1, ffn_w3, ffn_w2, moe_router, moe_w1, moe_w3, moe_w2):
    batch, seq, d_model = x.shape
    assert (seq, d_model) == (SEQ, D_MODEL)
    depth = w_mod.shape[0]
    mods = _modulation(c, w_mod, b_mod).reshape(depth, batch, 6, D_MODEL)
    bias_masks = _bias_masks(rel_bias_table)
    xt = x.reshape(batch * seq, d_model)
    for layer in range(depth):
        mod = mods[layer]
        xt = _token_mixer(xt, mod, norm_mix[layer], w_in[layer], q_gain[layer], k_gain[layer],
                          ret_gain[layer], w_out[layer], bias_masks)
        i = layer // 2
        if layer % 2 == 0:
            xt = _dense_ffn(xt, mod, norm_ffn[layer], ffn_w1[i], ffn_w3[i], ffn_w2[i])
        else:
            xt = _moe_ffn(xt, mod, norm_ffn[layer], moe_router[i], moe_w1[i], moe_w3[i], moe_w2[i])
    return xt.reshape(batch, seq, d_model)
```

```python
import functools
import math

import jax
import jax.numpy as jnp
import numpy as np
from jax import lax
from jax.experimental import pallas as pl
from jax.experimental.pallas import tpu as pltpu

D_MODEL = 1024
SEQ = 2048
ATTN_HEADS = 8
ATTN_HEAD_DIM = 64
ATTN_WIDTH = ATTN_HEADS * ATTN_HEAD_DIM
DILATED_PATTERNS = ((128, 1), (512, 4), (2048, 16))
BLOCK = 128
NUM_BUCKETS = 32
MAX_DISTANCE = 2048
RET_HEADS = 4
RET_KEY_DIM = 64
RET_VALUE_DIM = 128
RET_WIDTH = RET_HEADS * RET_VALUE_DIM
RET_QK_WIDTH = RET_HEADS * RET_KEY_DIM
RET_CHUNK = 128
ROPE_BASE = 10000.0
IN_WIDTH = 3 * ATTN_WIDTH + 2 * RET_QK_WIDTH + 2 * RET_WIDTH
RET_IN_WIDTH = IN_WIDTH - 3 * ATTN_WIDTH
N_EXPERTS = 8
EPS = 1e-6
NEG_INF = -1e30

LANES = 128
VMEM_LIMIT = 60 * 1024 * 1024

BF16 = jnp.bfloat16
F32 = jnp.float32

TM_PROJ = 256
TM_FFN = 512
MXU_COLS = 256
TR_MOE = 256
TF_MOE = 1792
assert TM_PROJ == TR_MOE
RET_ROWS = 512
SUBLANES = 8


def _cparams(sem):
    return pltpu.CompilerParams(dimension_semantics=sem, vmem_limit_bytes=VMEM_LIMIT)


def _dot(a, b):
    return jnp.dot(a, b, preferred_element_type=F32)


def _dot_nt(a, b):
    return lax.dot_general(a, b, (((1,), (1,)), ((), ())), preferred_element_type=F32)


def _dot_tn(a, b):
    return lax.dot_general(a, b, (((0,), (0,)), ((), ())), preferred_element_type=F32)


def _split_bf16(v):
    hi = v.astype(BF16)
    lo = (v - hi.astype(F32)).astype(BF16)
    return hi, lo


def _silu(v):
    return v * (1.0 / (1.0 + jnp.exp(-v)))


def _modulated_norm(x, gain, scale, shift):
    ms = jnp.mean(x * x, axis=-1, keepdims=True)
    y = x * lax.rsqrt(ms + EPS) * gain
    return y * (1.0 + scale) + shift


def _mod_kernel(c_ref, w_ref, b_ref, o_ref):
    ca = _silu(c_ref[...]).astype(BF16)
    o_ref[...] = _dot(ca, w_ref[...].astype(BF16)) + b_ref[...]


def _modulation(c, w_mod, b_mod):
    depth, _, width = w_mod.shape
    batch = c.shape[0]
    tn = 1536
    return pl.pallas_call(
        _mod_kernel,
        out_shape=jax.ShapeDtypeStruct((depth, batch, width), F32),
        grid=(depth, width // tn),
        in_specs=[
            pl.BlockSpec((batch, D_MODEL), lambda l, n: (0, 0)),
            pl.BlockSpec((None, D_MODEL, tn), lambda l, n: (l, 0, n)),
            pl.BlockSpec((None, 1, tn), lambda l, n: (l, 0, n)),
        ],
        out_specs=pl.BlockSpec((None, batch, tn), lambda l, n: (l, 0, n)),
        compiler_params=_cparams(("arbitrary", "arbitrary")),
        name="adaln_modulation",
    )(c, w_mod, b_mod.reshape(depth, 1, width))


def _bias_kernel(table_ref, bucket_ref, o_ref):
    h = pl.program_id(1)
    bucket = bucket_ref[...]
    acc = jnp.full(bucket.shape, NEG_INF, F32)
    for b in range(NUM_BUCKETS):
        acc = jnp.where(bucket == b, table_ref[b, h], acc)
    o_ref[...] = acc


BIAS_FULL = {1: 0, 4: 2}
BIAS_FIRST = {1: 1, 4: 3, 16: 4}
N_BIAS_SETS = 5


def _bias_masks(rel_bias_table):
    i = jnp.arange(BLOCK)[:, None]
    j = jnp.arange(2 * BLOCK)[None, :]
    max_exact = NUM_BUCKETS // 2

    def bucket_of(rel, dilation, w_sub, exists):
        n = jnp.maximum(rel * dilation, 0)
        nf = jnp.maximum(n.astype(F32), float(max_exact))
        large = max_exact + (jnp.log(nf / max_exact) / math.log(MAX_DISTANCE / max_exact)
                             * (NUM_BUCKETS - max_exact)).astype(jnp.int32)
        large = jnp.minimum(large, NUM_BUCKETS - 1)
        bucket = jnp.where(n < max_exact, n, large)
        allowed = (rel >= 0) & (rel <= w_sub) & exists
        return jnp.where(allowed, bucket, -1)

    sets = [None] * N_BIAS_SETS
    for window, dilation in DILATED_PATTERNS:
        w_sub = window // dilation
        if dilation in BIAS_FULL:
            sets[BIAS_FULL[dilation]] = bucket_of(i - j + BLOCK, dilation, w_sub, j >= 0)
        sets[BIAS_FIRST[dilation]] = bucket_of(i - j, dilation, w_sub, j < BLOCK)
    buckets = jnp.stack(sets).astype(jnp.int32)
    return pl.pallas_call(
        _bias_kernel,
        out_shape=jax.ShapeDtypeStruct((N_BIAS_SETS, ATTN_HEADS // 2, 2 * BLOCK, 2 * BLOCK), F32),
        grid=(N_BIAS_SETS, ATTN_HEADS),
        in_specs=[
            pl.BlockSpec(memory_space=pltpu.SMEM),
            pl.BlockSpec((None, BLOCK, 2 * BLOCK), lambda p, h: (p, 0, 0)),
        ],
        out_specs=pl.BlockSpec((None, None, BLOCK, 2 * BLOCK), lambda p, h: (p, h // 2, h % 2, 0)),
        compiler_params=_cparams(("arbitrary", "arbitrary")),
        name="relative_bias_masks",
    )(rel_bias_table, buckets)


HALF = ATTN_WIDTH // 2


def _in_proj_kernel(x_ref, mod_ref, gain_ref, w_ref, qg_ref, kg_ref, grp_ref,
                    q1_ref, k1_ref, v1_ref, q4_ref, k4_ref, v4_ref, q16_ref, k16_ref, v16_ref, r_ref,
                    perm_ref):
    mod = mod_ref[...]
    h = _modulated_norm(x_ref[...], gain_ref[...], mod[1:2], mod[0:1]).astype(BF16)
    proj = _dot(h, w_ref[...])
    grp = grp_ref[...]
    tm = proj.shape[0]

    def head_norm(t, gain):
        hi, lo = _split_bf16(t * t)
        ss = _dot(hi, grp) + _dot(lo, grp)
        return t * lax.rsqrt(ss * (1.0 / ATTN_HEAD_DIM) + EPS) * gain

    def emit(t, o1_ref, o4_ref, o16_ref):
        o1_ref[...] = t.astype(BF16)
        for j in range(ATTN_WIDTH // LANES):
            perm_ref[j] = t[:, j * LANES:(j + 1) * LANES]
        for dil, o_ref in ((4, o4_ref), (16, o16_ref)):
            for hf in range(2):
                for r in range(dil):
                    for jj in range(HALF // LANES):
                        c0 = (hf * dil + r) * HALF + jj * LANES
                        o_ref[:, c0:c0 + LANES] = perm_ref[hf * (HALF // LANES) + jj,
                                                           pl.ds(r, tm // dil, stride=dil), :].astype(BF16)

    emit(head_norm(proj[:, :ATTN_WIDTH], qg_ref[...]) * (ATTN_HEAD_DIM ** -0.5), q1_ref, q4_ref, q16_ref)
    emit(head_norm(proj[:, ATTN_WIDTH:2 * ATTN_WIDTH], kg_ref[...]), k1_ref, k4_ref, k16_ref)
    emit(proj[:, 2 * ATTN_WIDTH:3 * ATTN_WIDTH], v1_ref, v4_ref, v16_ref)
    r_ref[...] = proj[:, 3 * ATTN_WIDTH:]


def _in_proj(x, mod, gain, w_in, q_gain, k_gain):
    tokens = x.shape[0]
    tm = TM_PROJ
    per_seq = SEQ // tm
    grp = np.kron(np.eye(ATTN_HEADS), np.ones((ATTN_HEAD_DIM, ATTN_HEAD_DIM))).astype(np.float32)
    row = lambda i: (i, 0)
    const = lambda i: (0, 0)
    layouts = []
    for dil in (1, 4, 16):
        shape = jax.ShapeDtypeStruct((tokens // dil, dil * ATTN_WIDTH), BF16)
        spec = pl.BlockSpec((tm // dil, dil * ATTN_WIDTH), row)
        layouts.append(((shape,) * 3, (spec,) * 3))
    out_shape = sum((s for s, _ in layouts), ()) + (jax.ShapeDtypeStruct((tokens, RET_IN_WIDTH), F32),)
    out_specs = sum((s for _, s in layouts), ()) + (pl.BlockSpec((tm, RET_IN_WIDTH), row),)
    return pl.pallas_call(
        _in_proj_kernel,
        out_shape=out_shape,
        grid=(tokens // tm,),
        in_specs=[
            pl.BlockSpec((tm, D_MODEL), row),
            pl.BlockSpec((None, 6, D_MODEL), lambda i: (i // per_seq, 0, 0)),
            pl.BlockSpec((1, D_MODEL), const),
            pl.BlockSpec((D_MODEL, IN_WIDTH), const),
            pl.BlockSpec((1, ATTN_WIDTH), const),
            pl.BlockSpec((1, ATTN_WIDTH), const),
            pl.BlockSpec((ATTN_WIDTH, ATTN_WIDTH), const),
        ],
        out_specs=out_specs,
        scratch_shapes=[pltpu.VMEM((ATTN_WIDTH // LANES, tm, LANES), F32)],
        compiler_params=_cparams(("arbitrary",)),
        name="in_projection",
    )(x, mod, gain.reshape(1, D_MODEL), w_in.astype(BF16),
      jnp.tile(q_gain, ATTN_HEADS).reshape(1, ATTN_WIDTH),
      jnp.tile(k_gain, ATTN_HEADS).reshape(1, ATTN_WIDTH),
      jnp.asarray(grp, BF16))


PAIRS_PER_HALF = ATTN_HEADS // 4
GROUP = 4


def _pair_scores(qp, kp, vp, bias2, masks, low):
    q2 = jnp.concatenate([qp * masks[0], qp * masks[1]], axis=0)
    s = _dot_nt(q2, kp) + bias2
    m = jnp.max(s, axis=-1, keepdims=True)
    p = jnp.exp(s - m)
    den = jnp.sum(p, axis=-1, keepdims=True)
    pv = _dot(p.astype(BF16), vp)
    pick = lambda t: jnp.where(low, t[:BLOCK], t[BLOCK:])
    return pick(pv), pick(m), pick(den)


def _attn_kernel(q1_ref, k1_ref, v1_ref, q4_ref, k4_ref, v4_ref, q16_ref, k16_ref, v16_ref, bm_ref,
                 o_ref, acc_ref, max_ref, den_ref):
    lane = lax.broadcasted_iota(jnp.int32, (BLOCK, LANES), 1)
    low = lane < ATTN_HEAD_DIM
    masks = (jnp.where(low, 1.0, 0.0).astype(BF16), jnp.where(low, 0.0, 1.0).astype(BF16))

    def block(q_ref, k_ref, v_ref, c0, q0, w0, width, bias_set, rows, first):
        for p in range(PAIRS_PER_HALF):
            cs = slice(c0 + p * LANES, c0 + (p + 1) * LANES)
            acc, m, den = _pair_scores(q_ref[pl.ds(q0, BLOCK), cs], k_ref[pl.ds(w0, width), cs],
                                       v_ref[pl.ds(w0, width), cs], bm_ref[bias_set, p, :, 0:width],
                                       masks, low)
            if not first:
                m_old = max_ref[p, rows, :]
                m_new = jnp.maximum(m_old, m)
                a, b = jnp.exp(m_old - m_new), jnp.exp(m - m_new)
                den = den_ref[p, rows, :] * a + den * b
                acc = acc_ref[p, rows, :] * a + acc * b
                m = m_new
            max_ref[p, rows, :] = m
            den_ref[p, rows, :] = den
            acc_ref[p, rows, :] = acc

    def d1_group(g, carry):
        for u in range(GROUP):
            n = g * GROUP + u
            q0 = pl.multiple_of(n * BLOCK, BLOCK)
            w0 = pl.multiple_of(jnp.maximum(n - 1, 0) * BLOCK, BLOCK)
            bias_set = jnp.where(n == 0, BIAS_FIRST[1], BIAS_FULL[1])
            block(q1_ref, k1_ref, v1_ref, 0, q0, w0, 2 * BLOCK, bias_set, pl.ds(q0, BLOCK), True)
        return carry
    lax.fori_loop(0, SEQ // BLOCK // GROUP, d1_group, 0)

    for r in range(4):
        for n in range(SEQ // 4 // BLOCK):
            block(q4_ref, k4_ref, v4_ref, r * HALF, n * BLOCK, max(n - 1, 0) * BLOCK, 2 * BLOCK,
                  BIAS_FIRST[4] if n == 0 else BIAS_FULL[4],
                  pl.ds(r + 4 * BLOCK * n, BLOCK, stride=4), False)

    for r in range(16):
        block(q16_ref, k16_ref, v16_ref, r * HALF, 0, 0, BLOCK, BIAS_FIRST[16],
              pl.ds(r, BLOCK, stride=16), False)

    for n in range(SEQ // BLOCK):
        rows = slice(n * BLOCK, (n + 1) * BLOCK)
        for p in range(PAIRS_PER_HALF):
            o_ref[rows, p * LANES:(p + 1) * LANES] = (acc_ref[p, rows, :] / den_ref[p, rows, :]).astype(BF16)


def _dilated_attention(qkv, bias_masks):
    tokens = qkv[0].shape[0]
    batch = tokens // SEQ
    specs = []
    for dil in (1, 4, 16):
        specs += [pl.BlockSpec((SEQ // dil, dil * HALF), lambda b, hf: (b, hf))] * 3
    state = pltpu.VMEM((PAIRS_PER_HALF, SEQ, LANES), F32)
    return pl.pallas_call(
        _attn_kernel,
        out_shape=jax.ShapeDtypeStruct((tokens, ATTN_WIDTH), BF16),
        grid=(batch, 2),
        in_specs=specs + [pl.BlockSpec((N_BIAS_SETS, PAIRS_PER_HALF, 2 * BLOCK, 2 * BLOCK),
                                       lambda b, hf: (0, hf, 0, 0))],
        out_specs=pl.BlockSpec((SEQ, HALF), lambda b, hf: (b, hf)),
        scratch_shapes=[state, state, state],
        compiler_params=_cparams(("arbitrary", "arbitrary")),
        name="dilated_attention",
    )(*qkv, bias_masks)


def _retention_kernel(r_ref, cos_ref, sin_ref, dmask_ref, qdec_ref, kdec_ref, cdec_ref, gain_ref,
                      o_ref, state_ref):
    @pl.when(pl.program_id(1) == 0)
    def _():
        state_ref[...] = jnp.zeros_like(state_ref)

    lane = lax.broadcasted_iota(jnp.int32, (RET_CHUNK, LANES), 1)
    low = lane < RET_KEY_DIM
    first_half = (lane % RET_KEY_DIM) < (RET_KEY_DIM // 2)

    def rotate(t, cos, sin):
        partner = jnp.where(first_half, pltpu.roll(t, LANES - RET_KEY_DIM // 2, 1),
                            pltpu.roll(t, RET_KEY_DIM // 2, 1))
        return t * cos + partner * sin

    for c in range(RET_ROWS // RET_CHUNK):
        rows = slice(c * RET_CHUNK, (c + 1) * RET_CHUNK)
        for hp in range(RET_HEADS // 2):
            qs = slice(hp * LANES, (hp + 1) * LANES)
            ks = slice(RET_QK_WIDTH + hp * LANES, RET_QK_WIDTH + (hp + 1) * LANES)
            cos, sin = cos_ref[rows, qs], sin_ref[rows, qs]
            q_pair = rotate(r_ref[rows, qs], cos, sin) * (RET_KEY_DIM ** -0.5)
            k_pair = rotate(r_ref[rows, ks], cos, sin)
            for hh in range(2):
                head = 2 * hp + hh
                vs = slice(2 * RET_QK_WIDTH + head * LANES, 2 * RET_QK_WIDTH + (head + 1) * LANES)
                gs = slice(2 * RET_QK_WIDTH + RET_WIDTH + head * LANES,
                           2 * RET_QK_WIDTH + RET_WIDTH + (head + 1) * LANES)
                keep = low if hh == 0 else jnp.logical_not(low)
                qm = jnp.where(keep, q_pair, 0.0)
                vb = r_ref[rows, vs].astype(BF16)
                state = state_ref[head]
                inner = _dot_nt(qm.astype(BF16), k_pair.astype(BF16)) * dmask_ref[head]
                y = _dot(inner.astype(BF16), vb)
                y = y + _dot((qm * qdec_ref[head]).astype(BF16), state.astype(BF16))
                state_ref[head] = state * cdec_ref[head] + _dot_tn((k_pair * kdec_ref[head]).astype(BF16), vb)
                mu = jnp.mean(y, axis=-1, keepdims=True)
                yc = y - mu
                var = jnp.mean(yc * yc, axis=-1, keepdims=True)
                yn = yc * lax.rsqrt(var + EPS) * gain_ref[:, head * LANES:(head + 1) * LANES]
                o_ref[rows, head * LANES:(head + 1) * LANES] = (_silu(r_ref[rows, gs]) * yn).astype(BF16)


def _retention_tables():
    half = RET_KEY_DIM // 2
    pos = jnp.arange(SEQ, dtype=F32)
    inv = ROPE_BASE ** (-jnp.arange(half, dtype=F32) / half)
    ang = pos[:, None] * inv[None, :]
    cos, sin = jnp.cos(ang), jnp.sin(ang)
    cos_full = jnp.tile(jnp.concatenate([cos, cos], axis=-1), (1, RET_HEADS))
    sin_signed = jnp.tile(jnp.concatenate([-sin, sin], axis=-1), (1, RET_HEADS))
    log_g = jnp.log(1.0 - 2.0 ** (-5.0 - jnp.arange(RET_HEADS, dtype=F32)))
    idx = jnp.arange(RET_CHUNK, dtype=F32)
    diff = idx[:, None] - idx[None, :]
    dmask = jnp.where(diff >= 0, jnp.exp(jnp.maximum(diff, 0.0)[None] * log_g[:, None, None]), 0.0)
    q_decay = jnp.exp((idx + 1.0)[None, :] * log_g[:, None])[..., None]
    k_decay = jnp.exp((RET_CHUNK - 1.0 - idx)[None, :] * log_g[:, None])[..., None]
    chunk_decay = jnp.exp(RET_CHUNK * log_g)[:, None, None]
    full = (RET_HEADS, RET_CHUNK, LANES)
    return (cos_full, sin_signed, dmask, jnp.broadcast_to(q_decay, full),
            jnp.broadcast_to(k_decay, full), jnp.broadcast_to(chunk_decay, full))


def _retention(ret_in, ret_gain):
    tokens = ret_in.shape[0]
    batch = tokens // SEQ
    per_seq = SEQ // RET_ROWS
    cos, sin, dmask, qdec, kdec, cdec = _retention_tables()
    tab = pl.BlockSpec((RET_ROWS, RET_QK_WIDTH), lambda b, j: (j, 0))
    const3 = pl.BlockSpec((RET_HEADS, RET_CHUNK, LANES), lambda b, j: (0, 0, 0))
    return pl.pallas_call(
        _retention_kernel,
        out_shape=jax.ShapeDtypeStruct((tokens, RET_WIDTH), BF16),
        grid=(batch, per_seq),
        in_specs=[
            pl.BlockSpec((RET_ROWS, RET_IN_WIDTH), lambda b, j: (b * per_seq + j, 0)),
            tab, tab, const3, const3, const3, const3,
            pl.BlockSpec((1, RET_WIDTH), lambda b, j: (0, 0)),
        ],
        out_specs=pl.BlockSpec((RET_ROWS, RET_WIDTH), lambda b, j: (b * per_seq + j, 0)),
        scratch_shapes=[pltpu.VMEM((RET_HEADS, LANES, RET_VALUE_DIM), F32)],
        compiler_params=_cparams(("arbitrary", "arbitrary")),
        name="retention",
    )(ret_in, cos, sin, dmask, qdec, kdec, cdec, ret_gain.reshape(1, RET_WIDTH))


def _out_proj_kernel(attn_ref, ret_ref, x_ref, mod_ref, w_ref, out_ref):
    mix = _dot(attn_ref[...], w_ref[:ATTN_WIDTH, :]) + _dot(ret_ref[...], w_ref[ATTN_WIDTH:, :])
    out_ref[...] = x_ref[...] + mod_ref[2:3, :] * mix


def _out_proj(attn, ret, x, mod, w_out):
    tokens = x.shape[0]
    tm = TM_FFN
    per_seq = SEQ // tm
    row = lambda i: (i, 0)
    return pl.pallas_call(
        _out_proj_kernel,
        out_shape=jax.ShapeDtypeStruct((tokens, D_MODEL), F32),
        grid=(tokens // tm,),
        in_specs=[pl.BlockSpec((tm, ATTN_WIDTH), row),
                  pl.BlockSpec((tm, RET_WIDTH), row),
                  pl.BlockSpec((tm, D_MODEL), row),
                  pl.BlockSpec((None, 6, D_MODEL), lambda i: (i // per_seq, 0, 0)),
                  pl.BlockSpec((D_MODEL, D_MODEL), lambda i: (0, 0))],
        out_specs=pl.BlockSpec((tm, D_MODEL), row),
        compiler_params=_cparams(("arbitrary",)),
        name="out_projection",
    )(attn, ret, x, mod, w_out.astype(BF16))


def _swiglu_chunks(h, w1_ref, w3_ref, w2_ref):
    d_ff = w1_ref.shape[-1]
    total = None
    for c0 in range(0, d_ff, MXU_COLS):
        c1 = min(c0 + MXU_COLS, d_ff)
        z = (_silu(_dot(h, w1_ref[:, c0:c1])) * _dot(h, w3_ref[:, c0:c1])).astype(BF16)
        part = _dot(z, w2_ref[c0:c1, :])
        total = part if total is None else total + part
    return total


def _ffn_kernel(x_ref, mod_ref, gain_ref, w1_ref, w3_ref, w2_ref, out_ref):
    mod = mod_ref[...]
    x = x_ref[...]
    h = _modulated_norm(x, gain_ref[...], mod[4:5], mod[3:4]).astype(BF16)
    out_ref[...] = x + mod[5:6, :] * _swiglu_chunks(h, w1_ref, w3_ref, w2_ref)


def _dense_ffn(x, mod, gain, w1, w3, w2):
    tokens = x.shape[0]
    d_ff = w1.shape[1]
    tm = TM_FFN
    per_seq = SEQ // tm
    resident = pl.Buffered(1)
    return pl.pallas_call(
        _ffn_kernel,
        out_shape=jax.ShapeDtypeStruct((tokens, D_MODEL), F32),
        grid=(tokens // tm,),
        in_specs=[
            pl.BlockSpec((tm, D_MODEL), lambda i: (i, 0)),
            pl.BlockSpec((None, 6, D_MODEL), lambda i: (i // per_seq, 0, 0)),
            pl.BlockSpec((1, D_MODEL), lambda i: (0, 0)),
            pl.BlockSpec((D_MODEL, d_ff), lambda i: (0, 0), pipeline_mode=resident),
            pl.BlockSpec((D_MODEL, d_ff), lambda i: (0, 0), pipeline_mode=resident),
            pl.BlockSpec((d_ff, D_MODEL), lambda i: (0, 0), pipeline_mode=resident),
        ],
        out_specs=pl.BlockSpec((tm, D_MODEL), lambda i: (i, 0)),
        compiler_params=_cparams(("arbitrary",)),
        name="dense_swiglu",
    )(x, mod, gain.reshape(1, D_MODEL), w1.astype(BF16), w3.astype(BF16), w2.astype(BF16))


def _router_kernel(x_ref, mod_ref, gain_ref, wr_ref, h_ref, gates_ref, pos_ref, post_ref, start_ref,
                   cnt_ref, carry_ref, *, tiles_per_seq):
    i = pl.program_id(0)

    @pl.when(i % tiles_per_seq == 0)
    def _():
        carry_ref[...] = jnp.zeros_like(carry_ref)

    mod = mod_ref[...]
    h = _modulated_norm(x_ref[...], gain_ref[...], mod[4:5], mod[3:4]).astype(BF16)
    h_ref[...] = h
    tm = h.shape[0]
    lane = lax.broadcasted_iota(jnp.int32, (tm, LANES), 1).astype(F32)
    logits = jnp.where(lane < N_EXPERTS, _dot(h, wr_ref[...]), -jnp.inf)
    m1 = jnp.max(logits, axis=-1, keepdims=True)
    i1 = jnp.min(jnp.where(logits == m1, lane, float(LANES)), axis=-1, keepdims=True)
    rest = jnp.where(lane == i1, -jnp.inf, logits)
    m2 = jnp.max(rest, axis=-1, keepdims=True)
    i2 = jnp.min(jnp.where(rest == m2, lane, float(LANES)), axis=-1, keepdims=True)
    e2 = jnp.exp(m2 - m1)
    g1 = 1.0 / (1.0 + e2)
    g2 = e2 / (1.0 + e2)
    gates_ref[...] = jnp.where(lane == i1, g1, 0.0) + jnp.where(lane == i2, g2, 0.0)
    chosen = (lane == i1) | (lane == i2)
    onehot = jnp.where(chosen, 1.0, 0.0)
    r = lax.broadcasted_iota(jnp.int32, (tm, tm), 0)
    c = lax.broadcasted_iota(jnp.int32, (tm, tm), 1)
    tril = jnp.where(c <= r, 1.0, 0.0).astype(BF16)
    incl = _dot(tril, onehot.astype(BF16))
    carry = carry_ref[0:1, :]
    start_ref[...] = carry_ref[...]
    pos = jnp.where(chosen, incl - 1.0 + carry, -1.0)
    pos_ref[...] = pos
    post_ref[...] = pos.T[:N_EXPERTS, :]
    total = carry + incl[tm - 1:tm, :]
    carry_ref[...] = jnp.broadcast_to(total, carry_ref.shape)
    cnt_ref[...] = jnp.broadcast_to(total, cnt_ref.shape)


def _router(x, mod, gain, w_router):
    tokens = x.shape[0]
    batch = tokens // SEQ
    tm = TM_PROJ
    per_seq = SEQ // tm
    wr = jnp.zeros((D_MODEL, LANES), BF16).at[:, :N_EXPERTS].set(w_router.astype(BF16))
    row = lambda i: (i, 0)
    return pl.pallas_call(
        functools.partial(_router_kernel, tiles_per_seq=per_seq),
        out_shape=(
            jax.ShapeDtypeStruct((tokens, D_MODEL), BF16),
            jax.ShapeDtypeStruct((tokens, LANES), F32),
            jax.ShapeDtypeStruct((tokens, LANES), F32),
            jax.ShapeDtypeStruct((N_EXPERTS, tokens), F32),
            jax.ShapeDtypeStruct((tokens // tm, 8, LANES), F32),
            jax.ShapeDtypeStruct((batch, 8, LANES), F32),
        ),
        grid=(tokens // tm,),
        in_specs=[
            pl.BlockSpec((tm, D_MODEL), row),
            pl.BlockSpec((None, 6, D_MODEL), lambda i: (i // per_seq, 0, 0)),
            pl.BlockSpec((1, D_MODEL), lambda i: (0, 0)),
            pl.BlockSpec((D_MODEL, LANES), lambda i: (0, 0)),
        ],
        out_specs=(
            pl.BlockSpec((tm, D_MODEL), row),
            pl.BlockSpec((tm, LANES), row),
            pl.BlockSpec((tm, LANES), row),
            pl.BlockSpec((N_EXPERTS, tm), lambda i: (0, i)),
            pl.BlockSpec((None, 8, LANES), lambda i: (i, 0, 0)),
            pl.BlockSpec((None, 8, LANES), lambda i: (i // per_seq, 0, 0)),
        ),
        scratch_shapes=[pltpu.VMEM((8, LANES), F32)],
        compiler_params=_cparams(("arbitrary",)),
        name="expert_router",
    )(x, mod, gain.reshape(1, D_MODEL), wr)


TOK_BLOCKS = SEQ // TR_MOE
TAIL_TILES = (64, 128, TR_MOE)
SPILL_ROWS = 16


def _moe_kernel(tbl_ref, h_ref, post_ref, pos_ref, gates_ref, x_ref, mod_ref, w1_ref, w3_ref, w2_ref,
                out_ref, hs_ref, acc_ref):
    b, e, f = pl.program_id(0), pl.program_id(1), pl.program_id(2)
    last_f = pl.num_programs(2) - 1
    tr = TR_MOE
    base = b * (TOK_BLOCKS + 1) * N_EXPERTS + e
    bounds = [tbl_ref[base + tb * N_EXPERTS] for tb in range(TOK_BLOCKS + 1)]
    win = [pl.multiple_of(lax.shift_left(lax.shift_right_logical(s, 3), 3), SUBLANES) for s in bounds[:-1]]
    spills = [bounds[tb + 1] - win[tb] > tr for tb in range(TOK_BLOCKS)]
    total = bounds[-1]
    n_tiles = lax.shift_right_logical(total + (tr - 1), tr.bit_length() - 1)

    def clear(r, carry):
        acc_ref[pl.ds(pl.multiple_of(r * tr, tr), tr), :] = jnp.zeros((tr, D_MODEL), F32)
        return carry

    @pl.when((e == 0) & (f == 0))
    def _():
        out_ref[...] = jnp.zeros_like(out_ref)

    @pl.when(f == 0)
    def _():
        out_ref[pl.ds(pl.multiple_of(e * tr, tr), tr), :] += x_ref[...]
        lax.fori_loop(0, n_tiles + 2, clear, 0)
        slot = lax.broadcasted_iota(jnp.int32, (tr + SPILL_ROWS, tr), 0).astype(F32)
        for tb in range(TOK_BLOCKS):
            ts = slice(tb * tr, (tb + 1) * tr)
            local = post_ref[pl.ds(e, 1), ts] - win[tb].astype(F32)
            sel = jnp.where(local == slot, 1.0, 0.0).astype(BF16)
            acc_ref[pl.ds(win[tb], tr + SPILL_ROWS), :] += _dot(sel, h_ref[ts, :])

        def to_rows(r, carry):
            rows = pl.ds(pl.multiple_of(r * tr, tr), tr)
            hs_ref[rows, :] = acc_ref[rows, :].astype(BF16)
            acc_ref[rows, :] = jnp.zeros((tr, D_MODEL), F32)
            return carry
        lax.fori_loop(0, n_tiles + 2, to_rows, 0)

    def swiglu(r0, m):
        rows = pl.ds(r0, m)
        hr = hs_ref[rows, :]
        z = (_silu(_dot(hr, w1_ref[...])) * _dot(hr, w3_ref[...])).astype(BF16)
        acc_ref[rows, :] += _dot(z, w2_ref[...])

    def full_tile(r, carry):
        swiglu(pl.multiple_of(r * tr, tr), tr)
        return carry
    n_full = lax.shift_right_logical(total, tr.bit_length() - 1)
    lax.fori_loop(0, n_full, full_tile, 0)
    rest = total - n_full * tr
    tail0 = pl.multiple_of(n_full * tr, tr)
    lo = 0
    for m in TAIL_TILES:
        @pl.when((rest > lo) & (rest <= m))
        def _():
            swiglu(tail0, m)
        lo = m

    @pl.when(f == last_f)
    def _():
        lane = lax.broadcasted_iota(jnp.int32, (tr, LANES), 1)
        slot = lax.broadcasted_iota(jnp.int32, (tr, tr), 1).astype(F32)
        slot_spill = lax.broadcasted_iota(jnp.int32, (tr, SPILL_ROWS), 1).astype(F32) + float(tr)
        layer_gate = mod_ref[5:6, :]

        def columns(tb):
            ts = slice(tb * tr, (tb + 1) * tr)
            pos_col = jnp.sum(jnp.where(lane == e, pos_ref[ts, :], 0.0), axis=-1, keepdims=True)
            gate_col = jnp.sum(jnp.where(lane == e, gates_ref[ts, :], 0.0), axis=-1, keepdims=True)
            return ts, pos_col - win[tb].astype(F32), gate_col

        for tb in range(TOK_BLOCKS):
            ts, local, gate_col = columns(tb)
            rows = acc_ref[pl.ds(win[tb], tr), :].astype(BF16)
            sel = jnp.where(local == slot, 1.0, 0.0).astype(BF16)
            out_ref[ts, :] += layer_gate * (gate_col * _dot(sel, rows))

        for tb in range(TOK_BLOCKS):
            @pl.when(spills[tb])
            def _():
                ts, local, gate_col = columns(tb)
                rows2 = acc_ref[pl.ds(win[tb] + tr, SPILL_ROWS), :].astype(BF16)
                sel2 = jnp.where(local == slot_spill, 1.0, 0.0).astype(BF16)
                out_ref[ts, :] += layer_gate * (gate_col * _dot(sel2, rows2))


def _cast_kernel(w_ref, o_ref):
    o_ref[...] = w_ref[...].astype(BF16)


def _chunk_columns(w, tf):
    n_e, k, f = w.shape
    return pl.pallas_call(
        _cast_kernel,
        out_shape=jax.ShapeDtypeStruct((n_e, f // tf, k, tf), BF16),
        grid=(n_e, f // tf),
        in_specs=[pl.BlockSpec((None, k, tf), lambda e, c: (e, 0, c))],
        out_specs=pl.BlockSpec((None, None, k, tf), lambda e, c: (e, c, 0, 0)),
        compiler_params=_cparams(("arbitrary", "arbitrary")),
        name="chunk_expert_weights",
    )(w)


def _moe(x, mod, h, post, pos, gates, table, w1, w3, w2):
    tokens = h.shape[0]
    batch = tokens // SEQ
    d_ff = w1.shape[2]
    tf = TF_MOE
    n_f = d_ff // tf
    assert N_EXPERTS == TOK_BLOCKS
    once = pl.Buffered(1)
    max_rows = SEQ + 3 * TR_MOE
    grid_spec = pltpu.PrefetchScalarGridSpec(
        num_scalar_prefetch=1,
        grid=(batch, N_EXPERTS, n_f),
        in_specs=[
            pl.BlockSpec((SEQ, D_MODEL), lambda b, e, f, t: (b, 0), pipeline_mode=once),
            pl.BlockSpec((N_EXPERTS, SEQ), lambda b, e, f, t: (0, b), pipeline_mode=once),
            pl.BlockSpec((SEQ, LANES), lambda b, e, f, t: (b, 0), pipeline_mode=once),
            pl.BlockSpec((SEQ, LANES), lambda b, e, f, t: (b, 0), pipeline_mode=once),
            pl.BlockSpec((TR_MOE, D_MODEL), lambda b, e, f, t: (b * TOK_BLOCKS + e, 0)),
            pl.BlockSpec((None, 6, D_MODEL), lambda b, e, f, t: (b, 0, 0)),
            pl.BlockSpec((None, None, D_MODEL, tf), lambda b, e, f, t: (e, f, 0, 0)),
            pl.BlockSpec((None, None, D_MODEL, tf), lambda b, e, f, t: (e, f, 0, 0)),
            pl.BlockSpec((None, tf, D_MODEL), lambda b, e, f, t: (e, f, 0)),
        ],
        out_specs=pl.BlockSpec((SEQ, D_MODEL), lambda b, e, f, t: (b, 0), pipeline_mode=once),
        scratch_shapes=[pltpu.VMEM((max_rows, D_MODEL), BF16), pltpu.VMEM((max_rows, D_MODEL), F32)],
    )
    return pl.pallas_call(
        _moe_kernel,
        out_shape=jax.ShapeDtypeStruct((tokens, D_MODEL), F32),
        grid_spec=grid_spec,
        compiler_params=_cparams(("arbitrary", "arbitrary", "arbitrary")),
        name="expert_swiglu",
    )(table, h, post, pos, gates, x, mod, _chunk_columns(w1, tf), _chunk_columns(w3, tf), w2.astype(BF16))


def _token_mixer(x, mod, gain, w_in, q_gain, k_gain, ret_gain, w_out, bias_masks):
    *qkv, ret_in = _in_proj(x, mod, gain, w_in, q_gain, k_gain)
    attn = _dilated_attention(qkv, bias_masks)
    ret = _retention(ret_in, ret_gain)
    return _out_proj(attn, ret, x, mod, w_out)


def _moe_ffn(x, mod, gain, w_router, w1, w3, w2):
    h, gates, pos, post, start, cnt = _router(x, mod, gain, w_router)
    batch = cnt.shape[0]
    table = jnp.concatenate([start[:, 0, :N_EXPERTS].reshape(batch, TOK_BLOCKS, N_EXPERTS),
                             cnt[:, :1, :N_EXPERTS]], axis=1).astype(jnp.int32).reshape(-1)
    return _moe(x, mod, h, post, pos, gates, table, w1, w3, w2)


def kernel(x, c, rel_bias_table, norm_mix, norm_ffn, w_mod, b_mod, w_in, q_gain, k_gain, ret_gain, w_out,
           ffn_w1, ffn_w3, ffn_w2, moe_router, moe_w1, moe_w3, moe_w2):
    batch, seq, d_model = x.shape
    assert (seq, d_model) == (SEQ, D_MODEL)
    depth = w_mod.shape[0]
    mods = _modulation(c, w_mod, b_mod).reshape(depth, batch, 6, D_MODEL)
    bias_masks = _bias_masks(rel_bias_table)
    xt = x.reshape(batch * seq, d_model)
    for layer in range(depth):
        mod = mods[layer]
        xt = _token_mixer(xt, mod, norm_mix[layer], w_in[layer], q_gain[layer], k_gain[layer],
                          ret_gain[layer], w_out[layer], bias_masks)
        i = layer // 2
        if layer % 2 == 0:
            xt = _dense_ffn(xt, mod, norm_ffn[layer], ffn_w1[i], ffn_w3[i], ffn_w2[i])
        else:
            xt = _moe_ffn(xt, mod, norm_ffn[layer], moe_router[i], moe_w1[i], moe_w3[i], moe_w2[i])
    return xt.reshape(batch, seq, d_model)
```

```python
import functools
import math

import jax
import jax.numpy as jnp
import numpy as np
from jax import lax
from jax.experimental import pallas as pl
from jax.experimental.pallas import tpu as pltpu

D_MODEL = 1024
SEQ = 2048
ATTN_HEADS = 8
ATTN_HEAD_DIM = 64
ATTN_WIDTH = ATTN_HEADS * ATTN_HEAD_DIM
DILATED_PATTERNS = ((128, 1), (512, 4), (2048, 16))
BLOCK = 128
NUM_BUCKETS = 32
MAX_DISTANCE = 2048
RET_HEADS = 4
RET_KEY_DIM = 64
RET_VALUE_DIM = 128
RET_WIDTH = RET_HEADS * RET_VALUE_DIM
RET_QK_WIDTH = RET_HEADS * RET_KEY_DIM
RET_CHUNK = 128
ROPE_BASE = 10000.0
IN_WIDTH = 3 * ATTN_WIDTH + 2 * RET_QK_WIDTH + 2 * RET_WIDTH
RET_IN_WIDTH = IN_WIDTH - 3 * ATTN_WIDTH
N_EXPERTS = 8
EPS = 1e-6
NEG_INF = -1e30

LANES = 128
VMEM_LIMIT = 60 * 1024 * 1024

BF16 = jnp.bfloat16
F32 = jnp.float32

TM_PROJ = 256
TM_FFN = 512
MXU_COLS = 256
TR_MOE = 256
TF_MOE = 1792
assert TM_PROJ == TR_MOE
RET_ROWS = 512
SUBLANES = 8


def _cparams(sem):
    return pltpu.CompilerParams(dimension_semantics=sem, vmem_limit_bytes=VMEM_LIMIT)


def _dot(a, b):
    return jnp.dot(a, b, preferred_element_type=F32)


def _dot_nt(a, b):
    return lax.dot_general(a, b, (((1,), (1,)), ((), ())), preferred_element_type=F32)


def _dot_tn(a, b):
    return lax.dot_general(a, b, (((0,), (0,)), ((), ())), preferred_element_type=F32)


def _split_bf16(v):
    hi = v.astype(BF16)
    lo = (v - hi.astype(F32)).astype(BF16)
    return hi, lo


def _silu(v):
    return v * (1.0 / (1.0 + jnp.exp(-v)))


def _modulated_norm(x, gain, scale, shift):
    ms = jnp.mean(x * x, axis=-1, keepdims=True)
    y = x * lax.rsqrt(ms + EPS) * gain
    return y * (1.0 + scale) + shift


def _mod_kernel(c_ref, w_ref, b_ref, o_ref):
    ca = _silu(c_ref[...]).astype(BF16)
    o_ref[...] = _dot(ca, w_ref[...].astype(BF16)) + b_ref[...]


def _modulation(c, w_mod, b_mod):
    depth, _, width = w_mod.shape
    batch = c.shape[0]
    tn = 1536
    return pl.pallas_call(
        _mod_kernel,
        out_shape=jax.ShapeDtypeStruct((depth, batch, width), F32),
        grid=(depth, width // tn),
        in_specs=[
            pl.BlockSpec((batch, D_MODEL), lambda l, n: (0, 0)),
            pl.BlockSpec((None, D_MODEL, tn), lambda l, n: (l, 0, n)),
            pl.BlockSpec((None, 1, tn), lambda l, n: (l, 0, n)),
        ],
        out_specs=pl.BlockSpec((None, batch, tn), lambda l, n: (l, 0, n)),
        compiler_params=_cparams(("arbitrary", "arbitrary")),
        name="adaln_modulation",
    )(c, w_mod, b_mod.reshape(depth, 1, width))


def _bias_kernel(table_ref, bucket_ref, o_ref):
    h = pl.program_id(1)
    bucket = bucket_ref[...]
    acc = jnp.full(bucket.shape, NEG_INF, F32)
    for b in range(NUM_BUCKETS):
        acc = jnp.where(bucket == b, table_ref[b, h], acc)
    o_ref[...] = acc


BIAS_FULL = {1: 0, 4: 2}
BIAS_FIRST = {1: 1, 4: 3, 16: 4}
N_BIAS_SETS = 5


def _bias_masks(rel_bias_table):
    i = jnp.arange(BLOCK)[:, None]
    j = jnp.arange(2 * BLOCK)[None, :]
    max_exact = NUM_BUCKETS // 2

    def bucket_of(rel, dilation, w_sub, exists):
        n = jnp.maximum(rel * dilation, 0)
        nf = jnp.maximum(n.astype(F32), float(max_exact))
        large = max_exact + (jnp.log(nf / max_exact) / math.log(MAX_DISTANCE / max_exact)
                             * (NUM_BUCKETS - max_exact)).astype(jnp.int32)
        large = jnp.minimum(large, NUM_BUCKETS - 1)
        bucket = jnp.where(n < max_exact, n, large)
        allowed = (rel >= 0) & (rel <= w_sub) & exists
        return jnp.where(allowed, bucket, -1)

    sets = [None] * N_BIAS_SETS
    for window, dilation in DILATED_PATTERNS:
        w_sub = window // dilation
        if dilation in BIAS_FULL:
            sets[BIAS_FULL[dilation]] = bucket_of(i - j + BLOCK, dilation, w_sub, j >= 0)
        sets[BIAS_FIRST[dilation]] = bucket_of(i - j, dilation, w_sub, j < BLOCK)
    buckets = jnp.stack(sets).astype(jnp.int32)
    return pl.pallas_call(
        _bias_kernel,
        out_shape=jax.ShapeDtypeStruct((N_BIAS_SETS, ATTN_HEADS // 2, 2 * BLOCK, 2 * BLOCK), F32),
        grid=(N_BIAS_SETS, ATTN_HEADS),
        in_specs=[
            pl.BlockSpec(memory_space=pltpu.SMEM),
            pl.BlockSpec((None, BLOCK, 2 * BLOCK), lambda p, h: (p, 0, 0)),
        ],
        out_specs=pl.BlockSpec((None, None, BLOCK, 2 * BLOCK), lambda p, h: (p, h // 2, h % 2, 0)),
        compiler_params=_cparams(("arbitrary", "arbitrary")),
        name="relative_bias_masks",
    )(rel_bias_table, buckets)


HALF = ATTN_WIDTH // 2


def _in_proj_kernel(x_ref, mod_ref, gain_ref, w_ref, qg_ref, kg_ref, grp_ref,
                    q1_ref, k1_ref, v1_ref, q4_ref, k4_ref, v4_ref, q16_ref, k16_ref, v16_ref, r_ref,
                    perm_ref):
    mod = mod_ref[...]
    h = _modulated_norm(x_ref[...], gain_ref[...], mod[1:2], mod[0:1]).astype(BF16)
    proj = _dot(h, w_ref[...])
    grp = grp_ref[...]
    tm = proj.shape[0]

    def head_norm(t, gain):
        hi, lo = _split_bf16(t * t)
        ss = _dot(hi, grp) + _dot(lo, grp)
        return t * lax.rsqrt(ss * (1.0 / ATTN_HEAD_DIM) + EPS) * gain

    def emit(t, o1_ref, o4_ref, o16_ref):
        o1_ref[...] = t.astype(BF16)
        for j in range(ATTN_WIDTH // LANES):
            perm_ref[j] = t[:, j * LANES:(j + 1) * LANES]
        for dil, o_ref in ((4, o4_ref), (16, o16_ref)):
            for hf in range(2):
                for r in range(dil):
                    for jj in range(HALF // LANES):
                        c0 = (hf * dil + r) * HALF + jj * LANES
                        o_ref[:, c0:c0 + LANES] = perm_ref[hf * (HALF // LANES) + jj,
                                                           pl.ds(r, tm // dil, stride=dil), :].astype(BF16)

    emit(head_norm(proj[:, :ATTN_WIDTH], qg_ref[...]) * (ATTN_HEAD_DIM ** -0.5), q1_ref, q4_ref, q16_ref)
    emit(head_norm(proj[:, ATTN_WIDTH:2 * ATTN_WIDTH], kg_ref[...]), k1_ref, k4_ref, k16_ref)
    emit(proj[:, 2 * ATTN_WIDTH:3 * ATTN_WIDTH], v1_ref, v4_ref, v16_ref)
    r_ref[...] = proj[:, 3 * ATTN_WIDTH:]


def _in_proj(x, mod, gain, w_in, q_gain, k_gain):
    tokens = x.shape[0]
    tm = TM_PROJ
    per_seq = SEQ // tm
    grp = np.kron(np.eye(ATTN_HEADS), np.ones((ATTN_HEAD_DIM, ATTN_HEAD_DIM))).astype(np.float32)
    row = lambda i: (i, 0)
    const = lambda i: (0, 0)
    layouts = []
    for dil in (1, 4, 16):
        shape = jax.ShapeDtypeStruct((tokens // dil, dil * ATTN_WIDTH), BF16)
        spec = pl.BlockSpec((tm // dil, dil * ATTN_WIDTH), row)
        layouts.append(((shape,) * 3, (spec,) * 3))
    out_shape = sum((s for s, _ in layouts), ()) + (jax.ShapeDtypeStruct((tokens, RET_IN_WIDTH), F32),)
    out_specs = sum((s for _, s in layouts), ()) + (pl.BlockSpec((tm, RET_IN_WIDTH), row),)
    return pl.pallas_call(
        _in_proj_kernel,
        out_shape=out_shape,
        grid=(tokens // tm,),
        in_specs=[
            pl.BlockSpec((tm, D_MODEL), row),
            pl.BlockSpec((None, 6, D_MODEL), lambda i: (i // per_seq, 0, 0)),
            pl.BlockSpec((1, D_MODEL), const),
            pl.BlockSpec((D_MODEL, IN_WIDTH), const),
            pl.BlockSpec((1, ATTN_WIDTH), const),
            pl.BlockSpec((1, ATTN_WIDTH), const),
            pl.BlockSpec((ATTN_WIDTH, ATTN_WIDTH), const),
        ],
        out_specs=out_specs,
        scratch_shapes=[pltpu.VMEM((ATTN_WIDTH // LANES, tm, LANES), F32)],
        compiler_params=_cparams(("arbitrary",)),
        name="in_projection",
    )(x, mod, gain.reshape(1, D_MODEL), w_in.astype(BF16),
      jnp.tile(q_gain, ATTN_HEADS).reshape(1, ATTN_WIDTH),
      jnp.tile(k_gain, ATTN_HEADS).reshape(1, ATTN_WIDTH),
      jnp.asarray(grp, BF16))


PAIRS_PER_HALF = ATTN_HEADS // 4
GROUP = 4


def _pair_scores(qp, kp, vp, bias2, masks, low):
    q2 = jnp.concatenate([qp * masks[0], qp * masks[1]], axis=0)
    s = _dot_nt(q2, kp) + bias2
    m = jnp.max(s, axis=-1, keepdims=True)
    p = jnp.exp(s - m)
    den = jnp.sum(p, axis=-1, keepdims=True)
    pv = _dot(p.astype(BF16), vp)
    pick = lambda t: jnp.where(low, t[:BLOCK], t[BLOCK:])
    return pick(pv), pick(m), pick(den)


def _attn_kernel(q1_ref, k1_ref, v1_ref, q4_ref, k4_ref, v4_ref, q16_ref, k16_ref, v16_ref, bm_ref,
                 o_ref, acc_ref, max_ref, den_ref):
    lane = lax.broadcasted_iota(jnp.int32, (BLOCK, LANES), 1)
    low = lane < ATTN_HEAD_DIM
    masks = (jnp.where(low, 1.0, 0.0).astype(BF16), jnp.where(low, 0.0, 1.0).astype(BF16))

    def block(q_ref, k_ref, v_ref, c0, q0, w0, width, bias_set, rows, first):
        for p in range(PAIRS_PER_HALF):
            cs = slice(c0 + p * LANES, c0 + (p + 1) * LANES)
            acc, m, den = _pair_scores(q_ref[pl.ds(q0, BLOCK), cs], k_ref[pl.ds(w0, width), cs],
                                       v_ref[pl.ds(w0, width), cs], bm_ref[bias_set, p, :, 0:width],
                                       masks, low)
            if not first:
                m_old = max_ref[p, rows, :]
                m_new = jnp.maximum(m_old, m)
                a, b = jnp.exp(m_old - m_new), jnp.exp(m - m_new)
                den = den_ref[p, rows, :] * a + den * b
                acc = acc_ref[p, rows, :] * a + acc * b
                m = m_new
            max_ref[p, rows, :] = m
            den_ref[p, rows, :] = den
            acc_ref[p, rows, :] = acc

    def d1_group(g, carry):
        for u in range(GROUP):
            n = g * GROUP + u
            q0 = pl.multiple_of(n * BLOCK, BLOCK)
            w0 = pl.multiple_of(jnp.maximum(n - 1, 0) * BLOCK, BLOCK)
            bias_set = jnp.where(n == 0, BIAS_FIRST[1], BIAS_FULL[1])
            block(q1_ref, k1_ref, v1_ref, 0, q0, w0, 2 * BLOCK, bias_set, pl.ds(q0, BLOCK), True)
        return carry
    lax.fori_loop(0, SEQ // BLOCK // GROUP, d1_group, 0)

    for r in range(4):
        for n in range(SEQ // 4 // BLOCK):
            block(q4_ref, k4_ref, v4_ref, r * HALF, n * BLOCK, max(n - 1, 0) * BLOCK, 2 * BLOCK,
                  BIAS_FIRST[4] if n == 0 else BIAS_FULL[4],
                  pl.ds(r + 4 * BLOCK * n, BLOCK, stride=4), False)

    for r in range(16):
        block(q16_ref, k16_ref, v16_ref, r * HALF, 0, 0, BLOCK, BIAS_FIRST[16],
              pl.ds(r, BLOCK, stride=16), False)

    for n in range(SEQ // BLOCK):
        rows = slice(n * BLOCK, (n + 1) * BLOCK)
        for p in range(PAIRS_PER_HALF):
            o_ref[rows, p * LANES:(p + 1) * LANES] = (acc_ref[p, rows, :] / den_ref[p, rows, :]).astype(BF16)


def _dilated_attention(qkv, bias_masks):
    tokens = qkv[0].shape[0]
    batch = tokens // SEQ
    specs = []
    for dil in (1, 4, 16):
        specs += [pl.BlockSpec((SEQ // dil, dil * HALF), lambda b, hf: (b, hf))] * 3
    state = pltpu.VMEM((PAIRS_PER_HALF, SEQ, LANES), F32)
    return pl.pallas_call(
        _attn_kernel,
        out_shape=jax.ShapeDtypeStruct((tokens, ATTN_WIDTH), BF16),
        grid=(batch, 2),
        in_specs=specs + [pl.BlockSpec((N_BIAS_SETS, PAIRS_PER_HALF, 2 * BLOCK, 2 * BLOCK),
                                       lambda b, hf: (0, hf, 0, 0))],
        out_specs=pl.BlockSpec((SEQ, HALF), lambda b, hf: (b, hf)),
        scratch_shapes=[state, state, state],
        compiler_params=_cparams(("arbitrary", "arbitrary")),
        name="dilated_attention",
    )(*qkv, bias_masks)


def _retention_kernel(r_ref, cos_ref, sin_ref, dmask_ref, qdec_ref, kdec_ref, cdec_ref, gain_ref,
                      o_ref, state_ref):
    @pl.when(pl.program_id(1) == 0)
    def _():
        state_ref[...] = jnp.zeros_like(state_ref)

    lane = lax.broadcasted_iota(jnp.int32, (RET_CHUNK, LANES), 1)
    low = lane < RET_KEY_DIM
    first_half = (lane % RET_KEY_DIM) < (RET_KEY_DIM // 2)

    def rotate(t, cos, sin):
        partner = jnp.where(first_half, pltpu.roll(t, LANES - RET_KEY_DIM // 2, 1),
                            pltpu.roll(t, RET_KEY_DIM // 2, 1))
        return t * cos + partner * sin

    for c in range(RET_ROWS // RET_CHUNK):
        rows = slice(c * RET_CHUNK, (c + 1) * RET_CHUNK)
        for hp in range(RET_HEADS // 2):
            qs = slice(hp * LANES, (hp + 1) * LANES)
            ks = slice(RET_QK_WIDTH + hp * LANES, RET_QK_WIDTH + (hp + 1) * LANES)
            cos, sin = cos_ref[rows, qs], sin_ref[rows, qs]
            q_pair = rotate(r_ref[rows, qs], cos, sin) * (RET_KEY_DIM ** -0.5)
            k_pair = rotate(r_ref[rows, ks], cos, sin)
            for hh in range(2):
                head = 2 * hp + hh
                vs = slice(2 * RET_QK_WIDTH + head * LANES, 2 * RET_QK_WIDTH + (head + 1) * LANES)
                gs = slice(2 * RET_QK_WIDTH + RET_WIDTH + head * LANES,
                           2 * RET_QK_WIDTH + RET_WIDTH + (head + 1) * LANES)
                keep = low if hh == 0 else jnp.logical_not(low)
                qm = jnp.where(keep, q_pair, 0.0)
                vb = r_ref[rows, vs].astype(BF16)
                state = state_ref[head]
                inner = _dot_nt(qm.astype(BF16), k_pair.astype(BF16)) * dmask_ref[head]
                y = _dot(inner.astype(BF16), vb)
                y = y + _dot((qm * qdec_ref[head]).astype(BF16), state.astype(BF16))
                state_ref[head] = state * cdec_ref[head] + _dot_tn((k_pair * kdec_ref[head]).astype(BF16), vb)
                mu = jnp.mean(y, axis=-1, keepdims=True)
                yc = y - mu
                var = jnp.mean(yc * yc, axis=-1, keepdims=True)
                yn = yc * lax.rsqrt(var + EPS) * gain_ref[:, head * LANES:(head + 1) * LANES]
                o_ref[rows, head * LANES:(head + 1) * LANES] = (_silu(r_ref[rows, gs]) * yn).astype(BF16)


def _retention_tables():
    half = RET_KEY_DIM // 2
    pos = jnp.arange(SEQ, dtype=F32)
    inv = ROPE_BASE ** (-jnp.arange(half, dtype=F32) / half)
    ang = pos[:, None] * inv[None, :]
    cos, sin = jnp.cos(ang), jnp.sin(ang)
    cos_full = jnp.tile(jnp.concatenate([cos, cos], axis=-1), (1, RET_HEADS))
    sin_signed = jnp.tile(jnp.concatenate([-sin, sin], axis=-1), (1, RET_HEADS))
    log_g = jnp.log(1.0 - 2.0 ** (-5.0 - jnp.arange(RET_HEADS, dtype=F32)))
    idx = jnp.arange(RET_CHUNK, dtype=F32)
    diff = idx[:, None] - idx[None, :]
    dmask = jnp.where(diff >= 0, jnp.exp(jnp.maximum(diff, 0.0)[None] * log_g[:, None, None]), 0.0)
    q_decay = jnp.exp((idx + 1.0)[None, :] * log_g[:, None])[..., None]
    k_decay = jnp.exp((RET_CHUNK - 1.0 - idx)[None, :] * log_g[:, None])[..., None]
    chunk_decay = jnp.exp(RET_CHUNK * log_g)[:, None, None]
    full = (RET_HEADS, RET_CHUNK, LANES)
    return (cos_full, sin_signed, dmask, jnp.broadcast_to(q_decay, full),
            jnp.broadcast_to(k_decay, full), jnp.broadcast_to(chunk_decay, full))


def _retention(ret_in, ret_gain):
    tokens = ret_in.shape[0]
    batch = tokens // SEQ
    per_seq = SEQ // RET_ROWS
    cos, sin, dmask, qdec, kdec, cdec = _retention_tables()
    tab = pl.BlockSpec((RET_ROWS, RET_QK_WIDTH), lambda b, j: (j, 0))
    const3 = pl.BlockSpec((RET_HEADS, RET_CHUNK, LANES), lambda b, j: (0, 0, 0))
    return pl.pallas_call(
        _retention_kernel,
        out_shape=jax.ShapeDtypeStruct((tokens, RET_WIDTH), BF16),
        grid=(batch, per_seq),
        in_specs=[
            pl.BlockSpec((RET_ROWS, RET_IN_WIDTH), lambda b, j: (b * per_seq + j, 0)),
            tab, tab, const3, const3, const3, const3,
            pl.BlockSpec((1, RET_WIDTH), lambda b, j: (0, 0)),
        ],
        out_specs=pl.BlockSpec((RET_ROWS, RET_WIDTH), lambda b, j: (b * per_seq + j, 0)),
        scratch_shapes=[pltpu.VMEM((RET_HEADS, LANES, RET_VALUE_DIM), F32)],
        compiler_params=_cparams(("arbitrary", "arbitrary")),
        name="retention",
    )(ret_in, cos, sin, dmask, qdec, kdec, cdec, ret_gain.reshape(1, RET_WIDTH))


def _out_proj_kernel(attn_ref, ret_ref, x_ref, mod_ref, w_ref, out_ref):
    mix = _dot(attn_ref[...], w_ref[:ATTN_WIDTH, :]) + _dot(ret_ref[...], w_ref[ATTN_WIDTH:, :])
    out_ref[...] = x_ref[...] + mod_ref[2:3, :] * mix


def _out_proj(attn, ret, x, mod, w_out):
    tokens = x.shape[0]
    tm = TM_FFN
    per_seq = SEQ // tm
    row = lambda i: (i, 0)
    return pl.pallas_call(
        _out_proj_kernel,
        out_shape=jax.ShapeDtypeStruct((tokens, D_MODEL), F32),
        grid=(tokens // tm,),
        in_specs=[pl.BlockSpec((tm, ATTN_WIDTH), row),
                  pl.BlockSpec((tm, RET_WIDTH), row),
                  pl.BlockSpec((tm, D_MODEL), row),
                  pl.BlockSpec((None, 6, D_MODEL), lambda i: (i // per_seq, 0, 0)),
                  pl.BlockSpec((D_MODEL, D_MODEL), lambda i: (0, 0))],
        out_specs=pl.BlockSpec((tm, D_MODEL), row),
        compiler_params=_cparams(("arbitrary",)),
        name="out_projection",
    )(attn, ret, x, mod, w_out.astype(BF16))


def _swiglu_chunks(h, w1_ref, w3_ref, w2_ref):
    d_ff = w1_ref.shape[-1]
    total = None
    for c0 in range(0, d_ff, MXU_COLS):
        c1 = min(c0 + MXU_COLS, d_ff)
        z = (_silu(_dot(h, w1_ref[:, c0:c1])) * _dot(h, w3_ref[:, c0:c1])).astype(BF16)
        part = _dot(z, w2_ref[c0:c1, :])
        total = part if total is None else total + part
    return total


def _ffn_kernel(x_ref, mod_ref, gain_ref, w1_ref, w3_ref, w2_ref, out_ref):
    mod = mod_ref[...]
    x = x_ref[...]
    h = _modulated_norm(x, gain_ref[...], mod[4:5], mod[3:4]).astype(BF16)
    out_ref[...] = x + mod[5:6, :] * _swiglu_chunks(h, w1_ref, w3_ref, w2_ref)


def _dense_ffn(x, mod, gain, w1, w3, w2):
    tokens = x.shape[0]
    d_ff = w1.shape[1]
    tm = TM_FFN
    per_seq = SEQ // tm
    resident = pl.Buffered(1)
    return pl.pallas_call(
        _ffn_kernel,
        out_shape=jax.ShapeDtypeStruct((tokens, D_MODEL), F32),
        grid=(tokens // tm,),
        in_specs=[
            pl.BlockSpec((tm, D_MODEL), lambda i: (i, 0)),
            pl.BlockSpec((None, 6, D_MODEL), lambda i: (i // per_seq, 0, 0)),
            pl.BlockSpec((1, D_MODEL), lambda i: (0, 0)),
            pl.BlockSpec((D_MODEL, d_ff), lambda i: (0, 0), pipeline_mode=resident),
            pl.BlockSpec((D_MODEL, d_ff), lambda i: (0, 0), pipeline_mode=resident),
            pl.BlockSpec((d_ff, D_MODEL), lambda i: (0, 0), pipeline_mode=resident),
        ],
        out_specs=pl.BlockSpec((tm, D_MODEL), lambda i: (i, 0)),
        compiler_params=_cparams(("arbitrary",)),
        name="dense_swiglu",
    )(x, mod, gain.reshape(1, D_MODEL), w1.astype(BF16), w3.astype(BF16), w2.astype(BF16))


def _router_kernel(x_ref, mod_ref, gain_ref, wr_ref, h_ref, gates_ref, pos_ref, post_ref, start_ref,
                   cnt_ref, carry_ref, *, tiles_per_seq):
    i = pl.program_id(0)

    @pl.when(i % tiles_per_seq == 0)
    def _():
        carry_ref[...] = jnp.zeros_like(carry_ref)

    mod = mod_ref[...]
    h = _modulated_norm(x_ref[...], gain_ref[...], mod[4:5], mod[3:4]).astype(BF16)
    h_ref[...] = h
    tm = h.shape[0]
    lane = lax.broadcasted_iota(jnp.int32, (tm, LANES), 1).astype(F32)
    logits = jnp.where(lane < N_EXPERTS, _dot(h, wr_ref[...]), -jnp.inf)
    m1 = jnp.max(logits, axis=-1, keepdims=True)
    i1 = jnp.min(jnp.where(logits == m1, lane, float(LANES)), axis=-1, keepdims=True)
    rest = jnp.where(lane == i1, -jnp.inf, logits)
    m2 = jnp.max(rest, axis=-1, keepdims=True)
    i2 = jnp.min(jnp.where(rest == m2, lane, float(LANES)), axis=-1, keepdims=True)
    e2 = jnp.exp(m2 - m1)
    g1 = 1.0 / (1.0 + e2)
    g2 = e2 / (1.0 + e2)
    gates_ref[...] = jnp.where(lane == i1, g1, 0.0) + jnp.where(lane == i2, g2, 0.0)
    chosen = (lane == i1) | (lane == i2)
    onehot = jnp.where(chosen, 1.0, 0.0)
    r = lax.broadcasted_iota(jnp.int32, (tm, tm), 0)
    c = lax.broadcasted_iota(jnp.int32, (tm, tm), 1)
    tril = jnp.where(c <= r, 1.0, 0.0).astype(BF16)
    incl = _dot(tril, onehot.astype(BF16))
    carry = carry_ref[0:1, :]
    start_ref[...] = carry_ref[...]
    pos = jnp.where(chosen, incl - 1.0 + carry, -1.0)
    pos_ref[...] = pos
    post_ref[...] = pos.T[:N_EXPERTS, :]
    total = carry + incl[tm - 1:tm, :]
    carry_ref[...] = jnp.broadcast_to(total, carry_ref.shape)
    cnt_ref[...] = jnp.broadcast_to(total, cnt_ref.shape)


def _router(x, mod, gain, w_router):
    tokens = x.shape[0]
    batch = tokens // SEQ
    tm = TM_PROJ
    per_seq = SEQ // tm
    wr = jnp.zeros((D_MODEL, LANES), BF16).at[:, :N_EXPERTS].set(w_router.astype(BF16))
    row = lambda i: (i, 0)
    return pl.pallas_call(
        functools.partial(_router_kernel, tiles_per_seq=per_seq),
        out_shape=(
            jax.ShapeDtypeStruct((tokens, D_MODEL), BF16),
            jax.ShapeDtypeStruct((tokens, LANES), F32),
            jax.ShapeDtypeStruct((tokens, LANES), F32),
            jax.ShapeDtypeStruct((N_EXPERTS, tokens), F32),
            jax.ShapeDtypeStruct((tokens // tm, 8, LANES), F32),
            jax.ShapeDtypeStruct((batch, 8, LANES), F32),
        ),
        grid=(tokens // tm,),
        in_specs=[
            pl.BlockSpec((tm, D_MODEL), row),
            pl.BlockSpec((None, 6, D_MODEL), lambda i: (i // per_seq, 0, 0)),
            pl.BlockSpec((1, D_MODEL), lambda i: (0, 0)),
            pl.BlockSpec((D_MODEL, LANES), lambda i: (0, 0)),
        ],
        out_specs=(
            pl.BlockSpec((tm, D_MODEL), row),
            pl.BlockSpec((tm, LANES), row),
            pl.BlockSpec((tm, LANES), row),
            pl.BlockSpec((N_EXPERTS, tm), lambda i: (0, i)),
            pl.BlockSpec((None, 8, LANES), lambda i: (i, 0, 0)),
            pl.BlockSpec((None, 8, LANES), lambda i: (i // per_seq, 0, 0)),
        ),
        scratch_shapes=[pltpu.VMEM((8, LANES), F32)],
        compiler_params=_cparams(("arbitrary",)),
        name="expert_router",
    )(x, mod, gain.reshape(1, D_MODEL), wr)


TOK_BLOCKS = SEQ // TR_MOE
TAIL_TILES = (64, 128, TR_MOE)
SPILL_ROWS = 16
SMALL_WINDOW = 128


def _moe_kernel(tbl_ref, h_ref, post_ref, pos_ref, gates_ref, x_ref, mod_ref, w1_ref, w3_ref, w2_ref,
                out_ref, hs_ref, acc_ref):
    b, e, f = pl.program_id(0), pl.program_id(1), pl.program_id(2)
    last_f = pl.num_programs(2) - 1
    tr = TR_MOE
    base = b * (TOK_BLOCKS + 1) * N_EXPERTS + e
    bounds = [tbl_ref[base + tb * N_EXPERTS] for tb in range(TOK_BLOCKS + 1)]
    win = [pl.multiple_of(lax.shift_left(lax.shift_right_logical(s, 3), 3), SUBLANES) for s in bounds[:-1]]
    all_small = functools.reduce(jnp.logical_and,
                                 [bounds[tb + 1] - win[tb] <= SMALL_WINDOW for tb in range(TOK_BLOCKS)])
    total = bounds[-1]
    n_tiles = lax.shift_right_logical(total + (tr - 1), tr.bit_length() - 1)

    def clear(r, carry):
        acc_ref[pl.ds(pl.multiple_of(r * tr, tr), tr), :] = jnp.zeros((tr, D_MODEL), F32)
        return carry

    @pl.when((e == 0) & (f == 0))
    def _():
        out_ref[...] = jnp.zeros_like(out_ref)

    @pl.when(f == 0)
    def _():
        out_ref[pl.ds(pl.multiple_of(e * tr, tr), tr), :] += x_ref[...]
        lax.fori_loop(0, n_tiles + 2, clear, 0)

        def gather(height):
            slot = lax.broadcasted_iota(jnp.int32, (height, tr), 0).astype(F32)
            for tb in range(TOK_BLOCKS):
                ts = slice(tb * tr, (tb + 1) * tr)
                local = post_ref[pl.ds(e, 1), ts] - win[tb].astype(F32)
                sel = jnp.where(local == slot, 1.0, 0.0).astype(BF16)
                acc_ref[pl.ds(win[tb], height), :] += _dot(sel, h_ref[ts, :])

        @pl.when(all_small)
        def _():
            gather(SMALL_WINDOW)

        @pl.when(jnp.logical_not(all_small))
        def _():
            gather(tr + SPILL_ROWS)

        def to_rows(r, carry):
            rows = pl.ds(pl.multiple_of(r * tr, tr), tr)
            hs_ref[rows, :] = acc_ref[rows, :].astype(BF16)
            acc_ref[rows, :] = jnp.zeros((tr, D_MODEL), F32)
            return carry
        lax.fori_loop(0, n_tiles + 2, to_rows, 0)

    def swiglu(r0, m):
        rows = pl.ds(r0, m)
        hr = hs_ref[rows, :]
        z = (_silu(_dot(hr, w1_ref[...])) * _dot(hr, w3_ref[...])).astype(BF16)
        acc_ref[rows, :] += _dot(z, w2_ref[...])

    def full_tile(r, carry):
        swiglu(pl.multiple_of(r * tr, tr), tr)
        return carry
    n_full = lax.shift_right_logical(total, tr.bit_length() - 1)
    lax.fori_loop(0, n_full, full_tile, 0)
    rest = total - n_full * tr
    tail0 = pl.multiple_of(n_full * tr, tr)
    lo = 0
    for m in TAIL_TILES:
        @pl.when((rest > lo) & (rest <= m))
        def _():
            swiglu(tail0, m)
        lo = m

    @pl.when(f == last_f)
    def _():
        lane = lax.broadcasted_iota(jnp.int32, (tr, LANES), 1)
        layer_gate = mod_ref[5:6, :]

        def scatter(tb, first, height):
            ts = slice(tb * tr, (tb + 1) * tr)
            pos_col = jnp.sum(jnp.where(lane == e, pos_ref[ts, :], 0.0), axis=-1, keepdims=True)
            gate_col = jnp.sum(jnp.where(lane == e, gates_ref[ts, :], 0.0), axis=-1, keepdims=True)
            slot = lax.broadcasted_iota(jnp.int32, (tr, height), 1).astype(F32) + float(first)
            sel = jnp.where(pos_col - win[tb].astype(F32) == slot, 1.0, 0.0).astype(BF16)
            rows = acc_ref[pl.ds(win[tb] + first, height), :].astype(BF16)
            out_ref[ts, :] += layer_gate * (gate_col * _dot(sel, rows))

        @pl.when(all_small)
        def _():
            for tb in range(TOK_BLOCKS):
                scatter(tb, 0, SMALL_WINDOW)

        @pl.when(jnp.logical_not(all_small))
        def _():
            for tb in range(TOK_BLOCKS):
                scatter(tb, 0, tr)
            for tb in range(TOK_BLOCKS):
                @pl.when(bounds[tb + 1] - win[tb] > tr)
                def _():
                    scatter(tb, tr, SPILL_ROWS)


def _cast_kernel(w_ref, o_ref):
    o_ref[...] = w_ref[...].astype(BF16)


def _chunk_columns(w, tf):
    n_e, k, f = w.shape
    return pl.pallas_call(
        _cast_kernel,
        out_shape=jax.ShapeDtypeStruct((n_e, f // tf, k, tf), BF16),
        grid=(n_e, f // tf),
        in_specs=[pl.BlockSpec((None, k, tf), lambda e, c: (e, 0, c))],
        out_specs=pl.BlockSpec((None, None, k, tf), lambda e, c: (e, c, 0, 0)),
        compiler_params=_cparams(("arbitrary", "arbitrary")),
        name="chunk_expert_weights",
    )(w)


def _moe(x, mod, h, post, pos, gates, table, w1, w3, w2):
    tokens = h.shape[0]
    batch = tokens // SEQ
    d_ff = w1.shape[2]
    tf = TF_MOE
    n_f = d_ff // tf
    assert N_EXPERTS == TOK_BLOCKS
    once = pl.Buffered(1)
    max_rows = SEQ + 3 * TR_MOE
    grid_spec = pltpu.PrefetchScalarGridSpec(
        num_scalar_prefetch=1,
        grid=(batch, N_EXPERTS, n_f),
        in_specs=[
            pl.BlockSpec((SEQ, D_MODEL), lambda b, e, f, t: (b, 0), pipeline_mode=once),
            pl.BlockSpec((N_EXPERTS, SEQ), lambda b, e, f, t: (0, b), pipeline_mode=once),
            pl.BlockSpec((SEQ, LANES), lambda b, e, f, t: (b, 0), pipeline_mode=once),
            pl.BlockSpec((SEQ, LANES), lambda b, e, f, t: (b, 0), pipeline_mode=once),
            pl.BlockSpec((TR_MOE, D_MODEL), lambda b, e, f, t: (b * TOK_BLOCKS + e, 0)),
            pl.BlockSpec((None, 6, D_MODEL), lambda b, e, f, t: (b, 0, 0)),
            pl.BlockSpec((None, None, D_MODEL, tf), lambda b, e, f, t: (e, f, 0, 0)),
            pl.BlockSpec((None, None, D_MODEL, tf), lambda b, e, f, t: (e, f, 0, 0)),
            pl.BlockSpec((None, tf, D_MODEL), lambda b, e, f, t: (e, f, 0)),
        ],
        out_specs=pl.BlockSpec((SEQ, D_MODEL), lambda b, e, f, t: (b, 0), pipeline_mode=once),
        scratch_shapes=[pltpu.VMEM((max_rows, D_MODEL), BF16), pltpu.VMEM((max_rows, D_MODEL), F32)],
    )
    return pl.pallas_call(
        _moe_kernel,
        out_shape=jax.ShapeDtypeStruct((tokens, D_MODEL), F32),
        grid_spec=grid_spec,
        compiler_params=_cparams(("arbitrary", "arbitrary", "arbitrary")),
        name="expert_swiglu",
    )(table, h, post, pos, gates, x, mod, _chunk_columns(w1, tf), _chunk_columns(w3, tf), w2.astype(BF16))


def _token_mixer(x, mod, gain, w_in, q_gain, k_gain, ret_gain, w_out, bias_masks):
    *qkv, ret_in = _in_proj(x, mod, gain, w_in, q_gain, k_gain)
    attn = _dilated_attention(qkv, bias_masks)
    ret = _retention(ret_in, ret_gain)
    return _out_proj(attn, ret, x, mod, w_out)


def _moe_ffn(x, mod, gain, w_router, w1, w3, w2):
    h, gates, pos, post, start, cnt = _router(x, mod, gain, w_router)
    batch = cnt.shape[0]
    table = jnp.concatenate([start[:, 0, :N_EXPERTS].reshape(batch, TOK_BLOCKS, N_EXPERTS),
                             cnt[:, :1, :N_EXPERTS]], axis=1).astype(jnp.int32).reshape(-1)
    return _moe(x, mod, h, post, pos, gates, table, w1, w3, w2)


def kernel(x, c, rel_bias_table, norm_mix, norm_ffn, w_mod, b_mod, w_in, q_gain, k_gain, ret_gain, w_out,
           ffn_w1, ffn_w3, ffn_w2, moe_router, moe_w1, moe_w3, moe_w2):
    batch, seq, d_model = x.shape
    assert (seq, d_model) == (SEQ, D_MODEL)
    depth = w_mod.shape[0]
    mods = _modulation(c, w_mod, b_mod).reshape(depth, batch, 6, D_MODEL)
    bias_masks = _bias_masks(rel_bias_table)
    xt = x.reshape(batch * seq, d_model)
    for layer in range(depth):
        mod = mods[layer]
        xt = _token_mixer(xt, mod, norm_mix[layer], w_in[layer], q_gain[layer], k_gain[layer],
                          ret_gain[layer], w_out[layer], bias_masks)
        i = layer // 2
        if layer % 2 == 0:
            xt = _dense_ffn(xt, mod, norm_ffn[layer], ffn_w1[i], ffn_w3[i], ffn_w2[i])
        else:
            xt = _moe_ffn(xt, mod, norm_ffn[layer], moe_router[i], moe_w1[i], moe_w3[i], moe_w2[i])
    return xt.reshape(batch, seq, d_model)
```

```python
import functools
import math

import jax
import jax.numpy as jnp
import numpy as np
from jax import lax
from jax.experimental import pallas as pl
from jax.experimental.pallas import tpu as pltpu

D_MODEL = 1024
SEQ = 2048
ATTN_HEADS = 8
ATTN_HEAD_DIM = 64
ATTN_WIDTH = ATTN_HEADS * ATTN_HEAD_DIM
DILATED_PATTERNS = ((128, 1), (512, 4), (2048, 16))
BLOCK = 128
NUM_BUCKETS = 32
MAX_DISTANCE = 2048
RET_HEADS = 4
RET_KEY_DIM = 64
RET_VALUE_DIM = 128
RET_WIDTH = RET_HEADS * RET_VALUE_DIM
RET_QK_WIDTH = RET_HEADS * RET_KEY_DIM
RET_CHUNK = 128
ROPE_BASE = 10000.0
IN_WIDTH = 3 * ATTN_WIDTH + 2 * RET_QK_WIDTH + 2 * RET_WIDTH
RET_IN_WIDTH = IN_WIDTH - 3 * ATTN_WIDTH
N_EXPERTS = 8
EPS = 1e-6
NEG_INF = -1e30

LANES = 128
VMEM_LIMIT = 60 * 1024 * 1024

BF16 = jnp.bfloat16
F32 = jnp.float32

TM_PROJ = 256
TM_FFN = 512
MXU_COLS = 256
TR_MOE = 256
TF_MOE = 1792
assert TM_PROJ == TR_MOE
RET_ROWS = 512
SUBLANES = 8


def _cparams(sem):
    return pltpu.CompilerParams(dimension_semantics=sem, vmem_limit_bytes=VMEM_LIMIT)


def _dot(a, b):
    return jnp.dot(a, b, preferred_element_type=F32)


def _dot_nt(a, b):
    return lax.dot_general(a, b, (((1,), (1,)), ((), ())), preferred_element_type=F32)


def _dot_tn(a, b):
    return lax.dot_general(a, b, (((0,), (0,)), ((), ())), preferred_element_type=F32)


def _split_bf16(v):
    hi = v.astype(BF16)
    lo = (v - hi.astype(F32)).astype(BF16)
    return hi, lo


def _silu(v):
    return v * (1.0 / (1.0 + jnp.exp(-v)))


def _modulated_norm(x, gain, scale, shift):
    ms = jnp.mean(x * x, axis=-1, keepdims=True)
    y = x * lax.rsqrt(ms + EPS) * gain
    return y * (1.0 + scale) + shift


def _mod_kernel(c_ref, w_ref, b_ref, o_ref):
    ca = _silu(c_ref[...]).astype(BF16)
    o_ref[...] = _dot(ca, w_ref[...].astype(BF16)) + b_ref[...]


def _modulation(c, w_mod, b_mod):
    depth, _, width = w_mod.shape
    batch = c.shape[0]
    tn = 1536
    return pl.pallas_call(
        _mod_kernel,
        out_shape=jax.ShapeDtypeStruct((depth, batch, width), F32),
        grid=(depth, width // tn),
        in_specs=[
            pl.BlockSpec((batch, D_MODEL), lambda l, n: (0, 0)),
            pl.BlockSpec((None, D_MODEL, tn), lambda l, n: (l, 0, n)),
            pl.BlockSpec((None, 1, tn), lambda l, n: (l, 0, n)),
        ],
        out_specs=pl.BlockSpec((None, batch, tn), lambda l, n: (l, 0, n)),
        compiler_params=_cparams(("arbitrary", "arbitrary")),
        name="adaln_modulation",
    )(c, w_mod, b_mod.reshape(depth, 1, width))


def _bias_kernel(table_ref, bucket_ref, o_ref):
    h = pl.program_id(1)
    bucket = bucket_ref[...]
    acc = jnp.full(bucket.shape, NEG_INF, F32)
    for b in range(NUM_BUCKETS):
        acc = jnp.where(bucket == b, table_ref[b, h], acc)
    o_ref[...] = acc


BIAS_FULL = {1: 0, 4: 2}
BIAS_FIRST = {1: 1, 4: 3, 16: 4}
N_BIAS_SETS = 5


def _bias_masks(rel_bias_table):
    i = jnp.arange(BLOCK)[:, None]
    j = jnp.arange(2 * BLOCK)[None, :]
    max_exact = NUM_BUCKETS // 2

    def bucket_of(rel, dilation, w_sub, exists):
        n = jnp.maximum(rel * dilation, 0)
        nf = jnp.maximum(n.astype(F32), float(max_exact))
        large = max_exact + (jnp.log(nf / max_exact) / math.log(MAX_DISTANCE / max_exact)
                             * (NUM_BUCKETS - max_exact)).astype(jnp.int32)
        large = jnp.minimum(large, NUM_BUCKETS - 1)
        bucket = jnp.where(n < max_exact, n, large)
        allowed = (rel >= 0) & (rel <= w_sub) & exists
        return jnp.where(allowed, bucket, -1)

    sets = [None] * N_BIAS_SETS
    for window, dilation in DILATED_PATTERNS:
        w_sub = window // dilation
        if dilation in BIAS_FULL:
            sets[BIAS_FULL[dilation]] = bucket_of(i - j + BLOCK, dilation, w_sub, j >= 0)
        sets[BIAS_FIRST[dilation]] = bucket_of(i - j, dilation, w_sub, j < BLOCK)
    buckets = jnp.stack(sets).astype(jnp.int32)
    return pl.pallas_call(
        _bias_kernel,
        out_shape=jax.ShapeDtypeStruct((N_BIAS_SETS, ATTN_HEADS // 2, 2 * BLOCK, 2 * BLOCK), F32),
        grid=(N_BIAS_SETS, ATTN_HEADS),
        in_specs=[
            pl.BlockSpec(memory_space=pltpu.SMEM),
            pl.BlockSpec((None, BLOCK, 2 * BLOCK), lambda p, h: (p, 0, 0)),
        ],
        out_specs=pl.BlockSpec((None, None, BLOCK, 2 * BLOCK), lambda p, h: (p, h // 2, h % 2, 0)),
        compiler_params=_cparams(("arbitrary", "arbitrary")),
        name="relative_bias_masks",
    )(rel_bias_table, buckets)


HALF = ATTN_WIDTH // 2


def _in_proj_kernel(x_ref, mod_ref, gain_ref, w_ref, qg_ref, kg_ref, grp_ref,
                    q1_ref, k1_ref, v1_ref, q4_ref, k4_ref, v4_ref, q16_ref, k16_ref, v16_ref, r_ref,
                    perm_ref):
    mod = mod_ref[...]
    h = _modulated_norm(x_ref[...], gain_ref[...], mod[1:2], mod[0:1]).astype(BF16)
    proj = _dot(h, w_ref[...].astype(BF16))
    grp = grp_ref[...]
    tm = proj.shape[0]

    def head_norm(t, gain):
        hi, lo = _split_bf16(t * t)
        ss = _dot(hi, grp) + _dot(lo, grp)
        return t * lax.rsqrt(ss * (1.0 / ATTN_HEAD_DIM) + EPS) * gain

    def emit(t, o1_ref, o4_ref, o16_ref):
        o1_ref[...] = t.astype(BF16)
        for j in range(ATTN_WIDTH // LANES):
            perm_ref[j] = t[:, j * LANES:(j + 1) * LANES]
        for dil, o_ref in ((4, o4_ref), (16, o16_ref)):
            for hf in range(2):
                for r in range(dil):
                    for jj in range(HALF // LANES):
                        c0 = (hf * dil + r) * HALF + jj * LANES
                        o_ref[:, c0:c0 + LANES] = perm_ref[hf * (HALF // LANES) + jj,
                                                           pl.ds(r, tm // dil, stride=dil), :].astype(BF16)

    emit(head_norm(proj[:, :ATTN_WIDTH], qg_ref[...]) * (ATTN_HEAD_DIM ** -0.5), q1_ref, q4_ref, q16_ref)
    emit(head_norm(proj[:, ATTN_WIDTH:2 * ATTN_WIDTH], kg_ref[...]), k1_ref, k4_ref, k16_ref)
    emit(proj[:, 2 * ATTN_WIDTH:3 * ATTN_WIDTH], v1_ref, v4_ref, v16_ref)
    r_ref[...] = proj[:, 3 * ATTN_WIDTH:]


def _in_proj(x, mod, gain, w_in, q_gain, k_gain):
    tokens = x.shape[0]
    tm = TM_PROJ
    per_seq = SEQ // tm
    grp = np.kron(np.eye(ATTN_HEADS), np.ones((ATTN_HEAD_DIM, ATTN_HEAD_DIM))).astype(np.float32)
    row = lambda i: (i, 0)
    const = lambda i: (0, 0)
    layouts = []
    for dil in (1, 4, 16):
        shape = jax.ShapeDtypeStruct((tokens // dil, dil * ATTN_WIDTH), BF16)
        spec = pl.BlockSpec((tm // dil, dil * ATTN_WIDTH), row)
        layouts.append(((shape,) * 3, (spec,) * 3))
    out_shape = sum((s for s, _ in layouts), ()) + (jax.ShapeDtypeStruct((tokens, RET_IN_WIDTH), F32),)
    out_specs = sum((s for _, s in layouts), ()) + (pl.BlockSpec((tm, RET_IN_WIDTH), row),)
    return pl.pallas_call(
        _in_proj_kernel,
        out_shape=out_shape,
        grid=(tokens // tm,),
        in_specs=[
            pl.BlockSpec((tm, D_MODEL), row),
            pl.BlockSpec((None, 6, D_MODEL), lambda i: (i // per_seq, 0, 0)),
            pl.BlockSpec((1, D_MODEL), const),
            pl.BlockSpec((D_MODEL, IN_WIDTH), const, pipeline_mode=pl.Buffered(1)),
            pl.BlockSpec((1, ATTN_WIDTH), const),
            pl.BlockSpec((1, ATTN_WIDTH), const),
            pl.BlockSpec((ATTN_WIDTH, ATTN_WIDTH), const),
        ],
        out_specs=out_specs,
        scratch_shapes=[pltpu.VMEM((ATTN_WIDTH // LANES, tm, LANES), F32)],
        compiler_params=_cparams(("arbitrary",)),
        name="in_projection",
    )(x, mod, gain.reshape(1, D_MODEL), w_in,
      jnp.tile(q_gain, ATTN_HEADS).reshape(1, ATTN_WIDTH),
      jnp.tile(k_gain, ATTN_HEADS).reshape(1, ATTN_WIDTH),
      jnp.asarray(grp, BF16))


PAIRS_PER_HALF = ATTN_HEADS // 4
GROUP = 4


def _pair_scores(qp, kp, vp, bias2, masks, low):
    q2 = jnp.concatenate([qp * masks[0], qp * masks[1]], axis=0)
    s = _dot_nt(q2, kp) + bias2
    m = jnp.max(s, axis=-1, keepdims=True)
    p = jnp.exp(s - m)
    den = jnp.sum(p, axis=-1, keepdims=True)
    pv = _dot(p.astype(BF16), vp)
    pick = lambda t: jnp.where(low, t[:BLOCK], t[BLOCK:])
    return pick(pv), pick(m), pick(den)


def _attn_kernel(q1_ref, k1_ref, v1_ref, q4_ref, k4_ref, v4_ref, q16_ref, k16_ref, v16_ref, bm_ref,
                 o_ref, acc_ref, max_ref, den_ref):
    lane = lax.broadcasted_iota(jnp.int32, (BLOCK, LANES), 1)
    low = lane < ATTN_HEAD_DIM
    masks = (jnp.where(low, 1.0, 0.0).astype(BF16), jnp.where(low, 0.0, 1.0).astype(BF16))

    def block(q_ref, k_ref, v_ref, c0, q0, w0, width, bias_set, rows, first):
        for p in range(PAIRS_PER_HALF):
            cs = slice(c0 + p * LANES, c0 + (p + 1) * LANES)
            acc, m, den = _pair_scores(q_ref[pl.ds(q0, BLOCK), cs], k_ref[pl.ds(w0, width), cs],
                                       v_ref[pl.ds(w0, width), cs], bm_ref[bias_set, p, :, 0:width],
                                       masks, low)
            if not first:
                m_old = max_ref[p, rows, :]
                m_new = jnp.maximum(m_old, m)
                a, b = jnp.exp(m_old - m_new), jnp.exp(m - m_new)
                den = den_ref[p, rows, :] * a + den * b
                acc = acc_ref[p, rows, :] * a + acc * b
                m = m_new
            max_ref[p, rows, :] = m
            den_ref[p, rows, :] = den
            acc_ref[p, rows, :] = acc

    def d1_group(g, carry):
        for u in range(GROUP):
            n = g * GROUP + u
            q0 = pl.multiple_of(n * BLOCK, BLOCK)
            w0 = pl.multiple_of(jnp.maximum(n - 1, 0) * BLOCK, BLOCK)
            bias_set = jnp.where(n == 0, BIAS_FIRST[1], BIAS_FULL[1])
            block(q1_ref, k1_ref, v1_ref, 0, q0, w0, 2 * BLOCK, bias_set, pl.ds(q0, BLOCK), True)
        return carry
    lax.fori_loop(0, SEQ // BLOCK // GROUP, d1_group, 0)

    for r in range(4):
        for n in range(SEQ // 4 // BLOCK):
            block(q4_ref, k4_ref, v4_ref, r * HALF, n * BLOCK, max(n - 1, 0) * BLOCK, 2 * BLOCK,
                  BIAS_FIRST[4] if n == 0 else BIAS_FULL[4],
                  pl.ds(r + 4 * BLOCK * n, BLOCK, stride=4), False)

    for r in range(16):
        block(q16_ref, k16_ref, v16_ref, r * HALF, 0, 0, BLOCK, BIAS_FIRST[16],
              pl.ds(r, BLOCK, stride=16), False)

    for n in range(SEQ // BLOCK):
        rows = slice(n * BLOCK, (n + 1) * BLOCK)
        for p in range(PAIRS_PER_HALF):
            o_ref[rows, p * LANES:(p + 1) * LANES] = (acc_ref[p, rows, :] / den_ref[p, rows, :]).astype(BF16)


def _dilated_attention(qkv, bias_masks):
    tokens = qkv[0].shape[0]
    batch = tokens // SEQ
    specs = []
    for dil in (1, 4, 16):
        specs += [pl.BlockSpec((SEQ // dil, dil * HALF), lambda b, hf: (b, hf))] * 3
    state = pltpu.VMEM((PAIRS_PER_HALF, SEQ, LANES), F32)
    return pl.pallas_call(
        _attn_kernel,
        out_shape=jax.ShapeDtypeStruct((tokens, ATTN_WIDTH), BF16),
        grid=(batch, 2),
        in_specs=specs + [pl.BlockSpec((N_BIAS_SETS, PAIRS_PER_HALF, 2 * BLOCK, 2 * BLOCK),
                                       lambda b, hf: (0, hf, 0, 0))],
        out_specs=pl.BlockSpec((SEQ, HALF), lambda b, hf: (b, hf)),
        scratch_shapes=[state, state, state],
        compiler_params=_cparams(("arbitrary", "arbitrary")),
        name="dilated_attention",
    )(*qkv, bias_masks)


def _retention_kernel(r_ref, cos_ref, sin_ref, dmask_ref, qdec_ref, kdec_ref, cdec_ref, gain_ref,
                      o_ref, state_ref):
    @pl.when(pl.program_id(1) == 0)
    def _():
        state_ref[...] = jnp.zeros_like(state_ref)

    lane = lax.broadcasted_iota(jnp.int32, (RET_CHUNK, LANES), 1)
    low = lane < RET_KEY_DIM
    first_half = (lane % RET_KEY_DIM) < (RET_KEY_DIM // 2)

    def rotate(t, cos, sin):
        partner = jnp.where(first_half, pltpu.roll(t, LANES - RET_KEY_DIM // 2, 1),
                            pltpu.roll(t, RET_KEY_DIM // 2, 1))
        return t * cos + partner * sin

    for c in range(RET_ROWS // RET_CHUNK):
        rows = slice(c * RET_CHUNK, (c + 1) * RET_CHUNK)
        for hp in range(RET_HEADS // 2):
            qs = slice(hp * LANES, (hp + 1) * LANES)
            ks = slice(RET_QK_WIDTH + hp * LANES, RET_QK_WIDTH + (hp + 1) * LANES)
            cos, sin = cos_ref[rows, qs], sin_ref[rows, qs]
            q_pair = rotate(r_ref[rows, qs], cos, sin) * (RET_KEY_DIM ** -0.5)
            k_pair = rotate(r_ref[rows, ks], cos, sin)
            for hh in range(2):
                head = 2 * hp + hh
                vs = slice(2 * RET_QK_WIDTH + head * LANES, 2 * RET_QK_WIDTH + (head + 1) * LANES)
                gs = slice(2 * RET_QK_WIDTH + RET_WIDTH + head * LANES,
                           2 * RET_QK_WIDTH + RET_WIDTH + (head + 1) * LANES)
                keep = low if hh == 0 else jnp.logical_not(low)
                qm = jnp.where(keep, q_pair, 0.0)
                vb = r_ref[rows, vs].astype(BF16)
                state = state_ref[head]
                inner = _dot_nt(qm.astype(BF16), k_pair.astype(BF16)) * dmask_ref[head]
                y = _dot(inner.astype(BF16), vb)
                y = y + _dot((qm * qdec_ref[head]).astype(BF16), state.astype(BF16))
                state_ref[head] = state * cdec_ref[head] + _dot_tn((k_pair * kdec_ref[head]).astype(BF16), vb)
                mu = jnp.mean(y, axis=-1, keepdims=True)
                yc = y - mu
                var = jnp.mean(yc * yc, axis=-1, keepdims=True)
                yn = yc * lax.rsqrt(var + EPS) * gain_ref[:, head * LANES:(head + 1) * LANES]
                o_ref[rows, head * LANES:(head + 1) * LANES] = (_silu(r_ref[rows, gs]) * yn).astype(BF16)


def _retention_tables():
    half = RET_KEY_DIM // 2
    pos = jnp.arange(SEQ, dtype=F32)
    inv = ROPE_BASE ** (-jnp.arange(half, dtype=F32) / half)
    ang = pos[:, None] * inv[None, :]
    cos, sin = jnp.cos(ang), jnp.sin(ang)
    cos_full = jnp.tile(jnp.concatenate([cos, cos], axis=-1), (1, RET_HEADS))
    sin_signed = jnp.tile(jnp.concatenate([-sin, sin], axis=-1), (1, RET_HEADS))
    log_g = jnp.log(1.0 - 2.0 ** (-5.0 - jnp.arange(RET_HEADS, dtype=F32)))
    idx = jnp.arange(RET_CHUNK, dtype=F32)
    diff = idx[:, None] - idx[None, :]
    dmask = jnp.where(diff >= 0, jnp.exp(jnp.maximum(diff, 0.0)[None] * log_g[:, None, None]), 0.0)
    q_decay = jnp.exp((idx + 1.0)[None, :] * log_g[:, None])[..., None]
    k_decay = jnp.exp((RET_CHUNK - 1.0 - idx)[None, :] * log_g[:, None])[..., None]
    chunk_decay = jnp.exp(RET_CHUNK * log_g)[:, None, None]
    full = (RET_HEADS, RET_CHUNK, LANES)
    return (cos_full, sin_signed, dmask, jnp.broadcast_to(q_decay, full),
            jnp.broadcast_to(k_decay, full), jnp.broadcast_to(chunk_decay, full))


def _retention(ret_in, ret_gain):
    tokens = ret_in.shape[0]
    batch = tokens // SEQ
    per_seq = SEQ // RET_ROWS
    cos, sin, dmask, qdec, kdec, cdec = _retention_tables()
    tab = pl.BlockSpec((RET_ROWS, RET_QK_WIDTH), lambda b, j: (j, 0))
    const3 = pl.BlockSpec((RET_HEADS, RET_CHUNK, LANES), lambda b, j: (0, 0, 0))
    return pl.pallas_call(
        _retention_kernel,
        out_shape=jax.ShapeDtypeStruct((tokens, RET_WIDTH), BF16),
        grid=(batch, per_seq),
        in_specs=[
            pl.BlockSpec((RET_ROWS, RET_IN_WIDTH), lambda b, j: (b * per_seq + j, 0)),
            tab, tab, const3, const3, const3, const3,
            pl.BlockSpec((1, RET_WIDTH), lambda b, j: (0, 0)),
        ],
        out_specs=pl.BlockSpec((RET_ROWS, RET_WIDTH), lambda b, j: (b * per_seq + j, 0)),
        scratch_shapes=[pltpu.VMEM((RET_HEADS, LANES, RET_VALUE_DIM), F32)],
        compiler_params=_cparams(("arbitrary", "arbitrary")),
        name="retention",
    )(ret_in, cos, sin, dmask, qdec, kdec, cdec, ret_gain.reshape(1, RET_WIDTH))


def _out_proj_kernel(attn_ref, ret_ref, x_ref, mod_ref, w_ref, out_ref):
    mix = (_dot(attn_ref[...], w_ref[:ATTN_WIDTH, :].astype(BF16))
           + _dot(ret_ref[...], w_ref[ATTN_WIDTH:, :].astype(BF16)))
    out_ref[...] = x_ref[...] + mod_ref[2:3, :] * mix


def _out_proj(attn, ret, x, mod, w_out):
    tokens = x.shape[0]
    tm = TM_FFN
    per_seq = SEQ // tm
    row = lambda i: (i, 0)
    return pl.pallas_call(
        _out_proj_kernel,
        out_shape=jax.ShapeDtypeStruct((tokens, D_MODEL), F32),
        grid=(tokens // tm,),
        in_specs=[pl.BlockSpec((tm, ATTN_WIDTH), row),
                  pl.BlockSpec((tm, RET_WIDTH), row),
                  pl.BlockSpec((tm, D_MODEL), row),
                  pl.BlockSpec((None, 6, D_MODEL), lambda i: (i // per_seq, 0, 0)),
                  pl.BlockSpec((D_MODEL, D_MODEL), lambda i: (0, 0))],
        out_specs=pl.BlockSpec((tm, D_MODEL), row),
        compiler_params=_cparams(("arbitrary",)),
        name="out_projection",
    )(attn, ret, x, mod, w_out)


def _swiglu_chunks(h, w1_ref, w3_ref, w2_ref):
    d_ff = w1_ref.shape[-1]
    total = None
    for c0 in range(0, d_ff, MXU_COLS):
        c1 = min(c0 + MXU_COLS, d_ff)
        z = (_silu(_dot(h, w1_ref[:, c0:c1].astype(BF16))) * _dot(h, w3_ref[:, c0:c1].astype(BF16))).astype(BF16)
        part = _dot(z, w2_ref[c0:c1, :].astype(BF16))
        total = part if total is None else total + part
    return total


def _ffn_kernel(x_ref, mod_ref, gain_ref, w1_ref, w3_ref, w2_ref, out_ref):
    mod = mod_ref[...]
    x = x_ref[...]
    h = _modulated_norm(x, gain_ref[...], mod[4:5], mod[3:4]).astype(BF16)
    out_ref[...] = x + mod[5:6, :] * _swiglu_chunks(h, w1_ref, w3_ref, w2_ref)


def _dense_ffn(x, mod, gain, w1, w3, w2):
    tokens = x.shape[0]
    d_ff = w1.shape[1]
    tm = TM_FFN
    per_seq = SEQ // tm
    resident = pl.Buffered(1)
    return pl.pallas_call(
        _ffn_kernel,
        out_shape=jax.ShapeDtypeStruct((tokens, D_MODEL), F32),
        grid=(tokens // tm,),
        in_specs=[
            pl.BlockSpec((tm, D_MODEL), lambda i: (i, 0)),
            pl.BlockSpec((None, 6, D_MODEL), lambda i: (i // per_seq, 0, 0)),
            pl.BlockSpec((1, D_MODEL), lambda i: (0, 0)),
            pl.BlockSpec((D_MODEL, d_ff), lambda i: (0, 0), pipeline_mode=resident),
            pl.BlockSpec((D_MODEL, d_ff), lambda i: (0, 0), pipeline_mode=resident),
            pl.BlockSpec((d_ff, D_MODEL), lambda i: (0, 0), pipeline_mode=resident),
        ],
        out_specs=pl.BlockSpec((tm, D_MODEL), lambda i: (i, 0)),
        compiler_params=_cparams(("arbitrary",)),
        name="dense_swiglu",
    )(x, mod, gain.reshape(1, D_MODEL), w1, w3, w2)


def _router_kernel(x_ref, mod_ref, gain_ref, wr_ref, h_ref, gates_ref, pos_ref, post_ref, start_ref,
                   cnt_ref, carry_ref, *, tiles_per_seq):
    i = pl.program_id(0)

    @pl.when(i % tiles_per_seq == 0)
    def _():
        carry_ref[...] = jnp.zeros_like(carry_ref)

    mod = mod_ref[...]
    h = _modulated_norm(x_ref[...], gain_ref[...], mod[4:5], mod[3:4]).astype(BF16)
    h_ref[...] = h
    tm = h.shape[0]
    lane = lax.broadcasted_iota(jnp.int32, (tm, LANES), 1).astype(F32)
    logits = jnp.where(lane < N_EXPERTS, _dot(h, wr_ref[...]), -jnp.inf)
    m1 = jnp.max(logits, axis=-1, keepdims=True)
    i1 = jnp.min(jnp.where(logits == m1, lane, float(LANES)), axis=-1, keepdims=True)
    rest = jnp.where(lane == i1, -jnp.inf, logits)
    m2 = jnp.max(rest, axis=-1, keepdims=True)
    i2 = jnp.min(jnp.where(rest == m2, lane, float(LANES)), axis=-1, keepdims=True)
    e2 = jnp.exp(m2 - m1)
    g1 = 1.0 / (1.0 + e2)
    g2 = e2 / (1.0 + e2)
    gates_ref[...] = jnp.where(lane == i1, g1, 0.0) + jnp.where(lane == i2, g2, 0.0)
    chosen = (lane == i1) | (lane == i2)
    onehot = jnp.where(chosen, 1.0, 0.0)
    r = lax.broadcasted_iota(jnp.int32, (tm, tm), 0)
    c = lax.broadcasted_iota(jnp.int32, (tm, tm), 1)
    tril = jnp.where(c <= r, 1.0, 0.0).astype(BF16)
    incl = _dot(tril, onehot.astype(BF16))
    carry = carry_ref[0:1, :]
    start_ref[...] = carry_ref[...]
    pos = jnp.where(chosen, incl - 1.0 + carry, -1.0)
    pos_ref[...] = pos
    post_ref[...] = pos.T[:N_EXPERTS, :]
    total = carry + incl[tm - 1:tm, :]
    carry_ref[...] = jnp.broadcast_to(total, carry_ref.shape)
    cnt_ref[...] = jnp.broadcast_to(total, cnt_ref.shape)


def _router(x, mod, gain, w_router):
    tokens = x.shape[0]
    batch = tokens // SEQ
    tm = TM_PROJ
    per_seq = SEQ // tm
    wr = jnp.zeros((D_MODEL, LANES), BF16).at[:, :N_EXPERTS].set(w_router.astype(BF16))
    row = lambda i: (i, 0)
    return pl.pallas_call(
        functools.partial(_router_kernel, tiles_per_seq=per_seq),
        out_shape=(
            jax.ShapeDtypeStruct((tokens, D_MODEL), BF16),
            jax.ShapeDtypeStruct((tokens, LANES), F32),
            jax.ShapeDtypeStruct((tokens, LANES), F32),
            jax.ShapeDtypeStruct((N_EXPERTS, tokens), F32),
            jax.ShapeDtypeStruct((tokens // tm, 8, LANES), F32),
            jax.ShapeDtypeStruct((batch, 8, LANES), F32),
        ),
        grid=(tokens // tm,),
        in_specs=[
            pl.BlockSpec((tm, D_MODEL), row),
            pl.BlockSpec((None, 6, D_MODEL), lambda i: (i // per_seq, 0, 0)),
            pl.BlockSpec((1, D_MODEL), lambda i: (0, 0)),
            pl.BlockSpec((D_MODEL, LANES), lambda i: (0, 0)),
        ],
        out_specs=(
            pl.BlockSpec((tm, D_MODEL), row),
            pl.BlockSpec((tm, LANES), row),
            pl.BlockSpec((tm, LANES), row),
            pl.BlockSpec((N_EXPERTS, tm), lambda i: (0, i)),
            pl.BlockSpec((None, 8, LANES), lambda i: (i, 0, 0)),
            pl.BlockSpec((None, 8, LANES), lambda i: (i // per_seq, 0, 0)),
        ),
        scratch_shapes=[pltpu.VMEM((8, LANES), F32)],
        compiler_params=_cparams(("arbitrary",)),
        name="expert_router",
    )(x, mod, gain.reshape(1, D_MODEL), wr)


TOK_BLOCKS = SEQ // TR_MOE
TAIL_TILES = (64, 128, TR_MOE)
SPILL_ROWS = 16
SMALL_WINDOW = 128


def _moe_kernel(tbl_ref, h_ref, post_ref, pos_ref, gates_ref, x_ref, mod_ref, w1_ref, w3_ref, w2_ref,
                out_ref, hs_ref, acc_ref):
    b, e, f = pl.program_id(0), pl.program_id(1), pl.program_id(2)
    last_f = pl.num_programs(2) - 1
    tr = TR_MOE
    base = b * (TOK_BLOCKS + 1) * N_EXPERTS + e
    bounds = [tbl_ref[base + tb * N_EXPERTS] for tb in range(TOK_BLOCKS + 1)]
    win = [pl.multiple_of(lax.shift_left(lax.shift_right_logical(s, 3), 3), SUBLANES) for s in bounds[:-1]]
    all_small = functools.reduce(jnp.logical_and,
                                 [bounds[tb + 1] - win[tb] <= SMALL_WINDOW for tb in range(TOK_BLOCKS)])
    total = bounds[-1]
    n_tiles = lax.shift_right_logical(total + (tr - 1), tr.bit_length() - 1)

    def clear(r, carry):
        acc_ref[pl.ds(pl.multiple_of(r * tr, tr), tr), :] = jnp.zeros((tr, D_MODEL), F32)
        return carry

    @pl.when((e == 0) & (f == 0))
    def _():
        out_ref[...] = jnp.zeros_like(out_ref)

    @pl.when(f == 0)
    def _():
        out_ref[pl.ds(pl.multiple_of(e * tr, tr), tr), :] += x_ref[...]
        lax.fori_loop(0, n_tiles + 2, clear, 0)

        def gather(height):
            slot = lax.broadcasted_iota(jnp.int32, (height, tr), 0).astype(F32)
            for tb in range(TOK_BLOCKS):
                ts = slice(tb * tr, (tb + 1) * tr)
                local = post_ref[pl.ds(e, 1), ts] - win[tb].astype(F32)
                sel = jnp.where(local == slot, 1.0, 0.0).astype(BF16)
                acc_ref[pl.ds(win[tb], height), :] += _dot(sel, h_ref[ts, :])

        @pl.when(all_small)
        def _():
            gather(SMALL_WINDOW)

        @pl.when(jnp.logical_not(all_small))
        def _():
            gather(tr + SPILL_ROWS)

        def to_rows(r, carry):
            rows = pl.ds(pl.multiple_of(r * tr, tr), tr)
            hs_ref[rows, :] = acc_ref[rows, :].astype(BF16)
            acc_ref[rows, :] = jnp.zeros((tr, D_MODEL), F32)
            return carry
        lax.fori_loop(0, n_tiles + 2, to_rows, 0)

    def swiglu(r0, m):
        rows = pl.ds(r0, m)
        hr = hs_ref[rows, :]
        z = (_silu(_dot(hr, w1_ref[...])) * _dot(hr, w3_ref[...])).astype(BF16)
        acc_ref[rows, :] += _dot(z, w2_ref[...])

    def full_tile(r, carry):
        swiglu(pl.multiple_of(r * tr, tr), tr)
        return carry
    n_full = lax.shift_right_logical(total, tr.bit_length() - 1)
    lax.fori_loop(0, n_full, full_tile, 0)
    rest = total - n_full * tr
    tail0 = pl.multiple_of(n_full * tr, tr)
    lo = 0
    for m in TAIL_TILES:
        @pl.when((rest > lo) & (rest <= m))
        def _():
            swiglu(tail0, m)
        lo = m

    @pl.when(f == last_f)
    def _():
        lane = lax.broadcasted_iota(jnp.int32, (tr, LANES), 1)
        layer_gate = mod_ref[5:6, :]

        def scatter(tb, first, height):
            ts = slice(tb * tr, (tb + 1) * tr)
            pos_col = jnp.sum(jnp.where(lane == e, pos_ref[ts, :], 0.0), axis=-1, keepdims=True)
            gate_col = jnp.sum(jnp.where(lane == e, gates_ref[ts, :], 0.0), axis=-1, keepdims=True)
            slot = lax.broadcasted_iota(jnp.int32, (tr, height), 1).astype(F32) + float(first)
            sel = jnp.where(pos_col - win[tb].astype(F32) == slot, 1.0, 0.0).astype(BF16)
            rows = acc_ref[pl.ds(win[tb] + first, height), :].astype(BF16)
            out_ref[ts, :] += layer_gate * (gate_col * _dot(sel, rows))

        @pl.when(all_small)
        def _():
            for tb in range(TOK_BLOCKS):
                scatter(tb, 0, SMALL_WINDOW)

        @pl.when(jnp.logical_not(all_small))
        def _():
            for tb in range(TOK_BLOCKS):
                scatter(tb, 0, tr)
            for tb in range(TOK_BLOCKS):
                @pl.when(bounds[tb + 1] - win[tb] > tr)
                def _():
                    scatter(tb, tr, SPILL_ROWS)


def _cast_kernel(w_ref, o_ref):
    o_ref[...] = w_ref[...].astype(BF16)


def _chunk_columns(w, tf):
    n_e, k, f = w.shape
    return pl.pallas_call(
        _cast_kernel,
        out_shape=jax.ShapeDtypeStruct((n_e, f // tf, k, tf), BF16),
        grid=(n_e, f // tf),
        in_specs=[pl.BlockSpec((None, k, tf), lambda e, c: (e, 0, c))],
        out_specs=pl.BlockSpec((None, None, k, tf), lambda e, c: (e, c, 0, 0)),
        compiler_params=_cparams(("arbitrary", "arbitrary")),
        name="chunk_expert_weights",
    )(w)


def _moe(x, mod, h, post, pos, gates, table, w1, w3, w2):
    tokens = h.shape[0]
    batch = tokens // SEQ
    d_ff = w1.shape[2]
    tf = TF_MOE
    n_f = d_ff // tf
    assert N_EXPERTS == TOK_BLOCKS
    once = pl.Buffered(1)
    max_rows = SEQ + 3 * TR_MOE
    grid_spec = pltpu.PrefetchScalarGridSpec(
        num_scalar_prefetch=1,
        grid=(batch, N_EXPERTS, n_f),
        in_specs=[
            pl.BlockSpec((SEQ, D_MODEL), lambda b, e, f, t: (b, 0), pipeline_mode=once),
            pl.BlockSpec((N_EXPERTS, SEQ), lambda b, e, f, t: (0, b), pipeline_mode=once),
            pl.BlockSpec((SEQ, LANES), lambda b, e, f, t: (b, 0), pipeline_mode=once),
            pl.BlockSpec((SEQ, LANES), lambda b, e, f, t: (b, 0), pipeline_mode=once),
            pl.BlockSpec((TR_MOE, D_MODEL), lambda b, e, f, t: (b * TOK_BLOCKS + e, 0)),
            pl.BlockSpec((None, 6, D_MODEL), lambda b, e, f, t: (b, 0, 0)),
            pl.BlockSpec((None, None, D_MODEL, tf), lambda b, e, f, t: (e, f, 0, 0)),
            pl.BlockSpec((None, None, D_MODEL, tf), lambda b, e, f, t: (e, f, 0, 0)),
            pl.BlockSpec((None, tf, D_MODEL), lambda b, e, f, t: (e, f, 0)),
        ],
        out_specs=pl.BlockSpec((SEQ, D_MODEL), lambda b, e, f, t: (b, 0), pipeline_mode=once),
        scratch_shapes=[pltpu.VMEM((max_rows, D_MODEL), BF16), pltpu.VMEM((max_rows, D_MODEL), F32)],
    )
    return pl.pallas_call(
        _moe_kernel,
        out_shape=jax.ShapeDtypeStruct((tokens, D_MODEL), F32),
        grid_spec=grid_spec,
        compiler_params=_cparams(("arbitrary", "arbitrary", "arbitrary")),
        name="expert_swiglu",
    )(table, h, post, pos, gates, x, mod, _chunk_columns(w1, tf), _chunk_columns(w3, tf), w2.astype(BF16))


def _token_mixer(x, mod, gain, w_in, q_gain, k_gain, ret_gain, w_out, bias_masks):
    *qkv, ret_in = _in_proj(x, mod, gain, w_in, q_gain, k_gain)
    attn = _dilated_attention(qkv, bias_masks)
    ret = _retention(ret_in, ret_gain)
    return _out_proj(attn, ret, x, mod, w_out)


def _moe_ffn(x, mod, gain, w_router, w1, w3, w2):
    h, gates, pos, post, start, cnt = _router(x, mod, gain, w_router)
    batch = cnt.shape[0]
    table = jnp.concatenate([start[:, 0, :N_EXPERTS].reshape(batch, TOK_BLOCKS, N_EXPERTS),
                             cnt[:, :1, :N_EXPERTS]], axis=1).astype(jnp.int32).reshape(-1)
    return _moe(x, mod, h, post, pos, gates, table, w1, w3, w2)


def kernel(x, c, rel_bias_table, norm_mix, norm_ffn, w_mod, b_mod, w_in, q_gain, k_gain, ret_gain, w_out,
           ffn_w1, ffn_w3, ffn_w2, moe_router, moe_w1, moe_w3, moe_w2):
    batch, seq, d_model = x.shape
    assert (seq, d_model) == (SEQ, D_MODEL)
    depth = w_mod.shape[0]
    mods = _modulation(c, w_mod, b_mod).reshape(depth, batch, 6, D_MODEL)
    bias_masks = _bias_masks(rel_bias_table)
    xt = x.reshape(batch * seq, d_model)
    for layer in range(depth):
        mod = mods[layer]
        xt = _token_mixer(xt, mod, norm_mix[layer], w_in[layer], q_gain[layer], k_gain[layer],
                          ret_gain[layer], w_out[layer], bias_masks)
        i = layer // 2
        if layer % 2 == 0:
            xt = _dense_ffn(xt, mod, norm_ffn[layer], ffn_w1[i], ffn_w3[i], ffn_w2[i])
        else:
            xt = _moe_ffn(xt, mod, norm_ffn[layer], moe_router[i], moe_w1[i], moe_w3[i], moe_w2[i])
    return xt.reshape(batch, seq, d_model)
```

```python
import functools
import math

import jax
import jax.numpy as jnp
import numpy as np
from jax import lax
from jax.experimental import pallas as pl
from jax.experimental.pallas import tpu as pltpu

D_MODEL = 1024
SEQ = 2048
ATTN_HEADS = 8
ATTN_HEAD_DIM = 64
ATTN_WIDTH = ATTN_HEADS * ATTN_HEAD_DIM
DILATED_PATTERNS = ((128, 1), (512, 4), (2048, 16))
BLOCK = 128
NUM_BUCKETS = 32
MAX_DISTANCE = 2048
RET_HEADS = 4
RET_KEY_DIM = 64
RET_VALUE_DIM = 128
RET_WIDTH = RET_HEADS * RET_VALUE_DIM
RET_QK_WIDTH = RET_HEADS * RET_KEY_DIM
RET_CHUNK = 128
ROPE_BASE = 10000.0
IN_WIDTH = 3 * ATTN_WIDTH + 2 * RET_QK_WIDTH + 2 * RET_WIDTH
RET_IN_WIDTH = IN_WIDTH - 3 * ATTN_WIDTH
N_EXPERTS = 8
EPS = 1e-6
NEG_INF = -1e30

LANES = 128
VMEM_LIMIT = 60 * 1024 * 1024

BF16 = jnp.bfloat16
F32 = jnp.float32

TM_PROJ = 256
TM_FFN = 512
MXU_COLS = 256
TR_MOE = 256
TF_MOE = 512
assert TM_PROJ == TR_MOE
RET_ROWS = 512
SUBLANES = 8


def _cparams(sem):
    return pltpu.CompilerParams(dimension_semantics=sem, vmem_limit_bytes=VMEM_LIMIT)


def _dot(a, b):
    return jnp.dot(a, b, preferred_element_type=F32)


def _dot_nt(a, b):
    return lax.dot_general(a, b, (((1,), (1,)), ((), ())), preferred_element_type=F32)


def _dot_tn(a, b):
    return lax.dot_general(a, b, (((0,), (0,)), ((), ())), preferred_element_type=F32)


def _split_bf16(v):
    hi = v.astype(BF16)
    lo = (v - hi.astype(F32)).astype(BF16)
    return hi, lo


def _silu(v):
    return v * (1.0 / (1.0 + jnp.exp(-v)))


def _modulated_norm(x, gain, scale, shift):
    ms = jnp.mean(x * x, axis=-1, keepdims=True)
    y = x * lax.rsqrt(ms + EPS) * gain
    return y * (1.0 + scale) + shift


def _mod_kernel(c_ref, w_ref, b_ref, o_ref):
    ca = _silu(c_ref[...]).astype(BF16)
    o_ref[...] = _dot(ca, w_ref[...].astype(BF16)) + b_ref[...]


def _modulation(c, w_mod, b_mod):
    depth, _, width = w_mod.shape
    batch = c.shape[0]
    tn = 1536
    return pl.pallas_call(
        _mod_kernel,
        out_shape=jax.ShapeDtypeStruct((depth, batch, width), F32),
        grid=(depth, width // tn),
        in_specs=[
            pl.BlockSpec((batch, D_MODEL), lambda l, n: (0, 0)),
            pl.BlockSpec((None, D_MODEL, tn), lambda l, n: (l, 0, n)),
            pl.BlockSpec((None, 1, tn), lambda l, n: (l, 0, n)),
        ],
        out_specs=pl.BlockSpec((None, batch, tn), lambda l, n: (l, 0, n)),
        compiler_params=_cparams(("arbitrary", "arbitrary")),
        name="adaln_modulation",
    )(c, w_mod, b_mod.reshape(depth, 1, width))


def _bias_kernel(table_ref, bucket_ref, o_ref):
    h = pl.program_id(1)
    bucket = bucket_ref[...]
    acc = jnp.full(bucket.shape, NEG_INF, F32)
    for b in range(NUM_BUCKETS):
        acc = jnp.where(bucket == b, table_ref[b, h], acc)
    o_ref[...] = acc


BIAS_FULL = {1: 0, 4: 2}
BIAS_FIRST = {1: 1, 4: 3, 16: 4}
N_BIAS_SETS = 5


def _bias_masks(rel_bias_table):
    i = jnp.arange(BLOCK)[:, None]
    j = jnp.arange(2 * BLOCK)[None, :]
    max_exact = NUM_BUCKETS // 2

    def bucket_of(rel, dilation, w_sub, exists):
        n = jnp.maximum(rel * dilation, 0)
        nf = jnp.maximum(n.astype(F32), float(max_exact))
        large = max_exact + (jnp.log(nf / max_exact) / math.log(MAX_DISTANCE / max_exact)
                             * (NUM_BUCKETS - max_exact)).astype(jnp.int32)
        large = jnp.minimum(large, NUM_BUCKETS - 1)
        bucket = jnp.where(n < max_exact, n, large)
        allowed = (rel >= 0) & (rel <= w_sub) & exists
        return jnp.where(allowed, bucket, -1)

    sets = [None] * N_BIAS_SETS
    for window, dilation in DILATED_PATTERNS:
        w_sub = window // dilation
        if dilation in BIAS_FULL:
            sets[BIAS_FULL[dilation]] = bucket_of(i - j + BLOCK, dilation, w_sub, j >= 0)
        sets[BIAS_FIRST[dilation]] = bucket_of(i - j, dilation, w_sub, j < BLOCK)
    buckets = jnp.stack(sets).astype(jnp.int32)
    return pl.pallas_call(
        _bias_kernel,
        out_shape=jax.ShapeDtypeStruct((N_BIAS_SETS, ATTN_HEADS // 2, 2 * BLOCK, 2 * BLOCK), F32),
        grid=(N_BIAS_SETS, ATTN_HEADS),
        in_specs=[
            pl.BlockSpec(memory_space=pltpu.SMEM),
            pl.BlockSpec((None, BLOCK, 2 * BLOCK), lambda p, h: (p, 0, 0)),
        ],
        out_specs=pl.BlockSpec((None, None, BLOCK, 2 * BLOCK), lambda p, h: (p, h // 2, h % 2, 0)),
        compiler_params=_cparams(("arbitrary", "arbitrary")),
        name="relative_bias_masks",
    )(rel_bias_table, buckets)


HALF = ATTN_WIDTH // 2


def _in_proj_kernel(x_ref, mod_ref, gain_ref, w_ref, qg_ref, kg_ref, grp_ref,
                    q1_ref, k1_ref, v1_ref, q4_ref, k4_ref, v4_ref, q16_ref, k16_ref, v16_ref, r_ref,
                    perm_ref):
    mod = mod_ref[...]
    h = _modulated_norm(x_ref[...], gain_ref[...], mod[1:2], mod[0:1]).astype(BF16)
    proj = _dot(h, w_ref[...].astype(BF16))
    grp = grp_ref[...]
    tm = proj.shape[0]

    def head_norm(t, gain):
        hi, lo = _split_bf16(t * t)
        ss = _dot(hi, grp) + _dot(lo, grp)
        return t * lax.rsqrt(ss * (1.0 / ATTN_HEAD_DIM) + EPS) * gain

    def emit(t, o1_ref, o4_ref, o16_ref):
        o1_ref[...] = t.astype(BF16)
        for j in range(ATTN_WIDTH // LANES):
            perm_ref[j] = t[:, j * LANES:(j + 1) * LANES]
        for dil, o_ref in ((4, o4_ref), (16, o16_ref)):
            for hf in range(2):
                for r in range(dil):
                    for jj in range(HALF // LANES):
                        c0 = (hf * dil + r) * HALF + jj * LANES
                        o_ref[:, c0:c0 + LANES] = perm_ref[hf * (HALF // LANES) + jj,
                                                           pl.ds(r, tm // dil, stride=dil), :].astype(BF16)

    emit(head_norm(proj[:, :ATTN_WIDTH], qg_ref[...]) * (ATTN_HEAD_DIM ** -0.5), q1_ref, q4_ref, q16_ref)
    emit(head_norm(proj[:, ATTN_WIDTH:2 * ATTN_WIDTH], kg_ref[...]), k1_ref, k4_ref, k16_ref)
    emit(proj[:, 2 * ATTN_WIDTH:3 * ATTN_WIDTH], v1_ref, v4_ref, v16_ref)
    r_ref[...] = proj[:, 3 * ATTN_WIDTH:]


def _in_proj(x, mod, gain, w_in, q_gain, k_gain):
    tokens = x.shape[0]
    tm = TM_PROJ
    per_seq = SEQ // tm
    grp = np.kron(np.eye(ATTN_HEADS), np.ones((ATTN_HEAD_DIM, ATTN_HEAD_DIM))).astype(np.float32)
    row = lambda i: (i, 0)
    const = lambda i: (0, 0)
    layouts = []
    for dil in (1, 4, 16):
        shape = jax.ShapeDtypeStruct((tokens // dil, dil * ATTN_WIDTH), BF16)
        spec = pl.BlockSpec((tm // dil, dil * ATTN_WIDTH), row)
        layouts.append(((shape,) * 3, (spec,) * 3))
    out_shape = sum((s for s, _ in layouts), ()) + (jax.ShapeDtypeStruct((tokens, RET_IN_WIDTH), F32),)
    out_specs = sum((s for _, s in layouts), ()) + (pl.BlockSpec((tm, RET_IN_WIDTH), row),)
    return pl.pallas_call(
        _in_proj_kernel,
        out_shape=out_shape,
        grid=(tokens // tm,),
        in_specs=[
            pl.BlockSpec((tm, D_MODEL), row),
            pl.BlockSpec((None, 6, D_MODEL), lambda i: (i // per_seq, 0, 0)),
            pl.BlockSpec((1, D_MODEL), const),
            pl.BlockSpec((D_MODEL, IN_WIDTH), const, pipeline_mode=pl.Buffered(1)),
            pl.BlockSpec((1, ATTN_WIDTH), const),
            pl.BlockSpec((1, ATTN_WIDTH), const),
            pl.BlockSpec((ATTN_WIDTH, ATTN_WIDTH), const),
        ],
        out_specs=out_specs,
        scratch_shapes=[pltpu.VMEM((ATTN_WIDTH // LANES, tm, LANES), F32)],
        compiler_params=_cparams(("arbitrary",)),
        name="in_projection",
    )(x, mod, gain.reshape(1, D_MODEL), w_in,
      jnp.tile(q_gain, ATTN_HEADS).reshape(1, ATTN_WIDTH),
      jnp.tile(k_gain, ATTN_HEADS).reshape(1, ATTN_WIDTH),
      jnp.asarray(grp, BF16))


PAIRS_PER_HALF = ATTN_HEADS // 4
GROUP = 4


def _pair_scores(qp, kp, vp, bias2, masks, low):
    q2 = jnp.concatenate([qp * masks[0], qp * masks[1]], axis=0)
    s = _dot_nt(q2, kp) + bias2
    m = jnp.max(s, axis=-1, keepdims=True)
    p = jnp.exp(s - m)
    den = jnp.sum(p, axis=-1, keepdims=True)
    pv = _dot(p.astype(BF16), vp)
    pick = lambda t: jnp.where(low, t[:BLOCK], t[BLOCK:])
    return pick(pv), pick(m), pick(den)


def _attn_kernel(q1_ref, k1_ref, v1_ref, q4_ref, k4_ref, v4_ref, q16_ref, k16_ref, v16_ref, bm_ref,
                 o_ref, acc_ref, max_ref, den_ref):
    lane = lax.broadcasted_iota(jnp.int32, (BLOCK, LANES), 1)
    low = lane < ATTN_HEAD_DIM
    masks = (jnp.where(low, 1.0, 0.0).astype(BF16), jnp.where(low, 0.0, 1.0).astype(BF16))

    def block(q_ref, k_ref, v_ref, c0, q0, w0, width, bias_set, rows, first):
        for p in range(PAIRS_PER_HALF):
            cs = slice(c0 + p * LANES, c0 + (p + 1) * LANES)
            acc, m, den = _pair_scores(q_ref[pl.ds(q0, BLOCK), cs], k_ref[pl.ds(w0, width), cs],
                                       v_ref[pl.ds(w0, width), cs], bm_ref[bias_set, p, :, 0:width],
                                       masks, low)
            if not first:
                m_old = max_ref[p, rows, :]
                m_new = jnp.maximum(m_old, m)
                a, b = jnp.exp(m_old - m_new), jnp.exp(m - m_new)
                den = den_ref[p, rows, :] * a + den * b
                acc = acc_ref[p, rows, :] * a + acc * b
                m = m_new
            max_ref[p, rows, :] = m
            den_ref[p, rows, :] = den
            acc_ref[p, rows, :] = acc

    def d1_group(g, carry):
        for u in range(GROUP):
            n = g * GROUP + u
            q0 = pl.multiple_of(n * BLOCK, BLOCK)
            w0 = pl.multiple_of(jnp.maximum(n - 1, 0) * BLOCK, BLOCK)
            bias_set = jnp.where(n == 0, BIAS_FIRST[1], BIAS_FULL[1])
            block(q1_ref, k1_ref, v1_ref, 0, q0, w0, 2 * BLOCK, bias_set, pl.ds(q0, BLOCK), True)
        return carry
    lax.fori_loop(0, SEQ // BLOCK // GROUP, d1_group, 0)

    for r in range(4):
        for n in range(SEQ // 4 // BLOCK):
            block(q4_ref, k4_ref, v4_ref, r * HALF, n * BLOCK, max(n - 1, 0) * BLOCK, 2 * BLOCK,
                  BIAS_FIRST[4] if n == 0 else BIAS_FULL[4],
                  pl.ds(r + 4 * BLOCK * n, BLOCK, stride=4), False)

    for r in range(16):
        block(q16_ref, k16_ref, v16_ref, r * HALF, 0, 0, BLOCK, BIAS_FIRST[16],
              pl.ds(r, BLOCK, stride=16), False)

    for n in range(SEQ // BLOCK):
        rows = slice(n * BLOCK, (n + 1) * BLOCK)
        for p in range(PAIRS_PER_HALF):
            o_ref[rows, p * LANES:(p + 1) * LANES] = (acc_ref[p, rows, :] / den_ref[p, rows, :]).astype(BF16)


def _dilated_attention(qkv, bias_masks):
    tokens = qkv[0].shape[0]
    batch = tokens // SEQ
    specs = []
    for dil in (1, 4, 16):
        specs += [pl.BlockSpec((SEQ // dil, dil * HALF), lambda b, hf: (b, hf))] * 3
    state = pltpu.VMEM((PAIRS_PER_HALF, SEQ, LANES), F32)
    return pl.pallas_call(
        _attn_kernel,
        out_shape=jax.ShapeDtypeStruct((tokens, ATTN_WIDTH), BF16),
        grid=(batch, 2),
        in_specs=specs + [pl.BlockSpec((N_BIAS_SETS, PAIRS_PER_HALF, 2 * BLOCK, 2 * BLOCK),
                                       lambda b, hf: (0, hf, 0, 0))],
        out_specs=pl.BlockSpec((SEQ, HALF), lambda b, hf: (b, hf)),
        scratch_shapes=[state, state, state],
        compiler_params=_cparams(("arbitrary", "arbitrary")),
        name="dilated_attention",
    )(*qkv, bias_masks)


def _retention_kernel(r_ref, cos_ref, sin_ref, dmask_ref, qdec_ref, kdec_ref, cdec_ref, gain_ref,
                      o_ref, state_ref):
    @pl.when(pl.program_id(1) == 0)
    def _():
        state_ref[...] = jnp.zeros_like(state_ref)

    lane = lax.broadcasted_iota(jnp.int32, (RET_CHUNK, LANES), 1)
    low = lane < RET_KEY_DIM
    first_half = (lane % RET_KEY_DIM) < (RET_KEY_DIM // 2)

    def rotate(t, cos, sin):
        partner = jnp.where(first_half, pltpu.roll(t, LANES - RET_KEY_DIM // 2, 1),
                            pltpu.roll(t, RET_KEY_DIM // 2, 1))
        return t * cos + partner * sin

    for c in range(RET_ROWS // RET_CHUNK):
        rows = slice(c * RET_CHUNK, (c + 1) * RET_CHUNK)
        for hp in range(RET_HEADS // 2):
            qs = slice(hp * LANES, (hp + 1) * LANES)
            ks = slice(RET_QK_WIDTH + hp * LANES, RET_QK_WIDTH + (hp + 1) * LANES)
            cos, sin = cos_ref[rows, qs], sin_ref[rows, qs]
            q_pair = rotate(r_ref[rows, qs], cos, sin) * (RET_KEY_DIM ** -0.5)
            k_pair = rotate(r_ref[rows, ks], cos, sin)
            for hh in range(2):
                head = 2 * hp + hh
                vs = slice(2 * RET_QK_WIDTH + head * LANES, 2 * RET_QK_WIDTH + (head + 1) * LANES)
                gs = slice(2 * RET_QK_WIDTH + RET_WIDTH + head * LANES,
                           2 * RET_QK_WIDTH + RET_WIDTH + (head + 1) * LANES)
                keep = low if hh == 0 else jnp.logical_not(low)
                qm = jnp.where(keep, q_pair, 0.0)
                vb = r_ref[rows, vs].astype(BF16)
                state = state_ref[head]
                inner = _dot_nt(qm.astype(BF16), k_pair.astype(BF16)) * dmask_ref[head]
                y = _dot(inner.astype(BF16), vb)
                y = y + _dot((qm * qdec_ref[head]).astype(BF16), state.astype(BF16))
                state_ref[head] = state * cdec_ref[head] + _dot_tn((k_pair * kdec_ref[head]).astype(BF16), vb)
                mu = jnp.mean(y, axis=-1, keepdims=True)
                yc = y - mu
                var = jnp.mean(yc * yc, axis=-1, keepdims=True)
                yn = yc * lax.rsqrt(var + EPS) * gain_ref[:, head * LANES:(head + 1) * LANES]
                o_ref[rows, head * LANES:(head + 1) * LANES] = (_silu(r_ref[rows, gs]) * yn).astype(BF16)


def _retention_tables():
    half = RET_KEY_DIM // 2
    pos = jnp.arange(SEQ, dtype=F32)
    inv = ROPE_BASE ** (-jnp.arange(half, dtype=F32) / half)
    ang = pos[:, None] * inv[None, :]
    cos, sin = jnp.cos(ang), jnp.sin(ang)
    cos_full = jnp.tile(jnp.concatenate([cos, cos], axis=-1), (1, RET_HEADS))
    sin_signed = jnp.tile(jnp.concatenate([-sin, sin], axis=-1), (1, RET_HEADS))
    log_g = jnp.log(1.0 - 2.0 ** (-5.0 - jnp.arange(RET_HEADS, dtype=F32)))
    idx = jnp.arange(RET_CHUNK, dtype=F32)
    diff = idx[:, None] - idx[None, :]
    dmask = jnp.where(diff >= 0, jnp.exp(jnp.maximum(diff, 0.0)[None] * log_g[:, None, None]), 0.0)
    q_decay = jnp.exp((idx + 1.0)[None, :] * log_g[:, None])[..., None]
    k_decay = jnp.exp((RET_CHUNK - 1.0 - idx)[None, :] * log_g[:, None])[..., None]
    chunk_decay = jnp.exp(RET_CHUNK * log_g)[:, None, None]
    full = (RET_HEADS, RET_CHUNK, LANES)
    return (cos_full, sin_signed, dmask, jnp.broadcast_to(q_decay, full),
            jnp.broadcast_to(k_decay, full), jnp.broadcast_to(chunk_decay, full))


def _retention(ret_in, ret_gain):
    tokens = ret_in.shape[0]
    batch = tokens // SEQ
    per_seq = SEQ // RET_ROWS
    cos, sin, dmask, qdec, kdec, cdec = _retention_tables()
    tab = pl.BlockSpec((RET_ROWS, RET_QK_WIDTH), lambda b, j: (j, 0))
    const3 = pl.BlockSpec((RET_HEADS, RET_CHUNK, LANES), lambda b, j: (0, 0, 0))
    return pl.pallas_call(
        _retention_kernel,
        out_shape=jax.ShapeDtypeStruct((tokens, RET_WIDTH), BF16),
        grid=(batch, per_seq),
        in_specs=[
            pl.BlockSpec((RET_ROWS, RET_IN_WIDTH), lambda b, j: (b * per_seq + j, 0)),
            tab, tab, const3, const3, const3, const3,
            pl.BlockSpec((1, RET_WIDTH), lambda b, j: (0, 0)),
        ],
        out_specs=pl.BlockSpec((RET_ROWS, RET_WIDTH), lambda b, j: (b * per_seq + j, 0)),
        scratch_shapes=[pltpu.VMEM((RET_HEADS, LANES, RET_VALUE_DIM), F32)],
        compiler_params=_cparams(("arbitrary", "arbitrary")),
        name="retention",
    )(ret_in, cos, sin, dmask, qdec, kdec, cdec, ret_gain.reshape(1, RET_WIDTH))


def _out_proj_kernel(attn_ref, ret_ref, x_ref, mod_ref, w_ref, out_ref):
    mix = (_dot(attn_ref[...], w_ref[:ATTN_WIDTH, :].astype(BF16))
           + _dot(ret_ref[...], w_ref[ATTN_WIDTH:, :].astype(BF16)))
    out_ref[...] = x_ref[...] + mod_ref[2:3, :] * mix


def _out_proj(attn, ret, x, mod, w_out):
    tokens = x.shape[0]
    tm = TM_FFN
    per_seq = SEQ // tm
    row = lambda i: (i, 0)
    return pl.pallas_call(
        _out_proj_kernel,
        out_shape=jax.ShapeDtypeStruct((tokens, D_MODEL), F32),
        grid=(tokens // tm,),
        in_specs=[pl.BlockSpec((tm, ATTN_WIDTH), row),
                  pl.BlockSpec((tm, RET_WIDTH), row),
                  pl.BlockSpec((tm, D_MODEL), row),
                  pl.BlockSpec((None, 6, D_MODEL), lambda i: (i // per_seq, 0, 0)),
                  pl.BlockSpec((D_MODEL, D_MODEL), lambda i: (0, 0))],
        out_specs=pl.BlockSpec((tm, D_MODEL), row),
        compiler_params=_cparams(("arbitrary",)),
        name="out_projection",
    )(attn, ret, x, mod, w_out)


def _swiglu_chunks(h, w1_ref, w3_ref, w2_ref):
    d_ff = w1_ref.shape[-1]
    total = None
    for c0 in range(0, d_ff, MXU_COLS):
        c1 = min(c0 + MXU_COLS, d_ff)
        z = (_silu(_dot(h, w1_ref[:, c0:c1].astype(BF16))) * _dot(h, w3_ref[:, c0:c1].astype(BF16))).astype(BF16)
        part = _dot(z, w2_ref[c0:c1, :].astype(BF16))
        total = part if total is None else total + part
    return total


def _ffn_kernel(x_ref, mod_ref, gain_ref, w1_ref, w3_ref, w2_ref, out_ref):
    mod = mod_ref[...]
    x = x_ref[...]
    h = _modulated_norm(x, gain_ref[...], mod[4:5], mod[3:4]).astype(BF16)
    out_ref[...] = x + mod[5:6, :] * _swiglu_chunks(h, w1_ref, w3_ref, w2_ref)


def _dense_ffn(x, mod, gain, w1, w3, w2):
    tokens = x.shape[0]
    d_ff = w1.shape[1]
    tm = TM_FFN
    per_seq = SEQ // tm
    resident = pl.Buffered(1)
    return pl.pallas_call(
        _ffn_kernel,
        out_shape=jax.ShapeDtypeStruct((tokens, D_MODEL), F32),
        grid=(tokens // tm,),
        in_specs=[
            pl.BlockSpec((tm, D_MODEL), lambda i: (i, 0)),
            pl.BlockSpec((None, 6, D_MODEL), lambda i: (i // per_seq, 0, 0)),
            pl.BlockSpec((1, D_MODEL), lambda i: (0, 0)),
            pl.BlockSpec((D_MODEL, d_ff), lambda i: (0, 0), pipeline_mode=resident),
            pl.BlockSpec((D_MODEL, d_ff), lambda i: (0, 0), pipeline_mode=resident),
            pl.BlockSpec((d_ff, D_MODEL), lambda i: (0, 0), pipeline_mode=resident),
        ],
        out_specs=pl.BlockSpec((tm, D_MODEL), lambda i: (i, 0)),
        compiler_params=_cparams(("arbitrary",)),
        name="dense_swiglu",
    )(x, mod, gain.reshape(1, D_MODEL), w1, w3, w2)


def _router_kernel(x_ref, mod_ref, gain_ref, wr_ref, h_ref, gates_ref, pos_ref, post_ref, start_ref,
                   cnt_ref, carry_ref, *, tiles_per_seq):
    i = pl.program_id(0)

    @pl.when(i % tiles_per_seq == 0)
    def _():
        carry_ref[...] = jnp.zeros_like(carry_ref)

    mod = mod_ref[...]
    h = _modulated_norm(x_ref[...], gain_ref[...], mod[4:5], mod[3:4]).astype(BF16)
    h_ref[...] = h
    tm = h.shape[0]
    lane = lax.broadcasted_iota(jnp.int32, (tm, LANES), 1).astype(F32)
    logits = jnp.where(lane < N_EXPERTS, _dot(h, wr_ref[...]), -jnp.inf)
    m1 = jnp.max(logits, axis=-1, keepdims=True)
    i1 = jnp.min(jnp.where(logits == m1, lane, float(LANES)), axis=-1, keepdims=True)
    rest = jnp.where(lane == i1, -jnp.inf, logits)
    m2 = jnp.max(rest, axis=-1, keepdims=True)
    i2 = jnp.min(jnp.where(rest == m2, lane, float(LANES)), axis=-1, keepdims=True)
    e2 = jnp.exp(m2 - m1)
    g1 = 1.0 / (1.0 + e2)
    g2 = e2 / (1.0 + e2)
    gates_ref[...] = jnp.where(lane == i1, g1, 0.0) + jnp.where(lane == i2, g2, 0.0)
    chosen = (lane == i1) | (lane == i2)
    onehot = jnp.where(chosen, 1.0, 0.0)
    r = lax.broadcasted_iota(jnp.int32, (tm, tm), 0)
    c = lax.broadcasted_iota(jnp.int32, (tm, tm), 1)
    tril = jnp.where(c <= r, 1.0, 0.0).astype(BF16)
    incl = _dot(tril, onehot.astype(BF16))
    carry = carry_ref[0:1, :]
    start_ref[...] = carry_ref[...]
    pos = jnp.where(chosen, incl - 1.0 + carry, -1.0)
    pos_ref[...] = pos
    post_ref[...] = pos.T[:N_EXPERTS, :]
    total = carry + incl[tm - 1:tm, :]
    carry_ref[...] = jnp.broadcast_to(total, carry_ref.shape)
    cnt_ref[...] = jnp.broadcast_to(total, cnt_ref.shape)


def _router(x, mod, gain, w_router):
    tokens = x.shape[0]
    batch = tokens // SEQ
    tm = TM_PROJ
    per_seq = SEQ // tm
    wr = jnp.zeros((D_MODEL, LANES), BF16).at[:, :N_EXPERTS].set(w_router.astype(BF16))
    row = lambda i: (i, 0)
    return pl.pallas_call(
        functools.partial(_router_kernel, tiles_per_seq=per_seq),
        out_shape=(
            jax.ShapeDtypeStruct((tokens, D_MODEL), BF16),
            jax.ShapeDtypeStruct((tokens, LANES), F32),
            jax.ShapeDtypeStruct((tokens, LANES), F32),
            jax.ShapeDtypeStruct((N_EXPERTS, tokens), F32),
            jax.ShapeDtypeStruct((tokens // tm, 8, LANES), F32),
            jax.ShapeDtypeStruct((batch, 8, LANES), F32),
        ),
        grid=(tokens // tm,),
        in_specs=[
            pl.BlockSpec((tm, D_MODEL), row),
            pl.BlockSpec((None, 6, D_MODEL), lambda i: (i // per_seq, 0, 0)),
            pl.BlockSpec((1, D_MODEL), lambda i: (0, 0)),
            pl.BlockSpec((D_MODEL, LANES), lambda i: (0, 0)),
        ],
        out_specs=(
            pl.BlockSpec((tm, D_MODEL), row),
            pl.BlockSpec((tm, LANES), row),
            pl.BlockSpec((tm, LANES), row),
            pl.BlockSpec((N_EXPERTS, tm), lambda i: (0, i)),
            pl.BlockSpec((None, 8, LANES), lambda i: (i, 0, 0)),
            pl.BlockSpec((None, 8, LANES), lambda i: (i // per_seq, 0, 0)),
        ),
        scratch_shapes=[pltpu.VMEM((8, LANES), F32)],
        compiler_params=_cparams(("arbitrary",)),
        name="expert_router",
    )(x, mod, gain.reshape(1, D_MODEL), wr)


TOK_BLOCKS = SEQ // TR_MOE
TAIL_TILES = (64, 128, TR_MOE)
SPILL_ROWS = 16
SMALL_WINDOW = 128
SEG_ALIGN = 16
GROUP_ROWS = 2048
BIG_CHUNK = 128
STAGE_ROWS = SEQ + TR_MOE
N_BOUNDS = TOK_BLOCKS + 1


def _moe_plan(start, cnt):
    batch = cnt.shape[0]
    counts = cnt[:, 0, :N_EXPERTS].astype(jnp.int32)
    bounds = jnp.concatenate([start[:, 0, :N_EXPERTS].reshape(batch, TOK_BLOCKS, N_EXPERTS).astype(jnp.int32),
                              counts[:, None, :]], axis=1)
    seg = (counts + (SEG_ALIGN - 1)) // SEG_ALIGN * SEG_ALIGN
    rows_e = jnp.sum(seg, axis=0)
    groups_e = (rows_e + (GROUP_ROWS - 1)) // GROUP_ROWS
    first_group = jnp.cumsum(groups_e) - groups_e
    seg_off = first_group[None, :] * GROUP_ROWS + jnp.cumsum(seg, axis=0) - seg
    n_groups = batch * SEQ * 2 // GROUP_ROWS + N_EXPERTS
    g = jnp.arange(n_groups)
    g_expert = jnp.minimum(jnp.sum(g[:, None] >= jnp.cumsum(groups_e)[None, :], axis=1), N_EXPERTS - 1)
    g_rows = jnp.clip(rows_e[g_expert] - (g - first_group[g_expert]) * GROUP_ROWS, 0, GROUP_ROWS)
    g_rows = jnp.where(g < jnp.sum(groups_e), g_rows, 0)
    return (bounds.reshape(-1), seg.reshape(-1), seg_off.reshape(-1),
            g_expert.astype(jnp.int32), g_rows.astype(jnp.int32), n_groups)


def _segment_copies(seg, make_copy, action):
    n_big = lax.shift_right_logical(seg, BIG_CHUNK.bit_length() - 1)

    def big(i, carry):
        getattr(make_copy(pl.multiple_of(i * BIG_CHUNK, BIG_CHUNK), BIG_CHUNK), action)()
        return carry
    lax.fori_loop(0, n_big, big, 0)
    rest0 = n_big * BIG_CHUNK
    n_small = lax.shift_right_logical(seg - rest0, SEG_ALIGN.bit_length() - 1)

    def small(i, carry):
        getattr(make_copy(pl.multiple_of(rest0 + i * SEG_ALIGN, SEG_ALIGN), SEG_ALIGN), action)()
        return carry
    lax.fori_loop(0, n_small, small, 0)


def _block_windows(tbl_ref, b, e, align):
    base = b * N_BOUNDS * N_EXPERTS + e
    bounds = [tbl_ref[base + tb * N_EXPERTS] for tb in range(N_BOUNDS)]
    shift = align.bit_length() - 1
    win = [pl.multiple_of(lax.shift_left(lax.shift_right_logical(s, shift), shift), align) for s in bounds[:-1]]
    all_small = functools.reduce(jnp.logical_and,
                                 [bounds[tb + 1] - win[tb] <= SMALL_WINDOW for tb in range(TOK_BLOCKS)])
    return bounds, win, all_small


def _dispatch_kernel(tbl_ref, seg_ref, off_ref, h_ref, post_ref, rows_hbm, acc_ref, stage_ref, sem):
    b = pl.program_id(0)
    tr = TR_MOE

    def copies(e, action):
        slot = e % 2
        off = off_ref[b * N_EXPERTS + e]

        def make_copy(r0, rows):
            return pltpu.make_async_copy(stage_ref.at[slot, pl.ds(r0, rows), :],
                                         rows_hbm.at[pl.ds(pl.multiple_of(off + r0, SEG_ALIGN), rows), :],
                                         sem.at[slot])
        _segment_copies(seg_ref[b * N_EXPERTS + e], make_copy, action)

    for e in range(N_EXPERTS):
        bounds, win, all_small = _block_windows(tbl_ref, b, e, SUBLANES)
        n_tiles = lax.shift_right_logical(bounds[-1] + (tr - 1), tr.bit_length() - 1)

        def clear(r, carry):
            acc_ref[pl.ds(pl.multiple_of(r * tr, tr), tr), :] = jnp.zeros((tr, D_MODEL), F32)
            return carry
        lax.fori_loop(0, n_tiles + 2, clear, 0)

        def gather(height):
            slot_id = lax.broadcasted_iota(jnp.int32, (height, tr), 0).astype(F32)
            for tb in range(TOK_BLOCKS):
                ts = slice(tb * tr, (tb + 1) * tr)
                local = post_ref[e:e + 1, ts] - win[tb].astype(F32)
                sel = jnp.where(local == slot_id, 1.0, 0.0).astype(BF16)
                acc_ref[pl.ds(win[tb], height), :] += _dot(sel, h_ref[ts, :])

        @pl.when(all_small)
        def _():
            gather(SMALL_WINDOW)

        @pl.when(jnp.logical_not(all_small))
        def _():
            gather(tr + SPILL_ROWS)

        if e >= 2:
            copies(e - 2, "wait")

        def to_stage(r, carry):
            rows = pl.ds(pl.multiple_of(r * tr, tr), tr)
            stage_ref[e % 2, rows, :] = acc_ref[rows, :].astype(BF16)
            return carry
        lax.fori_loop(0, n_tiles, to_stage, 0)
        copies(e, "start")
    copies(N_EXPERTS - 2, "wait")
    copies(N_EXPERTS - 1, "wait")


def _dispatch(h, post, bounds, seg, seg_off, total_rows):
    tokens = h.shape[0]
    grid_spec = pltpu.PrefetchScalarGridSpec(
        num_scalar_prefetch=3,
        grid=(tokens // SEQ,),
        in_specs=[pl.BlockSpec((SEQ, D_MODEL), lambda b, *_: (b, 0)),
                  pl.BlockSpec((N_EXPERTS, SEQ), lambda b, *_: (0, b))],
        out_specs=pl.BlockSpec(memory_space=pl.ANY),
        scratch_shapes=[pltpu.VMEM((SEQ + 3 * TR_MOE, D_MODEL), F32),
                        pltpu.VMEM((2, STAGE_ROWS, D_MODEL), BF16),
                        pltpu.SemaphoreType.DMA((2,))],
    )
    return pl.pallas_call(
        _dispatch_kernel,
        out_shape=jax.ShapeDtypeStruct((total_rows, D_MODEL), BF16),
        grid_spec=grid_spec,
        compiler_params=_cparams(("arbitrary",)),
        name="expert_dispatch",
    )(bounds, seg, seg_off, h, post)


def _experts_kernel(ge_ref, rows_ref, hs_ref, w1_ref, w3_ref, w2_ref, ys_ref, acc_ref, wb1_ref, wb3_ref, wb2_ref):
    g, f = pl.program_id(0), pl.program_id(1)
    last_f = pl.num_programs(1) - 1
    tr = TR_MOE
    rows = rows_ref[g]
    n_full = lax.shift_right_logical(rows, tr.bit_length() - 1)
    rest = rows - n_full * tr
    n_tiles = lax.shift_right_logical(rows + (tr - 1), tr.bit_length() - 1)
    tail0 = pl.multiple_of(n_full * tr, tr)

    @pl.when(rows > 0)
    def _():
        wb1_ref[...] = w1_ref[...].astype(BF16)
        wb3_ref[...] = w3_ref[...].astype(BF16)
        wb2_ref[...] = w2_ref[...].astype(BF16)

    @pl.when(f == 0)
    def _():
        def clear(r, carry):
            acc_ref[pl.ds(pl.multiple_of(r * tr, tr), tr), :] = jnp.zeros((tr, D_MODEL), F32)
            return carry
        lax.fori_loop(0, n_tiles, clear, 0)

    def swiglu(hr, r0, m):
        z = (_silu(_dot(hr, wb1_ref[...])) * _dot(hr, wb3_ref[...])).astype(BF16)
        acc_ref[pl.ds(r0, m), :] += _dot(z, wb2_ref[...])

    def full_tile(r, carry):
        r0 = pl.multiple_of(r * tr, tr)
        swiglu(hs_ref[pl.ds(r0, tr), :], r0, tr)
        return carry
    lax.fori_loop(0, n_full, full_tile, 0)
    lo = 0
    for m in TAIL_TILES:
        @pl.when((rest > lo) & (rest <= m))
        def _():
            valid = lax.broadcasted_iota(jnp.int32, (m, D_MODEL), 0) < rest
            hr = hs_ref[pl.ds(tail0, m), :]
            swiglu(jnp.where(valid, hr, jnp.zeros_like(hr)), tail0, m)
        lo = m

    @pl.when(f == last_f)
    def _():
        def store(r, carry):
            rr = pl.ds(pl.multiple_of(r * tr, tr), tr)
            ys_ref[rr, :] = acc_ref[rr, :].astype(BF16)
            return carry
        lax.fori_loop(0, n_tiles, store, 0)


def _experts(hs, g_expert, g_rows, n_groups, w1, w3, w2):
    d_ff = w1.shape[2]
    tf = TF_MOE
    grid_spec = pltpu.PrefetchScalarGridSpec(
        num_scalar_prefetch=2,
        grid=(n_groups, d_ff // tf),
        in_specs=[
            pl.BlockSpec((GROUP_ROWS, D_MODEL), lambda g, f, ge, gr: (g, 0)),
            pl.BlockSpec((None, D_MODEL, tf), lambda g, f, ge, gr: (ge[g], 0, f)),
            pl.BlockSpec((None, D_MODEL, tf), lambda g, f, ge, gr: (ge[g], 0, f)),
            pl.BlockSpec((None, tf, D_MODEL), lambda g, f, ge, gr: (ge[g], f, 0)),
        ],
        out_specs=pl.BlockSpec((GROUP_ROWS, D_MODEL), lambda g, f, ge, gr: (g, 0)),
        scratch_shapes=[pltpu.VMEM((GROUP_ROWS, D_MODEL), F32),
                        pltpu.VMEM((D_MODEL, tf), BF16), pltpu.VMEM((D_MODEL, tf), BF16),
                        pltpu.VMEM((tf, D_MODEL), BF16)],
    )
    return pl.pallas_call(
        _experts_kernel,
        out_shape=jax.ShapeDtypeStruct(hs.shape, BF16),
        grid_spec=grid_spec,
        compiler_params=_cparams(("arbitrary", "arbitrary")),
        name="expert_swiglu",
    )(g_expert, g_rows, hs, w1, w3, w2)


def _combine_kernel(tbl_ref, seg_ref, off_ref, rows_hbm, pos_ref, gates_ref, x_ref, mod_ref, out_ref, buf_ref, sem):
    b = pl.program_id(0)
    tr = TR_MOE

    @pl.when(b == 0)
    def _():
        buf_ref[...] = jnp.zeros_like(buf_ref)

    def copies(e, action):
        slot = e % 2
        off = off_ref[b * N_EXPERTS + e]

        def make_copy(r0, rows):
            return pltpu.make_async_copy(rows_hbm.at[pl.ds(pl.multiple_of(off + r0, SEG_ALIGN), rows), :],
                                         buf_ref.at[slot, pl.ds(r0, rows), :], sem.at[slot])
        _segment_copies(seg_ref[b * N_EXPERTS + e], make_copy, action)

    out_ref[...] = x_ref[...]
    lane = lax.broadcasted_iota(jnp.int32, (tr, LANES), 1)
    layer_gate = mod_ref[5:6, :]
    copies(0, "start")
    for e in range(N_EXPERTS):
        if e + 1 < N_EXPERTS:
            copies(e + 1, "start")
        copies(e, "wait")
        bounds, win, all_small = _block_windows(tbl_ref, b, e, SEG_ALIGN)

        def scatter(tb, first, height):
            ts = slice(tb * tr, (tb + 1) * tr)
            pos_col = jnp.sum(jnp.where(lane == e, pos_ref[ts, :], 0.0), axis=-1, keepdims=True)
            gate_col = jnp.sum(jnp.where(lane == e, gates_ref[ts, :], 0.0), axis=-1, keepdims=True)
            slot_id = lax.broadcasted_iota(jnp.int32, (tr, height), 1).astype(F32) + float(first)
            sel = jnp.where(pos_col - win[tb].astype(F32) == slot_id, 1.0, 0.0).astype(BF16)
            rows = buf_ref[e % 2, pl.ds(win[tb] + first, height), :]
            out_ref[ts, :] += layer_gate * (gate_col * _dot(sel, rows))

        @pl.when(all_small)
        def _():
            for tb in range(TOK_BLOCKS):
                scatter(tb, 0, SMALL_WINDOW)

        @pl.when(jnp.logical_not(all_small))
        def _():
            for tb in range(TOK_BLOCKS):
                scatter(tb, 0, tr)
            for tb in range(TOK_BLOCKS):
                @pl.when(bounds[tb + 1] - win[tb] > tr)
                def _():
                    scatter(tb, tr, SPILL_ROWS)


def _combine(ys, pos, gates, x, mod, bounds, seg, seg_off):
    tokens = x.shape[0]
    grid_spec = pltpu.PrefetchScalarGridSpec(
        num_scalar_prefetch=3,
        grid=(tokens // SEQ,),
        in_specs=[pl.BlockSpec(memory_space=pl.ANY),
                  pl.BlockSpec((SEQ, LANES), lambda b, *_: (b, 0)),
                  pl.BlockSpec((SEQ, LANES), lambda b, *_: (b, 0)),
                  pl.BlockSpec((SEQ, D_MODEL), lambda b, *_: (b, 0)),
                  pl.BlockSpec((None, 6, D_MODEL), lambda b, *_: (b, 0, 0))],
        out_specs=pl.BlockSpec((SEQ, D_MODEL), lambda b, *_: (b, 0)),
        scratch_shapes=[pltpu.VMEM((2, STAGE_ROWS + SPILL_ROWS, D_MODEL), BF16),
                        pltpu.SemaphoreType.DMA((2,))],
    )
    return pl.pallas_call(
        _combine_kernel,
        out_shape=jax.ShapeDtypeStruct((tokens, D_MODEL), F32),
        grid_spec=grid_spec,
        compiler_params=_cparams(("arbitrary",)),
        name="expert_combine",
    )(bounds, seg, seg_off, ys, pos, gates, x, mod)


def _token_mixer(x, mod, gain, w_in, q_gain, k_gain, ret_gain, w_out, bias_masks):
    *qkv, ret_in = _in_proj(x, mod, gain, w_in, q_gain, k_gain)
    attn = _dilated_attention(qkv, bias_masks)
    ret = _retention(ret_in, ret_gain)
    return _out_proj(attn, ret, x, mod, w_out)


def _moe_ffn(x, mod, gain, w_router, w1, w3, w2):
    h, gates, pos, post, start, cnt = _router(x, mod, gain, w_router)
    bounds, seg, seg_off, g_expert, g_rows, n_groups = _moe_plan(start, cnt)
    hs = _dispatch(h, post, bounds, seg, seg_off, n_groups * GROUP_ROWS)
    ys = _experts(hs, g_expert, g_rows, n_groups, w1, w3, w2)
    return _combine(ys, pos, gates, x, mod, bounds, seg, seg_off)


def kernel(x, c, rel_bias_table, norm_mix, norm_ffn, w_mod, b_mod, w_in, q_gain, k_gain, ret_gain, w_out,
           ffn_w1, ffn_w3, ffn_w2, moe_router, moe_w1, moe_w3, moe_w2):
    batch, seq, d_model = x.shape
    assert (seq, d_model) == (SEQ, D_MODEL)
    depth = w_mod.shape[0]
    mods = _modulation(c, w_mod, b_mod).reshape(depth, batch, 6, D_MODEL)
    bias_masks = _bias_masks(rel_bias_table)
    xt = x.reshape(batch * seq, d_model)
    for layer in range(depth):
        mod = mods[layer]
        xt = _token_mixer(xt, mod, norm_mix[layer], w_in[layer], q_gain[layer], k_gain[layer],
                          ret_gain[layer], w_out[layer], bias_masks)
        i = layer // 2
        if layer % 2 == 0:
            xt = _dense_ffn(xt, mod, norm_ffn[layer], ffn_w1[i], ffn_w3[i], ffn_w2[i])
        else:
            xt = _moe_ffn(xt, mod, norm_ffn[layer], moe_router[i], moe_w1[i], moe_w3[i], moe_w2[i])
    return xt.reshape(batch, seq, d_model)
```

```python
import functools
import math

import jax
import jax.numpy as jnp
import numpy as np
from jax import lax
from jax.experimental import pallas as pl
from jax.experimental.pallas import tpu as pltpu

D_MODEL = 1024
SEQ = 2048
ATTN_HEADS = 8
ATTN_HEAD_DIM = 64
ATTN_WIDTH = ATTN_HEADS * ATTN_HEAD_DIM
DILATED_PATTERNS = ((128, 1), (512, 4), (2048, 16))
BLOCK = 128
NUM_BUCKETS = 32
MAX_DISTANCE = 2048
RET_HEADS = 4
RET_KEY_DIM = 64
RET_VALUE_DIM = 128
RET_WIDTH = RET_HEADS * RET_VALUE_DIM
RET_QK_WIDTH = RET_HEADS * RET_KEY_DIM
RET_CHUNK = 128
ROPE_BASE = 10000.0
IN_WIDTH = 3 * ATTN_WIDTH + 2 * RET_QK_WIDTH + 2 * RET_WIDTH
RET_IN_WIDTH = IN_WIDTH - 3 * ATTN_WIDTH
N_EXPERTS = 8
EPS = 1e-6
NEG_INF = -1e30

LANES = 128
VMEM_LIMIT = 60 * 1024 * 1024

BF16 = jnp.bfloat16
F32 = jnp.float32

TM_PROJ = 256
TM_FFN = 512
MXU_COLS = 256
TR_MOE = 256
TF_MOE = 512
assert TM_PROJ == TR_MOE
RET_ROWS = 512
SUBLANES = 8


def _cparams(sem):
    return pltpu.CompilerParams(dimension_semantics=sem, vmem_limit_bytes=VMEM_LIMIT)


def _dot(a, b):
    return jnp.dot(a, b, preferred_element_type=F32)


def _dot_nt(a, b):
    return lax.dot_general(a, b, (((1,), (1,)), ((), ())), preferred_element_type=F32)


def _dot_tn(a, b):
    return lax.dot_general(a, b, (((0,), (0,)), ((), ())), preferred_element_type=F32)


def _split_bf16(v):
    hi = v.astype(BF16)
    lo = (v - hi.astype(F32)).astype(BF16)
    return hi, lo


def _silu(v):
    return v * (1.0 / (1.0 + jnp.exp(-v)))


def _modulated_norm(x, gain, scale, shift):
    ms = jnp.mean(x * x, axis=-1, keepdims=True)
    y = x * lax.rsqrt(ms + EPS) * gain
    return y * (1.0 + scale) + shift


def _mod_kernel(c_ref, w_ref, b_ref, o_ref):
    ca = _silu(c_ref[...]).astype(BF16)
    o_ref[...] = _dot(ca, w_ref[...].astype(BF16)) + b_ref[...]


def _modulation(c, w_mod, b_mod):
    depth, _, width = w_mod.shape
    batch = c.shape[0]
    tn = 1536
    return pl.pallas_call(
        _mod_kernel,
        out_shape=jax.ShapeDtypeStruct((depth, batch, width), F32),
        grid=(depth, width // tn),
        in_specs=[
            pl.BlockSpec((batch, D_MODEL), lambda l, n: (0, 0)),
            pl.BlockSpec((None, D_MODEL, tn), lambda l, n: (l, 0, n)),
            pl.BlockSpec((None, 1, tn), lambda l, n: (l, 0, n)),
        ],
        out_specs=pl.BlockSpec((None, batch, tn), lambda l, n: (l, 0, n)),
        compiler_params=_cparams(("arbitrary", "arbitrary")),
        name="adaln_modulation",
    )(c, w_mod, b_mod.reshape(depth, 1, width))


def _bias_kernel(table_ref, bucket_ref, o_ref):
    h = pl.program_id(1)
    bucket = bucket_ref[...]
    acc = jnp.full(bucket.shape, NEG_INF, F32)
    for b in range(NUM_BUCKETS):
        acc = jnp.where(bucket == b, table_ref[b, h], acc)
    o_ref[...] = acc


BIAS_FULL = {1: 0, 4: 2}
BIAS_FIRST = {1: 1, 4: 3, 16: 4}
N_BIAS_SETS = 5


def _bias_masks(rel_bias_table):
    i = jnp.arange(BLOCK)[:, None]
    j = jnp.arange(2 * BLOCK)[None, :]
    max_exact = NUM_BUCKETS // 2

    def bucket_of(rel, dilation, w_sub, exists):
        n = jnp.maximum(rel * dilation, 0)
        nf = jnp.maximum(n.astype(F32), float(max_exact))
        large = max_exact + (jnp.log(nf / max_exact) / math.log(MAX_DISTANCE / max_exact)
                             * (NUM_BUCKETS - max_exact)).astype(jnp.int32)
        large = jnp.minimum(large, NUM_BUCKETS - 1)
        bucket = jnp.where(n < max_exact, n, large)
        allowed = (rel >= 0) & (rel <= w_sub) & exists
        return jnp.where(allowed, bucket, -1)

    sets = [None] * N_BIAS_SETS
    for window, dilation in DILATED_PATTERNS:
        w_sub = window // dilation
        if dilation in BIAS_FULL:
            sets[BIAS_FULL[dilation]] = bucket_of(i - j + BLOCK, dilation, w_sub, j >= 0)
        sets[BIAS_FIRST[dilation]] = bucket_of(i - j, dilation, w_sub, j < BLOCK)
    buckets = jnp.stack(sets).astype(jnp.int32)
    return pl.pallas_call(
        _bias_kernel,
        out_shape=jax.ShapeDtypeStruct((N_BIAS_SETS, ATTN_HEADS // 2, 2 * BLOCK, 2 * BLOCK), F32),
        grid=(N_BIAS_SETS, ATTN_HEADS),
        in_specs=[
            pl.BlockSpec(memory_space=pltpu.SMEM),
            pl.BlockSpec((None, BLOCK, 2 * BLOCK), lambda p, h: (p, 0, 0)),
        ],
        out_specs=pl.BlockSpec((None, None, BLOCK, 2 * BLOCK), lambda p, h: (p, h // 2, h % 2, 0)),
        compiler_params=_cparams(("arbitrary", "arbitrary")),
        name="relative_bias_masks",
    )(rel_bias_table, buckets)


HALF = ATTN_WIDTH // 2


def _in_proj_kernel(x_ref, mod_ref, gain_ref, w_ref, qg_ref, kg_ref, grp_ref,
                    q1_ref, k1_ref, v1_ref, q4_ref, k4_ref, v4_ref, q16_ref, k16_ref, v16_ref, r_ref,
                    perm_ref):
    mod = mod_ref[...]
    h = _modulated_norm(x_ref[...], gain_ref[...], mod[1:2], mod[0:1]).astype(BF16)
    proj = _dot(h, w_ref[...].astype(BF16))
    grp = grp_ref[...]
    tm = proj.shape[0]

    def head_norm(t, gain):
        hi, lo = _split_bf16(t * t)
        ss = _dot(hi, grp) + _dot(lo, grp)
        return t * lax.rsqrt(ss * (1.0 / ATTN_HEAD_DIM) + EPS) * gain

    def emit(t, o1_ref, o4_ref, o16_ref):
        o1_ref[...] = t.astype(BF16)
        for j in range(ATTN_WIDTH // LANES):
            perm_ref[j] = t[:, j * LANES:(j + 1) * LANES]
        for dil, o_ref in ((4, o4_ref), (16, o16_ref)):
            for hf in range(2):
                for r in range(dil):
                    for jj in range(HALF // LANES):
                        c0 = (hf * dil + r) * HALF + jj * LANES
                        o_ref[:, c0:c0 + LANES] = perm_ref[hf * (HALF // LANES) + jj,
                                                           pl.ds(r, tm // dil, stride=dil), :].astype(BF16)

    emit(head_norm(proj[:, :ATTN_WIDTH], qg_ref[...]) * (ATTN_HEAD_DIM ** -0.5), q1_ref, q4_ref, q16_ref)
    emit(head_norm(proj[:, ATTN_WIDTH:2 * ATTN_WIDTH], kg_ref[...]), k1_ref, k4_ref, k16_ref)
    emit(proj[:, 2 * ATTN_WIDTH:3 * ATTN_WIDTH], v1_ref, v4_ref, v16_ref)
    r_ref[...] = proj[:, 3 * ATTN_WIDTH:]


def _in_proj(x, mod, gain, w_in, q_gain, k_gain):
    tokens = x.shape[0]
    tm = TM_PROJ
    per_seq = SEQ // tm
    grp = np.kron(np.eye(ATTN_HEADS), np.ones((ATTN_HEAD_DIM, ATTN_HEAD_DIM))).astype(np.float32)
    row = lambda i: (i, 0)
    const = lambda i: (0, 0)
    layouts = []
    for dil in (1, 4, 16):
        shape = jax.ShapeDtypeStruct((tokens // dil, dil * ATTN_WIDTH), BF16)
        spec = pl.BlockSpec((tm // dil, dil * ATTN_WIDTH), row)
        layouts.append(((shape,) * 3, (spec,) * 3))
    out_shape = sum((s for s, _ in layouts), ()) + (jax.ShapeDtypeStruct((tokens, RET_IN_WIDTH), F32),)
    out_specs = sum((s for _, s in layouts), ()) + (pl.BlockSpec((tm, RET_IN_WIDTH), row),)
    return pl.pallas_call(
        _in_proj_kernel,
        out_shape=out_shape,
        grid=(tokens // tm,),
        in_specs=[
            pl.BlockSpec((tm, D_MODEL), row),
            pl.BlockSpec((None, 6, D_MODEL), lambda i: (i // per_seq, 0, 0)),
            pl.BlockSpec((1, D_MODEL), const),
            pl.BlockSpec((D_MODEL, IN_WIDTH), const, pipeline_mode=pl.Buffered(1)),
            pl.BlockSpec((1, ATTN_WIDTH), const),
            pl.BlockSpec((1, ATTN_WIDTH), const),
            pl.BlockSpec((ATTN_WIDTH, ATTN_WIDTH), const),
        ],
        out_specs=out_specs,
        scratch_shapes=[pltpu.VMEM((ATTN_WIDTH // LANES, tm, LANES), F32)],
        compiler_params=_cparams(("arbitrary",)),
        name="in_projection",
    )(x, mod, gain.reshape(1, D_MODEL), w_in,
      jnp.tile(q_gain, ATTN_HEADS).reshape(1, ATTN_WIDTH),
      jnp.tile(k_gain, ATTN_HEADS).reshape(1, ATTN_WIDTH),
      jnp.asarray(grp, BF16))


PAIRS_PER_HALF = ATTN_HEADS // 4
GROUP = 4


def _pair_scores(qp, kp, vp, bias2, masks, low):
    q2 = jnp.concatenate([qp * masks[0], qp * masks[1]], axis=0)
    s = _dot_nt(q2, kp) + bias2
    m = jnp.max(s, axis=-1, keepdims=True)
    p = jnp.exp(s - m)
    den = jnp.sum(p, axis=-1, keepdims=True)
    pv = _dot(p.astype(BF16), vp)
    pick = lambda t: jnp.where(low, t[:BLOCK], t[BLOCK:])
    return pick(pv), pick(m), pick(den)


def _attn_kernel(q1_ref, k1_ref, v1_ref, q4_ref, k4_ref, v4_ref, q16_ref, k16_ref, v16_ref, bm_ref,
                 o_ref, acc_ref, max_ref, den_ref):
    lane = lax.broadcasted_iota(jnp.int32, (BLOCK, LANES), 1)
    low = lane < ATTN_HEAD_DIM
    masks = (jnp.where(low, 1.0, 0.0).astype(BF16), jnp.where(low, 0.0, 1.0).astype(BF16))

    def block(q_ref, k_ref, v_ref, c0, q0, w0, width, bias_set, rows, first):
        for p in range(PAIRS_PER_HALF):
            cs = slice(c0 + p * LANES, c0 + (p + 1) * LANES)
            acc, m, den = _pair_scores(q_ref[pl.ds(q0, BLOCK), cs], k_ref[pl.ds(w0, width), cs],
                                       v_ref[pl.ds(w0, width), cs], bm_ref[bias_set, p, :, 0:width],
                                       masks, low)
            if not first:
                m_old = max_ref[p, rows, :]
                m_new = jnp.maximum(m_old, m)
                a, b = jnp.exp(m_old - m_new), jnp.exp(m - m_new)
                den = den_ref[p, rows, :] * a + den * b
                acc = acc_ref[p, rows, :] * a + acc * b
                m = m_new
            max_ref[p, rows, :] = m
            den_ref[p, rows, :] = den
            acc_ref[p, rows, :] = acc

    def d1_group(g, carry):
        for u in range(GROUP):
            n = g * GROUP + u
            q0 = pl.multiple_of(n * BLOCK, BLOCK)
            w0 = pl.multiple_of(jnp.maximum(n - 1, 0) * BLOCK, BLOCK)
            bias_set = jnp.where(n == 0, BIAS_FIRST[1], BIAS_FULL[1])
            block(q1_ref, k1_ref, v1_ref, 0, q0, w0, 2 * BLOCK, bias_set, pl.ds(q0, BLOCK), True)
        return carry
    lax.fori_loop(0, SEQ // BLOCK // GROUP, d1_group, 0)

    for r in range(4):
        for n in range(SEQ // 4 // BLOCK):
            block(q4_ref, k4_ref, v4_ref, r * HALF, n * BLOCK, max(n - 1, 0) * BLOCK, 2 * BLOCK,
                  BIAS_FIRST[4] if n == 0 else BIAS_FULL[4],
                  pl.ds(r + 4 * BLOCK * n, BLOCK, stride=4), False)

    for r in range(16):
        block(q16_ref, k16_ref, v16_ref, r * HALF, 0, 0, BLOCK, BIAS_FIRST[16],
              pl.ds(r, BLOCK, stride=16), False)

    for n in range(SEQ // BLOCK):
        rows = slice(n * BLOCK, (n + 1) * BLOCK)
        for p in range(PAIRS_PER_HALF):
            o_ref[rows, p * LANES:(p + 1) * LANES] = (acc_ref[p, rows, :] / den_ref[p, rows, :]).astype(BF16)


def _dilated_attention(qkv, bias_masks):
    tokens = qkv[0].shape[0]
    batch = tokens // SEQ
    specs = []
    for dil in (1, 4, 16):
        specs += [pl.BlockSpec((SEQ // dil, dil * HALF), lambda b, hf: (b, hf))] * 3
    state = pltpu.VMEM((PAIRS_PER_HALF, SEQ, LANES), F32)
    return pl.pallas_call(
        _attn_kernel,
        out_shape=jax.ShapeDtypeStruct((tokens, ATTN_WIDTH), BF16),
        grid=(batch, 2),
        in_specs=specs + [pl.BlockSpec((N_BIAS_SETS, PAIRS_PER_HALF, 2 * BLOCK, 2 * BLOCK),
                                       lambda b, hf: (0, hf, 0, 0))],
        out_specs=pl.BlockSpec((SEQ, HALF), lambda b, hf: (b, hf)),
        scratch_shapes=[state, state, state],
        compiler_params=_cparams(("arbitrary", "arbitrary")),
        name="dilated_attention",
    )(*qkv, bias_masks)


def _retention_kernel(r_ref, cos_ref, sin_ref, dmask_ref, qdec_ref, kdec_ref, cdec_ref, gain_ref,
                      o_ref, state_ref):
    @pl.when(pl.program_id(1) == 0)
    def _():
        state_ref[...] = jnp.zeros_like(state_ref)

    lane = lax.broadcasted_iota(jnp.int32, (RET_CHUNK, LANES), 1)
    low = lane < RET_KEY_DIM
    first_half = (lane % RET_KEY_DIM) < (RET_KEY_DIM // 2)

    def rotate(t, cos, sin):
        partner = jnp.where(first_half, pltpu.roll(t, LANES - RET_KEY_DIM // 2, 1),
                            pltpu.roll(t, RET_KEY_DIM // 2, 1))
        return t * cos + partner * sin

    for c in range(RET_ROWS // RET_CHUNK):
        rows = slice(c * RET_CHUNK, (c + 1) * RET_CHUNK)
        for hp in range(RET_HEADS // 2):
            qs = slice(hp * LANES, (hp + 1) * LANES)
            ks = slice(RET_QK_WIDTH + hp * LANES, RET_QK_WIDTH + (hp + 1) * LANES)
            cos, sin = cos_ref[rows, qs], sin_ref[rows, qs]
            q_pair = rotate(r_ref[rows, qs], cos, sin) * (RET_KEY_DIM ** -0.5)
            k_pair = rotate(r_ref[rows, ks], cos, sin)
            for hh in range(2):
                head = 2 * hp + hh
                vs = slice(2 * RET_QK_WIDTH + head * LANES, 2 * RET_QK_WIDTH + (head + 1) * LANES)
                gs = slice(2 * RET_QK_WIDTH + RET_WIDTH + head * LANES,
                           2 * RET_QK_WIDTH + RET_WIDTH + (head + 1) * LANES)
                keep = low if hh == 0 else jnp.logical_not(low)
                qm = jnp.where(keep, q_pair, 0.0)
                vb = r_ref[rows, vs].astype(BF16)
                state = state_ref[head]
                inner = _dot_nt(qm.astype(BF16), k_pair.astype(BF16)) * dmask_ref[head]
                y = _dot(inner.astype(BF16), vb)
                y = y + _dot((qm * qdec_ref[head]).astype(BF16), state.astype(BF16))
                state_ref[head] = state * cdec_ref[head] + _dot_tn((k_pair * kdec_ref[head]).astype(BF16), vb)
                mu = jnp.mean(y, axis=-1, keepdims=True)
                yc = y - mu
                var = jnp.mean(yc * yc, axis=-1, keepdims=True)
                yn = yc * lax.rsqrt(var + EPS) * gain_ref[:, head * LANES:(head + 1) * LANES]
                o_ref[rows, head * LANES:(head + 1) * LANES] = (_silu(r_ref[rows, gs]) * yn).astype(BF16)


def _retention_tables():
    half = RET_KEY_DIM // 2
    pos = jnp.arange(SEQ, dtype=F32)
    inv = ROPE_BASE ** (-jnp.arange(half, dtype=F32) / half)
    ang = pos[:, None] * inv[None, :]
    cos, sin = jnp.cos(ang), jnp.sin(ang)
    cos_full = jnp.tile(jnp.concatenate([cos, cos], axis=-1), (1, RET_HEADS))
    sin_signed = jnp.tile(jnp.concatenate([-sin, sin], axis=-1), (1, RET_HEADS))
    log_g = jnp.log(1.0 - 2.0 ** (-5.0 - jnp.arange(RET_HEADS, dtype=F32)))
    idx = jnp.arange(RET_CHUNK, dtype=F32)
    diff = idx[:, None] - idx[None, :]
    dmask = jnp.where(diff >= 0, jnp.exp(jnp.maximum(diff, 0.0)[None] * log_g[:, None, None]), 0.0)
    q_decay = jnp.exp((idx + 1.0)[None, :] * log_g[:, None])[..., None]
    k_decay = jnp.exp((RET_CHUNK - 1.0 - idx)[None, :] * log_g[:, None])[..., None]
    chunk_decay = jnp.exp(RET_CHUNK * log_g)[:, None, None]
    full = (RET_HEADS, RET_CHUNK, LANES)
    return (cos_full, sin_signed, dmask, jnp.broadcast_to(q_decay, full),
            jnp.broadcast_to(k_decay, full), jnp.broadcast_to(chunk_decay, full))


def _retention(ret_in, ret_gain):
    tokens = ret_in.shape[0]
    batch = tokens // SEQ
    per_seq = SEQ // RET_ROWS
    cos, sin, dmask, qdec, kdec, cdec = _retention_tables()
    tab = pl.BlockSpec((RET_ROWS, RET_QK_WIDTH), lambda b, j: (j, 0))
    const3 = pl.BlockSpec((RET_HEADS, RET_CHUNK, LANES), lambda b, j: (0, 0, 0))
    return pl.pallas_call(
        _retention_kernel,
        out_shape=jax.ShapeDtypeStruct((tokens, RET_WIDTH), BF16),
        grid=(batch, per_seq),
        in_specs=[
            pl.BlockSpec((RET_ROWS, RET_IN_WIDTH), lambda b, j: (b * per_seq + j, 0)),
            tab, tab, const3, const3, const3, const3,
            pl.BlockSpec((1, RET_WIDTH), lambda b, j: (0, 0)),
        ],
        out_specs=pl.BlockSpec((RET_ROWS, RET_WIDTH), lambda b, j: (b * per_seq + j, 0)),
        scratch_shapes=[pltpu.VMEM((RET_HEADS, LANES, RET_VALUE_DIM), F32)],
        compiler_params=_cparams(("arbitrary", "arbitrary")),
        name="retention",
    )(ret_in, cos, sin, dmask, qdec, kdec, cdec, ret_gain.reshape(1, RET_WIDTH))


def _out_proj_kernel(attn_ref, ret_ref, x_ref, mod_ref, w_ref, out_ref):
    mix = (_dot(attn_ref[...], w_ref[:ATTN_WIDTH, :].astype(BF16))
           + _dot(ret_ref[...], w_ref[ATTN_WIDTH:, :].astype(BF16)))
    out_ref[...] = x_ref[...] + mod_ref[2:3, :] * mix


def _out_proj(attn, ret, x, mod, w_out):
    tokens = x.shape[0]
    tm = TM_FFN
    per_seq = SEQ // tm
    row = lambda i: (i, 0)
    return pl.pallas_call(
        _out_proj_kernel,
        out_shape=jax.ShapeDtypeStruct((tokens, D_MODEL), F32),
        grid=(tokens // tm,),
        in_specs=[pl.BlockSpec((tm, ATTN_WIDTH), row),
                  pl.BlockSpec((tm, RET_WIDTH), row),
                  pl.BlockSpec((tm, D_MODEL), row),
                  pl.BlockSpec((None, 6, D_MODEL), lambda i: (i // per_seq, 0, 0)),
                  pl.BlockSpec((D_MODEL, D_MODEL), lambda i: (0, 0))],
        out_specs=pl.BlockSpec((tm, D_MODEL), row),
        compiler_params=_cparams(("arbitrary",)),
        name="out_projection",
    )(attn, ret, x, mod, w_out)


def _swiglu_chunks(h, w1_ref, w3_ref, w2_ref):
    d_ff = w1_ref.shape[-1]
    total = None
    for c0 in range(0, d_ff, MXU_COLS):
        c1 = min(c0 + MXU_COLS, d_ff)
        z = (_silu(_dot(h, w1_ref[:, c0:c1].astype(BF16))) * _dot(h, w3_ref[:, c0:c1].astype(BF16))).astype(BF16)
        part = _dot(z, w2_ref[c0:c1, :].astype(BF16))
        total = part if total is None else total + part
    return total


def _ffn_kernel(x_ref, mod_ref, gain_ref, w1_ref, w3_ref, w2_ref, out_ref):
    mod = mod_ref[...]
    x = x_ref[...]
    h = _modulated_norm(x, gain_ref[...], mod[4:5], mod[3:4]).astype(BF16)
    out_ref[...] = x + mod[5:6, :] * _swiglu_chunks(h, w1_ref, w3_ref, w2_ref)


def _dense_ffn(x, mod, gain, w1, w3, w2):
    tokens = x.shape[0]
    d_ff = w1.shape[1]
    tm = TM_FFN
    per_seq = SEQ // tm
    resident = pl.Buffered(1)
    return pl.pallas_call(
        _ffn_kernel,
        out_shape=jax.ShapeDtypeStruct((tokens, D_MODEL), F32),
        grid=(tokens // tm,),
        in_specs=[
            pl.BlockSpec((tm, D_MODEL), lambda i: (i, 0)),
            pl.BlockSpec((None, 6, D_MODEL), lambda i: (i // per_seq, 0, 0)),
            pl.BlockSpec((1, D_MODEL), lambda i: (0, 0)),
            pl.BlockSpec((D_MODEL, d_ff), lambda i: (0, 0), pipeline_mode=resident),
            pl.BlockSpec((D_MODEL, d_ff), lambda i: (0, 0), pipeline_mode=resident),
            pl.BlockSpec((d_ff, D_MODEL), lambda i: (0, 0), pipeline_mode=resident),
        ],
        out_specs=pl.BlockSpec((tm, D_MODEL), lambda i: (i, 0)),
        compiler_params=_cparams(("arbitrary",)),
        name="dense_swiglu",
    )(x, mod, gain.reshape(1, D_MODEL), w1, w3, w2)


def _router_kernel(x_ref, mod_ref, gain_ref, wr_ref, h_ref, gates_ref, pos_ref, post_ref, start_ref,
                   cnt_ref, carry_ref, *, tiles_per_seq):
    i = pl.program_id(0)

    @pl.when(i % tiles_per_seq == 0)
    def _():
        carry_ref[...] = jnp.zeros_like(carry_ref)

    mod = mod_ref[...]
    h = _modulated_norm(x_ref[...], gain_ref[...], mod[4:5], mod[3:4]).astype(BF16)
    h_ref[...] = h
    tm = h.shape[0]
    lane = lax.broadcasted_iota(jnp.int32, (tm, LANES), 1).astype(F32)
    logits = jnp.where(lane < N_EXPERTS, _dot(h, wr_ref[...]), -jnp.inf)
    m1 = jnp.max(logits, axis=-1, keepdims=True)
    i1 = jnp.min(jnp.where(logits == m1, lane, float(LANES)), axis=-1, keepdims=True)
    rest = jnp.where(lane == i1, -jnp.inf, logits)
    m2 = jnp.max(rest, axis=-1, keepdims=True)
    i2 = jnp.min(jnp.where(rest == m2, lane, float(LANES)), axis=-1, keepdims=True)
    e2 = jnp.exp(m2 - m1)
    g1 = 1.0 / (1.0 + e2)
    g2 = e2 / (1.0 + e2)
    gates_ref[...] = jnp.where(lane == i1, g1, 0.0) + jnp.where(lane == i2, g2, 0.0)
    chosen = (lane == i1) | (lane == i2)
    onehot = jnp.where(chosen, 1.0, 0.0)
    r = lax.broadcasted_iota(jnp.int32, (tm, tm), 0)
    c = lax.broadcasted_iota(jnp.int32, (tm, tm), 1)
    tril = jnp.where(c <= r, 1.0, 0.0).astype(BF16)
    incl = _dot(tril, onehot.astype(BF16))
    carry = carry_ref[0:1, :]
    start_ref[...] = carry_ref[...]
    pos = jnp.where(chosen, incl - 1.0 + carry, -1.0)
    pos_ref[...] = pos
    post_ref[...] = pos.T[:N_EXPERTS, :]
    total = carry + incl[tm - 1:tm, :]
    carry_ref[...] = jnp.broadcast_to(total, carry_ref.shape)
    cnt_ref[...] = jnp.broadcast_to(total, cnt_ref.shape)


def _router(x, mod, gain, w_router):
    tokens = x.shape[0]
    batch = tokens // SEQ
    tm = TM_PROJ
    per_seq = SEQ // tm
    wr = jnp.zeros((D_MODEL, LANES), BF16).at[:, :N_EXPERTS].set(w_router.astype(BF16))
    row = lambda i: (i, 0)
    return pl.pallas_call(
        functools.partial(_router_kernel, tiles_per_seq=per_seq),
        out_shape=(
            jax.ShapeDtypeStruct((tokens, D_MODEL), BF16),
            jax.ShapeDtypeStruct((tokens, LANES), F32),
            jax.ShapeDtypeStruct((tokens, LANES), F32),
            jax.ShapeDtypeStruct((N_EXPERTS, tokens), F32),
            jax.ShapeDtypeStruct((tokens // tm, 8, LANES), F32),
            jax.ShapeDtypeStruct((batch, 8, LANES), F32),
        ),
        grid=(tokens // tm,),
        in_specs=[
            pl.BlockSpec((tm, D_MODEL), row),
            pl.BlockSpec((None, 6, D_MODEL), lambda i: (i // per_seq, 0, 0)),
            pl.BlockSpec((1, D_MODEL), lambda i: (0, 0)),
            pl.BlockSpec((D_MODEL, LANES), lambda i: (0, 0)),
        ],
        out_specs=(
            pl.BlockSpec((tm, D_MODEL), row),
            pl.BlockSpec((tm, LANES), row),
            pl.BlockSpec((tm, LANES), row),
            pl.BlockSpec((N_EXPERTS, tm), lambda i: (0, i)),
            pl.BlockSpec((None, 8, LANES), lambda i: (i, 0, 0)),
            pl.BlockSpec((None, 8, LANES), lambda i: (i // per_seq, 0, 0)),
        ),
        scratch_shapes=[pltpu.VMEM((8, LANES), F32)],
        compiler_params=_cparams(("arbitrary",)),
        name="expert_router",
    )(x, mod, gain.reshape(1, D_MODEL), wr)


TOK_BLOCKS = SEQ // TR_MOE
TAIL_TILES = (64, 128, TR_MOE)
SPILL_ROWS = 16
SMALL_WINDOW = 128
SEG_ALIGN = 16
GROUP_ROWS = 3072
PACKED_ROWS = 2 * SEQ + N_EXPERTS * SEG_ALIGN + 2 * TR_MOE
BIG_CHUNK = 128
STAGE_ROWS = SEQ + TR_MOE
N_BOUNDS = TOK_BLOCKS + 1


def _moe_plan(start, cnt):
    batch = cnt.shape[0]
    counts = cnt[:, 0, :N_EXPERTS].astype(jnp.int32)
    bounds = jnp.concatenate([start[:, 0, :N_EXPERTS].reshape(batch, TOK_BLOCKS, N_EXPERTS).astype(jnp.int32),
                              counts[:, None, :]], axis=1)
    seg = (counts + (SEG_ALIGN - 1)) // SEG_ALIGN * SEG_ALIGN
    rows_e = jnp.sum(seg, axis=0)
    groups_e = (rows_e + (GROUP_ROWS - 1)) // GROUP_ROWS
    first_group = jnp.cumsum(groups_e) - groups_e
    seg_off = first_group[None, :] * GROUP_ROWS + jnp.cumsum(seg, axis=0) - seg
    max_rows = batch * (2 * SEQ + N_EXPERTS * (SEG_ALIGN - 1))
    n_groups = max_rows // GROUP_ROWS + N_EXPERTS
    g = jnp.arange(n_groups)
    g_expert = jnp.minimum(jnp.sum(g[:, None] >= jnp.cumsum(groups_e)[None, :], axis=1), N_EXPERTS - 1)
    g_rows = jnp.clip(rows_e[g_expert] - (g - first_group[g_expert]) * GROUP_ROWS, 0, GROUP_ROWS)
    g_rows = jnp.where(g < jnp.sum(groups_e), g_rows, 0)
    return (bounds.reshape(-1), seg.reshape(-1), seg_off.reshape(-1),
            g_expert.astype(jnp.int32), g_rows.astype(jnp.int32), n_groups)


def _segment_copies(seg, make_copy, action):
    n_big = lax.shift_right_logical(seg, BIG_CHUNK.bit_length() - 1)

    def big(i, carry):
        getattr(make_copy(pl.multiple_of(i * BIG_CHUNK, BIG_CHUNK), BIG_CHUNK), action)()
        return carry
    lax.fori_loop(0, n_big, big, 0)
    rest0 = n_big * BIG_CHUNK
    n_small = lax.shift_right_logical(seg - rest0, SEG_ALIGN.bit_length() - 1)

    def small(i, carry):
        getattr(make_copy(pl.multiple_of(rest0 + i * SEG_ALIGN, SEG_ALIGN), SEG_ALIGN), action)()
        return carry
    lax.fori_loop(0, n_small, small, 0)


def _block_windows(tbl_ref, b, e, align):
    base = b * N_BOUNDS * N_EXPERTS + e
    bounds = [tbl_ref[base + tb * N_EXPERTS] for tb in range(N_BOUNDS)]
    shift = align.bit_length() - 1
    win = [pl.multiple_of(lax.shift_left(lax.shift_right_logical(s, shift), shift), align) for s in bounds[:-1]]
    all_small = functools.reduce(jnp.logical_and,
                                 [bounds[tb + 1] - win[tb] <= SMALL_WINDOW for tb in range(TOK_BLOCKS)])
    return bounds, win, all_small


def _dispatch_kernel(tbl_ref, seg_ref, off_ref, h_ref, post_ref, rows_hbm, acc_ref, stage_ref, sem):
    b = pl.program_id(0)
    tr = TR_MOE

    def copies(e, action):
        slot = e % 2
        off = off_ref[b * N_EXPERTS + e]

        def make_copy(r0, rows):
            return pltpu.make_async_copy(stage_ref.at[slot, pl.ds(r0, rows), :],
                                         rows_hbm.at[pl.ds(pl.multiple_of(off + r0, SEG_ALIGN), rows), :],
                                         sem.at[slot])
        _segment_copies(seg_ref[b * N_EXPERTS + e], make_copy, action)

    for e in range(N_EXPERTS):
        bounds, win, all_small = _block_windows(tbl_ref, b, e, SUBLANES)
        n_tiles = lax.shift_right_logical(bounds[-1] + (tr - 1), tr.bit_length() - 1)

        def clear(r, carry):
            acc_ref[pl.ds(pl.multiple_of(r * tr, tr), tr), :] = jnp.zeros((tr, D_MODEL), F32)
            return carry
        lax.fori_loop(0, n_tiles + 2, clear, 0)

        def gather(height):
            slot_id = lax.broadcasted_iota(jnp.int32, (height, tr), 0).astype(F32)
            for tb in range(TOK_BLOCKS):
                ts = slice(tb * tr, (tb + 1) * tr)
                local = post_ref[e:e + 1, ts] - win[tb].astype(F32)
                sel = jnp.where(local == slot_id, 1.0, 0.0).astype(BF16)
                acc_ref[pl.ds(win[tb], height), :] += _dot(sel, h_ref[ts, :])

        @pl.when(all_small)
        def _():
            gather(SMALL_WINDOW)

        @pl.when(jnp.logical_not(all_small))
        def _():
            gather(tr + SPILL_ROWS)

        if e >= 2:
            copies(e - 2, "wait")

        def to_stage(r, carry):
            rows = pl.ds(pl.multiple_of(r * tr, tr), tr)
            stage_ref[e % 2, rows, :] = acc_ref[rows, :].astype(BF16)
            return carry
        lax.fori_loop(0, n_tiles, to_stage, 0)
        copies(e, "start")
    copies(N_EXPERTS - 2, "wait")
    copies(N_EXPERTS - 1, "wait")


def _dispatch(h, post, bounds, seg, seg_off, total_rows):
    tokens = h.shape[0]
    grid_spec = pltpu.PrefetchScalarGridSpec(
        num_scalar_prefetch=3,
        grid=(tokens // SEQ,),
        in_specs=[pl.BlockSpec((SEQ, D_MODEL), lambda b, *_: (b, 0)),
                  pl.BlockSpec((N_EXPERTS, SEQ), lambda b, *_: (0, b))],
        out_specs=pl.BlockSpec(memory_space=pl.ANY),
        scratch_shapes=[pltpu.VMEM((SEQ + 3 * TR_MOE, D_MODEL), F32),
                        pltpu.VMEM((2, STAGE_ROWS, D_MODEL), BF16),
                        pltpu.SemaphoreType.DMA((2,))],
    )
    return pl.pallas_call(
        _dispatch_kernel,
        out_shape=jax.ShapeDtypeStruct((total_rows, D_MODEL), BF16),
        grid_spec=grid_spec,
        compiler_params=_cparams(("arbitrary",)),
        name="expert_dispatch",
    )(bounds, seg, seg_off, h, post)


def _experts_kernel(ge_ref, rows_ref, hs_ref, w1_ref, w3_ref, w2_ref, ys_ref, acc_ref, wb1_ref, wb3_ref, wb2_ref):
    g, f = pl.program_id(0), pl.program_id(1)
    last_f = pl.num_programs(1) - 1
    tr = TR_MOE
    rows = rows_ref[g]
    n_full = lax.shift_right_logical(rows, tr.bit_length() - 1)
    rest = rows - n_full * tr
    n_tiles = lax.shift_right_logical(rows + (tr - 1), tr.bit_length() - 1)
    tail0 = pl.multiple_of(n_full * tr, tr)

    @pl.when(rows > 0)
    def _():
        wb1_ref[...] = w1_ref[...].astype(BF16)
        wb3_ref[...] = w3_ref[...].astype(BF16)
        wb2_ref[...] = w2_ref[...].astype(BF16)

    @pl.when(f == 0)
    def _():
        def clear(r, carry):
            acc_ref[pl.ds(pl.multiple_of(r * tr, tr), tr), :] = jnp.zeros((tr, D_MODEL), F32)
            return carry
        lax.fori_loop(0, n_tiles, clear, 0)

    def swiglu(hr, r0, m):
        z = (_silu(_dot(hr, wb1_ref[...])) * _dot(hr, wb3_ref[...])).astype(BF16)
        acc_ref[pl.ds(r0, m), :] += _dot(z, wb2_ref[...])

    def full_tile(r, carry):
        r0 = pl.multiple_of(r * tr, tr)
        swiglu(hs_ref[pl.ds(r0, tr), :], r0, tr)
        return carry
    lax.fori_loop(0, n_full, full_tile, 0)
    lo = 0
    for m in TAIL_TILES:
        @pl.when((rest > lo) & (rest <= m))
        def _():
            valid = lax.broadcasted_iota(jnp.int32, (m, D_MODEL), 0) < rest
            hr = hs_ref[pl.ds(tail0, m), :]
            swiglu(jnp.where(valid, hr, jnp.zeros_like(hr)), tail0, m)
        lo = m

    @pl.when(f == last_f)
    def _():
        ys_ref[...] = jnp.zeros_like(ys_ref)

        def store(r, carry):
            rr = pl.ds(pl.multiple_of(r * tr, tr), tr)
            ys_ref[rr, :] = acc_ref[rr, :].astype(BF16)
            return carry
        lax.fori_loop(0, n_tiles, store, 0)


def _experts(hs, g_expert, g_rows, n_groups, w1, w3, w2):
    d_ff = w1.shape[2]
    tf = TF_MOE
    n_f = d_ff // tf
    chunk = lambda g, f, gr: jnp.where(gr[g] > 0, f, n_f - 1)
    grid_spec = pltpu.PrefetchScalarGridSpec(
        num_scalar_prefetch=2,
        grid=(n_groups, n_f),
        in_specs=[
            pl.BlockSpec((GROUP_ROWS, D_MODEL), lambda g, f, ge, gr: (g, 0)),
            pl.BlockSpec((None, D_MODEL, tf), lambda g, f, ge, gr: (ge[g], 0, chunk(g, f, gr))),
            pl.BlockSpec((None, D_MODEL, tf), lambda g, f, ge, gr: (ge[g], 0, chunk(g, f, gr))),
            pl.BlockSpec((None, tf, D_MODEL), lambda g, f, ge, gr: (ge[g], chunk(g, f, gr), 0)),
        ],
        out_specs=pl.BlockSpec((GROUP_ROWS, D_MODEL), lambda g, f, ge, gr: (g, 0)),
        scratch_shapes=[pltpu.VMEM((GROUP_ROWS, D_MODEL), F32),
                        pltpu.VMEM((D_MODEL, tf), BF16), pltpu.VMEM((D_MODEL, tf), BF16),
                        pltpu.VMEM((tf, D_MODEL), BF16)],
    )
    return pl.pallas_call(
        _experts_kernel,
        out_shape=jax.ShapeDtypeStruct(hs.shape, BF16),
        grid_spec=grid_spec,
        compiler_params=_cparams(("arbitrary", "arbitrary")),
        name="expert_swiglu",
    )(g_expert, g_rows, hs, w1, w3, w2)


def _combine_kernel(tbl_ref, seg_ref, off_ref, rows_hbm, pos_ref, gates_ref, x_ref, mod_ref, out_ref, buf_ref, sem):
    b = pl.program_id(0)
    n_seq = pl.num_programs(0)
    tr = TR_MOE

    def packed_starts(seq):
        starts, first = [], 0
        for e in range(N_EXPERTS):
            starts.append(first)
            first = first + seg_ref[seq * N_EXPERTS + e]
        return starts

    def copy_segment(seq, e, first, action):
        off = off_ref[seq * N_EXPERTS + e]

        def make_copy(r0, rows):
            return pltpu.make_async_copy(
                rows_hbm.at[pl.ds(pl.multiple_of(off + r0, SEG_ALIGN), rows), :],
                buf_ref.at[seq % 2, pl.ds(pl.multiple_of(first + r0, SEG_ALIGN), rows), :],
                sem.at[seq % 2, e])
        _segment_copies(seg_ref[seq * N_EXPERTS + e], make_copy, action)

    def start_all(seq):
        for e, first in enumerate(packed_starts(seq)):
            copy_segment(seq, e, first, "start")

    @pl.when(b == 0)
    def _():
        buf_ref[...] = jnp.zeros_like(buf_ref)
        start_all(b)

    @pl.when(b + 1 < n_seq)
    def _():
        start_all(b + 1)

    out_ref[...] = x_ref[...]
    lane = lax.broadcasted_iota(jnp.int32, (tr, LANES), 1)
    layer_gate = mod_ref[5:6, :]
    for e, seg_start in enumerate(packed_starts(b)):
        copy_segment(b, e, seg_start, "wait")
    for e, seg_start in enumerate(packed_starts(b)):
        bounds, win, all_small = _block_windows(tbl_ref, b, e, SEG_ALIGN)

        def scatter(tb, first, height):
            ts = slice(tb * tr, (tb + 1) * tr)
            pos_col = jnp.sum(jnp.where(lane == e, pos_ref[ts, :], 0.0), axis=-1, keepdims=True)
            gate_col = jnp.sum(jnp.where(lane == e, gates_ref[ts, :], 0.0), axis=-1, keepdims=True)
            slot_id = lax.broadcasted_iota(jnp.int32, (tr, height), 1).astype(F32) + float(first)
            sel = jnp.where(pos_col - win[tb].astype(F32) == slot_id, 1.0, 0.0).astype(BF16)
            rows = buf_ref[b % 2, pl.ds(pl.multiple_of(seg_start + win[tb] + first, SEG_ALIGN), height), :]
            out_ref[ts, :] += layer_gate * (gate_col * _dot(sel, rows))

        @pl.when(all_small)
        def _():
            for tb in range(TOK_BLOCKS):
                scatter(tb, 0, SMALL_WINDOW)

        @pl.when(jnp.logical_not(all_small))
        def _():
            for tb in range(TOK_BLOCKS):
                scatter(tb, 0, tr)
            for tb in range(TOK_BLOCKS):
                @pl.when(bounds[tb + 1] - win[tb] > tr)
                def _():
                    scatter(tb, tr, SPILL_ROWS)


def _combine(ys, pos, gates, x, mod, bounds, seg, seg_off):
    tokens = x.shape[0]
    grid_spec = pltpu.PrefetchScalarGridSpec(
        num_scalar_prefetch=3,
        grid=(tokens // SEQ,),
        in_specs=[pl.BlockSpec(memory_space=pl.ANY),
                  pl.BlockSpec((SEQ, LANES), lambda b, *_: (b, 0)),
                  pl.BlockSpec((SEQ, LANES), lambda b, *_: (b, 0)),
                  pl.BlockSpec((SEQ, D_MODEL), lambda b, *_: (b, 0)),
                  pl.BlockSpec((None, 6, D_MODEL), lambda b, *_: (b, 0, 0))],
        out_specs=pl.BlockSpec((SEQ, D_MODEL), lambda b, *_: (b, 0)),
        scratch_shapes=[pltpu.VMEM((2, PACKED_ROWS, D_MODEL), BF16),
                        pltpu.SemaphoreType.DMA((2, N_EXPERTS))],
    )
    return pl.pallas_call(
        _combine_kernel,
        out_shape=jax.ShapeDtypeStruct((tokens, D_MODEL), F32),
        grid_spec=grid_spec,
        compiler_params=_cparams(("arbitrary",)),
        name="expert_combine",
    )(bounds, seg, seg_off, ys, pos, gates, x, mod)


def _token_mixer(x, mod, gain, w_in, q_gain, k_gain, ret_gain, w_out, bias_masks):
    *qkv, ret_in = _in_proj(x, mod, gain, w_in, q_gain, k_gain)
    attn = _dilated_attention(qkv, bias_masks)
    ret = _retention(ret_in, ret_gain)
    return _out_proj(attn, ret, x, mod, w_out)


def _moe_ffn(x, mod, gain, w_router, w1, w3, w2):
    h, gates, pos, post, start, cnt = _router(x, mod, gain, w_router)
    bounds, seg, seg_off, g_expert, g_rows, n_groups = _moe_plan(start, cnt)
    hs = _dispatch(h, post, bounds, seg, seg_off, n_groups * GROUP_ROWS)
    ys = _experts(hs, g_expert, g_rows, n_groups, w1, w3, w2)
    return _combine(ys, pos, gates, x, mod, bounds, seg, seg_off)


def kernel(x, c, rel_bias_table, norm_mix, norm_ffn, w_mod, b_mod, w_in, q_gain, k_gain, ret_gain, w_out,
           ffn_w1, ffn_w3, ffn_w2, moe_router, moe_w1, moe_w3, moe_w2):
    batch, seq, d_model = x.shape
    assert (seq, d_model) == (SEQ, D_MODEL)
    depth = w_mod.shape[0]
    mods = _modulation(c, w_mod, b_mod).reshape(depth, batch, 6, D_MODEL)
    bias_masks = _bias_masks(rel_bias_table)
    xt = x.reshape(batch * seq, d_model)
    for layer in range(depth):
        mod = mods[layer]
        xt = _token_mixer(xt, mod, norm_mix[layer], w_in[layer], q_gain[layer], k_gain[layer],
                          ret_gain[layer], w_out[layer], bias_masks)
        i = layer // 2
        if layer % 2 == 0:
            xt = _dense_ffn(xt, mod, norm_ffn[layer], ffn_w1[i], ffn_w3[i], ffn_w2[i])
        else:
            xt = _moe_ffn(xt, mod, norm_ffn[layer], moe_router[i], moe_w1[i], moe_w3[i], moe_w2[i])
    return xt.reshape(batch, seq, d_model)
```

```python
import functools
import math

import jax
import jax.numpy as jnp
import numpy as np
from jax import lax
from jax.experimental import pallas as pl
from jax.experimental.pallas import tpu as pltpu

D_MODEL = 1024
SEQ = 2048
ATTN_HEADS = 8
ATTN_HEAD_DIM = 64
ATTN_WIDTH = ATTN_HEADS * ATTN_HEAD_DIM
DILATED_PATTERNS = ((128, 1), (512, 4), (2048, 16))
BLOCK = 128
NUM_BUCKETS = 32
MAX_DISTANCE = 2048
RET_HEADS = 4
RET_KEY_DIM = 64
RET_VALUE_DIM = 128
RET_WIDTH = RET_HEADS * RET_VALUE_DIM
RET_QK_WIDTH = RET_HEADS * RET_KEY_DIM
RET_CHUNK = 128
ROPE_BASE = 10000.0
IN_WIDTH = 3 * ATTN_WIDTH + 2 * RET_QK_WIDTH + 2 * RET_WIDTH
RET_IN_WIDTH = IN_WIDTH - 3 * ATTN_WIDTH
N_EXPERTS = 8
EPS = 1e-6
NEG_INF = -1e30

LANES = 128
VMEM_LIMIT = 60 * 1024 * 1024

BF16 = jnp.bfloat16
F32 = jnp.float32

TM_PROJ = 256
TM_FFN = 512
MXU_COLS = 256
TR_MOE = 256
TF_MOE = 512
assert TM_PROJ == TR_MOE
RET_ROWS = 512
SUBLANES = 8


def _cparams(sem):
    return pltpu.CompilerParams(dimension_semantics=sem, vmem_limit_bytes=VMEM_LIMIT)


def _dot(a, b):
    return jnp.dot(a, b, preferred_element_type=F32)


def _dot_nt(a, b):
    return lax.dot_general(a, b, (((1,), (1,)), ((), ())), preferred_element_type=F32)


def _dot_tn(a, b):
    return lax.dot_general(a, b, (((0,), (0,)), ((), ())), preferred_element_type=F32)


def _split_bf16(v):
    hi = v.astype(BF16)
    lo = (v - hi.astype(F32)).astype(BF16)
    return hi, lo


def _silu(v):
    return v * (1.0 / (1.0 + jnp.exp(-v)))


def _modulated_norm(x, gain, scale, shift):
    ms = jnp.mean(x * x, axis=-1, keepdims=True)
    y = x * lax.rsqrt(ms + EPS) * gain
    return y * (1.0 + scale) + shift


def _mod_kernel(c_ref, w_ref, b_ref, o_ref):
    ca = _silu(c_ref[...]).astype(BF16)
    o_ref[...] = _dot(ca, w_ref[...].astype(BF16)) + b_ref[...]


def _modulation(c, w_mod, b_mod):
    depth, _, width = w_mod.shape
    batch = c.shape[0]
    tn = 1536
    return pl.pallas_call(
        _mod_kernel,
        out_shape=jax.ShapeDtypeStruct((depth, batch, width), F32),
        grid=(depth, width // tn),
        in_specs=[
            pl.BlockSpec((batch, D_MODEL), lambda l, n: (0, 0)),
            pl.BlockSpec((None, D_MODEL, tn), lambda l, n: (l, 0, n)),
            pl.BlockSpec((None, 1, tn), lambda l, n: (l, 0, n)),
        ],
        out_specs=pl.BlockSpec((None, batch, tn), lambda l, n: (l, 0, n)),
        compiler_params=_cparams(("arbitrary", "arbitrary")),
        name="adaln_modulation",
    )(c, w_mod, b_mod.reshape(depth, 1, width))


def _bias_kernel(table_ref, bucket_ref, o_ref):
    h = pl.program_id(1)
    bucket = bucket_ref[...]
    acc = jnp.full(bucket.shape, NEG_INF, F32)
    for b in range(NUM_BUCKETS):
        acc = jnp.where(bucket == b, table_ref[b, h], acc)
    o_ref[...] = acc


BIAS_FULL = {1: 0, 4: 2}
BIAS_FIRST = {1: 1, 4: 3, 16: 4}
N_BIAS_SETS = 5


def _bias_masks(rel_bias_table):
    i = jnp.arange(BLOCK)[:, None]
    j = jnp.arange(2 * BLOCK)[None, :]
    max_exact = NUM_BUCKETS // 2

    def bucket_of(rel, dilation, w_sub, exists):
        n = jnp.maximum(rel * dilation, 0)
        nf = jnp.maximum(n.astype(F32), float(max_exact))
        large = max_exact + (jnp.log(nf / max_exact) / math.log(MAX_DISTANCE / max_exact)
                             * (NUM_BUCKETS - max_exact)).astype(jnp.int32)
        large = jnp.minimum(large, NUM_BUCKETS - 1)
        bucket = jnp.where(n < max_exact, n, large)
        allowed = (rel >= 0) & (rel <= w_sub) & exists
        return jnp.where(allowed, bucket, -1)

    sets = [None] * N_BIAS_SETS
    for window, dilation in DILATED_PATTERNS:
        w_sub = window // dilation
        if dilation in BIAS_FULL:
            sets[BIAS_FULL[dilation]] = bucket_of(i - j + BLOCK, dilation, w_sub, j >= 0)
        sets[BIAS_FIRST[dilation]] = bucket_of(i - j, dilation, w_sub, j < BLOCK)
    buckets = jnp.stack(sets).astype(jnp.int32)
    return pl.pallas_call(
        _bias_kernel,
        out_shape=jax.ShapeDtypeStruct((N_BIAS_SETS, ATTN_HEADS // 2, 2 * BLOCK, 2 * BLOCK), F32),
        grid=(N_BIAS_SETS, ATTN_HEADS),
        in_specs=[
            pl.BlockSpec(memory_space=pltpu.SMEM),
            pl.BlockSpec((None, BLOCK, 2 * BLOCK), lambda p, h: (p, 0, 0)),
        ],
        out_specs=pl.BlockSpec((None, None, BLOCK, 2 * BLOCK), lambda p, h: (p, h // 2, h % 2, 0)),
        compiler_params=_cparams(("arbitrary", "arbitrary")),
        name="relative_bias_masks",
    )(rel_bias_table, buckets)


HALF = ATTN_WIDTH // 2


def _in_proj_kernel(x_ref, mod_ref, gain_ref, w_ref, qg_ref, kg_ref, grp_ref,
                    q1_ref, k1_ref, v1_ref, q4_ref, k4_ref, v4_ref, q16_ref, k16_ref, v16_ref, r_ref,
                    perm_ref):
    mod = mod_ref[...]
    h = _modulated_norm(x_ref[...], gain_ref[...], mod[1:2], mod[0:1]).astype(BF16)
    proj = _dot(h, w_ref[...].astype(BF16))
    grp = grp_ref[...]
    tm = proj.shape[0]

    def head_norm(t, gain):
        hi, lo = _split_bf16(t * t)
        ss = _dot(hi, grp) + _dot(lo, grp)
        return t * lax.rsqrt(ss * (1.0 / ATTN_HEAD_DIM) + EPS) * gain

    def emit(t, o1_ref, o4_ref, o16_ref):
        o1_ref[...] = t.astype(BF16)
        for j in range(ATTN_WIDTH // LANES):
            perm_ref[j] = t[:, j * LANES:(j + 1) * LANES]
        for dil, o_ref in ((4, o4_ref), (16, o16_ref)):
            for hf in range(2):
                for r in range(dil):
                    for jj in range(HALF // LANES):
                        c0 = (hf * dil + r) * HALF + jj * LANES
                        o_ref[:, c0:c0 + LANES] = perm_ref[hf * (HALF // LANES) + jj,
                                                           pl.ds(r, tm // dil, stride=dil), :].astype(BF16)

    emit(head_norm(proj[:, :ATTN_WIDTH], qg_ref[...]) * (ATTN_HEAD_DIM ** -0.5), q1_ref, q4_ref, q16_ref)
    emit(head_norm(proj[:, ATTN_WIDTH:2 * ATTN_WIDTH], kg_ref[...]), k1_ref, k4_ref, k16_ref)
    emit(proj[:, 2 * ATTN_WIDTH:3 * ATTN_WIDTH], v1_ref, v4_ref, v16_ref)
    r_ref[...] = proj[:, 3 * ATTN_WIDTH:]


def _in_proj(x, mod, gain, w_in, q_gain, k_gain):
    tokens = x.shape[0]
    tm = TM_PROJ
    per_seq = SEQ // tm
    grp = np.kron(np.eye(ATTN_HEADS), np.ones((ATTN_HEAD_DIM, ATTN_HEAD_DIM))).astype(np.float32)
    row = lambda i: (i, 0)
    const = lambda i: (0, 0)
    layouts = []
    for dil in (1, 4, 16):
        shape = jax.ShapeDtypeStruct((tokens // dil, dil * ATTN_WIDTH), BF16)
        spec = pl.BlockSpec((tm // dil, dil * ATTN_WIDTH), row)
        layouts.append(((shape,) * 3, (spec,) * 3))
    out_shape = sum((s for s, _ in layouts), ()) + (jax.ShapeDtypeStruct((tokens, RET_IN_WIDTH), F32),)
    out_specs = sum((s for _, s in layouts), ()) + (pl.BlockSpec((tm, RET_IN_WIDTH), row),)
    return pl.pallas_call(
        _in_proj_kernel,
        out_shape=out_shape,
        grid=(tokens // tm,),
        in_specs=[
            pl.BlockSpec((tm, D_MODEL), row),
            pl.BlockSpec((None, 6, D_MODEL), lambda i: (i // per_seq, 0, 0)),
            pl.BlockSpec((1, D_MODEL), const),
            pl.BlockSpec((D_MODEL, IN_WIDTH), const, pipeline_mode=pl.Buffered(1)),
            pl.BlockSpec((1, ATTN_WIDTH), const),
            pl.BlockSpec((1, ATTN_WIDTH), const),
            pl.BlockSpec((ATTN_WIDTH, ATTN_WIDTH), const),
        ],
        out_specs=out_specs,
        scratch_shapes=[pltpu.VMEM((ATTN_WIDTH // LANES, tm, LANES), F32)],
        compiler_params=_cparams(("arbitrary",)),
        name="in_projection",
    )(x, mod, gain.reshape(1, D_MODEL), w_in,
      jnp.tile(q_gain, ATTN_HEADS).reshape(1, ATTN_WIDTH),
      jnp.tile(k_gain, ATTN_HEADS).reshape(1, ATTN_WIDTH),
      jnp.asarray(grp, BF16))


PAIRS_PER_HALF = ATTN_HEADS // 4
GROUP = 4


def _pair_scores(qp, kp, vp, bias2, masks, low):
    q2 = jnp.concatenate([qp * masks[0], qp * masks[1]], axis=0)
    s = _dot_nt(q2, kp) + bias2
    m = jnp.max(s, axis=-1, keepdims=True)
    p = jnp.exp(s - m)
    den = jnp.sum(p, axis=-1, keepdims=True)
    pv = _dot(p.astype(BF16), vp)
    pick = lambda t: jnp.where(low, t[:BLOCK], t[BLOCK:])
    return pick(pv), pick(m), pick(den)


def _attn_kernel(q1_ref, k1_ref, v1_ref, q4_ref, k4_ref, v4_ref, q16_ref, k16_ref, v16_ref, bm_ref,
                 o_ref, acc_ref, max_ref, den_ref):
    lane = lax.broadcasted_iota(jnp.int32, (BLOCK, LANES), 1)
    low = lane < ATTN_HEAD_DIM
    masks = (jnp.where(low, 1.0, 0.0).astype(BF16), jnp.where(low, 0.0, 1.0).astype(BF16))

    def block(q_ref, k_ref, v_ref, c0, q0, w0, width, bias_set, rows, first):
        for p in range(PAIRS_PER_HALF):
            cs = slice(c0 + p * LANES, c0 + (p + 1) * LANES)
            acc, m, den = _pair_scores(q_ref[pl.ds(q0, BLOCK), cs], k_ref[pl.ds(w0, width), cs],
                                       v_ref[pl.ds(w0, width), cs], bm_ref[bias_set, p, :, 0:width],
                                       masks, low)
            if not first:
                m_old = max_ref[p, rows, :]
                m_new = jnp.maximum(m_old, m)
                a, b = jnp.exp(m_old - m_new), jnp.exp(m - m_new)
                den = den_ref[p, rows, :] * a + den * b
                acc = acc_ref[p, rows, :] * a + acc * b
                m = m_new
            max_ref[p, rows, :] = m
            den_ref[p, rows, :] = den
            acc_ref[p, rows, :] = acc

    def d1_group(g, carry):
        for u in range(GROUP):
            n = g * GROUP + u
            q0 = pl.multiple_of(n * BLOCK, BLOCK)
            w0 = pl.multiple_of(jnp.maximum(n - 1, 0) * BLOCK, BLOCK)
            bias_set = jnp.where(n == 0, BIAS_FIRST[1], BIAS_FULL[1])
            block(q1_ref, k1_ref, v1_ref, 0, q0, w0, 2 * BLOCK, bias_set, pl.ds(q0, BLOCK), True)
        return carry
    lax.fori_loop(0, SEQ // BLOCK // GROUP, d1_group, 0)

    for r in range(4):
        for n in range(SEQ // 4 // BLOCK):
            block(q4_ref, k4_ref, v4_ref, r * HALF, n * BLOCK, max(n - 1, 0) * BLOCK, 2 * BLOCK,
                  BIAS_FIRST[4] if n == 0 else BIAS_FULL[4],
                  pl.ds(r + 4 * BLOCK * n, BLOCK, stride=4), False)

    for r in range(16):
        block(q16_ref, k16_ref, v16_ref, r * HALF, 0, 0, BLOCK, BIAS_FIRST[16],
              pl.ds(r, BLOCK, stride=16), False)

    for n in range(SEQ // BLOCK):
        rows = slice(n * BLOCK, (n + 1) * BLOCK)
        for p in range(PAIRS_PER_HALF):
            o_ref[rows, p * LANES:(p + 1) * LANES] = (acc_ref[p, rows, :] / den_ref[p, rows, :]).astype(BF16)


def _dilated_attention(qkv, bias_masks):
    tokens = qkv[0].shape[0]
    batch = tokens // SEQ
    specs = []
    for dil in (1, 4, 16):
        specs += [pl.BlockSpec((SEQ // dil, dil * HALF), lambda b, hf: (b, hf))] * 3
    state = pltpu.VMEM((PAIRS_PER_HALF, SEQ, LANES), F32)
    return pl.pallas_call(
        _attn_kernel,
        out_shape=jax.ShapeDtypeStruct((tokens, ATTN_WIDTH), BF16),
        grid=(batch, 2),
        in_specs=specs + [pl.BlockSpec((N_BIAS_SETS, PAIRS_PER_HALF, 2 * BLOCK, 2 * BLOCK),
                                       lambda b, hf: (0, hf, 0, 0))],
        out_specs=pl.BlockSpec((SEQ, HALF), lambda b, hf: (b, hf)),
        scratch_shapes=[state, state, state],
        compiler_params=_cparams(("arbitrary", "arbitrary")),
        name="dilated_attention",
    )(*qkv, bias_masks)


def _retention_kernel(r_ref, cos_ref, sin_ref, dmask_ref, qdec_ref, kdec_ref, cdec_ref, gain_ref,
                      o_ref, state_ref):
    @pl.when(pl.program_id(1) == 0)
    def _():
        state_ref[...] = jnp.zeros_like(state_ref)

    lane = lax.broadcasted_iota(jnp.int32, (RET_CHUNK, LANES), 1)
    low = lane < RET_KEY_DIM
    first_half = (lane % RET_KEY_DIM) < (RET_KEY_DIM // 2)

    def rotate(t, cos, sin):
        partner = jnp.where(first_half, pltpu.roll(t, LANES - RET_KEY_DIM // 2, 1),
                            pltpu.roll(t, RET_KEY_DIM // 2, 1))
        return t * cos + partner * sin

    for c in range(RET_ROWS // RET_CHUNK):
        rows = slice(c * RET_CHUNK, (c + 1) * RET_CHUNK)
        for hp in range(RET_HEADS // 2):
            qs = slice(hp * LANES, (hp + 1) * LANES)
            ks = slice(RET_QK_WIDTH + hp * LANES, RET_QK_WIDTH + (hp + 1) * LANES)
            cos, sin = cos_ref[rows, qs], sin_ref[rows, qs]
            q_pair = rotate(r_ref[rows, qs], cos, sin) * (RET_KEY_DIM ** -0.5)
            k_pair = rotate(r_ref[rows, ks], cos, sin)
            for hh in range(2):
                head = 2 * hp + hh
                vs = slice(2 * RET_QK_WIDTH + head * LANES, 2 * RET_QK_WIDTH + (head + 1) * LANES)
                gs = slice(2 * RET_QK_WIDTH + RET_WIDTH + head * LANES,
                           2 * RET_QK_WIDTH + RET_WIDTH + (head + 1) * LANES)
                keep = low if hh == 0 else jnp.logical_not(low)
                qm = jnp.where(keep, q_pair, 0.0)
                vb = r_ref[rows, vs].astype(BF16)
                state = state_ref[head]
                inner = _dot_nt(qm.astype(BF16), k_pair.astype(BF16)) * dmask_ref[head]
                y = _dot(inner.astype(BF16), vb)
                y = y + _dot((qm * qdec_ref[head]).astype(BF16), state.astype(BF16))
                state_ref[head] = state * cdec_ref[head] + _dot_tn((k_pair * kdec_ref[head]).astype(BF16), vb)
                mu = jnp.mean(y, axis=-1, keepdims=True)
                yc = y - mu
                var = jnp.mean(yc * yc, axis=-1, keepdims=True)
                yn = yc * lax.rsqrt(var + EPS) * gain_ref[:, head * LANES:(head + 1) * LANES]
                o_ref[rows, head * LANES:(head + 1) * LANES] = (_silu(r_ref[rows, gs]) * yn).astype(BF16)


def _retention_tables():
    half = RET_KEY_DIM // 2
    pos = jnp.arange(SEQ, dtype=F32)
    inv = ROPE_BASE ** (-jnp.arange(half, dtype=F32) / half)
    ang = pos[:, None] * inv[None, :]
    cos, sin = jnp.cos(ang), jnp.sin(ang)
    cos_full = jnp.tile(jnp.concatenate([cos, cos], axis=-1), (1, RET_HEADS))
    sin_signed = jnp.tile(jnp.concatenate([-sin, sin], axis=-1), (1, RET_HEADS))
    log_g = jnp.log(1.0 - 2.0 ** (-5.0 - jnp.arange(RET_HEADS, dtype=F32)))
    idx = jnp.arange(RET_CHUNK, dtype=F32)
    diff = idx[:, None] - idx[None, :]
    dmask = jnp.where(diff >= 0, jnp.exp(jnp.maximum(diff, 0.0)[None] * log_g[:, None, None]), 0.0)
    q_decay = jnp.exp((idx + 1.0)[None, :] * log_g[:, None])[..., None]
    k_decay = jnp.exp((RET_CHUNK - 1.0 - idx)[None, :] * log_g[:, None])[..., None]
    chunk_decay = jnp.exp(RET_CHUNK * log_g)[:, None, None]
    full = (RET_HEADS, RET_CHUNK, LANES)
    return (cos_full, sin_signed, dmask, jnp.broadcast_to(q_decay, full),
            jnp.broadcast_to(k_decay, full), jnp.broadcast_to(chunk_decay, full))


def _retention(ret_in, ret_gain):
    tokens = ret_in.shape[0]
    batch = tokens // SEQ
    per_seq = SEQ // RET_ROWS
    cos, sin, dmask, qdec, kdec, cdec = _retention_tables()
    tab = pl.BlockSpec((RET_ROWS, RET_QK_WIDTH), lambda b, j: (j, 0))
    const3 = pl.BlockSpec((RET_HEADS, RET_CHUNK, LANES), lambda b, j: (0, 0, 0))
    return pl.pallas_call(
        _retention_kernel,
        out_shape=jax.ShapeDtypeStruct((tokens, RET_WIDTH), BF16),
        grid=(batch, per_seq),
        in_specs=[
            pl.BlockSpec((RET_ROWS, RET_IN_WIDTH), lambda b, j: (b * per_seq + j, 0)),
            tab, tab, const3, const3, const3, const3,
            pl.BlockSpec((1, RET_WIDTH), lambda b, j: (0, 0)),
        ],
        out_specs=pl.BlockSpec((RET_ROWS, RET_WIDTH), lambda b, j: (b * per_seq + j, 0)),
        scratch_shapes=[pltpu.VMEM((RET_HEADS, LANES, RET_VALUE_DIM), F32)],
        compiler_params=_cparams(("arbitrary", "arbitrary")),
        name="retention",
    )(ret_in, cos, sin, dmask, qdec, kdec, cdec, ret_gain.reshape(1, RET_WIDTH))


def _out_proj_kernel(attn_ref, ret_ref, x_ref, mod_ref, w_ref, out_ref):
    mix = (_dot(attn_ref[...], w_ref[:ATTN_WIDTH, :].astype(BF16))
           + _dot(ret_ref[...], w_ref[ATTN_WIDTH:, :].astype(BF16)))
    out_ref[...] = x_ref[...] + mod_ref[2:3, :] * mix


def _out_proj(attn, ret, x, mod, w_out):
    tokens = x.shape[0]
    tm = TM_FFN
    per_seq = SEQ // tm
    row = lambda i: (i, 0)
    return pl.pallas_call(
        _out_proj_kernel,
        out_shape=jax.ShapeDtypeStruct((tokens, D_MODEL), F32),
        grid=(tokens // tm,),
        in_specs=[pl.BlockSpec((tm, ATTN_WIDTH), row),
                  pl.BlockSpec((tm, RET_WIDTH), row),
                  pl.BlockSpec((tm, D_MODEL), row),
                  pl.BlockSpec((None, 6, D_MODEL), lambda i: (i // per_seq, 0, 0)),
                  pl.BlockSpec((D_MODEL, D_MODEL), lambda i: (0, 0))],
        out_specs=pl.BlockSpec((tm, D_MODEL), row),
        compiler_params=_cparams(("arbitrary",)),
        name="out_projection",
    )(attn, ret, x, mod, w_out)


def _swiglu_chunks(h, w1_ref, w3_ref, w2_ref):
    d_ff = w1_ref.shape[-1]
    total = None
    for c0 in range(0, d_ff, MXU_COLS):
        c1 = min(c0 + MXU_COLS, d_ff)
        z = (_silu(_dot(h, w1_ref[:, c0:c1].astype(BF16))) * _dot(h, w3_ref[:, c0:c1].astype(BF16))).astype(BF16)
        part = _dot(z, w2_ref[c0:c1, :].astype(BF16))
        total = part if total is None else total + part
    return total


def _ffn_kernel(x_ref, mod_ref, gain_ref, w1_ref, w3_ref, w2_ref, out_ref):
    mod = mod_ref[...]
    x = x_ref[...]
    h = _modulated_norm(x, gain_ref[...], mod[4:5], mod[3:4]).astype(BF16)
    out_ref[...] = x + mod[5:6, :] * _swiglu_chunks(h, w1_ref, w3_ref, w2_ref)


def _dense_ffn(x, mod, gain, w1, w3, w2):
    tokens = x.shape[0]
    d_ff = w1.shape[1]
    tm = TM_FFN
    per_seq = SEQ // tm
    resident = pl.Buffered(1)
    return pl.pallas_call(
        _ffn_kernel,
        out_shape=jax.ShapeDtypeStruct((tokens, D_MODEL), F32),
        grid=(tokens // tm,),
        in_specs=[
            pl.BlockSpec((tm, D_MODEL), lambda i: (i, 0)),
            pl.BlockSpec((None, 6, D_MODEL), lambda i: (i // per_seq, 0, 0)),
            pl.BlockSpec((1, D_MODEL), lambda i: (0, 0)),
            pl.BlockSpec((D_MODEL, d_ff), lambda i: (0, 0), pipeline_mode=resident),
            pl.BlockSpec((D_MODEL, d_ff), lambda i: (0, 0), pipeline_mode=resident),
            pl.BlockSpec((d_ff, D_MODEL), lambda i: (0, 0), pipeline_mode=resident),
        ],
        out_specs=pl.BlockSpec((tm, D_MODEL), lambda i: (i, 0)),
        compiler_params=_cparams(("arbitrary",)),
        name="dense_swiglu",
    )(x, mod, gain.reshape(1, D_MODEL), w1, w3, w2)


def _router_kernel(x_ref, mod_ref, gain_ref, wr_ref, h_ref, gates_ref, pos_ref, post_ref, start_ref,
                   cnt_ref, carry_ref, *, tiles_per_seq):
    i = pl.program_id(0)

    @pl.when(i % tiles_per_seq == 0)
    def _():
        carry_ref[...] = jnp.zeros_like(carry_ref)

    mod = mod_ref[...]
    h = _modulated_norm(x_ref[...], gain_ref[...], mod[4:5], mod[3:4]).astype(BF16)
    h_ref[...] = h
    tm = h.shape[0]
    lane = lax.broadcasted_iota(jnp.int32, (tm, LANES), 1).astype(F32)
    logits = jnp.where(lane < N_EXPERTS, _dot(h, wr_ref[...]), -jnp.inf)
    m1 = jnp.max(logits, axis=-1, keepdims=True)
    i1 = jnp.min(jnp.where(logits == m1, lane, float(LANES)), axis=-1, keepdims=True)
    rest = jnp.where(lane == i1, -jnp.inf, logits)
    m2 = jnp.max(rest, axis=-1, keepdims=True)
    i2 = jnp.min(jnp.where(rest == m2, lane, float(LANES)), axis=-1, keepdims=True)
    e2 = jnp.exp(m2 - m1)
    g1 = 1.0 / (1.0 + e2)
    g2 = e2 / (1.0 + e2)
    gates_ref[...] = jnp.where(lane == i1, g1, 0.0) + jnp.where(lane == i2, g2, 0.0)
    chosen = (lane == i1) | (lane == i2)
    onehot = jnp.where(chosen, 1.0, 0.0)
    r = lax.broadcasted_iota(jnp.int32, (tm, tm), 0)
    c = lax.broadcasted_iota(jnp.int32, (tm, tm), 1)
    tril = jnp.where(c <= r, 1.0, 0.0).astype(BF16)
    incl = _dot(tril, onehot.astype(BF16))
    carry = carry_ref[0:1, :]
    start_ref[...] = carry_ref[...]
    pos = jnp.where(chosen, incl - 1.0 + carry, -1.0)
    pos_ref[...] = pos
    post_ref[...] = pos.T[:N_EXPERTS, :]
    total = carry + incl[tm - 1:tm, :]
    carry_ref[...] = jnp.broadcast_to(total, carry_ref.shape)
    cnt_ref[...] = jnp.broadcast_to(total, cnt_ref.shape)


def _router(x, mod, gain, w_router):
    tokens = x.shape[0]
    batch = tokens // SEQ
    tm = TM_PROJ
    per_seq = SEQ // tm
    wr = jnp.zeros((D_MODEL, LANES), BF16).at[:, :N_EXPERTS].set(w_router.astype(BF16))
    row = lambda i: (i, 0)
    return pl.pallas_call(
        functools.partial(_router_kernel, tiles_per_seq=per_seq),
        out_shape=(
            jax.ShapeDtypeStruct((tokens, D_MODEL), BF16),
            jax.ShapeDtypeStruct((tokens, LANES), F32),
            jax.ShapeDtypeStruct((tokens, LANES), F32),
            jax.ShapeDtypeStruct((N_EXPERTS, tokens), F32),
            jax.ShapeDtypeStruct((tokens // tm, 8, LANES), F32),
            jax.ShapeDtypeStruct((batch, 8, LANES), F32),
        ),
        grid=(tokens // tm,),
        in_specs=[
            pl.BlockSpec((tm, D_MODEL), row),
            pl.BlockSpec((None, 6, D_MODEL), lambda i: (i // per_seq, 0, 0)),
            pl.BlockSpec((1, D_MODEL), lambda i: (0, 0)),
            pl.BlockSpec((D_MODEL, LANES), lambda i: (0, 0)),
        ],
        out_specs=(
            pl.BlockSpec((tm, D_MODEL), row),
            pl.BlockSpec((tm, LANES), row),
            pl.BlockSpec((tm, LANES), row),
            pl.BlockSpec((N_EXPERTS, tm), lambda i: (0, i)),
            pl.BlockSpec((None, 8, LANES), lambda i: (i, 0, 0)),
            pl.BlockSpec((None, 8, LANES), lambda i: (i // per_seq, 0, 0)),
        ),
        scratch_shapes=[pltpu.VMEM((8, LANES), F32)],
        compiler_params=_cparams(("arbitrary",)),
        name="expert_router",
    )(x, mod, gain.reshape(1, D_MODEL), wr)


TOK_BLOCKS = SEQ // TR_MOE
TAIL_TILES = (64, 128, TR_MOE)
SPILL_ROWS = 16
SMALL_WINDOW = 128
SEG_ALIGN = 16
GROUP_ROWS = 3072
PACKED_ROWS = 2 * SEQ + N_EXPERTS * SEG_ALIGN + 2 * TR_MOE
BIG_CHUNK = 128
STAGE_ROWS = SEQ + TR_MOE
N_BOUNDS = TOK_BLOCKS + 1


def _moe_plan(start, cnt):
    batch = cnt.shape[0]
    counts = cnt[:, 0, :N_EXPERTS].astype(jnp.int32)
    bounds = jnp.concatenate([start[:, 0, :N_EXPERTS].reshape(batch, TOK_BLOCKS, N_EXPERTS).astype(jnp.int32),
                              counts[:, None, :]], axis=1)
    seg = (counts + (SEG_ALIGN - 1)) // SEG_ALIGN * SEG_ALIGN
    rows_e = jnp.sum(seg, axis=0)
    groups_e = (rows_e + (GROUP_ROWS - 1)) // GROUP_ROWS
    first_group = jnp.cumsum(groups_e) - groups_e
    seg_off = first_group[None, :] * GROUP_ROWS + jnp.cumsum(seg, axis=0) - seg
    max_rows = batch * (2 * SEQ + N_EXPERTS * (SEG_ALIGN - 1))
    n_groups = max_rows // GROUP_ROWS + N_EXPERTS
    g = jnp.arange(n_groups)
    g_expert = jnp.minimum(jnp.sum(g[:, None] >= jnp.cumsum(groups_e)[None, :], axis=1), N_EXPERTS - 1)
    g_rows = jnp.clip(rows_e[g_expert] - (g - first_group[g_expert]) * GROUP_ROWS, 0, GROUP_ROWS)
    g_rows = jnp.where(g < jnp.sum(groups_e), g_rows, 0)
    return (bounds.reshape(-1), seg.reshape(-1), seg_off.reshape(-1),
            g_expert.astype(jnp.int32), g_rows.astype(jnp.int32), n_groups)


def _segment_copies(seg, make_copy, action):
    n_big = lax.shift_right_logical(seg, BIG_CHUNK.bit_length() - 1)

    def big(i, carry):
        getattr(make_copy(pl.multiple_of(i * BIG_CHUNK, BIG_CHUNK), BIG_CHUNK), action)()
        return carry
    lax.fori_loop(0, n_big, big, 0)
    rest0 = n_big * BIG_CHUNK
    n_small = lax.shift_right_logical(seg - rest0, SEG_ALIGN.bit_length() - 1)

    def small(i, carry):
        getattr(make_copy(pl.multiple_of(rest0 + i * SEG_ALIGN, SEG_ALIGN), SEG_ALIGN), action)()
        return carry
    lax.fori_loop(0, n_small, small, 0)


def _block_windows(tbl_ref, b, e, align):
    base = b * N_BOUNDS * N_EXPERTS + e
    bounds = [tbl_ref[base + tb * N_EXPERTS] for tb in range(N_BOUNDS)]
    shift = align.bit_length() - 1
    win = [pl.multiple_of(lax.shift_left(lax.shift_right_logical(s, shift), shift), align) for s in bounds[:-1]]
    all_small = functools.reduce(jnp.logical_and,
                                 [bounds[tb + 1] - win[tb] <= SMALL_WINDOW for tb in range(TOK_BLOCKS)])
    return bounds, win, all_small


def _dispatch_kernel(tbl_ref, seg_ref, off_ref, h_ref, post_ref, rows_hbm, acc_ref, stage_ref, sem):
    b = pl.program_id(0)
    tr = TR_MOE

    def copies(e, action):
        slot = e % 2
        off = off_ref[b * N_EXPERTS + e]

        def make_copy(r0, rows):
            return pltpu.make_async_copy(stage_ref.at[slot, pl.ds(r0, rows), :],
                                         rows_hbm.at[pl.ds(pl.multiple_of(off + r0, SEG_ALIGN), rows), :],
                                         sem.at[slot])
        _segment_copies(seg_ref[b * N_EXPERTS + e], make_copy, action)

    def expert(e, carry):
        bounds, win, all_small = _block_windows(tbl_ref, b, e, SUBLANES)
        n_tiles = lax.shift_right_logical(bounds[-1] + (tr - 1), tr.bit_length() - 1)

        def clear(r, c):
            acc_ref[pl.ds(pl.multiple_of(r * tr, tr), tr), :] = jnp.zeros((tr, D_MODEL), F32)
            return c
        lax.fori_loop(0, n_tiles + 2, clear, 0)

        def gather(height):
            slot_id = lax.broadcasted_iota(jnp.int32, (height, tr), 0).astype(F32)
            for tb in range(TOK_BLOCKS):
                ts = slice(tb * tr, (tb + 1) * tr)
                local = post_ref[pl.ds(e, 1), ts] - win[tb].astype(F32)
                sel = jnp.where(local == slot_id, 1.0, 0.0).astype(BF16)
                acc_ref[pl.ds(win[tb], height), :] += _dot(sel, h_ref[ts, :])

        @pl.when(all_small)
        def _():
            gather(SMALL_WINDOW)

        @pl.when(jnp.logical_not(all_small))
        def _():
            gather(tr + SPILL_ROWS)

        @pl.when(e >= 2)
        def _():
            copies(e - 2, "wait")

        def to_stage(r, c):
            rows = pl.ds(pl.multiple_of(r * tr, tr), tr)
            stage_ref[e % 2, rows, :] = acc_ref[rows, :].astype(BF16)
            return c
        lax.fori_loop(0, n_tiles, to_stage, 0)
        copies(e, "start")
        return carry
    lax.fori_loop(0, N_EXPERTS, expert, 0)
    copies(N_EXPERTS - 2, "wait")
    copies(N_EXPERTS - 1, "wait")


def _dispatch(h, post, bounds, seg, seg_off, total_rows):
    tokens = h.shape[0]
    grid_spec = pltpu.PrefetchScalarGridSpec(
        num_scalar_prefetch=3,
        grid=(tokens // SEQ,),
        in_specs=[pl.BlockSpec((SEQ, D_MODEL), lambda b, *_: (b, 0)),
                  pl.BlockSpec((N_EXPERTS, SEQ), lambda b, *_: (0, b))],
        out_specs=pl.BlockSpec(memory_space=pl.ANY),
        scratch_shapes=[pltpu.VMEM((SEQ + 3 * TR_MOE, D_MODEL), F32),
                        pltpu.VMEM((2, STAGE_ROWS, D_MODEL), BF16),
                        pltpu.SemaphoreType.DMA((2,))],
    )
    return pl.pallas_call(
        _dispatch_kernel,
        out_shape=jax.ShapeDtypeStruct((total_rows, D_MODEL), BF16),
        grid_spec=grid_spec,
        compiler_params=_cparams(("arbitrary",)),
        name="expert_dispatch",
    )(bounds, seg, seg_off, h, post)


def _experts_kernel(ge_ref, rows_ref, hs_ref, w1_ref, w3_ref, w2_ref, ys_ref, acc_ref, wb1_ref, wb3_ref, wb2_ref):
    g, f = pl.program_id(0), pl.program_id(1)
    last_f = pl.num_programs(1) - 1
    tr = TR_MOE
    rows = rows_ref[g]
    n_full = lax.shift_right_logical(rows, tr.bit_length() - 1)
    rest = rows - n_full * tr
    n_tiles = lax.shift_right_logical(rows + (tr - 1), tr.bit_length() - 1)
    tail0 = pl.multiple_of(n_full * tr, tr)

    @pl.when(rows > 0)
    def _():
        wb1_ref[...] = w1_ref[...].astype(BF16)
        wb3_ref[...] = w3_ref[...].astype(BF16)
        wb2_ref[...] = w2_ref[...].astype(BF16)

    @pl.when(f == 0)
    def _():
        def clear(r, carry):
            acc_ref[pl.ds(pl.multiple_of(r * tr, tr), tr), :] = jnp.zeros((tr, D_MODEL), F32)
            return carry
        lax.fori_loop(0, n_tiles, clear, 0)

    def swiglu(hr, r0, m):
        z = (_silu(_dot(hr, wb1_ref[...])) * _dot(hr, wb3_ref[...])).astype(BF16)
        acc_ref[pl.ds(r0, m), :] += _dot(z, wb2_ref[...])

    def full_tile(r, carry):
        r0 = pl.multiple_of(r * tr, tr)
        swiglu(hs_ref[pl.ds(r0, tr), :], r0, tr)
        return carry
    lax.fori_loop(0, n_full, full_tile, 0)
    lo = 0
    for m in TAIL_TILES:
        @pl.when((rest > lo) & (rest <= m))
        def _():
            valid = lax.broadcasted_iota(jnp.int32, (m, D_MODEL), 0) < rest
            hr = hs_ref[pl.ds(tail0, m), :]
            swiglu(jnp.where(valid, hr, jnp.zeros_like(hr)), tail0, m)
        lo = m

    @pl.when(f == last_f)
    def _():
        ys_ref[...] = jnp.zeros_like(ys_ref)

        def store(r, carry):
            rr = pl.ds(pl.multiple_of(r * tr, tr), tr)
            ys_ref[rr, :] = acc_ref[rr, :].astype(BF16)
            return carry
        lax.fori_loop(0, n_tiles, store, 0)


def _experts(hs, g_expert, g_rows, n_groups, w1, w3, w2):
    d_ff = w1.shape[2]
    tf = TF_MOE
    n_f = d_ff // tf
    chunk = lambda g, f, gr: jnp.where(gr[g] > 0, f, n_f - 1)
    grid_spec = pltpu.PrefetchScalarGridSpec(
        num_scalar_prefetch=2,
        grid=(n_groups, n_f),
        in_specs=[
            pl.BlockSpec((GROUP_ROWS, D_MODEL), lambda g, f, ge, gr: (g, 0)),
            pl.BlockSpec((None, D_MODEL, tf), lambda g, f, ge, gr: (ge[g], 0, chunk(g, f, gr))),
            pl.BlockSpec((None, D_MODEL, tf), lambda g, f, ge, gr: (ge[g], 0, chunk(g, f, gr))),
            pl.BlockSpec((None, tf, D_MODEL), lambda g, f, ge, gr: (ge[g], chunk(g, f, gr), 0)),
        ],
        out_specs=pl.BlockSpec((GROUP_ROWS, D_MODEL), lambda g, f, ge, gr: (g, 0)),
        scratch_shapes=[pltpu.VMEM((GROUP_ROWS, D_MODEL), F32),
                        pltpu.VMEM((D_MODEL, tf), BF16), pltpu.VMEM((D_MODEL, tf), BF16),
                        pltpu.VMEM((tf, D_MODEL), BF16)],
    )
    return pl.pallas_call(
        _experts_kernel,
        out_shape=jax.ShapeDtypeStruct(hs.shape, BF16),
        grid_spec=grid_spec,
        compiler_params=_cparams(("arbitrary", "arbitrary")),
        name="expert_swiglu",
    )(g_expert, g_rows, hs, w1, w3, w2)


def _combine_kernel(tbl_ref, seg_ref, off_ref, rows_hbm, pos_ref, gates_ref, x_ref, mod_ref, out_ref, buf_ref, sem):
    b = pl.program_id(0)
    n_seq = pl.num_programs(0)
    tr = TR_MOE

    def all_segments(seq, action):
        def body(e, first):
            off = off_ref[seq * N_EXPERTS + e]
            seg = seg_ref[seq * N_EXPERTS + e]

            def make_copy(r0, rows):
                return pltpu.make_async_copy(
                    rows_hbm.at[pl.ds(pl.multiple_of(off + r0, SEG_ALIGN), rows), :],
                    buf_ref.at[seq % 2, pl.ds(pl.multiple_of(first + r0, SEG_ALIGN), rows), :],
                    sem.at[seq % 2, e])
            _segment_copies(seg, make_copy, action)
            return first + seg
        lax.fori_loop(0, N_EXPERTS, body, 0)

    @pl.when(b == 0)
    def _():
        buf_ref[...] = jnp.zeros_like(buf_ref)
        all_segments(b, "start")

    @pl.when(b + 1 < n_seq)
    def _():
        all_segments(b + 1, "start")

    out_ref[...] = x_ref[...]
    lane = lax.broadcasted_iota(jnp.int32, (tr, LANES), 1)
    layer_gate = mod_ref[5:6, :]
    all_segments(b, "wait")

    def expert(e, seg_start):
        bounds, win, all_small = _block_windows(tbl_ref, b, e, SEG_ALIGN)

        def scatter(tb, first, height):
            ts = slice(tb * tr, (tb + 1) * tr)
            pos_col = jnp.sum(jnp.where(lane == e, pos_ref[ts, :], 0.0), axis=-1, keepdims=True)
            gate_col = jnp.sum(jnp.where(lane == e, gates_ref[ts, :], 0.0), axis=-1, keepdims=True)
            slot_id = lax.broadcasted_iota(jnp.int32, (tr, height), 1).astype(F32) + float(first)
            sel = jnp.where(pos_col - win[tb].astype(F32) == slot_id, 1.0, 0.0).astype(BF16)
            rows = buf_ref[b % 2, pl.ds(pl.multiple_of(seg_start + win[tb] + first, SEG_ALIGN), height), :]
            out_ref[ts, :] += layer_gate * (gate_col * _dot(sel, rows))

        @pl.when(all_small)
        def _():
            for tb in range(TOK_BLOCKS):
                scatter(tb, 0, SMALL_WINDOW)

        @pl.when(jnp.logical_not(all_small))
        def _():
            for tb in range(TOK_BLOCKS):
                scatter(tb, 0, tr)
            for tb in range(TOK_BLOCKS):
                @pl.when(bounds[tb + 1] - win[tb] > tr)
                def _():
                    scatter(tb, tr, SPILL_ROWS)
        return seg_start + seg_ref[b * N_EXPERTS + e]
    lax.fori_loop(0, N_EXPERTS, expert, 0)


def _combine(ys, pos, gates, x, mod, bounds, seg, seg_off):
    tokens = x.shape[0]
    grid_spec = pltpu.PrefetchScalarGridSpec(
        num_scalar_prefetch=3,
        grid=(tokens // SEQ,),
        in_specs=[pl.BlockSpec(memory_space=pl.ANY),
                  pl.BlockSpec((SEQ, LANES), lambda b, *_: (b, 0)),
                  pl.BlockSpec((SEQ, LANES), lambda b, *_: (b, 0)),
                  pl.BlockSpec((SEQ, D_MODEL), lambda b, *_: (b, 0)),
                  pl.BlockSpec((None, 6, D_MODEL), lambda b, *_: (b, 0, 0))],
        out_specs=pl.BlockSpec((SEQ, D_MODEL), lambda b, *_: (b, 0)),
        scratch_shapes=[pltpu.VMEM((2, PACKED_ROWS, D_MODEL), BF16),
                        pltpu.SemaphoreType.DMA((2, N_EXPERTS))],
    )
    return pl.pallas_call(
        _combine_kernel,
        out_shape=jax.ShapeDtypeStruct((tokens, D_MODEL), F32),
        grid_spec=grid_spec,
        compiler_params=_cparams(("arbitrary",)),
        name="expert_combine",
    )(bounds, seg, seg_off, ys, pos, gates, x, mod)


def _token_mixer(x, mod, gain, w_in, q_gain, k_gain, ret_gain, w_out, bias_masks):
    *qkv, ret_in = _in_proj(x, mod, gain, w_in, q_gain, k_gain)
    attn = _dilated_attention(qkv, bias_masks)
    ret = _retention(ret_in, ret_gain)
    return _out_proj(attn, ret, x, mod, w_out)


def _moe_ffn(x, mod, gain, w_router, w1, w3, w2):
    h, gates, pos, post, start, cnt = _router(x, mod, gain, w_router)
    bounds, seg, seg_off, g_expert, g_rows, n_groups = _moe_plan(start, cnt)
    hs = _dispatch(h, post, bounds, seg, seg_off, n_groups * GROUP_ROWS)
    ys = _experts(hs, g_expert, g_rows, n_groups, w1, w3, w2)
    return _combine(ys, pos, gates, x, mod, bounds, seg, seg_off)


def kernel(x, c, rel_bias_table, norm_mix, norm_ffn, w_mod, b_mod, w_in, q_gain, k_gain, ret_gain, w_out,
           ffn_w1, ffn_w3, ffn_w2, moe_router, moe_w1, moe_w3, moe_w2):
    batch, seq, d_model = x.shape
    assert (seq, d_model) == (SEQ, D_MODEL)
    depth = w_mod.shape[0]
    mods = _modulation(c, w_mod, b_mod).reshape(depth, batch, 6, D_MODEL)
    bias_masks = _bias_masks(rel_bias_table)
    xt = x.reshape(batch * seq, d_model)
    for layer in range(depth):
        mod = mods[layer]
        xt = _token_mixer(xt, mod, norm_mix[layer], w_in[layer], q_gain[layer], k_gain[layer],
                          ret_gain[layer], w_out[layer], bias_masks)
        i = layer // 2
        if layer % 2 == 0:
            xt = _dense_ffn(xt, mod, norm_ffn[layer], ffn_w1[i], ffn_w3[i], ffn_w2[i])
        else:
            xt = _moe_ffn(xt, mod, norm_ffn[layer], moe_router[i], moe_w1[i], moe_w3[i], moe_w2[i])
    return xt.reshape(batch, seq, d_model)
```

```python
import functools
import math

import jax
import jax.numpy as jnp
import numpy as np
from jax import lax
from jax.experimental import pallas as pl
from jax.experimental.pallas import tpu as pltpu

D_MODEL = 1024
SEQ = 2048
ATTN_HEADS = 8
ATTN_HEAD_DIM = 64
ATTN_WIDTH = ATTN_HEADS * ATTN_HEAD_DIM
DILATED_PATTERNS = ((128, 1), (512, 4), (2048, 16))
BLOCK = 128
NUM_BUCKETS = 32
MAX_DISTANCE = 2048
RET_HEADS = 4
RET_KEY_DIM = 64
RET_VALUE_DIM = 128
RET_WIDTH = RET_HEADS * RET_VALUE_DIM
RET_QK_WIDTH = RET_HEADS * RET_KEY_DIM
RET_CHUNK = 128
ROPE_BASE = 10000.0
IN_WIDTH = 3 * ATTN_WIDTH + 2 * RET_QK_WIDTH + 2 * RET_WIDTH
RET_IN_WIDTH = IN_WIDTH - 3 * ATTN_WIDTH
N_EXPERTS = 8
EPS = 1e-6
NEG_INF = -1e30

LANES = 128
VMEM_LIMIT = 60 * 1024 * 1024

BF16 = jnp.bfloat16
F32 = jnp.float32

TM_PROJ = 256
TM_FFN = 512
MXU_COLS = 256
TR_MOE = 256
TF_MOE = 512
assert TM_PROJ == TR_MOE
RET_ROWS = 512
SUBLANES = 8


def _cparams(sem):
    return pltpu.CompilerParams(dimension_semantics=sem, vmem_limit_bytes=VMEM_LIMIT)


def _dot(a, b):
    return jnp.dot(a, b, preferred_element_type=F32)


def _dot_nt(a, b):
    return lax.dot_general(a, b, (((1,), (1,)), ((), ())), preferred_element_type=F32)


def _dot_tn(a, b):
    return lax.dot_general(a, b, (((0,), (0,)), ((), ())), preferred_element_type=F32)


def _split_bf16(v):
    hi = v.astype(BF16)
    lo = (v - hi.astype(F32)).astype(BF16)
    return hi, lo


def _silu(v):
    return v * (1.0 / (1.0 + jnp.exp(-v)))


def _modulated_norm(x, gain, scale, shift):
    ms = jnp.mean(x * x, axis=-1, keepdims=True)
    y = x * lax.rsqrt(ms + EPS) * gain
    return y * (1.0 + scale) + shift


def _mod_kernel(c_ref, w_ref, b_ref, o_ref):
    ca = _silu(c_ref[...]).astype(BF16)
    o_ref[...] = _dot(ca, w_ref[...].astype(BF16)) + b_ref[...]


def _modulation(c, w_mod, b_mod):
    depth, _, width = w_mod.shape
    batch = c.shape[0]
    tn = 1536
    return pl.pallas_call(
        _mod_kernel,
        out_shape=jax.ShapeDtypeStruct((depth, batch, width), F32),
        grid=(depth, width // tn),
        in_specs=[
            pl.BlockSpec((batch, D_MODEL), lambda l, n: (0, 0)),
            pl.BlockSpec((None, D_MODEL, tn), lambda l, n: (l, 0, n)),
            pl.BlockSpec((None, 1, tn), lambda l, n: (l, 0, n)),
        ],
        out_specs=pl.BlockSpec((None, batch, tn), lambda l, n: (l, 0, n)),
        compiler_params=_cparams(("arbitrary", "arbitrary")),
        name="adaln_modulation",
    )(c, w_mod, b_mod.reshape(depth, 1, width))


def _bias_kernel(table_ref, bucket_ref, o_ref):
    h = pl.program_id(1)
    bucket = bucket_ref[...]
    acc = jnp.full(bucket.shape, NEG_INF, F32)
    for b in range(NUM_BUCKETS):
        acc = jnp.where(bucket == b, table_ref[b, h], acc)
    o_ref[...] = acc


BIAS_FULL = {1: 0, 4: 2}
BIAS_FIRST = {1: 1, 4: 3, 16: 4}
N_BIAS_SETS = 5


def _bias_masks(rel_bias_table):
    i = jnp.arange(BLOCK)[:, None]
    j = jnp.arange(2 * BLOCK)[None, :]
    max_exact = NUM_BUCKETS // 2

    def bucket_of(rel, dilation, w_sub, exists):
        n = jnp.maximum(rel * dilation, 0)
        nf = jnp.maximum(n.astype(F32), float(max_exact))
        large = max_exact + (jnp.log(nf / max_exact) / math.log(MAX_DISTANCE / max_exact)
                             * (NUM_BUCKETS - max_exact)).astype(jnp.int32)
        large = jnp.minimum(large, NUM_BUCKETS - 1)
        bucket = jnp.where(n < max_exact, n, large)
        allowed = (rel >= 0) & (rel <= w_sub) & exists
        return jnp.where(allowed, bucket, -1)

    sets = [None] * N_BIAS_SETS
    for window, dilation in DILATED_PATTERNS:
        w_sub = window // dilation
        if dilation in BIAS_FULL:
            sets[BIAS_FULL[dilation]] = bucket_of(i - j + BLOCK, dilation, w_sub, j >= 0)
        sets[BIAS_FIRST[dilation]] = bucket_of(i - j, dilation, w_sub, j < BLOCK)
    buckets = jnp.stack(sets).astype(jnp.int32)
    return pl.pallas_call(
        _bias_kernel,
        out_shape=jax.ShapeDtypeStruct((N_BIAS_SETS, ATTN_HEADS // 2, 2 * BLOCK, 2 * BLOCK), F32),
        grid=(N_BIAS_SETS, ATTN_HEADS),
        in_specs=[
            pl.BlockSpec(memory_space=pltpu.SMEM),
            pl.BlockSpec((None, BLOCK, 2 * BLOCK), lambda p, h: (p, 0, 0)),
        ],
        out_specs=pl.BlockSpec((None, None, BLOCK, 2 * BLOCK), lambda p, h: (p, h // 2, h % 2, 0)),
        compiler_params=_cparams(("arbitrary", "arbitrary")),
        name="relative_bias_masks",
    )(rel_bias_table, buckets)


HALF = ATTN_WIDTH // 2


def _in_proj_kernel(x_ref, mod_ref, gain_ref, w_ref, qg_ref, kg_ref, grp_ref,
                    q1_ref, k1_ref, v1_ref, q4_ref, k4_ref, v4_ref, q16_ref, k16_ref, v16_ref, r_ref,
                    perm_ref):
    mod = mod_ref[...]
    h = _modulated_norm(x_ref[...], gain_ref[...], mod[1:2], mod[0:1]).astype(BF16)
    proj = _dot(h, w_ref[...].astype(BF16))
    grp = grp_ref[...]
    tm = proj.shape[0]

    def head_norm(t, gain):
        hi, lo = _split_bf16(t * t)
        ss = _dot(hi, grp) + _dot(lo, grp)
        return t * lax.rsqrt(ss * (1.0 / ATTN_HEAD_DIM) + EPS) * gain

    def emit(t, o1_ref, o4_ref, o16_ref):
        o1_ref[...] = t.astype(BF16)
        for j in range(ATTN_WIDTH // LANES):
            perm_ref[j] = t[:, j * LANES:(j + 1) * LANES]
        for dil, o_ref in ((4, o4_ref), (16, o16_ref)):
            for hf in range(2):
                for r in range(dil):
                    for jj in range(HALF // LANES):
                        c0 = (hf * dil + r) * HALF + jj * LANES
                        o_ref[:, c0:c0 + LANES] = perm_ref[hf * (HALF // LANES) + jj,
                                                           pl.ds(r, tm // dil, stride=dil), :].astype(BF16)

    emit(head_norm(proj[:, :ATTN_WIDTH], qg_ref[...]) * (ATTN_HEAD_DIM ** -0.5), q1_ref, q4_ref, q16_ref)
    emit(head_norm(proj[:, ATTN_WIDTH:2 * ATTN_WIDTH], kg_ref[...]), k1_ref, k4_ref, k16_ref)
    emit(proj[:, 2 * ATTN_WIDTH:3 * ATTN_WIDTH], v1_ref, v4_ref, v16_ref)
    r_ref[...] = proj[:, 3 * ATTN_WIDTH:]


def _in_proj(x, mod, gain, w_in, q_gain, k_gain):
    tokens = x.shape[0]
    tm = TM_PROJ
    per_seq = SEQ // tm
    grp = np.kron(np.eye(ATTN_HEADS), np.ones((ATTN_HEAD_DIM, ATTN_HEAD_DIM))).astype(np.float32)
    row = lambda i: (i, 0)
    const = lambda i: (0, 0)
    layouts = []
    for dil in (1, 4, 16):
        shape = jax.ShapeDtypeStruct((tokens // dil, dil * ATTN_WIDTH), BF16)
        spec = pl.BlockSpec((tm // dil, dil * ATTN_WIDTH), row)
        layouts.append(((shape,) * 3, (spec,) * 3))
    out_shape = sum((s for s, _ in layouts), ()) + (jax.ShapeDtypeStruct((tokens, RET_IN_WIDTH), F32),)
    out_specs = sum((s for _, s in layouts), ()) + (pl.BlockSpec((tm, RET_IN_WIDTH), row),)
    return pl.pallas_call(
        _in_proj_kernel,
        out_shape=out_shape,
        grid=(tokens // tm,),
        in_specs=[
            pl.BlockSpec((tm, D_MODEL), row),
            pl.BlockSpec((None, 6, D_MODEL), lambda i: (i // per_seq, 0, 0)),
            pl.BlockSpec((1, D_MODEL), const),
            pl.BlockSpec((D_MODEL, IN_WIDTH), const, pipeline_mode=pl.Buffered(1)),
            pl.BlockSpec((1, ATTN_WIDTH), const),
            pl.BlockSpec((1, ATTN_WIDTH), const),
            pl.BlockSpec((ATTN_WIDTH, ATTN_WIDTH), const),
        ],
        out_specs=out_specs,
        scratch_shapes=[pltpu.VMEM((ATTN_WIDTH // LANES, tm, LANES), F32)],
        compiler_params=_cparams(("arbitrary",)),
        name="in_projection",
    )(x, mod, gain.reshape(1, D_MODEL), w_in,
      jnp.tile(q_gain, ATTN_HEADS).reshape(1, ATTN_WIDTH),
      jnp.tile(k_gain, ATTN_HEADS).reshape(1, ATTN_WIDTH),
      jnp.asarray(grp, BF16))


PAIRS_PER_HALF = ATTN_HEADS // 4
GROUP = 4


def _pair_scores(qp, kp, vp, bias2, masks, low):
    q2 = jnp.concatenate([qp * masks[0], qp * masks[1]], axis=0)
    s = _dot_nt(q2, kp) + bias2
    m = jnp.max(s, axis=-1, keepdims=True)
    p = jnp.exp(s - m)
    den = jnp.sum(p, axis=-1, keepdims=True)
    pv = _dot(p.astype(BF16), vp)
    pick = lambda t: jnp.where(low, t[:BLOCK], t[BLOCK:])
    return pick(pv), pick(m), pick(den)


def _attn_kernel(q1_ref, k1_ref, v1_ref, q4_ref, k4_ref, v4_ref, q16_ref, k16_ref, v16_ref, bm_ref,
                 o_ref, acc_ref, max_ref, den_ref):
    lane = lax.broadcasted_iota(jnp.int32, (BLOCK, LANES), 1)
    low = lane < ATTN_HEAD_DIM
    masks = (jnp.where(low, 1.0, 0.0).astype(BF16), jnp.where(low, 0.0, 1.0).astype(BF16))

    def block(q_ref, k_ref, v_ref, c0, q0, w0, width, bias_set, rows, first):
        for p in range(PAIRS_PER_HALF):
            cs = slice(c0 + p * LANES, c0 + (p + 1) * LANES)
            acc, m, den = _pair_scores(q_ref[pl.ds(q0, BLOCK), cs], k_ref[pl.ds(w0, width), cs],
                                       v_ref[pl.ds(w0, width), cs], bm_ref[bias_set, p, :, 0:width],
                                       masks, low)
            if not first:
                m_old = max_ref[p, rows, :]
                m_new = jnp.maximum(m_old, m)
                a, b = jnp.exp(m_old - m_new), jnp.exp(m - m_new)
                den = den_ref[p, rows, :] * a + den * b
                acc = acc_ref[p, rows, :] * a + acc * b
                m = m_new
            max_ref[p, rows, :] = m
            den_ref[p, rows, :] = den
            acc_ref[p, rows, :] = acc

    def d1_group(g, carry):
        for u in range(GROUP):
            n = g * GROUP + u
            q0 = pl.multiple_of(n * BLOCK, BLOCK)
            w0 = pl.multiple_of(jnp.maximum(n - 1, 0) * BLOCK, BLOCK)
            bias_set = jnp.where(n == 0, BIAS_FIRST[1], BIAS_FULL[1])
            block(q1_ref, k1_ref, v1_ref, 0, q0, w0, 2 * BLOCK, bias_set, pl.ds(q0, BLOCK), True)
        return carry
    lax.fori_loop(0, SEQ // BLOCK // GROUP, d1_group, 0)

    for r in range(4):
        for n in range(SEQ // 4 // BLOCK):
            block(q4_ref, k4_ref, v4_ref, r * HALF, n * BLOCK, max(n - 1, 0) * BLOCK, 2 * BLOCK,
                  BIAS_FIRST[4] if n == 0 else BIAS_FULL[4],
                  pl.ds(r + 4 * BLOCK * n, BLOCK, stride=4), False)

    for r in range(16):
        block(q16_ref, k16_ref, v16_ref, r * HALF, 0, 0, BLOCK, BIAS_FIRST[16],
              pl.ds(r, BLOCK, stride=16), False)

    for n in range(SEQ // BLOCK):
        rows = slice(n * BLOCK, (n + 1) * BLOCK)
        for p in range(PAIRS_PER_HALF):
            o_ref[rows, p * LANES:(p + 1) * LANES] = (acc_ref[p, rows, :] / den_ref[p, rows, :]).astype(BF16)


def _dilated_attention(qkv, bias_masks):
    tokens = qkv[0].shape[0]
    batch = tokens // SEQ
    specs = []
    for dil in (1, 4, 16):
        specs += [pl.BlockSpec((SEQ // dil, dil * HALF), lambda b, hf: (b, hf))] * 3
    state = pltpu.VMEM((PAIRS_PER_HALF, SEQ, LANES), F32)
    return pl.pallas_call(
        _attn_kernel,
        out_shape=jax.ShapeDtypeStruct((tokens, ATTN_WIDTH), BF16),
        grid=(batch, 2),
        in_specs=specs + [pl.BlockSpec((N_BIAS_SETS, PAIRS_PER_HALF, 2 * BLOCK, 2 * BLOCK),
                                       lambda b, hf: (0, hf, 0, 0))],
        out_specs=pl.BlockSpec((SEQ, HALF), lambda b, hf: (b, hf)),
        scratch_shapes=[state, state, state],
        compiler_params=_cparams(("arbitrary", "arbitrary")),
        name="dilated_attention",
    )(*qkv, bias_masks)


def _retention_kernel(r_ref, cos_ref, sin_ref, dmask_ref, qdec_ref, kdec_ref, cdec_ref, gain_ref,
                      o_ref, state_ref):
    @pl.when(pl.program_id(1) == 0)
    def _():
        state_ref[...] = jnp.zeros_like(state_ref)

    lane = lax.broadcasted_iota(jnp.int32, (RET_CHUNK, LANES), 1)
    low = lane < RET_KEY_DIM
    first_half = (lane % RET_KEY_DIM) < (RET_KEY_DIM // 2)

    def rotate(t, cos, sin):
        partner = jnp.where(first_half, pltpu.roll(t, LANES - RET_KEY_DIM // 2, 1),
                            pltpu.roll(t, RET_KEY_DIM // 2, 1))
        return t * cos + partner * sin

    for c in range(RET_ROWS // RET_CHUNK):
        rows = slice(c * RET_CHUNK, (c + 1) * RET_CHUNK)
        for hp in range(RET_HEADS // 2):
            qs = slice(hp * LANES, (hp + 1) * LANES)
            ks = slice(RET_QK_WIDTH + hp * LANES, RET_QK_WIDTH + (hp + 1) * LANES)
            cos, sin = cos_ref[rows, qs], sin_ref[rows, qs]
            q_pair = rotate(r_ref[rows, qs], cos, sin) * (RET_KEY_DIM ** -0.5)
            k_pair = rotate(r_ref[rows, ks], cos, sin)
            for hh in range(2):
                head = 2 * hp + hh
                vs = slice(2 * RET_QK_WIDTH + head * LANES, 2 * RET_QK_WIDTH + (head + 1) * LANES)
                gs = slice(2 * RET_QK_WIDTH + RET_WIDTH + head * LANES,
                           2 * RET_QK_WIDTH + RET_WIDTH + (head + 1) * LANES)
                keep = low if hh == 0 else jnp.logical_not(low)
                qm = jnp.where(keep, q_pair, 0.0)
                vb = r_ref[rows, vs].astype(BF16)
                state = state_ref[head]
                inner = _dot_nt(qm.astype(BF16), k_pair.astype(BF16)) * dmask_ref[head]
                y = _dot(inner.astype(BF16), vb)
                y = y + _dot((qm * qdec_ref[head]).astype(BF16), state.astype(BF16))
                state_ref[head] = state * cdec_ref[head] + _dot_tn((k_pair * kdec_ref[head]).astype(BF16), vb)
                mu = jnp.mean(y, axis=-1, keepdims=True)
                yc = y - mu
                var = jnp.mean(yc * yc, axis=-1, keepdims=True)
                yn = yc * lax.rsqrt(var + EPS) * gain_ref[:, head * LANES:(head + 1) * LANES]
                o_ref[rows, head * LANES:(head + 1) * LANES] = (_silu(r_ref[rows, gs]) * yn).astype(BF16)


def _retention_tables():
    half = RET_KEY_DIM // 2
    pos = jnp.arange(SEQ, dtype=F32)
    inv = ROPE_BASE ** (-jnp.arange(half, dtype=F32) / half)
    ang = pos[:, None] * inv[None, :]
    cos, sin = jnp.cos(ang), jnp.sin(ang)
    cos_full = jnp.tile(jnp.concatenate([cos, cos], axis=-1), (1, RET_HEADS))
    sin_signed = jnp.tile(jnp.concatenate([-sin, sin], axis=-1), (1, RET_HEADS))
    log_g = jnp.log(1.0 - 2.0 ** (-5.0 - jnp.arange(RET_HEADS, dtype=F32)))
    idx = jnp.arange(RET_CHUNK, dtype=F32)
    diff = idx[:, None] - idx[None, :]
    dmask = jnp.where(diff >= 0, jnp.exp(jnp.maximum(diff, 0.0)[None] * log_g[:, None, None]), 0.0)
    q_decay = jnp.exp((idx + 1.0)[None, :] * log_g[:, None])[..., None]
    k_decay = jnp.exp((RET_CHUNK - 1.0 - idx)[None, :] * log_g[:, None])[..., None]
    chunk_decay = jnp.exp(RET_CHUNK * log_g)[:, None, None]
    full = (RET_HEADS, RET_CHUNK, LANES)
    return (cos_full, sin_signed, dmask, jnp.broadcast_to(q_decay, full),
            jnp.broadcast_to(k_decay, full), jnp.broadcast_to(chunk_decay, full))


def _retention(ret_in, ret_gain):
    tokens = ret_in.shape[0]
    batch = tokens // SEQ
    per_seq = SEQ // RET_ROWS
    cos, sin, dmask, qdec, kdec, cdec = _retention_tables()
    tab = pl.BlockSpec((RET_ROWS, RET_QK_WIDTH), lambda b, j: (j, 0))
    const3 = pl.BlockSpec((RET_HEADS, RET_CHUNK, LANES), lambda b, j: (0, 0, 0))
    return pl.pallas_call(
        _retention_kernel,
        out_shape=jax.ShapeDtypeStruct((tokens, RET_WIDTH), BF16),
        grid=(batch, per_seq),
        in_specs=[
            pl.BlockSpec((RET_ROWS, RET_IN_WIDTH), lambda b, j: (b * per_seq + j, 0)),
            tab, tab, const3, const3, const3, const3,
            pl.BlockSpec((1, RET_WIDTH), lambda b, j: (0, 0)),
        ],
        out_specs=pl.BlockSpec((RET_ROWS, RET_WIDTH), lambda b, j: (b * per_seq + j, 0)),
        scratch_shapes=[pltpu.VMEM((RET_HEADS, LANES, RET_VALUE_DIM), F32)],
        compiler_params=_cparams(("arbitrary", "arbitrary")),
        name="retention",
    )(ret_in, cos, sin, dmask, qdec, kdec, cdec, ret_gain.reshape(1, RET_WIDTH))


def _out_proj_kernel(attn_ref, ret_ref, x_ref, mod_ref, w_ref, out_ref):
    mix = (_dot(attn_ref[...], w_ref[:ATTN_WIDTH, :].astype(BF16))
           + _dot(ret_ref[...], w_ref[ATTN_WIDTH:, :].astype(BF16)))
    out_ref[...] = x_ref[...] + mod_ref[2:3, :] * mix


def _out_proj(attn, ret, x, mod, w_out):
    tokens = x.shape[0]
    tm = TM_FFN
    per_seq = SEQ // tm
    row = lambda i: (i, 0)
    return pl.pallas_call(
        _out_proj_kernel,
        out_shape=jax.ShapeDtypeStruct((tokens, D_MODEL), F32),
        grid=(tokens // tm,),
        in_specs=[pl.BlockSpec((tm, ATTN_WIDTH), row),
                  pl.BlockSpec((tm, RET_WIDTH), row),
                  pl.BlockSpec((tm, D_MODEL), row),
                  pl.BlockSpec((None, 6, D_MODEL), lambda i: (i // per_seq, 0, 0)),
                  pl.BlockSpec((D_MODEL, D_MODEL), lambda i: (0, 0))],
        out_specs=pl.BlockSpec((tm, D_MODEL), row),
        compiler_params=_cparams(("arbitrary",)),
        name="out_projection",
    )(attn, ret, x, mod, w_out)


def _swiglu_chunks(h, w1_ref, w3_ref, w2_ref):
    d_ff = w1_ref.shape[-1]
    total = None
    for c0 in range(0, d_ff, MXU_COLS):
        c1 = min(c0 + MXU_COLS, d_ff)
        z = (_silu(_dot(h, w1_ref[:, c0:c1].astype(BF16))) * _dot(h, w3_ref[:, c0:c1].astype(BF16))).astype(BF16)
        part = _dot(z, w2_ref[c0:c1, :].astype(BF16))
        total = part if total is None else total + part
    return total


def _ffn_kernel(x_ref, mod_ref, gain_ref, w1_ref, w3_ref, w2_ref, out_ref):
    mod = mod_ref[...]
    x = x_ref[...]
    h = _modulated_norm(x, gain_ref[...], mod[4:5], mod[3:4]).astype(BF16)
    out_ref[...] = x + mod[5:6, :] * _swiglu_chunks(h, w1_ref, w3_ref, w2_ref)


def _dense_ffn(x, mod, gain, w1, w3, w2):
    tokens = x.shape[0]
    d_ff = w1.shape[1]
    tm = TM_FFN
    per_seq = SEQ // tm
    resident = pl.Buffered(1)
    return pl.pallas_call(
        _ffn_kernel,
        out_shape=jax.ShapeDtypeStruct((tokens, D_MODEL), F32),
        grid=(tokens // tm,),
        in_specs=[
            pl.BlockSpec((tm, D_MODEL), lambda i: (i, 0)),
            pl.BlockSpec((None, 6, D_MODEL), lambda i: (i // per_seq, 0, 0)),
            pl.BlockSpec((1, D_MODEL), lambda i: (0, 0)),
            pl.BlockSpec((D_MODEL, d_ff), lambda i: (0, 0), pipeline_mode=resident),
            pl.BlockSpec((D_MODEL, d_ff), lambda i: (0, 0), pipeline_mode=resident),
            pl.BlockSpec((d_ff, D_MODEL), lambda i: (0, 0), pipeline_mode=resident),
        ],
        out_specs=pl.BlockSpec((tm, D_MODEL), lambda i: (i, 0)),
        compiler_params=_cparams(("arbitrary",)),
        name="dense_swiglu",
    )(x, mod, gain.reshape(1, D_MODEL), w1, w3, w2)


def _router_kernel(x_ref, mod_ref, gain_ref, wr_ref, h_ref, gates_ref, pos_ref, post_ref, start_ref,
                   cnt_ref, carry_ref, *, tiles_per_seq):
    i = pl.program_id(0)

    @pl.when(i % tiles_per_seq == 0)
    def _():
        carry_ref[...] = jnp.zeros_like(carry_ref)

    mod = mod_ref[...]
    h = _modulated_norm(x_ref[...], gain_ref[...], mod[4:5], mod[3:4]).astype(BF16)
    h_ref[...] = h
    tm = h.shape[0]
    lane = lax.broadcasted_iota(jnp.int32, (tm, LANES), 1).astype(F32)
    logits = jnp.where(lane < N_EXPERTS, _dot(h, wr_ref[...]), -jnp.inf)
    m1 = jnp.max(logits, axis=-1, keepdims=True)
    i1 = jnp.min(jnp.where(logits == m1, lane, float(LANES)), axis=-1, keepdims=True)
    rest = jnp.where(lane == i1, -jnp.inf, logits)
    m2 = jnp.max(rest, axis=-1, keepdims=True)
    i2 = jnp.min(jnp.where(rest == m2, lane, float(LANES)), axis=-1, keepdims=True)
    e2 = jnp.exp(m2 - m1)
    g1 = 1.0 / (1.0 + e2)
    g2 = e2 / (1.0 + e2)
    gates_ref[...] = jnp.where(lane == i1, g1, 0.0) + jnp.where(lane == i2, g2, 0.0)
    chosen = (lane == i1) | (lane == i2)
    onehot = jnp.where(chosen, 1.0, 0.0)
    r = lax.broadcasted_iota(jnp.int32, (tm, tm), 0)
    c = lax.broadcasted_iota(jnp.int32, (tm, tm), 1)
    tril = jnp.where(c <= r, 1.0, 0.0).astype(BF16)
    incl = _dot(tril, onehot.astype(BF16))
    carry = carry_ref[0:1, :]
    start_ref[...] = carry_ref[...]
    pos = jnp.where(chosen, incl - 1.0 + carry, -1.0)
    pos_ref[...] = pos
    post_ref[...] = pos.T[:N_EXPERTS, :]
    total = carry + incl[tm - 1:tm, :]
    carry_ref[...] = jnp.broadcast_to(total, carry_ref.shape)
    cnt_ref[...] = jnp.broadcast_to(total, cnt_ref.shape)


def _router(x, mod, gain, w_router):
    tokens = x.shape[0]
    batch = tokens // SEQ
    tm = TM_PROJ
    per_seq = SEQ // tm
    wr = jnp.zeros((D_MODEL, LANES), BF16).at[:, :N_EXPERTS].set(w_router.astype(BF16))
    row = lambda i: (i, 0)
    return pl.pallas_call(
        functools.partial(_router_kernel, tiles_per_seq=per_seq),
        out_shape=(
            jax.ShapeDtypeStruct((tokens, D_MODEL), BF16),
            jax.ShapeDtypeStruct((tokens, LANES), F32),
            jax.ShapeDtypeStruct((tokens, LANES), F32),
            jax.ShapeDtypeStruct((N_EXPERTS, tokens), F32),
            jax.ShapeDtypeStruct((tokens // tm, 8, LANES), F32),
            jax.ShapeDtypeStruct((batch, 8, LANES), F32),
        ),
        grid=(tokens // tm,),
        in_specs=[
            pl.BlockSpec((tm, D_MODEL), row),
            pl.BlockSpec((None, 6, D_MODEL), lambda i: (i // per_seq, 0, 0)),
            pl.BlockSpec((1, D_MODEL), lambda i: (0, 0)),
            pl.BlockSpec((D_MODEL, LANES), lambda i: (0, 0)),
        ],
        out_specs=(
            pl.BlockSpec((tm, D_MODEL), row),
            pl.BlockSpec((tm, LANES), row),
            pl.BlockSpec((tm, LANES), row),
            pl.BlockSpec((N_EXPERTS, tm), lambda i: (0, i)),
            pl.BlockSpec((None, 8, LANES), lambda i: (i, 0, 0)),
            pl.BlockSpec((None, 8, LANES), lambda i: (i // per_seq, 0, 0)),
        ),
        scratch_shapes=[pltpu.VMEM((8, LANES), F32)],
        compiler_params=_cparams(("arbitrary",)),
        name="expert_router",
    )(x, mod, gain.reshape(1, D_MODEL), wr)


TOK_BLOCKS = SEQ // TR_MOE
TAIL_TILES = (64, 128, TR_MOE)
SPILL_ROWS = 16
SMALL_WINDOW = 128
SEG_ALIGN = 16
GROUP_ROWS = 3072
PACKED_ROWS = 2 * SEQ + N_EXPERTS * SEG_ALIGN + 2 * TR_MOE
BIG_CHUNK = 128
STAGE_ROWS = SEQ + TR_MOE
N_BOUNDS = TOK_BLOCKS + 1


def _moe_plan(start, cnt):
    batch = cnt.shape[0]
    counts = cnt[:, 0, :N_EXPERTS].astype(jnp.int32)
    bounds = jnp.concatenate([start[:, 0, :N_EXPERTS].reshape(batch, TOK_BLOCKS, N_EXPERTS).astype(jnp.int32),
                              counts[:, None, :]], axis=1)
    seg = (counts + (SEG_ALIGN - 1)) // SEG_ALIGN * SEG_ALIGN
    rows_e = jnp.sum(seg, axis=0)
    groups_e = (rows_e + (GROUP_ROWS - 1)) // GROUP_ROWS
    first_group = jnp.cumsum(groups_e) - groups_e
    seg_off = first_group[None, :] * GROUP_ROWS + jnp.cumsum(seg, axis=0) - seg
    max_rows = batch * (2 * SEQ + N_EXPERTS * (SEG_ALIGN - 1))
    n_groups = max_rows // GROUP_ROWS + N_EXPERTS
    g = jnp.arange(n_groups)
    g_expert = jnp.minimum(jnp.sum(g[:, None] >= jnp.cumsum(groups_e)[None, :], axis=1), N_EXPERTS - 1)
    g_rows = jnp.clip(rows_e[g_expert] - (g - first_group[g_expert]) * GROUP_ROWS, 0, GROUP_ROWS)
    g_rows = jnp.where(g < jnp.sum(groups_e), g_rows, 0)
    return (bounds.reshape(-1), seg.reshape(-1), seg_off.reshape(-1),
            g_expert.astype(jnp.int32), g_rows.astype(jnp.int32), n_groups)


def _segment_copies(seg, make_copy, action):
    n_big = lax.shift_right_logical(seg, BIG_CHUNK.bit_length() - 1)

    def big(i, carry):
        getattr(make_copy(pl.multiple_of(i * BIG_CHUNK, BIG_CHUNK), BIG_CHUNK), action)()
        return carry
    lax.fori_loop(0, n_big, big, 0)
    rest0 = n_big * BIG_CHUNK
    n_small = lax.shift_right_logical(seg - rest0, SEG_ALIGN.bit_length() - 1)

    def small(i, carry):
        getattr(make_copy(pl.multiple_of(rest0 + i * SEG_ALIGN, SEG_ALIGN), SEG_ALIGN), action)()
        return carry
    lax.fori_loop(0, n_small, small, 0)


def _block_windows(tbl_ref, b, e, align):
    base = b * N_BOUNDS * N_EXPERTS + e
    bounds = [tbl_ref[base + tb * N_EXPERTS] for tb in range(N_BOUNDS)]
    shift = align.bit_length() - 1
    win = [pl.multiple_of(lax.shift_left(lax.shift_right_logical(s, shift), shift), align) for s in bounds[:-1]]
    all_small = functools.reduce(jnp.logical_and,
                                 [bounds[tb + 1] - win[tb] <= SMALL_WINDOW for tb in range(TOK_BLOCKS)])
    return bounds, win, all_small


def _dispatch_kernel(tbl_ref, seg_ref, off_ref, h_ref, post_ref, rows_hbm, acc_ref, stage_ref, sem):
    b = pl.program_id(0)
    tr = TR_MOE

    def copies(e, action):
        slot = e % 2
        off = off_ref[b * N_EXPERTS + e]

        def make_copy(r0, rows):
            return pltpu.make_async_copy(stage_ref.at[slot, pl.ds(r0, rows), :],
                                         rows_hbm.at[pl.ds(pl.multiple_of(off + r0, SEG_ALIGN), rows), :],
                                         sem.at[slot])
        _segment_copies(seg_ref[b * N_EXPERTS + e], make_copy, action)

    def expert(e, carry):
        bounds, win, all_small = _block_windows(tbl_ref, b, e, SUBLANES)
        n_tiles = lax.shift_right_logical(bounds[-1] + (tr - 1), tr.bit_length() - 1)

        def clear(r, c):
            acc_ref[pl.ds(pl.multiple_of(r * tr, tr), tr), :] = jnp.zeros((tr, D_MODEL), F32)
            return c
        lax.fori_loop(0, n_tiles + 2, clear, 0)

        def gather(height):
            slot_id = lax.broadcasted_iota(jnp.int32, (height, tr), 0).astype(F32)
            for tb in range(TOK_BLOCKS):
                ts = slice(tb * tr, (tb + 1) * tr)
                local = post_ref[pl.ds(e, 1), ts] - win[tb].astype(F32)
                sel = jnp.where(local == slot_id, 1.0, 0.0).astype(BF16)
                acc_ref[pl.ds(win[tb], height), :] += _dot(sel, h_ref[ts, :])

        @pl.when(all_small)
        def _():
            gather(SMALL_WINDOW)

        @pl.when(jnp.logical_not(all_small))
        def _():
            gather(tr + SPILL_ROWS)

        @pl.when(e >= 2)
        def _():
            copies(e - 2, "wait")

        def to_stage(r, c):
            rows = pl.ds(pl.multiple_of(r * tr, tr), tr)
            stage_ref[e % 2, rows, :] = acc_ref[rows, :].astype(BF16)
            return c
        lax.fori_loop(0, n_tiles, to_stage, 0)
        copies(e, "start")
        return carry
    lax.fori_loop(0, N_EXPERTS, expert, 0)
    copies(N_EXPERTS - 2, "wait")
    copies(N_EXPERTS - 1, "wait")


def _dispatch(h, post, bounds, seg, seg_off, total_rows):
    tokens = h.shape[0]
    grid_spec = pltpu.PrefetchScalarGridSpec(
        num_scalar_prefetch=3,
        grid=(tokens // SEQ,),
        in_specs=[pl.BlockSpec((SEQ, D_MODEL), lambda b, *_: (b, 0)),
                  pl.BlockSpec((N_EXPERTS, SEQ), lambda b, *_: (0, b))],
        out_specs=pl.BlockSpec(memory_space=pl.ANY),
        scratch_shapes=[pltpu.VMEM((SEQ + 3 * TR_MOE, D_MODEL), F32),
                        pltpu.VMEM((2, STAGE_ROWS, D_MODEL), BF16),
                        pltpu.SemaphoreType.DMA((2,))],
    )
    return pl.pallas_call(
        _dispatch_kernel,
        out_shape=jax.ShapeDtypeStruct((total_rows, D_MODEL), BF16),
        grid_spec=grid_spec,
        compiler_params=_cparams(("arbitrary",)),
        name="expert_dispatch",
    )(bounds, seg, seg_off, h, post)


def _experts_kernel(ge_ref, rows_ref, hs_ref, w1_ref, w3_ref, w2_ref, ys_ref, acc_ref, wb1_ref, wb3_ref, wb2_ref):
    g, f = pl.program_id(0), pl.program_id(1)
    last_f = pl.num_programs(1) - 1
    tr = TR_MOE
    rows = rows_ref[g]
    n_big = lax.shift_right_logical(rows, tr.bit_length())
    n_full = lax.shift_right_logical(rows, tr.bit_length() - 1)
    rest = rows - n_full * tr
    n_tiles = lax.shift_right_logical(rows + (tr - 1), tr.bit_length() - 1)
    tail0 = pl.multiple_of(n_full * tr, tr)

    @pl.when(rows > 0)
    def _():
        wb1_ref[...] = w1_ref[...].astype(BF16)
        wb3_ref[...] = w3_ref[...].astype(BF16)
        wb2_ref[...] = w2_ref[...].astype(BF16)

    @pl.when(f == 0)
    def _():
        def clear(r, carry):
            acc_ref[pl.ds(pl.multiple_of(r * tr, tr), tr), :] = jnp.zeros((tr, D_MODEL), F32)
            return carry
        lax.fori_loop(0, n_tiles, clear, 0)

    def swiglu(hr, r0, m):
        z = (_silu(_dot(hr, wb1_ref[...])) * _dot(hr, wb3_ref[...])).astype(BF16)
        acc_ref[pl.ds(r0, m), :] += _dot(z, wb2_ref[...])

    def big_tile(r, carry):
        r0 = pl.multiple_of(r * 2 * tr, 2 * tr)
        swiglu(hs_ref[pl.ds(r0, 2 * tr), :], r0, 2 * tr)
        return carry
    lax.fori_loop(0, n_big, big_tile, 0)

    @pl.when(n_full > 2 * n_big)
    def _():
        r0 = pl.multiple_of(n_big * 2 * tr, tr)
        swiglu(hs_ref[pl.ds(r0, tr), :], r0, tr)
    lo = 0
    for m in TAIL_TILES:
        @pl.when((rest > lo) & (rest <= m))
        def _():
            valid = lax.broadcasted_iota(jnp.int32, (m, D_MODEL), 0) < rest
            hr = hs_ref[pl.ds(tail0, m), :]
            swiglu(jnp.where(valid, hr, jnp.zeros_like(hr)), tail0, m)
        lo = m

    @pl.when(f == last_f)
    def _():
        ys_ref[...] = jnp.zeros_like(ys_ref)

        def store(r, carry):
            rr = pl.ds(pl.multiple_of(r * tr, tr), tr)
            ys_ref[rr, :] = acc_ref[rr, :].astype(BF16)
            return carry
        lax.fori_loop(0, n_tiles, store, 0)


def _experts(hs, g_expert, g_rows, n_groups, w1, w3, w2):
    d_ff = w1.shape[2]
    tf = TF_MOE
    n_f = d_ff // tf
    chunk = lambda g, f, gr: jnp.where(gr[g] > 0, f, n_f - 1)
    grid_spec = pltpu.PrefetchScalarGridSpec(
        num_scalar_prefetch=2,
        grid=(n_groups, n_f),
        in_specs=[
            pl.BlockSpec((GROUP_ROWS, D_MODEL), lambda g, f, ge, gr: (g, 0)),
            pl.BlockSpec((None, D_MODEL, tf), lambda g, f, ge, gr: (ge[g], 0, chunk(g, f, gr))),
            pl.BlockSpec((None, D_MODEL, tf), lambda g, f, ge, gr: (ge[g], 0, chunk(g, f, gr))),
            pl.BlockSpec((None, tf, D_MODEL), lambda g, f, ge, gr: (ge[g], chunk(g, f, gr), 0)),
        ],
        out_specs=pl.BlockSpec((GROUP_ROWS, D_MODEL), lambda g, f, ge, gr: (g, 0)),
        scratch_shapes=[pltpu.VMEM((GROUP_ROWS, D_MODEL), F32),
                        pltpu.VMEM((D_MODEL, tf), BF16), pltpu.VMEM((D_MODEL, tf), BF16),
                        pltpu.VMEM((tf, D_MODEL), BF16)],
    )
    return pl.pallas_call(
        _experts_kernel,
        out_shape=jax.ShapeDtypeStruct(hs.shape, BF16),
        grid_spec=grid_spec,
        compiler_params=_cparams(("arbitrary", "arbitrary")),
        name="expert_swiglu",
    )(g_expert, g_rows, hs, w1, w3, w2)


def _combine_kernel(tbl_ref, seg_ref, off_ref, rows_hbm, pos_ref, gates_ref, x_ref, mod_ref, out_ref, buf_ref, sem):
    b = pl.program_id(0)
    n_seq = pl.num_programs(0)
    tr = TR_MOE

    def all_segments(seq, action):
        def body(e, first):
            off = off_ref[seq * N_EXPERTS + e]
            seg = seg_ref[seq * N_EXPERTS + e]

            def make_copy(r0, rows):
                return pltpu.make_async_copy(
                    rows_hbm.at[pl.ds(pl.multiple_of(off + r0, SEG_ALIGN), rows), :],
                    buf_ref.at[seq % 2, pl.ds(pl.multiple_of(first + r0, SEG_ALIGN), rows), :],
                    sem.at[seq % 2, e])
            _segment_copies(seg, make_copy, action)
            return first + seg
        lax.fori_loop(0, N_EXPERTS, body, 0)

    @pl.when(b == 0)
    def _():
        buf_ref[...] = jnp.zeros_like(buf_ref)
        all_segments(b, "start")

    @pl.when(b + 1 < n_seq)
    def _():
        all_segments(b + 1, "start")

    out_ref[...] = x_ref[...]
    lane = lax.broadcasted_iota(jnp.int32, (tr, LANES), 1)
    layer_gate = mod_ref[5:6, :]
    all_segments(b, "wait")

    def expert(e, seg_start):
        bounds, win, all_small = _block_windows(tbl_ref, b, e, SEG_ALIGN)

        def scatter(tb, first, height):
            ts = slice(tb * tr, (tb + 1) * tr)
            pos_col = jnp.sum(jnp.where(lane == e, pos_ref[ts, :], 0.0), axis=-1, keepdims=True)
            gate_col = jnp.sum(jnp.where(lane == e, gates_ref[ts, :], 0.0), axis=-1, keepdims=True)
            slot_id = lax.broadcasted_iota(jnp.int32, (tr, height), 1).astype(F32) + float(first)
            sel = jnp.where(pos_col - win[tb].astype(F32) == slot_id, 1.0, 0.0).astype(BF16)
            rows = buf_ref[b % 2, pl.ds(pl.multiple_of(seg_start + win[tb] + first, SEG_ALIGN), height), :]
            out_ref[ts, :] += layer_gate * (gate_col * _dot(sel, rows))

        @pl.when(all_small)
        def _():
            for tb in range(TOK_BLOCKS):
                scatter(tb, 0, SMALL_WINDOW)

        @pl.when(jnp.logical_not(all_small))
        def _():
            for tb in range(TOK_BLOCKS):
                scatter(tb, 0, tr)
            for tb in range(TOK_BLOCKS):
                @pl.when(bounds[tb + 1] - win[tb] > tr)
                def _():
                    scatter(tb, tr, SPILL_ROWS)
        return seg_start + seg_ref[b * N_EXPERTS + e]
    lax.fori_loop(0, N_EXPERTS, expert, 0)


def _combine(ys, pos, gates, x, mod, bounds, seg, seg_off):
    tokens = x.shape[0]
    grid_spec = pltpu.PrefetchScalarGridSpec(
        num_scalar_prefetch=3,
        grid=(tokens // SEQ,),
        in_specs=[pl.BlockSpec(memory_space=pl.ANY),
                  pl.BlockSpec((SEQ, LANES), lambda b, *_: (b, 0)),
                  pl.BlockSpec((SEQ, LANES), lambda b, *_: (b, 0)),
                  pl.BlockSpec((SEQ, D_MODEL), lambda b, *_: (b, 0)),
                  pl.BlockSpec((None, 6, D_MODEL), lambda b, *_: (b, 0, 0))],
        out_specs=pl.BlockSpec((SEQ, D_MODEL), lambda b, *_: (b, 0)),
        scratch_shapes=[pltpu.VMEM((2, PACKED_ROWS, D_MODEL), BF16),
                        pltpu.SemaphoreType.DMA((2, N_EXPERTS))],
    )
    return pl.pallas_call(
        _combine_kernel,
        out_shape=jax.ShapeDtypeStruct((tokens, D_MODEL), F32),
        grid_spec=grid_spec,
        compiler_params=_cparams(("arbitrary",)),
        name="expert_combine",
    )(bounds, seg, seg_off, ys, pos, gates, x, mod)


def _token_mixer(x, mod, gain, w_in, q_gain, k_gain, ret_gain, w_out, bias_masks):
    *qkv, ret_in = _in_proj(x, mod, gain, w_in, q_gain, k_gain)
    attn = _dilated_attention(qkv, bias_masks)
    ret = _retention(ret_in, ret_gain)
    return _out_proj(attn, ret, x, mod, w_out)


def _moe_ffn(x, mod, gain, w_router, w1, w3, w2):
    h, gates, pos, post, start, cnt = _router(x, mod, gain, w_router)
    bounds, seg, seg_off, g_expert, g_rows, n_groups = _moe_plan(start, cnt)
    hs = _dispatch(h, post, bounds, seg, seg_off, n_groups * GROUP_ROWS)
    ys = _experts(hs, g_expert, g_rows, n_groups, w1, w3, w2)
    return _combine(ys, pos, gates, x, mod, bounds, seg, seg_off)


def kernel(x, c, rel_bias_table, norm_mix, norm_ffn, w_mod, b_mod, w_in, q_gain, k_gain, ret_gain, w_out,
           ffn_w1, ffn_w3, ffn_w2, moe_router, moe_w1, moe_w3, moe_w2):
    batch, seq, d_model = x.shape
    assert (seq, d_model) == (SEQ, D_MODEL)
    depth = w_mod.shape[0]
    mods = _modulation(c, w_mod, b_mod).reshape(depth, batch, 6, D_MODEL)
    bias_masks = _bias_masks(rel_bias_table)
    xt = x.reshape(batch * seq, d_model)
    for layer in range(depth):
        mod = mods[layer]
        xt = _token_mixer(xt, mod, norm_mix[layer], w_in[layer], q_gain[layer], k_gain[layer],
                          ret_gain[layer], w_out[layer], bias_masks)
        i = layer // 2
        if layer % 2 == 0:
            xt = _dense_ffn(xt, mod, norm_ffn[layer], ffn_w1[i], ffn_w3[i], ffn_w2[i])
        else:
            xt = _moe_ffn(xt, mod, norm_ffn[layer], moe_router[i], moe_w1[i], moe_w3[i], moe_w2[i])
    return xt.reshape(batch, seq, d_model)
```

```python
import functools
import math

import jax
import jax.numpy as jnp
import numpy as np
from jax import lax
from jax.experimental import pallas as pl
from jax.experimental.pallas import tpu as pltpu

D_MODEL = 1024
SEQ = 2048
ATTN_HEADS = 8
ATTN_HEAD_DIM = 64
ATTN_WIDTH = ATTN_HEADS * ATTN_HEAD_DIM
DILATED_PATTERNS = ((128, 1), (512, 4), (2048, 16))
BLOCK = 128
NUM_BUCKETS = 32
MAX_DISTANCE = 2048
RET_HEADS = 4
RET_KEY_DIM = 64
RET_VALUE_DIM = 128
RET_WIDTH = RET_HEADS * RET_VALUE_DIM
RET_QK_WIDTH = RET_HEADS * RET_KEY_DIM
RET_CHUNK = 128
ROPE_BASE = 10000.0
IN_WIDTH = 3 * ATTN_WIDTH + 2 * RET_QK_WIDTH + 2 * RET_WIDTH
RET_IN_WIDTH = IN_WIDTH - 3 * ATTN_WIDTH
N_EXPERTS = 8
EPS = 1e-6
NEG_INF = -1e30

LANES = 128
VMEM_LIMIT = 60 * 1024 * 1024

BF16 = jnp.bfloat16
F32 = jnp.float32

TM_PROJ = 256
TM_FFN = 512
MXU_COLS = 256
TR_MOE = 256
TF_MOE = 512
assert TM_PROJ == TR_MOE
RET_ROWS = 512
SUBLANES = 8


def _cparams(sem):
    return pltpu.CompilerParams(dimension_semantics=sem, vmem_limit_bytes=VMEM_LIMIT)


def _dot(a, b):
    return jnp.dot(a, b, preferred_element_type=F32)


def _dot_nt(a, b):
    return lax.dot_general(a, b, (((1,), (1,)), ((), ())), preferred_element_type=F32)


def _dot_tn(a, b):
    return lax.dot_general(a, b, (((0,), (0,)), ((), ())), preferred_element_type=F32)


def _split_bf16(v):
    hi = v.astype(BF16)
    lo = (v - hi.astype(F32)).astype(BF16)
    return hi, lo


def _silu(v):
    return v * (1.0 / (1.0 + jnp.exp(-v)))


def _modulated_norm(x, gain, scale, shift):
    ms = jnp.mean(x * x, axis=-1, keepdims=True)
    y = x * lax.rsqrt(ms + EPS) * gain
    return y * (1.0 + scale) + shift


def _mod_kernel(c_ref, w_ref, b_ref, o_ref):
    ca = _silu(c_ref[...]).astype(BF16)
    o_ref[...] = _dot(ca, w_ref[...].astype(BF16)) + b_ref[...]


def _modulation(c, w_mod, b_mod):
    depth, _, width = w_mod.shape
    batch = c.shape[0]
    tn = 1536
    return pl.pallas_call(
        _mod_kernel,
        out_shape=jax.ShapeDtypeStruct((depth, batch, width), F32),
        grid=(depth, width // tn),
        in_specs=[
            pl.BlockSpec((batch, D_MODEL), lambda l, n: (0, 0)),
            pl.BlockSpec((None, D_MODEL, tn), lambda l, n: (l, 0, n)),
            pl.BlockSpec((None, 1, tn), lambda l, n: (l, 0, n)),
        ],
        out_specs=pl.BlockSpec((None, batch, tn), lambda l, n: (l, 0, n)),
        compiler_params=_cparams(("arbitrary", "arbitrary")),
        name="adaln_modulation",
    )(c, w_mod, b_mod.reshape(depth, 1, width))


def _bias_kernel(table_ref, bucket_ref, o_ref):
    h = pl.program_id(1)
    bucket = bucket_ref[...]
    acc = jnp.full(bucket.shape, NEG_INF, F32)
    for b in range(NUM_BUCKETS):
        acc = jnp.where(bucket == b, table_ref[b, h], acc)
    o_ref[...] = acc


BIAS_FULL = {1: 0, 4: 2}
BIAS_FIRST = {1: 1, 4: 3, 16: 4}
N_BIAS_SETS = 5


def _bias_masks(rel_bias_table):
    i = jnp.arange(BLOCK)[:, None]
    j = jnp.arange(2 * BLOCK)[None, :]
    max_exact = NUM_BUCKETS // 2

    def bucket_of(rel, dilation, w_sub, exists):
        n = jnp.maximum(rel * dilation, 0)
        nf = jnp.maximum(n.astype(F32), float(max_exact))
        large = max_exact + (jnp.log(nf / max_exact) / math.log(MAX_DISTANCE / max_exact)
                             * (NUM_BUCKETS - max_exact)).astype(jnp.int32)
        large = jnp.minimum(large, NUM_BUCKETS - 1)
        bucket = jnp.where(n < max_exact, n, large)
        allowed = (rel >= 0) & (rel <= w_sub) & exists
        return jnp.where(allowed, bucket, -1)

    sets = [None] * N_BIAS_SETS
    for window, dilation in DILATED_PATTERNS:
        w_sub = window // dilation
        if dilation in BIAS_FULL:
            sets[BIAS_FULL[dilation]] = bucket_of(i - j + BLOCK, dilation, w_sub, j >= 0)
        sets[BIAS_FIRST[dilation]] = bucket_of(i - j, dilation, w_sub, j < BLOCK)
    buckets = jnp.stack(sets).astype(jnp.int32)
    return pl.pallas_call(
        _bias_kernel,
        out_shape=jax.ShapeDtypeStruct((N_BIAS_SETS, ATTN_HEADS // 2, 2 * BLOCK, 2 * BLOCK), F32),
        grid=(N_BIAS_SETS, ATTN_HEADS),
        in_specs=[
            pl.BlockSpec(memory_space=pltpu.SMEM),
            pl.BlockSpec((None, BLOCK, 2 * BLOCK), lambda p, h: (p, 0, 0)),
        ],
        out_specs=pl.BlockSpec((None, None, BLOCK, 2 * BLOCK), lambda p, h: (p, h // 2, h % 2, 0)),
        compiler_params=_cparams(("arbitrary", "arbitrary")),
        name="relative_bias_masks",
    )(rel_bias_table, buckets)


HALF = ATTN_WIDTH // 2


def _in_proj_kernel(x_ref, mod_ref, gain_ref, w_ref, qg_ref, kg_ref, grp_ref,
                    q1_ref, k1_ref, v1_ref, q4_ref, k4_ref, v4_ref, q16_ref, k16_ref, v16_ref, r_ref,
                    perm_ref):
    mod = mod_ref[...]
    h = _modulated_norm(x_ref[...], gain_ref[...], mod[1:2], mod[0:1]).astype(BF16)
    proj = _dot(h, w_ref[...].astype(BF16))
    grp = grp_ref[...]
    tm = proj.shape[0]

    def head_norm(t, gain):
        hi, lo = _split_bf16(t * t)
        ss = _dot(hi, grp) + _dot(lo, grp)
        return t * lax.rsqrt(ss * (1.0 / ATTN_HEAD_DIM) + EPS) * gain

    def emit(t, o1_ref, o4_ref, o16_ref):
        o1_ref[...] = t.astype(BF16)
        for j in range(ATTN_WIDTH // LANES):
            perm_ref[j] = t[:, j * LANES:(j + 1) * LANES]
        for dil, o_ref in ((4, o4_ref), (16, o16_ref)):
            for hf in range(2):
                for r in range(dil):
                    for jj in range(HALF // LANES):
                        o_ref[hf, r, :, jj * LANES:(jj + 1) * LANES] = perm_ref[
                            hf * (HALF // LANES) + jj, pl.ds(r, tm // dil, stride=dil), :].astype(BF16)

    emit(head_norm(proj[:, :ATTN_WIDTH], qg_ref[...]) * (ATTN_HEAD_DIM ** -0.5), q1_ref, q4_ref, q16_ref)
    emit(head_norm(proj[:, ATTN_WIDTH:2 * ATTN_WIDTH], kg_ref[...]), k1_ref, k4_ref, k16_ref)
    emit(proj[:, 2 * ATTN_WIDTH:3 * ATTN_WIDTH], v1_ref, v4_ref, v16_ref)
    r_ref[...] = proj[:, 3 * ATTN_WIDTH:]


def _in_proj(x, mod, gain, w_in, q_gain, k_gain):
    tokens = x.shape[0]
    tm = TM_PROJ
    per_seq = SEQ // tm
    grp = np.kron(np.eye(ATTN_HEADS), np.ones((ATTN_HEAD_DIM, ATTN_HEAD_DIM))).astype(np.float32)
    row = lambda i: (i, 0)
    const = lambda i: (0, 0)
    layouts = [((jax.ShapeDtypeStruct((tokens, ATTN_WIDTH), BF16),) * 3,
                (pl.BlockSpec((tm, ATTN_WIDTH), row),) * 3)]
    for dil in (4, 16):
        shape = jax.ShapeDtypeStruct((2, dil, tokens // dil, HALF), BF16)
        spec = pl.BlockSpec((2, dil, tm // dil, HALF), lambda i: (0, 0, i, 0))
        layouts.append(((shape,) * 3, (spec,) * 3))
    out_shape = sum((s for s, _ in layouts), ()) + (jax.ShapeDtypeStruct((tokens, RET_IN_WIDTH), F32),)
    out_specs = sum((s for _, s in layouts), ()) + (pl.BlockSpec((tm, RET_IN_WIDTH), row),)
    return pl.pallas_call(
        _in_proj_kernel,
        out_shape=out_shape,
        grid=(tokens // tm,),
        in_specs=[
            pl.BlockSpec((tm, D_MODEL), row),
            pl.BlockSpec((None, 6, D_MODEL), lambda i: (i // per_seq, 0, 0)),
            pl.BlockSpec((1, D_MODEL), const),
            pl.BlockSpec((D_MODEL, IN_WIDTH), const, pipeline_mode=pl.Buffered(1)),
            pl.BlockSpec((1, ATTN_WIDTH), const),
            pl.BlockSpec((1, ATTN_WIDTH), const),
            pl.BlockSpec((ATTN_WIDTH, ATTN_WIDTH), const),
        ],
        out_specs=out_specs,
        scratch_shapes=[pltpu.VMEM((ATTN_WIDTH // LANES, tm, LANES), F32)],
        compiler_params=_cparams(("arbitrary",)),
        name="in_projection",
    )(x, mod, gain.reshape(1, D_MODEL), w_in,
      jnp.tile(q_gain, ATTN_HEADS).reshape(1, ATTN_WIDTH),
      jnp.tile(k_gain, ATTN_HEADS).reshape(1, ATTN_WIDTH),
      jnp.asarray(grp, BF16))


PAIRS_PER_HALF = ATTN_HEADS // 4
GROUP = 4


def _pair_scores(qp, kp, vp, bias2, masks, low):
    q2 = jnp.concatenate([qp * masks[0], qp * masks[1]], axis=0)
    s = _dot_nt(q2, kp) + bias2
    m = jnp.max(s, axis=-1, keepdims=True)
    p = jnp.exp(s - m)
    den = jnp.sum(p, axis=-1, keepdims=True)
    pv = _dot(p.astype(BF16), vp)
    pick = lambda t: jnp.where(low, t[:BLOCK], t[BLOCK:])
    return pick(pv), pick(m), pick(den)


def _attn_kernel(q1_ref, k1_ref, v1_ref, q4_ref, k4_ref, v4_ref, q16_ref, k16_ref, v16_ref, bm_ref,
                 o_ref, acc_ref, max_ref, den_ref):
    lane = lax.broadcasted_iota(jnp.int32, (BLOCK, LANES), 1)
    low = lane < ATTN_HEAD_DIM
    masks = (jnp.where(low, 1.0, 0.0).astype(BF16), jnp.where(low, 0.0, 1.0).astype(BF16))

    def block(q_ref, k_ref, v_ref, sub, q0, w0, width, bias_set, rows, first):
        for p in range(PAIRS_PER_HALF):
            cs = slice(p * LANES, (p + 1) * LANES)
            acc, m, den = _pair_scores(q_ref[sub + (pl.ds(q0, BLOCK), cs)], k_ref[sub + (pl.ds(w0, width), cs)],
                                       v_ref[sub + (pl.ds(w0, width), cs)], bm_ref[bias_set, p, :, 0:width],
                                       masks, low)
            if not first:
                m_old = max_ref[p, rows, :]
                m_new = jnp.maximum(m_old, m)
                a, b = jnp.exp(m_old - m_new), jnp.exp(m - m_new)
                den = den_ref[p, rows, :] * a + den * b
                acc = acc_ref[p, rows, :] * a + acc * b
                m = m_new
            max_ref[p, rows, :] = m
            den_ref[p, rows, :] = den
            acc_ref[p, rows, :] = acc

    def d1_group(g, carry):
        for u in range(GROUP):
            n = g * GROUP + u
            q0 = pl.multiple_of(n * BLOCK, BLOCK)
            w0 = pl.multiple_of(jnp.maximum(n - 1, 0) * BLOCK, BLOCK)
            bias_set = jnp.where(n == 0, BIAS_FIRST[1], BIAS_FULL[1])
            block(q1_ref, k1_ref, v1_ref, (), q0, w0, 2 * BLOCK, bias_set, pl.ds(q0, BLOCK), True)
        return carry
    lax.fori_loop(0, SEQ // BLOCK // GROUP, d1_group, 0)

    def d4_residue(r, carry):
        for n in range(SEQ // 4 // BLOCK):
            block(q4_ref, k4_ref, v4_ref, (r,), n * BLOCK, max(n - 1, 0) * BLOCK, 2 * BLOCK,
                  BIAS_FIRST[4] if n == 0 else BIAS_FULL[4],
                  pl.ds(r + 4 * BLOCK * n, BLOCK, stride=4), False)
        return carry
    lax.fori_loop(0, 4, d4_residue, 0)

    def d16_group(g, carry):
        for u in range(GROUP):
            r = g * GROUP + u
            block(q16_ref, k16_ref, v16_ref, (r,), 0, 0, BLOCK, BIAS_FIRST[16],
                  pl.ds(r, BLOCK, stride=16), False)
        return carry
    lax.fori_loop(0, 16 // GROUP, d16_group, 0)

    def normalise(n, carry):
        rows = pl.ds(pl.multiple_of(n * BLOCK, BLOCK), BLOCK)
        for p in range(PAIRS_PER_HALF):
            o_ref[rows, p * LANES:(p + 1) * LANES] = (acc_ref[p, rows, :] / den_ref[p, rows, :]).astype(BF16)
        return carry
    lax.fori_loop(0, SEQ // BLOCK, normalise, 0)


def _dilated_attention(qkv, bias_masks):
    tokens = qkv[0].shape[0]
    batch = tokens // SEQ
    specs = [pl.BlockSpec((SEQ, HALF), lambda b, hf: (b, hf))] * 3
    for dil in (4, 16):
        specs += [pl.BlockSpec((None, dil, SEQ // dil, HALF), lambda b, hf: (hf, 0, b, 0))] * 3
    state = pltpu.VMEM((PAIRS_PER_HALF, SEQ, LANES), F32)
    return pl.pallas_call(
        _attn_kernel,
        out_shape=jax.ShapeDtypeStruct((tokens, ATTN_WIDTH), BF16),
        grid=(batch, 2),
        in_specs=specs + [pl.BlockSpec((N_BIAS_SETS, PAIRS_PER_HALF, 2 * BLOCK, 2 * BLOCK),
                                       lambda b, hf: (0, hf, 0, 0))],
        out_specs=pl.BlockSpec((SEQ, HALF), lambda b, hf: (b, hf)),
        scratch_shapes=[state, state, state],
        compiler_params=_cparams(("arbitrary", "arbitrary")),
        name="dilated_attention",
    )(*qkv, bias_masks)


def _retention_kernel(r_ref, cos_ref, sin_ref, dmask_ref, qdec_ref, kdec_ref, cdec_ref, gain_ref,
                      o_ref, state_ref):
    @pl.when(pl.program_id(1) == 0)
    def _():
        state_ref[...] = jnp.zeros_like(state_ref)

    lane = lax.broadcasted_iota(jnp.int32, (RET_CHUNK, LANES), 1)
    low = lane < RET_KEY_DIM
    first_half = (lane % RET_KEY_DIM) < (RET_KEY_DIM // 2)

    def rotate(t, cos, sin):
        partner = jnp.where(first_half, pltpu.roll(t, LANES - RET_KEY_DIM // 2, 1),
                            pltpu.roll(t, RET_KEY_DIM // 2, 1))
        return t * cos + partner * sin

    for c in range(RET_ROWS // RET_CHUNK):
        rows = slice(c * RET_CHUNK, (c + 1) * RET_CHUNK)
        for hp in range(RET_HEADS // 2):
            qs = slice(hp * LANES, (hp + 1) * LANES)
            ks = slice(RET_QK_WIDTH + hp * LANES, RET_QK_WIDTH + (hp + 1) * LANES)
            cos, sin = cos_ref[rows, qs], sin_ref[rows, qs]
            q_pair = rotate(r_ref[rows, qs], cos, sin) * (RET_KEY_DIM ** -0.5)
            k_pair = rotate(r_ref[rows, ks], cos, sin)
            for hh in range(2):
                head = 2 * hp + hh
                vs = slice(2 * RET_QK_WIDTH + head * LANES, 2 * RET_QK_WIDTH + (head + 1) * LANES)
                gs = slice(2 * RET_QK_WIDTH + RET_WIDTH + head * LANES,
                           2 * RET_QK_WIDTH + RET_WIDTH + (head + 1) * LANES)
                keep = low if hh == 0 else jnp.logical_not(low)
                qm = jnp.where(keep, q_pair, 0.0)
                vb = r_ref[rows, vs].astype(BF16)
                state = state_ref[head]
                inner = _dot_nt(qm.astype(BF16), k_pair.astype(BF16)) * dmask_ref[head]
                y = _dot(inner.astype(BF16), vb)
                y = y + _dot((qm * qdec_ref[head]).astype(BF16), state.astype(BF16))
                state_ref[head] = state * cdec_ref[head] + _dot_tn((k_pair * kdec_ref[head]).astype(BF16), vb)
                mu = jnp.mean(y, axis=-1, keepdims=True)
                yc = y - mu
                var = jnp.mean(yc * yc, axis=-1, keepdims=True)
                yn = yc * lax.rsqrt(var + EPS) * gain_ref[:, head * LANES:(head + 1) * LANES]
                o_ref[rows, head * LANES:(head + 1) * LANES] = (_silu(r_ref[rows, gs]) * yn).astype(BF16)


def _retention_tables():
    half = RET_KEY_DIM // 2
    pos = jnp.arange(SEQ, dtype=F32)
    inv = ROPE_BASE ** (-jnp.arange(half, dtype=F32) / half)
    ang = pos[:, None] * inv[None, :]
    cos, sin = jnp.cos(ang), jnp.sin(ang)
    cos_full = jnp.tile(jnp.concatenate([cos, cos], axis=-1), (1, RET_HEADS))
    sin_signed = jnp.tile(jnp.concatenate([-sin, sin], axis=-1), (1, RET_HEADS))
    log_g = jnp.log(1.0 - 2.0 ** (-5.0 - jnp.arange(RET_HEADS, dtype=F32)))
    idx = jnp.arange(RET_CHUNK, dtype=F32)
    diff = idx[:, None] - idx[None, :]
    dmask = jnp.where(diff >= 0, jnp.exp(jnp.maximum(diff, 0.0)[None] * log_g[:, None, None]), 0.0)
    q_decay = jnp.exp((idx + 1.0)[None, :] * log_g[:, None])[..., None]
    k_decay = jnp.exp((RET_CHUNK - 1.0 - idx)[None, :] * log_g[:, None])[..., None]
    chunk_decay = jnp.exp(RET_CHUNK * log_g)[:, None, None]
    full = (RET_HEADS, RET_CHUNK, LANES)
    return (cos_full, sin_signed, dmask, jnp.broadcast_to(q_decay, full),
            jnp.broadcast_to(k_decay, full), jnp.broadcast_to(chunk_decay, full))


def _retention(ret_in, ret_gain):
    tokens = ret_in.shape[0]
    batch = tokens // SEQ
    per_seq = SEQ // RET_ROWS
    cos, sin, dmask, qdec, kdec, cdec = _retention_tables()
    tab = pl.BlockSpec((RET_ROWS, RET_QK_WIDTH), lambda b, j: (j, 0))
    const3 = pl.BlockSpec((RET_HEADS, RET_CHUNK, LANES), lambda b, j: (0, 0, 0))
    return pl.pallas_call(
        _retention_kernel,
        out_shape=jax.ShapeDtypeStruct((tokens, RET_WIDTH), BF16),
        grid=(batch, per_seq),
        in_specs=[
            pl.BlockSpec((RET_ROWS, RET_IN_WIDTH), lambda b, j: (b * per_seq + j, 0)),
            tab, tab, const3, const3, const3, const3,
            pl.BlockSpec((1, RET_WIDTH), lambda b, j: (0, 0)),
        ],
        out_specs=pl.BlockSpec((RET_ROWS, RET_WIDTH), lambda b, j: (b * per_seq + j, 0)),
        scratch_shapes=[pltpu.VMEM((RET_HEADS, LANES, RET_VALUE_DIM), F32)],
        compiler_params=_cparams(("arbitrary", "arbitrary")),
        name="retention",
    )(ret_in, cos, sin, dmask, qdec, kdec, cdec, ret_gain.reshape(1, RET_WIDTH))


def _out_proj_kernel(attn_ref, ret_ref, x_ref, mod_ref, w_ref, out_ref):
    mix = (_dot(attn_ref[...], w_ref[:ATTN_WIDTH, :].astype(BF16))
           + _dot(ret_ref[...], w_ref[ATTN_WIDTH:, :].astype(BF16)))
    out_ref[...] = x_ref[...] + mod_ref[2:3, :] * mix


def _out_proj(attn, ret, x, mod, w_out):
    tokens = x.shape[0]
    tm = TM_FFN
    per_seq = SEQ // tm
    row = lambda i: (i, 0)
    return pl.pallas_call(
        _out_proj_kernel,
        out_shape=jax.ShapeDtypeStruct((tokens, D_MODEL), F32),
        grid=(tokens // tm,),
        in_specs=[pl.BlockSpec((tm, ATTN_WIDTH), row),
                  pl.BlockSpec((tm, RET_WIDTH), row),
                  pl.BlockSpec((tm, D_MODEL), row),
                  pl.BlockSpec((None, 6, D_MODEL), lambda i: (i // per_seq, 0, 0)),
                  pl.BlockSpec((D_MODEL, D_MODEL), lambda i: (0, 0))],
        out_specs=pl.BlockSpec((tm, D_MODEL), row),
        compiler_params=_cparams(("arbitrary",)),
        name="out_projection",
    )(attn, ret, x, mod, w_out)


def _swiglu_chunks(h, w1_ref, w3_ref, w2_ref):
    d_ff = w1_ref.shape[-1]
    total = None
    for c0 in range(0, d_ff, MXU_COLS):
        c1 = min(c0 + MXU_COLS, d_ff)
        z = (_silu(_dot(h, w1_ref[:, c0:c1].astype(BF16))) * _dot(h, w3_ref[:, c0:c1].astype(BF16))).astype(BF16)
        part = _dot(z, w2_ref[c0:c1, :].astype(BF16))
        total = part if total is None else total + part
    return total


def _ffn_kernel(x_ref, mod_ref, gain_ref, w1_ref, w3_ref, w2_ref, out_ref):
    mod = mod_ref[...]
    x = x_ref[...]
    h = _modulated_norm(x, gain_ref[...], mod[4:5], mod[3:4]).astype(BF16)
    out_ref[...] = x + mod[5:6, :] * _swiglu_chunks(h, w1_ref, w3_ref, w2_ref)


def _dense_ffn(x, mod, gain, w1, w3, w2):
    tokens = x.shape[0]
    d_ff = w1.shape[1]
    tm = TM_FFN
    per_seq = SEQ // tm
    resident = pl.Buffered(1)
    return pl.pallas_call(
        _ffn_kernel,
        out_shape=jax.ShapeDtypeStruct((tokens, D_MODEL), F32),
        grid=(tokens // tm,),
        in_specs=[
            pl.BlockSpec((tm, D_MODEL), lambda i: (i, 0)),
            pl.BlockSpec((None, 6, D_MODEL), lambda i: (i // per_seq, 0, 0)),
            pl.BlockSpec((1, D_MODEL), lambda i: (0, 0)),
            pl.BlockSpec((D_MODEL, d_ff), lambda i: (0, 0), pipeline_mode=resident),
            pl.BlockSpec((D_MODEL, d_ff), lambda i: (0, 0), pipeline_mode=resident),
            pl.BlockSpec((d_ff, D_MODEL), lambda i: (0, 0), pipeline_mode=resident),
        ],
        out_specs=pl.BlockSpec((tm, D_MODEL), lambda i: (i, 0)),
        compiler_params=_cparams(("arbitrary",)),
        name="dense_swiglu",
    )(x, mod, gain.reshape(1, D_MODEL), w1, w3, w2)


def _router_kernel(x_ref, mod_ref, gain_ref, wr_ref, h_ref, gates_ref, pos_ref, post_ref, start_ref,
                   cnt_ref, carry_ref, *, tiles_per_seq):
    i = pl.program_id(0)

    @pl.when(i % tiles_per_seq == 0)
    def _():
        carry_ref[...] = jnp.zeros_like(carry_ref)

    mod = mod_ref[...]
    h = _modulated_norm(x_ref[...], gain_ref[...], mod[4:5], mod[3:4]).astype(BF16)
    h_ref[...] = h
    tm = h.shape[0]
    lane = lax.broadcasted_iota(jnp.int32, (tm, LANES), 1).astype(F32)
    logits = jnp.where(lane < N_EXPERTS, _dot(h, wr_ref[...]), -jnp.inf)
    m1 = jnp.max(logits, axis=-1, keepdims=True)
    i1 = jnp.min(jnp.where(logits == m1, lane, float(LANES)), axis=-1, keepdims=True)
    rest = jnp.where(lane == i1, -jnp.inf, logits)
    m2 = jnp.max(rest, axis=-1, keepdims=True)
    i2 = jnp.min(jnp.where(rest == m2, lane, float(LANES)), axis=-1, keepdims=True)
    e2 = jnp.exp(m2 - m1)
    g1 = 1.0 / (1.0 + e2)
    g2 = e2 / (1.0 + e2)
    gates_ref[...] = jnp.where(lane == i1, g1, 0.0) + jnp.where(lane == i2, g2, 0.0)
    chosen = (lane == i1) | (lane == i2)
    onehot = jnp.where(chosen, 1.0, 0.0)
    r = lax.broadcasted_iota(jnp.int32, (tm, tm), 0)
    c = lax.broadcasted_iota(jnp.int32, (tm, tm), 1)
    tril = jnp.where(c <= r, 1.0, 0.0).astype(BF16)
    incl = _dot(tril, onehot.astype(BF16))
    carry = carry_ref[0:1, :]
    start_ref[...] = carry_ref[...]
    pos = jnp.where(chosen, incl - 1.0 + carry, -1.0)
    pos_ref[...] = pos
    post_ref[...] = pos.T[:N_EXPERTS, :]
    total = carry + incl[tm - 1:tm, :]
    carry_ref[...] = jnp.broadcast_to(total, carry_ref.shape)
    cnt_ref[...] = jnp.broadcast_to(total, cnt_ref.shape)


def _router(x, mod, gain, w_router):
    tokens = x.shape[0]
    batch = tokens // SEQ
    tm = TM_PROJ
    per_seq = SEQ // tm
    wr = jnp.zeros((D_MODEL, LANES), BF16).at[:, :N_EXPERTS].set(w_router.astype(BF16))
    row = lambda i: (i, 0)
    return pl.pallas_call(
        functools.partial(_router_kernel, tiles_per_seq=per_seq),
        out_shape=(
            jax.ShapeDtypeStruct((tokens, D_MODEL), BF16),
            jax.ShapeDtypeStruct((tokens, LANES), F32),
            jax.ShapeDtypeStruct((tokens, LANES), F32),
            jax.ShapeDtypeStruct((N_EXPERTS, tokens), F32),
            jax.ShapeDtypeStruct((tokens // tm, 8, LANES), F32),
            jax.ShapeDtypeStruct((batch, 8, LANES), F32),
        ),
        grid=(tokens // tm,),
        in_specs=[
            pl.BlockSpec((tm, D_MODEL), row),
            pl.BlockSpec((None, 6, D_MODEL), lambda i: (i // per_seq, 0, 0)),
            pl.BlockSpec((1, D_MODEL), lambda i: (0, 0)),
            pl.BlockSpec((D_MODEL, LANES), lambda i: (0, 0)),
        ],
        out_specs=(
            pl.BlockSpec((tm, D_MODEL), row),
            pl.BlockSpec((tm, LANES), row),
            pl.BlockSpec((tm, LANES), row),
            pl.BlockSpec((N_EXPERTS, tm), lambda i: (0, i)),
            pl.BlockSpec((None, 8, LANES), lambda i: (i, 0, 0)),
            pl.BlockSpec((None, 8, LANES), lambda i: (i // per_seq, 0, 0)),
        ),
        scratch_shapes=[pltpu.VMEM((8, LANES), F32)],
        compiler_params=_cparams(("arbitrary",)),
        name="expert_router",
    )(x, mod, gain.reshape(1, D_MODEL), wr)


TOK_BLOCKS = SEQ // TR_MOE
TAIL_TILES = (64, 128, TR_MOE)
SPILL_ROWS = 16
SMALL_WINDOW = 128
SEG_ALIGN = 16
GROUP_ROWS = 3072
PACKED_ROWS = 2 * SEQ + N_EXPERTS * SEG_ALIGN + 2 * TR_MOE
BIG_CHUNK = 128
STAGE_ROWS = SEQ + TR_MOE
N_BOUNDS = TOK_BLOCKS + 1


def _moe_plan(start, cnt):
    batch = cnt.shape[0]
    counts = cnt[:, 0, :N_EXPERTS].astype(jnp.int32)
    bounds = jnp.concatenate([start[:, 0, :N_EXPERTS].reshape(batch, TOK_BLOCKS, N_EXPERTS).astype(jnp.int32),
                              counts[:, None, :]], axis=1)
    seg = (counts + (SEG_ALIGN - 1)) // SEG_ALIGN * SEG_ALIGN
    rows_e = jnp.sum(seg, axis=0)
    groups_e = (rows_e + (GROUP_ROWS - 1)) // GROUP_ROWS
    first_group = jnp.cumsum(groups_e) - groups_e
    seg_off = first_group[None, :] * GROUP_ROWS + jnp.cumsum(seg, axis=0) - seg
    max_rows = batch * (2 * SEQ + N_EXPERTS * (SEG_ALIGN - 1))
    n_groups = max_rows // GROUP_ROWS + N_EXPERTS
    g = jnp.arange(n_groups)
    g_expert = jnp.minimum(jnp.sum(g[:, None] >= jnp.cumsum(groups_e)[None, :], axis=1), N_EXPERTS - 1)
    g_rows = jnp.clip(rows_e[g_expert] - (g - first_group[g_expert]) * GROUP_ROWS, 0, GROUP_ROWS)
    g_rows = jnp.where(g < jnp.sum(groups_e), g_rows, 0)
    return (bounds.reshape(-1), seg.reshape(-1), seg_off.reshape(-1),
            g_expert.astype(jnp.int32), g_rows.astype(jnp.int32), n_groups)


def _segment_copies(seg, make_copy, action):
    n_big = lax.shift_right_logical(seg, BIG_CHUNK.bit_length() - 1)

    def big(i, carry):
        getattr(make_copy(pl.multiple_of(i * BIG_CHUNK, BIG_CHUNK), BIG_CHUNK), action)()
        return carry
    lax.fori_loop(0, n_big, big, 0)
    rest0 = n_big * BIG_CHUNK
    n_small = lax.shift_right_logical(seg - rest0, SEG_ALIGN.bit_length() - 1)

    def small(i, carry):
        getattr(make_copy(pl.multiple_of(rest0 + i * SEG_ALIGN, SEG_ALIGN), SEG_ALIGN), action)()
        return carry
    lax.fori_loop(0, n_small, small, 0)


def _block_windows(tbl_ref, b, e, align):
    base = b * N_BOUNDS * N_EXPERTS + e
    bounds = [tbl_ref[base + tb * N_EXPERTS] for tb in range(N_BOUNDS)]
    shift = align.bit_length() - 1
    win = [pl.multiple_of(lax.shift_left(lax.shift_right_logical(s, shift), shift), align) for s in bounds[:-1]]
    all_small = functools.reduce(jnp.logical_and,
                                 [bounds[tb + 1] - win[tb] <= SMALL_WINDOW for tb in range(TOK_BLOCKS)])
    return bounds, win, all_small


def _dispatch_kernel(tbl_ref, seg_ref, off_ref, h_ref, post_ref, rows_hbm, acc_ref, stage_ref, sem):
    b = pl.program_id(0)
    tr = TR_MOE

    def copies(e, action):
        slot = e % 2
        off = off_ref[b * N_EXPERTS + e]

        def make_copy(r0, rows):
            return pltpu.make_async_copy(stage_ref.at[slot, pl.ds(r0, rows), :],
                                         rows_hbm.at[pl.ds(pl.multiple_of(off + r0, SEG_ALIGN), rows), :],
                                         sem.at[slot])
        _segment_copies(seg_ref[b * N_EXPERTS + e], make_copy, action)

    def expert(e, carry):
        bounds, win, all_small = _block_windows(tbl_ref, b, e, SUBLANES)
        n_tiles = lax.shift_right_logical(bounds[-1] + (tr - 1), tr.bit_length() - 1)

        def clear(r, c):
            acc_ref[pl.ds(pl.multiple_of(r * tr, tr), tr), :] = jnp.zeros((tr, D_MODEL), F32)
            return c
        lax.fori_loop(0, n_tiles + 2, clear, 0)

        def gather(height):
            slot_id = lax.broadcasted_iota(jnp.int32, (height, tr), 0).astype(F32)
            for tb in range(TOK_BLOCKS):
                ts = slice(tb * tr, (tb + 1) * tr)
                local = post_ref[pl.ds(e, 1), ts] - win[tb].astype(F32)
                sel = jnp.where(local == slot_id, 1.0, 0.0).astype(BF16)
                acc_ref[pl.ds(win[tb], height), :] += _dot(sel, h_ref[ts, :])

        @pl.when(all_small)
        def _():
            gather(SMALL_WINDOW)

        @pl.when(jnp.logical_not(all_small))
        def _():
            gather(tr + SPILL_ROWS)

        @pl.when(e >= 2)
        def _():
            copies(e - 2, "wait")

        def to_stage(r, c):
            rows = pl.ds(pl.multiple_of(r * tr, tr), tr)
            stage_ref[e % 2, rows, :] = acc_ref[rows, :].astype(BF16)
            return c
        lax.fori_loop(0, n_tiles, to_stage, 0)
        copies(e, "start")
        return carry
    lax.fori_loop(0, N_EXPERTS, expert, 0)
    copies(N_EXPERTS - 2, "wait")
    copies(N_EXPERTS - 1, "wait")


def _dispatch(h, post, bounds, seg, seg_off, total_rows):
    tokens = h.shape[0]
    grid_spec = pltpu.PrefetchScalarGridSpec(
        num_scalar_prefetch=3,
        grid=(tokens // SEQ,),
        in_specs=[pl.BlockSpec((SEQ, D_MODEL), lambda b, *_: (b, 0)),
                  pl.BlockSpec((N_EXPERTS, SEQ), lambda b, *_: (0, b))],
        out_specs=pl.BlockSpec(memory_space=pl.ANY),
        scratch_shapes=[pltpu.VMEM((SEQ + 3 * TR_MOE, D_MODEL), F32),
                        pltpu.VMEM((2, STAGE_ROWS, D_MODEL), BF16),
                        pltpu.SemaphoreType.DMA((2,))],
    )
    return pl.pallas_call(
        _dispatch_kernel,
        out_shape=jax.ShapeDtypeStruct((total_rows, D_MODEL), BF16),
        grid_spec=grid_spec,
        compiler_params=_cparams(("arbitrary",)),
        name="expert_dispatch",
    )(bounds, seg, seg_off, h, post)


def _experts_kernel(ge_ref, rows_ref, hs_ref, w1_ref, w3_ref, w2_ref, ys_ref, acc_ref, wb1_ref, wb3_ref, wb2_ref):
    g, f = pl.program_id(0), pl.program_id(1)
    last_f = pl.num_programs(1) - 1
    tr = TR_MOE
    rows = rows_ref[g]
    n_big = lax.shift_right_logical(rows, tr.bit_length())
    n_full = lax.shift_right_logical(rows, tr.bit_length() - 1)
    rest = rows - n_full * tr
    n_tiles = lax.shift_right_logical(rows + (tr - 1), tr.bit_length() - 1)
    tail0 = pl.multiple_of(n_full * tr, tr)

    @pl.when(rows > 0)
    def _():
        wb1_ref[...] = w1_ref[...].astype(BF16)
        wb3_ref[...] = w3_ref[...].astype(BF16)
        wb2_ref[...] = w2_ref[...].astype(BF16)

    @pl.when(f == 0)
    def _():
        def clear(r, carry):
            acc_ref[pl.ds(pl.multiple_of(r * tr, tr), tr), :] = jnp.zeros((tr, D_MODEL), F32)
            return carry
        lax.fori_loop(0, n_tiles, clear, 0)

    def swiglu(hr, r0, m):
        z = (_silu(_dot(hr, wb1_ref[...])) * _dot(hr, wb3_ref[...])).astype(BF16)
        acc_ref[pl.ds(r0, m), :] += _dot(z, wb2_ref[...])

    def big_tile(r, carry):
        r0 = pl.multiple_of(r * 2 * tr, 2 * tr)
        swiglu(hs_ref[pl.ds(r0, 2 * tr), :], r0, 2 * tr)
        return carry
    lax.fori_loop(0, n_big, big_tile, 0)

    @pl.when(n_full > 2 * n_big)
    def _():
        r0 = pl.multiple_of(n_big * 2 * tr, tr)
        swiglu(hs_ref[pl.ds(r0, tr), :], r0, tr)
    lo = 0
    for m in TAIL_TILES:
        @pl.when((rest > lo) & (rest <= m))
        def _():
            valid = lax.broadcasted_iota(jnp.int32, (m, D_MODEL), 0) < rest
            hr = hs_ref[pl.ds(tail0, m), :]
            swiglu(jnp.where(valid, hr, jnp.zeros_like(hr)), tail0, m)
        lo = m

    @pl.when(f == last_f)
    def _():
        ys_ref[...] = jnp.zeros_like(ys_ref)

        def store(r, carry):
            rr = pl.ds(pl.multiple_of(r * tr, tr), tr)
            ys_ref[rr, :] = acc_ref[rr, :].astype(BF16)
            return carry
        lax.fori_loop(0, n_tiles, store, 0)


def _experts(hs, g_expert, g_rows, n_groups, w1, w3, w2):
    d_ff = w1.shape[2]
    tf = TF_MOE
    n_f = d_ff // tf
    chunk = lambda g, f, gr: jnp.where(gr[g] > 0, f, n_f - 1)
    grid_spec = pltpu.PrefetchScalarGridSpec(
        num_scalar_prefetch=2,
        grid=(n_groups, n_f),
        in_specs=[
            pl.BlockSpec((GROUP_ROWS, D_MODEL), lambda g, f, ge, gr: (g, 0)),
            pl.BlockSpec((None, D_MODEL, tf), lambda g, f, ge, gr: (ge[g], 0, chunk(g, f, gr))),
            pl.BlockSpec((None, D_MODEL, tf), lambda g, f, ge, gr: (ge[g], 0, chunk(g, f, gr))),
            pl.BlockSpec((None, tf, D_MODEL), lambda g, f, ge, gr: (ge[g], chunk(g, f, gr), 0)),
        ],
        out_specs=pl.BlockSpec((GROUP_ROWS, D_MODEL), lambda g, f, ge, gr: (g, 0)),
        scratch_shapes=[pltpu.VMEM((GROUP_ROWS, D_MODEL), F32),
                        pltpu.VMEM((D_MODEL, tf), BF16), pltpu.VMEM((D_MODEL, tf), BF16),
                        pltpu.VMEM((tf, D_MODEL), BF16)],
    )
    return pl.pallas_call(
        _experts_kernel,
        out_shape=jax.ShapeDtypeStruct(hs.shape, BF16),
        grid_spec=grid_spec,
        compiler_params=_cparams(("arbitrary", "arbitrary")),
        name="expert_swiglu",
    )(g_expert, g_rows, hs, w1, w3, w2)


def _combine_kernel(tbl_ref, seg_ref, off_ref, rows_hbm, pos_ref, gates_ref, x_ref, mod_ref, out_ref, buf_ref, sem):
    b = pl.program_id(0)
    n_seq = pl.num_programs(0)
    tr = TR_MOE

    def all_segments(seq, action):
        def body(e, first):
            off = off_ref[seq * N_EXPERTS + e]
            seg = seg_ref[seq * N_EXPERTS + e]

            def make_copy(r0, rows):
                return pltpu.make_async_copy(
                    rows_hbm.at[pl.ds(pl.multiple_of(off + r0, SEG_ALIGN), rows), :],
                    buf_ref.at[seq % 2, pl.ds(pl.multiple_of(first + r0, SEG_ALIGN), rows), :],
                    sem.at[seq % 2, e])
            _segment_copies(seg, make_copy, action)
            return first + seg
        lax.fori_loop(0, N_EXPERTS, body, 0)

    @pl.when(b == 0)
    def _():
        buf_ref[...] = jnp.zeros_like(buf_ref)
        all_segments(b, "start")

    @pl.when(b + 1 < n_seq)
    def _():
        all_segments(b + 1, "start")

    out_ref[...] = x_ref[...]
    lane = lax.broadcasted_iota(jnp.int32, (tr, LANES), 1)
    layer_gate = mod_ref[5:6, :]
    all_segments(b, "wait")

    def expert(e, seg_start):
        bounds, win, all_small = _block_windows(tbl_ref, b, e, SEG_ALIGN)

        def scatter(tb, first, height):
            ts = slice(tb * tr, (tb + 1) * tr)
            pos_col = jnp.sum(jnp.where(lane == e, pos_ref[ts, :], 0.0), axis=-1, keepdims=True)
            gate_col = jnp.sum(jnp.where(lane == e, gates_ref[ts, :], 0.0), axis=-1, keepdims=True)
            slot_id = lax.broadcasted_iota(jnp.int32, (tr, height), 1).astype(F32) + float(first)
            sel = jnp.where(pos_col - win[tb].astype(F32) == slot_id, 1.0, 0.0).astype(BF16)
            rows = buf_ref[b % 2, pl.ds(pl.multiple_of(seg_start + win[tb] + first, SEG_ALIGN), height), :]
            out_ref[ts, :] += layer_gate * (gate_col * _dot(sel, rows))

        @pl.when(all_small)
        def _():
            for tb in range(TOK_BLOCKS):
                scatter(tb, 0, SMALL_WINDOW)

        @pl.when(jnp.logical_not(all_small))
        def _():
            for tb in range(TOK_BLOCKS):
                scatter(tb, 0, tr)
            for tb in range(TOK_BLOCKS):
                @pl.when(bounds[tb + 1] - win[tb] > tr)
                def _():
                    scatter(tb, tr, SPILL_ROWS)
        return seg_start + seg_ref[b * N_EXPERTS + e]
    lax.fori_loop(0, N_EXPERTS, expert, 0)


def _combine(ys, pos, gates, x, mod, bounds, seg, seg_off):
    tokens = x.shape[0]
    grid_spec = pltpu.PrefetchScalarGridSpec(
        num_scalar_prefetch=3,
        grid=(tokens // SEQ,),
        in_specs=[pl.BlockSpec(memory_space=pl.ANY),
                  pl.BlockSpec((SEQ, LANES), lambda b, *_: (b, 0)),
                  pl.BlockSpec((SEQ, LANES), lambda b, *_: (b, 0)),
                  pl.BlockSpec((SEQ, D_MODEL), lambda b, *_: (b, 0)),
                  pl.BlockSpec((None, 6, D_MODEL), lambda b, *_: (b, 0, 0))],
        out_specs=pl.BlockSpec((SEQ, D_MODEL), lambda b, *_: (b, 0)),
        scratch_shapes=[pltpu.VMEM((2, PACKED_ROWS, D_MODEL), BF16),
                        pltpu.SemaphoreType.DMA((2, N_EXPERTS))],
    )
    return pl.pallas_call(
        _combine_kernel,
        out_shape=jax.ShapeDtypeStruct((tokens, D_MODEL), F32),
        grid_spec=grid_spec,
        compiler_params=_cparams(("arbitrary",)),
        name="expert_combine",
    )(bounds, seg, seg_off, ys, pos, gates, x, mod)


def _token_mixer(x, mod, gain, w_in, q_gain, k_gain, ret_gain, w_out, bias_masks):
    *qkv, ret_in = _in_proj(x, mod, gain, w_in, q_gain, k_gain)
    attn = _dilated_attention(qkv, bias_masks)
    ret = _retention(ret_in, ret_gain)
    return _out_proj(attn, ret, x, mod, w_out)


def _moe_ffn(x, mod, gain, w_router, w1, w3, w2):
    h, gates, pos, post, start, cnt = _router(x, mod, gain, w_router)
    bounds, seg, seg_off, g_expert, g_rows, n_groups = _moe_plan(start, cnt)
    hs = _dispatch(h, post, bounds, seg, seg_off, n_groups * GROUP_ROWS)
    ys = _experts(hs, g_expert, g_rows, n_groups, w1, w3, w2)
    return _combine(ys, pos, gates, x, mod, bounds, seg, seg_off)


def kernel(x, c, rel_bias_table, norm_mix, norm_ffn, w_mod, b_mod, w_in, q_gain, k_gain, ret_gain, w_out,
           ffn_w1, ffn_w3, ffn_w2, moe_router, moe_w1, moe_w3, moe_w2):
    batch, seq, d_model = x.shape
    assert (seq, d_model) == (SEQ, D_MODEL)
    depth = w_mod.shape[0]
    mods = _modulation(c, w_mod, b_mod).reshape(depth, batch, 6, D_MODEL)
    bias_masks = _bias_masks(rel_bias_table)
    xt = x.reshape(batch * seq, d_model)
    for layer in range(depth):
        mod = mods[layer]
        xt = _token_mixer(xt, mod, norm_mix[layer], w_in[layer], q_gain[layer], k_gain[layer],
                          ret_gain[layer], w_out[layer], bias_masks)
        i = layer // 2
        if layer % 2 == 0:
            xt = _dense_ffn(xt, mod, norm_ffn[layer], ffn_w1[i], ffn_w3[i], ffn_w2[i])
        else:
            xt = _moe_ffn(xt, mod, norm_ffn[layer], moe_router[i], moe_w1[i], moe_w3[i], moe_w2[i])
    return xt.reshape(batch, seq, d_model)
```

```python
import functools
import math

import jax
import jax.numpy as jnp
import numpy as np
from jax import lax
from jax.experimental import pallas as pl
from jax.experimental.pallas import tpu as pltpu

D_MODEL = 1024
SEQ = 2048
ATTN_HEADS = 8
ATTN_HEAD_DIM = 64
ATTN_WIDTH = ATTN_HEADS * ATTN_HEAD_DIM
DILATED_PATTERNS = ((128, 1), (512, 4), (2048, 16))
BLOCK = 128
NUM_BUCKETS = 32
MAX_DISTANCE = 2048
RET_HEADS = 4
RET_KEY_DIM = 64
RET_VALUE_DIM = 128
RET_WIDTH = RET_HEADS * RET_VALUE_DIM
RET_QK_WIDTH = RET_HEADS * RET_KEY_DIM
RET_CHUNK = 128
ROPE_BASE = 10000.0
IN_WIDTH = 3 * ATTN_WIDTH + 2 * RET_QK_WIDTH + 2 * RET_WIDTH
RET_IN_WIDTH = IN_WIDTH - 3 * ATTN_WIDTH
N_EXPERTS = 8
EPS = 1e-6
NEG_INF = -1e30

LANES = 128
VMEM_LIMIT = 60 * 1024 * 1024

BF16 = jnp.bfloat16
F32 = jnp.float32

TM_PROJ = 256
TM_FFN = 512
MXU_COLS = 256
TR_MOE = 256
TF_MOE = 512
assert TM_PROJ == TR_MOE
RET_ROWS = 512
SUBLANES = 8


def _cparams(sem):
    return pltpu.CompilerParams(dimension_semantics=sem, vmem_limit_bytes=VMEM_LIMIT)


def _dot(a, b):
    return jnp.dot(a, b, preferred_element_type=F32)


def _dot_nt(a, b):
    return lax.dot_general(a, b, (((1,), (1,)), ((), ())), preferred_element_type=F32)


def _dot_tn(a, b):
    return lax.dot_general(a, b, (((0,), (0,)), ((), ())), preferred_element_type=F32)


def _split_bf16(v):
    hi = v.astype(BF16)
    lo = (v - hi.astype(F32)).astype(BF16)
    return hi, lo


def _silu(v):
    return v * (1.0 / (1.0 + jnp.exp(-v)))


def _modulated_norm(x, gain, scale, shift):
    ms = jnp.mean(x * x, axis=-1, keepdims=True)
    y = x * lax.rsqrt(ms + EPS) * gain
    return y * (1.0 + scale) + shift


def _mod_kernel(c_ref, w_ref, b_ref, o_ref):
    ca = _silu(c_ref[...]).astype(BF16)
    o_ref[...] = _dot(ca, w_ref[...].astype(BF16)) + b_ref[...]


def _modulation(c, w_mod, b_mod):
    depth, _, width = w_mod.shape
    batch = c.shape[0]
    tn = 1536
    return pl.pallas_call(
        _mod_kernel,
        out_shape=jax.ShapeDtypeStruct((depth, batch, width), F32),
        grid=(depth, width // tn),
        in_specs=[
            pl.BlockSpec((batch, D_MODEL), lambda l, n: (0, 0)),
            pl.BlockSpec((None, D_MODEL, tn), lambda l, n: (l, 0, n)),
            pl.BlockSpec((None, 1, tn), lambda l, n: (l, 0, n)),
        ],
        out_specs=pl.BlockSpec((None, batch, tn), lambda l, n: (l, 0, n)),
        compiler_params=_cparams(("arbitrary", "arbitrary")),
        name="adaln_modulation",
    )(c, w_mod, b_mod.reshape(depth, 1, width))


def _bias_kernel(table_ref, bucket_ref, o_ref):
    h = pl.program_id(1)
    bucket = bucket_ref[...]
    acc = jnp.full(bucket.shape, NEG_INF, F32)
    for b in range(NUM_BUCKETS):
        acc = jnp.where(bucket == b, table_ref[b, h], acc)
    o_ref[...] = acc


BIAS_FULL = {1: 0, 4: 2}
BIAS_FIRST = {1: 1, 4: 3, 16: 4}
N_BIAS_SETS = 5


def _bias_masks(rel_bias_table):
    i = jnp.arange(BLOCK)[:, None]
    j = jnp.arange(2 * BLOCK)[None, :]
    max_exact = NUM_BUCKETS // 2

    def bucket_of(rel, dilation, w_sub, exists):
        n = jnp.maximum(rel * dilation, 0)
        nf = jnp.maximum(n.astype(F32), float(max_exact))
        large = max_exact + (jnp.log(nf / max_exact) / math.log(MAX_DISTANCE / max_exact)
                             * (NUM_BUCKETS - max_exact)).astype(jnp.int32)
        large = jnp.minimum(large, NUM_BUCKETS - 1)
        bucket = jnp.where(n < max_exact, n, large)
        allowed = (rel >= 0) & (rel <= w_sub) & exists
        return jnp.where(allowed, bucket, -1)

    sets = [None] * N_BIAS_SETS
    for window, dilation in DILATED_PATTERNS:
        w_sub = window // dilation
        if dilation in BIAS_FULL:
            sets[BIAS_FULL[dilation]] = bucket_of(i - j + BLOCK, dilation, w_sub, j >= 0)
        sets[BIAS_FIRST[dilation]] = bucket_of(i - j, dilation, w_sub, j < BLOCK)
    buckets = jnp.stack(sets).astype(jnp.int32)
    return pl.pallas_call(
        _bias_kernel,
        out_shape=jax.ShapeDtypeStruct((N_BIAS_SETS, ATTN_HEADS // 2, 2 * BLOCK, 2 * BLOCK), F32),
        grid=(N_BIAS_SETS, ATTN_HEADS),
        in_specs=[
            pl.BlockSpec(memory_space=pltpu.SMEM),
            pl.BlockSpec((None, BLOCK, 2 * BLOCK), lambda p, h: (p, 0, 0)),
        ],
        out_specs=pl.BlockSpec((None, None, BLOCK, 2 * BLOCK), lambda p, h: (p, h // 2, h % 2, 0)),
        compiler_params=_cparams(("arbitrary", "arbitrary")),
        name="relative_bias_masks",
    )(rel_bias_table, buckets)


HALF = ATTN_WIDTH // 2


def _in_proj_kernel(x_ref, mod_ref, gain_ref, w_ref, qg_ref, kg_ref, grp_ref,
                    q1_ref, k1_ref, v1_ref, q4_ref, k4_ref, v4_ref, q16_ref, k16_ref, v16_ref, r_ref,
                    perm_ref):
    mod = mod_ref[...]
    h = _modulated_norm(x_ref[...], gain_ref[...], mod[1:2], mod[0:1]).astype(BF16)
    proj = _dot(h, w_ref[...].astype(BF16))
    grp = grp_ref[...]
    tm = proj.shape[0]

    def head_norm(t, gain):
        hi, lo = _split_bf16(t * t)
        ss = _dot(hi, grp) + _dot(lo, grp)
        return t * lax.rsqrt(ss * (1.0 / ATTN_HEAD_DIM) + EPS) * gain

    def emit(t, o1_ref, o4_ref, o16_ref):
        o1_ref[...] = t.astype(BF16)
        for j in range(ATTN_WIDTH // LANES):
            perm_ref[j] = t[:, j * LANES:(j + 1) * LANES]
        for dil, o_ref in ((4, o4_ref), (16, o16_ref)):
            for hf in range(2):
                for r in range(dil):
                    for jj in range(HALF // LANES):
                        c0 = (hf * dil + r) * HALF + jj * LANES
                        o_ref[:, c0:c0 + LANES] = perm_ref[hf * (HALF // LANES) + jj,
                                                           pl.ds(r, tm // dil, stride=dil), :].astype(BF16)

    emit(head_norm(proj[:, :ATTN_WIDTH], qg_ref[...]) * (ATTN_HEAD_DIM ** -0.5), q1_ref, q4_ref, q16_ref)
    emit(head_norm(proj[:, ATTN_WIDTH:2 * ATTN_WIDTH], kg_ref[...]), k1_ref, k4_ref, k16_ref)
    emit(proj[:, 2 * ATTN_WIDTH:3 * ATTN_WIDTH], v1_ref, v4_ref, v16_ref)
    r_ref[...] = proj[:, 3 * ATTN_WIDTH:]


def _in_proj(x, mod, gain, w_in, q_gain, k_gain):
    tokens = x.shape[0]
    tm = TM_PROJ
    per_seq = SEQ // tm
    grp = np.kron(np.eye(ATTN_HEADS), np.ones((ATTN_HEAD_DIM, ATTN_HEAD_DIM))).astype(np.float32)
    row = lambda i: (i, 0)
    const = lambda i: (0, 0)
    layouts = []
    for dil in (1, 4, 16):
        shape = jax.ShapeDtypeStruct((tokens // dil, dil * ATTN_WIDTH), BF16)
        spec = pl.BlockSpec((tm // dil, dil * ATTN_WIDTH), row)
        layouts.append(((shape,) * 3, (spec,) * 3))
    out_shape = sum((s for s, _ in layouts), ()) + (jax.ShapeDtypeStruct((tokens, RET_IN_WIDTH), F32),)
    out_specs = sum((s for _, s in layouts), ()) + (pl.BlockSpec((tm, RET_IN_WIDTH), row),)
    return pl.pallas_call(
        _in_proj_kernel,
        out_shape=out_shape,
        grid=(tokens // tm,),
        in_specs=[
            pl.BlockSpec((tm, D_MODEL), row),
            pl.BlockSpec((None, 6, D_MODEL), lambda i: (i // per_seq, 0, 0)),
            pl.BlockSpec((1, D_MODEL), const),
            pl.BlockSpec((D_MODEL, IN_WIDTH), const, pipeline_mode=pl.Buffered(1)),
            pl.BlockSpec((1, ATTN_WIDTH), const),
            pl.BlockSpec((1, ATTN_WIDTH), const),
            pl.BlockSpec((ATTN_WIDTH, ATTN_WIDTH), const),
        ],
        out_specs=out_specs,
        scratch_shapes=[pltpu.VMEM((ATTN_WIDTH // LANES, tm, LANES), F32)],
        compiler_params=_cparams(("arbitrary",)),
        name="in_projection",
    )(x, mod, gain.reshape(1, D_MODEL), w_in,
      jnp.tile(q_gain, ATTN_HEADS).reshape(1, ATTN_WIDTH),
      jnp.tile(k_gain, ATTN_HEADS).reshape(1, ATTN_WIDTH),
      jnp.asarray(grp, BF16))


PAIRS_PER_HALF = ATTN_HEADS // 4
GROUP = 4


def _pair_scores(qp, kp, vp, bias2, masks, low):
    q2 = jnp.concatenate([qp * masks[0], qp * masks[1]], axis=0)
    s = _dot_nt(q2, kp) + bias2
    m = jnp.max(s, axis=-1, keepdims=True)
    p = jnp.exp(s - m)
    den = jnp.sum(p, axis=-1, keepdims=True)
    pv = _dot(p.astype(BF16), vp)
    pick = lambda t: jnp.where(low, t[:BLOCK], t[BLOCK:])
    return pick(pv), pick(m), pick(den)


def _attn_kernel(q1_ref, k1_ref, v1_ref, q4_ref, k4_ref, v4_ref, q16_ref, k16_ref, v16_ref, bm_ref,
                 o_ref, acc_ref, max_ref, den_ref):
    lane = lax.broadcasted_iota(jnp.int32, (BLOCK, LANES), 1)
    low = lane < ATTN_HEAD_DIM
    masks = (jnp.where(low, 1.0, 0.0).astype(BF16), jnp.where(low, 0.0, 1.0).astype(BF16))

    def block(q_ref, k_ref, v_ref, c0, q0, w0, width, bias_set, rows, first):
        for p in range(PAIRS_PER_HALF):
            cs = slice(c0 + p * LANES, c0 + (p + 1) * LANES)
            acc, m, den = _pair_scores(q_ref[pl.ds(q0, BLOCK), cs], k_ref[pl.ds(w0, width), cs],
                                       v_ref[pl.ds(w0, width), cs], bm_ref[bias_set, p, :, 0:width],
                                       masks, low)
            if not first:
                m_old = max_ref[p, rows, :]
                m_new = jnp.maximum(m_old, m)
                a, b = jnp.exp(m_old - m_new), jnp.exp(m - m_new)
                den = den_ref[p, rows, :] * a + den * b
                acc = acc_ref[p, rows, :] * a + acc * b
                m = m_new
            max_ref[p, rows, :] = m
            den_ref[p, rows, :] = den
            acc_ref[p, rows, :] = acc

    def d1_group(g, carry):
        for u in range(GROUP):
            n = g * GROUP + u
            q0 = pl.multiple_of(n * BLOCK, BLOCK)
            w0 = pl.multiple_of(jnp.maximum(n - 1, 0) * BLOCK, BLOCK)
            bias_set = jnp.where(n == 0, BIAS_FIRST[1], BIAS_FULL[1])
            block(q1_ref, k1_ref, v1_ref, 0, q0, w0, 2 * BLOCK, bias_set, pl.ds(q0, BLOCK), True)
        return carry
    lax.fori_loop(0, SEQ // BLOCK // GROUP, d1_group, 0)

    for r in range(4):
        for n in range(SEQ // 4 // BLOCK):
            block(q4_ref, k4_ref, v4_ref, r * HALF, n * BLOCK, max(n - 1, 0) * BLOCK, 2 * BLOCK,
                  BIAS_FIRST[4] if n == 0 else BIAS_FULL[4],
                  pl.ds(r + 4 * BLOCK * n, BLOCK, stride=4), False)

    for r in range(16):
        block(q16_ref, k16_ref, v16_ref, r * HALF, 0, 0, BLOCK, BIAS_FIRST[16],
              pl.ds(r, BLOCK, stride=16), False)

    for n in range(SEQ // BLOCK):
        rows = slice(n * BLOCK, (n + 1) * BLOCK)
        for p in range(PAIRS_PER_HALF):
            o_ref[rows, p * LANES:(p + 1) * LANES] = (acc_ref[p, rows, :] / den_ref[p, rows, :]).astype(BF16)


def _dilated_attention(qkv, bias_masks):
    tokens = qkv[0].shape[0]
    batch = tokens // SEQ
    specs = []
    for dil in (1, 4, 16):
        specs += [pl.BlockSpec((SEQ // dil, dil * HALF), lambda b, hf: (b, hf))] * 3
    state = pltpu.VMEM((PAIRS_PER_HALF, SEQ, LANES), F32)
    return pl.pallas_call(
        _attn_kernel,
        out_shape=jax.ShapeDtypeStruct((tokens, ATTN_WIDTH), BF16),
        grid=(batch, 2),
        in_specs=specs + [pl.BlockSpec((N_BIAS_SETS, PAIRS_PER_HALF, 2 * BLOCK, 2 * BLOCK),
                                       lambda b, hf: (0, hf, 0, 0))],
        out_specs=pl.BlockSpec((SEQ, HALF), lambda b, hf: (b, hf)),
        scratch_shapes=[state, state, state],
        compiler_params=_cparams(("arbitrary", "arbitrary")),
        name="dilated_attention",
    )(*qkv, bias_masks)


def _retention_kernel(r_ref, cos_ref, sin_ref, dmask_ref, qdec_ref, kdec_ref, cdec_ref, gain_ref,
                      o_ref, state_ref):
    @pl.when(pl.program_id(1) == 0)
    def _():
        state_ref[...] = jnp.zeros_like(state_ref)

    lane = lax.broadcasted_iota(jnp.int32, (RET_CHUNK, LANES), 1)
    low = lane < RET_KEY_DIM
    first_half = (lane % RET_KEY_DIM) < (RET_KEY_DIM // 2)

    def rotate(t, cos, sin):
        partner = jnp.where(first_half, pltpu.roll(t, LANES - RET_KEY_DIM // 2, 1),
                            pltpu.roll(t, RET_KEY_DIM // 2, 1))
        return t * cos + partner * sin

    for c in range(RET_ROWS // RET_CHUNK):
        rows = slice(c * RET_CHUNK, (c + 1) * RET_CHUNK)
        for hp in range(RET_HEADS // 2):
            qs = slice(hp * LANES, (hp + 1) * LANES)
            ks = slice(RET_QK_WIDTH + hp * LANES, RET_QK_WIDTH + (hp + 1) * LANES)
            cos, sin = cos_ref[rows, qs], sin_ref[rows, qs]
            q_pair = rotate(r_ref[rows, qs], cos, sin) * (RET_KEY_DIM ** -0.5)
            k_pair = rotate(r_ref[rows, ks], cos, sin)
            for hh in range(2):
                head = 2 * hp + hh
                vs = slice(2 * RET_QK_WIDTH + head * LANES, 2 * RET_QK_WIDTH + (head + 1) * LANES)
                gs = slice(2 * RET_QK_WIDTH + RET_WIDTH + head * LANES,
                           2 * RET_QK_WIDTH + RET_WIDTH + (head + 1) * LANES)
                keep = low if hh == 0 else jnp.logical_not(low)
                qm = jnp.where(keep, q_pair, 0.0)
                vb = r_ref[rows, vs].astype(BF16)
                state = state_ref[head]
                inner = _dot_nt(qm.astype(BF16), k_pair.astype(BF16)) * dmask_ref[head]
                y = _dot(inner.astype(BF16), vb)
                y = y + _dot((qm * qdec_ref[head]).astype(BF16), state.astype(BF16))
                state_ref[head] = state * cdec_ref[head] + _dot_tn((k_pair * kdec_ref[head]).astype(BF16), vb)
                mu = jnp.mean(y, axis=-1, keepdims=True)
                yc = y - mu
                var = jnp.mean(yc * yc, axis=-1, keepdims=True)
                yn = yc * lax.rsqrt(var + EPS) * gain_ref[:, head * LANES:(head + 1) * LANES]
                o_ref[rows, head * LANES:(head + 1) * LANES] = (_silu(r_ref[rows, gs]) * yn).astype(BF16)


def _retention_tables():
    half = RET_KEY_DIM // 2
    pos = jnp.arange(SEQ, dtype=F32)
    inv = ROPE_BASE ** (-jnp.arange(half, dtype=F32) / half)
    ang = pos[:, None] * inv[None, :]
    cos, sin = jnp.cos(ang), jnp.sin(ang)
    cos_full = jnp.tile(jnp.concatenate([cos, cos], axis=-1), (1, RET_HEADS))
    sin_signed = jnp.tile(jnp.concatenate([-sin, sin], axis=-1), (1, RET_HEADS))
    log_g = jnp.log(1.0 - 2.0 ** (-5.0 - jnp.arange(RET_HEADS, dtype=F32)))
    idx = jnp.arange(RET_CHUNK, dtype=F32)
    diff = idx[:, None] - idx[None, :]
    dmask = jnp.where(diff >= 0, jnp.exp(jnp.maximum(diff, 0.0)[None] * log_g[:, None, None]), 0.0)
    q_decay = jnp.exp((idx + 1.0)[None, :] * log_g[:, None])[..., None]
    k_decay = jnp.exp((RET_CHUNK - 1.0 - idx)[None, :] * log_g[:, None])[..., None]
    chunk_decay = jnp.exp(RET_CHUNK * log_g)[:, None, None]
    full = (RET_HEADS, RET_CHUNK, LANES)
    return (cos_full, sin_signed, dmask, jnp.broadcast_to(q_decay, full),
            jnp.broadcast_to(k_decay, full), jnp.broadcast_to(chunk_decay, full))


def _retention(ret_in, ret_gain):
    tokens = ret_in.shape[0]
    batch = tokens // SEQ
    per_seq = SEQ // RET_ROWS
    cos, sin, dmask, qdec, kdec, cdec = _retention_tables()
    tab = pl.BlockSpec((RET_ROWS, RET_QK_WIDTH), lambda b, j: (j, 0))
    const3 = pl.BlockSpec((RET_HEADS, RET_CHUNK, LANES), lambda b, j: (0, 0, 0))
    return pl.pallas_call(
        _retention_kernel,
        out_shape=jax.ShapeDtypeStruct((tokens, RET_WIDTH), BF16),
        grid=(batch, per_seq),
        in_specs=[
            pl.BlockSpec((RET_ROWS, RET_IN_WIDTH), lambda b, j: (b * per_seq + j, 0)),
            tab, tab, const3, const3, const3, const3,
            pl.BlockSpec((1, RET_WIDTH), lambda b, j: (0, 0)),
        ],
        out_specs=pl.BlockSpec((RET_ROWS, RET_WIDTH), lambda b, j: (b * per_seq + j, 0)),
        scratch_shapes=[pltpu.VMEM((RET_HEADS, LANES, RET_VALUE_DIM), F32)],
        compiler_params=_cparams(("arbitrary", "arbitrary")),
        name="retention",
    )(ret_in, cos, sin, dmask, qdec, kdec, cdec, ret_gain.reshape(1, RET_WIDTH))


def _out_proj_kernel(attn_ref, ret_ref, x_ref, mod_ref, w_ref, out_ref):
    mix = (_dot(attn_ref[...], w_ref[:ATTN_WIDTH, :].astype(BF16))
           + _dot(ret_ref[...], w_ref[ATTN_WIDTH:, :].astype(BF16)))
    out_ref[...] = x_ref[...] + mod_ref[2:3, :] * mix


def _out_proj(attn, ret, x, mod, w_out):
    tokens = x.shape[0]
    tm = TM_FFN
    per_seq = SEQ // tm
    row = lambda i: (i, 0)
    return pl.pallas_call(
        _out_proj_kernel,
        out_shape=jax.ShapeDtypeStruct((tokens, D_MODEL), F32),
        grid=(tokens // tm,),
        in_specs=[pl.BlockSpec((tm, ATTN_WIDTH), row),
                  pl.BlockSpec((tm, RET_WIDTH), row),
                  pl.BlockSpec((tm, D_MODEL), row),
                  pl.BlockSpec((None, 6, D_MODEL), lambda i: (i // per_seq, 0, 0)),
                  pl.BlockSpec((D_MODEL, D_MODEL), lambda i: (0, 0))],
        out_specs=pl.BlockSpec((tm, D_MODEL), row),
        compiler_params=_cparams(("arbitrary",)),
        name="out_projection",
    )(attn, ret, x, mod, w_out)


def _swiglu_chunks(h, w1_ref, w3_ref, w2_ref):
    d_ff = w1_ref.shape[-1]
    total = None
    for c0 in range(0, d_ff, MXU_COLS):
        c1 = min(c0 + MXU_COLS, d_ff)
        z = (_silu(_dot(h, w1_ref[:, c0:c1].astype(BF16))) * _dot(h, w3_ref[:, c0:c1].astype(BF16))).astype(BF16)
        part = _dot(z, w2_ref[c0:c1, :].astype(BF16))
        total = part if total is None else total + part
    return total


def _out_proj_ffn_kernel(attn_ref, ret_ref, x_ref, mod_ref, wo_ref, gain_ref, w1_ref, w3_ref, w2_ref, out_ref):
    mod = mod_ref[...]
    mix = (_dot(attn_ref[...], wo_ref[:ATTN_WIDTH, :].astype(BF16))
           + _dot(ret_ref[...], wo_ref[ATTN_WIDTH:, :].astype(BF16)))
    x = x_ref[...] + mod[2:3, :] * mix
    h = _modulated_norm(x, gain_ref[...], mod[4:5], mod[3:4]).astype(BF16)
    out_ref[...] = x + mod[5:6, :] * _swiglu_chunks(h, w1_ref, w3_ref, w2_ref)


def _out_proj_dense_ffn(attn, ret, x, mod, w_out, gain, w1, w3, w2):
    tokens = x.shape[0]
    d_ff = w1.shape[1]
    tm = TM_FFN
    per_seq = SEQ // tm
    row = lambda i: (i, 0)
    resident = pl.Buffered(1)
    return pl.pallas_call(
        _out_proj_ffn_kernel,
        out_shape=jax.ShapeDtypeStruct((tokens, D_MODEL), F32),
        grid=(tokens // tm,),
        in_specs=[
            pl.BlockSpec((tm, ATTN_WIDTH), row),
            pl.BlockSpec((tm, RET_WIDTH), row),
            pl.BlockSpec((tm, D_MODEL), row),
            pl.BlockSpec((None, 6, D_MODEL), lambda i: (i // per_seq, 0, 0)),
            pl.BlockSpec((D_MODEL, D_MODEL), lambda i: (0, 0), pipeline_mode=resident),
            pl.BlockSpec((1, D_MODEL), lambda i: (0, 0)),
            pl.BlockSpec((D_MODEL, d_ff), lambda i: (0, 0), pipeline_mode=resident),
            pl.BlockSpec((D_MODEL, d_ff), lambda i: (0, 0), pipeline_mode=resident),
            pl.BlockSpec((d_ff, D_MODEL), lambda i: (0, 0), pipeline_mode=resident),
        ],
        out_specs=pl.BlockSpec((tm, D_MODEL), row),
        compiler_params=_cparams(("arbitrary",)),
        name="out_projection_dense_swiglu",
    )(attn, ret, x, mod, w_out, gain.reshape(1, D_MODEL), w1, w3, w2)


def _router_kernel(x_ref, mod_ref, gain_ref, wr_ref, h_ref, gates_ref, pos_ref, post_ref, start_ref,
                   cnt_ref, carry_ref, *, tiles_per_seq):
    i = pl.program_id(0)

    @pl.when(i % tiles_per_seq == 0)
    def _():
        carry_ref[...] = jnp.zeros_like(carry_ref)

    mod = mod_ref[...]
    h = _modulated_norm(x_ref[...], gain_ref[...], mod[4:5], mod[3:4]).astype(BF16)
    h_ref[...] = h
    tm = h.shape[0]
    lane = lax.broadcasted_iota(jnp.int32, (tm, LANES), 1).astype(F32)
    logits = jnp.where(lane < N_EXPERTS, _dot(h, wr_ref[...]), -jnp.inf)
    m1 = jnp.max(logits, axis=-1, keepdims=True)
    i1 = jnp.min(jnp.where(logits == m1, lane, float(LANES)), axis=-1, keepdims=True)
    rest = jnp.where(lane == i1, -jnp.inf, logits)
    m2 = jnp.max(rest, axis=-1, keepdims=True)
    i2 = jnp.min(jnp.where(rest == m2, lane, float(LANES)), axis=-1, keepdims=True)
    e2 = jnp.exp(m2 - m1)
    g1 = 1.0 / (1.0 + e2)
    g2 = e2 / (1.0 + e2)
    gates_ref[...] = jnp.where(lane == i1, g1, 0.0) + jnp.where(lane == i2, g2, 0.0)
    chosen = (lane == i1) | (lane == i2)
    onehot = jnp.where(chosen, 1.0, 0.0)
    r = lax.broadcasted_iota(jnp.int32, (tm, tm), 0)
    c = lax.broadcasted_iota(jnp.int32, (tm, tm), 1)
    tril = jnp.where(c <= r, 1.0, 0.0).astype(BF16)
    incl = _dot(tril, onehot.astype(BF16))
    carry = carry_ref[0:1, :]
    start_ref[...] = carry_ref[...]
    pos = jnp.where(chosen, incl - 1.0 + carry, -1.0)
    pos_ref[...] = pos
    post_ref[...] = pos.T[:N_EXPERTS, :]
    total = carry + incl[tm - 1:tm, :]
    carry_ref[...] = jnp.broadcast_to(total, carry_ref.shape)
    cnt_ref[...] = jnp.broadcast_to(total, cnt_ref.shape)


def _router(x, mod, gain, w_router):
    tokens = x.shape[0]
    batch = tokens // SEQ
    tm = TM_PROJ
    per_seq = SEQ // tm
    wr = jnp.zeros((D_MODEL, LANES), BF16).at[:, :N_EXPERTS].set(w_router.astype(BF16))
    row = lambda i: (i, 0)
    return pl.pallas_call(
        functools.partial(_router_kernel, tiles_per_seq=per_seq),
        out_shape=(
            jax.ShapeDtypeStruct((tokens, D_MODEL), BF16),
            jax.ShapeDtypeStruct((tokens, LANES), F32),
            jax.ShapeDtypeStruct((tokens, LANES), F32),
            jax.ShapeDtypeStruct((N_EXPERTS, tokens), F32),
            jax.ShapeDtypeStruct((tokens // tm, 8, LANES), F32),
            jax.ShapeDtypeStruct((batch, 8, LANES), F32),
        ),
        grid=(tokens // tm,),
        in_specs=[
            pl.BlockSpec((tm, D_MODEL), row),
            pl.BlockSpec((None, 6, D_MODEL), lambda i: (i // per_seq, 0, 0)),
            pl.BlockSpec((1, D_MODEL), lambda i: (0, 0)),
            pl.BlockSpec((D_MODEL, LANES), lambda i: (0, 0)),
        ],
        out_specs=(
            pl.BlockSpec((tm, D_MODEL), row),
            pl.BlockSpec((tm, LANES), row),
            pl.BlockSpec((tm, LANES), row),
            pl.BlockSpec((N_EXPERTS, tm), lambda i: (0, i)),
            pl.BlockSpec((None, 8, LANES), lambda i: (i, 0, 0)),
            pl.BlockSpec((None, 8, LANES), lambda i: (i // per_seq, 0, 0)),
        ),
        scratch_shapes=[pltpu.VMEM((8, LANES), F32)],
        compiler_params=_cparams(("arbitrary",)),
        name="expert_router",
    )(x, mod, gain.reshape(1, D_MODEL), wr)


TOK_BLOCKS = SEQ // TR_MOE
TAIL_TILES = (64, 128, TR_MOE)
SPILL_ROWS = 16
SMALL_WINDOW = 128
SEG_ALIGN = 16
GROUP_ROWS = 3072
PACKED_ROWS = 2 * SEQ + N_EXPERTS * SEG_ALIGN + 2 * TR_MOE
BIG_CHUNK = 128
STAGE_ROWS = SEQ + TR_MOE
N_BOUNDS = TOK_BLOCKS + 1


def _moe_plan(start, cnt):
    batch = cnt.shape[0]
    counts = cnt[:, 0, :N_EXPERTS].astype(jnp.int32)
    bounds = jnp.concatenate([start[:, 0, :N_EXPERTS].reshape(batch, TOK_BLOCKS, N_EXPERTS).astype(jnp.int32),
                              counts[:, None, :]], axis=1)
    seg = (counts + (SEG_ALIGN - 1)) // SEG_ALIGN * SEG_ALIGN
    rows_e = jnp.sum(seg, axis=0)
    groups_e = (rows_e + (GROUP_ROWS - 1)) // GROUP_ROWS
    first_group = jnp.cumsum(groups_e) - groups_e
    seg_off = first_group[None, :] * GROUP_ROWS + jnp.cumsum(seg, axis=0) - seg
    max_rows = batch * (2 * SEQ + N_EXPERTS * (SEG_ALIGN - 1))
    n_groups = max_rows // GROUP_ROWS + N_EXPERTS
    g = jnp.arange(n_groups)
    g_expert = jnp.minimum(jnp.sum(g[:, None] >= jnp.cumsum(groups_e)[None, :], axis=1), N_EXPERTS - 1)
    g_rows = jnp.clip(rows_e[g_expert] - (g - first_group[g_expert]) * GROUP_ROWS, 0, GROUP_ROWS)
    g_rows = jnp.where(g < jnp.sum(groups_e), g_rows, 0)
    return (bounds.reshape(-1), seg.reshape(-1), seg_off.reshape(-1),
            g_expert.astype(jnp.int32), g_rows.astype(jnp.int32), n_groups)


def _segment_copies(seg, make_copy, action):
    n_big = lax.shift_right_logical(seg, BIG_CHUNK.bit_length() - 1)

    def big(i, carry):
        getattr(make_copy(pl.multiple_of(i * BIG_CHUNK, BIG_CHUNK), BIG_CHUNK), action)()
        return carry
    lax.fori_loop(0, n_big, big, 0)
    rest0 = n_big * BIG_CHUNK
    n_small = lax.shift_right_logical(seg - rest0, SEG_ALIGN.bit_length() - 1)

    def small(i, carry):
        getattr(make_copy(pl.multiple_of(rest0 + i * SEG_ALIGN, SEG_ALIGN), SEG_ALIGN), action)()
        return carry
    lax.fori_loop(0, n_small, small, 0)


def _block_windows(tbl_ref, b, e, align):
    base = b * N_BOUNDS * N_EXPERTS + e
    bounds = [tbl_ref[base + tb * N_EXPERTS] for tb in range(N_BOUNDS)]
    shift = align.bit_length() - 1
    win = [pl.multiple_of(lax.shift_left(lax.shift_right_logical(s, shift), shift), align) for s in bounds[:-1]]
    all_small = functools.reduce(jnp.logical_and,
                                 [bounds[tb + 1] - win[tb] <= SMALL_WINDOW for tb in range(TOK_BLOCKS)])
    return bounds, win, all_small


def _dispatch_kernel(tbl_ref, seg_ref, off_ref, h_ref, post_ref, rows_hbm, acc_ref, stage_ref, sem):
    b = pl.program_id(0)
    tr = TR_MOE

    def copies(e, action):
        slot = e % 2
        off = off_ref[b * N_EXPERTS + e]

        def make_copy(r0, rows):
            return pltpu.make_async_copy(stage_ref.at[slot, pl.ds(r0, rows), :],
                                         rows_hbm.at[pl.ds(pl.multiple_of(off + r0, SEG_ALIGN), rows), :],
                                         sem.at[slot])
        _segment_copies(seg_ref[b * N_EXPERTS + e], make_copy, action)

    def expert(e, carry):
        bounds, win, all_small = _block_windows(tbl_ref, b, e, SUBLANES)
        n_tiles = lax.shift_right_logical(bounds[-1] + (tr - 1), tr.bit_length() - 1)

        def clear(r, c):
            acc_ref[pl.ds(pl.multiple_of(r * tr, tr), tr), :] = jnp.zeros((tr, D_MODEL), F32)
            return c
        lax.fori_loop(0, n_tiles + 2, clear, 0)

        def gather(height):
            slot_id = lax.broadcasted_iota(jnp.int32, (height, tr), 0).astype(F32)
            for tb in range(TOK_BLOCKS):
                ts = slice(tb * tr, (tb + 1) * tr)
                local = post_ref[pl.ds(e, 1), ts] - win[tb].astype(F32)
                sel = jnp.where(local == slot_id, 1.0, 0.0).astype(BF16)
                acc_ref[pl.ds(win[tb], height), :] += _dot(sel, h_ref[ts, :])

        @pl.when(all_small)
        def _():
            gather(SMALL_WINDOW)

        @pl.when(jnp.logical_not(all_small))
        def _():
            gather(tr + SPILL_ROWS)

        @pl.when(e >= 2)
        def _():
            copies(e - 2, "wait")

        def to_stage(r, c):
            rows = pl.ds(pl.multiple_of(r * tr, tr), tr)
            stage_ref[e % 2, rows, :] = acc_ref[rows, :].astype(BF16)
            return c
        lax.fori_loop(0, n_tiles, to_stage, 0)
        copies(e, "start")
        return carry
    lax.fori_loop(0, N_EXPERTS, expert, 0)
    copies(N_EXPERTS - 2, "wait")
    copies(N_EXPERTS - 1, "wait")


def _dispatch(h, post, bounds, seg, seg_off, total_rows):
    tokens = h.shape[0]
    grid_spec = pltpu.PrefetchScalarGridSpec(
        num_scalar_prefetch=3,
        grid=(tokens // SEQ,),
        in_specs=[pl.BlockSpec((SEQ, D_MODEL), lambda b, *_: (b, 0)),
                  pl.BlockSpec((N_EXPERTS, SEQ), lambda b, *_: (0, b))],
        out_specs=pl.BlockSpec(memory_space=pl.ANY),
        scratch_shapes=[pltpu.VMEM((SEQ + 3 * TR_MOE, D_MODEL), F32),
                        pltpu.VMEM((2, STAGE_ROWS, D_MODEL), BF16),
                        pltpu.SemaphoreType.DMA((2,))],
    )
    return pl.pallas_call(
        _dispatch_kernel,
        out_shape=jax.ShapeDtypeStruct((total_rows, D_MODEL), BF16),
        grid_spec=grid_spec,
        compiler_params=_cparams(("arbitrary",)),
        name="expert_dispatch",
    )(bounds, seg, seg_off, h, post)


def _experts_kernel(ge_ref, rows_ref, hs_ref, w1_ref, w3_ref, w2_ref, ys_ref, acc_ref, wb1_ref, wb3_ref, wb2_ref):
    g, f = pl.program_id(0), pl.program_id(1)
    last_f = pl.num_programs(1) - 1
    tr = TR_MOE
    rows = rows_ref[g]
    n_big = lax.shift_right_logical(rows, tr.bit_length())
    n_full = lax.shift_right_logical(rows, tr.bit_length() - 1)
    rest = rows - n_full * tr
    n_tiles = lax.shift_right_logical(rows + (tr - 1), tr.bit_length() - 1)
    tail0 = pl.multiple_of(n_full * tr, tr)

    @pl.when(rows > 0)
    def _():
        wb1_ref[...] = w1_ref[...].astype(BF16)
        wb3_ref[...] = w3_ref[...].astype(BF16)
        wb2_ref[...] = w2_ref[...].astype(BF16)

    @pl.when(f == 0)
    def _():
        def clear(r, carry):
            acc_ref[pl.ds(pl.multiple_of(r * tr, tr), tr), :] = jnp.zeros((tr, D_MODEL), F32)
            return carry
        lax.fori_loop(0, n_tiles, clear, 0)

    def swiglu(hr, r0, m):
        z = (_silu(_dot(hr, wb1_ref[...])) * _dot(hr, wb3_ref[...])).astype(BF16)
        acc_ref[pl.ds(r0, m), :] += _dot(z, wb2_ref[...])

    def big_tile(r, carry):
        r0 = pl.multiple_of(r * 2 * tr, 2 * tr)
        swiglu(hs_ref[pl.ds(r0, 2 * tr), :], r0, 2 * tr)
        return carry
    lax.fori_loop(0, n_big, big_tile, 0)

    @pl.when(n_full > 2 * n_big)
    def _():
        r0 = pl.multiple_of(n_big * 2 * tr, tr)
        swiglu(hs_ref[pl.ds(r0, tr), :], r0, tr)
    lo = 0
    for m in TAIL_TILES:
        @pl.when((rest > lo) & (rest <= m))
        def _():
            valid = lax.broadcasted_iota(jnp.int32, (m, D_MODEL), 0) < rest
            hr = hs_ref[pl.ds(tail0, m), :]
            swiglu(jnp.where(valid, hr, jnp.zeros_like(hr)), tail0, m)
        lo = m

    @pl.when(f == last_f)
    def _():
        ys_ref[...] = jnp.zeros_like(ys_ref)

        def store(r, carry):
            rr = pl.ds(pl.multiple_of(r * tr, tr), tr)
            ys_ref[rr, :] = acc_ref[rr, :].astype(BF16)
            return carry
        lax.fori_loop(0, n_tiles, store, 0)


def _experts(hs, g_expert, g_rows, n_groups, w1, w3, w2):
    d_ff = w1.shape[2]
    tf = TF_MOE
    n_f = d_ff // tf
    chunk = lambda g, f, gr: jnp.where(gr[g] > 0, f, n_f - 1)
    grid_spec = pltpu.PrefetchScalarGridSpec(
        num_scalar_prefetch=2,
        grid=(n_groups, n_f),
        in_specs=[
            pl.BlockSpec((GROUP_ROWS, D_MODEL), lambda g, f, ge, gr: (g, 0)),
            pl.BlockSpec((None, D_MODEL, tf), lambda g, f, ge, gr: (ge[g], 0, chunk(g, f, gr))),
            pl.BlockSpec((None, D_MODEL, tf), lambda g, f, ge, gr: (ge[g], 0, chunk(g, f, gr))),
            pl.BlockSpec((None, tf, D_MODEL), lambda g, f, ge, gr: (ge[g], chunk(g, f, gr), 0)),
        ],
        out_specs=pl.BlockSpec((GROUP_ROWS, D_MODEL), lambda g, f, ge, gr: (g, 0)),
        scratch_shapes=[pltpu.VMEM((GROUP_ROWS, D_MODEL), F32),
                        pltpu.VMEM((D_MODEL, tf), BF16), pltpu.VMEM((D_MODEL, tf), BF16),
                        pltpu.VMEM((tf, D_MODEL), BF16)],
    )
    return pl.pallas_call(
        _experts_kernel,
        out_shape=jax.ShapeDtypeStruct(hs.shape, BF16),
        grid_spec=grid_spec,
        compiler_params=_cparams(("arbitrary", "arbitrary")),
        name="expert_swiglu",
    )(g_expert, g_rows, hs, w1, w3, w2)


def _combine_kernel(tbl_ref, seg_ref, off_ref, rows_hbm, pos_ref, gates_ref, x_ref, mod_ref, out_ref, buf_ref, sem):
    b = pl.program_id(0)
    n_seq = pl.num_programs(0)
    tr = TR_MOE

    def all_segments(seq, action):
        def body(e, first):
            off = off_ref[seq * N_EXPERTS + e]
            seg = seg_ref[seq * N_EXPERTS + e]

            def make_copy(r0, rows):
                return pltpu.make_async_copy(
                    rows_hbm.at[pl.ds(pl.multiple_of(off + r0, SEG_ALIGN), rows), :],
                    buf_ref.at[seq % 2, pl.ds(pl.multiple_of(first + r0, SEG_ALIGN), rows), :],
                    sem.at[seq % 2, e])
            _segment_copies(seg, make_copy, action)
            return first + seg
        lax.fori_loop(0, N_EXPERTS, body, 0)

    @pl.when(b == 0)
    def _():
        buf_ref[...] = jnp.zeros_like(buf_ref)
        all_segments(b, "start")

    @pl.when(b + 1 < n_seq)
    def _():
        all_segments(b + 1, "start")

    out_ref[...] = x_ref[...]
    lane = lax.broadcasted_iota(jnp.int32, (tr, LANES), 1)
    layer_gate = mod_ref[5:6, :]
    all_segments(b, "wait")

    def expert(e, seg_start):
        bounds, win, all_small = _block_windows(tbl_ref, b, e, SEG_ALIGN)

        def scatter(tb, first, height):
            ts = slice(tb * tr, (tb + 1) * tr)
            pos_col = jnp.sum(jnp.where(lane == e, pos_ref[ts, :], 0.0), axis=-1, keepdims=True)
            gate_col = jnp.sum(jnp.where(lane == e, gates_ref[ts, :], 0.0), axis=-1, keepdims=True)
            slot_id = lax.broadcasted_iota(jnp.int32, (tr, height), 1).astype(F32) + float(first)
            sel = jnp.where(pos_col - win[tb].astype(F32) == slot_id, 1.0, 0.0).astype(BF16)
            rows = buf_ref[b % 2, pl.ds(pl.multiple_of(seg_start + win[tb] + first, SEG_ALIGN), height), :]
            out_ref[ts, :] += layer_gate * (gate_col * _dot(sel, rows))

        @pl.when(all_small)
        def _():
            for tb in range(TOK_BLOCKS):
                scatter(tb, 0, SMALL_WINDOW)

        @pl.when(jnp.logical_not(all_small))
        def _():
            for tb in range(TOK_BLOCKS):
                scatter(tb, 0, tr)
            for tb in range(TOK_BLOCKS):
                @pl.when(bounds[tb + 1] - win[tb] > tr)
                def _():
                    scatter(tb, tr, SPILL_ROWS)
        return seg_start + seg_ref[b * N_EXPERTS + e]
    lax.fori_loop(0, N_EXPERTS, expert, 0)


def _combine(ys, pos, gates, x, mod, bounds, seg, seg_off):
    tokens = x.shape[0]
    grid_spec = pltpu.PrefetchScalarGridSpec(
        num_scalar_prefetch=3,
        grid=(tokens // SEQ,),
        in_specs=[pl.BlockSpec(memory_space=pl.ANY),
                  pl.BlockSpec((SEQ, LANES), lambda b, *_: (b, 0)),
                  pl.BlockSpec((SEQ, LANES), lambda b, *_: (b, 0)),
                  pl.BlockSpec((SEQ, D_MODEL), lambda b, *_: (b, 0)),
                  pl.BlockSpec((None, 6, D_MODEL), lambda b, *_: (b, 0, 0))],
        out_specs=pl.BlockSpec((SEQ, D_MODEL), lambda b, *_: (b, 0)),
        scratch_shapes=[pltpu.VMEM((2, PACKED_ROWS, D_MODEL), BF16),
                        pltpu.SemaphoreType.DMA((2, N_EXPERTS))],
    )
    return pl.pallas_call(
        _combine_kernel,
        out_shape=jax.ShapeDtypeStruct((tokens, D_MODEL), F32),
        grid_spec=grid_spec,
        compiler_params=_cparams(("arbitrary",)),
        name="expert_combine",
    )(bounds, seg, seg_off, ys, pos, gates, x, mod)


def _token_mixer(x, mod, gain, w_in, q_gain, k_gain, ret_gain, bias_masks):
    *qkv, ret_in = _in_proj(x, mod, gain, w_in, q_gain, k_gain)
    return _dilated_attention(qkv, bias_masks), _retention(ret_in, ret_gain)


def _moe_ffn(x, mod, gain, w_router, w1, w3, w2):
    h, gates, pos, post, start, cnt = _router(x, mod, gain, w_router)
    bounds, seg, seg_off, g_expert, g_rows, n_groups = _moe_plan(start, cnt)
    hs = _dispatch(h, post, bounds, seg, seg_off, n_groups * GROUP_ROWS)
    ys = _experts(hs, g_expert, g_rows, n_groups, w1, w3, w2)
    return _combine(ys, pos, gates, x, mod, bounds, seg, seg_off)


def kernel(x, c, rel_bias_table, norm_mix, norm_ffn, w_mod, b_mod, w_in, q_gain, k_gain, ret_gain, w_out,
           ffn_w1, ffn_w3, ffn_w2, moe_router, moe_w1, moe_w3, moe_w2):
    batch, seq, d_model = x.shape
    assert (seq, d_model) == (SEQ, D_MODEL)
    depth = w_mod.shape[0]
    mods = _modulation(c, w_mod, b_mod).reshape(depth, batch, 6, D_MODEL)
    bias_masks = _bias_masks(rel_bias_table)
    xt = x.reshape(batch * seq, d_model)
    for layer in range(depth):
        mod = mods[layer]
        attn, ret = _token_mixer(xt, mod, norm_mix[layer], w_in[layer], q_gain[layer], k_gain[layer],
                                 ret_gain[layer], bias_masks)
        i = layer // 2
        if layer % 2 == 0:
            xt = _out_proj_dense_ffn(attn, ret, xt, mod, w_out[layer], norm_ffn[layer],
                                     ffn_w1[i], ffn_w3[i], ffn_w2[i])
        else:
            xt = _out_proj(attn, ret, xt, mod, w_out[layer])
            xt = _moe_ffn(xt, mod, norm_ffn[layer], moe_router[i], moe_w1[i], moe_w3[i], moe_w2[i])
    return xt.reshape(batch, seq, d_model)
```

```python
import functools
import math

import jax
import jax.numpy as jnp
import numpy as np
from jax import lax
from jax.experimental import pallas as pl
from jax.experimental.pallas import tpu as pltpu

D_MODEL = 1024
SEQ = 2048
ATTN_HEADS = 8
ATTN_HEAD_DIM = 64
ATTN_WIDTH = ATTN_HEADS * ATTN_HEAD_DIM
DILATED_PATTERNS = ((128, 1), (512, 4), (2048, 16))
BLOCK = 128
NUM_BUCKETS = 32
MAX_DISTANCE = 2048
RET_HEADS = 4
RET_KEY_DIM = 64
RET_VALUE_DIM = 128
RET_WIDTH = RET_HEADS * RET_VALUE_DIM
RET_QK_WIDTH = RET_HEADS * RET_KEY_DIM
RET_CHUNK = 128
ROPE_BASE = 10000.0
IN_WIDTH = 3 * ATTN_WIDTH + 2 * RET_QK_WIDTH + 2 * RET_WIDTH
RET_IN_WIDTH = IN_WIDTH - 3 * ATTN_WIDTH
N_EXPERTS = 8
EPS = 1e-6
NEG_INF = -1e30

LANES = 128
VMEM_LIMIT = 60 * 1024 * 1024

BF16 = jnp.bfloat16
F32 = jnp.float32

TM_PROJ = 256
TM_FFN = 512
MXU_COLS = 256
TR_MOE = 256
TF_MOE = 512
assert TM_PROJ == TR_MOE
RET_ROWS = 512
SUBLANES = 8


def _cparams(sem):
    return pltpu.CompilerParams(dimension_semantics=sem, vmem_limit_bytes=VMEM_LIMIT)


def _dot(a, b):
    return jnp.dot(a, b, preferred_element_type=F32)


def _dot_nt(a, b):
    return lax.dot_general(a, b, (((1,), (1,)), ((), ())), preferred_element_type=F32)


def _dot_tn(a, b):
    return lax.dot_general(a, b, (((0,), (0,)), ((), ())), preferred_element_type=F32)


def _split_bf16(v):
    hi = v.astype(BF16)
    lo = (v - hi.astype(F32)).astype(BF16)
    return hi, lo


def _silu(v):
    return v * (1.0 / (1.0 + jnp.exp(-v)))


def _modulated_norm(x, gain, scale, shift):
    ms = jnp.mean(x * x, axis=-1, keepdims=True)
    y = x * lax.rsqrt(ms + EPS) * gain
    return y * (1.0 + scale) + shift


def _mod_kernel(c_ref, w_ref, b_ref, o_ref):
    ca = _silu(c_ref[...]).astype(BF16)
    o_ref[...] = _dot(ca, w_ref[...].astype(BF16)) + b_ref[...]


def _modulation(c, w_mod, b_mod):
    depth, _, width = w_mod.shape
    batch = c.shape[0]
    tn = 1536
    return pl.pallas_call(
        _mod_kernel,
        out_shape=jax.ShapeDtypeStruct((depth, batch, width), F32),
        grid=(depth, width // tn),
        in_specs=[
            pl.BlockSpec((batch, D_MODEL), lambda l, n: (0, 0)),
            pl.BlockSpec((None, D_MODEL, tn), lambda l, n: (l, 0, n)),
            pl.BlockSpec((None, 1, tn), lambda l, n: (l, 0, n)),
        ],
        out_specs=pl.BlockSpec((None, batch, tn), lambda l, n: (l, 0, n)),
        compiler_params=_cparams(("arbitrary", "arbitrary")),
        name="adaln_modulation",
    )(c, w_mod, b_mod.reshape(depth, 1, width))


def _bias_kernel(table_ref, bucket_ref, o_ref):
    h = pl.program_id(1)
    bucket = bucket_ref[...]
    acc = jnp.full(bucket.shape, NEG_INF, F32)
    for b in range(NUM_BUCKETS):
        acc = jnp.where(bucket == b, table_ref[b, h], acc)
    o_ref[...] = acc


BIAS_FULL = {1: 0, 4: 2}
BIAS_FIRST = {1: 1, 4: 3, 16: 4}
N_BIAS_SETS = 5


def _bias_masks(rel_bias_table):
    i = jnp.arange(BLOCK)[:, None]
    j = jnp.arange(2 * BLOCK)[None, :]
    max_exact = NUM_BUCKETS // 2

    def bucket_of(rel, dilation, w_sub, exists):
        n = jnp.maximum(rel * dilation, 0)
        nf = jnp.maximum(n.astype(F32), float(max_exact))
        large = max_exact + (jnp.log(nf / max_exact) / math.log(MAX_DISTANCE / max_exact)
                             * (NUM_BUCKETS - max_exact)).astype(jnp.int32)
        large = jnp.minimum(large, NUM_BUCKETS - 1)
        bucket = jnp.where(n < max_exact, n, large)
        allowed = (rel >= 0) & (rel <= w_sub) & exists
        return jnp.where(allowed, bucket, -1)

    sets = [None] * N_BIAS_SETS
    for window, dilation in DILATED_PATTERNS:
        w_sub = window // dilation
        if dilation in BIAS_FULL:
            sets[BIAS_FULL[dilation]] = bucket_of(i - j + BLOCK, dilation, w_sub, j >= 0)
        sets[BIAS_FIRST[dilation]] = bucket_of(i - j, dilation, w_sub, j < BLOCK)
    buckets = jnp.stack(sets).astype(jnp.int32)
    return pl.pallas_call(
        _bias_kernel,
        out_shape=jax.ShapeDtypeStruct((N_BIAS_SETS, ATTN_HEADS // 2, 2 * BLOCK, 2 * BLOCK), F32),
        grid=(N_BIAS_SETS, ATTN_HEADS),
        in_specs=[
            pl.BlockSpec(memory_space=pltpu.SMEM),
            pl.BlockSpec((None, BLOCK, 2 * BLOCK), lambda p, h: (p, 0, 0)),
        ],
        out_specs=pl.BlockSpec((None, None, BLOCK, 2 * BLOCK), lambda p, h: (p, h // 2, h % 2, 0)),
        compiler_params=_cparams(("arbitrary", "arbitrary")),
        name="relative_bias_masks",
    )(rel_bias_table, buckets)


HALF = ATTN_WIDTH // 2


def _in_proj_kernel(x_ref, mod_ref, gain_ref, w_ref, qg_ref, kg_ref, grp_ref,
                    q1_ref, k1_ref, v1_ref, q4_ref, k4_ref, v4_ref, q16_ref, k16_ref, v16_ref, r_ref,
                    perm_ref):
    mod = mod_ref[...]
    h = _modulated_norm(x_ref[...], gain_ref[...], mod[1:2], mod[0:1]).astype(BF16)
    proj = _dot(h, w_ref[...].astype(BF16))
    grp = grp_ref[...]
    tm = proj.shape[0]

    def head_norm(t, gain):
        hi, lo = _split_bf16(t * t)
        ss = _dot(hi, grp) + _dot(lo, grp)
        return t * lax.rsqrt(ss * (1.0 / ATTN_HEAD_DIM) + EPS) * gain

    def emit(t, o1_ref, o4_ref, o16_ref):
        o1_ref[...] = t.astype(BF16)
        for j in range(ATTN_WIDTH // LANES):
            perm_ref[j] = t[:, j * LANES:(j + 1) * LANES]
        for dil, o_ref in ((4, o4_ref), (16, o16_ref)):
            for hf in range(2):
                for r in range(dil):
                    for jj in range(HALF // LANES):
                        c0 = (hf * dil + r) * HALF + jj * LANES
                        o_ref[:, c0:c0 + LANES] = perm_ref[hf * (HALF // LANES) + jj,
                                                           pl.ds(r, tm // dil, stride=dil), :].astype(BF16)

    emit(head_norm(proj[:, :ATTN_WIDTH], qg_ref[...]) * (ATTN_HEAD_DIM ** -0.5), q1_ref, q4_ref, q16_ref)
    emit(head_norm(proj[:, ATTN_WIDTH:2 * ATTN_WIDTH], kg_ref[...]), k1_ref, k4_ref, k16_ref)
    emit(proj[:, 2 * ATTN_WIDTH:3 * ATTN_WIDTH], v1_ref, v4_ref, v16_ref)
    r_ref[...] = proj[:, 3 * ATTN_WIDTH:]


def _in_proj(x, mod, gain, w_in, q_gain, k_gain):
    tokens = x.shape[0]
    tm = TM_PROJ
    per_seq = SEQ // tm
    grp = np.kron(np.eye(ATTN_HEADS), np.ones((ATTN_HEAD_DIM, ATTN_HEAD_DIM))).astype(np.float32)
    row = lambda i: (i, 0)
    const = lambda i: (0, 0)
    layouts = []
    for dil in (1, 4, 16):
        shape = jax.ShapeDtypeStruct((tokens // dil, dil * ATTN_WIDTH), BF16)
        spec = pl.BlockSpec((tm // dil, dil * ATTN_WIDTH), row)
        layouts.append(((shape,) * 3, (spec,) * 3))
    out_shape = sum((s for s, _ in layouts), ()) + (jax.ShapeDtypeStruct((tokens, RET_IN_WIDTH), F32),)
    out_specs = sum((s for _, s in layouts), ()) + (pl.BlockSpec((tm, RET_IN_WIDTH), row),)
    return pl.pallas_call(
        _in_proj_kernel,
        out_shape=out_shape,
        grid=(tokens // tm,),
        in_specs=[
            pl.BlockSpec((tm, D_MODEL), row),
            pl.BlockSpec((None, 6, D_MODEL), lambda i: (i // per_seq, 0, 0)),
            pl.BlockSpec((1, D_MODEL), const),
            pl.BlockSpec((D_MODEL, IN_WIDTH), const, pipeline_mode=pl.Buffered(1)),
            pl.BlockSpec((1, ATTN_WIDTH), const),
            pl.BlockSpec((1, ATTN_WIDTH), const),
            pl.BlockSpec((ATTN_WIDTH, ATTN_WIDTH), const),
        ],
        out_specs=out_specs,
        scratch_shapes=[pltpu.VMEM((ATTN_WIDTH // LANES, tm, LANES), F32)],
        compiler_params=_cparams(("arbitrary",)),
        name="in_projection",
    )(x, mod, gain.reshape(1, D_MODEL), w_in,
      jnp.tile(q_gain, ATTN_HEADS).reshape(1, ATTN_WIDTH),
      jnp.tile(k_gain, ATTN_HEADS).reshape(1, ATTN_WIDTH),
      jnp.asarray(grp, BF16))


PAIRS_PER_HALF = ATTN_HEADS // 4
GROUP = 4


def _pair_scores(qp, kp, vp, bias2, masks, low):
    q2 = jnp.concatenate([qp * masks[0], qp * masks[1]], axis=0)
    s = _dot_nt(q2, kp) + bias2
    m = jnp.max(s, axis=-1, keepdims=True)
    p = jnp.exp(s - m)
    den = jnp.sum(p, axis=-1, keepdims=True)
    pv = _dot(p.astype(BF16), vp)
    pick = lambda t: jnp.where(low, t[:BLOCK], t[BLOCK:])
    return pick(pv), pick(m), pick(den)


def _attn_kernel(q1_ref, k1_ref, v1_ref, q4_ref, k4_ref, v4_ref, q16_ref, k16_ref, v16_ref, bm_ref,
                 o_ref, acc_ref, max_ref, den_ref):
    lane = lax.broadcasted_iota(jnp.int32, (BLOCK, LANES), 1)
    low = lane < ATTN_HEAD_DIM
    masks = (jnp.where(low, 1.0, 0.0).astype(BF16), jnp.where(low, 0.0, 1.0).astype(BF16))

    def block(q_ref, k_ref, v_ref, c0, q0, w0, width, bias_set, rows, first):
        for p in range(PAIRS_PER_HALF):
            cs = slice(c0 + p * LANES, c0 + (p + 1) * LANES)
            acc, m, den = _pair_scores(q_ref[pl.ds(q0, BLOCK), cs], k_ref[pl.ds(w0, width), cs],
                                       v_ref[pl.ds(w0, width), cs], bm_ref[bias_set, p, :, 0:width],
                                       masks, low)
            if not first:
                m_old = max_ref[p, rows, :]
                m_new = jnp.maximum(m_old, m)
                a, b = jnp.exp(m_old - m_new), jnp.exp(m - m_new)
                den = den_ref[p, rows, :] * a + den * b
                acc = acc_ref[p, rows, :] * a + acc * b
                m = m_new
            max_ref[p, rows, :] = m
            den_ref[p, rows, :] = den
            acc_ref[p, rows, :] = acc

    def d1_group(g, carry):
        for u in range(GROUP):
            n = g * GROUP + u
            q0 = pl.multiple_of(n * BLOCK, BLOCK)
            w0 = pl.multiple_of(jnp.maximum(n - 1, 0) * BLOCK, BLOCK)
            bias_set = jnp.where(n == 0, BIAS_FIRST[1], BIAS_FULL[1])
            block(q1_ref, k1_ref, v1_ref, 0, q0, w0, 2 * BLOCK, bias_set, pl.ds(q0, BLOCK), True)
        return carry
    lax.fori_loop(0, SEQ // BLOCK // GROUP, d1_group, 0)

    for r in range(4):
        for n in range(SEQ // 4 // BLOCK):
            block(q4_ref, k4_ref, v4_ref, r * HALF, n * BLOCK, max(n - 1, 0) * BLOCK, 2 * BLOCK,
                  BIAS_FIRST[4] if n == 0 else BIAS_FULL[4],
                  pl.ds(r + 4 * BLOCK * n, BLOCK, stride=4), False)

    for r in range(16):
        block(q16_ref, k16_ref, v16_ref, r * HALF, 0, 0, BLOCK, BIAS_FIRST[16],
              pl.ds(r, BLOCK, stride=16), False)

    for n in range(SEQ // BLOCK):
        rows = slice(n * BLOCK, (n + 1) * BLOCK)
        for p in range(PAIRS_PER_HALF):
            o_ref[rows, p * LANES:(p + 1) * LANES] = (acc_ref[p, rows, :] / den_ref[p, rows, :]).astype(BF16)


def _dilated_attention(qkv, bias_masks):
    tokens = qkv[0].shape[0]
    batch = tokens // SEQ
    specs = []
    for dil in (1, 4, 16):
        specs += [pl.BlockSpec((SEQ // dil, dil * HALF), lambda b, hf: (b, hf))] * 3
    state = pltpu.VMEM((PAIRS_PER_HALF, SEQ, LANES), F32)
    return pl.pallas_call(
        _attn_kernel,
        out_shape=jax.ShapeDtypeStruct((tokens, ATTN_WIDTH), BF16),
        grid=(batch, 2),
        in_specs=specs + [pl.BlockSpec((N_BIAS_SETS, PAIRS_PER_HALF, 2 * BLOCK, 2 * BLOCK),
                                       lambda b, hf: (0, hf, 0, 0))],
        out_specs=pl.BlockSpec((SEQ, HALF), lambda b, hf: (b, hf)),
        scratch_shapes=[state, state, state],
        compiler_params=_cparams(("arbitrary", "arbitrary")),
        name="dilated_attention",
    )(*qkv, bias_masks)


def _retention_kernel(r_ref, cos_ref, sin_ref, dmask_ref, qdec_ref, kdec_ref, cdec_ref, gain_ref,
                      o_ref, state_ref):
    @pl.when(pl.program_id(1) == 0)
    def _():
        state_ref[...] = jnp.zeros_like(state_ref)

    lane = lax.broadcasted_iota(jnp.int32, (RET_CHUNK, LANES), 1)
    low = lane < RET_KEY_DIM
    first_half = (lane % RET_KEY_DIM) < (RET_KEY_DIM // 2)

    def rotate(t, cos, sin):
        partner = jnp.where(first_half, pltpu.roll(t, LANES - RET_KEY_DIM // 2, 1),
                            pltpu.roll(t, RET_KEY_DIM // 2, 1))
        return t * cos + partner * sin

    for c in range(RET_ROWS // RET_CHUNK):
        rows = slice(c * RET_CHUNK, (c + 1) * RET_CHUNK)
        for hp in range(RET_HEADS // 2):
            qs = slice(hp * LANES, (hp + 1) * LANES)
            ks = slice(RET_QK_WIDTH + hp * LANES, RET_QK_WIDTH + (hp + 1) * LANES)
            cos, sin = cos_ref[rows, qs], sin_ref[rows, qs]
            q_pair = rotate(r_ref[rows, qs], cos, sin) * (RET_KEY_DIM ** -0.5)
            k_pair = rotate(r_ref[rows, ks], cos, sin)
            for hh in range(2):
                head = 2 * hp + hh
                vs = slice(2 * RET_QK_WIDTH + head * LANES, 2 * RET_QK_WIDTH + (head + 1) * LANES)
                gs = slice(2 * RET_QK_WIDTH + RET_WIDTH + head * LANES,
                           2 * RET_QK_WIDTH + RET_WIDTH + (head + 1) * LANES)
                keep = low if hh == 0 else jnp.logical_not(low)
                qm = jnp.where(keep, q_pair, 0.0)
                vb = r_ref[rows, vs].astype(BF16)
                state = state_ref[head]
                inner = _dot_nt(qm.astype(BF16), k_pair.astype(BF16)) * dmask_ref[head]
                y = _dot(inner.astype(BF16), vb)
                y = y + _dot((qm * qdec_ref[head]).astype(BF16), state.astype(BF16))
                state_ref[head] = state * cdec_ref[head] + _dot_tn((k_pair * kdec_ref[head]).astype(BF16), vb)
                mu = jnp.mean(y, axis=-1, keepdims=True)
                yc = y - mu
                var = jnp.mean(yc * yc, axis=-1, keepdims=True)
                yn = yc * lax.rsqrt(var + EPS) * gain_ref[:, head * LANES:(head + 1) * LANES]
                o_ref[rows, head * LANES:(head + 1) * LANES] = (_silu(r_ref[rows, gs]) * yn).astype(BF16)


def _retention_tables():
    half = RET_KEY_DIM // 2
    pos = jnp.arange(SEQ, dtype=F32)
    inv = ROPE_BASE ** (-jnp.arange(half, dtype=F32) / half)
    ang = pos[:, None] * inv[None, :]
    cos, sin = jnp.cos(ang), jnp.sin(ang)
    cos_full = jnp.tile(jnp.concatenate([cos, cos], axis=-1), (1, RET_HEADS))
    sin_signed = jnp.tile(jnp.concatenate([-sin, sin], axis=-1), (1, RET_HEADS))
    log_g = jnp.log(1.0 - 2.0 ** (-5.0 - jnp.arange(RET_HEADS, dtype=F32)))
    idx = jnp.arange(RET_CHUNK, dtype=F32)
    diff = idx[:, None] - idx[None, :]
    dmask = jnp.where(diff >= 0, jnp.exp(jnp.maximum(diff, 0.0)[None] * log_g[:, None, None]), 0.0)
    q_decay = jnp.exp((idx + 1.0)[None, :] * log_g[:, None])[..., None]
    k_decay = jnp.exp((RET_CHUNK - 1.0 - idx)[None, :] * log_g[:, None])[..., None]
    chunk_decay = jnp.exp(RET_CHUNK * log_g)[:, None, None]
    full = (RET_HEADS, RET_CHUNK, LANES)
    return (cos_full, sin_signed, dmask, jnp.broadcast_to(q_decay, full),
            jnp.broadcast_to(k_decay, full), jnp.broadcast_to(chunk_decay, full))


def _retention(ret_in, ret_gain):
    tokens = ret_in.shape[0]
    batch = tokens // SEQ
    per_seq = SEQ // RET_ROWS
    cos, sin, dmask, qdec, kdec, cdec = _retention_tables()
    tab = pl.BlockSpec((RET_ROWS, RET_QK_WIDTH), lambda b, j: (j, 0))
    const3 = pl.BlockSpec((RET_HEADS, RET_CHUNK, LANES), lambda b, j: (0, 0, 0))
    return pl.pallas_call(
        _retention_kernel,
        out_shape=jax.ShapeDtypeStruct((tokens, RET_WIDTH), BF16),
        grid=(batch, per_seq),
        in_specs=[
            pl.BlockSpec((RET_ROWS, RET_IN_WIDTH), lambda b, j: (b * per_seq + j, 0)),
            tab, tab, const3, const3, const3, const3,
            pl.BlockSpec((1, RET_WIDTH), lambda b, j: (0, 0)),
        ],
        out_specs=pl.BlockSpec((RET_ROWS, RET_WIDTH), lambda b, j: (b * per_seq + j, 0)),
        scratch_shapes=[pltpu.VMEM((RET_HEADS, LANES, RET_VALUE_DIM), F32)],
        compiler_params=_cparams(("arbitrary", "arbitrary")),
        name="retention",
    )(ret_in, cos, sin, dmask, qdec, kdec, cdec, ret_gain.reshape(1, RET_WIDTH))


def _swiglu_chunks(h, w1_ref, w3_ref, w2_ref):
    d_ff = w1_ref.shape[-1]
    total = None
    for c0 in range(0, d_ff, MXU_COLS):
        c1 = min(c0 + MXU_COLS, d_ff)
        z = (_silu(_dot(h, w1_ref[:, c0:c1].astype(BF16))) * _dot(h, w3_ref[:, c0:c1].astype(BF16))).astype(BF16)
        part = _dot(z, w2_ref[c0:c1, :].astype(BF16))
        total = part if total is None else total + part
    return total


def _out_proj_ffn_kernel(attn_ref, ret_ref, x_ref, mod_ref, wo_ref, gain_ref, w1_ref, w3_ref, w2_ref, out_ref):
    mod = mod_ref[...]
    mix = (_dot(attn_ref[...], wo_ref[:ATTN_WIDTH, :].astype(BF16))
           + _dot(ret_ref[...], wo_ref[ATTN_WIDTH:, :].astype(BF16)))
    x = x_ref[...] + mod[2:3, :] * mix
    h = _modulated_norm(x, gain_ref[...], mod[4:5], mod[3:4]).astype(BF16)
    out_ref[...] = x + mod[5:6, :] * _swiglu_chunks(h, w1_ref, w3_ref, w2_ref)


def _out_proj_dense_ffn(attn, ret, x, mod, w_out, gain, w1, w3, w2):
    tokens = x.shape[0]
    d_ff = w1.shape[1]
    tm = TM_FFN
    per_seq = SEQ // tm
    row = lambda i: (i, 0)
    resident = pl.Buffered(1)
    return pl.pallas_call(
        _out_proj_ffn_kernel,
        out_shape=jax.ShapeDtypeStruct((tokens, D_MODEL), F32),
        grid=(tokens // tm,),
        in_specs=[
            pl.BlockSpec((tm, ATTN_WIDTH), row),
            pl.BlockSpec((tm, RET_WIDTH), row),
            pl.BlockSpec((tm, D_MODEL), row),
            pl.BlockSpec((None, 6, D_MODEL), lambda i: (i // per_seq, 0, 0)),
            pl.BlockSpec((D_MODEL, D_MODEL), lambda i: (0, 0), pipeline_mode=resident),
            pl.BlockSpec((1, D_MODEL), lambda i: (0, 0)),
            pl.BlockSpec((D_MODEL, d_ff), lambda i: (0, 0), pipeline_mode=resident),
            pl.BlockSpec((D_MODEL, d_ff), lambda i: (0, 0), pipeline_mode=resident),
            pl.BlockSpec((d_ff, D_MODEL), lambda i: (0, 0), pipeline_mode=resident),
        ],
        out_specs=pl.BlockSpec((tm, D_MODEL), row),
        compiler_params=_cparams(("arbitrary",)),
        name="out_projection_dense_swiglu",
    )(attn, ret, x, mod, w_out, gain.reshape(1, D_MODEL), w1, w3, w2)


def _router_kernel(attn_ref, ret_ref, x_ref, mod_ref, wo_ref, gain_ref, wr_ref,
                   xo_ref, h_ref, gates_ref, pos_ref, post_ref, start_ref, cnt_ref, carry_ref, *, tiles_per_seq):
    i = pl.program_id(0)

    @pl.when(i % tiles_per_seq == 0)
    def _():
        carry_ref[...] = jnp.zeros_like(carry_ref)

    mod = mod_ref[...]
    mix = (_dot(attn_ref[...], wo_ref[:ATTN_WIDTH, :].astype(BF16))
           + _dot(ret_ref[...], wo_ref[ATTN_WIDTH:, :].astype(BF16)))
    x = x_ref[...] + mod[2:3, :] * mix
    xo_ref[...] = x
    h = _modulated_norm(x, gain_ref[...], mod[4:5], mod[3:4]).astype(BF16)
    h_ref[...] = h
    tm = h.shape[0]
    lane = lax.broadcasted_iota(jnp.int32, (tm, LANES), 1).astype(F32)
    logits = jnp.where(lane < N_EXPERTS, _dot(h, wr_ref[...]), -jnp.inf)
    m1 = jnp.max(logits, axis=-1, keepdims=True)
    i1 = jnp.min(jnp.where(logits == m1, lane, float(LANES)), axis=-1, keepdims=True)
    rest = jnp.where(lane == i1, -jnp.inf, logits)
    m2 = jnp.max(rest, axis=-1, keepdims=True)
    i2 = jnp.min(jnp.where(rest == m2, lane, float(LANES)), axis=-1, keepdims=True)
    e2 = jnp.exp(m2 - m1)
    g1 = 1.0 / (1.0 + e2)
    g2 = e2 / (1.0 + e2)
    gates_ref[...] = jnp.where(lane == i1, g1, 0.0) + jnp.where(lane == i2, g2, 0.0)
    chosen = (lane == i1) | (lane == i2)
    onehot = jnp.where(chosen, 1.0, 0.0)
    r = lax.broadcasted_iota(jnp.int32, (tm, tm), 0)
    c = lax.broadcasted_iota(jnp.int32, (tm, tm), 1)
    tril = jnp.where(c <= r, 1.0, 0.0).astype(BF16)
    incl = _dot(tril, onehot.astype(BF16))
    carry = carry_ref[0:1, :]
    start_ref[...] = carry_ref[...]
    pos = jnp.where(chosen, incl - 1.0 + carry, -1.0)
    pos_ref[...] = pos
    post_ref[...] = pos.T[:N_EXPERTS, :]
    total = carry + incl[tm - 1:tm, :]
    carry_ref[...] = jnp.broadcast_to(total, carry_ref.shape)
    cnt_ref[...] = jnp.broadcast_to(total, cnt_ref.shape)


def _out_proj_router(attn, ret, x, mod, w_out, gain, w_router):
    tokens = x.shape[0]
    batch = tokens // SEQ
    tm = TM_PROJ
    per_seq = SEQ // tm
    wr = jnp.zeros((D_MODEL, LANES), BF16).at[:, :N_EXPERTS].set(w_router.astype(BF16))
    row = lambda i: (i, 0)
    return pl.pallas_call(
        functools.partial(_router_kernel, tiles_per_seq=per_seq),
        out_shape=(
            jax.ShapeDtypeStruct((tokens, D_MODEL), F32),
            jax.ShapeDtypeStruct((tokens, D_MODEL), BF16),
            jax.ShapeDtypeStruct((tokens, LANES), F32),
            jax.ShapeDtypeStruct((tokens, LANES), F32),
            jax.ShapeDtypeStruct((N_EXPERTS, tokens), F32),
            jax.ShapeDtypeStruct((tokens // tm, 8, LANES), F32),
            jax.ShapeDtypeStruct((batch, 8, LANES), F32),
        ),
        grid=(tokens // tm,),
        in_specs=[
            pl.BlockSpec((tm, ATTN_WIDTH), row),
            pl.BlockSpec((tm, RET_WIDTH), row),
            pl.BlockSpec((tm, D_MODEL), row),
            pl.BlockSpec((None, 6, D_MODEL), lambda i: (i // per_seq, 0, 0)),
            pl.BlockSpec((D_MODEL, D_MODEL), lambda i: (0, 0)),
            pl.BlockSpec((1, D_MODEL), lambda i: (0, 0)),
            pl.BlockSpec((D_MODEL, LANES), lambda i: (0, 0)),
        ],
        out_specs=(
            pl.BlockSpec((tm, D_MODEL), row),
            pl.BlockSpec((tm, D_MODEL), row),
            pl.BlockSpec((tm, LANES), row),
            pl.BlockSpec((tm, LANES), row),
            pl.BlockSpec((N_EXPERTS, tm), lambda i: (0, i)),
            pl.BlockSpec((None, 8, LANES), lambda i: (i, 0, 0)),
            pl.BlockSpec((None, 8, LANES), lambda i: (i // per_seq, 0, 0)),
        ),
        scratch_shapes=[pltpu.VMEM((8, LANES), F32)],
        compiler_params=_cparams(("arbitrary",)),
        name="out_projection_router",
    )(attn, ret, x, mod, w_out, gain.reshape(1, D_MODEL), wr)


TOK_BLOCKS = SEQ // TR_MOE
TAIL_TILES = (64, 128, TR_MOE)
SPILL_ROWS = 16
SMALL_WINDOW = 128
SEG_ALIGN = 16
GROUP_ROWS = 3072
PACKED_ROWS = 2 * SEQ + N_EXPERTS * SEG_ALIGN + 2 * TR_MOE
BIG_CHUNK = 128
STAGE_ROWS = SEQ + TR_MOE
N_BOUNDS = TOK_BLOCKS + 1


def _moe_plan(start, cnt):
    batch = cnt.shape[0]
    counts = cnt[:, 0, :N_EXPERTS].astype(jnp.int32)
    bounds = jnp.concatenate([start[:, 0, :N_EXPERTS].reshape(batch, TOK_BLOCKS, N_EXPERTS).astype(jnp.int32),
                              counts[:, None, :]], axis=1)
    seg = (counts + (SEG_ALIGN - 1)) // SEG_ALIGN * SEG_ALIGN
    rows_e = jnp.sum(seg, axis=0)
    groups_e = (rows_e + (GROUP_ROWS - 1)) // GROUP_ROWS
    first_group = jnp.cumsum(groups_e) - groups_e
    seg_off = first_group[None, :] * GROUP_ROWS + jnp.cumsum(seg, axis=0) - seg
    max_rows = batch * (2 * SEQ + N_EXPERTS * (SEG_ALIGN - 1))
    n_groups = max_rows // GROUP_ROWS + N_EXPERTS
    g = jnp.arange(n_groups)
    g_expert = jnp.minimum(jnp.sum(g[:, None] >= jnp.cumsum(groups_e)[None, :], axis=1), N_EXPERTS - 1)
    g_rows = jnp.clip(rows_e[g_expert] - (g - first_group[g_expert]) * GROUP_ROWS, 0, GROUP_ROWS)
    g_rows = jnp.where(g < jnp.sum(groups_e), g_rows, 0)
    return (bounds.reshape(-1), seg.reshape(-1), seg_off.reshape(-1),
            g_expert.astype(jnp.int32), g_rows.astype(jnp.int32), n_groups)


def _segment_copies(seg, make_copy, action):
    n_big = lax.shift_right_logical(seg, BIG_CHUNK.bit_length() - 1)

    def big(i, carry):
        getattr(make_copy(pl.multiple_of(i * BIG_CHUNK, BIG_CHUNK), BIG_CHUNK), action)()
        return carry
    lax.fori_loop(0, n_big, big, 0)
    rest0 = n_big * BIG_CHUNK
    n_small = lax.shift_right_logical(seg - rest0, SEG_ALIGN.bit_length() - 1)

    def small(i, carry):
        getattr(make_copy(pl.multiple_of(rest0 + i * SEG_ALIGN, SEG_ALIGN), SEG_ALIGN), action)()
        return carry
    lax.fori_loop(0, n_small, small, 0)


def _block_windows(tbl_ref, b, e, align):
    base = b * N_BOUNDS * N_EXPERTS + e
    bounds = [tbl_ref[base + tb * N_EXPERTS] for tb in range(N_BOUNDS)]
    shift = align.bit_length() - 1
    win = [pl.multiple_of(lax.shift_left(lax.shift_right_logical(s, shift), shift), align) for s in bounds[:-1]]
    all_small = functools.reduce(jnp.logical_and,
                                 [bounds[tb + 1] - win[tb] <= SMALL_WINDOW for tb in range(TOK_BLOCKS)])
    return bounds, win, all_small


def _dispatch_kernel(tbl_ref, seg_ref, off_ref, h_ref, post_ref, rows_hbm, acc_ref, stage_ref, sem):
    b = pl.program_id(0)
    tr = TR_MOE

    def copies(e, action):
        slot = e % 2
        off = off_ref[b * N_EXPERTS + e]

        def make_copy(r0, rows):
            return pltpu.make_async_copy(stage_ref.at[slot, pl.ds(r0, rows), :],
                                         rows_hbm.at[pl.ds(pl.multiple_of(off + r0, SEG_ALIGN), rows), :],
                                         sem.at[slot])
        _segment_copies(seg_ref[b * N_EXPERTS + e], make_copy, action)

    def expert(e, carry):
        bounds, win, all_small = _block_windows(tbl_ref, b, e, SUBLANES)
        n_tiles = lax.shift_right_logical(bounds[-1] + (tr - 1), tr.bit_length() - 1)

        def clear(r, c):
            acc_ref[pl.ds(pl.multiple_of(r * tr, tr), tr), :] = jnp.zeros((tr, D_MODEL), F32)
            return c
        lax.fori_loop(0, n_tiles + 2, clear, 0)

        def gather(height):
            slot_id = lax.broadcasted_iota(jnp.int32, (height, tr), 0).astype(F32)
            for tb in range(TOK_BLOCKS):
                ts = slice(tb * tr, (tb + 1) * tr)
                local = post_ref[pl.ds(e, 1), ts] - win[tb].astype(F32)
                sel = jnp.where(local == slot_id, 1.0, 0.0).astype(BF16)
                acc_ref[pl.ds(win[tb], height), :] += _dot(sel, h_ref[ts, :])

        @pl.when(all_small)
        def _():
            gather(SMALL_WINDOW)

        @pl.when(jnp.logical_not(all_small))
        def _():
            gather(tr + SPILL_ROWS)

        @pl.when(e >= 2)
        def _():
            copies(e - 2, "wait")

        def to_stage(r, c):
            rows = pl.ds(pl.multiple_of(r * tr, tr), tr)
            stage_ref[e % 2, rows, :] = acc_ref[rows, :].astype(BF16)
            return c
        lax.fori_loop(0, n_tiles, to_stage, 0)
        copies(e, "start")
        return carry
    lax.fori_loop(0, N_EXPERTS, expert, 0)
    copies(N_EXPERTS - 2, "wait")
    copies(N_EXPERTS - 1, "wait")


def _dispatch(h, post, bounds, seg, seg_off, total_rows):
    tokens = h.shape[0]
    grid_spec = pltpu.PrefetchScalarGridSpec(
        num_scalar_prefetch=3,
        grid=(tokens // SEQ,),
        in_specs=[pl.BlockSpec((SEQ, D_MODEL), lambda b, *_: (b, 0)),
                  pl.BlockSpec((N_EXPERTS, SEQ), lambda b, *_: (0, b))],
        out_specs=pl.BlockSpec(memory_space=pl.ANY),
        scratch_shapes=[pltpu.VMEM((SEQ + 3 * TR_MOE, D_MODEL), F32),
                        pltpu.VMEM((2, STAGE_ROWS, D_MODEL), BF16),
                        pltpu.SemaphoreType.DMA((2,))],
    )
    return pl.pallas_call(
        _dispatch_kernel,
        out_shape=jax.ShapeDtypeStruct((total_rows, D_MODEL), BF16),
        grid_spec=grid_spec,
        compiler_params=_cparams(("arbitrary",)),
        name="expert_dispatch",
    )(bounds, seg, seg_off, h, post)


def _experts_kernel(ge_ref, rows_ref, hs_ref, w1_ref, w3_ref, w2_ref, ys_ref, acc_ref, wb1_ref, wb3_ref, wb2_ref):
    g, f = pl.program_id(0), pl.program_id(1)
    last_f = pl.num_programs(1) - 1
    tr = TR_MOE
    rows = rows_ref[g]
    n_big = lax.shift_right_logical(rows, tr.bit_length())
    n_full = lax.shift_right_logical(rows, tr.bit_length() - 1)
    rest = rows - n_full * tr
    n_tiles = lax.shift_right_logical(rows + (tr - 1), tr.bit_length() - 1)
    tail0 = pl.multiple_of(n_full * tr, tr)

    @pl.when(rows > 0)
    def _():
        wb1_ref[...] = w1_ref[...].astype(BF16)
        wb3_ref[...] = w3_ref[...].astype(BF16)
        wb2_ref[...] = w2_ref[...].astype(BF16)

    @pl.when(f == 0)
    def _():
        def clear(r, carry):
            acc_ref[pl.ds(pl.multiple_of(r * tr, tr), tr), :] = jnp.zeros((tr, D_MODEL), F32)
            return carry
        lax.fori_loop(0, n_tiles, clear, 0)

    def swiglu(hr, r0, m):
        z = (_silu(_dot(hr, wb1_ref[...])) * _dot(hr, wb3_ref[...])).astype(BF16)
        acc_ref[pl.ds(r0, m), :] += _dot(z, wb2_ref[...])

    def big_tile(r, carry):
        r0 = pl.multiple_of(r * 2 * tr, 2 * tr)
        swiglu(hs_ref[pl.ds(r0, 2 * tr), :], r0, 2 * tr)
        return carry
    lax.fori_loop(0, n_big, big_tile, 0)

    @pl.when(n_full > 2 * n_big)
    def _():
        r0 = pl.multiple_of(n_big * 2 * tr, tr)
        swiglu(hs_ref[pl.ds(r0, tr), :], r0, tr)
    lo = 0
    for m in TAIL_TILES:
        @pl.when((rest > lo) & (rest <= m))
        def _():
            valid = lax.broadcasted_iota(jnp.int32, (m, D_MODEL), 0) < rest
            hr = hs_ref[pl.ds(tail0, m), :]
            swiglu(jnp.where(valid, hr, jnp.zeros_like(hr)), tail0, m)
        lo = m

    @pl.when(f == last_f)
    def _():
        ys_ref[...] = jnp.zeros_like(ys_ref)

        def store(r, carry):
            rr = pl.ds(pl.multiple_of(r * tr, tr), tr)
            ys_ref[rr, :] = acc_ref[rr, :].astype(BF16)
            return carry
        lax.fori_loop(0, n_tiles, store, 0)


def _experts(hs, g_expert, g_rows, n_groups, w1, w3, w2):
    d_ff = w1.shape[2]
    tf = TF_MOE
    n_f = d_ff // tf
    chunk = lambda g, f, gr: jnp.where(gr[g] > 0, f, n_f - 1)
    grid_spec = pltpu.PrefetchScalarGridSpec(
        num_scalar_prefetch=2,
        grid=(n_groups, n_f),
        in_specs=[
            pl.BlockSpec((GROUP_ROWS, D_MODEL), lambda g, f, ge, gr: (g, 0)),
            pl.BlockSpec((None, D_MODEL, tf), lambda g, f, ge, gr: (ge[g], 0, chunk(g, f, gr))),
            pl.BlockSpec((None, D_MODEL, tf), lambda g, f, ge, gr: (ge[g], 0, chunk(g, f, gr))),
            pl.BlockSpec((None, tf, D_MODEL), lambda g, f, ge, gr: (ge[g], chunk(g, f, gr), 0)),
        ],
        out_specs=pl.BlockSpec((GROUP_ROWS, D_MODEL), lambda g, f, ge, gr: (g, 0)),
        scratch_shapes=[pltpu.VMEM((GROUP_ROWS, D_MODEL), F32),
                        pltpu.VMEM((D_MODEL, tf), BF16), pltpu.VMEM((D_MODEL, tf), BF16),
                        pltpu.VMEM((tf, D_MODEL), BF16)],
    )
    return pl.pallas_call(
        _experts_kernel,
        out_shape=jax.ShapeDtypeStruct(hs.shape, BF16),
        grid_spec=grid_spec,
        compiler_params=_cparams(("arbitrary", "arbitrary")),
        name="expert_swiglu",
    )(g_expert, g_rows, hs, w1, w3, w2)


def _combine_kernel(tbl_ref, seg_ref, off_ref, rows_hbm, pos_ref, gates_ref, x_ref, mod_ref, out_ref, buf_ref, sem):
    b = pl.program_id(0)
    n_seq = pl.num_programs(0)
    tr = TR_MOE

    def all_segments(seq, action):
        def body(e, first):
            off = off_ref[seq * N_EXPERTS + e]
            seg = seg_ref[seq * N_EXPERTS + e]

            def make_copy(r0, rows):
                return pltpu.make_async_copy(
                    rows_hbm.at[pl.ds(pl.multiple_of(off + r0, SEG_ALIGN), rows), :],
                    buf_ref.at[seq % 2, pl.ds(pl.multiple_of(first + r0, SEG_ALIGN), rows), :],
                    sem.at[seq % 2, e])
            _segment_copies(seg, make_copy, action)
            return first + seg
        lax.fori_loop(0, N_EXPERTS, body, 0)

    @pl.when(b == 0)
    def _():
        buf_ref[...] = jnp.zeros_like(buf_ref)
        all_segments(b, "start")

    @pl.when(b + 1 < n_seq)
    def _():
        all_segments(b + 1, "start")

    out_ref[...] = x_ref[...]
    lane = lax.broadcasted_iota(jnp.int32, (tr, LANES), 1)
    layer_gate = mod_ref[5:6, :]
    all_segments(b, "wait")

    def expert(e, seg_start):
        bounds, win, all_small = _block_windows(tbl_ref, b, e, SEG_ALIGN)

        def scatter(tb, first, height):
            ts = slice(tb * tr, (tb + 1) * tr)
            pos_col = jnp.sum(jnp.where(lane == e, pos_ref[ts, :], 0.0), axis=-1, keepdims=True)
            gate_col = jnp.sum(jnp.where(lane == e, gates_ref[ts, :], 0.0), axis=-1, keepdims=True)
            slot_id = lax.broadcasted_iota(jnp.int32, (tr, height), 1).astype(F32) + float(first)
            sel = jnp.where(pos_col - win[tb].astype(F32) == slot_id, 1.0, 0.0).astype(BF16)
            rows = buf_ref[b % 2, pl.ds(pl.multiple_of(seg_start + win[tb] + first, SEG_ALIGN), height), :]
            out_ref[ts, :] += layer_gate * (gate_col * _dot(sel, rows))

        @pl.when(all_small)
        def _():
            for tb in range(TOK_BLOCKS):
                scatter(tb, 0, SMALL_WINDOW)

        @pl.when(jnp.logical_not(all_small))
        def _():
            for tb in range(TOK_BLOCKS):
                scatter(tb, 0, tr)
            for tb in range(TOK_BLOCKS):
                @pl.when(bounds[tb + 1] - win[tb] > tr)
                def _():
                    scatter(tb, tr, SPILL_ROWS)
        return seg_start + seg_ref[b * N_EXPERTS + e]
    lax.fori_loop(0, N_EXPERTS, expert, 0)


def _combine(ys, pos, gates, x, mod, bounds, seg, seg_off):
    tokens = x.shape[0]
    grid_spec = pltpu.PrefetchScalarGridSpec(
        num_scalar_prefetch=3,
        grid=(tokens // SEQ,),
        in_specs=[pl.BlockSpec(memory_space=pl.ANY),
                  pl.BlockSpec((SEQ, LANES), lambda b, *_: (b, 0)),
                  pl.BlockSpec((SEQ, LANES), lambda b, *_: (b, 0)),
                  pl.BlockSpec((SEQ, D_MODEL), lambda b, *_: (b, 0)),
                  pl.BlockSpec((None, 6, D_MODEL), lambda b, *_: (b, 0, 0))],
        out_specs=pl.BlockSpec((SEQ, D_MODEL), lambda b, *_: (b, 0)),
        scratch_shapes=[pltpu.VMEM((2, PACKED_ROWS, D_MODEL), BF16),
                        pltpu.SemaphoreType.DMA((2, N_EXPERTS))],
    )
    return pl.pallas_call(
        _combine_kernel,
        out_shape=jax.ShapeDtypeStruct((tokens, D_MODEL), F32),
        grid_spec=grid_spec,
        compiler_params=_cparams(("arbitrary",)),
        name="expert_combine",
    )(bounds, seg, seg_off, ys, pos, gates, x, mod)


def _token_mixer(x, mod, gain, w_in, q_gain, k_gain, ret_gain, bias_masks):
    *qkv, ret_in = _in_proj(x, mod, gain, w_in, q_gain, k_gain)
    return _dilated_attention(qkv, bias_masks), _retention(ret_in, ret_gain)


def _out_proj_moe_ffn(attn, ret, x, mod, w_out, gain, w_router, w1, w3, w2):
    x, h, gates, pos, post, start, cnt = _out_proj_router(attn, ret, x, mod, w_out, gain, w_router)
    bounds, seg, seg_off, g_expert, g_rows, n_groups = _moe_plan(start, cnt)
    hs = _dispatch(h, post, bounds, seg, seg_off, n_groups * GROUP_ROWS)
    ys = _experts(hs, g_expert, g_rows, n_groups, w1, w3, w2)
    return _combine(ys, pos, gates, x, mod, bounds, seg, seg_off)


def kernel(x, c, rel_bias_table, norm_mix, norm_ffn, w_mod, b_mod, w_in, q_gain, k_gain, ret_gain, w_out,
           ffn_w1, ffn_w3, ffn_w2, moe_router, moe_w1, moe_w3, moe_w2):
    batch, seq, d_model = x.shape
    assert (seq, d_model) == (SEQ, D_MODEL)
    depth = w_mod.shape[0]
    mods = _modulation(c, w_mod, b_mod).reshape(depth, batch, 6, D_MODEL)
    bias_masks = _bias_masks(rel_bias_table)
    xt = x.reshape(batch * seq, d_model)
    for layer in range(depth):
        mod = mods[layer]
        attn, ret = _token_mixer(xt, mod, norm_mix[layer], w_in[layer], q_gain[layer], k_gain[layer],
                                 ret_gain[layer], bias_masks)
        i = layer // 2
        if layer % 2 == 0:
            xt = _out_proj_dense_ffn(attn, ret, xt, mod, w_out[layer], norm_ffn[layer],
                                     ffn_w1[i], ffn_w3[i], ffn_w2[i])
        else:
            xt = _out_proj_moe_ffn(attn, ret, xt, mod, w_out[layer], norm_ffn[layer],
                                   moe_router[i], moe_w1[i], moe_w3[i], moe_w2[i])
    return xt.reshape(batch, seq, d_model)
```

```python
import functools
import math

import jax
import jax.numpy as jnp
import numpy as np
from jax import lax
from jax.experimental import pallas as pl
from jax.experimental.pallas import tpu as pltpu

D_MODEL = 1024
SEQ = 2048
ATTN_HEADS = 8
ATTN_HEAD_DIM = 64
ATTN_WIDTH = ATTN_HEADS * ATTN_HEAD_DIM
DILATED_PATTERNS = ((128, 1), (512, 4), (2048, 16))
BLOCK = 128
NUM_BUCKETS = 32
MAX_DISTANCE = 2048
RET_HEADS = 4
RET_KEY_DIM = 64
RET_VALUE_DIM = 128
RET_WIDTH = RET_HEADS * RET_VALUE_DIM
RET_QK_WIDTH = RET_HEADS * RET_KEY_DIM
RET_CHUNK = 128
ROPE_BASE = 10000.0
IN_WIDTH = 3 * ATTN_WIDTH + 2 * RET_QK_WIDTH + 2 * RET_WIDTH
RET_IN_WIDTH = IN_WIDTH - 3 * ATTN_WIDTH
N_EXPERTS = 8
EPS = 1e-6
NEG_INF = -1e30

LANES = 128
VMEM_LIMIT = 60 * 1024 * 1024

BF16 = jnp.bfloat16
F32 = jnp.float32

TM_PROJ = 256
TM_FFN = 512
MXU_COLS = 256
TR_MOE = 256
TF_MOE = 512
assert TM_PROJ == TR_MOE
RET_ROWS = 512
SUBLANES = 8


def _cparams(sem):
    return pltpu.CompilerParams(dimension_semantics=sem, vmem_limit_bytes=VMEM_LIMIT)


def _dot(a, b):
    return jnp.dot(a, b, preferred_element_type=F32)


def _dot_nt(a, b):
    return lax.dot_general(a, b, (((1,), (1,)), ((), ())), preferred_element_type=F32)


def _dot_tn(a, b):
    return lax.dot_general(a, b, (((0,), (0,)), ((), ())), preferred_element_type=F32)


def _split_bf16(v):
    hi = v.astype(BF16)
    lo = (v - hi.astype(F32)).astype(BF16)
    return hi, lo


def _silu(v):
    return v * (1.0 / (1.0 + jnp.exp(-v)))


def _modulated_norm(x, gain, scale, shift):
    ms = jnp.mean(x * x, axis=-1, keepdims=True)
    y = x * lax.rsqrt(ms + EPS) * gain
    return y * (1.0 + scale) + shift


def _mod_kernel(c_ref, w_ref, b_ref, o_ref):
    ca = _silu(c_ref[...]).astype(BF16)
    o_ref[...] = _dot(ca, w_ref[...].astype(BF16)) + b_ref[...]


def _modulation(c, w_mod, b_mod):
    depth, _, width = w_mod.shape
    batch = c.shape[0]
    tn = 1536
    return pl.pallas_call(
        _mod_kernel,
        out_shape=jax.ShapeDtypeStruct((depth, batch, width), F32),
        grid=(depth, width // tn),
        in_specs=[
            pl.BlockSpec((batch, D_MODEL), lambda l, n: (0, 0)),
            pl.BlockSpec((None, D_MODEL, tn), lambda l, n: (l, 0, n)),
            pl.BlockSpec((None, 1, tn), lambda l, n: (l, 0, n)),
        ],
        out_specs=pl.BlockSpec((None, batch, tn), lambda l, n: (l, 0, n)),
        compiler_params=_cparams(("arbitrary", "arbitrary")),
        name="adaln_modulation",
    )(c, w_mod, b_mod.reshape(depth, 1, width))


def _bias_kernel(table_ref, bucket_ref, o_ref):
    h = pl.program_id(1)
    bucket = bucket_ref[...]
    acc = jnp.full(bucket.shape, NEG_INF, F32)
    for b in range(NUM_BUCKETS):
        acc = jnp.where(bucket == b, table_ref[b, h], acc)
    o_ref[...] = acc


BIAS_FULL = {1: 0, 4: 2}
BIAS_FIRST = {1: 1, 4: 3, 16: 4}
N_BIAS_SETS = 5


def _bias_masks(rel_bias_table):
    i = jnp.arange(BLOCK)[:, None]
    j = jnp.arange(2 * BLOCK)[None, :]
    max_exact = NUM_BUCKETS // 2

    def bucket_of(rel, dilation, w_sub, exists):
        n = jnp.maximum(rel * dilation, 0)
        nf = jnp.maximum(n.astype(F32), float(max_exact))
        large = max_exact + (jnp.log(nf / max_exact) / math.log(MAX_DISTANCE / max_exact)
                             * (NUM_BUCKETS - max_exact)).astype(jnp.int32)
        large = jnp.minimum(large, NUM_BUCKETS - 1)
        bucket = jnp.where(n < max_exact, n, large)
        allowed = (rel >= 0) & (rel <= w_sub) & exists
        return jnp.where(allowed, bucket, -1)

    sets = [None] * N_BIAS_SETS
    for window, dilation in DILATED_PATTERNS:
        w_sub = window // dilation
        if dilation in BIAS_FULL:
            sets[BIAS_FULL[dilation]] = bucket_of(i - j + BLOCK, dilation, w_sub, j >= 0)
        sets[BIAS_FIRST[dilation]] = bucket_of(i - j, dilation, w_sub, j < BLOCK)
    buckets = jnp.stack(sets).astype(jnp.int32)
    return pl.pallas_call(
        _bias_kernel,
        out_shape=jax.ShapeDtypeStruct((N_BIAS_SETS, ATTN_HEADS // 2, 2 * BLOCK, 2 * BLOCK), F32),
        grid=(N_BIAS_SETS, ATTN_HEADS),
        in_specs=[
            pl.BlockSpec(memory_space=pltpu.SMEM),
            pl.BlockSpec((None, BLOCK, 2 * BLOCK), lambda p, h: (p, 0, 0)),
        ],
        out_specs=pl.BlockSpec((None, None, BLOCK, 2 * BLOCK), lambda p, h: (p, h // 2, h % 2, 0)),
        compiler_params=_cparams(("arbitrary", "arbitrary")),
        name="relative_bias_masks",
    )(rel_bias_table, buckets)


HALF = ATTN_WIDTH // 2


def _in_proj_kernel(x_ref, mod_ref, gain_ref, w_ref, qg_ref, kg_ref, grp_ref,
                    q1_ref, k1_ref, v1_ref, q4_ref, k4_ref, v4_ref, q16_ref, k16_ref, v16_ref, r_ref,
                    perm_ref):
    mod = mod_ref[...]
    h = _modulated_norm(x_ref[...], gain_ref[...], mod[1:2], mod[0:1]).astype(BF16)
    proj = _dot(h, w_ref[...].astype(BF16))
    grp = grp_ref[...]
    tm = proj.shape[0]

    def head_norm(t, gain):
        hi, lo = _split_bf16(t * t)
        ss = _dot(hi, grp) + _dot(lo, grp)
        return t * lax.rsqrt(ss * (1.0 / ATTN_HEAD_DIM) + EPS) * gain

    def emit(t, o1_ref, o4_ref, o16_ref):
        o1_ref[...] = t.astype(BF16)
        for j in range(ATTN_WIDTH // LANES):
            perm_ref[j] = t[:, j * LANES:(j + 1) * LANES]
        for dil, o_ref in ((4, o4_ref), (16, o16_ref)):
            for hf in range(2):
                for r in range(dil):
                    for jj in range(HALF // LANES):
                        c0 = (hf * dil + r) * HALF + jj * LANES
                        o_ref[:, c0:c0 + LANES] = perm_ref[hf * (HALF // LANES) + jj,
                                                           pl.ds(r, tm // dil, stride=dil), :].astype(BF16)

    emit(head_norm(proj[:, :ATTN_WIDTH], qg_ref[...]) * (ATTN_HEAD_DIM ** -0.5), q1_ref, q4_ref, q16_ref)
    emit(head_norm(proj[:, ATTN_WIDTH:2 * ATTN_WIDTH], kg_ref[...]), k1_ref, k4_ref, k16_ref)
    emit(proj[:, 2 * ATTN_WIDTH:3 * ATTN_WIDTH], v1_ref, v4_ref, v16_ref)
    r_ref[...] = proj[:, 3 * ATTN_WIDTH:]


def _in_proj(x, mod, gain, w_in, layer, q_gain, k_gain):
    tokens = x.shape[0]
    tm = TM_PROJ
    per_seq = SEQ // tm
    grp = np.kron(np.eye(ATTN_HEADS), np.ones((ATTN_HEAD_DIM, ATTN_HEAD_DIM))).astype(np.float32)
    row = lambda i: (i, 0)
    const = lambda i: (0, 0)
    layouts = []
    for dil in (1, 4, 16):
        shape = jax.ShapeDtypeStruct((tokens // dil, dil * ATTN_WIDTH), BF16)
        spec = pl.BlockSpec((tm // dil, dil * ATTN_WIDTH), row)
        layouts.append(((shape,) * 3, (spec,) * 3))
    out_shape = sum((s for s, _ in layouts), ()) + (jax.ShapeDtypeStruct((tokens, RET_IN_WIDTH), F32),)
    out_specs = sum((s for _, s in layouts), ()) + (pl.BlockSpec((tm, RET_IN_WIDTH), row),)
    return pl.pallas_call(
        _in_proj_kernel,
        out_shape=out_shape,
        grid=(tokens // tm,),
        in_specs=[
            pl.BlockSpec((tm, D_MODEL), row),
            pl.BlockSpec((None, 6, D_MODEL), lambda i: (i // per_seq, 0, 0)),
            pl.BlockSpec((1, D_MODEL), const),
            pl.BlockSpec((None, D_MODEL, IN_WIDTH), lambda i: (layer, 0, 0), pipeline_mode=pl.Buffered(1)),
            pl.BlockSpec((1, ATTN_WIDTH), const),
            pl.BlockSpec((1, ATTN_WIDTH), const),
            pl.BlockSpec((ATTN_WIDTH, ATTN_WIDTH), const),
        ],
        out_specs=out_specs,
        scratch_shapes=[pltpu.VMEM((ATTN_WIDTH // LANES, tm, LANES), F32)],
        compiler_params=_cparams(("arbitrary",)),
        name="in_projection",
    )(x, mod, gain.reshape(1, D_MODEL), w_in,
      jnp.tile(q_gain, ATTN_HEADS).reshape(1, ATTN_WIDTH),
      jnp.tile(k_gain, ATTN_HEADS).reshape(1, ATTN_WIDTH),
      jnp.asarray(grp, BF16))


PAIRS_PER_HALF = ATTN_HEADS // 4
GROUP = 4


def _pair_scores(qp, kp, vp, bias2, masks, low):
    q2 = jnp.concatenate([qp * masks[0], qp * masks[1]], axis=0)
    s = _dot_nt(q2, kp) + bias2
    m = jnp.max(s, axis=-1, keepdims=True)
    p = jnp.exp(s - m)
    den = jnp.sum(p, axis=-1, keepdims=True)
    pv = _dot(p.astype(BF16), vp)
    pick = lambda t: jnp.where(low, t[:BLOCK], t[BLOCK:])
    return pick(pv), pick(m), pick(den)


def _attn_kernel(q1_ref, k1_ref, v1_ref, q4_ref, k4_ref, v4_ref, q16_ref, k16_ref, v16_ref, bm_ref,
                 o_ref, acc_ref, max_ref, den_ref):
    lane = lax.broadcasted_iota(jnp.int32, (BLOCK, LANES), 1)
    low = lane < ATTN_HEAD_DIM
    masks = (jnp.where(low, 1.0, 0.0).astype(BF16), jnp.where(low, 0.0, 1.0).astype(BF16))

    def block(q_ref, k_ref, v_ref, c0, q0, w0, width, bias_set, rows, first):
        for p in range(PAIRS_PER_HALF):
            cs = slice(c0 + p * LANES, c0 + (p + 1) * LANES)
            acc, m, den = _pair_scores(q_ref[pl.ds(q0, BLOCK), cs], k_ref[pl.ds(w0, width), cs],
                                       v_ref[pl.ds(w0, width), cs], bm_ref[bias_set, p, :, 0:width],
                                       masks, low)
            if not first:
                m_old = max_ref[p, rows, :]
                m_new = jnp.maximum(m_old, m)
                a, b = jnp.exp(m_old - m_new), jnp.exp(m - m_new)
                den = den_ref[p, rows, :] * a + den * b
                acc = acc_ref[p, rows, :] * a + acc * b
                m = m_new
            max_ref[p, rows, :] = m
            den_ref[p, rows, :] = den
            acc_ref[p, rows, :] = acc

    def d1_group(g, carry):
        for u in range(GROUP):
            n = g * GROUP + u
            q0 = pl.multiple_of(n * BLOCK, BLOCK)
            w0 = pl.multiple_of(jnp.maximum(n - 1, 0) * BLOCK, BLOCK)
            bias_set = jnp.where(n == 0, BIAS_FIRST[1], BIAS_FULL[1])
            block(q1_ref, k1_ref, v1_ref, 0, q0, w0, 2 * BLOCK, bias_set, pl.ds(q0, BLOCK), True)
        return carry
    lax.fori_loop(0, SEQ // BLOCK // GROUP, d1_group, 0)

    for r in range(4):
        for n in range(SEQ // 4 // BLOCK):
            block(q4_ref, k4_ref, v4_ref, r * HALF, n * BLOCK, max(n - 1, 0) * BLOCK, 2 * BLOCK,
                  BIAS_FIRST[4] if n == 0 else BIAS_FULL[4],
                  pl.ds(r + 4 * BLOCK * n, BLOCK, stride=4), False)

    for r in range(16):
        block(q16_ref, k16_ref, v16_ref, r * HALF, 0, 0, BLOCK, BIAS_FIRST[16],
              pl.ds(r, BLOCK, stride=16), False)

    for n in range(SEQ // BLOCK):
        rows = slice(n * BLOCK, (n + 1) * BLOCK)
        for p in range(PAIRS_PER_HALF):
            o_ref[rows, p * LANES:(p + 1) * LANES] = (acc_ref[p, rows, :] / den_ref[p, rows, :]).astype(BF16)


def _dilated_attention(qkv, bias_masks):
    tokens = qkv[0].shape[0]
    batch = tokens // SEQ
    specs = []
    for dil in (1, 4, 16):
        specs += [pl.BlockSpec((SEQ // dil, dil * HALF), lambda b, hf: (b, hf))] * 3
    state = pltpu.VMEM((PAIRS_PER_HALF, SEQ, LANES), F32)
    return pl.pallas_call(
        _attn_kernel,
        out_shape=jax.ShapeDtypeStruct((tokens, ATTN_WIDTH), BF16),
        grid=(batch, 2),
        in_specs=specs + [pl.BlockSpec((N_BIAS_SETS, PAIRS_PER_HALF, 2 * BLOCK, 2 * BLOCK),
                                       lambda b, hf: (0, hf, 0, 0))],
        out_specs=pl.BlockSpec((SEQ, HALF), lambda b, hf: (b, hf)),
        scratch_shapes=[state, state, state],
        compiler_params=_cparams(("arbitrary", "arbitrary")),
        name="dilated_attention",
    )(*qkv, bias_masks)


def _retention_kernel(r_ref, cos_ref, sin_ref, dmask_ref, qdec_ref, kdec_ref, cdec_ref, gain_ref,
                      o_ref, state_ref):
    @pl.when(pl.program_id(1) == 0)
    def _():
        state_ref[...] = jnp.zeros_like(state_ref)

    lane = lax.broadcasted_iota(jnp.int32, (RET_CHUNK, LANES), 1)
    low = lane < RET_KEY_DIM
    first_half = (lane % RET_KEY_DIM) < (RET_KEY_DIM // 2)

    def rotate(t, cos, sin):
        partner = jnp.where(first_half, pltpu.roll(t, LANES - RET_KEY_DIM // 2, 1),
                            pltpu.roll(t, RET_KEY_DIM // 2, 1))
        return t * cos + partner * sin

    for c in range(RET_ROWS // RET_CHUNK):
        rows = slice(c * RET_CHUNK, (c + 1) * RET_CHUNK)
        for hp in range(RET_HEADS // 2):
            qs = slice(hp * LANES, (hp + 1) * LANES)
            ks = slice(RET_QK_WIDTH + hp * LANES, RET_QK_WIDTH + (hp + 1) * LANES)
            cos, sin = cos_ref[rows, qs], sin_ref[rows, qs]
            q_pair = rotate(r_ref[rows, qs], cos, sin) * (RET_KEY_DIM ** -0.5)
            k_pair = rotate(r_ref[rows, ks], cos, sin)
            for hh in range(2):
                head = 2 * hp + hh
                vs = slice(2 * RET_QK_WIDTH + head * LANES, 2 * RET_QK_WIDTH + (head + 1) * LANES)
                gs = slice(2 * RET_QK_WIDTH + RET_WIDTH + head * LANES,
                           2 * RET_QK_WIDTH + RET_WIDTH + (head + 1) * LANES)
                keep = low if hh == 0 else jnp.logical_not(low)
                qm = jnp.where(keep, q_pair, 0.0)
                vb = r_ref[rows, vs].astype(BF16)
                state = state_ref[head]
                inner = _dot_nt(qm.astype(BF16), k_pair.astype(BF16)) * dmask_ref[head]
                y = _dot(inner.astype(BF16), vb)
                y = y + _dot((qm * qdec_ref[head]).astype(BF16), state.astype(BF16))
                state_ref[head] = state * cdec_ref[head] + _dot_tn((k_pair * kdec_ref[head]).astype(BF16), vb)
                mu = jnp.mean(y, axis=-1, keepdims=True)
                yc = y - mu
                var = jnp.mean(yc * yc, axis=-1, keepdims=True)
                yn = yc * lax.rsqrt(var + EPS) * gain_ref[:, head * LANES:(head + 1) * LANES]
                o_ref[rows, head * LANES:(head + 1) * LANES] = (_silu(r_ref[rows, gs]) * yn).astype(BF16)


def _retention_tables():
    half = RET_KEY_DIM // 2
    pos = jnp.arange(SEQ, dtype=F32)
    inv = ROPE_BASE ** (-jnp.arange(half, dtype=F32) / half)
    ang = pos[:, None] * inv[None, :]
    cos, sin = jnp.cos(ang), jnp.sin(ang)
    cos_full = jnp.tile(jnp.concatenate([cos, cos], axis=-1), (1, RET_HEADS))
    sin_signed = jnp.tile(jnp.concatenate([-sin, sin], axis=-1), (1, RET_HEADS))
    log_g = jnp.log(1.0 - 2.0 ** (-5.0 - jnp.arange(RET_HEADS, dtype=F32)))
    idx = jnp.arange(RET_CHUNK, dtype=F32)
    diff = idx[:, None] - idx[None, :]
    dmask = jnp.where(diff >= 0, jnp.exp(jnp.maximum(diff, 0.0)[None] * log_g[:, None, None]), 0.0)
    q_decay = jnp.exp((idx + 1.0)[None, :] * log_g[:, None])[..., None]
    k_decay = jnp.exp((RET_CHUNK - 1.0 - idx)[None, :] * log_g[:, None])[..., None]
    chunk_decay = jnp.exp(RET_CHUNK * log_g)[:, None, None]
    full = (RET_HEADS, RET_CHUNK, LANES)
    return (cos_full, sin_signed, dmask, jnp.broadcast_to(q_decay, full),
            jnp.broadcast_to(k_decay, full), jnp.broadcast_to(chunk_decay, full))


def _retention(ret_in, ret_gain):
    tokens = ret_in.shape[0]
    batch = tokens // SEQ
    per_seq = SEQ // RET_ROWS
    cos, sin, dmask, qdec, kdec, cdec = _retention_tables()
    tab = pl.BlockSpec((RET_ROWS, RET_QK_WIDTH), lambda b, j: (j, 0))
    const3 = pl.BlockSpec((RET_HEADS, RET_CHUNK, LANES), lambda b, j: (0, 0, 0))
    return pl.pallas_call(
        _retention_kernel,
        out_shape=jax.ShapeDtypeStruct((tokens, RET_WIDTH), BF16),
        grid=(batch, per_seq),
        in_specs=[
            pl.BlockSpec((RET_ROWS, RET_IN_WIDTH), lambda b, j: (b * per_seq + j, 0)),
            tab, tab, const3, const3, const3, const3,
            pl.BlockSpec((1, RET_WIDTH), lambda b, j: (0, 0)),
        ],
        out_specs=pl.BlockSpec((RET_ROWS, RET_WIDTH), lambda b, j: (b * per_seq + j, 0)),
        scratch_shapes=[pltpu.VMEM((RET_HEADS, LANES, RET_VALUE_DIM), F32)],
        compiler_params=_cparams(("arbitrary", "arbitrary")),
        name="retention",
    )(ret_in, cos, sin, dmask, qdec, kdec, cdec, ret_gain.reshape(1, RET_WIDTH))


def _swiglu_chunks(h, w1_ref, w3_ref, w2_ref):
    d_ff = w1_ref.shape[-1]
    total = None
    for c0 in range(0, d_ff, MXU_COLS):
        c1 = min(c0 + MXU_COLS, d_ff)
        z = (_silu(_dot(h, w1_ref[:, c0:c1].astype(BF16))) * _dot(h, w3_ref[:, c0:c1].astype(BF16))).astype(BF16)
        part = _dot(z, w2_ref[c0:c1, :].astype(BF16))
        total = part if total is None else total + part
    return total


def _out_proj_ffn_kernel(attn_ref, ret_ref, x_ref, mod_ref, wo_ref, gain_ref, w1_ref, w3_ref, w2_ref, out_ref):
    mod = mod_ref[...]
    mix = (_dot(attn_ref[...], wo_ref[:ATTN_WIDTH, :].astype(BF16))
           + _dot(ret_ref[...], wo_ref[ATTN_WIDTH:, :].astype(BF16)))
    x = x_ref[...] + mod[2:3, :] * mix
    h = _modulated_norm(x, gain_ref[...], mod[4:5], mod[3:4]).astype(BF16)
    out_ref[...] = x + mod[5:6, :] * _swiglu_chunks(h, w1_ref, w3_ref, w2_ref)


def _out_proj_dense_ffn(attn, ret, x, mod, w_out, layer, gain, w1, w3, w2, index):
    tokens = x.shape[0]
    d_ff = w1.shape[2]
    tm = TM_FFN
    per_seq = SEQ // tm
    row = lambda i: (i, 0)
    resident = pl.Buffered(1)
    return pl.pallas_call(
        _out_proj_ffn_kernel,
        out_shape=jax.ShapeDtypeStruct((tokens, D_MODEL), F32),
        grid=(tokens // tm,),
        in_specs=[
            pl.BlockSpec((tm, ATTN_WIDTH), row),
            pl.BlockSpec((tm, RET_WIDTH), row),
            pl.BlockSpec((tm, D_MODEL), row),
            pl.BlockSpec((None, 6, D_MODEL), lambda i: (i // per_seq, 0, 0)),
            pl.BlockSpec((None, D_MODEL, D_MODEL), lambda i: (layer, 0, 0), pipeline_mode=resident),
            pl.BlockSpec((1, D_MODEL), lambda i: (0, 0)),
            pl.BlockSpec((None, D_MODEL, d_ff), lambda i: (index, 0, 0), pipeline_mode=resident),
            pl.BlockSpec((None, D_MODEL, d_ff), lambda i: (index, 0, 0), pipeline_mode=resident),
            pl.BlockSpec((None, d_ff, D_MODEL), lambda i: (index, 0, 0), pipeline_mode=resident),
        ],
        out_specs=pl.BlockSpec((tm, D_MODEL), row),
        compiler_params=_cparams(("arbitrary",)),
        name="out_projection_dense_swiglu",
    )(attn, ret, x, mod, w_out, gain.reshape(1, D_MODEL), w1, w3, w2)


def _router_kernel(attn_ref, ret_ref, x_ref, mod_ref, wo_ref, gain_ref, wr_ref,
                   xo_ref, h_ref, gates_ref, pos_ref, post_ref, start_ref, cnt_ref, carry_ref, *, tiles_per_seq):
    i = pl.program_id(0)

    @pl.when(i % tiles_per_seq == 0)
    def _():
        carry_ref[...] = jnp.zeros_like(carry_ref)

    mod = mod_ref[...]
    mix = (_dot(attn_ref[...], wo_ref[:ATTN_WIDTH, :].astype(BF16))
           + _dot(ret_ref[...], wo_ref[ATTN_WIDTH:, :].astype(BF16)))
    x = x_ref[...] + mod[2:3, :] * mix
    xo_ref[...] = x
    h = _modulated_norm(x, gain_ref[...], mod[4:5], mod[3:4]).astype(BF16)
    h_ref[...] = h
    tm = h.shape[0]
    lane = lax.broadcasted_iota(jnp.int32, (tm, LANES), 1).astype(F32)
    logits = jnp.where(lane < N_EXPERTS, _dot(h, wr_ref[...]), -jnp.inf)
    m1 = jnp.max(logits, axis=-1, keepdims=True)
    i1 = jnp.min(jnp.where(logits == m1, lane, float(LANES)), axis=-1, keepdims=True)
    rest = jnp.where(lane == i1, -jnp.inf, logits)
    m2 = jnp.max(rest, axis=-1, keepdims=True)
    i2 = jnp.min(jnp.where(rest == m2, lane, float(LANES)), axis=-1, keepdims=True)
    e2 = jnp.exp(m2 - m1)
    g1 = 1.0 / (1.0 + e2)
    g2 = e2 / (1.0 + e2)
    gates_ref[...] = jnp.where(lane == i1, g1, 0.0) + jnp.where(lane == i2, g2, 0.0)
    chosen = (lane == i1) | (lane == i2)
    onehot = jnp.where(chosen, 1.0, 0.0)
    r = lax.broadcasted_iota(jnp.int32, (tm, tm), 0)
    c = lax.broadcasted_iota(jnp.int32, (tm, tm), 1)
    tril = jnp.where(c <= r, 1.0, 0.0).astype(BF16)
    incl = _dot(tril, onehot.astype(BF16))
    carry = carry_ref[0:1, :]
    start_ref[...] = carry_ref[...]
    pos = jnp.where(chosen, incl - 1.0 + carry, -1.0)
    pos_ref[...] = pos
    post_ref[...] = pos.T[:N_EXPERTS, :]
    total = carry + incl[tm - 1:tm, :]
    carry_ref[...] = jnp.broadcast_to(total, carry_ref.shape)
    cnt_ref[...] = jnp.broadcast_to(total, cnt_ref.shape)


def _out_proj_router(attn, ret, x, mod, w_out, layer, gain, w_router):
    tokens = x.shape[0]
    batch = tokens // SEQ
    tm = TM_PROJ
    per_seq = SEQ // tm
    wr = jnp.zeros((D_MODEL, LANES), BF16).at[:, :N_EXPERTS].set(w_router.astype(BF16))
    row = lambda i: (i, 0)
    return pl.pallas_call(
        functools.partial(_router_kernel, tiles_per_seq=per_seq),
        out_shape=(
            jax.ShapeDtypeStruct((tokens, D_MODEL), F32),
            jax.ShapeDtypeStruct((tokens, D_MODEL), BF16),
            jax.ShapeDtypeStruct((tokens, LANES), F32),
            jax.ShapeDtypeStruct((tokens, LANES), F32),
            jax.ShapeDtypeStruct((N_EXPERTS, tokens), F32),
            jax.ShapeDtypeStruct((tokens // tm, 8, LANES), F32),
            jax.ShapeDtypeStruct((batch, 8, LANES), F32),
        ),
        grid=(tokens // tm,),
        in_specs=[
            pl.BlockSpec((tm, ATTN_WIDTH), row),
            pl.BlockSpec((tm, RET_WIDTH), row),
            pl.BlockSpec((tm, D_MODEL), row),
            pl.BlockSpec((None, 6, D_MODEL), lambda i: (i // per_seq, 0, 0)),
            pl.BlockSpec((None, D_MODEL, D_MODEL), lambda i: (layer, 0, 0)),
            pl.BlockSpec((1, D_MODEL), lambda i: (0, 0)),
            pl.BlockSpec((D_MODEL, LANES), lambda i: (0, 0)),
        ],
        out_specs=(
            pl.BlockSpec((tm, D_MODEL), row),
            pl.BlockSpec((tm, D_MODEL), row),
            pl.BlockSpec((tm, LANES), row),
            pl.BlockSpec((tm, LANES), row),
            pl.BlockSpec((N_EXPERTS, tm), lambda i: (0, i)),
            pl.BlockSpec((None, 8, LANES), lambda i: (i, 0, 0)),
            pl.BlockSpec((None, 8, LANES), lambda i: (i // per_seq, 0, 0)),
        ),
        scratch_shapes=[pltpu.VMEM((8, LANES), F32)],
        compiler_params=_cparams(("arbitrary",)),
        name="out_projection_router",
    )(attn, ret, x, mod, w_out, gain.reshape(1, D_MODEL), wr)


TOK_BLOCKS = SEQ // TR_MOE
TAIL_TILES = (64, 128, TR_MOE)
SPILL_ROWS = 16
SMALL_WINDOW = 128
SEG_ALIGN = 16
GROUP_ROWS = 3072
PACKED_ROWS = 2 * SEQ + N_EXPERTS * SEG_ALIGN + 2 * TR_MOE
BIG_CHUNK = 128
STAGE_ROWS = SEQ + TR_MOE
N_BOUNDS = TOK_BLOCKS + 1


def _moe_plan(start, cnt):
    batch = cnt.shape[0]
    counts = cnt[:, 0, :N_EXPERTS].astype(jnp.int32)
    bounds = jnp.concatenate([start[:, 0, :N_EXPERTS].reshape(batch, TOK_BLOCKS, N_EXPERTS).astype(jnp.int32),
                              counts[:, None, :]], axis=1)
    seg = (counts + (SEG_ALIGN - 1)) // SEG_ALIGN * SEG_ALIGN
    rows_e = jnp.sum(seg, axis=0)
    groups_e = (rows_e + (GROUP_ROWS - 1)) // GROUP_ROWS
    first_group = jnp.cumsum(groups_e) - groups_e
    seg_off = first_group[None, :] * GROUP_ROWS + jnp.cumsum(seg, axis=0) - seg
    max_rows = batch * (2 * SEQ + N_EXPERTS * (SEG_ALIGN - 1))
    n_groups = max_rows // GROUP_ROWS + N_EXPERTS
    g = jnp.arange(n_groups)
    g_expert = jnp.minimum(jnp.sum(g[:, None] >= jnp.cumsum(groups_e)[None, :], axis=1), N_EXPERTS - 1)
    g_rows = jnp.clip(rows_e[g_expert] - (g - first_group[g_expert]) * GROUP_ROWS, 0, GROUP_ROWS)
    g_rows = jnp.where(g < jnp.sum(groups_e), g_rows, 0)
    return (bounds.reshape(-1), seg.reshape(-1), seg_off.reshape(-1),
            g_expert.astype(jnp.int32), g_rows.astype(jnp.int32), n_groups)


def _segment_copies(seg, make_copy, action):
    n_big = lax.shift_right_logical(seg, BIG_CHUNK.bit_length() - 1)

    def big(i, carry):
        getattr(make_copy(pl.multiple_of(i * BIG_CHUNK, BIG_CHUNK), BIG_CHUNK), action)()
        return carry
    lax.fori_loop(0, n_big, big, 0)
    rest0 = n_big * BIG_CHUNK
    n_small = lax.shift_right_logical(seg - rest0, SEG_ALIGN.bit_length() - 1)

    def small(i, carry):
        getattr(make_copy(pl.multiple_of(rest0 + i * SEG_ALIGN, SEG_ALIGN), SEG_ALIGN), action)()
        return carry
    lax.fori_loop(0, n_small, small, 0)


def _block_windows(tbl_ref, b, e, align):
    base = b * N_BOUNDS * N_EXPERTS + e
    bounds = [tbl_ref[base + tb * N_EXPERTS] for tb in range(N_BOUNDS)]
    shift = align.bit_length() - 1
    win = [pl.multiple_of(lax.shift_left(lax.shift_right_logical(s, shift), shift), align) for s in bounds[:-1]]
    all_small = functools.reduce(jnp.logical_and,
                                 [bounds[tb + 1] - win[tb] <= SMALL_WINDOW for tb in range(TOK_BLOCKS)])
    return bounds, win, all_small


def _dispatch_kernel(tbl_ref, seg_ref, off_ref, h_ref, post_ref, rows_hbm, acc_ref, stage_ref, sem):
    b = pl.program_id(0)
    tr = TR_MOE

    def copies(e, action):
        slot = e % 2
        off = off_ref[b * N_EXPERTS + e]

        def make_copy(r0, rows):
            return pltpu.make_async_copy(stage_ref.at[slot, pl.ds(r0, rows), :],
                                         rows_hbm.at[pl.ds(pl.multiple_of(off + r0, SEG_ALIGN), rows), :],
                                         sem.at[slot])
        _segment_copies(seg_ref[b * N_EXPERTS + e], make_copy, action)

    def expert(e, carry):
        bounds, win, all_small = _block_windows(tbl_ref, b, e, SUBLANES)
        n_tiles = lax.shift_right_logical(bounds[-1] + (tr - 1), tr.bit_length() - 1)

        def clear(r, c):
            acc_ref[pl.ds(pl.multiple_of(r * tr, tr), tr), :] = jnp.zeros((tr, D_MODEL), F32)
            return c
        lax.fori_loop(0, n_tiles + 2, clear, 0)

        def gather(height):
            slot_id = lax.broadcasted_iota(jnp.int32, (height, tr), 0).astype(F32)
            for tb in range(TOK_BLOCKS):
                ts = slice(tb * tr, (tb + 1) * tr)
                local = post_ref[pl.ds(e, 1), ts] - win[tb].astype(F32)
                sel = jnp.where(local == slot_id, 1.0, 0.0).astype(BF16)
                acc_ref[pl.ds(win[tb], height), :] += _dot(sel, h_ref[ts, :])

        @pl.when(all_small)
        def _():
            gather(SMALL_WINDOW)

        @pl.when(jnp.logical_not(all_small))
        def _():
            gather(tr + SPILL_ROWS)

        @pl.when(e >= 2)
        def _():
            copies(e - 2, "wait")

        def to_stage(r, c):
            rows = pl.ds(pl.multiple_of(r * tr, tr), tr)
            stage_ref[e % 2, rows, :] = acc_ref[rows, :].astype(BF16)
            return c
        lax.fori_loop(0, n_tiles, to_stage, 0)
        copies(e, "start")
        return carry
    lax.fori_loop(0, N_EXPERTS, expert, 0)
    copies(N_EXPERTS - 2, "wait")
    copies(N_EXPERTS - 1, "wait")


def _dispatch(h, post, bounds, seg, seg_off, total_rows):
    tokens = h.shape[0]
    grid_spec = pltpu.PrefetchScalarGridSpec(
        num_scalar_prefetch=3,
        grid=(tokens // SEQ,),
        in_specs=[pl.BlockSpec((SEQ, D_MODEL), lambda b, *_: (b, 0)),
                  pl.BlockSpec((N_EXPERTS, SEQ), lambda b, *_: (0, b))],
        out_specs=pl.BlockSpec(memory_space=pl.ANY),
        scratch_shapes=[pltpu.VMEM((SEQ + 3 * TR_MOE, D_MODEL), F32),
                        pltpu.VMEM((2, STAGE_ROWS, D_MODEL), BF16),
                        pltpu.SemaphoreType.DMA((2,))],
    )
    return pl.pallas_call(
        _dispatch_kernel,
        out_shape=jax.ShapeDtypeStruct((total_rows, D_MODEL), BF16),
        grid_spec=grid_spec,
        compiler_params=_cparams(("arbitrary",)),
        name="expert_dispatch",
    )(bounds, seg, seg_off, h, post)


def _experts_kernel(ge_ref, rows_ref, hs_ref, w1_ref, w3_ref, w2_ref, ys_ref, acc_ref, wb1_ref, wb3_ref, wb2_ref):
    g, f = pl.program_id(0), pl.program_id(1)
    last_f = pl.num_programs(1) - 1
    tr = TR_MOE
    rows = rows_ref[g]
    n_big = lax.shift_right_logical(rows, tr.bit_length())
    n_full = lax.shift_right_logical(rows, tr.bit_length() - 1)
    rest = rows - n_full * tr
    n_tiles = lax.shift_right_logical(rows + (tr - 1), tr.bit_length() - 1)
    tail0 = pl.multiple_of(n_full * tr, tr)

    @pl.when(rows > 0)
    def _():
        wb1_ref[...] = w1_ref[...].astype(BF16)
        wb3_ref[...] = w3_ref[...].astype(BF16)
        wb2_ref[...] = w2_ref[...].astype(BF16)

    @pl.when(f == 0)
    def _():
        def clear(r, carry):
            acc_ref[pl.ds(pl.multiple_of(r * tr, tr), tr), :] = jnp.zeros((tr, D_MODEL), F32)
            return carry
        lax.fori_loop(0, n_tiles, clear, 0)

    def swiglu(hr, r0, m):
        z = (_silu(_dot(hr, wb1_ref[...])) * _dot(hr, wb3_ref[...])).astype(BF16)
        acc_ref[pl.ds(r0, m), :] += _dot(z, wb2_ref[...])

    def big_tile(r, carry):
        r0 = pl.multiple_of(r * 2 * tr, 2 * tr)
        swiglu(hs_ref[pl.ds(r0, 2 * tr), :], r0, 2 * tr)
        return carry
    lax.fori_loop(0, n_big, big_tile, 0)

    @pl.when(n_full > 2 * n_big)
    def _():
        r0 = pl.multiple_of(n_big * 2 * tr, tr)
        swiglu(hs_ref[pl.ds(r0, tr), :], r0, tr)
    lo = 0
    for m in TAIL_TILES:
        @pl.when((rest > lo) & (rest <= m))
        def _():
            valid = lax.broadcasted_iota(jnp.int32, (m, D_MODEL), 0) < rest
            hr = hs_ref[pl.ds(tail0, m), :]
            swiglu(jnp.where(valid, hr, jnp.zeros_like(hr)), tail0, m)
        lo = m

    @pl.when(f == last_f)
    def _():
        ys_ref[...] = jnp.zeros_like(ys_ref)

        def store(r, carry):
            rr = pl.ds(pl.multiple_of(r * tr, tr), tr)
            ys_ref[rr, :] = acc_ref[rr, :].astype(BF16)
            return carry
        lax.fori_loop(0, n_tiles, store, 0)


def _experts(hs, g_expert, g_rows, n_groups, w1, w3, w2):
    d_ff = w1.shape[2]
    tf = TF_MOE
    n_f = d_ff // tf
    chunk = lambda g, f, gr: jnp.where(gr[g] > 0, f, n_f - 1)
    grid_spec = pltpu.PrefetchScalarGridSpec(
        num_scalar_prefetch=2,
        grid=(n_groups, n_f),
        in_specs=[
            pl.BlockSpec((GROUP_ROWS, D_MODEL), lambda g, f, ge, gr: (g, 0)),
            pl.BlockSpec((None, D_MODEL, tf), lambda g, f, ge, gr: (ge[g], 0, chunk(g, f, gr))),
            pl.BlockSpec((None, D_MODEL, tf), lambda g, f, ge, gr: (ge[g], 0, chunk(g, f, gr))),
            pl.BlockSpec((None, tf, D_MODEL), lambda g, f, ge, gr: (ge[g], chunk(g, f, gr), 0)),
        ],
        out_specs=pl.BlockSpec((GROUP_ROWS, D_MODEL), lambda g, f, ge, gr: (g, 0)),
        scratch_shapes=[pltpu.VMEM((GROUP_ROWS, D_MODEL), F32),
                        pltpu.VMEM((D_MODEL, tf), BF16), pltpu.VMEM((D_MODEL, tf), BF16),
                        pltpu.VMEM((tf, D_MODEL), BF16)],
    )
    return pl.pallas_call(
        _experts_kernel,
        out_shape=jax.ShapeDtypeStruct(hs.shape, BF16),
        grid_spec=grid_spec,
        compiler_params=_cparams(("arbitrary", "arbitrary")),
        name="expert_swiglu",
    )(g_expert, g_rows, hs, w1, w3, w2)


def _combine_kernel(tbl_ref, seg_ref, off_ref, rows_hbm, pos_ref, gates_ref, x_ref, mod_ref, out_ref, buf_ref, sem):
    b = pl.program_id(0)
    n_seq = pl.num_programs(0)
    tr = TR_MOE

    def all_segments(seq, action):
        def body(e, first):
            off = off_ref[seq * N_EXPERTS + e]
            seg = seg_ref[seq * N_EXPERTS + e]

            def make_copy(r0, rows):
                return pltpu.make_async_copy(
                    rows_hbm.at[pl.ds(pl.multiple_of(off + r0, SEG_ALIGN), rows), :],
                    buf_ref.at[seq % 2, pl.ds(pl.multiple_of(first + r0, SEG_ALIGN), rows), :],
                    sem.at[seq % 2, e])
            _segment_copies(seg, make_copy, action)
            return first + seg
        lax.fori_loop(0, N_EXPERTS, body, 0)

    @pl.when(b == 0)
    def _():
        buf_ref[...] = jnp.zeros_like(buf_ref)
        all_segments(b, "start")

    @pl.when(b + 1 < n_seq)
    def _():
        all_segments(b + 1, "start")

    out_ref[...] = x_ref[...]
    lane = lax.broadcasted_iota(jnp.int32, (tr, LANES), 1)
    layer_gate = mod_ref[5:6, :]
    all_segments(b, "wait")

    def expert(e, seg_start):
        bounds, win, all_small = _block_windows(tbl_ref, b, e, SEG_ALIGN)

        def scatter(tb, first, height):
            ts = slice(tb * tr, (tb + 1) * tr)
            pos_col = jnp.sum(jnp.where(lane == e, pos_ref[ts, :], 0.0), axis=-1, keepdims=True)
            gate_col = jnp.sum(jnp.where(lane == e, gates_ref[ts, :], 0.0), axis=-1, keepdims=True)
            slot_id = lax.broadcasted_iota(jnp.int32, (tr, height), 1).astype(F32) + float(first)
            sel = jnp.where(pos_col - win[tb].astype(F32) == slot_id, 1.0, 0.0).astype(BF16)
            rows = buf_ref[b % 2, pl.ds(pl.multiple_of(seg_start + win[tb] + first, SEG_ALIGN), height), :]
            out_ref[ts, :] += layer_gate * (gate_col * _dot(sel, rows))

        @pl.when(all_small)
        def _():
            for tb in range(TOK_BLOCKS):
                scatter(tb, 0, SMALL_WINDOW)

        @pl.when(jnp.logical_not(all_small))
        def _():
            for tb in range(TOK_BLOCKS):
                scatter(tb, 0, tr)
            for tb in range(TOK_BLOCKS):
                @pl.when(bounds[tb + 1] - win[tb] > tr)
                def _():
                    scatter(tb, tr, SPILL_ROWS)
        return seg_start + seg_ref[b * N_EXPERTS + e]
    lax.fori_loop(0, N_EXPERTS, expert, 0)


def _combine(ys, pos, gates, x, mod, bounds, seg, seg_off):
    tokens = x.shape[0]
    grid_spec = pltpu.PrefetchScalarGridSpec(
        num_scalar_prefetch=3,
        grid=(tokens // SEQ,),
        in_specs=[pl.BlockSpec(memory_space=pl.ANY),
                  pl.BlockSpec((SEQ, LANES), lambda b, *_: (b, 0)),
                  pl.BlockSpec((SEQ, LANES), lambda b, *_: (b, 0)),
                  pl.BlockSpec((SEQ, D_MODEL), lambda b, *_: (b, 0)),
                  pl.BlockSpec((None, 6, D_MODEL), lambda b, *_: (b, 0, 0))],
        out_specs=pl.BlockSpec((SEQ, D_MODEL), lambda b, *_: (b, 0)),
        scratch_shapes=[pltpu.VMEM((2, PACKED_ROWS, D_MODEL), BF16),
                        pltpu.SemaphoreType.DMA((2, N_EXPERTS))],
    )
    return pl.pallas_call(
        _combine_kernel,
        out_shape=jax.ShapeDtypeStruct((tokens, D_MODEL), F32),
        grid_spec=grid_spec,
        compiler_params=_cparams(("arbitrary",)),
        name="expert_combine",
    )(bounds, seg, seg_off, ys, pos, gates, x, mod)


def _token_mixer(x, mod, gain, w_in, layer, q_gain, k_gain, ret_gain, bias_masks):
    *qkv, ret_in = _in_proj(x, mod, gain, w_in, layer, q_gain, k_gain)
    return _dilated_attention(qkv, bias_masks), _retention(ret_in, ret_gain)


def _out_proj_moe_ffn(attn, ret, x, mod, w_out, layer, gain, w_router, w1, w3, w2):
    x, h, gates, pos, post, start, cnt = _out_proj_router(attn, ret, x, mod, w_out, layer, gain, w_router)
    bounds, seg, seg_off, g_expert, g_rows, n_groups = _moe_plan(start, cnt)
    hs = _dispatch(h, post, bounds, seg, seg_off, n_groups * GROUP_ROWS)
    ys = _experts(hs, g_expert, g_rows, n_groups, w1, w3, w2)
    return _combine(ys, pos, gates, x, mod, bounds, seg, seg_off)


def kernel(x, c, rel_bias_table, norm_mix, norm_ffn, w_mod, b_mod, w_in, q_gain, k_gain, ret_gain, w_out,
           ffn_w1, ffn_w3, ffn_w2, moe_router, moe_w1, moe_w3, moe_w2):
    batch, seq, d_model = x.shape
    assert (seq, d_model) == (SEQ, D_MODEL)
    depth = w_mod.shape[0]
    mods = _modulation(c, w_mod, b_mod).reshape(depth, batch, 6, D_MODEL)
    bias_masks = _bias_masks(rel_bias_table)
    xt = x.reshape(batch * seq, d_model)
    for layer in range(depth):
        mod = mods[layer]
        attn, ret = _token_mixer(xt, mod, norm_mix[layer], w_in, layer, q_gain[layer], k_gain[layer],
                                 ret_gain[layer], bias_masks)
        i = layer // 2
        if layer % 2 == 0:
            xt = _out_proj_dense_ffn(attn, ret, xt, mod, w_out, layer, norm_ffn[layer],
                                     ffn_w1, ffn_w3, ffn_w2, i)
        else:
            xt = _out_proj_moe_ffn(attn, ret, xt, mod, w_out, layer, norm_ffn[layer],
                                   moe_router[i], moe_w1[i], moe_w3[i], moe_w2[i])
    return xt.reshape(batch, seq, d_model)
```

```python
import functools
import math

import jax
import jax.numpy as jnp
import numpy as np
from jax import lax
from jax.experimental import pallas as pl
from jax.experimental.pallas import tpu as pltpu

D_MODEL = 1024
SEQ = 2048
ATTN_HEADS = 8
ATTN_HEAD_DIM = 64
ATTN_WIDTH = ATTN_HEADS * ATTN_HEAD_DIM
DILATED_PATTERNS = ((128, 1), (512, 4), (2048, 16))
BLOCK = 128
NUM_BUCKETS = 32
MAX_DISTANCE = 2048
RET_HEADS = 4
RET_KEY_DIM = 64
RET_VALUE_DIM = 128
RET_WIDTH = RET_HEADS * RET_VALUE_DIM
RET_QK_WIDTH = RET_HEADS * RET_KEY_DIM
RET_CHUNK = 128
ROPE_BASE = 10000.0
IN_WIDTH = 3 * ATTN_WIDTH + 2 * RET_QK_WIDTH + 2 * RET_WIDTH
RET_IN_WIDTH = IN_WIDTH - 3 * ATTN_WIDTH
N_EXPERTS = 8
EPS = 1e-6
NEG_INF = -1e30

LANES = 128
VMEM_LIMIT = 60 * 1024 * 1024

BF16 = jnp.bfloat16
F32 = jnp.float32

TM_PROJ = 256
TM_FFN = 512
MXU_COLS = 256
TR_MOE = 256
TF_MOE = 512
assert TM_PROJ == TR_MOE
RET_ROWS = 512
SUBLANES = 8


def _cparams(sem):
    return pltpu.CompilerParams(dimension_semantics=sem, vmem_limit_bytes=VMEM_LIMIT)


def _dot(a, b):
    return jnp.dot(a, b, preferred_element_type=F32)


def _dot_nt(a, b):
    return lax.dot_general(a, b, (((1,), (1,)), ((), ())), preferred_element_type=F32)


def _dot_tn(a, b):
    return lax.dot_general(a, b, (((0,), (0,)), ((), ())), preferred_element_type=F32)


def _split_bf16(v):
    hi = v.astype(BF16)
    lo = (v - hi.astype(F32)).astype(BF16)
    return hi, lo


def _silu(v):
    return v * (1.0 / (1.0 + jnp.exp(-v)))


def _modulated_norm(x, gain, scale, shift):
    ms = jnp.mean(x * x, axis=-1, keepdims=True)
    y = x * lax.rsqrt(ms + EPS) * gain
    return y * (1.0 + scale) + shift


def _mod_kernel(c_ref, w_ref, b_ref, o_ref):
    ca = _silu(c_ref[...]).astype(BF16)
    o_ref[...] = _dot(ca, w_ref[...].astype(BF16)) + b_ref[...]


def _modulation(c, w_mod, b_mod):
    depth, _, width = w_mod.shape
    batch = c.shape[0]
    tn = 1536
    return pl.pallas_call(
        _mod_kernel,
        out_shape=jax.ShapeDtypeStruct((depth, batch, width), F32),
        grid=(depth, width // tn),
        in_specs=[
            pl.BlockSpec((batch, D_MODEL), lambda l, n: (0, 0)),
            pl.BlockSpec((None, D_MODEL, tn), lambda l, n: (l, 0, n)),
            pl.BlockSpec((None, 1, tn), lambda l, n: (l, 0, n)),
        ],
        out_specs=pl.BlockSpec((None, batch, tn), lambda l, n: (l, 0, n)),
        compiler_params=_cparams(("arbitrary", "arbitrary")),
        name="adaln_modulation",
    )(c, w_mod, b_mod.reshape(depth, 1, width))


def _bias_kernel(table_ref, bucket_ref, o_ref):
    bucket = bucket_ref[...]
    acc = [jnp.full(bucket.shape, NEG_INF, F32) for _ in range(ATTN_HEADS)]
    for b in range(NUM_BUCKETS):
        hit = bucket == b
        for h in range(ATTN_HEADS):
            acc[h] = jnp.where(hit, table_ref[b, h], acc[h])
    for h in range(ATTN_HEADS):
        o_ref[h // 2, (h % 2) * BLOCK:(h % 2 + 1) * BLOCK, :] = acc[h]


BIAS_FULL = {1: 0, 4: 2}
BIAS_FIRST = {1: 1, 4: 3, 16: 4}
N_BIAS_SETS = 5


def _bias_masks(rel_bias_table):
    i = jnp.arange(BLOCK)[:, None]
    j = jnp.arange(2 * BLOCK)[None, :]
    max_exact = NUM_BUCKETS // 2

    def bucket_of(rel, dilation, w_sub, exists):
        n = jnp.maximum(rel * dilation, 0)
        nf = jnp.maximum(n.astype(F32), float(max_exact))
        large = max_exact + (jnp.log(nf / max_exact) / math.log(MAX_DISTANCE / max_exact)
                             * (NUM_BUCKETS - max_exact)).astype(jnp.int32)
        large = jnp.minimum(large, NUM_BUCKETS - 1)
        bucket = jnp.where(n < max_exact, n, large)
        allowed = (rel >= 0) & (rel <= w_sub) & exists
        return jnp.where(allowed, bucket, -1)

    sets = [None] * N_BIAS_SETS
    for window, dilation in DILATED_PATTERNS:
        w_sub = window // dilation
        if dilation in BIAS_FULL:
            sets[BIAS_FULL[dilation]] = bucket_of(i - j + BLOCK, dilation, w_sub, j >= 0)
        sets[BIAS_FIRST[dilation]] = bucket_of(i - j, dilation, w_sub, j < BLOCK)
    buckets = jnp.stack(sets).astype(jnp.int32)
    return pl.pallas_call(
        _bias_kernel,
        out_shape=jax.ShapeDtypeStruct((N_BIAS_SETS, ATTN_HEADS // 2, 2 * BLOCK, 2 * BLOCK), F32),
        grid=(N_BIAS_SETS,),
        in_specs=[
            pl.BlockSpec(memory_space=pltpu.SMEM),
            pl.BlockSpec((None, BLOCK, 2 * BLOCK), lambda p: (p, 0, 0)),
        ],
        out_specs=pl.BlockSpec((None, ATTN_HEADS // 2, 2 * BLOCK, 2 * BLOCK), lambda p: (p, 0, 0, 0)),
        compiler_params=_cparams(("arbitrary",)),
        name="relative_bias_masks",
    )(rel_bias_table, buckets)


HALF = ATTN_WIDTH // 2


def _in_proj_kernel(x_ref, mod_ref, gain_ref, w_ref, qg_ref, kg_ref, grp_ref,
                    q1_ref, k1_ref, v1_ref, q4_ref, k4_ref, v4_ref, q16_ref, k16_ref, v16_ref, r_ref,
                    perm_ref):
    mod = mod_ref[...]
    h = _modulated_norm(x_ref[...], gain_ref[...], mod[1:2], mod[0:1]).astype(BF16)
    proj = _dot(h, w_ref[...].astype(BF16))
    grp = grp_ref[...]
    tm = proj.shape[0]

    def head_norm(t, gain):
        hi, lo = _split_bf16(t * t)
        ss = _dot(hi, grp) + _dot(lo, grp)
        return t * lax.rsqrt(ss * (1.0 / ATTN_HEAD_DIM) + EPS) * gain

    def emit(t, o1_ref, o4_ref, o16_ref):
        o1_ref[...] = t.astype(BF16)
        for j in range(ATTN_WIDTH // LANES):
            perm_ref[j] = t[:, j * LANES:(j + 1) * LANES]
        for dil, o_ref in ((4, o4_ref), (16, o16_ref)):
            for hf in range(2):
                for r in range(dil):
                    for jj in range(HALF // LANES):
                        c0 = (hf * dil + r) * HALF + jj * LANES
                        o_ref[:, c0:c0 + LANES] = perm_ref[hf * (HALF // LANES) + jj,
                                                           pl.ds(r, tm // dil, stride=dil), :].astype(BF16)

    emit(head_norm(proj[:, :ATTN_WIDTH], qg_ref[...]) * (ATTN_HEAD_DIM ** -0.5), q1_ref, q4_ref, q16_ref)
    emit(head_norm(proj[:, ATTN_WIDTH:2 * ATTN_WIDTH], kg_ref[...]), k1_ref, k4_ref, k16_ref)
    emit(proj[:, 2 * ATTN_WIDTH:3 * ATTN_WIDTH], v1_ref, v4_ref, v16_ref)
    r_ref[...] = proj[:, 3 * ATTN_WIDTH:]


def _in_proj(x, mod, gain, w_in, layer, q_gain, k_gain):
    tokens = x.shape[0]
    tm = TM_PROJ
    per_seq = SEQ // tm
    grp = np.kron(np.eye(ATTN_HEADS), np.ones((ATTN_HEAD_DIM, ATTN_HEAD_DIM))).astype(np.float32)
    row = lambda i: (i, 0)
    const = lambda i: (0, 0)
    layouts = []
    for dil in (1, 4, 16):
        shape = jax.ShapeDtypeStruct((tokens // dil, dil * ATTN_WIDTH), BF16)
        spec = pl.BlockSpec((tm // dil, dil * ATTN_WIDTH), row)
        layouts.append(((shape,) * 3, (spec,) * 3))
    out_shape = sum((s for s, _ in layouts), ()) + (jax.ShapeDtypeStruct((tokens, RET_IN_WIDTH), F32),)
    out_specs = sum((s for _, s in layouts), ()) + (pl.BlockSpec((tm, RET_IN_WIDTH), row),)
    return pl.pallas_call(
        _in_proj_kernel,
        out_shape=out_shape,
        grid=(tokens // tm,),
        in_specs=[
            pl.BlockSpec((tm, D_MODEL), row),
            pl.BlockSpec((None, 6, D_MODEL), lambda i: (i // per_seq, 0, 0)),
            pl.BlockSpec((1, D_MODEL), const),
            pl.BlockSpec((None, D_MODEL, IN_WIDTH), lambda i: (layer, 0, 0), pipeline_mode=pl.Buffered(1)),
            pl.BlockSpec((1, ATTN_WIDTH), const),
            pl.BlockSpec((1, ATTN_WIDTH), const),
            pl.BlockSpec((ATTN_WIDTH, ATTN_WIDTH), const),
        ],
        out_specs=out_specs,
        scratch_shapes=[pltpu.VMEM((ATTN_WIDTH // LANES, tm, LANES), F32)],
        compiler_params=_cparams(("arbitrary",)),
        name="in_projection",
    )(x, mod, gain.reshape(1, D_MODEL), w_in,
      jnp.tile(q_gain, ATTN_HEADS).reshape(1, ATTN_WIDTH),
      jnp.tile(k_gain, ATTN_HEADS).reshape(1, ATTN_WIDTH),
      jnp.asarray(grp, BF16))


PAIRS_PER_HALF = ATTN_HEADS // 4
GROUP = 4


def _pair_scores(qp, kp, vp, bias2, masks, low):
    q2 = jnp.concatenate([qp * masks[0], qp * masks[1]], axis=0)
    s = _dot_nt(q2, kp) + bias2
    m = jnp.max(s, axis=-1, keepdims=True)
    p = jnp.exp(s - m)
    den = jnp.sum(p, axis=-1, keepdims=True)
    pv = _dot(p.astype(BF16), vp)
    pick = lambda t: jnp.where(low, t[:BLOCK], t[BLOCK:])
    return pick(pv), pick(m), pick(den)


def _attn_kernel(q1_ref, k1_ref, v1_ref, q4_ref, k4_ref, v4_ref, q16_ref, k16_ref, v16_ref, bm_ref,
                 o_ref, acc_ref, max_ref, den_ref):
    lane = lax.broadcasted_iota(jnp.int32, (BLOCK, LANES), 1)
    low = lane < ATTN_HEAD_DIM
    masks = (jnp.where(low, 1.0, 0.0).astype(BF16), jnp.where(low, 0.0, 1.0).astype(BF16))

    def block(q_ref, k_ref, v_ref, c0, q0, w0, width, bias_set, rows, first):
        for p in range(PAIRS_PER_HALF):
            cs = slice(c0 + p * LANES, c0 + (p + 1) * LANES)
            acc, m, den = _pair_scores(q_ref[pl.ds(q0, BLOCK), cs], k_ref[pl.ds(w0, width), cs],
                                       v_ref[pl.ds(w0, width), cs], bm_ref[bias_set, p, :, 0:width],
                                       masks, low)
            if not first:
                m_old = max_ref[p, rows, :]
                m_new = jnp.maximum(m_old, m)
                a, b = jnp.exp(m_old - m_new), jnp.exp(m - m_new)
                den = den_ref[p, rows, :] * a + den * b
                acc = acc_ref[p, rows, :] * a + acc * b
                m = m_new
            max_ref[p, rows, :] = m
            den_ref[p, rows, :] = den
            acc_ref[p, rows, :] = acc

    def d1_group(g, carry):
        for u in range(GROUP):
            n = g * GROUP + u
            q0 = pl.multiple_of(n * BLOCK, BLOCK)
            w0 = pl.multiple_of(jnp.maximum(n - 1, 0) * BLOCK, BLOCK)
            bias_set = jnp.where(n == 0, BIAS_FIRST[1], BIAS_FULL[1])
            block(q1_ref, k1_ref, v1_ref, 0, q0, w0, 2 * BLOCK, bias_set, pl.ds(q0, BLOCK), True)
        return carry
    lax.fori_loop(0, SEQ // BLOCK // GROUP, d1_group, 0)

    for r in range(4):
        for n in range(SEQ // 4 // BLOCK):
            block(q4_ref, k4_ref, v4_ref, r * HALF, n * BLOCK, max(n - 1, 0) * BLOCK, 2 * BLOCK,
                  BIAS_FIRST[4] if n == 0 else BIAS_FULL[4],
                  pl.ds(r + 4 * BLOCK * n, BLOCK, stride=4), False)

    for r in range(16):
        block(q16_ref, k16_ref, v16_ref, r * HALF, 0, 0, BLOCK, BIAS_FIRST[16],
              pl.ds(r, BLOCK, stride=16), False)

    for n in range(SEQ // BLOCK):
        rows = slice(n * BLOCK, (n + 1) * BLOCK)
        for p in range(PAIRS_PER_HALF):
            o_ref[rows, p * LANES:(p + 1) * LANES] = (acc_ref[p, rows, :] / den_ref[p, rows, :]).astype(BF16)


def _dilated_attention(qkv, bias_masks):
    tokens = qkv[0].shape[0]
    batch = tokens // SEQ
    specs = []
    for dil in (1, 4, 16):
        specs += [pl.BlockSpec((SEQ // dil, dil * HALF), lambda b, hf: (b, hf))] * 3
    state = pltpu.VMEM((PAIRS_PER_HALF, SEQ, LANES), F32)
    return pl.pallas_call(
        _attn_kernel,
        out_shape=jax.ShapeDtypeStruct((tokens, ATTN_WIDTH), BF16),
        grid=(batch, 2),
        in_specs=specs + [pl.BlockSpec((N_BIAS_SETS, PAIRS_PER_HALF, 2 * BLOCK, 2 * BLOCK),
                                       lambda b, hf: (0, hf, 0, 0))],
        out_specs=pl.BlockSpec((SEQ, HALF), lambda b, hf: (b, hf)),
        scratch_shapes=[state, state, state],
        compiler_params=_cparams(("arbitrary", "arbitrary")),
        name="dilated_attention",
    )(*qkv, bias_masks)


def _retention_kernel(r_ref, cos_ref, sin_ref, dmask_ref, qdec_ref, kdec_ref, cdec_ref, gain_ref,
                      o_ref, state_ref):
    @pl.when(pl.program_id(1) == 0)
    def _():
        state_ref[...] = jnp.zeros_like(state_ref)

    lane = lax.broadcasted_iota(jnp.int32, (RET_CHUNK, LANES), 1)
    low = lane < RET_KEY_DIM
    first_half = (lane % RET_KEY_DIM) < (RET_KEY_DIM // 2)

    def rotate(t, cos, sin):
        partner = jnp.where(first_half, pltpu.roll(t, LANES - RET_KEY_DIM // 2, 1),
                            pltpu.roll(t, RET_KEY_DIM // 2, 1))
        return t * cos + partner * sin

    for c in range(RET_ROWS // RET_CHUNK):
        rows = slice(c * RET_CHUNK, (c + 1) * RET_CHUNK)
        for hp in range(RET_HEADS // 2):
            qs = slice(hp * LANES, (hp + 1) * LANES)
            ks = slice(RET_QK_WIDTH + hp * LANES, RET_QK_WIDTH + (hp + 1) * LANES)
            cos, sin = cos_ref[rows, qs], sin_ref[rows, qs]
            q_pair = rotate(r_ref[rows, qs], cos, sin) * (RET_KEY_DIM ** -0.5)
            k_pair = rotate(r_ref[rows, ks], cos, sin)
            for hh in range(2):
                head = 2 * hp + hh
                vs = slice(2 * RET_QK_WIDTH + head * LANES, 2 * RET_QK_WIDTH + (head + 1) * LANES)
                gs = slice(2 * RET_QK_WIDTH + RET_WIDTH + head * LANES,
                           2 * RET_QK_WIDTH + RET_WIDTH + (head + 1) * LANES)
                keep = low if hh == 0 else jnp.logical_not(low)
                qm = jnp.where(keep, q_pair, 0.0)
                vb = r_ref[rows, vs].astype(BF16)
                state = state_ref[head]
                inner = _dot_nt(qm.astype(BF16), k_pair.astype(BF16)) * dmask_ref[head]
                y = _dot(inner.astype(BF16), vb)
                y = y + _dot((qm * qdec_ref[head]).astype(BF16), state.astype(BF16))
                state_ref[head] = state * cdec_ref[head] + _dot_tn((k_pair * kdec_ref[head]).astype(BF16), vb)
                mu = jnp.mean(y, axis=-1, keepdims=True)
                yc = y - mu
                var = jnp.mean(yc * yc, axis=-1, keepdims=True)
                yn = yc * lax.rsqrt(var + EPS) * gain_ref[:, head * LANES:(head + 1) * LANES]
                o_ref[rows, head * LANES:(head + 1) * LANES] = (_silu(r_ref[rows, gs]) * yn).astype(BF16)


def _retention_tables():
    half = RET_KEY_DIM // 2
    pos = jnp.arange(SEQ, dtype=F32)
    inv = ROPE_BASE ** (-jnp.arange(half, dtype=F32) / half)
    ang = pos[:, None] * inv[None, :]
    cos, sin = jnp.cos(ang), jnp.sin(ang)
    cos_full = jnp.tile(jnp.concatenate([cos, cos], axis=-1), (1, RET_HEADS))
    sin_signed = jnp.tile(jnp.concatenate([-sin, sin], axis=-1), (1, RET_HEADS))
    log_g = jnp.log(1.0 - 2.0 ** (-5.0 - jnp.arange(RET_HEADS, dtype=F32)))
    idx = jnp.arange(RET_CHUNK, dtype=F32)
    diff = idx[:, None] - idx[None, :]
    dmask = jnp.where(diff >= 0, jnp.exp(jnp.maximum(diff, 0.0)[None] * log_g[:, None, None]), 0.0)
    q_decay = jnp.exp((idx + 1.0)[None, :] * log_g[:, None])[..., None]
    k_decay = jnp.exp((RET_CHUNK - 1.0 - idx)[None, :] * log_g[:, None])[..., None]
    chunk_decay = jnp.exp(RET_CHUNK * log_g)[:, None, None]
    full = (RET_HEADS, RET_CHUNK, LANES)
    return (cos_full, sin_signed, dmask, jnp.broadcast_to(q_decay, full),
            jnp.broadcast_to(k_decay, full), jnp.broadcast_to(chunk_decay, full))


def _retention(ret_in, ret_gain):
    tokens = ret_in.shape[0]
    batch = tokens // SEQ
    per_seq = SEQ // RET_ROWS
    cos, sin, dmask, qdec, kdec, cdec = _retention_tables()
    tab = pl.BlockSpec((RET_ROWS, RET_QK_WIDTH), lambda b, j: (j, 0))
    const3 = pl.BlockSpec((RET_HEADS, RET_CHUNK, LANES), lambda b, j: (0, 0, 0))
    return pl.pallas_call(
        _retention_kernel,
        out_shape=jax.ShapeDtypeStruct((tokens, RET_WIDTH), BF16),
        grid=(batch, per_seq),
        in_specs=[
            pl.BlockSpec((RET_ROWS, RET_IN_WIDTH), lambda b, j: (b * per_seq + j, 0)),
            tab, tab, const3, const3, const3, const3,
            pl.BlockSpec((1, RET_WIDTH), lambda b, j: (0, 0)),
        ],
        out_specs=pl.BlockSpec((RET_ROWS, RET_WIDTH), lambda b, j: (b * per_seq + j, 0)),
        scratch_shapes=[pltpu.VMEM((RET_HEADS, LANES, RET_VALUE_DIM), F32)],
        compiler_params=_cparams(("arbitrary", "arbitrary")),
        name="retention",
    )(ret_in, cos, sin, dmask, qdec, kdec, cdec, ret_gain.reshape(1, RET_WIDTH))


def _swiglu_chunks(h, w1_ref, w3_ref, w2_ref):
    d_ff = w1_ref.shape[-1]
    total = None
    for c0 in range(0, d_ff, MXU_COLS):
        c1 = min(c0 + MXU_COLS, d_ff)
        z = (_silu(_dot(h, w1_ref[:, c0:c1].astype(BF16))) * _dot(h, w3_ref[:, c0:c1].astype(BF16))).astype(BF16)
        part = _dot(z, w2_ref[c0:c1, :].astype(BF16))
        total = part if total is None else total + part
    return total


def _out_proj_ffn_kernel(attn_ref, ret_ref, x_ref, mod_ref, wo_ref, gain_ref, w1_ref, w3_ref, w2_ref, out_ref):
    mod = mod_ref[...]
    mix = (_dot(attn_ref[...], wo_ref[:ATTN_WIDTH, :].astype(BF16))
           + _dot(ret_ref[...], wo_ref[ATTN_WIDTH:, :].astype(BF16)))
    x = x_ref[...] + mod[2:3, :] * mix
    h = _modulated_norm(x, gain_ref[...], mod[4:5], mod[3:4]).astype(BF16)
    out_ref[...] = x + mod[5:6, :] * _swiglu_chunks(h, w1_ref, w3_ref, w2_ref)


def _out_proj_dense_ffn(attn, ret, x, mod, w_out, layer, gain, w1, w3, w2, index):
    tokens = x.shape[0]
    d_ff = w1.shape[2]
    tm = TM_FFN
    per_seq = SEQ // tm
    row = lambda i: (i, 0)
    resident = pl.Buffered(1)
    return pl.pallas_call(
        _out_proj_ffn_kernel,
        out_shape=jax.ShapeDtypeStruct((tokens, D_MODEL), F32),
        grid=(tokens // tm,),
        in_specs=[
            pl.BlockSpec((tm, ATTN_WIDTH), row),
            pl.BlockSpec((tm, RET_WIDTH), row),
            pl.BlockSpec((tm, D_MODEL), row),
            pl.BlockSpec((None, 6, D_MODEL), lambda i: (i // per_seq, 0, 0)),
            pl.BlockSpec((None, D_MODEL, D_MODEL), lambda i: (layer, 0, 0), pipeline_mode=resident),
            pl.BlockSpec((1, D_MODEL), lambda i: (0, 0)),
            pl.BlockSpec((None, D_MODEL, d_ff), lambda i: (index, 0, 0), pipeline_mode=resident),
            pl.BlockSpec((None, D_MODEL, d_ff), lambda i: (index, 0, 0), pipeline_mode=resident),
            pl.BlockSpec((None, d_ff, D_MODEL), lambda i: (index, 0, 0), pipeline_mode=resident),
        ],
        out_specs=pl.BlockSpec((tm, D_MODEL), row),
        compiler_params=_cparams(("arbitrary",)),
        name="out_projection_dense_swiglu",
    )(attn, ret, x, mod, w_out, gain.reshape(1, D_MODEL), w1, w3, w2)


def _router_kernel(attn_ref, ret_ref, x_ref, mod_ref, wo_ref, gain_ref, wr_ref,
                   xo_ref, h_ref, gates_ref, pos_ref, post_ref, start_ref, cnt_ref, carry_ref, *, tiles_per_seq):
    i = pl.program_id(0)

    @pl.when(i % tiles_per_seq == 0)
    def _():
        carry_ref[...] = jnp.zeros_like(carry_ref)

    mod = mod_ref[...]
    mix = (_dot(attn_ref[...], wo_ref[:ATTN_WIDTH, :].astype(BF16))
           + _dot(ret_ref[...], wo_ref[ATTN_WIDTH:, :].astype(BF16)))
    x = x_ref[...] + mod[2:3, :] * mix
    xo_ref[...] = x
    h = _modulated_norm(x, gain_ref[...], mod[4:5], mod[3:4]).astype(BF16)
    h_ref[...] = h
    tm = h.shape[0]
    lane = lax.broadcasted_iota(jnp.int32, (tm, LANES), 1).astype(F32)
    logits = jnp.where(lane < N_EXPERTS, _dot(h, wr_ref[...]), -jnp.inf)
    m1 = jnp.max(logits, axis=-1, keepdims=True)
    i1 = jnp.min(jnp.where(logits == m1, lane, float(LANES)), axis=-1, keepdims=True)
    rest = jnp.where(lane == i1, -jnp.inf, logits)
    m2 = jnp.max(rest, axis=-1, keepdims=True)
    i2 = jnp.min(jnp.where(rest == m2, lane, float(LANES)), axis=-1, keepdims=True)
    e2 = jnp.exp(m2 - m1)
    g1 = 1.0 / (1.0 + e2)
    g2 = e2 / (1.0 + e2)
    gates_ref[...] = jnp.where(lane == i1, g1, 0.0) + jnp.where(lane == i2, g2, 0.0)
    chosen = (lane == i1) | (lane == i2)
    onehot = jnp.where(chosen, 1.0, 0.0)
    r = lax.broadcasted_iota(jnp.int32, (tm, tm), 0)
    c = lax.broadcasted_iota(jnp.int32, (tm, tm), 1)
    tril = jnp.where(c <= r, 1.0, 0.0).astype(BF16)
    incl = _dot(tril, onehot.astype(BF16))
    carry = carry_ref[0:1, :]
    start_ref[...] = carry_ref[...]
    pos = jnp.where(chosen, incl - 1.0 + carry, -1.0)
    pos_ref[...] = pos
    post_ref[...] = pos.T[:N_EXPERTS, :]
    total = carry + incl[tm - 1:tm, :]
    carry_ref[...] = jnp.broadcast_to(total, carry_ref.shape)
    cnt_ref[...] = jnp.broadcast_to(total, cnt_ref.shape)


def _out_proj_router(attn, ret, x, mod, w_out, layer, gain, w_router):
    tokens = x.shape[0]
    batch = tokens // SEQ
    tm = TM_PROJ
    per_seq = SEQ // tm
    wr = jnp.zeros((D_MODEL, LANES), BF16).at[:, :N_EXPERTS].set(w_router.astype(BF16))
    row = lambda i: (i, 0)
    return pl.pallas_call(
        functools.partial(_router_kernel, tiles_per_seq=per_seq),
        out_shape=(
            jax.ShapeDtypeStruct((tokens, D_MODEL), F32),
            jax.ShapeDtypeStruct((tokens, D_MODEL), BF16),
            jax.ShapeDtypeStruct((tokens, LANES), F32),
            jax.ShapeDtypeStruct((tokens, LANES), F32),
            jax.ShapeDtypeStruct((N_EXPERTS, tokens), F32),
            jax.ShapeDtypeStruct((tokens // tm, 8, LANES), F32),
            jax.ShapeDtypeStruct((batch, 8, LANES), F32),
        ),
        grid=(tokens // tm,),
        in_specs=[
            pl.BlockSpec((tm, ATTN_WIDTH), row),
            pl.BlockSpec((tm, RET_WIDTH), row),
            pl.BlockSpec((tm, D_MODEL), row),
            pl.BlockSpec((None, 6, D_MODEL), lambda i: (i // per_seq, 0, 0)),
            pl.BlockSpec((None, D_MODEL, D_MODEL), lambda i: (layer, 0, 0)),
            pl.BlockSpec((1, D_MODEL), lambda i: (0, 0)),
            pl.BlockSpec((D_MODEL, LANES), lambda i: (0, 0)),
        ],
        out_specs=(
            pl.BlockSpec((tm, D_MODEL), row),
            pl.BlockSpec((tm, D_MODEL), row),
            pl.BlockSpec((tm, LANES), row),
            pl.BlockSpec((tm, LANES), row),
            pl.BlockSpec((N_EXPERTS, tm), lambda i: (0, i)),
            pl.BlockSpec((None, 8, LANES), lambda i: (i, 0, 0)),
            pl.BlockSpec((None, 8, LANES), lambda i: (i // per_seq, 0, 0)),
        ),
        scratch_shapes=[pltpu.VMEM((8, LANES), F32)],
        compiler_params=_cparams(("arbitrary",)),
        name="out_projection_router",
    )(attn, ret, x, mod, w_out, gain.reshape(1, D_MODEL), wr)


TOK_BLOCKS = SEQ // TR_MOE
TAIL_TILES = (64, 128, TR_MOE)
SPILL_ROWS = 16
SMALL_WINDOW = 128
BIG_TILE = 1024
SEG_ALIGN = 16
GROUP_ROWS = 3072
PACKED_ROWS = 2 * SEQ + N_EXPERTS * SEG_ALIGN + 2 * TR_MOE
BIG_CHUNK = 128
STAGE_ROWS = SEQ + TR_MOE
N_BOUNDS = TOK_BLOCKS + 1


def _moe_plan(start, cnt):
    batch = cnt.shape[0]
    counts = cnt[:, 0, :N_EXPERTS].astype(jnp.int32)
    bounds = jnp.concatenate([start[:, 0, :N_EXPERTS].reshape(batch, TOK_BLOCKS, N_EXPERTS).astype(jnp.int32),
                              counts[:, None, :]], axis=1)
    seg = (counts + (SEG_ALIGN - 1)) // SEG_ALIGN * SEG_ALIGN
    rows_e = jnp.sum(seg, axis=0)
    groups_e = (rows_e + (GROUP_ROWS - 1)) // GROUP_ROWS
    first_group = jnp.cumsum(groups_e) - groups_e
    seg_off = first_group[None, :] * GROUP_ROWS + jnp.cumsum(seg, axis=0) - seg
    max_rows = batch * (2 * SEQ + N_EXPERTS * (SEG_ALIGN - 1))
    n_groups = max_rows // GROUP_ROWS + N_EXPERTS
    g = jnp.arange(n_groups)
    g_expert = jnp.minimum(jnp.sum(g[:, None] >= jnp.cumsum(groups_e)[None, :], axis=1), N_EXPERTS - 1)
    g_rows = jnp.clip(rows_e[g_expert] - (g - first_group[g_expert]) * GROUP_ROWS, 0, GROUP_ROWS)
    g_rows = jnp.where(g < jnp.sum(groups_e), g_rows, 0)
    return (bounds.reshape(-1), seg.reshape(-1), seg_off.reshape(-1),
            g_expert.astype(jnp.int32), g_rows.astype(jnp.int32), n_groups)


def _segment_copies(seg, make_copy, action):
    n_big = lax.shift_right_logical(seg, BIG_CHUNK.bit_length() - 1)

    def big(i, carry):
        getattr(make_copy(pl.multiple_of(i * BIG_CHUNK, BIG_CHUNK), BIG_CHUNK), action)()
        return carry
    lax.fori_loop(0, n_big, big, 0)
    rest0 = n_big * BIG_CHUNK
    n_small = lax.shift_right_logical(seg - rest0, SEG_ALIGN.bit_length() - 1)

    def small(i, carry):
        getattr(make_copy(pl.multiple_of(rest0 + i * SEG_ALIGN, SEG_ALIGN), SEG_ALIGN), action)()
        return carry
    lax.fori_loop(0, n_small, small, 0)


def _block_windows(tbl_ref, b, e, align):
    base = b * N_BOUNDS * N_EXPERTS + e
    bounds = [tbl_ref[base + tb * N_EXPERTS] for tb in range(N_BOUNDS)]
    shift = align.bit_length() - 1
    win = [pl.multiple_of(lax.shift_left(lax.shift_right_logical(s, shift), shift), align) for s in bounds[:-1]]
    all_small = functools.reduce(jnp.logical_and,
                                 [bounds[tb + 1] - win[tb] <= SMALL_WINDOW for tb in range(TOK_BLOCKS)])
    return bounds, win, all_small


def _dispatch_kernel(tbl_ref, seg_ref, off_ref, h_ref, post_ref, rows_hbm, acc_ref, stage_ref, sem):
    b = pl.program_id(0)
    tr = TR_MOE

    def copies(e, action):
        slot = e % 2
        off = off_ref[b * N_EXPERTS + e]

        def make_copy(r0, rows):
            return pltpu.make_async_copy(stage_ref.at[slot, pl.ds(r0, rows), :],
                                         rows_hbm.at[pl.ds(pl.multiple_of(off + r0, SEG_ALIGN), rows), :],
                                         sem.at[slot])
        _segment_copies(seg_ref[b * N_EXPERTS + e], make_copy, action)

    def expert(e, carry):
        bounds, win, all_small = _block_windows(tbl_ref, b, e, SUBLANES)
        n_tiles = lax.shift_right_logical(bounds[-1] + (tr - 1), tr.bit_length() - 1)

        def clear(r, c):
            acc_ref[pl.ds(pl.multiple_of(r * tr, tr), tr), :] = jnp.zeros((tr, D_MODEL), F32)
            return c
        lax.fori_loop(0, n_tiles + 2, clear, 0)

        def gather(height):
            slot_id = lax.broadcasted_iota(jnp.int32, (height, tr), 0).astype(F32)
            for tb in range(TOK_BLOCKS):
                ts = slice(tb * tr, (tb + 1) * tr)
                local = post_ref[pl.ds(e, 1), ts] - win[tb].astype(F32)
                sel = jnp.where(local == slot_id, 1.0, 0.0).astype(BF16)
                acc_ref[pl.ds(win[tb], height), :] += _dot(sel, h_ref[ts, :])

        @pl.when(all_small)
        def _():
            gather(SMALL_WINDOW)

        @pl.when(jnp.logical_not(all_small))
        def _():
            gather(tr + SPILL_ROWS)

        @pl.when(e >= 2)
        def _():
            copies(e - 2, "wait")

        def to_stage(r, c):
            rows = pl.ds(pl.multiple_of(r * tr, tr), tr)
            stage_ref[e % 2, rows, :] = acc_ref[rows, :].astype(BF16)
            return c
        lax.fori_loop(0, n_tiles, to_stage, 0)
        copies(e, "start")
        return carry
    lax.fori_loop(0, N_EXPERTS, expert, 0)
    copies(N_EXPERTS - 2, "wait")
    copies(N_EXPERTS - 1, "wait")


def _dispatch(h, post, bounds, seg, seg_off, total_rows):
    tokens = h.shape[0]
    grid_spec = pltpu.PrefetchScalarGridSpec(
        num_scalar_prefetch=3,
        grid=(tokens // SEQ,),
        in_specs=[pl.BlockSpec((SEQ, D_MODEL), lambda b, *_: (b, 0)),
                  pl.BlockSpec((N_EXPERTS, SEQ), lambda b, *_: (0, b))],
        out_specs=pl.BlockSpec(memory_space=pl.ANY),
        scratch_shapes=[pltpu.VMEM((SEQ + 3 * TR_MOE, D_MODEL), F32),
                        pltpu.VMEM((2, STAGE_ROWS, D_MODEL), BF16),
                        pltpu.SemaphoreType.DMA((2,))],
    )
    return pl.pallas_call(
        _dispatch_kernel,
        out_shape=jax.ShapeDtypeStruct((total_rows, D_MODEL), BF16),
        grid_spec=grid_spec,
        compiler_params=_cparams(("arbitrary",)),
        name="expert_dispatch",
    )(bounds, seg, seg_off, h, post)


def _experts_kernel(ge_ref, rows_ref, hs_ref, w1_ref, w3_ref, w2_ref, ys_ref, acc_ref, wb1_ref, wb3_ref, wb2_ref):
    g, f = pl.program_id(0), pl.program_id(1)
    last_f = pl.num_programs(1) - 1
    tr = TR_MOE
    rows = rows_ref[g]
    n_big = lax.shift_right_logical(rows, BIG_TILE.bit_length() - 1)
    n_full = lax.shift_right_logical(rows, tr.bit_length() - 1)
    rest = rows - n_full * tr
    n_tiles = lax.shift_right_logical(rows + (tr - 1), tr.bit_length() - 1)
    tail0 = pl.multiple_of(n_full * tr, tr)

    @pl.when(rows > 0)
    def _():
        wb1_ref[...] = w1_ref[...].astype(BF16)
        wb3_ref[...] = w3_ref[...].astype(BF16)
        wb2_ref[...] = w2_ref[...].astype(BF16)

    @pl.when(f == 0)
    def _():
        def clear(r, carry):
            acc_ref[pl.ds(pl.multiple_of(r * tr, tr), tr), :] = jnp.zeros((tr, D_MODEL), F32)
            return carry
        lax.fori_loop(0, n_tiles, clear, 0)

    def swiglu(hr, r0, m):
        if m > tr:
            acc_ref[pl.ds(r0, m), :] += _swiglu_chunks(hr, wb1_ref, wb3_ref, wb2_ref)
        else:
            z = (_silu(_dot(hr, wb1_ref[...])) * _dot(hr, wb3_ref[...])).astype(BF16)
            acc_ref[pl.ds(r0, m), :] += _dot(z, wb2_ref[...])

    def big_tile(r, carry):
        r0 = pl.multiple_of(r * BIG_TILE, BIG_TILE)
        swiglu(hs_ref[pl.ds(r0, BIG_TILE), :], r0, BIG_TILE)
        return carry
    lax.fori_loop(0, n_big, big_tile, 0)

    done = n_big * BIG_TILE
    m = BIG_TILE // 2
    while m >= tr:
        take = (rows - done) >= m

        @pl.when(take)
        def _(done=done, m=m):
            r0 = pl.multiple_of(done, tr)
            swiglu(hs_ref[pl.ds(r0, m), :], r0, m)
        done = done + jnp.where(take, m, 0)
        m //= 2
    lo = 0
    for m in TAIL_TILES:
        @pl.when((rest > lo) & (rest <= m))
        def _():
            valid = lax.broadcasted_iota(jnp.int32, (m, D_MODEL), 0) < rest
            hr = hs_ref[pl.ds(tail0, m), :]
            swiglu(jnp.where(valid, hr, jnp.zeros_like(hr)), tail0, m)
        lo = m

    @pl.when(f == last_f)
    def _():
        ys_ref[...] = jnp.zeros_like(ys_ref)

        def store(r, carry):
            rr = pl.ds(pl.multiple_of(r * tr, tr), tr)
            ys_ref[rr, :] = acc_ref[rr, :].astype(BF16)
            return carry
        lax.fori_loop(0, n_tiles, store, 0)


def _experts(hs, g_expert, g_rows, n_groups, w1, w3, w2):
    d_ff = w1.shape[2]
    tf = TF_MOE
    n_f = d_ff // tf
    chunk = lambda g, f, gr: jnp.where(gr[g] > 0, f, n_f - 1)
    grid_spec = pltpu.PrefetchScalarGridSpec(
        num_scalar_prefetch=2,
        grid=(n_groups, n_f),
        in_specs=[
            pl.BlockSpec((GROUP_ROWS, D_MODEL), lambda g, f, ge, gr: (g, 0)),
            pl.BlockSpec((None, D_MODEL, tf), lambda g, f, ge, gr: (ge[g], 0, chunk(g, f, gr))),
            pl.BlockSpec((None, D_MODEL, tf), lambda g, f, ge, gr: (ge[g], 0, chunk(g, f, gr))),
            pl.BlockSpec((None, tf, D_MODEL), lambda g, f, ge, gr: (ge[g], chunk(g, f, gr), 0)),
        ],
        out_specs=pl.BlockSpec((GROUP_ROWS, D_MODEL), lambda g, f, ge, gr: (g, 0)),
        scratch_shapes=[pltpu.VMEM((GROUP_ROWS, D_MODEL), F32),
                        pltpu.VMEM((D_MODEL, tf), BF16), pltpu.VMEM((D_MODEL, tf), BF16),
                        pltpu.VMEM((tf, D_MODEL), BF16)],
    )
    return pl.pallas_call(
        _experts_kernel,
        out_shape=jax.ShapeDtypeStruct(hs.shape, BF16),
        grid_spec=grid_spec,
        compiler_params=_cparams(("arbitrary", "arbitrary")),
        name="expert_swiglu",
    )(g_expert, g_rows, hs, w1, w3, w2)


def _combine_kernel(tbl_ref, seg_ref, off_ref, rows_hbm, pos_ref, gates_ref, x_ref, mod_ref, out_ref, buf_ref, sem):
    b = pl.program_id(0)
    n_seq = pl.num_programs(0)
    tr = TR_MOE

    def all_segments(seq, action):
        def body(e, first):
            off = off_ref[seq * N_EXPERTS + e]
            seg = seg_ref[seq * N_EXPERTS + e]

            def make_copy(r0, rows):
                return pltpu.make_async_copy(
                    rows_hbm.at[pl.ds(pl.multiple_of(off + r0, SEG_ALIGN), rows), :],
                    buf_ref.at[seq % 2, pl.ds(pl.multiple_of(first + r0, SEG_ALIGN), rows), :],
                    sem.at[seq % 2, e])
            _segment_copies(seg, make_copy, action)
            return first + seg
        lax.fori_loop(0, N_EXPERTS, body, 0)

    @pl.when(b == 0)
    def _():
        buf_ref[...] = jnp.zeros_like(buf_ref)
        all_segments(b, "start")

    @pl.when(b + 1 < n_seq)
    def _():
        all_segments(b + 1, "start")

    out_ref[...] = x_ref[...]
    lane = lax.broadcasted_iota(jnp.int32, (tr, LANES), 1)
    layer_gate = mod_ref[5:6, :]
    all_segments(b, "wait")

    def expert(e, seg_start):
        bounds, win, all_small = _block_windows(tbl_ref, b, e, SEG_ALIGN)

        def scatter(tb, first, height):
            ts = slice(tb * tr, (tb + 1) * tr)
            pos_col = jnp.sum(jnp.where(lane == e, pos_ref[ts, :], 0.0), axis=-1, keepdims=True)
            gate_col = jnp.sum(jnp.where(lane == e, gates_ref[ts, :], 0.0), axis=-1, keepdims=True)
            slot_id = lax.broadcasted_iota(jnp.int32, (tr, height), 1).astype(F32) + float(first)
            sel = jnp.where(pos_col - win[tb].astype(F32) == slot_id, 1.0, 0.0).astype(BF16)
            rows = buf_ref[b % 2, pl.ds(pl.multiple_of(seg_start + win[tb] + first, SEG_ALIGN), height), :]
            out_ref[ts, :] += layer_gate * (gate_col * _dot(sel, rows))

        @pl.when(all_small)
        def _():
            for tb in range(TOK_BLOCKS):
                scatter(tb, 0, SMALL_WINDOW)

        @pl.when(jnp.logical_not(all_small))
        def _():
            for tb in range(TOK_BLOCKS):
                scatter(tb, 0, tr)
            for tb in range(TOK_BLOCKS):
                @pl.when(bounds[tb + 1] - win[tb] > tr)
                def _():
                    scatter(tb, tr, SPILL_ROWS)
        return seg_start + seg_ref[b * N_EXPERTS + e]
    lax.fori_loop(0, N_EXPERTS, expert, 0)


def _combine(ys, pos, gates, x, mod, bounds, seg, seg_off):
    tokens = x.shape[0]
    grid_spec = pltpu.PrefetchScalarGridSpec(
        num_scalar_prefetch=3,
        grid=(tokens // SEQ,),
        in_specs=[pl.BlockSpec(memory_space=pl.ANY),
                  pl.BlockSpec((SEQ, LANES), lambda b, *_: (b, 0)),
                  pl.BlockSpec((SEQ, LANES), lambda b, *_: (b, 0)),
                  pl.BlockSpec((SEQ, D_MODEL), lambda b, *_: (b, 0)),
                  pl.BlockSpec((None, 6, D_MODEL), lambda b, *_: (b, 0, 0))],
        out_specs=pl.BlockSpec((SEQ, D_MODEL), lambda b, *_: (b, 0)),
        scratch_shapes=[pltpu.VMEM((2, PACKED_ROWS, D_MODEL), BF16),
                        pltpu.SemaphoreType.DMA((2, N_EXPERTS))],
    )
    return pl.pallas_call(
        _combine_kernel,
        out_shape=jax.ShapeDtypeStruct((tokens, D_MODEL), F32),
        grid_spec=grid_spec,
        compiler_params=_cparams(("arbitrary",)),
        name="expert_combine",
    )(bounds, seg, seg_off, ys, pos, gates, x, mod)


def _token_mixer(x, mod, gain, w_in, layer, q_gain, k_gain, ret_gain, bias_masks):
    *qkv, ret_in = _in_proj(x, mod, gain, w_in, layer, q_gain, k_gain)
    return _dilated_attention(qkv, bias_masks), _retention(ret_in, ret_gain)


def _out_proj_moe_ffn(attn, ret, x, mod, w_out, layer, gain, w_router, w1, w3, w2):
    x, h, gates, pos, post, start, cnt = _out_proj_router(attn, ret, x, mod, w_out, layer, gain, w_router)
    bounds, seg, seg_off, g_expert, g_rows, n_groups = _moe_plan(start, cnt)
    hs = _dispatch(h, post, bounds, seg, seg_off, n_groups * GROUP_ROWS)
    ys = _experts(hs, g_expert, g_rows, n_groups, w1, w3, w2)
    return _combine(ys, pos, gates, x, mod, bounds, seg, seg_off)


def kernel(x, c, rel_bias_table, norm_mix, norm_ffn, w_mod, b_mod, w_in, q_gain, k_gain, ret_gain, w_out,
           ffn_w1, ffn_w3, ffn_w2, moe_router, moe_w1, moe_w3, moe_w2):
    batch, seq, d_model = x.shape
    assert (seq, d_model) == (SEQ, D_MODEL)
    depth = w_mod.shape[0]
    mods = _modulation(c, w_mod, b_mod).reshape(depth, batch, 6, D_MODEL)
    bias_masks = _bias_masks(rel_bias_table)
    xt = x.reshape(batch * seq, d_model)
    for layer in range(depth):
        mod = mods[layer]
        attn, ret = _token_mixer(xt, mod, norm_mix[layer], w_in, layer, q_gain[layer], k_gain[layer],
                                 ret_gain[layer], bias_masks)
        i = layer // 2
        if layer % 2 == 0:
            xt = _out_proj_dense_ffn(attn, ret, xt, mod, w_out, layer, norm_ffn[layer],
                                     ffn_w1, ffn_w3, ffn_w2, i)
        else:
            xt = _out_proj_moe_ffn(attn, ret, xt, mod, w_out, layer, norm_ffn[layer],
                                   moe_router[i], moe_w1[i], moe_w3[i], moe_w2[i])
    return xt.reshape(batch, seq, d_model)
```

```python
import functools
import math

import jax
import jax.numpy as jnp
import numpy as np
from jax import lax
from jax.experimental import pallas as pl
from jax.experimental.pallas import tpu as pltpu

D_MODEL = 1024
SEQ = 2048
ATTN_HEADS = 8
ATTN_HEAD_DIM = 64
ATTN_WIDTH = ATTN_HEADS * ATTN_HEAD_DIM
DILATED_PATTERNS = ((128, 1), (512, 4), (2048, 16))
BLOCK = 128
NUM_BUCKETS = 32
MAX_DISTANCE = 2048
RET_HEADS = 4
RET_KEY_DIM = 64
RET_VALUE_DIM = 128
RET_WIDTH = RET_HEADS * RET_VALUE_DIM
RET_QK_WIDTH = RET_HEADS * RET_KEY_DIM
RET_CHUNK = 128
ROPE_BASE = 10000.0
IN_WIDTH = 3 * ATTN_WIDTH + 2 * RET_QK_WIDTH + 2 * RET_WIDTH
RET_IN_WIDTH = IN_WIDTH - 3 * ATTN_WIDTH
N_EXPERTS = 8
EPS = 1e-6
NEG_INF = -1e30

LANES = 128
VMEM_LIMIT = 60 * 1024 * 1024

BF16 = jnp.bfloat16
F32 = jnp.float32

TM_PROJ = 256
TM_IN = 512
IN_SUBTILES = 2
TM_FFN = 512
MXU_COLS = 256
TR_MOE = 256
TF_MOE = 512
assert TM_PROJ == TR_MOE
RET_ROWS = 512
SUBLANES = 8


def _cparams(sem):
    return pltpu.CompilerParams(dimension_semantics=sem, vmem_limit_bytes=VMEM_LIMIT)


def _dot(a, b):
    return jnp.dot(a, b, preferred_element_type=F32)


def _dot_nt(a, b):
    return lax.dot_general(a, b, (((1,), (1,)), ((), ())), preferred_element_type=F32)


def _dot_tn(a, b):
    return lax.dot_general(a, b, (((0,), (0,)), ((), ())), preferred_element_type=F32)


def _split_bf16(v):
    hi = v.astype(BF16)
    lo = (v - hi.astype(F32)).astype(BF16)
    return hi, lo


def _silu(v):
    return v * (1.0 / (1.0 + jnp.exp(-v)))


def _modulated_norm(x, gain, scale, shift):
    ms = jnp.mean(x * x, axis=-1, keepdims=True)
    y = x * lax.rsqrt(ms + EPS) * gain
    return y * (1.0 + scale) + shift


def _mod_kernel(c_ref, w_ref, b_ref, o_ref):
    ca = _silu(c_ref[...]).astype(BF16)
    o_ref[...] = _dot(ca, w_ref[...].astype(BF16)) + b_ref[...]


def _modulation(c, w_mod, b_mod):
    depth, _, width = w_mod.shape
    batch = c.shape[0]
    tn = 1536
    return pl.pallas_call(
        _mod_kernel,
        out_shape=jax.ShapeDtypeStruct((depth, batch, width), F32),
        grid=(depth, width // tn),
        in_specs=[
            pl.BlockSpec((batch, D_MODEL), lambda l, n: (0, 0)),
            pl.BlockSpec((None, D_MODEL, tn), lambda l, n: (l, 0, n)),
            pl.BlockSpec((None, 1, tn), lambda l, n: (l, 0, n)),
        ],
        out_specs=pl.BlockSpec((None, batch, tn), lambda l, n: (l, 0, n)),
        compiler_params=_cparams(("arbitrary", "arbitrary")),
        name="adaln_modulation",
    )(c, w_mod, b_mod.reshape(depth, 1, width))


def _bias_kernel(table_ref, bucket_ref, o_ref):
    bucket = bucket_ref[...]
    acc = [jnp.full(bucket.shape, NEG_INF, F32) for _ in range(ATTN_HEADS)]
    for b in range(NUM_BUCKETS):
        hit = bucket == b
        for h in range(ATTN_HEADS):
            acc[h] = jnp.where(hit, table_ref[b, h], acc[h])
    for h in range(ATTN_HEADS):
        o_ref[h // 2, (h % 2) * BLOCK:(h % 2 + 1) * BLOCK, :] = acc[h]


BIAS_FULL = {1: 0, 4: 2}
BIAS_FIRST = {1: 1, 4: 3, 16: 4}
N_BIAS_SETS = 5


def _bias_masks(rel_bias_table):
    i = jnp.arange(BLOCK)[:, None]
    j = jnp.arange(2 * BLOCK)[None, :]
    max_exact = NUM_BUCKETS // 2

    def bucket_of(rel, dilation, w_sub, exists):
        n = jnp.maximum(rel * dilation, 0)
        nf = jnp.maximum(n.astype(F32), float(max_exact))
        large = max_exact + (jnp.log(nf / max_exact) / math.log(MAX_DISTANCE / max_exact)
                             * (NUM_BUCKETS - max_exact)).astype(jnp.int32)
        large = jnp.minimum(large, NUM_BUCKETS - 1)
        bucket = jnp.where(n < max_exact, n, large)
        allowed = (rel >= 0) & (rel <= w_sub) & exists
        return jnp.where(allowed, bucket, -1)

    sets = [None] * N_BIAS_SETS
    for window, dilation in DILATED_PATTERNS:
        w_sub = window // dilation
        if dilation in BIAS_FULL:
            sets[BIAS_FULL[dilation]] = bucket_of(i - j + BLOCK, dilation, w_sub, j >= 0)
        sets[BIAS_FIRST[dilation]] = bucket_of(i - j, dilation, w_sub, j < BLOCK)
    buckets = jnp.stack(sets).astype(jnp.int32)
    return pl.pallas_call(
        _bias_kernel,
        out_shape=jax.ShapeDtypeStruct((N_BIAS_SETS, ATTN_HEADS // 2, 2 * BLOCK, 2 * BLOCK), F32),
        grid=(N_BIAS_SETS,),
        in_specs=[
            pl.BlockSpec(memory_space=pltpu.SMEM),
            pl.BlockSpec((None, BLOCK, 2 * BLOCK), lambda p: (p, 0, 0)),
        ],
        out_specs=pl.BlockSpec((None, ATTN_HEADS // 2, 2 * BLOCK, 2 * BLOCK), lambda p: (p, 0, 0, 0)),
        compiler_params=_cparams(("arbitrary",)),
        name="relative_bias_masks",
    )(rel_bias_table, buckets)


HALF = ATTN_WIDTH // 2


def _in_proj_kernel(x_ref, mod_ref, gain_ref, w_ref, qg_ref, kg_ref, grp_ref,
                    q1_ref, k1_ref, v1_ref, q4_ref, k4_ref, v4_ref, q16_ref, k16_ref, v16_ref, r_ref,
                    perm_ref):
    mod = mod_ref[...]
    grp = grp_ref[...]
    sub = TM_IN // IN_SUBTILES

    def head_norm(t, gain):
        hi, lo = _split_bf16(t * t)
        ss = _dot(hi, grp) + _dot(lo, grp)
        return t * lax.rsqrt(ss * (1.0 / ATTN_HEAD_DIM) + EPS) * gain

    for st in range(IN_SUBTILES):
        rows = slice(st * sub, (st + 1) * sub)
        h = _modulated_norm(x_ref[rows, :], gain_ref[...], mod[1:2], mod[0:1]).astype(BF16)
        proj = _dot(h, w_ref[...].astype(BF16))

        def emit(t, o1_ref, o4_ref, o16_ref):
            o1_ref[rows, :] = t.astype(BF16)
            for j in range(ATTN_WIDTH // LANES):
                perm_ref[st, j] = t[:, j * LANES:(j + 1) * LANES]
            for dil, o_ref in ((4, o4_ref), (16, o16_ref)):
                orows = slice(st * sub // dil, (st + 1) * sub // dil)
                for hf in range(2):
                    for r in range(dil):
                        for jj in range(HALF // LANES):
                            c0 = (hf * dil + r) * HALF + jj * LANES
                            o_ref[orows, c0:c0 + LANES] = perm_ref[st, hf * (HALF // LANES) + jj,
                                                                   pl.ds(r, sub // dil, stride=dil), :].astype(BF16)

        emit(head_norm(proj[:, :ATTN_WIDTH], qg_ref[...]) * (ATTN_HEAD_DIM ** -0.5), q1_ref, q4_ref, q16_ref)
        emit(head_norm(proj[:, ATTN_WIDTH:2 * ATTN_WIDTH], kg_ref[...]), k1_ref, k4_ref, k16_ref)
        emit(proj[:, 2 * ATTN_WIDTH:3 * ATTN_WIDTH], v1_ref, v4_ref, v16_ref)
        r_ref[rows, :] = proj[:, 3 * ATTN_WIDTH:]


def _in_proj(x, mod, gain, w_in, layer, q_gain, k_gain):
    tokens = x.shape[0]
    tm = TM_IN
    per_seq = SEQ // tm
    grp = np.kron(np.eye(ATTN_HEADS), np.ones((ATTN_HEAD_DIM, ATTN_HEAD_DIM))).astype(np.float32)
    row = lambda i: (i, 0)
    const = lambda i: (0, 0)
    layouts = []
    for dil in (1, 4, 16):
        shape = jax.ShapeDtypeStruct((tokens // dil, dil * ATTN_WIDTH), BF16)
        spec = pl.BlockSpec((tm // dil, dil * ATTN_WIDTH), row)
        layouts.append(((shape,) * 3, (spec,) * 3))
    out_shape = sum((s for s, _ in layouts), ()) + (jax.ShapeDtypeStruct((tokens, RET_IN_WIDTH), F32),)
    out_specs = sum((s for _, s in layouts), ()) + (pl.BlockSpec((tm, RET_IN_WIDTH), row),)
    return pl.pallas_call(
        _in_proj_kernel,
        out_shape=out_shape,
        grid=(tokens // tm,),
        in_specs=[
            pl.BlockSpec((tm, D_MODEL), row),
            pl.BlockSpec((None, 6, D_MODEL), lambda i: (i // per_seq, 0, 0)),
            pl.BlockSpec((1, D_MODEL), const),
            pl.BlockSpec((None, D_MODEL, IN_WIDTH), lambda i: (layer, 0, 0), pipeline_mode=pl.Buffered(1)),
            pl.BlockSpec((1, ATTN_WIDTH), const),
            pl.BlockSpec((1, ATTN_WIDTH), const),
            pl.BlockSpec((ATTN_WIDTH, ATTN_WIDTH), const),
        ],
        out_specs=out_specs,
        scratch_shapes=[pltpu.VMEM((IN_SUBTILES, ATTN_WIDTH // LANES, tm // IN_SUBTILES, LANES), F32)],
        compiler_params=_cparams(("arbitrary",)),
        name="in_projection",
    )(x, mod, gain.reshape(1, D_MODEL), w_in,
      jnp.tile(q_gain, ATTN_HEADS).reshape(1, ATTN_WIDTH),
      jnp.tile(k_gain, ATTN_HEADS).reshape(1, ATTN_WIDTH),
      jnp.asarray(grp, BF16))


PAIRS_PER_HALF = ATTN_HEADS // 4
GROUP = 4


def _pair_scores(qp, kp, vp, bias2, masks, low):
    q2 = jnp.concatenate([qp * masks[0], qp * masks[1]], axis=0)
    s = _dot_nt(q2, kp) + bias2
    m = jnp.max(s, axis=-1, keepdims=True)
    p = jnp.exp(s - m)
    den = jnp.sum(p, axis=-1, keepdims=True)
    pv = _dot(p.astype(BF16), vp)
    pick = lambda t: jnp.where(low, t[:BLOCK], t[BLOCK:])
    return pick(pv), pick(m), pick(den)


def _attn_kernel(q1_ref, k1_ref, v1_ref, q4_ref, k4_ref, v4_ref, q16_ref, k16_ref, v16_ref, bm_ref,
                 o_ref, acc_ref, max_ref, den_ref):
    lane = lax.broadcasted_iota(jnp.int32, (BLOCK, LANES), 1)
    low = lane < ATTN_HEAD_DIM
    masks = (jnp.where(low, 1.0, 0.0).astype(BF16), jnp.where(low, 0.0, 1.0).astype(BF16))

    def block(q_ref, k_ref, v_ref, c0, q0, w0, width, bias_set, rows, first):
        for p in range(PAIRS_PER_HALF):
            cs = slice(c0 + p * LANES, c0 + (p + 1) * LANES)
            acc, m, den = _pair_scores(q_ref[pl.ds(q0, BLOCK), cs], k_ref[pl.ds(w0, width), cs],
                                       v_ref[pl.ds(w0, width), cs], bm_ref[bias_set, p, :, 0:width],
                                       masks, low)
            if not first:
                m_old = max_ref[p, rows, :]
                m_new = jnp.maximum(m_old, m)
                a, b = jnp.exp(m_old - m_new), jnp.exp(m - m_new)
                den = den_ref[p, rows, :] * a + den * b
                acc = acc_ref[p, rows, :] * a + acc * b
                m = m_new
            max_ref[p, rows, :] = m
            den_ref[p, rows, :] = den
            acc_ref[p, rows, :] = acc

    def d1_group(g, carry):
        for u in range(GROUP):
            n = g * GROUP + u
            q0 = pl.multiple_of(n * BLOCK, BLOCK)
            w0 = pl.multiple_of(jnp.maximum(n - 1, 0) * BLOCK, BLOCK)
            bias_set = jnp.where(n == 0, BIAS_FIRST[1], BIAS_FULL[1])
            block(q1_ref, k1_ref, v1_ref, 0, q0, w0, 2 * BLOCK, bias_set, pl.ds(q0, BLOCK), True)
        return carry
    lax.fori_loop(0, SEQ // BLOCK // GROUP, d1_group, 0)

    for r in range(4):
        for n in range(SEQ // 4 // BLOCK):
            block(q4_ref, k4_ref, v4_ref, r * HALF, n * BLOCK, max(n - 1, 0) * BLOCK, 2 * BLOCK,
                  BIAS_FIRST[4] if n == 0 else BIAS_FULL[4],
                  pl.ds(r + 4 * BLOCK * n, BLOCK, stride=4), False)

    for r in range(16):
        block(q16_ref, k16_ref, v16_ref, r * HALF, 0, 0, BLOCK, BIAS_FIRST[16],
              pl.ds(r, BLOCK, stride=16), False)

    for n in range(SEQ // BLOCK):
        rows = slice(n * BLOCK, (n + 1) * BLOCK)
        for p in range(PAIRS_PER_HALF):
            o_ref[rows, p * LANES:(p + 1) * LANES] = (acc_ref[p, rows, :] / den_ref[p, rows, :]).astype(BF16)


def _dilated_attention(qkv, bias_masks):
    tokens = qkv[0].shape[0]
    batch = tokens // SEQ
    specs = []
    for dil in (1, 4, 16):
        specs += [pl.BlockSpec((SEQ // dil, dil * HALF), lambda b, hf: (b, hf))] * 3
    state = pltpu.VMEM((PAIRS_PER_HALF, SEQ, LANES), F32)
    return pl.pallas_call(
        _attn_kernel,
        out_shape=jax.ShapeDtypeStruct((tokens, ATTN_WIDTH), BF16),
        grid=(batch, 2),
        in_specs=specs + [pl.BlockSpec((N_BIAS_SETS, PAIRS_PER_HALF, 2 * BLOCK, 2 * BLOCK),
                                       lambda b, hf: (0, hf, 0, 0))],
        out_specs=pl.BlockSpec((SEQ, HALF), lambda b, hf: (b, hf)),
        scratch_shapes=[state, state, state],
        compiler_params=_cparams(("arbitrary", "arbitrary")),
        name="dilated_attention",
    )(*qkv, bias_masks)


def _retention_kernel(r_ref, cos_ref, sin_ref, dmask_ref, qdec_ref, kdec_ref, cdec_ref, gain_ref,
                      o_ref, state_ref):
    @pl.when(pl.program_id(1) == 0)
    def _():
        state_ref[...] = jnp.zeros_like(state_ref)

    lane = lax.broadcasted_iota(jnp.int32, (RET_CHUNK, LANES), 1)
    low = lane < RET_KEY_DIM
    first_half = (lane % RET_KEY_DIM) < (RET_KEY_DIM // 2)

    def rotate(t, cos, sin):
        partner = jnp.where(first_half, pltpu.roll(t, LANES - RET_KEY_DIM // 2, 1),
                            pltpu.roll(t, RET_KEY_DIM // 2, 1))
        return t * cos + partner * sin

    for c in range(RET_ROWS // RET_CHUNK):
        rows = slice(c * RET_CHUNK, (c + 1) * RET_CHUNK)
        for hp in range(RET_HEADS // 2):
            qs = slice(hp * LANES, (hp + 1) * LANES)
            ks = slice(RET_QK_WIDTH + hp * LANES, RET_QK_WIDTH + (hp + 1) * LANES)
            cos, sin = cos_ref[rows, qs], sin_ref[rows, qs]
            q_pair = rotate(r_ref[rows, qs], cos, sin) * (RET_KEY_DIM ** -0.5)
            k_pair = rotate(r_ref[rows, ks], cos, sin)
            for hh in range(2):
                head = 2 * hp + hh
                vs = slice(2 * RET_QK_WIDTH + head * LANES, 2 * RET_QK_WIDTH + (head + 1) * LANES)
                gs = slice(2 * RET_QK_WIDTH + RET_WIDTH + head * LANES,
                           2 * RET_QK_WIDTH + RET_WIDTH + (head + 1) * LANES)
                keep = low if hh == 0 else jnp.logical_not(low)
                qm = jnp.where(keep, q_pair, 0.0)
                vb = r_ref[rows, vs].astype(BF16)
                state = state_ref[head]
                inner = _dot_nt(qm.astype(BF16), k_pair.astype(BF16)) * dmask_ref[head]
                y = _dot(inner.astype(BF16), vb)
                y = y + _dot((qm * qdec_ref[head]).astype(BF16), state.astype(BF16))
                state_ref[head] = state * cdec_ref[head] + _dot_tn((k_pair * kdec_ref[head]).astype(BF16), vb)
                mu = jnp.mean(y, axis=-1, keepdims=True)
                yc = y - mu
                var = jnp.mean(yc * yc, axis=-1, keepdims=True)
                yn = yc * lax.rsqrt(var + EPS) * gain_ref[:, head * LANES:(head + 1) * LANES]
                o_ref[rows, head * LANES:(head + 1) * LANES] = (_silu(r_ref[rows, gs]) * yn).astype(BF16)


def _retention_tables():
    half = RET_KEY_DIM // 2
    pos = jnp.arange(SEQ, dtype=F32)
    inv = ROPE_BASE ** (-jnp.arange(half, dtype=F32) / half)
    ang = pos[:, None] * inv[None, :]
    cos, sin = jnp.cos(ang), jnp.sin(ang)
    cos_full = jnp.tile(jnp.concatenate([cos, cos], axis=-1), (1, RET_HEADS))
    sin_signed = jnp.tile(jnp.concatenate([-sin, sin], axis=-1), (1, RET_HEADS))
    log_g = jnp.log(1.0 - 2.0 ** (-5.0 - jnp.arange(RET_HEADS, dtype=F32)))
    idx = jnp.arange(RET_CHUNK, dtype=F32)
    diff = idx[:, None] - idx[None, :]
    dmask = jnp.where(diff >= 0, jnp.exp(jnp.maximum(diff, 0.0)[None] * log_g[:, None, None]), 0.0)
    q_decay = jnp.exp((idx + 1.0)[None, :] * log_g[:, None])[..., None]
    k_decay = jnp.exp((RET_CHUNK - 1.0 - idx)[None, :] * log_g[:, None])[..., None]
    chunk_decay = jnp.exp(RET_CHUNK * log_g)[:, None, None]
    full = (RET_HEADS, RET_CHUNK, LANES)
    return (cos_full, sin_signed, dmask, jnp.broadcast_to(q_decay, full),
            jnp.broadcast_to(k_decay, full), jnp.broadcast_to(chunk_decay, full))


def _retention(ret_in, ret_gain):
    tokens = ret_in.shape[0]
    batch = tokens // SEQ
    per_seq = SEQ // RET_ROWS
    cos, sin, dmask, qdec, kdec, cdec = _retention_tables()
    tab = pl.BlockSpec((RET_ROWS, RET_QK_WIDTH), lambda b, j: (j, 0))
    const3 = pl.BlockSpec((RET_HEADS, RET_CHUNK, LANES), lambda b, j: (0, 0, 0))
    return pl.pallas_call(
        _retention_kernel,
        out_shape=jax.ShapeDtypeStruct((tokens, RET_WIDTH), BF16),
        grid=(batch, per_seq),
        in_specs=[
            pl.BlockSpec((RET_ROWS, RET_IN_WIDTH), lambda b, j: (b * per_seq + j, 0)),
            tab, tab, const3, const3, const3, const3,
            pl.BlockSpec((1, RET_WIDTH), lambda b, j: (0, 0)),
        ],
        out_specs=pl.BlockSpec((RET_ROWS, RET_WIDTH), lambda b, j: (b * per_seq + j, 0)),
        scratch_shapes=[pltpu.VMEM((RET_HEADS, LANES, RET_VALUE_DIM), F32)],
        compiler_params=_cparams(("arbitrary", "arbitrary")),
        name="retention",
    )(ret_in, cos, sin, dmask, qdec, kdec, cdec, ret_gain.reshape(1, RET_WIDTH))


def _swiglu_chunks(h, w1_ref, w3_ref, w2_ref):
    d_ff = w1_ref.shape[-1]
    total = None
    for c0 in range(0, d_ff, MXU_COLS):
        c1 = min(c0 + MXU_COLS, d_ff)
        z = (_silu(_dot(h, w1_ref[:, c0:c1].astype(BF16))) * _dot(h, w3_ref[:, c0:c1].astype(BF16))).astype(BF16)
        part = _dot(z, w2_ref[c0:c1, :].astype(BF16))
        total = part if total is None else total + part
    return total


def _out_proj_ffn_kernel(attn_ref, ret_ref, x_ref, mod_ref, wo_ref, gain_ref, w1_ref, w3_ref, w2_ref, out_ref):
    mod = mod_ref[...]
    mix = (_dot(attn_ref[...], wo_ref[:ATTN_WIDTH, :].astype(BF16))
           + _dot(ret_ref[...], wo_ref[ATTN_WIDTH:, :].astype(BF16)))
    x = x_ref[...] + mod[2:3, :] * mix
    h = _modulated_norm(x, gain_ref[...], mod[4:5], mod[3:4]).astype(BF16)
    out_ref[...] = x + mod[5:6, :] * _swiglu_chunks(h, w1_ref, w3_ref, w2_ref)


def _out_proj_dense_ffn(attn, ret, x, mod, w_out, layer, gain, w1, w3, w2, index):
    tokens = x.shape[0]
    d_ff = w1.shape[2]
    tm = TM_FFN
    per_seq = SEQ // tm
    row = lambda i: (i, 0)
    resident = pl.Buffered(1)
    return pl.pallas_call(
        _out_proj_ffn_kernel,
        out_shape=jax.ShapeDtypeStruct((tokens, D_MODEL), F32),
        grid=(tokens // tm,),
        in_specs=[
            pl.BlockSpec((tm, ATTN_WIDTH), row),
            pl.BlockSpec((tm, RET_WIDTH), row),
            pl.BlockSpec((tm, D_MODEL), row),
            pl.BlockSpec((None, 6, D_MODEL), lambda i: (i // per_seq, 0, 0)),
            pl.BlockSpec((None, D_MODEL, D_MODEL), lambda i: (layer, 0, 0), pipeline_mode=resident),
            pl.BlockSpec((1, D_MODEL), lambda i: (0, 0)),
            pl.BlockSpec((None, D_MODEL, d_ff), lambda i: (index, 0, 0), pipeline_mode=resident),
            pl.BlockSpec((None, D_MODEL, d_ff), lambda i: (index, 0, 0), pipeline_mode=resident),
            pl.BlockSpec((None, d_ff, D_MODEL), lambda i: (index, 0, 0), pipeline_mode=resident),
        ],
        out_specs=pl.BlockSpec((tm, D_MODEL), row),
        compiler_params=_cparams(("arbitrary",)),
        name="out_projection_dense_swiglu",
    )(attn, ret, x, mod, w_out, gain.reshape(1, D_MODEL), w1, w3, w2)


def _router_kernel(attn_ref, ret_ref, x_ref, mod_ref, wo_ref, gain_ref, wr_ref,
                   xo_ref, h_ref, gates_ref, pos_ref, post_ref, start_ref, cnt_ref, carry_ref, *, tiles_per_seq):
    i = pl.program_id(0)

    @pl.when(i % tiles_per_seq == 0)
    def _():
        carry_ref[...] = jnp.zeros_like(carry_ref)

    mod = mod_ref[...]
    mix = (_dot(attn_ref[...], wo_ref[:ATTN_WIDTH, :].astype(BF16))
           + _dot(ret_ref[...], wo_ref[ATTN_WIDTH:, :].astype(BF16)))
    x = x_ref[...] + mod[2:3, :] * mix
    xo_ref[...] = x
    h = _modulated_norm(x, gain_ref[...], mod[4:5], mod[3:4]).astype(BF16)
    h_ref[...] = h
    tm = h.shape[0]
    lane = lax.broadcasted_iota(jnp.int32, (tm, LANES), 1).astype(F32)
    logits = jnp.where(lane < N_EXPERTS, _dot(h, wr_ref[...]), -jnp.inf)
    m1 = jnp.max(logits, axis=-1, keepdims=True)
    i1 = jnp.min(jnp.where(logits == m1, lane, float(LANES)), axis=-1, keepdims=True)
    rest = jnp.where(lane == i1, -jnp.inf, logits)
    m2 = jnp.max(rest, axis=-1, keepdims=True)
    i2 = jnp.min(jnp.where(rest == m2, lane, float(LANES)), axis=-1, keepdims=True)
    e2 = jnp.exp(m2 - m1)
    g1 = 1.0 / (1.0 + e2)
    g2 = e2 / (1.0 + e2)
    gates_ref[...] = jnp.where(lane == i1, g1, 0.0) + jnp.where(lane == i2, g2, 0.0)
    chosen = (lane == i1) | (lane == i2)
    onehot = jnp.where(chosen, 1.0, 0.0)
    r = lax.broadcasted_iota(jnp.int32, (tm, tm), 0)
    c = lax.broadcasted_iota(jnp.int32, (tm, tm), 1)
    tril = jnp.where(c <= r, 1.0, 0.0).astype(BF16)
    incl = _dot(tril, onehot.astype(BF16))
    carry = carry_ref[0:1, :]
    start_ref[...] = carry_ref[...]
    pos = jnp.where(chosen, incl - 1.0 + carry, -1.0)
    pos_ref[...] = pos
    post_ref[...] = pos.T[:N_EXPERTS, :]
    total = carry + incl[tm - 1:tm, :]
    carry_ref[...] = jnp.broadcast_to(total, carry_ref.shape)
    cnt_ref[...] = jnp.broadcast_to(total, cnt_ref.shape)


def _out_proj_router(attn, ret, x, mod, w_out, layer, gain, w_router):
    tokens = x.shape[0]
    batch = tokens // SEQ
    tm = TM_PROJ
    per_seq = SEQ // tm
    wr = jnp.zeros((D_MODEL, LANES), BF16).at[:, :N_EXPERTS].set(w_router.astype(BF16))
    row = lambda i: (i, 0)
    return pl.pallas_call(
        functools.partial(_router_kernel, tiles_per_seq=per_seq),
        out_shape=(
            jax.ShapeDtypeStruct((tokens, D_MODEL), F32),
            jax.ShapeDtypeStruct((tokens, D_MODEL), BF16),
            jax.ShapeDtypeStruct((tokens, LANES), F32),
            jax.ShapeDtypeStruct((tokens, LANES), F32),
            jax.ShapeDtypeStruct((N_EXPERTS, tokens), F32),
            jax.ShapeDtypeStruct((tokens // tm, 8, LANES), F32),
            jax.ShapeDtypeStruct((batch, 8, LANES), F32),
        ),
        grid=(tokens // tm,),
        in_specs=[
            pl.BlockSpec((tm, ATTN_WIDTH), row),
            pl.BlockSpec((tm, RET_WIDTH), row),
            pl.BlockSpec((tm, D_MODEL), row),
            pl.BlockSpec((None, 6, D_MODEL), lambda i: (i // per_seq, 0, 0)),
            pl.BlockSpec((None, D_MODEL, D_MODEL), lambda i: (layer, 0, 0)),
            pl.BlockSpec((1, D_MODEL), lambda i: (0, 0)),
            pl.BlockSpec((D_MODEL, LANES), lambda i: (0, 0)),
        ],
        out_specs=(
            pl.BlockSpec((tm, D_MODEL), row),
            pl.BlockSpec((tm, D_MODEL), row),
            pl.BlockSpec((tm, LANES), row),
            pl.BlockSpec((tm, LANES), row),
            pl.BlockSpec((N_EXPERTS, tm), lambda i: (0, i)),
            pl.BlockSpec((None, 8, LANES), lambda i: (i, 0, 0)),
            pl.BlockSpec((None, 8, LANES), lambda i: (i // per_seq, 0, 0)),
        ),
        scratch_shapes=[pltpu.VMEM((8, LANES), F32)],
        compiler_params=_cparams(("arbitrary",)),
        name="out_projection_router",
    )(attn, ret, x, mod, w_out, gain.reshape(1, D_MODEL), wr)


TOK_BLOCKS = SEQ // TR_MOE
TAIL_TILES = (64, 128, TR_MOE)
SPILL_ROWS = 16
SMALL_WINDOW = 128
BIG_TILE = 1024
SEG_ALIGN = 16
GROUP_ROWS = 3072
PACKED_ROWS = 2 * SEQ + N_EXPERTS * SEG_ALIGN + 2 * TR_MOE
BIG_CHUNK = 128
STAGE_ROWS = SEQ + TR_MOE
N_BOUNDS = TOK_BLOCKS + 1


def _moe_plan(start, cnt):
    batch = cnt.shape[0]
    counts = cnt[:, 0, :N_EXPERTS].astype(jnp.int32)
    bounds = jnp.concatenate([start[:, 0, :N_EXPERTS].reshape(batch, TOK_BLOCKS, N_EXPERTS).astype(jnp.int32),
                              counts[:, None, :]], axis=1)
    seg = (counts + (SEG_ALIGN - 1)) // SEG_ALIGN * SEG_ALIGN
    rows_e = jnp.sum(seg, axis=0)
    groups_e = (rows_e + (GROUP_ROWS - 1)) // GROUP_ROWS
    first_group = jnp.cumsum(groups_e) - groups_e
    seg_off = first_group[None, :] * GROUP_ROWS + jnp.cumsum(seg, axis=0) - seg
    max_rows = batch * (2 * SEQ + N_EXPERTS * (SEG_ALIGN - 1))
    n_groups = max_rows // GROUP_ROWS + N_EXPERTS
    g = jnp.arange(n_groups)
    g_expert = jnp.minimum(jnp.sum(g[:, None] >= jnp.cumsum(groups_e)[None, :], axis=1), N_EXPERTS - 1)
    g_rows = jnp.clip(rows_e[g_expert] - (g - first_group[g_expert]) * GROUP_ROWS, 0, GROUP_ROWS)
    g_rows = jnp.where(g < jnp.sum(groups_e), g_rows, 0)
    return (bounds.reshape(-1), seg.reshape(-1), seg_off.reshape(-1),
            g_expert.astype(jnp.int32), g_rows.astype(jnp.int32), n_groups)


def _segment_copies(seg, make_copy, action):
    n_big = lax.shift_right_logical(seg, BIG_CHUNK.bit_length() - 1)

    def big(i, carry):
        getattr(make_copy(pl.multiple_of(i * BIG_CHUNK, BIG_CHUNK), BIG_CHUNK), action)()
        return carry
    lax.fori_loop(0, n_big, big, 0)
    rest0 = n_big * BIG_CHUNK
    n_small = lax.shift_right_logical(seg - rest0, SEG_ALIGN.bit_length() - 1)

    def small(i, carry):
        getattr(make_copy(pl.multiple_of(rest0 + i * SEG_ALIGN, SEG_ALIGN), SEG_ALIGN), action)()
        return carry
    lax.fori_loop(0, n_small, small, 0)


def _block_windows(tbl_ref, b, e, align):
    base = b * N_BOUNDS * N_EXPERTS + e
    bounds = [tbl_ref[base + tb * N_EXPERTS] for tb in range(N_BOUNDS)]
    shift = align.bit_length() - 1
    win = [pl.multiple_of(lax.shift_left(lax.shift_right_logical(s, shift), shift), align) for s in bounds[:-1]]
    all_small = functools.reduce(jnp.logical_and,
                                 [bounds[tb + 1] - win[tb] <= SMALL_WINDOW for tb in range(TOK_BLOCKS)])
    return bounds, win, all_small


def _dispatch_kernel(tbl_ref, seg_ref, off_ref, h_ref, post_ref, rows_hbm, acc_ref, stage_ref, sem):
    b = pl.program_id(0)
    tr = TR_MOE

    def copies(e, action):
        slot = e % 2
        off = off_ref[b * N_EXPERTS + e]

        def make_copy(r0, rows):
            return pltpu.make_async_copy(stage_ref.at[slot, pl.ds(r0, rows), :],
                                         rows_hbm.at[pl.ds(pl.multiple_of(off + r0, SEG_ALIGN), rows), :],
                                         sem.at[slot])
        _segment_copies(seg_ref[b * N_EXPERTS + e], make_copy, action)

    def expert(e, carry):
        bounds, win, all_small = _block_windows(tbl_ref, b, e, SUBLANES)
        n_tiles = lax.shift_right_logical(bounds[-1] + (tr - 1), tr.bit_length() - 1)

        def clear(r, c):
            acc_ref[pl.ds(pl.multiple_of(r * tr, tr), tr), :] = jnp.zeros((tr, D_MODEL), F32)
            return c
        lax.fori_loop(0, n_tiles + 2, clear, 0)

        def gather(height):
            slot_id = lax.broadcasted_iota(jnp.int32, (height, tr), 0).astype(F32)
            for tb in range(TOK_BLOCKS):
                ts = slice(tb * tr, (tb + 1) * tr)
                local = post_ref[pl.ds(e, 1), ts] - win[tb].astype(F32)
                sel = jnp.where(local == slot_id, 1.0, 0.0).astype(BF16)
                acc_ref[pl.ds(win[tb], height), :] += _dot(sel, h_ref[ts, :])

        @pl.when(all_small)
        def _():
            gather(SMALL_WINDOW)

        @pl.when(jnp.logical_not(all_small))
        def _():
            gather(tr + SPILL_ROWS)

        @pl.when(e >= 2)
        def _():
            copies(e - 2, "wait")

        def to_stage(r, c):
            rows = pl.ds(pl.multiple_of(r * tr, tr), tr)
            stage_ref[e % 2, rows, :] = acc_ref[rows, :].astype(BF16)
            return c
        lax.fori_loop(0, n_tiles, to_stage, 0)
        copies(e, "start")
        return carry
    lax.fori_loop(0, N_EXPERTS, expert, 0)
    copies(N_EXPERTS - 2, "wait")
    copies(N_EXPERTS - 1, "wait")


def _dispatch(h, post, bounds, seg, seg_off, total_rows):
    tokens = h.shape[0]
    grid_spec = pltpu.PrefetchScalarGridSpec(
        num_scalar_prefetch=3,
        grid=(tokens // SEQ,),
        in_specs=[pl.BlockSpec((SEQ, D_MODEL), lambda b, *_: (b, 0)),
                  pl.BlockSpec((N_EXPERTS, SEQ), lambda b, *_: (0, b))],
        out_specs=pl.BlockSpec(memory_space=pl.ANY),
        scratch_shapes=[pltpu.VMEM((SEQ + 3 * TR_MOE, D_MODEL), F32),
                        pltpu.VMEM((2, STAGE_ROWS, D_MODEL), BF16),
                        pltpu.SemaphoreType.DMA((2,))],
    )
    return pl.pallas_call(
        _dispatch_kernel,
        out_shape=jax.ShapeDtypeStruct((total_rows, D_MODEL), BF16),
        grid_spec=grid_spec,
        compiler_params=_cparams(("arbitrary",)),
        name="expert_dispatch",
    )(bounds, seg, seg_off, h, post)


def _experts_kernel(ge_ref, rows_ref, hs_ref, w1_ref, w3_ref, w2_ref, ys_ref, acc_ref, wb1_ref, wb3_ref, wb2_ref):
    g, f = pl.program_id(0), pl.program_id(1)
    last_f = pl.num_programs(1) - 1
    tr = TR_MOE
    rows = rows_ref[g]
    n_big = lax.shift_right_logical(rows, BIG_TILE.bit_length() - 1)
    n_full = lax.shift_right_logical(rows, tr.bit_length() - 1)
    rest = rows - n_full * tr
    n_tiles = lax.shift_right_logical(rows + (tr - 1), tr.bit_length() - 1)
    tail0 = pl.multiple_of(n_full * tr, tr)

    @pl.when(rows > 0)
    def _():
        wb1_ref[...] = w1_ref[...].astype(BF16)
        wb3_ref[...] = w3_ref[...].astype(BF16)
        wb2_ref[...] = w2_ref[...].astype(BF16)

    @pl.when(f == 0)
    def _():
        def clear(r, carry):
            acc_ref[pl.ds(pl.multiple_of(r * tr, tr), tr), :] = jnp.zeros((tr, D_MODEL), F32)
            return carry
        lax.fori_loop(0, n_tiles, clear, 0)

    def swiglu(hr, r0, m):
        if m > tr:
            acc_ref[pl.ds(r0, m), :] += _swiglu_chunks(hr, wb1_ref, wb3_ref, wb2_ref)
        else:
            z = (_silu(_dot(hr, wb1_ref[...])) * _dot(hr, wb3_ref[...])).astype(BF16)
            acc_ref[pl.ds(r0, m), :] += _dot(z, wb2_ref[...])

    def big_tile(r, carry):
        r0 = pl.multiple_of(r * BIG_TILE, BIG_TILE)
        swiglu(hs_ref[pl.ds(r0, BIG_TILE), :], r0, BIG_TILE)
        return carry
    lax.fori_loop(0, n_big, big_tile, 0)

    done = n_big * BIG_TILE
    m = BIG_TILE // 2
    while m >= tr:
        take = (rows - done) >= m

        @pl.when(take)
        def _(done=done, m=m):
            r0 = pl.multiple_of(done, tr)
            swiglu(hs_ref[pl.ds(r0, m), :], r0, m)
        done = done + jnp.where(take, m, 0)
        m //= 2
    lo = 0
    for m in TAIL_TILES:
        @pl.when((rest > lo) & (rest <= m))
        def _():
            valid = lax.broadcasted_iota(jnp.int32, (m, D_MODEL), 0) < rest
            hr = hs_ref[pl.ds(tail0, m), :]
            swiglu(jnp.where(valid, hr, jnp.zeros_like(hr)), tail0, m)
        lo = m

    @pl.when(f == last_f)
    def _():
        ys_ref[...] = jnp.zeros_like(ys_ref)

        def store(r, carry):
            rr = pl.ds(pl.multiple_of(r * tr, tr), tr)
            ys_ref[rr, :] = acc_ref[rr, :].astype(BF16)
            return carry
        lax.fori_loop(0, n_tiles, store, 0)


def _experts(hs, g_expert, g_rows, n_groups, w1, w3, w2):
    d_ff = w1.shape[2]
    tf = TF_MOE
    n_f = d_ff // tf
    chunk = lambda g, f, gr: jnp.where(gr[g] > 0, f, n_f - 1)
    grid_spec = pltpu.PrefetchScalarGridSpec(
        num_scalar_prefetch=2,
        grid=(n_groups, n_f),
        in_specs=[
            pl.BlockSpec((GROUP_ROWS, D_MODEL), lambda g, f, ge, gr: (g, 0)),
            pl.BlockSpec((None, D_MODEL, tf), lambda g, f, ge, gr: (ge[g], 0, chunk(g, f, gr))),
            pl.BlockSpec((None, D_MODEL, tf), lambda g, f, ge, gr: (ge[g], 0, chunk(g, f, gr))),
            pl.BlockSpec((None, tf, D_MODEL), lambda g, f, ge, gr: (ge[g], chunk(g, f, gr), 0)),
        ],
        out_specs=pl.BlockSpec((GROUP_ROWS, D_MODEL), lambda g, f, ge, gr: (g, 0)),
        scratch_shapes=[pltpu.VMEM((GROUP_ROWS, D_MODEL), F32),
                        pltpu.VMEM((D_MODEL, tf), BF16), pltpu.VMEM((D_MODEL, tf), BF16),
                        pltpu.VMEM((tf, D_MODEL), BF16)],
    )
    return pl.pallas_call(
        _experts_kernel,
        out_shape=jax.ShapeDtypeStruct(hs.shape, BF16),
        grid_spec=grid_spec,
        compiler_params=_cparams(("arbitrary", "arbitrary")),
        name="expert_swiglu",
    )(g_expert, g_rows, hs, w1, w3, w2)


def _combine_kernel(tbl_ref, seg_ref, off_ref, rows_hbm, pos_ref, gates_ref, x_ref, mod_ref, out_ref, buf_ref, sem):
    b = pl.program_id(0)
    n_seq = pl.num_programs(0)
    tr = TR_MOE

    def all_segments(seq, action):
        def body(e, first):
            off = off_ref[seq * N_EXPERTS + e]
            seg = seg_ref[seq * N_EXPERTS + e]

            def make_copy(r0, rows):
                return pltpu.make_async_copy(
                    rows_hbm.at[pl.ds(pl.multiple_of(off + r0, SEG_ALIGN), rows), :],
                    buf_ref.at[seq % 2, pl.ds(pl.multiple_of(first + r0, SEG_ALIGN), rows), :],
                    sem.at[seq % 2, e])
            _segment_copies(seg, make_copy, action)
            return first + seg
        lax.fori_loop(0, N_EXPERTS, body, 0)

    @pl.when(b == 0)
    def _():
        buf_ref[...] = jnp.zeros_like(buf_ref)
        all_segments(b, "start")

    @pl.when(b + 1 < n_seq)
    def _():
        all_segments(b + 1, "start")

    out_ref[...] = x_ref[...]
    lane = lax.broadcasted_iota(jnp.int32, (tr, LANES), 1)
    layer_gate = mod_ref[5:6, :]
    all_segments(b, "wait")

    def expert(e, seg_start):
        bounds, win, all_small = _block_windows(tbl_ref, b, e, SEG_ALIGN)

        def scatter(tb, first, height):
            ts = slice(tb * tr, (tb + 1) * tr)
            pos_col = jnp.sum(jnp.where(lane == e, pos_ref[ts, :], 0.0), axis=-1, keepdims=True)
            gate_col = jnp.sum(jnp.where(lane == e, gates_ref[ts, :], 0.0), axis=-1, keepdims=True)
            slot_id = lax.broadcasted_iota(jnp.int32, (tr, height), 1).astype(F32) + float(first)
            sel = jnp.where(pos_col - win[tb].astype(F32) == slot_id, 1.0, 0.0).astype(BF16)
            rows = buf_ref[b % 2, pl.ds(pl.multiple_of(seg_start + win[tb] + first, SEG_ALIGN), height), :]
            out_ref[ts, :] += layer_gate * (gate_col * _dot(sel, rows))

        @pl.when(all_small)
        def _():
            for tb in range(TOK_BLOCKS):
                scatter(tb, 0, SMALL_WINDOW)

        @pl.when(jnp.logical_not(all_small))
        def _():
            for tb in range(TOK_BLOCKS):
                scatter(tb, 0, tr)
            for tb in range(TOK_BLOCKS):
                @pl.when(bounds[tb + 1] - win[tb] > tr)
                def _():
                    scatter(tb, tr, SPILL_ROWS)
        return seg_start + seg_ref[b * N_EXPERTS + e]
    lax.fori_loop(0, N_EXPERTS, expert, 0)


def _combine(ys, pos, gates, x, mod, bounds, seg, seg_off):
    tokens = x.shape[0]
    grid_spec = pltpu.PrefetchScalarGridSpec(
        num_scalar_prefetch=3,
        grid=(tokens // SEQ,),
        in_specs=[pl.BlockSpec(memory_space=pl.ANY),
                  pl.BlockSpec((SEQ, LANES), lambda b, *_: (b, 0)),
                  pl.BlockSpec((SEQ, LANES), lambda b, *_: (b, 0)),
                  pl.BlockSpec((SEQ, D_MODEL), lambda b, *_: (b, 0)),
                  pl.BlockSpec((None, 6, D_MODEL), lambda b, *_: (b, 0, 0))],
        out_specs=pl.BlockSpec((SEQ, D_MODEL), lambda b, *_: (b, 0)),
        scratch_shapes=[pltpu.VMEM((2, PACKED_ROWS, D_MODEL), BF16),
                        pltpu.SemaphoreType.DMA((2, N_EXPERTS))],
    )
    return pl.pallas_call(
        _combine_kernel,
        out_shape=jax.ShapeDtypeStruct((tokens, D_MODEL), F32),
        grid_spec=grid_spec,
        compiler_params=_cparams(("arbitrary",)),
        name="expert_combine",
    )(bounds, seg, seg_off, ys, pos, gates, x, mod)


def _token_mixer(x, mod, gain, w_in, layer, q_gain, k_gain, ret_gain, bias_masks):
    *qkv, ret_in = _in_proj(x, mod, gain, w_in, layer, q_gain, k_gain)
    return _dilated_attention(qkv, bias_masks), _retention(ret_in, ret_gain)


def _out_proj_moe_ffn(attn, ret, x, mod, w_out, layer, gain, w_router, w1, w3, w2):
    x, h, gates, pos, post, start, cnt = _out_proj_router(attn, ret, x, mod, w_out, layer, gain, w_router)
    bounds, seg, seg_off, g_expert, g_rows, n_groups = _moe_plan(start, cnt)
    hs = _dispatch(h, post, bounds, seg, seg_off, n_groups * GROUP_ROWS)
    ys = _experts(hs, g_expert, g_rows, n_groups, w1, w3, w2)
    return _combine(ys, pos, gates, x, mod, bounds, seg, seg_off)


def kernel(x, c, rel_bias_table, norm_mix, norm_ffn, w_mod, b_mod, w_in, q_gain, k_gain, ret_gain, w_out,
           ffn_w1, ffn_w3, ffn_w2, moe_router, moe_w1, moe_w3, moe_w2):
    batch, seq, d_model = x.shape
    assert (seq, d_model) == (SEQ, D_MODEL)
    depth = w_mod.shape[0]
    mods = _modulation(c, w_mod, b_mod).reshape(depth, batch, 6, D_MODEL)
    bias_masks = _bias_masks(rel_bias_table)
    xt = x.reshape(batch * seq, d_model)
    for layer in range(depth):
        mod = mods[layer]
        attn, ret = _token_mixer(xt, mod, norm_mix[layer], w_in, layer, q_gain[layer], k_gain[layer],
                                 ret_gain[layer], bias_masks)
        i = layer // 2
        if layer % 2 == 0:
            xt = _out_proj_dense_ffn(attn, ret, xt, mod, w_out, layer, norm_ffn[layer],
                                     ffn_w1, ffn_w3, ffn_w2, i)
        else:
            xt = _out_proj_moe_ffn(attn, ret, xt, mod, w_out, layer, norm_ffn[layer],
                                   moe_router[i], moe_w1[i], moe_w3[i], moe_w2[i])
    return xt.reshape(batch, seq, d_model)
```

```python
import functools
import math

import jax
import jax.numpy as jnp
import numpy as np
from jax import lax
from jax.experimental import pallas as pl
from jax.experimental.pallas import tpu as pltpu

D_MODEL = 1024
SEQ = 2048
ATTN_HEADS = 8
ATTN_HEAD_DIM = 64
ATTN_WIDTH = ATTN_HEADS * ATTN_HEAD_DIM
DILATED_PATTERNS = ((128, 1), (512, 4), (2048, 16))
BLOCK = 128
NUM_BUCKETS = 32
MAX_DISTANCE = 2048
RET_HEADS = 4
RET_KEY_DIM = 64
RET_VALUE_DIM = 128
RET_WIDTH = RET_HEADS * RET_VALUE_DIM
RET_QK_WIDTH = RET_HEADS * RET_KEY_DIM
RET_CHUNK = 128
ROPE_BASE = 10000.0
IN_WIDTH = 3 * ATTN_WIDTH + 2 * RET_QK_WIDTH + 2 * RET_WIDTH
RET_IN_WIDTH = IN_WIDTH - 3 * ATTN_WIDTH
N_EXPERTS = 8
EPS = 1e-6
NEG_INF = -1e30

LANES = 128
VMEM_LIMIT = 60 * 1024 * 1024

BF16 = jnp.bfloat16
F32 = jnp.float32

TM_PROJ = 256
TM_FFN = 512
MXU_COLS = 256
TR_MOE = 256
TF_MOE = 512
assert TM_PROJ == TR_MOE
RET_ROWS = 512
SUBLANES = 8


def _cparams(sem):
    return pltpu.CompilerParams(dimension_semantics=sem, vmem_limit_bytes=VMEM_LIMIT)


def _dot(a, b):
    return jnp.dot(a, b, preferred_element_type=F32)


def _dot_nt(a, b):
    return lax.dot_general(a, b, (((1,), (1,)), ((), ())), preferred_element_type=F32)


def _dot_tn(a, b):
    return lax.dot_general(a, b, (((0,), (0,)), ((), ())), preferred_element_type=F32)


def _split_bf16(v):
    hi = v.astype(BF16)
    lo = (v - hi.astype(F32)).astype(BF16)
    return hi, lo


def _silu(v):
    return v * (1.0 / (1.0 + jnp.exp(-v)))


def _modulated_norm(x, gain, scale, shift):
    ms = jnp.mean(x * x, axis=-1, keepdims=True)
    y = x * lax.rsqrt(ms + EPS) * gain
    return y * (1.0 + scale) + shift


def _mod_kernel(c_ref, w_ref, b_ref, o_ref):
    ca = _silu(c_ref[...]).astype(BF16)
    o_ref[...] = _dot(ca, w_ref[...].astype(BF16)) + b_ref[...]


def _modulation(c, w_mod, b_mod):
    depth, _, width = w_mod.shape
    batch = c.shape[0]
    tn = 1536
    return pl.pallas_call(
        _mod_kernel,
        out_shape=jax.ShapeDtypeStruct((depth, batch, width), F32),
        grid=(depth, width // tn),
        in_specs=[
            pl.BlockSpec((batch, D_MODEL), lambda l, n: (0, 0)),
            pl.BlockSpec((None, D_MODEL, tn), lambda l, n: (l, 0, n)),
            pl.BlockSpec((None, 1, tn), lambda l, n: (l, 0, n)),
        ],
        out_specs=pl.BlockSpec((None, batch, tn), lambda l, n: (l, 0, n)),
        compiler_params=_cparams(("arbitrary", "arbitrary")),
        name="adaln_modulation",
    )(c, w_mod, b_mod.reshape(depth, 1, width))


def _bias_kernel(table_ref, bucket_ref, o_ref):
    bucket = bucket_ref[...]
    acc = [jnp.full(bucket.shape, NEG_INF, F32) for _ in range(ATTN_HEADS)]
    for b in range(NUM_BUCKETS):
        hit = bucket == b
        for h in range(ATTN_HEADS):
            acc[h] = jnp.where(hit, table_ref[b, h], acc[h])
    for h in range(ATTN_HEADS):
        o_ref[h // 2, (h % 2) * BLOCK:(h % 2 + 1) * BLOCK, :] = acc[h]


BIAS_FULL = {1: 0, 4: 2}
BIAS_FIRST = {1: 1, 4: 3, 16: 4}
N_BIAS_SETS = 5


def _bias_masks(rel_bias_table):
    i = jnp.arange(BLOCK)[:, None]
    j = jnp.arange(2 * BLOCK)[None, :]
    max_exact = NUM_BUCKETS // 2

    def bucket_of(rel, dilation, w_sub, exists):
        n = jnp.maximum(rel * dilation, 0)
        nf = jnp.maximum(n.astype(F32), float(max_exact))
        large = max_exact + (jnp.log(nf / max_exact) / math.log(MAX_DISTANCE / max_exact)
                             * (NUM_BUCKETS - max_exact)).astype(jnp.int32)
        large = jnp.minimum(large, NUM_BUCKETS - 1)
        bucket = jnp.where(n < max_exact, n, large)
        allowed = (rel >= 0) & (rel <= w_sub) & exists
        return jnp.where(allowed, bucket, -1)

    sets = [None] * N_BIAS_SETS
    for window, dilation in DILATED_PATTERNS:
        w_sub = window // dilation
        if dilation in BIAS_FULL:
            sets[BIAS_FULL[dilation]] = bucket_of(i - j + BLOCK, dilation, w_sub, j >= 0)
        sets[BIAS_FIRST[dilation]] = bucket_of(i - j, dilation, w_sub, j < BLOCK)
    buckets = jnp.stack(sets).astype(jnp.int32)
    return pl.pallas_call(
        _bias_kernel,
        out_shape=jax.ShapeDtypeStruct((N_BIAS_SETS, ATTN_HEADS // 2, 2 * BLOCK, 2 * BLOCK), F32),
        grid=(N_BIAS_SETS,),
        in_specs=[
            pl.BlockSpec(memory_space=pltpu.SMEM),
            pl.BlockSpec((None, BLOCK, 2 * BLOCK), lambda p: (p, 0, 0)),
        ],
        out_specs=pl.BlockSpec((None, ATTN_HEADS // 2, 2 * BLOCK, 2 * BLOCK), lambda p: (p, 0, 0, 0)),
        compiler_params=_cparams(("arbitrary",)),
        name="relative_bias_masks",
    )(rel_bias_table, buckets)


HALF = ATTN_WIDTH // 2


def _in_proj_kernel(x_ref, mod_ref, gain_ref, w_ref, qg_ref, kg_ref, grp_ref,
                    q1_ref, k1_ref, v1_ref, q4_ref, k4_ref, v4_ref, q16_ref, k16_ref, v16_ref, r_ref,
                    perm_ref):
    mod = mod_ref[...]
    h = _modulated_norm(x_ref[...], gain_ref[...], mod[1:2], mod[0:1]).astype(BF16)
    proj = _dot(h, w_ref[...].astype(BF16))
    grp = grp_ref[...]
    tm = proj.shape[0]

    def head_norm(t, gain):
        hi, lo = _split_bf16(t * t)
        ss = _dot(hi, grp) + _dot(lo, grp)
        return t * lax.rsqrt(ss * (1.0 / ATTN_HEAD_DIM) + EPS) * gain

    def emit(t, o1_ref, o4_ref, o16_ref):
        o1_ref[...] = t.astype(BF16)
        for j in range(ATTN_WIDTH // LANES):
            perm_ref[j] = t[:, j * LANES:(j + 1) * LANES]
        for dil, o_ref in ((4, o4_ref), (16, o16_ref)):
            for hf in range(2):
                for r in range(dil):
                    for jj in range(HALF // LANES):
                        c0 = (hf * dil + r) * HALF + jj * LANES
                        o_ref[:, c0:c0 + LANES] = perm_ref[hf * (HALF // LANES) + jj,
                                                           pl.ds(r, tm // dil, stride=dil), :].astype(BF16)

    emit(head_norm(proj[:, :ATTN_WIDTH], qg_ref[...]) * (ATTN_HEAD_DIM ** -0.5), q1_ref, q4_ref, q16_ref)
    emit(head_norm(proj[:, ATTN_WIDTH:2 * ATTN_WIDTH], kg_ref[...]), k1_ref, k4_ref, k16_ref)
    emit(proj[:, 2 * ATTN_WIDTH:3 * ATTN_WIDTH], v1_ref, v4_ref, v16_ref)
    r_ref[...] = proj[:, 3 * ATTN_WIDTH:]


def _in_proj(x, mod, gain, w_in, layer, q_gain, k_gain):
    tokens = x.shape[0]
    tm = TM_PROJ
    per_seq = SEQ // tm
    grp = np.kron(np.eye(ATTN_HEADS), np.ones((ATTN_HEAD_DIM, ATTN_HEAD_DIM))).astype(np.float32)
    row = lambda i: (i, 0)
    const = lambda i: (0, 0)
    layouts = []
    for dil in (1, 4, 16):
        shape = jax.ShapeDtypeStruct((tokens // dil, dil * ATTN_WIDTH), BF16)
        spec = pl.BlockSpec((tm // dil, dil * ATTN_WIDTH), row)
        layouts.append(((shape,) * 3, (spec,) * 3))
    out_shape = sum((s for s, _ in layouts), ()) + (jax.ShapeDtypeStruct((tokens, RET_IN_WIDTH), F32),)
    out_specs = sum((s for _, s in layouts), ()) + (pl.BlockSpec((tm, RET_IN_WIDTH), row),)
    return pl.pallas_call(
        _in_proj_kernel,
        out_shape=out_shape,
        grid=(tokens // tm,),
        in_specs=[
            pl.BlockSpec((tm, D_MODEL), row),
            pl.BlockSpec((None, 6, D_MODEL), lambda i: (i // per_seq, 0, 0)),
            pl.BlockSpec((1, D_MODEL), const),
            pl.BlockSpec((None, D_MODEL, IN_WIDTH), lambda i: (layer, 0, 0), pipeline_mode=pl.Buffered(1)),
            pl.BlockSpec((1, ATTN_WIDTH), const),
            pl.BlockSpec((1, ATTN_WIDTH), const),
            pl.BlockSpec((ATTN_WIDTH, ATTN_WIDTH), const),
        ],
        out_specs=out_specs,
        scratch_shapes=[pltpu.VMEM((ATTN_WIDTH // LANES, tm, LANES), F32)],
        compiler_params=_cparams(("arbitrary",)),
        name="in_projection",
    )(x, mod, gain.reshape(1, D_MODEL), w_in,
      jnp.tile(q_gain, ATTN_HEADS).reshape(1, ATTN_WIDTH),
      jnp.tile(k_gain, ATTN_HEADS).reshape(1, ATTN_WIDTH),
      jnp.asarray(grp, BF16))


PAIRS_PER_HALF = ATTN_HEADS // 4
GROUP = 8


def _pair_scores(qp, kp, vp, bias2, masks, low):
    q2 = jnp.concatenate([qp * masks[0], qp * masks[1]], axis=0)
    s = _dot_nt(q2, kp) + bias2
    m = jnp.max(s, axis=-1, keepdims=True)
    p = jnp.exp(s - m)
    den = jnp.sum(p, axis=-1, keepdims=True)
    pv = _dot(p.astype(BF16), vp)
    pick = lambda t: jnp.where(low, t[:BLOCK], t[BLOCK:])
    return pick(pv), pick(m), pick(den)


def _attn_kernel(q1_ref, k1_ref, v1_ref, q4_ref, k4_ref, v4_ref, q16_ref, k16_ref, v16_ref, bm_ref,
                 o_ref, acc_ref, max_ref, den_ref):
    lane = lax.broadcasted_iota(jnp.int32, (BLOCK, LANES), 1)
    low = lane < ATTN_HEAD_DIM
    masks = (jnp.where(low, 1.0, 0.0).astype(BF16), jnp.where(low, 0.0, 1.0).astype(BF16))

    def block(q_ref, k_ref, v_ref, c0, q0, w0, width, bias_set, rows, first):
        for p in range(PAIRS_PER_HALF):
            cs = slice(c0 + p * LANES, c0 + (p + 1) * LANES)
            acc, m, den = _pair_scores(q_ref[pl.ds(q0, BLOCK), cs], k_ref[pl.ds(w0, width), cs],
                                       v_ref[pl.ds(w0, width), cs], bm_ref[bias_set, p, :, 0:width],
                                       masks, low)
            if not first:
                m_old = max_ref[p, rows, :]
                m_new = jnp.maximum(m_old, m)
                a, b = jnp.exp(m_old - m_new), jnp.exp(m - m_new)
                den = den_ref[p, rows, :] * a + den * b
                acc = acc_ref[p, rows, :] * a + acc * b
                m = m_new
            max_ref[p, rows, :] = m
            den_ref[p, rows, :] = den
            acc_ref[p, rows, :] = acc

    def d1_group(g, carry):
        for u in range(GROUP):
            n = g * GROUP + u
            q0 = pl.multiple_of(n * BLOCK, BLOCK)
            w0 = pl.multiple_of(jnp.maximum(n - 1, 0) * BLOCK, BLOCK)
            bias_set = jnp.where(n == 0, BIAS_FIRST[1], BIAS_FULL[1])
            block(q1_ref, k1_ref, v1_ref, 0, q0, w0, 2 * BLOCK, bias_set, pl.ds(q0, BLOCK), True)
        return carry
    lax.fori_loop(0, SEQ // BLOCK // GROUP, d1_group, 0)

    for r in range(4):
        for n in range(SEQ // 4 // BLOCK):
            block(q4_ref, k4_ref, v4_ref, r * HALF, n * BLOCK, max(n - 1, 0) * BLOCK, 2 * BLOCK,
                  BIAS_FIRST[4] if n == 0 else BIAS_FULL[4],
                  pl.ds(r + 4 * BLOCK * n, BLOCK, stride=4), False)

    for r in range(16):
        block(q16_ref, k16_ref, v16_ref, r * HALF, 0, 0, BLOCK, BIAS_FIRST[16],
              pl.ds(r, BLOCK, stride=16), False)

    for n in range(SEQ // BLOCK):
        rows = slice(n * BLOCK, (n + 1) * BLOCK)
        for p in range(PAIRS_PER_HALF):
            o_ref[rows, p * LANES:(p + 1) * LANES] = (acc_ref[p, rows, :] / den_ref[p, rows, :]).astype(BF16)


def _dilated_attention(qkv, bias_masks):
    tokens = qkv[0].shape[0]
    batch = tokens // SEQ
    specs = []
    for dil in (1, 4, 16):
        specs += [pl.BlockSpec((SEQ // dil, dil * HALF), lambda b, hf: (b, hf))] * 3
    state = pltpu.VMEM((PAIRS_PER_HALF, SEQ, LANES), F32)
    return pl.pallas_call(
        _attn_kernel,
        out_shape=jax.ShapeDtypeStruct((tokens, ATTN_WIDTH), BF16),
        grid=(batch, 2),
        in_specs=specs + [pl.BlockSpec((N_BIAS_SETS, PAIRS_PER_HALF, 2 * BLOCK, 2 * BLOCK),
                                       lambda b, hf: (0, hf, 0, 0))],
        out_specs=pl.BlockSpec((SEQ, HALF), lambda b, hf: (b, hf)),
        scratch_shapes=[state, state, state],
        compiler_params=_cparams(("arbitrary", "arbitrary")),
        name="dilated_attention",
    )(*qkv, bias_masks)


def _retention_kernel(r_ref, cos_ref, sin_ref, dmask_ref, qdec_ref, kdec_ref, cdec_ref, gain_ref,
                      o_ref, state_ref):
    @pl.when(pl.program_id(1) == 0)
    def _():
        state_ref[...] = jnp.zeros_like(state_ref)

    lane = lax.broadcasted_iota(jnp.int32, (RET_CHUNK, LANES), 1)
    low = lane < RET_KEY_DIM
    first_half = (lane % RET_KEY_DIM) < (RET_KEY_DIM // 2)

    def rotate(t, cos, sin):
        partner = jnp.where(first_half, pltpu.roll(t, LANES - RET_KEY_DIM // 2, 1),
                            pltpu.roll(t, RET_KEY_DIM // 2, 1))
        return t * cos + partner * sin

    for c in range(RET_ROWS // RET_CHUNK):
        rows = slice(c * RET_CHUNK, (c + 1) * RET_CHUNK)
        for hp in range(RET_HEADS // 2):
            qs = slice(hp * LANES, (hp + 1) * LANES)
            ks = slice(RET_QK_WIDTH + hp * LANES, RET_QK_WIDTH + (hp + 1) * LANES)
            cos, sin = cos_ref[rows, qs], sin_ref[rows, qs]
            q_pair = rotate(r_ref[rows, qs], cos, sin) * (RET_KEY_DIM ** -0.5)
            k_pair = rotate(r_ref[rows, ks], cos, sin)
            for hh in range(2):
                head = 2 * hp + hh
                vs = slice(2 * RET_QK_WIDTH + head * LANES, 2 * RET_QK_WIDTH + (head + 1) * LANES)
                gs = slice(2 * RET_QK_WIDTH + RET_WIDTH + head * LANES,
                           2 * RET_QK_WIDTH + RET_WIDTH + (head + 1) * LANES)
                keep = low if hh == 0 else jnp.logical_not(low)
                qm = jnp.where(keep, q_pair, 0.0)
                vb = r_ref[rows, vs].astype(BF16)
                state = state_ref[head]
                inner = _dot_nt(qm.astype(BF16), k_pair.astype(BF16)) * dmask_ref[head]
                y = _dot(inner.astype(BF16), vb)
                y = y + _dot((qm * qdec_ref[head]).astype(BF16), state.astype(BF16))
                state_ref[head] = state * cdec_ref[head] + _dot_tn((k_pair * kdec_ref[head]).astype(BF16), vb)
                mu = jnp.mean(y, axis=-1, keepdims=True)
                yc = y - mu
                var = jnp.mean(yc * yc, axis=-1, keepdims=True)
                yn = yc * lax.rsqrt(var + EPS) * gain_ref[:, head * LANES:(head + 1) * LANES]
                o_ref[rows, head * LANES:(head + 1) * LANES] = (_silu(r_ref[rows, gs]) * yn).astype(BF16)


def _retention_tables():
    half = RET_KEY_DIM // 2
    pos = jnp.arange(SEQ, dtype=F32)
    inv = ROPE_BASE ** (-jnp.arange(half, dtype=F32) / half)
    ang = pos[:, None] * inv[None, :]
    cos, sin = jnp.cos(ang), jnp.sin(ang)
    cos_full = jnp.tile(jnp.concatenate([cos, cos], axis=-1), (1, RET_HEADS))
    sin_signed = jnp.tile(jnp.concatenate([-sin, sin], axis=-1), (1, RET_HEADS))
    log_g = jnp.log(1.0 - 2.0 ** (-5.0 - jnp.arange(RET_HEADS, dtype=F32)))
    idx = jnp.arange(RET_CHUNK, dtype=F32)
    diff = idx[:, None] - idx[None, :]
    dmask = jnp.where(diff >= 0, jnp.exp(jnp.maximum(diff, 0.0)[None] * log_g[:, None, None]), 0.0)
    q_decay = jnp.exp((idx + 1.0)[None, :] * log_g[:, None])[..., None]
    k_decay = jnp.exp((RET_CHUNK - 1.0 - idx)[None, :] * log_g[:, None])[..., None]
    chunk_decay = jnp.exp(RET_CHUNK * log_g)[:, None, None]
    full = (RET_HEADS, RET_CHUNK, LANES)
    return (cos_full, sin_signed, dmask, jnp.broadcast_to(q_decay, full),
            jnp.broadcast_to(k_decay, full), jnp.broadcast_to(chunk_decay, full))


def _retention(ret_in, ret_gain):
    tokens = ret_in.shape[0]
    batch = tokens // SEQ
    per_seq = SEQ // RET_ROWS
    cos, sin, dmask, qdec, kdec, cdec = _retention_tables()
    tab = pl.BlockSpec((RET_ROWS, RET_QK_WIDTH), lambda b, j: (j, 0))
    const3 = pl.BlockSpec((RET_HEADS, RET_CHUNK, LANES), lambda b, j: (0, 0, 0))
    return pl.pallas_call(
        _retention_kernel,
        out_shape=jax.ShapeDtypeStruct((tokens, RET_WIDTH), BF16),
        grid=(batch, per_seq),
        in_specs=[
            pl.BlockSpec((RET_ROWS, RET_IN_WIDTH), lambda b, j: (b * per_seq + j, 0)),
            tab, tab, const3, const3, const3, const3,
            pl.BlockSpec((1, RET_WIDTH), lambda b, j: (0, 0)),
        ],
        out_specs=pl.BlockSpec((RET_ROWS, RET_WIDTH), lambda b, j: (b * per_seq + j, 0)),
        scratch_shapes=[pltpu.VMEM((RET_HEADS, LANES, RET_VALUE_DIM), F32)],
        compiler_params=_cparams(("arbitrary", "arbitrary")),
        name="retention",
    )(ret_in, cos, sin, dmask, qdec, kdec, cdec, ret_gain.reshape(1, RET_WIDTH))


def _swiglu_chunks(h, w1_ref, w3_ref, w2_ref):
    d_ff = w1_ref.shape[-1]
    total = None
    for c0 in range(0, d_ff, MXU_COLS):
        c1 = min(c0 + MXU_COLS, d_ff)
        z = (_silu(_dot(h, w1_ref[:, c0:c1].astype(BF16))) * _dot(h, w3_ref[:, c0:c1].astype(BF16))).astype(BF16)
        part = _dot(z, w2_ref[c0:c1, :].astype(BF16))
        total = part if total is None else total + part
    return total


def _out_proj_ffn_kernel(attn_ref, ret_ref, x_ref, mod_ref, wo_ref, gain_ref, w1_ref, w3_ref, w2_ref, out_ref):
    mod = mod_ref[...]
    mix = (_dot(attn_ref[...], wo_ref[:ATTN_WIDTH, :].astype(BF16))
           + _dot(ret_ref[...], wo_ref[ATTN_WIDTH:, :].astype(BF16)))
    x = x_ref[...] + mod[2:3, :] * mix
    h = _modulated_norm(x, gain_ref[...], mod[4:5], mod[3:4]).astype(BF16)
    out_ref[...] = x + mod[5:6, :] * _swiglu_chunks(h, w1_ref, w3_ref, w2_ref)


def _out_proj_dense_ffn(attn, ret, x, mod, w_out, layer, gain, w1, w3, w2, index):
    tokens = x.shape[0]
    d_ff = w1.shape[2]
    tm = TM_FFN
    per_seq = SEQ // tm
    row = lambda i: (i, 0)
    resident = pl.Buffered(1)
    return pl.pallas_call(
        _out_proj_ffn_kernel,
        out_shape=jax.ShapeDtypeStruct((tokens, D_MODEL), F32),
        grid=(tokens // tm,),
        in_specs=[
            pl.BlockSpec((tm, ATTN_WIDTH), row),
            pl.BlockSpec((tm, RET_WIDTH), row),
            pl.BlockSpec((tm, D_MODEL), row),
            pl.BlockSpec((None, 6, D_MODEL), lambda i: (i // per_seq, 0, 0)),
            pl.BlockSpec((None, D_MODEL, D_MODEL), lambda i: (layer, 0, 0), pipeline_mode=resident),
            pl.BlockSpec((1, D_MODEL), lambda i: (0, 0)),
            pl.BlockSpec((None, D_MODEL, d_ff), lambda i: (index, 0, 0), pipeline_mode=resident),
            pl.BlockSpec((None, D_MODEL, d_ff), lambda i: (index, 0, 0), pipeline_mode=resident),
            pl.BlockSpec((None, d_ff, D_MODEL), lambda i: (index, 0, 0), pipeline_mode=resident),
        ],
        out_specs=pl.BlockSpec((tm, D_MODEL), row),
        compiler_params=_cparams(("arbitrary",)),
        name="out_projection_dense_swiglu",
    )(attn, ret, x, mod, w_out, gain.reshape(1, D_MODEL), w1, w3, w2)


def _router_kernel(attn_ref, ret_ref, x_ref, mod_ref, wo_ref, gain_ref, wr_ref,
                   xo_ref, h_ref, gates_ref, pos_ref, post_ref, start_ref, cnt_ref, carry_ref, *, tiles_per_seq):
    i = pl.program_id(0)

    @pl.when(i % tiles_per_seq == 0)
    def _():
        carry_ref[...] = jnp.zeros_like(carry_ref)

    mod = mod_ref[...]
    mix = (_dot(attn_ref[...], wo_ref[:ATTN_WIDTH, :].astype(BF16))
           + _dot(ret_ref[...], wo_ref[ATTN_WIDTH:, :].astype(BF16)))
    x = x_ref[...] + mod[2:3, :] * mix
    xo_ref[...] = x
    h = _modulated_norm(x, gain_ref[...], mod[4:5], mod[3:4]).astype(BF16)
    h_ref[...] = h
    tm = h.shape[0]
    lane = lax.broadcasted_iota(jnp.int32, (tm, LANES), 1).astype(F32)
    logits = jnp.where(lane < N_EXPERTS, _dot(h, wr_ref[...]), -jnp.inf)
    m1 = jnp.max(logits, axis=-1, keepdims=True)
    i1 = jnp.min(jnp.where(logits == m1, lane, float(LANES)), axis=-1, keepdims=True)
    rest = jnp.where(lane == i1, -jnp.inf, logits)
    m2 = jnp.max(rest, axis=-1, keepdims=True)
    i2 = jnp.min(jnp.where(rest == m2, lane, float(LANES)), axis=-1, keepdims=True)
    e2 = jnp.exp(m2 - m1)
    g1 = 1.0 / (1.0 + e2)
    g2 = e2 / (1.0 + e2)
    gates_ref[...] = jnp.where(lane == i1, g1, 0.0) + jnp.where(lane == i2, g2, 0.0)
    chosen = (lane == i1) | (lane == i2)
    onehot = jnp.where(chosen, 1.0, 0.0)
    r = lax.broadcasted_iota(jnp.int32, (tm, tm), 0)
    c = lax.broadcasted_iota(jnp.int32, (tm, tm), 1)
    tril = jnp.where(c <= r, 1.0, 0.0).astype(BF16)
    incl = _dot(tril, onehot.astype(BF16))
    carry = carry_ref[0:1, :]
    start_ref[...] = carry_ref[...]
    pos = jnp.where(chosen, incl - 1.0 + carry, -1.0)
    pos_ref[...] = pos
    post_ref[...] = pos.T[:N_EXPERTS, :]
    total = carry + incl[tm - 1:tm, :]
    carry_ref[...] = jnp.broadcast_to(total, carry_ref.shape)
    cnt_ref[...] = jnp.broadcast_to(total, cnt_ref.shape)


def _out_proj_router(attn, ret, x, mod, w_out, layer, gain, w_router):
    tokens = x.shape[0]
    batch = tokens // SEQ
    tm = TM_PROJ
    per_seq = SEQ // tm
    wr = jnp.zeros((D_MODEL, LANES), BF16).at[:, :N_EXPERTS].set(w_router.astype(BF16))
    row = lambda i: (i, 0)
    return pl.pallas_call(
        functools.partial(_router_kernel, tiles_per_seq=per_seq),
        out_shape=(
            jax.ShapeDtypeStruct((tokens, D_MODEL), F32),
            jax.ShapeDtypeStruct((tokens, D_MODEL), BF16),
            jax.ShapeDtypeStruct((tokens, LANES), F32),
            jax.ShapeDtypeStruct((tokens, LANES), F32),
            jax.ShapeDtypeStruct((N_EXPERTS, tokens), F32),
            jax.ShapeDtypeStruct((tokens // tm, 8, LANES), F32),
            jax.ShapeDtypeStruct((batch, 8, LANES), F32),
        ),
        grid=(tokens // tm,),
        in_specs=[
            pl.BlockSpec((tm, ATTN_WIDTH), row),
            pl.BlockSpec((tm, RET_WIDTH), row),
            pl.BlockSpec((tm, D_MODEL), row),
            pl.BlockSpec((None, 6, D_MODEL), lambda i: (i // per_seq, 0, 0)),
            pl.BlockSpec((None, D_MODEL, D_MODEL), lambda i: (layer, 0, 0)),
            pl.BlockSpec((1, D_MODEL), lambda i: (0, 0)),
            pl.BlockSpec((D_MODEL, LANES), lambda i: (0, 0)),
        ],
        out_specs=(
            pl.BlockSpec((tm, D_MODEL), row),
            pl.BlockSpec((tm, D_MODEL), row),
            pl.BlockSpec((tm, LANES), row),
            pl.BlockSpec((tm, LANES), row),
            pl.BlockSpec((N_EXPERTS, tm), lambda i: (0, i)),
            pl.BlockSpec((None, 8, LANES), lambda i: (i, 0, 0)),
            pl.BlockSpec((None, 8, LANES), lambda i: (i // per_seq, 0, 0)),
        ),
        scratch_shapes=[pltpu.VMEM((8, LANES), F32)],
        compiler_params=_cparams(("arbitrary",)),
        name="out_projection_router",
    )(attn, ret, x, mod, w_out, gain.reshape(1, D_MODEL), wr)


TOK_BLOCKS = SEQ // TR_MOE
TAIL_TILES = (64, 128, TR_MOE)
SPILL_ROWS = 16
SMALL_WINDOW = 128
BIG_TILE = 1024
SEG_ALIGN = 16
GROUP_ROWS = 3072
PACKED_ROWS = 2 * SEQ + N_EXPERTS * SEG_ALIGN + 2 * TR_MOE
BIG_CHUNK = 128
STAGE_ROWS = SEQ + TR_MOE
N_BOUNDS = TOK_BLOCKS + 1


def _moe_plan(start, cnt):
    batch = cnt.shape[0]
    counts = cnt[:, 0, :N_EXPERTS].astype(jnp.int32)
    bounds = jnp.concatenate([start[:, 0, :N_EXPERTS].reshape(batch, TOK_BLOCKS, N_EXPERTS).astype(jnp.int32),
                              counts[:, None, :]], axis=1)
    seg = (counts + (SEG_ALIGN - 1)) // SEG_ALIGN * SEG_ALIGN
    rows_e = jnp.sum(seg, axis=0)
    groups_e = (rows_e + (GROUP_ROWS - 1)) // GROUP_ROWS
    first_group = jnp.cumsum(groups_e) - groups_e
    seg_off = first_group[None, :] * GROUP_ROWS + jnp.cumsum(seg, axis=0) - seg
    max_rows = batch * (2 * SEQ + N_EXPERTS * (SEG_ALIGN - 1))
    n_groups = max_rows // GROUP_ROWS + N_EXPERTS
    g = jnp.arange(n_groups)
    g_expert = jnp.minimum(jnp.sum(g[:, None] >= jnp.cumsum(groups_e)[None, :], axis=1), N_EXPERTS - 1)
    g_rows = jnp.clip(rows_e[g_expert] - (g - first_group[g_expert]) * GROUP_ROWS, 0, GROUP_ROWS)
    g_rows = jnp.where(g < jnp.sum(groups_e), g_rows, 0)
    return (bounds.reshape(-1), seg.reshape(-1), seg_off.reshape(-1),
            g_expert.astype(jnp.int32), g_rows.astype(jnp.int32), n_groups)


def _segment_copies(seg, make_copy, action):
    n_big = lax.shift_right_logical(seg, BIG_CHUNK.bit_length() - 1)

    def big(i, carry):
        getattr(make_copy(pl.multiple_of(i * BIG_CHUNK, BIG_CHUNK), BIG_CHUNK), action)()
        return carry
    lax.fori_loop(0, n_big, big, 0)
    rest0 = n_big * BIG_CHUNK
    n_small = lax.shift_right_logical(seg - rest0, SEG_ALIGN.bit_length() - 1)

    def small(i, carry):
        getattr(make_copy(pl.multiple_of(rest0 + i * SEG_ALIGN, SEG_ALIGN), SEG_ALIGN), action)()
        return carry
    lax.fori_loop(0, n_small, small, 0)


def _block_windows(tbl_ref, b, e, align):
    base = b * N_BOUNDS * N_EXPERTS + e
    bounds = [tbl_ref[base + tb * N_EXPERTS] for tb in range(N_BOUNDS)]
    shift = align.bit_length() - 1
    win = [pl.multiple_of(lax.shift_left(lax.shift_right_logical(s, shift), shift), align) for s in bounds[:-1]]
    all_small = functools.reduce(jnp.logical_and,
                                 [bounds[tb + 1] - win[tb] <= SMALL_WINDOW for tb in range(TOK_BLOCKS)])
    return bounds, win, all_small


def _dispatch_kernel(tbl_ref, seg_ref, off_ref, h_ref, post_ref, rows_hbm, acc_ref, stage_ref, sem):
    b = pl.program_id(0)
    tr = TR_MOE

    def copies(e, action):
        slot = e % 2
        off = off_ref[b * N_EXPERTS + e]

        def make_copy(r0, rows):
            return pltpu.make_async_copy(stage_ref.at[slot, pl.ds(r0, rows), :],
                                         rows_hbm.at[pl.ds(pl.multiple_of(off + r0, SEG_ALIGN), rows), :],
                                         sem.at[slot])
        _segment_copies(seg_ref[b * N_EXPERTS + e], make_copy, action)

    def expert(e, carry):
        bounds, win, all_small = _block_windows(tbl_ref, b, e, SUBLANES)
        n_tiles = lax.shift_right_logical(bounds[-1] + (tr - 1), tr.bit_length() - 1)

        def clear(r, c):
            acc_ref[pl.ds(pl.multiple_of(r * tr, tr), tr), :] = jnp.zeros((tr, D_MODEL), F32)
            return c
        lax.fori_loop(0, n_tiles + 2, clear, 0)

        def gather(height):
            slot_id = lax.broadcasted_iota(jnp.int32, (height, tr), 0).astype(F32)
            for tb in range(TOK_BLOCKS):
                ts = slice(tb * tr, (tb + 1) * tr)
                local = post_ref[pl.ds(e, 1), ts] - win[tb].astype(F32)
                sel = jnp.where(local == slot_id, 1.0, 0.0).astype(BF16)
                acc_ref[pl.ds(win[tb], height), :] += _dot(sel, h_ref[ts, :])

        @pl.when(all_small)
        def _():
            gather(SMALL_WINDOW)

        @pl.when(jnp.logical_not(all_small))
        def _():
            gather(tr + SPILL_ROWS)

        @pl.when(e >= 2)
        def _():
            copies(e - 2, "wait")

        def to_stage(r, c):
            rows = pl.ds(pl.multiple_of(r * tr, tr), tr)
            stage_ref[e % 2, rows, :] = acc_ref[rows, :].astype(BF16)
            return c
        lax.fori_loop(0, n_tiles, to_stage, 0)
        copies(e, "start")
        return carry
    lax.fori_loop(0, N_EXPERTS, expert, 0)
    copies(N_EXPERTS - 2, "wait")
    copies(N_EXPERTS - 1, "wait")


def _dispatch(h, post, bounds, seg, seg_off, total_rows):
    tokens = h.shape[0]
    grid_spec = pltpu.PrefetchScalarGridSpec(
        num_scalar_prefetch=3,
        grid=(tokens // SEQ,),
        in_specs=[pl.BlockSpec((SEQ, D_MODEL), lambda b, *_: (b, 0)),
                  pl.BlockSpec((N_EXPERTS, SEQ), lambda b, *_: (0, b))],
        out_specs=pl.BlockSpec(memory_space=pl.ANY),
        scratch_shapes=[pltpu.VMEM((SEQ + 3 * TR_MOE, D_MODEL), F32),
                        pltpu.VMEM((2, STAGE_ROWS, D_MODEL), BF16),
                        pltpu.SemaphoreType.DMA((2,))],
    )
    return pl.pallas_call(
        _dispatch_kernel,
        out_shape=jax.ShapeDtypeStruct((total_rows, D_MODEL), BF16),
        grid_spec=grid_spec,
        compiler_params=_cparams(("arbitrary",)),
        name="expert_dispatch",
    )(bounds, seg, seg_off, h, post)


def _experts_kernel(ge_ref, rows_ref, hs_ref, w1_ref, w3_ref, w2_ref, ys_ref, acc_ref, wb1_ref, wb3_ref, wb2_ref):
    g, f = pl.program_id(0), pl.program_id(1)
    last_f = pl.num_programs(1) - 1
    tr = TR_MOE
    rows = rows_ref[g]
    n_big = lax.shift_right_logical(rows, BIG_TILE.bit_length() - 1)
    n_full = lax.shift_right_logical(rows, tr.bit_length() - 1)
    rest = rows - n_full * tr
    n_tiles = lax.shift_right_logical(rows + (tr - 1), tr.bit_length() - 1)
    tail0 = pl.multiple_of(n_full * tr, tr)

    @pl.when(rows > 0)
    def _():
        wb1_ref[...] = w1_ref[...].astype(BF16)
        wb3_ref[...] = w3_ref[...].astype(BF16)
        wb2_ref[...] = w2_ref[...].astype(BF16)

    @pl.when(f == 0)
    def _():
        def clear(r, carry):
            acc_ref[pl.ds(pl.multiple_of(r * tr, tr), tr), :] = jnp.zeros((tr, D_MODEL), F32)
            return carry
        lax.fori_loop(0, n_tiles, clear, 0)

    def swiglu(hr, r0, m):
        if m > tr:
            acc_ref[pl.ds(r0, m), :] += _swiglu_chunks(hr, wb1_ref, wb3_ref, wb2_ref)
        else:
            z = (_silu(_dot(hr, wb1_ref[...])) * _dot(hr, wb3_ref[...])).astype(BF16)
            acc_ref[pl.ds(r0, m), :] += _dot(z, wb2_ref[...])

    def big_tile(r, carry):
        r0 = pl.multiple_of(r * BIG_TILE, BIG_TILE)
        swiglu(hs_ref[pl.ds(r0, BIG_TILE), :], r0, BIG_TILE)
        return carry
    lax.fori_loop(0, n_big, big_tile, 0)

    done = n_big * BIG_TILE
    m = BIG_TILE // 2
    while m >= tr:
        take = (rows - done) >= m

        @pl.when(take)
        def _(done=done, m=m):
            r0 = pl.multiple_of(done, tr)
            swiglu(hs_ref[pl.ds(r0, m), :], r0, m)
        done = done + jnp.where(take, m, 0)
        m //= 2
    lo = 0
    for m in TAIL_TILES:
        @pl.when((rest > lo) & (rest <= m))
        def _():
            valid = lax.broadcasted_iota(jnp.int32, (m, D_MODEL), 0) < rest
            hr = hs_ref[pl.ds(tail0, m), :]
            swiglu(jnp.where(valid, hr, jnp.zeros_like(hr)), tail0, m)
        lo = m

    @pl.when(f == last_f)
    def _():
        ys_ref[...] = jnp.zeros_like(ys_ref)

        def store(r, carry):
            rr = pl.ds(pl.multiple_of(r * tr, tr), tr)
            ys_ref[rr, :] = acc_ref[rr, :].astype(BF16)
            return carry
        lax.fori_loop(0, n_tiles, store, 0)


def _experts(hs, g_expert, g_rows, n_groups, w1, w3, w2):
    d_ff = w1.shape[2]
    tf = TF_MOE
    n_f = d_ff // tf
    chunk = lambda g, f, gr: jnp.where(gr[g] > 0, f, n_f - 1)
    grid_spec = pltpu.PrefetchScalarGridSpec(
        num_scalar_prefetch=2,
        grid=(n_groups, n_f),
        in_specs=[
            pl.BlockSpec((GROUP_ROWS, D_MODEL), lambda g, f, ge, gr: (g, 0)),
            pl.BlockSpec((None, D_MODEL, tf), lambda g, f, ge, gr: (ge[g], 0, chunk(g, f, gr))),
            pl.BlockSpec((None, D_MODEL, tf), lambda g, f, ge, gr: (ge[g], 0, chunk(g, f, gr))),
            pl.BlockSpec((None, tf, D_MODEL), lambda g, f, ge, gr: (ge[g], chunk(g, f, gr), 0)),
        ],
        out_specs=pl.BlockSpec((GROUP_ROWS, D_MODEL), lambda g, f, ge, gr: (g, 0)),
        scratch_shapes=[pltpu.VMEM((GROUP_ROWS, D_MODEL), F32),
                        pltpu.VMEM((D_MODEL, tf), BF16), pltpu.VMEM((D_MODEL, tf), BF16),
                        pltpu.VMEM((tf, D_MODEL), BF16)],
    )
    return pl.pallas_call(
        _experts_kernel,
        out_shape=jax.ShapeDtypeStruct(hs.shape, BF16),
        grid_spec=grid_spec,
        compiler_params=_cparams(("arbitrary", "arbitrary")),
        name="expert_swiglu",
    )(g_expert, g_rows, hs, w1, w3, w2)


def _combine_kernel(tbl_ref, seg_ref, off_ref, rows_hbm, pos_ref, gates_ref, x_ref, mod_ref, out_ref, buf_ref, sem):
    b = pl.program_id(0)
    n_seq = pl.num_programs(0)
    tr = TR_MOE

    def all_segments(seq, action):
        def body(e, first):
            off = off_ref[seq * N_EXPERTS + e]
            seg = seg_ref[seq * N_EXPERTS + e]

            def make_copy(r0, rows):
                return pltpu.make_async_copy(
                    rows_hbm.at[pl.ds(pl.multiple_of(off + r0, SEG_ALIGN), rows), :],
                    buf_ref.at[seq % 2, pl.ds(pl.multiple_of(first + r0, SEG_ALIGN), rows), :],
                    sem.at[seq % 2, e])
            _segment_copies(seg, make_copy, action)
            return first + seg
        lax.fori_loop(0, N_EXPERTS, body, 0)

    @pl.when(b == 0)
    def _():
        buf_ref[...] = jnp.zeros_like(buf_ref)
        all_segments(b, "start")

    @pl.when(b + 1 < n_seq)
    def _():
        all_segments(b + 1, "start")

    out_ref[...] = x_ref[...]
    lane = lax.broadcasted_iota(jnp.int32, (tr, LANES), 1)
    layer_gate = mod_ref[5:6, :]
    all_segments(b, "wait")

    def expert(e, seg_start):
        bounds, win, all_small = _block_windows(tbl_ref, b, e, SEG_ALIGN)

        def scatter(tb, first, height):
            ts = slice(tb * tr, (tb + 1) * tr)
            pos_col = jnp.sum(jnp.where(lane == e, pos_ref[ts, :], 0.0), axis=-1, keepdims=True)
            gate_col = jnp.sum(jnp.where(lane == e, gates_ref[ts, :], 0.0), axis=-1, keepdims=True)
            slot_id = lax.broadcasted_iota(jnp.int32, (tr, height), 1).astype(F32) + float(first)
            sel = jnp.where(pos_col - win[tb].astype(F32) == slot_id, 1.0, 0.0).astype(BF16)
            rows = buf_ref[b % 2, pl.ds(pl.multiple_of(seg_start + win[tb] + first, SEG_ALIGN), height), :]
            out_ref[ts, :] += layer_gate * (gate_col * _dot(sel, rows))

        @pl.when(all_small)
        def _():
            for tb in range(TOK_BLOCKS):
                scatter(tb, 0, SMALL_WINDOW)

        @pl.when(jnp.logical_not(all_small))
        def _():
            for tb in range(TOK_BLOCKS):
                scatter(tb, 0, tr)
            for tb in range(TOK_BLOCKS):
                @pl.when(bounds[tb + 1] - win[tb] > tr)
                def _():
                    scatter(tb, tr, SPILL_ROWS)
        return seg_start + seg_ref[b * N_EXPERTS + e]
    lax.fori_loop(0, N_EXPERTS, expert, 0)


def _combine(ys, pos, gates, x, mod, bounds, seg, seg_off):
    tokens = x.shape[0]
    grid_spec = pltpu.PrefetchScalarGridSpec(
        num_scalar_prefetch=3,
        grid=(tokens // SEQ,),
        in_specs=[pl.BlockSpec(memory_space=pl.ANY),
                  pl.BlockSpec((SEQ, LANES), lambda b, *_: (b, 0)),
                  pl.BlockSpec((SEQ, LANES), lambda b, *_: (b, 0)),
                  pl.BlockSpec((SEQ, D_MODEL), lambda b, *_: (b, 0)),
                  pl.BlockSpec((None, 6, D_MODEL), lambda b, *_: (b, 0, 0))],
        out_specs=pl.BlockSpec((SEQ, D_MODEL), lambda b, *_: (b, 0)),
        scratch_shapes=[pltpu.VMEM((2, PACKED_ROWS, D_MODEL), BF16),
                        pltpu.SemaphoreType.DMA((2, N_EXPERTS))],
    )
    return pl.pallas_call(
        _combine_kernel,
        out_shape=jax.ShapeDtypeStruct((tokens, D_MODEL), F32),
        grid_spec=grid_spec,
        compiler_params=_cparams(("arbitrary",)),
        name="expert_combine",
    )(bounds, seg, seg_off, ys, pos, gates, x, mod)


def _token_mixer(x, mod, gain, w_in, layer, q_gain, k_gain, ret_gain, bias_masks):
    *qkv, ret_in = _in_proj(x, mod, gain, w_in, layer, q_gain, k_gain)
    return _dilated_attention(qkv, bias_masks), _retention(ret_in, ret_gain)


def _out_proj_moe_ffn(attn, ret, x, mod, w_out, layer, gain, w_router, w1, w3, w2):
    x, h, gates, pos, post, start, cnt = _out_proj_router(attn, ret, x, mod, w_out, layer, gain, w_router)
    bounds, seg, seg_off, g_expert, g_rows, n_groups = _moe_plan(start, cnt)
    hs = _dispatch(h, post, bounds, seg, seg_off, n_groups * GROUP_ROWS)
    ys = _experts(hs, g_expert, g_rows, n_groups, w1, w3, w2)
    return _combine(ys, pos, gates, x, mod, bounds, seg, seg_off)


def kernel(x, c, rel_bias_table, norm_mix, norm_ffn, w_mod, b_mod, w_in, q_gain, k_gain, ret_gain, w_out,
           ffn_w1, ffn_w3, ffn_w2, moe_router, moe_w1, moe_w3, moe_w2):
    batch, seq, d_model = x.shape
    assert (seq, d_model) == (SEQ, D_MODEL)
    depth = w_mod.shape[0]
    mods = _modulation(c, w_mod, b_mod).reshape(depth, batch, 6, D_MODEL)
    bias_masks = _bias_masks(rel_bias_table)
    xt = x.reshape(batch * seq, d_model)
    for layer in range(depth):
        mod = mods[layer]
        attn, ret = _token_mixer(xt, mod, norm_mix[layer], w_in, layer, q_gain[layer], k_gain[layer],
                                 ret_gain[layer], bias_masks)
        i = layer // 2
        if layer % 2 == 0:
            xt = _out_proj_dense_ffn(attn, ret, xt, mod, w_out, layer, norm_ffn[layer],
                                     ffn_w1, ffn_w3, ffn_w2, i)
        else:
            xt = _out_proj_moe_ffn(attn, ret, xt, mod, w_out, layer, norm_ffn[layer],
                                   moe_router[i], moe_w1[i], moe_w3[i], moe_w2[i])
    return xt.reshape(batch, seq, d_model)
```

```python
import functools
import math

import jax
import jax.numpy as jnp
import numpy as np
from jax import lax
from jax.experimental import pallas as pl
from jax.experimental.pallas import tpu as pltpu

D_MODEL = 1024
SEQ = 2048
ATTN_HEADS = 8
ATTN_HEAD_DIM = 64
ATTN_WIDTH = ATTN_HEADS * ATTN_HEAD_DIM
DILATED_PATTERNS = ((128, 1), (512, 4), (2048, 16))
BLOCK = 128
NUM_BUCKETS = 32
MAX_DISTANCE = 2048
RET_HEADS = 4
RET_KEY_DIM = 64
RET_VALUE_DIM = 128
RET_WIDTH = RET_HEADS * RET_VALUE_DIM
RET_QK_WIDTH = RET_HEADS * RET_KEY_DIM
RET_CHUNK = 128
ROPE_BASE = 10000.0
IN_WIDTH = 3 * ATTN_WIDTH + 2 * RET_QK_WIDTH + 2 * RET_WIDTH
RET_IN_WIDTH = IN_WIDTH - 3 * ATTN_WIDTH
N_EXPERTS = 8
EPS = 1e-6
NEG_INF = -1e30

LANES = 128
VMEM_LIMIT = 60 * 1024 * 1024

BF16 = jnp.bfloat16
F32 = jnp.float32

TM_PROJ = 256
TM_FFN = 512
MXU_COLS = 256
TR_MOE = 256
TF_MOE = 512
assert TM_PROJ == TR_MOE
RET_ROWS = 512
SUBLANES = 8


def _cparams(sem):
    return pltpu.CompilerParams(dimension_semantics=sem, vmem_limit_bytes=VMEM_LIMIT)


def _dot(a, b):
    return jnp.dot(a, b, preferred_element_type=F32)


def _dot_nt(a, b):
    return lax.dot_general(a, b, (((1,), (1,)), ((), ())), preferred_element_type=F32)


def _dot_tn(a, b):
    return lax.dot_general(a, b, (((0,), (0,)), ((), ())), preferred_element_type=F32)


def _split_bf16(v):
    hi = v.astype(BF16)
    lo = (v - hi.astype(F32)).astype(BF16)
    return hi, lo


def _silu(v):
    return v * (1.0 / (1.0 + jnp.exp(-v)))


def _modulated_norm(x, gain, scale, shift):
    ms = jnp.mean(x * x, axis=-1, keepdims=True)
    y = x * lax.rsqrt(ms + EPS) * gain
    return y * (1.0 + scale) + shift


def _mod_kernel(c_ref, w_ref, b_ref, o_ref):
    ca = _silu(c_ref[...]).astype(BF16)
    o_ref[...] = _dot(ca, w_ref[...].astype(BF16)) + b_ref[...]


def _modulation(c, w_mod, b_mod):
    depth, _, width = w_mod.shape
    batch = c.shape[0]
    tn = 1536
    return pl.pallas_call(
        _mod_kernel,
        out_shape=jax.ShapeDtypeStruct((depth, batch, width), F32),
        grid=(depth, width // tn),
        in_specs=[
            pl.BlockSpec((batch, D_MODEL), lambda l, n: (0, 0)),
            pl.BlockSpec((None, D_MODEL, tn), lambda l, n: (l, 0, n)),
            pl.BlockSpec((None, 1, tn), lambda l, n: (l, 0, n)),
        ],
        out_specs=pl.BlockSpec((None, batch, tn), lambda l, n: (l, 0, n)),
        compiler_params=_cparams(("arbitrary", "arbitrary")),
        name="adaln_modulation",
    )(c, w_mod, b_mod.reshape(depth, 1, width))


def _bias_kernel(table_ref, bucket_ref, o_ref):
    bucket = bucket_ref[...]
    acc = [jnp.full(bucket.shape, NEG_INF, F32) for _ in range(ATTN_HEADS)]
    for b in range(NUM_BUCKETS):
        hit = bucket == b
        for h in range(ATTN_HEADS):
            acc[h] = jnp.where(hit, table_ref[b, h], acc[h])
    for h in range(ATTN_HEADS):
        o_ref[h // 2, (h % 2) * BLOCK:(h % 2 + 1) * BLOCK, :] = acc[h]


BIAS_FULL = {1: 0, 4: 2}
BIAS_FIRST = {1: 1, 4: 3, 16: 4}
N_BIAS_SETS = 5


def _bias_masks(rel_bias_table):
    i = jnp.arange(BLOCK)[:, None]
    j = jnp.arange(2 * BLOCK)[None, :]
    max_exact = NUM_BUCKETS // 2

    def bucket_of(rel, dilation, w_sub, exists):
        n = jnp.maximum(rel * dilation, 0)
        nf = jnp.maximum(n.astype(F32), float(max_exact))
        large = max_exact + (jnp.log(nf / max_exact) / math.log(MAX_DISTANCE / max_exact)
                             * (NUM_BUCKETS - max_exact)).astype(jnp.int32)
        large = jnp.minimum(large, NUM_BUCKETS - 1)
        bucket = jnp.where(n < max_exact, n, large)
        allowed = (rel >= 0) & (rel <= w_sub) & exists
        return jnp.where(allowed, bucket, -1)

    sets = [None] * N_BIAS_SETS
    for window, dilation in DILATED_PATTERNS:
        w_sub = window // dilation
        if dilation in BIAS_FULL:
            sets[BIAS_FULL[dilation]] = bucket_of(i - j + BLOCK, dilation, w_sub, j >= 0)
        sets[BIAS_FIRST[dilation]] = bucket_of(i - j, dilation, w_sub, j < BLOCK)
    buckets = jnp.stack(sets).astype(jnp.int32)
    return pl.pallas_call(
        _bias_kernel,
        out_shape=jax.ShapeDtypeStruct((N_BIAS_SETS, ATTN_HEADS // 2, 2 * BLOCK, 2 * BLOCK), F32),
        grid=(N_BIAS_SETS,),
        in_specs=[
            pl.BlockSpec(memory_space=pltpu.SMEM),
            pl.BlockSpec((None, BLOCK, 2 * BLOCK), lambda p: (p, 0, 0)),
        ],
        out_specs=pl.BlockSpec((None, ATTN_HEADS // 2, 2 * BLOCK, 2 * BLOCK), lambda p: (p, 0, 0, 0)),
        compiler_params=_cparams(("arbitrary",)),
        name="relative_bias_masks",
    )(rel_bias_table, buckets)


HALF = ATTN_WIDTH // 2


def _in_proj_kernel(x_ref, mod_ref, gain_ref, w_ref, qg_ref, kg_ref, grp_ref,
                    q1_ref, k1_ref, v1_ref, q4_ref, k4_ref, v4_ref, q16_ref, k16_ref, v16_ref, r_ref,
                    perm_ref):
    mod = mod_ref[...]
    h = _modulated_norm(x_ref[...], gain_ref[...], mod[1:2], mod[0:1]).astype(BF16)
    proj = _dot(h, w_ref[...].astype(BF16))
    grp = grp_ref[...]
    tm = proj.shape[0]

    def head_norm(t, gain):
        hi, lo = _split_bf16(t * t)
        ss = _dot(hi, grp) + _dot(lo, grp)
        return t * lax.rsqrt(ss * (1.0 / ATTN_HEAD_DIM) + EPS) * gain

    def emit(t, o1_ref, o4_ref, o16_ref):
        o1_ref[...] = t.astype(BF16)
        for j in range(ATTN_WIDTH // LANES):
            perm_ref[j] = t[:, j * LANES:(j + 1) * LANES]
        for dil, o_ref in ((4, o4_ref), (16, o16_ref)):
            for hf in range(2):
                for r in range(dil):
                    for jj in range(HALF // LANES):
                        c0 = (hf * dil + r) * HALF + jj * LANES
                        o_ref[:, c0:c0 + LANES] = perm_ref[hf * (HALF // LANES) + jj,
                                                           pl.ds(r, tm // dil, stride=dil), :].astype(BF16)

    emit(head_norm(proj[:, :ATTN_WIDTH], qg_ref[...]) * (ATTN_HEAD_DIM ** -0.5), q1_ref, q4_ref, q16_ref)
    emit(head_norm(proj[:, ATTN_WIDTH:2 * ATTN_WIDTH], kg_ref[...]), k1_ref, k4_ref, k16_ref)
    emit(proj[:, 2 * ATTN_WIDTH:3 * ATTN_WIDTH], v1_ref, v4_ref, v16_ref)
    r_ref[...] = proj[:, 3 * ATTN_WIDTH:]


def _in_proj(x, mod, gain, w_in, layer, q_gain, k_gain):
    tokens = x.shape[0]
    tm = TM_PROJ
    per_seq = SEQ // tm
    grp = np.kron(np.eye(ATTN_HEADS), np.ones((ATTN_HEAD_DIM, ATTN_HEAD_DIM))).astype(np.float32)
    row = lambda i: (i, 0)
    const = lambda i: (0, 0)
    layouts = []
    for dil in (1, 4, 16):
        shape = jax.ShapeDtypeStruct((tokens // dil, dil * ATTN_WIDTH), BF16)
        spec = pl.BlockSpec((tm // dil, dil * ATTN_WIDTH), row)
        layouts.append(((shape,) * 3, (spec,) * 3))
    out_shape = sum((s for s, _ in layouts), ()) + (jax.ShapeDtypeStruct((tokens, RET_IN_WIDTH), F32),)
    out_specs = sum((s for _, s in layouts), ()) + (pl.BlockSpec((tm, RET_IN_WIDTH), row),)
    return pl.pallas_call(
        _in_proj_kernel,
        out_shape=out_shape,
        grid=(tokens // tm,),
        in_specs=[
            pl.BlockSpec((tm, D_MODEL), row),
            pl.BlockSpec((None, 6, D_MODEL), lambda i: (i // per_seq, 0, 0)),
            pl.BlockSpec((1, D_MODEL), const),
            pl.BlockSpec((None, D_MODEL, IN_WIDTH), lambda i: (layer, 0, 0), pipeline_mode=pl.Buffered(1)),
            pl.BlockSpec((1, ATTN_WIDTH), const),
            pl.BlockSpec((1, ATTN_WIDTH), const),
            pl.BlockSpec((ATTN_WIDTH, ATTN_WIDTH), const),
        ],
        out_specs=out_specs,
        scratch_shapes=[pltpu.VMEM((ATTN_WIDTH // LANES, tm, LANES), F32)],
        compiler_params=_cparams(("arbitrary",)),
        name="in_projection",
    )(x, mod, gain.reshape(1, D_MODEL), w_in,
      jnp.tile(q_gain, ATTN_HEADS).reshape(1, ATTN_WIDTH),
      jnp.tile(k_gain, ATTN_HEADS).reshape(1, ATTN_WIDTH),
      jnp.asarray(grp, BF16))


PAIRS_PER_HALF = ATTN_HEADS // 4
GROUP = 16


def _pair_scores(qp, kp, vp, bias2, masks, low):
    q2 = jnp.concatenate([qp * masks[0], qp * masks[1]], axis=0)
    s = _dot_nt(q2, kp) + bias2
    m = jnp.max(s, axis=-1, keepdims=True)
    p = jnp.exp(s - m)
    den = jnp.sum(p, axis=-1, keepdims=True)
    pv = _dot(p.astype(BF16), vp)
    pick = lambda t: jnp.where(low, t[:BLOCK], t[BLOCK:])
    return pick(pv), pick(m), pick(den)


def _attn_kernel(q1_ref, k1_ref, v1_ref, q4_ref, k4_ref, v4_ref, q16_ref, k16_ref, v16_ref, bm_ref,
                 o_ref, acc_ref, max_ref, den_ref):
    lane = lax.broadcasted_iota(jnp.int32, (BLOCK, LANES), 1)
    low = lane < ATTN_HEAD_DIM
    masks = (jnp.where(low, 1.0, 0.0).astype(BF16), jnp.where(low, 0.0, 1.0).astype(BF16))

    def block(q_ref, k_ref, v_ref, c0, q0, w0, width, bias_set, rows, first):
        for p in range(PAIRS_PER_HALF):
            cs = slice(c0 + p * LANES, c0 + (p + 1) * LANES)
            acc, m, den = _pair_scores(q_ref[pl.ds(q0, BLOCK), cs], k_ref[pl.ds(w0, width), cs],
                                       v_ref[pl.ds(w0, width), cs], bm_ref[bias_set, p, :, 0:width],
                                       masks, low)
            if not first:
                m_old = max_ref[p, rows, :]
                m_new = jnp.maximum(m_old, m)
                a, b = jnp.exp(m_old - m_new), jnp.exp(m - m_new)
                den = den_ref[p, rows, :] * a + den * b
                acc = acc_ref[p, rows, :] * a + acc * b
                m = m_new
            max_ref[p, rows, :] = m
            den_ref[p, rows, :] = den
            acc_ref[p, rows, :] = acc

    def d1_group(g, carry):
        for u in range(GROUP):
            n = g * GROUP + u
            q0 = pl.multiple_of(n * BLOCK, BLOCK)
            w0 = pl.multiple_of(jnp.maximum(n - 1, 0) * BLOCK, BLOCK)
            bias_set = jnp.where(n == 0, BIAS_FIRST[1], BIAS_FULL[1])
            block(q1_ref, k1_ref, v1_ref, 0, q0, w0, 2 * BLOCK, bias_set, pl.ds(q0, BLOCK), True)
        return carry
    lax.fori_loop(0, SEQ // BLOCK // GROUP, d1_group, 0)

    for r in range(4):
        for n in range(SEQ // 4 // BLOCK):
            block(q4_ref, k4_ref, v4_ref, r * HALF, n * BLOCK, max(n - 1, 0) * BLOCK, 2 * BLOCK,
                  BIAS_FIRST[4] if n == 0 else BIAS_FULL[4],
                  pl.ds(r + 4 * BLOCK * n, BLOCK, stride=4), False)

    for r in range(16):
        block(q16_ref, k16_ref, v16_ref, r * HALF, 0, 0, BLOCK, BIAS_FIRST[16],
              pl.ds(r, BLOCK, stride=16), False)

    for n in range(SEQ // BLOCK):
        rows = slice(n * BLOCK, (n + 1) * BLOCK)
        for p in range(PAIRS_PER_HALF):
            o_ref[rows, p * LANES:(p + 1) * LANES] = (acc_ref[p, rows, :] / den_ref[p, rows, :]).astype(BF16)


def _dilated_attention(qkv, bias_masks):
    tokens = qkv[0].shape[0]
    batch = tokens // SEQ
    specs = []
    for dil in (1, 4, 16):
        specs += [pl.BlockSpec((SEQ // dil, dil * HALF), lambda b, hf: (b, hf))] * 3
    state = pltpu.VMEM((PAIRS_PER_HALF, SEQ, LANES), F32)
    return pl.pallas_call(
        _attn_kernel,
        out_shape=jax.ShapeDtypeStruct((tokens, ATTN_WIDTH), BF16),
        grid=(batch, 2),
        in_specs=specs + [pl.BlockSpec((N_BIAS_SETS, PAIRS_PER_HALF, 2 * BLOCK, 2 * BLOCK),
                                       lambda b, hf: (0, hf, 0, 0))],
        out_specs=pl.BlockSpec((SEQ, HALF), lambda b, hf: (b, hf)),
        scratch_shapes=[state, state, state],
        compiler_params=_cparams(("arbitrary", "arbitrary")),
        name="dilated_attention",
    )(*qkv, bias_masks)


def _retention_kernel(r_ref, cos_ref, sin_ref, dmask_ref, qdec_ref, kdec_ref, cdec_ref, gain_ref,
                      o_ref, state_ref):
    @pl.when(pl.program_id(1) == 0)
    def _():
        state_ref[...] = jnp.zeros_like(state_ref)

    lane = lax.broadcasted_iota(jnp.int32, (RET_CHUNK, LANES), 1)
    low = lane < RET_KEY_DIM
    first_half = (lane % RET_KEY_DIM) < (RET_KEY_DIM // 2)

    def rotate(t, cos, sin):
        partner = jnp.where(first_half, pltpu.roll(t, LANES - RET_KEY_DIM // 2, 1),
                            pltpu.roll(t, RET_KEY_DIM // 2, 1))
        return t * cos + partner * sin

    for c in range(RET_ROWS // RET_CHUNK):
        rows = slice(c * RET_CHUNK, (c + 1) * RET_CHUNK)
        for hp in range(RET_HEADS // 2):
            qs = slice(hp * LANES, (hp + 1) * LANES)
            ks = slice(RET_QK_WIDTH + hp * LANES, RET_QK_WIDTH + (hp + 1) * LANES)
            cos, sin = cos_ref[rows, qs], sin_ref[rows, qs]
            q_pair = rotate(r_ref[rows, qs], cos, sin) * (RET_KEY_DIM ** -0.5)
            k_pair = rotate(r_ref[rows, ks], cos, sin)
            for hh in range(2):
                head = 2 * hp + hh
                vs = slice(2 * RET_QK_WIDTH + head * LANES, 2 * RET_QK_WIDTH + (head + 1) * LANES)
                gs = slice(2 * RET_QK_WIDTH + RET_WIDTH + head * LANES,
                           2 * RET_QK_WIDTH + RET_WIDTH + (head + 1) * LANES)
                keep = low if hh == 0 else jnp.logical_not(low)
                qm = jnp.where(keep, q_pair, 0.0)
                vb = r_ref[rows, vs].astype(BF16)
                state = state_ref[head]
                inner = _dot_nt(qm.astype(BF16), k_pair.astype(BF16)) * dmask_ref[head]
                y = _dot(inner.astype(BF16), vb)
                y = y + _dot((qm * qdec_ref[head]).astype(BF16), state.astype(BF16))
                state_ref[head] = state * cdec_ref[head] + _dot_tn((k_pair * kdec_ref[head]).astype(BF16), vb)
                mu = jnp.mean(y, axis=-1, keepdims=True)
                yc = y - mu
                var = jnp.mean(yc * yc, axis=-1, keepdims=True)
                yn = yc * lax.rsqrt(var + EPS) * gain_ref[:, head * LANES:(head + 1) * LANES]
                o_ref[rows, head * LANES:(head + 1) * LANES] = (_silu(r_ref[rows, gs]) * yn).astype(BF16)


def _retention_tables():
    half = RET_KEY_DIM // 2
    pos = jnp.arange(SEQ, dtype=F32)
    inv = ROPE_BASE ** (-jnp.arange(half, dtype=F32) / half)
    ang = pos[:, None] * inv[None, :]
    cos, sin = jnp.cos(ang), jnp.sin(ang)
    cos_full = jnp.tile(jnp.concatenate([cos, cos], axis=-1), (1, RET_HEADS))
    sin_signed = jnp.tile(jnp.concatenate([-sin, sin], axis=-1), (1, RET_HEADS))
    log_g = jnp.log(1.0 - 2.0 ** (-5.0 - jnp.arange(RET_HEADS, dtype=F32)))
    idx = jnp.arange(RET_CHUNK, dtype=F32)
    diff = idx[:, None] - idx[None, :]
    dmask = jnp.where(diff >= 0, jnp.exp(jnp.maximum(diff, 0.0)[None] * log_g[:, None, None]), 0.0)
    q_decay = jnp.exp((idx + 1.0)[None, :] * log_g[:, None])[..., None]
    k_decay = jnp.exp((RET_CHUNK - 1.0 - idx)[None, :] * log_g[:, None])[..., None]
    chunk_decay = jnp.exp(RET_CHUNK * log_g)[:, None, None]
    full = (RET_HEADS, RET_CHUNK, LANES)
    return (cos_full, sin_signed, dmask, jnp.broadcast_to(q_decay, full),
            jnp.broadcast_to(k_decay, full), jnp.broadcast_to(chunk_decay, full))


def _retention(ret_in, ret_gain):
    tokens = ret_in.shape[0]
    batch = tokens // SEQ
    per_seq = SEQ // RET_ROWS
    cos, sin, dmask, qdec, kdec, cdec = _retention_tables()
    tab = pl.BlockSpec((RET_ROWS, RET_QK_WIDTH), lambda b, j: (j, 0))
    const3 = pl.BlockSpec((RET_HEADS, RET_CHUNK, LANES), lambda b, j: (0, 0, 0))
    return pl.pallas_call(
        _retention_kernel,
        out_shape=jax.ShapeDtypeStruct((tokens, RET_WIDTH), BF16),
        grid=(batch, per_seq),
        in_specs=[
            pl.BlockSpec((RET_ROWS, RET_IN_WIDTH), lambda b, j: (b * per_seq + j, 0)),
            tab, tab, const3, const3, const3, const3,
            pl.BlockSpec((1, RET_WIDTH), lambda b, j: (0, 0)),
        ],
        out_specs=pl.BlockSpec((RET_ROWS, RET_WIDTH), lambda b, j: (b * per_seq + j, 0)),
        scratch_shapes=[pltpu.VMEM((RET_HEADS, LANES, RET_VALUE_DIM), F32)],
        compiler_params=_cparams(("arbitrary", "arbitrary")),
        name="retention",
    )(ret_in, cos, sin, dmask, qdec, kdec, cdec, ret_gain.reshape(1, RET_WIDTH))


def _swiglu_chunks(h, w1_ref, w3_ref, w2_ref):
    d_ff = w1_ref.shape[-1]
    total = None
    for c0 in range(0, d_ff, MXU_COLS):
        c1 = min(c0 + MXU_COLS, d_ff)
        z = (_silu(_dot(h, w1_ref[:, c0:c1].astype(BF16))) * _dot(h, w3_ref[:, c0:c1].astype(BF16))).astype(BF16)
        part = _dot(z, w2_ref[c0:c1, :].astype(BF16))
        total = part if total is None else total + part
    return total


def _out_proj_ffn_kernel(attn_ref, ret_ref, x_ref, mod_ref, wo_ref, gain_ref, w1_ref, w3_ref, w2_ref, out_ref):
    mod = mod_ref[...]
    mix = (_dot(attn_ref[...], wo_ref[:ATTN_WIDTH, :].astype(BF16))
           + _dot(ret_ref[...], wo_ref[ATTN_WIDTH:, :].astype(BF16)))
    x = x_ref[...] + mod[2:3, :] * mix
    h = _modulated_norm(x, gain_ref[...], mod[4:5], mod[3:4]).astype(BF16)
    out_ref[...] = x + mod[5:6, :] * _swiglu_chunks(h, w1_ref, w3_ref, w2_ref)


def _out_proj_dense_ffn(attn, ret, x, mod, w_out, layer, gain, w1, w3, w2, index):
    tokens = x.shape[0]
    d_ff = w1.shape[2]
    tm = TM_FFN
    per_seq = SEQ // tm
    row = lambda i: (i, 0)
    resident = pl.Buffered(1)
    return pl.pallas_call(
        _out_proj_ffn_kernel,
        out_shape=jax.ShapeDtypeStruct((tokens, D_MODEL), F32),
        grid=(tokens // tm,),
        in_specs=[
            pl.BlockSpec((tm, ATTN_WIDTH), row),
            pl.BlockSpec((tm, RET_WIDTH), row),
            pl.BlockSpec((tm, D_MODEL), row),
            pl.BlockSpec((None, 6, D_MODEL), lambda i: (i // per_seq, 0, 0)),
            pl.BlockSpec((None, D_MODEL, D_MODEL), lambda i: (layer, 0, 0), pipeline_mode=resident),
            pl.BlockSpec((1, D_MODEL), lambda i: (0, 0)),
            pl.BlockSpec((None, D_MODEL, d_ff), lambda i: (index, 0, 0), pipeline_mode=resident),
            pl.BlockSpec((None, D_MODEL, d_ff), lambda i: (index, 0, 0), pipeline_mode=resident),
            pl.BlockSpec((None, d_ff, D_MODEL), lambda i: (index, 0, 0), pipeline_mode=resident),
        ],
        out_specs=pl.BlockSpec((tm, D_MODEL), row),
        compiler_params=_cparams(("arbitrary",)),
        name="out_projection_dense_swiglu",
    )(attn, ret, x, mod, w_out, gain.reshape(1, D_MODEL), w1, w3, w2)


def _router_kernel(attn_ref, ret_ref, x_ref, mod_ref, wo_ref, gain_ref, wr_ref,
                   xo_ref, h_ref, gates_ref, pos_ref, post_ref, start_ref, cnt_ref, carry_ref, *, tiles_per_seq):
    i = pl.program_id(0)

    @pl.when(i % tiles_per_seq == 0)
    def _():
        carry_ref[...] = jnp.zeros_like(carry_ref)

    mod = mod_ref[...]
    mix = (_dot(attn_ref[...], wo_ref[:ATTN_WIDTH, :].astype(BF16))
           + _dot(ret_ref[...], wo_ref[ATTN_WIDTH:, :].astype(BF16)))
    x = x_ref[...] + mod[2:3, :] * mix
    xo_ref[...] = x
    h = _modulated_norm(x, gain_ref[...], mod[4:5], mod[3:4]).astype(BF16)
    h_ref[...] = h
    tm = h.shape[0]
    lane = lax.broadcasted_iota(jnp.int32, (tm, LANES), 1).astype(F32)
    logits = jnp.where(lane < N_EXPERTS, _dot(h, wr_ref[...]), -jnp.inf)
    m1 = jnp.max(logits, axis=-1, keepdims=True)
    i1 = jnp.min(jnp.where(logits == m1, lane, float(LANES)), axis=-1, keepdims=True)
    rest = jnp.where(lane == i1, -jnp.inf, logits)
    m2 = jnp.max(rest, axis=-1, keepdims=True)
    i2 = jnp.min(jnp.where(rest == m2, lane, float(LANES)), axis=-1, keepdims=True)
    e2 = jnp.exp(m2 - m1)
    g1 = 1.0 / (1.0 + e2)
    g2 = e2 / (1.0 + e2)
    gates_ref[...] = jnp.where(lane == i1, g1, 0.0) + jnp.where(lane == i2, g2, 0.0)
    chosen = (lane == i1) | (lane == i2)
    onehot = jnp.where(chosen, 1.0, 0.0)
    r = lax.broadcasted_iota(jnp.int32, (tm, tm), 0)
    c = lax.broadcasted_iota(jnp.int32, (tm, tm), 1)
    tril = jnp.where(c <= r, 1.0, 0.0).astype(BF16)
    incl = _dot(tril, onehot.astype(BF16))
    carry = carry_ref[0:1, :]
    start_ref[...] = carry_ref[...]
    pos = jnp.where(chosen, incl - 1.0 + carry, -1.0)
    pos_ref[...] = pos
    post_ref[...] = pos.T[:N_EXPERTS, :]
    total = carry + incl[tm - 1:tm, :]
    carry_ref[...] = jnp.broadcast_to(total, carry_ref.shape)
    cnt_ref[...] = jnp.broadcast_to(total, cnt_ref.shape)


def _out_proj_router(attn, ret, x, mod, w_out, layer, gain, w_router):
    tokens = x.shape[0]
    batch = tokens // SEQ
    tm = TM_PROJ
    per_seq = SEQ // tm
    wr = jnp.zeros((D_MODEL, LANES), BF16).at[:, :N_EXPERTS].set(w_router.astype(BF16))
    row = lambda i: (i, 0)
    return pl.pallas_call(
        functools.partial(_router_kernel, tiles_per_seq=per_seq),
        out_shape=(
            jax.ShapeDtypeStruct((tokens, D_MODEL), F32),
            jax.ShapeDtypeStruct((tokens, D_MODEL), BF16),
            jax.ShapeDtypeStruct((tokens, LANES), F32),
            jax.ShapeDtypeStruct((tokens, LANES), F32),
            jax.ShapeDtypeStruct((N_EXPERTS, tokens), F32),
            jax.ShapeDtypeStruct((tokens // tm, 8, LANES), F32),
            jax.ShapeDtypeStruct((batch, 8, LANES), F32),
        ),
        grid=(tokens // tm,),
        in_specs=[
            pl.BlockSpec((tm, ATTN_WIDTH), row),
            pl.BlockSpec((tm, RET_WIDTH), row),
            pl.BlockSpec((tm, D_MODEL), row),
            pl.BlockSpec((None, 6, D_MODEL), lambda i: (i // per_seq, 0, 0)),
            pl.BlockSpec((None, D_MODEL, D_MODEL), lambda i: (layer, 0, 0)),
            pl.BlockSpec((1, D_MODEL), lambda i: (0, 0)),
            pl.BlockSpec((D_MODEL, LANES), lambda i: (0, 0)),
        ],
        out_specs=(
            pl.BlockSpec((tm, D_MODEL), row),
            pl.BlockSpec((tm, D_MODEL), row),
            pl.BlockSpec((tm, LANES), row),
            pl.BlockSpec((tm, LANES), row),
            pl.BlockSpec((N_EXPERTS, tm), lambda i: (0, i)),
            pl.BlockSpec((None, 8, LANES), lambda i: (i, 0, 0)),
            pl.BlockSpec((None, 8, LANES), lambda i: (i // per_seq, 0, 0)),
        ),
        scratch_shapes=[pltpu.VMEM((8, LANES), F32)],
        compiler_params=_cparams(("arbitrary",)),
        name="out_projection_router",
    )(attn, ret, x, mod, w_out, gain.reshape(1, D_MODEL), wr)


TOK_BLOCKS = SEQ // TR_MOE
TAIL_TILES = (64, 128, TR_MOE)
SPILL_ROWS = 16
SMALL_WINDOW = 128
BIG_TILE = 1024
SEG_ALIGN = 16
GROUP_ROWS = 3072
PACKED_ROWS = 2 * SEQ + N_EXPERTS * SEG_ALIGN + 2 * TR_MOE
BIG_CHUNK = 128
STAGE_ROWS = SEQ + TR_MOE
N_BOUNDS = TOK_BLOCKS + 1


def _moe_plan(start, cnt):
    batch = cnt.shape[0]
    counts = cnt[:, 0, :N_EXPERTS].astype(jnp.int32)
    bounds = jnp.concatenate([start[:, 0, :N_EXPERTS].reshape(batch, TOK_BLOCKS, N_EXPERTS).astype(jnp.int32),
                              counts[:, None, :]], axis=1)
    seg = (counts + (SEG_ALIGN - 1)) // SEG_ALIGN * SEG_ALIGN
    rows_e = jnp.sum(seg, axis=0)
    groups_e = (rows_e + (GROUP_ROWS - 1)) // GROUP_ROWS
    first_group = jnp.cumsum(groups_e) - groups_e
    seg_off = first_group[None, :] * GROUP_ROWS + jnp.cumsum(seg, axis=0) - seg
    max_rows = batch * (2 * SEQ + N_EXPERTS * (SEG_ALIGN - 1))
    n_groups = max_rows // GROUP_ROWS + N_EXPERTS
    g = jnp.arange(n_groups)
    g_expert = jnp.minimum(jnp.sum(g[:, None] >= jnp.cumsum(groups_e)[None, :], axis=1), N_EXPERTS - 1)
    g_rows = jnp.clip(rows_e[g_expert] - (g - first_group[g_expert]) * GROUP_ROWS, 0, GROUP_ROWS)
    g_rows = jnp.where(g < jnp.sum(groups_e), g_rows, 0)
    return (bounds.reshape(-1), seg.reshape(-1), seg_off.reshape(-1),
            g_expert.astype(jnp.int32), g_rows.astype(jnp.int32), n_groups)


def _segment_copies(seg, make_copy, action):
    n_big = lax.shift_right_logical(seg, BIG_CHUNK.bit_length() - 1)

    def big(i, carry):
        getattr(make_copy(pl.multiple_of(i * BIG_CHUNK, BIG_CHUNK), BIG_CHUNK), action)()
        return carry
    lax.fori_loop(0, n_big, big, 0)
    rest0 = n_big * BIG_CHUNK
    n_small = lax.shift_right_logical(seg - rest0, SEG_ALIGN.bit_length() - 1)

    def small(i, carry):
        getattr(make_copy(pl.multiple_of(rest0 + i * SEG_ALIGN, SEG_ALIGN), SEG_ALIGN), action)()
        return carry
    lax.fori_loop(0, n_small, small, 0)


def _block_windows(tbl_ref, b, e, align):
    base = b * N_BOUNDS * N_EXPERTS + e
    bounds = [tbl_ref[base + tb * N_EXPERTS] for tb in range(N_BOUNDS)]
    shift = align.bit_length() - 1
    win = [pl.multiple_of(lax.shift_left(lax.shift_right_logical(s, shift), shift), align) for s in bounds[:-1]]
    all_small = functools.reduce(jnp.logical_and,
                                 [bounds[tb + 1] - win[tb] <= SMALL_WINDOW for tb in range(TOK_BLOCKS)])
    return bounds, win, all_small


def _dispatch_kernel(tbl_ref, seg_ref, off_ref, h_ref, post_ref, rows_hbm, acc_ref, stage_ref, sem):
    b = pl.program_id(0)
    tr = TR_MOE

    def copies(e, action):
        slot = e % 2
        off = off_ref[b * N_EXPERTS + e]

        def make_copy(r0, rows):
            return pltpu.make_async_copy(stage_ref.at[slot, pl.ds(r0, rows), :],
                                         rows_hbm.at[pl.ds(pl.multiple_of(off + r0, SEG_ALIGN), rows), :],
                                         sem.at[slot])
        _segment_copies(seg_ref[b * N_EXPERTS + e], make_copy, action)

    def expert(e, carry):
        bounds, win, all_small = _block_windows(tbl_ref, b, e, SUBLANES)
        n_tiles = lax.shift_right_logical(bounds[-1] + (tr - 1), tr.bit_length() - 1)

        def clear(r, c):
            acc_ref[pl.ds(pl.multiple_of(r * tr, tr), tr), :] = jnp.zeros((tr, D_MODEL), F32)
            return c
        lax.fori_loop(0, n_tiles + 2, clear, 0)

        def gather(height):
            slot_id = lax.broadcasted_iota(jnp.int32, (height, tr), 0).astype(F32)
            for tb in range(TOK_BLOCKS):
                ts = slice(tb * tr, (tb + 1) * tr)
                local = post_ref[pl.ds(e, 1), ts] - win[tb].astype(F32)
                sel = jnp.where(local == slot_id, 1.0, 0.0).astype(BF16)
                acc_ref[pl.ds(win[tb], height), :] += _dot(sel, h_ref[ts, :])

        @pl.when(all_small)
        def _():
            gather(SMALL_WINDOW)

        @pl.when(jnp.logical_not(all_small))
        def _():
            gather(tr + SPILL_ROWS)

        @pl.when(e >= 2)
        def _():
            copies(e - 2, "wait")

        def to_stage(r, c):
            rows = pl.ds(pl.multiple_of(r * tr, tr), tr)
            stage_ref[e % 2, rows, :] = acc_ref[rows, :].astype(BF16)
            return c
        lax.fori_loop(0, n_tiles, to_stage, 0)
        copies(e, "start")
        return carry
    lax.fori_loop(0, N_EXPERTS, expert, 0)
    copies(N_EXPERTS - 2, "wait")
    copies(N_EXPERTS - 1, "wait")


def _dispatch(h, post, bounds, seg, seg_off, total_rows):
    tokens = h.shape[0]
    grid_spec = pltpu.PrefetchScalarGridSpec(
        num_scalar_prefetch=3,
        grid=(tokens // SEQ,),
        in_specs=[pl.BlockSpec((SEQ, D_MODEL), lambda b, *_: (b, 0)),
                  pl.BlockSpec((N_EXPERTS, SEQ), lambda b, *_: (0, b))],
        out_specs=pl.BlockSpec(memory_space=pl.ANY),
        scratch_shapes=[pltpu.VMEM((SEQ + 3 * TR_MOE, D_MODEL), F32),
                        pltpu.VMEM((2, STAGE_ROWS, D_MODEL), BF16),
                        pltpu.SemaphoreType.DMA((2,))],
    )
    return pl.pallas_call(
        _dispatch_kernel,
        out_shape=jax.ShapeDtypeStruct((total_rows, D_MODEL), BF16),
        grid_spec=grid_spec,
        compiler_params=_cparams(("arbitrary",)),
        name="expert_dispatch",
    )(bounds, seg, seg_off, h, post)


def _experts_kernel(ge_ref, rows_ref, hs_ref, w1_ref, w3_ref, w2_ref, ys_ref, acc_ref, wb1_ref, wb3_ref, wb2_ref):
    g, f = pl.program_id(0), pl.program_id(1)
    last_f = pl.num_programs(1) - 1
    tr = TR_MOE
    rows = rows_ref[g]
    n_big = lax.shift_right_logical(rows, BIG_TILE.bit_length() - 1)
    n_full = lax.shift_right_logical(rows, tr.bit_length() - 1)
    rest = rows - n_full * tr
    n_tiles = lax.shift_right_logical(rows + (tr - 1), tr.bit_length() - 1)
    tail0 = pl.multiple_of(n_full * tr, tr)

    @pl.when(rows > 0)
    def _():
        wb1_ref[...] = w1_ref[...].astype(BF16)
        wb3_ref[...] = w3_ref[...].astype(BF16)
        wb2_ref[...] = w2_ref[...].astype(BF16)

    @pl.when(f == 0)
    def _():
        def clear(r, carry):
            acc_ref[pl.ds(pl.multiple_of(r * tr, tr), tr), :] = jnp.zeros((tr, D_MODEL), F32)
            return carry
        lax.fori_loop(0, n_tiles, clear, 0)

    def swiglu(hr, r0, m):
        if m > tr:
            acc_ref[pl.ds(r0, m), :] += _swiglu_chunks(hr, wb1_ref, wb3_ref, wb2_ref)
        else:
            z = (_silu(_dot(hr, wb1_ref[...])) * _dot(hr, wb3_ref[...])).astype(BF16)
            acc_ref[pl.ds(r0, m), :] += _dot(z, wb2_ref[...])

    def big_tile(r, carry):
        r0 = pl.multiple_of(r * BIG_TILE, BIG_TILE)
        swiglu(hs_ref[pl.ds(r0, BIG_TILE), :], r0, BIG_TILE)
        return carry
    lax.fori_loop(0, n_big, big_tile, 0)

    done = n_big * BIG_TILE
    m = BIG_TILE // 2
    while m >= tr:
        take = (rows - done) >= m

        @pl.when(take)
        def _(done=done, m=m):
            r0 = pl.multiple_of(done, tr)
            swiglu(hs_ref[pl.ds(r0, m), :], r0, m)
        done = done + jnp.where(take, m, 0)
        m //= 2
    lo = 0
    for m in TAIL_TILES:
        @pl.when((rest > lo) & (rest <= m))
        def _():
            valid = lax.broadcasted_iota(jnp.int32, (m, D_MODEL), 0) < rest
            hr = hs_ref[pl.ds(tail0, m), :]
            swiglu(jnp.where(valid, hr, jnp.zeros_like(hr)), tail0, m)
        lo = m

    @pl.when(f == last_f)
    def _():
        ys_ref[...] = jnp.zeros_like(ys_ref)

        def store(r, carry):
            rr = pl.ds(pl.multiple_of(r * tr, tr), tr)
            ys_ref[rr, :] = acc_ref[rr, :].astype(BF16)
            return carry
        lax.fori_loop(0, n_tiles, store, 0)


def _experts(hs, g_expert, g_rows, n_groups, w1, w3, w2):
    d_ff = w1.shape[2]
    tf = TF_MOE
    n_f = d_ff // tf
    chunk = lambda g, f, gr: jnp.where(gr[g] > 0, f, n_f - 1)
    grid_spec = pltpu.PrefetchScalarGridSpec(
        num_scalar_prefetch=2,
        grid=(n_groups, n_f),
        in_specs=[
            pl.BlockSpec((GROUP_ROWS, D_MODEL), lambda g, f, ge, gr: (g, 0)),
            pl.BlockSpec((None, D_MODEL, tf), lambda g, f, ge, gr: (ge[g], 0, chunk(g, f, gr))),
            pl.BlockSpec((None, D_MODEL, tf), lambda g, f, ge, gr: (ge[g], 0, chunk(g, f, gr))),
            pl.BlockSpec((None, tf, D_MODEL), lambda g, f, ge, gr: (ge[g], chunk(g, f, gr), 0)),
        ],
        out_specs=pl.BlockSpec((GROUP_ROWS, D_MODEL), lambda g, f, ge, gr: (g, 0)),
        scratch_shapes=[pltpu.VMEM((GROUP_ROWS, D_MODEL), F32),
                        pltpu.VMEM((D_MODEL, tf), BF16), pltpu.VMEM((D_MODEL, tf), BF16),
                        pltpu.VMEM((tf, D_MODEL), BF16)],
    )
    return pl.pallas_call(
        _experts_kernel,
        out_shape=jax.ShapeDtypeStruct(hs.shape, BF16),
        grid_spec=grid_spec,
        compiler_params=_cparams(("arbitrary", "arbitrary")),
        name="expert_swiglu",
    )(g_expert, g_rows, hs, w1, w3, w2)


def _combine_kernel(tbl_ref, seg_ref, off_ref, rows_hbm, pos_ref, gates_ref, x_ref, mod_ref, out_ref, buf_ref, sem):
    b = pl.program_id(0)
    n_seq = pl.num_programs(0)
    tr = TR_MOE

    def all_segments(seq, action):
        def body(e, first):
            off = off_ref[seq * N_EXPERTS + e]
            seg = seg_ref[seq * N_EXPERTS + e]

            def make_copy(r0, rows):
                return pltpu.make_async_copy(
                    rows_hbm.at[pl.ds(pl.multiple_of(off + r0, SEG_ALIGN), rows), :],
                    buf_ref.at[seq % 2, pl.ds(pl.multiple_of(first + r0, SEG_ALIGN), rows), :],
                    sem.at[seq % 2, e])
            _segment_copies(seg, make_copy, action)
            return first + seg
        lax.fori_loop(0, N_EXPERTS, body, 0)

    @pl.when(b == 0)
    def _():
        buf_ref[...] = jnp.zeros_like(buf_ref)
        all_segments(b, "start")

    @pl.when(b + 1 < n_seq)
    def _():
        all_segments(b + 1, "start")

    out_ref[...] = x_ref[...]
    lane = lax.broadcasted_iota(jnp.int32, (tr, LANES), 1)
    layer_gate = mod_ref[5:6, :]
    all_segments(b, "wait")

    def expert(e, seg_start):
        bounds, win, all_small = _block_windows(tbl_ref, b, e, SEG_ALIGN)

        def scatter(tb, first, height):
            ts = slice(tb * tr, (tb + 1) * tr)
            pos_col = jnp.sum(jnp.where(lane == e, pos_ref[ts, :], 0.0), axis=-1, keepdims=True)
            gate_col = jnp.sum(jnp.where(lane == e, gates_ref[ts, :], 0.0), axis=-1, keepdims=True)
            slot_id = lax.broadcasted_iota(jnp.int32, (tr, height), 1).astype(F32) + float(first)
            sel = jnp.where(pos_col - win[tb].astype(F32) == slot_id, 1.0, 0.0).astype(BF16)
            rows = buf_ref[b % 2, pl.ds(pl.multiple_of(seg_start + win[tb] + first, SEG_ALIGN), height), :]
            out_ref[ts, :] += layer_gate * (gate_col * _dot(sel, rows))

        @pl.when(all_small)
        def _():
            for tb in range(TOK_BLOCKS):
                scatter(tb, 0, SMALL_WINDOW)

        @pl.when(jnp.logical_not(all_small))
        def _():
            for tb in range(TOK_BLOCKS):
                scatter(tb, 0, tr)
            for tb in range(TOK_BLOCKS):
                @pl.when(bounds[tb + 1] - win[tb] > tr)
                def _():
                    scatter(tb, tr, SPILL_ROWS)
        return seg_start + seg_ref[b * N_EXPERTS + e]
    lax.fori_loop(0, N_EXPERTS, expert, 0)


def _combine(ys, pos, gates, x, mod, bounds, seg, seg_off):
    tokens = x.shape[0]
    grid_spec = pltpu.PrefetchScalarGridSpec(
        num_scalar_prefetch=3,
        grid=(tokens // SEQ,),
        in_specs=[pl.BlockSpec(memory_space=pl.ANY),
                  pl.BlockSpec((SEQ, LANES), lambda b, *_: (b, 0)),
                  pl.BlockSpec((SEQ, LANES), lambda b, *_: (b, 0)),
                  pl.BlockSpec((SEQ, D_MODEL), lambda b, *_: (b, 0)),
                  pl.BlockSpec((None, 6, D_MODEL), lambda b, *_: (b, 0, 0))],
        out_specs=pl.BlockSpec((SEQ, D_MODEL), lambda b, *_: (b, 0)),
        scratch_shapes=[pltpu.VMEM((2, PACKED_ROWS, D_MODEL), BF16),
                        pltpu.SemaphoreType.DMA((2, N_EXPERTS))],
    )
    return pl.pallas_call(
        _combine_kernel,
        out_shape=jax.ShapeDtypeStruct((tokens, D_MODEL), F32),
        grid_spec=grid_spec,
        compiler_params=_cparams(("arbitrary",)),
        name="expert_combine",
    )(bounds, seg, seg_off, ys, pos, gates, x, mod)


def _token_mixer(x, mod, gain, w_in, layer, q_gain, k_gain, ret_gain, bias_masks):
    *qkv, ret_in = _in_proj(x, mod, gain, w_in, layer, q_gain, k_gain)
    return _dilated_attention(qkv, bias_masks), _retention(ret_in, ret_gain)


def _out_proj_moe_ffn(attn, ret, x, mod, w_out, layer, gain, w_router, w1, w3, w2):
    x, h, gates, pos, post, start, cnt = _out_proj_router(attn, ret, x, mod, w_out, layer, gain, w_router)
    bounds, seg, seg_off, g_expert, g_rows, n_groups = _moe_plan(start, cnt)
    hs = _dispatch(h, post, bounds, seg, seg_off, n_groups * GROUP_ROWS)
    ys = _experts(hs, g_expert, g_rows, n_groups, w1, w3, w2)
    return _combine(ys, pos, gates, x, mod, bounds, seg, seg_off)


def kernel(x, c, rel_bias_table, norm_mix, norm_ffn, w_mod, b_mod, w_in, q_gain, k_gain, ret_gain, w_out,
           ffn_w1, ffn_w3, ffn_w2, moe_router, moe_w1, moe_w3, moe_w2):
    batch, seq, d_model = x.shape
    assert (seq, d_model) == (SEQ, D_MODEL)
    depth = w_mod.shape[0]
    mods = _modulation(c, w_mod, b_mod).reshape(depth, batch, 6, D_MODEL)
    bias_masks = _bias_masks(rel_bias_table)
    xt = x.reshape(batch * seq, d_model)
    for layer in range(depth):
        mod = mods[layer]
        attn, ret = _token_mixer(xt, mod, norm_mix[layer], w_in, layer, q_gain[layer], k_gain[layer],
                                 ret_gain[layer], bias_masks)
        i = layer // 2
        if layer % 2 == 0:
            xt = _out_proj_dense_ffn(attn, ret, xt, mod, w_out, layer, norm_ffn[layer],
                                     ffn_w1, ffn_w3, ffn_w2, i)
        else:
            xt = _out_proj_moe_ffn(attn, ret, xt, mod, w_out, layer, norm_ffn[layer],
                                   moe_router[i], moe_w1[i], moe_w3[i], moe_w2[i])
    return xt.reshape(batch, seq, d_model)
```

```python
import functools
import math

import jax
import jax.numpy as jnp
import numpy as np
from jax import lax
from jax.experimental import pallas as pl
from jax.experimental.pallas import tpu as pltpu

D_MODEL = 1024
SEQ = 2048
ATTN_HEADS = 8
ATTN_HEAD_DIM = 64
ATTN_WIDTH = ATTN_HEADS * ATTN_HEAD_DIM
DILATED_PATTERNS = ((128, 1), (512, 4), (2048, 16))
BLOCK = 128
NUM_BUCKETS = 32
MAX_DISTANCE = 2048
RET_HEADS = 4
RET_KEY_DIM = 64
RET_VALUE_DIM = 128
RET_WIDTH = RET_HEADS * RET_VALUE_DIM
RET_QK_WIDTH = RET_HEADS * RET_KEY_DIM
RET_CHUNK = 128
ROPE_BASE = 10000.0
IN_WIDTH = 3 * ATTN_WIDTH + 2 * RET_QK_WIDTH + 2 * RET_WIDTH
RET_IN_WIDTH = IN_WIDTH - 3 * ATTN_WIDTH
N_EXPERTS = 8
EPS = 1e-6
NEG_INF = -1e30

LANES = 128
VMEM_LIMIT = 60 * 1024 * 1024

BF16 = jnp.bfloat16
F32 = jnp.float32

TM_PROJ = 256
TM_FFN = 512
MXU_COLS = 256
TR_MOE = 256
TF_MOE = 512
assert TM_PROJ == TR_MOE
RET_ROWS = 2048
SUBLANES = 8


def _cparams(sem):
    return pltpu.CompilerParams(dimension_semantics=sem, vmem_limit_bytes=VMEM_LIMIT)


def _dot(a, b):
    return jnp.dot(a, b, preferred_element_type=F32)


def _dot_nt(a, b):
    return lax.dot_general(a, b, (((1,), (1,)), ((), ())), preferred_element_type=F32)


def _dot_tn(a, b):
    return lax.dot_general(a, b, (((0,), (0,)), ((), ())), preferred_element_type=F32)


def _split_bf16(v):
    hi = v.astype(BF16)
    lo = (v - hi.astype(F32)).astype(BF16)
    return hi, lo


def _silu(v):
    return v * (1.0 / (1.0 + jnp.exp(-v)))


def _modulated_norm(x, gain, scale, shift):
    ms = jnp.mean(x * x, axis=-1, keepdims=True)
    y = x * lax.rsqrt(ms + EPS) * gain
    return y * (1.0 + scale) + shift


def _mod_kernel(c_ref, w_ref, b_ref, o_ref):
    ca = _silu(c_ref[...]).astype(BF16)
    o_ref[...] = _dot(ca, w_ref[...].astype(BF16)) + b_ref[...]


def _modulation(c, w_mod, b_mod):
    depth, _, width = w_mod.shape
    batch = c.shape[0]
    tn = 1536
    return pl.pallas_call(
        _mod_kernel,
        out_shape=jax.ShapeDtypeStruct((depth, batch, width), F32),
        grid=(depth, width // tn),
        in_specs=[
            pl.BlockSpec((batch, D_MODEL), lambda l, n: (0, 0)),
            pl.BlockSpec((None, D_MODEL, tn), lambda l, n: (l, 0, n)),
            pl.BlockSpec((None, 1, tn), lambda l, n: (l, 0, n)),
        ],
        out_specs=pl.BlockSpec((None, batch, tn), lambda l, n: (l, 0, n)),
        compiler_params=_cparams(("arbitrary", "arbitrary")),
        name="adaln_modulation",
    )(c, w_mod, b_mod.reshape(depth, 1, width))


def _bias_kernel(table_ref, bucket_ref, o_ref):
    bucket = bucket_ref[...]
    acc = [jnp.full(bucket.shape, NEG_INF, F32) for _ in range(ATTN_HEADS)]
    for b in range(NUM_BUCKETS):
        hit = bucket == b
        for h in range(ATTN_HEADS):
            acc[h] = jnp.where(hit, table_ref[b, h], acc[h])
    for h in range(ATTN_HEADS):
        o_ref[h // 2, (h % 2) * BLOCK:(h % 2 + 1) * BLOCK, :] = acc[h]


BIAS_FULL = {1: 0, 4: 2}
BIAS_FIRST = {1: 1, 4: 3, 16: 4}
N_BIAS_SETS = 5


def _bias_masks(rel_bias_table):
    i = jnp.arange(BLOCK)[:, None]
    j = jnp.arange(2 * BLOCK)[None, :]
    max_exact = NUM_BUCKETS // 2

    def bucket_of(rel, dilation, w_sub, exists):
        n = jnp.maximum(rel * dilation, 0)
        nf = jnp.maximum(n.astype(F32), float(max_exact))
        large = max_exact + (jnp.log(nf / max_exact) / math.log(MAX_DISTANCE / max_exact)
                             * (NUM_BUCKETS - max_exact)).astype(jnp.int32)
        large = jnp.minimum(large, NUM_BUCKETS - 1)
        bucket = jnp.where(n < max_exact, n, large)
        allowed = (rel >= 0) & (rel <= w_sub) & exists
        return jnp.where(allowed, bucket, -1)

    sets = [None] * N_BIAS_SETS
    for window, dilation in DILATED_PATTERNS:
        w_sub = window // dilation
        if dilation in BIAS_FULL:
            sets[BIAS_FULL[dilation]] = bucket_of(i - j + BLOCK, dilation, w_sub, j >= 0)
        sets[BIAS_FIRST[dilation]] = bucket_of(i - j, dilation, w_sub, j < BLOCK)
    buckets = jnp.stack(sets).astype(jnp.int32)
    return pl.pallas_call(
        _bias_kernel,
        out_shape=jax.ShapeDtypeStruct((N_BIAS_SETS, ATTN_HEADS // 2, 2 * BLOCK, 2 * BLOCK), F32),
        grid=(N_BIAS_SETS,),
        in_specs=[
            pl.BlockSpec(memory_space=pltpu.SMEM),
            pl.BlockSpec((None, BLOCK, 2 * BLOCK), lambda p: (p, 0, 0)),
        ],
        out_specs=pl.BlockSpec((None, ATTN_HEADS // 2, 2 * BLOCK, 2 * BLOCK), lambda p: (p, 0, 0, 0)),
        compiler_params=_cparams(("arbitrary",)),
        name="relative_bias_masks",
    )(rel_bias_table, buckets)


HALF = ATTN_WIDTH // 2


def _in_proj_kernel(x_ref, mod_ref, gain_ref, w_ref, qg_ref, kg_ref, grp_ref,
                    q1_ref, k1_ref, v1_ref, q4_ref, k4_ref, v4_ref, q16_ref, k16_ref, v16_ref, r_ref,
                    perm_ref):
    mod = mod_ref[...]
    h = _modulated_norm(x_ref[...], gain_ref[...], mod[1:2], mod[0:1]).astype(BF16)
    proj = _dot(h, w_ref[...].astype(BF16))
    grp = grp_ref[...]
    tm = proj.shape[0]

    def head_norm(t, gain):
        hi, lo = _split_bf16(t * t)
        ss = _dot(hi, grp) + _dot(lo, grp)
        return t * lax.rsqrt(ss * (1.0 / ATTN_HEAD_DIM) + EPS) * gain

    def emit(t, o1_ref, o4_ref, o16_ref):
        o1_ref[...] = t.astype(BF16)
        for j in range(ATTN_WIDTH // LANES):
            perm_ref[j] = t[:, j * LANES:(j + 1) * LANES]
        for dil, o_ref in ((4, o4_ref), (16, o16_ref)):
            for hf in range(2):
                for r in range(dil):
                    for jj in range(HALF // LANES):
                        c0 = (hf * dil + r) * HALF + jj * LANES
                        o_ref[:, c0:c0 + LANES] = perm_ref[hf * (HALF // LANES) + jj,
                                                           pl.ds(r, tm // dil, stride=dil), :].astype(BF16)

    emit(head_norm(proj[:, :ATTN_WIDTH], qg_ref[...]) * (ATTN_HEAD_DIM ** -0.5), q1_ref, q4_ref, q16_ref)
    emit(head_norm(proj[:, ATTN_WIDTH:2 * ATTN_WIDTH], kg_ref[...]), k1_ref, k4_ref, k16_ref)
    emit(proj[:, 2 * ATTN_WIDTH:3 * ATTN_WIDTH], v1_ref, v4_ref, v16_ref)
    r_ref[...] = proj[:, 3 * ATTN_WIDTH:]


def _in_proj(x, mod, gain, w_in, layer, q_gain, k_gain):
    tokens = x.shape[0]
    tm = TM_PROJ
    per_seq = SEQ // tm
    grp = np.kron(np.eye(ATTN_HEADS), np.ones((ATTN_HEAD_DIM, ATTN_HEAD_DIM))).astype(np.float32)
    row = lambda i: (i, 0)
    const = lambda i: (0, 0)
    layouts = []
    for dil in (1, 4, 16):
        shape = jax.ShapeDtypeStruct((tokens // dil, dil * ATTN_WIDTH), BF16)
        spec = pl.BlockSpec((tm // dil, dil * ATTN_WIDTH), row)
        layouts.append(((shape,) * 3, (spec,) * 3))
    out_shape = sum((s for s, _ in layouts), ()) + (jax.ShapeDtypeStruct((tokens, RET_IN_WIDTH), F32),)
    out_specs = sum((s for _, s in layouts), ()) + (pl.BlockSpec((tm, RET_IN_WIDTH), row),)
    return pl.pallas_call(
        _in_proj_kernel,
        out_shape=out_shape,
        grid=(tokens // tm,),
        in_specs=[
            pl.BlockSpec((tm, D_MODEL), row),
            pl.BlockSpec((None, 6, D_MODEL), lambda i: (i // per_seq, 0, 0)),
            pl.BlockSpec((1, D_MODEL), const),
            pl.BlockSpec((None, D_MODEL, IN_WIDTH), lambda i: (layer, 0, 0), pipeline_mode=pl.Buffered(1)),
            pl.BlockSpec((1, ATTN_WIDTH), const),
            pl.BlockSpec((1, ATTN_WIDTH), const),
            pl.BlockSpec((ATTN_WIDTH, ATTN_WIDTH), const),
        ],
        out_specs=out_specs,
        scratch_shapes=[pltpu.VMEM((ATTN_WIDTH // LANES, tm, LANES), F32)],
        compiler_params=_cparams(("arbitrary",)),
        name="in_projection",
    )(x, mod, gain.reshape(1, D_MODEL), w_in,
      jnp.tile(q_gain, ATTN_HEADS).reshape(1, ATTN_WIDTH),
      jnp.tile(k_gain, ATTN_HEADS).reshape(1, ATTN_WIDTH),
      jnp.asarray(grp, BF16))


PAIRS_PER_HALF = ATTN_HEADS // 4
GROUP = 16


def _pair_scores(qp, kp, vp, bias2, masks, low):
    q2 = jnp.concatenate([qp * masks[0], qp * masks[1]], axis=0)
    s = _dot_nt(q2, kp) + bias2
    m = jnp.max(s, axis=-1, keepdims=True)
    p = jnp.exp(s - m)
    den = jnp.sum(p, axis=-1, keepdims=True)
    pv = _dot(p.astype(BF16), vp)
    pick = lambda t: jnp.where(low, t[:BLOCK], t[BLOCK:])
    return pick(pv), pick(m), pick(den)


def _attn_kernel(q1_ref, k1_ref, v1_ref, q4_ref, k4_ref, v4_ref, q16_ref, k16_ref, v16_ref, bm_ref,
                 o_ref, acc_ref, max_ref, den_ref):
    lane = lax.broadcasted_iota(jnp.int32, (BLOCK, LANES), 1)
    low = lane < ATTN_HEAD_DIM
    masks = (jnp.where(low, 1.0, 0.0).astype(BF16), jnp.where(low, 0.0, 1.0).astype(BF16))

    def block(q_ref, k_ref, v_ref, c0, q0, w0, width, bias_set, rows, first):
        for p in range(PAIRS_PER_HALF):
            cs = slice(c0 + p * LANES, c0 + (p + 1) * LANES)
            acc, m, den = _pair_scores(q_ref[pl.ds(q0, BLOCK), cs], k_ref[pl.ds(w0, width), cs],
                                       v_ref[pl.ds(w0, width), cs], bm_ref[bias_set, p, :, 0:width],
                                       masks, low)
            if not first:
                m_old = max_ref[p, rows, :]
                m_new = jnp.maximum(m_old, m)
                a, b = jnp.exp(m_old - m_new), jnp.exp(m - m_new)
                den = den_ref[p, rows, :] * a + den * b
                acc = acc_ref[p, rows, :] * a + acc * b
                m = m_new
            max_ref[p, rows, :] = m
            den_ref[p, rows, :] = den
            acc_ref[p, rows, :] = acc

    def d1_group(g, carry):
        for u in range(GROUP):
            n = g * GROUP + u
            q0 = pl.multiple_of(n * BLOCK, BLOCK)
            w0 = pl.multiple_of(jnp.maximum(n - 1, 0) * BLOCK, BLOCK)
            bias_set = jnp.where(n == 0, BIAS_FIRST[1], BIAS_FULL[1])
            block(q1_ref, k1_ref, v1_ref, 0, q0, w0, 2 * BLOCK, bias_set, pl.ds(q0, BLOCK), True)
        return carry
    lax.fori_loop(0, SEQ // BLOCK // GROUP, d1_group, 0)

    for r in range(4):
        for n in range(SEQ // 4 // BLOCK):
            block(q4_ref, k4_ref, v4_ref, r * HALF, n * BLOCK, max(n - 1, 0) * BLOCK, 2 * BLOCK,
                  BIAS_FIRST[4] if n == 0 else BIAS_FULL[4],
                  pl.ds(r + 4 * BLOCK * n, BLOCK, stride=4), False)

    for r in range(16):
        block(q16_ref, k16_ref, v16_ref, r * HALF, 0, 0, BLOCK, BIAS_FIRST[16],
              pl.ds(r, BLOCK, stride=16), False)

    for n in range(SEQ // BLOCK):
        rows = slice(n * BLOCK, (n + 1) * BLOCK)
        for p in range(PAIRS_PER_HALF):
            o_ref[rows, p * LANES:(p + 1) * LANES] = (acc_ref[p, rows, :] / den_ref[p, rows, :]).astype(BF16)


def _dilated_attention(qkv, bias_masks):
    tokens = qkv[0].shape[0]
    batch = tokens // SEQ
    specs = []
    for dil in (1, 4, 16):
        specs += [pl.BlockSpec((SEQ // dil, dil * HALF), lambda b, hf: (b, hf))] * 3
    state = pltpu.VMEM((PAIRS_PER_HALF, SEQ, LANES), F32)
    return pl.pallas_call(
        _attn_kernel,
        out_shape=jax.ShapeDtypeStruct((tokens, ATTN_WIDTH), BF16),
        grid=(batch, 2),
        in_specs=specs + [pl.BlockSpec((N_BIAS_SETS, PAIRS_PER_HALF, 2 * BLOCK, 2 * BLOCK),
                                       lambda b, hf: (0, hf, 0, 0))],
        out_specs=pl.BlockSpec((SEQ, HALF), lambda b, hf: (b, hf)),
        scratch_shapes=[state, state, state],
        compiler_params=_cparams(("arbitrary", "arbitrary")),
        name="dilated_attention",
    )(*qkv, bias_masks)


def _retention_kernel(r_ref, cos_ref, sin_ref, dmask_ref, qdec_ref, kdec_ref, cdec_ref, gain_ref,
                      o_ref, state_ref):
    @pl.when(pl.program_id(1) == 0)
    def _():
        state_ref[...] = jnp.zeros_like(state_ref)

    lane = lax.broadcasted_iota(jnp.int32, (RET_CHUNK, LANES), 1)
    low = lane < RET_KEY_DIM
    first_half = (lane % RET_KEY_DIM) < (RET_KEY_DIM // 2)

    def rotate(t, cos, sin):
        partner = jnp.where(first_half, pltpu.roll(t, LANES - RET_KEY_DIM // 2, 1),
                            pltpu.roll(t, RET_KEY_DIM // 2, 1))
        return t * cos + partner * sin

    for c in range(RET_ROWS // RET_CHUNK):
        rows = slice(c * RET_CHUNK, (c + 1) * RET_CHUNK)
        for hp in range(RET_HEADS // 2):
            qs = slice(hp * LANES, (hp + 1) * LANES)
            ks = slice(RET_QK_WIDTH + hp * LANES, RET_QK_WIDTH + (hp + 1) * LANES)
            cos, sin = cos_ref[rows, qs], sin_ref[rows, qs]
            q_pair = rotate(r_ref[rows, qs], cos, sin) * (RET_KEY_DIM ** -0.5)
            k_pair = rotate(r_ref[rows, ks], cos, sin)
            for hh in range(2):
                head = 2 * hp + hh
                vs = slice(2 * RET_QK_WIDTH + head * LANES, 2 * RET_QK_WIDTH + (head + 1) * LANES)
                gs = slice(2 * RET_QK_WIDTH + RET_WIDTH + head * LANES,
                           2 * RET_QK_WIDTH + RET_WIDTH + (head + 1) * LANES)
                keep = low if hh == 0 else jnp.logical_not(low)
                qm = jnp.where(keep, q_pair, 0.0)
                vb = r_ref[rows, vs].astype(BF16)
                state = state_ref[head]
                inner = _dot_nt(qm.astype(BF16), k_pair.astype(BF16)) * dmask_ref[head]
                y = _dot(inner.astype(BF16), vb)
                y = y + _dot((qm * qdec_ref[head]).astype(BF16), state.astype(BF16))
                state_ref[head] = state * cdec_ref[head] + _dot_tn((k_pair * kdec_ref[head]).astype(BF16), vb)
                mu = jnp.mean(y, axis=-1, keepdims=True)
                yc = y - mu
                var = jnp.mean(yc * yc, axis=-1, keepdims=True)
                yn = yc * lax.rsqrt(var + EPS) * gain_ref[:, head * LANES:(head + 1) * LANES]
                o_ref[rows, head * LANES:(head + 1) * LANES] = (_silu(r_ref[rows, gs]) * yn).astype(BF16)


def _retention_tables():
    half = RET_KEY_DIM // 2
    pos = jnp.arange(SEQ, dtype=F32)
    inv = ROPE_BASE ** (-jnp.arange(half, dtype=F32) / half)
    ang = pos[:, None] * inv[None, :]
    cos, sin = jnp.cos(ang), jnp.sin(ang)
    cos_full = jnp.tile(jnp.concatenate([cos, cos], axis=-1), (1, RET_HEADS))
    sin_signed = jnp.tile(jnp.concatenate([-sin, sin], axis=-1), (1, RET_HEADS))
    log_g = jnp.log(1.0 - 2.0 ** (-5.0 - jnp.arange(RET_HEADS, dtype=F32)))
    idx = jnp.arange(RET_CHUNK, dtype=F32)
    diff = idx[:, None] - idx[None, :]
    dmask = jnp.where(diff >= 0, jnp.exp(jnp.maximum(diff, 0.0)[None] * log_g[:, None, None]), 0.0)
    q_decay = jnp.exp((idx + 1.0)[None, :] * log_g[:, None])[..., None]
    k_decay = jnp.exp((RET_CHUNK - 1.0 - idx)[None, :] * log_g[:, None])[..., None]
    chunk_decay = jnp.exp(RET_CHUNK * log_g)[:, None, None]
    full = (RET_HEADS, RET_CHUNK, LANES)
    return (cos_full, sin_signed, dmask, jnp.broadcast_to(q_decay, full),
            jnp.broadcast_to(k_decay, full), jnp.broadcast_to(chunk_decay, full))


def _retention(ret_in, ret_gain):
    tokens = ret_in.shape[0]
    batch = tokens // SEQ
    per_seq = SEQ // RET_ROWS
    cos, sin, dmask, qdec, kdec, cdec = _retention_tables()
    tab = pl.BlockSpec((RET_ROWS, RET_QK_WIDTH), lambda b, j: (j, 0))
    const3 = pl.BlockSpec((RET_HEADS, RET_CHUNK, LANES), lambda b, j: (0, 0, 0))
    return pl.pallas_call(
        _retention_kernel,
        out_shape=jax.ShapeDtypeStruct((tokens, RET_WIDTH), BF16),
        grid=(batch, per_seq),
        in_specs=[
            pl.BlockSpec((RET_ROWS, RET_IN_WIDTH), lambda b, j: (b * per_seq + j, 0)),
            tab, tab, const3, const3, const3, const3,
            pl.BlockSpec((1, RET_WIDTH), lambda b, j: (0, 0)),
        ],
        out_specs=pl.BlockSpec((RET_ROWS, RET_WIDTH), lambda b, j: (b * per_seq + j, 0)),
        scratch_shapes=[pltpu.VMEM((RET_HEADS, LANES, RET_VALUE_DIM), F32)],
        compiler_params=_cparams(("arbitrary", "arbitrary")),
        name="retention",
    )(ret_in, cos, sin, dmask, qdec, kdec, cdec, ret_gain.reshape(1, RET_WIDTH))


def _swiglu_chunks(h, w1_ref, w3_ref, w2_ref):
    d_ff = w1_ref.shape[-1]
    total = None
    for c0 in range(0, d_ff, MXU_COLS):
        c1 = min(c0 + MXU_COLS, d_ff)
        z = (_silu(_dot(h, w1_ref[:, c0:c1].astype(BF16))) * _dot(h, w3_ref[:, c0:c1].astype(BF16))).astype(BF16)
        part = _dot(z, w2_ref[c0:c1, :].astype(BF16))
        total = part if total is None else total + part
    return total


def _out_proj_ffn_kernel(attn_ref, ret_ref, x_ref, mod_ref, wo_ref, gain_ref, w1_ref, w3_ref, w2_ref, out_ref):
    mod = mod_ref[...]
    mix = (_dot(attn_ref[...], wo_ref[:ATTN_WIDTH, :].astype(BF16))
           + _dot(ret_ref[...], wo_ref[ATTN_WIDTH:, :].astype(BF16)))
    x = x_ref[...] + mod[2:3, :] * mix
    h = _modulated_norm(x, gain_ref[...], mod[4:5], mod[3:4]).astype(BF16)
    out_ref[...] = x + mod[5:6, :] * _swiglu_chunks(h, w1_ref, w3_ref, w2_ref)


def _out_proj_dense_ffn(attn, ret, x, mod, w_out, layer, gain, w1, w3, w2, index):
    tokens = x.shape[0]
    d_ff = w1.shape[2]
    tm = TM_FFN
    per_seq = SEQ // tm
    row = lambda i: (i, 0)
    resident = pl.Buffered(1)
    return pl.pallas_call(
        _out_proj_ffn_kernel,
        out_shape=jax.ShapeDtypeStruct((tokens, D_MODEL), F32),
        grid=(tokens // tm,),
        in_specs=[
            pl.BlockSpec((tm, ATTN_WIDTH), row),
            pl.BlockSpec((tm, RET_WIDTH), row),
            pl.BlockSpec((tm, D_MODEL), row),
            pl.BlockSpec((None, 6, D_MODEL), lambda i: (i // per_seq, 0, 0)),
            pl.BlockSpec((None, D_MODEL, D_MODEL), lambda i: (layer, 0, 0), pipeline_mode=resident),
            pl.BlockSpec((1, D_MODEL), lambda i: (0, 0)),
            pl.BlockSpec((None, D_MODEL, d_ff), lambda i: (index, 0, 0), pipeline_mode=resident),
            pl.BlockSpec((None, D_MODEL, d_ff), lambda i: (index, 0, 0), pipeline_mode=resident),
            pl.BlockSpec((None, d_ff, D_MODEL), lambda i: (index, 0, 0), pipeline_mode=resident),
        ],
        out_specs=pl.BlockSpec((tm, D_MODEL), row),
        compiler_params=_cparams(("arbitrary",)),
        name="out_projection_dense_swiglu",
    )(attn, ret, x, mod, w_out, gain.reshape(1, D_MODEL), w1, w3, w2)


def _router_kernel(attn_ref, ret_ref, x_ref, mod_ref, wo_ref, gain_ref, wr_ref,
                   xo_ref, h_ref, gates_ref, pos_ref, post_ref, start_ref, cnt_ref, carry_ref, *, tiles_per_seq):
    i = pl.program_id(0)

    @pl.when(i % tiles_per_seq == 0)
    def _():
        carry_ref[...] = jnp.zeros_like(carry_ref)

    mod = mod_ref[...]
    mix = (_dot(attn_ref[...], wo_ref[:ATTN_WIDTH, :].astype(BF16))
           + _dot(ret_ref[...], wo_ref[ATTN_WIDTH:, :].astype(BF16)))
    x = x_ref[...] + mod[2:3, :] * mix
    xo_ref[...] = x
    h = _modulated_norm(x, gain_ref[...], mod[4:5], mod[3:4]).astype(BF16)
    h_ref[...] = h
    tm = h.shape[0]
    lane = lax.broadcasted_iota(jnp.int32, (tm, LANES), 1).astype(F32)
    logits = jnp.where(lane < N_EXPERTS, _dot(h, wr_ref[...]), -jnp.inf)
    m1 = jnp.max(logits, axis=-1, keepdims=True)
    i1 = jnp.min(jnp.where(logits == m1, lane, float(LANES)), axis=-1, keepdims=True)
    rest = jnp.where(lane == i1, -jnp.inf, logits)
    m2 = jnp.max(rest, axis=-1, keepdims=True)
    i2 = jnp.min(jnp.where(rest == m2, lane, float(LANES)), axis=-1, keepdims=True)
    e2 = jnp.exp(m2 - m1)
    g1 = 1.0 / (1.0 + e2)
    g2 = e2 / (1.0 + e2)
    gates_ref[...] = jnp.where(lane == i1, g1, 0.0) + jnp.where(lane == i2, g2, 0.0)
    chosen = (lane == i1) | (lane == i2)
    onehot = jnp.where(chosen, 1.0, 0.0)
    r = lax.broadcasted_iota(jnp.int32, (tm, tm), 0)
    c = lax.broadcasted_iota(jnp.int32, (tm, tm), 1)
    tril = jnp.where(c <= r, 1.0, 0.0).astype(BF16)
    incl = _dot(tril, onehot.astype(BF16))
    carry = carry_ref[0:1, :]
    start_ref[...] = carry_ref[...]
    pos = jnp.where(chosen, incl - 1.0 + carry, -1.0)
    pos_ref[...] = pos
    post_ref[...] = pos.T[:N_EXPERTS, :]
    total = carry + incl[tm - 1:tm, :]
    carry_ref[...] = jnp.broadcast_to(total, carry_ref.shape)
    cnt_ref[...] = jnp.broadcast_to(total, cnt_ref.shape)


def _out_proj_router(attn, ret, x, mod, w_out, layer, gain, w_router):
    tokens = x.shape[0]
    batch = tokens // SEQ
    tm = TM_PROJ
    per_seq = SEQ // tm
    wr = jnp.zeros((D_MODEL, LANES), BF16).at[:, :N_EXPERTS].set(w_router.astype(BF16))
    row = lambda i: (i, 0)
    return pl.pallas_call(
        functools.partial(_router_kernel, tiles_per_seq=per_seq),
        out_shape=(
            jax.ShapeDtypeStruct((tokens, D_MODEL), F32),
            jax.ShapeDtypeStruct((tokens, D_MODEL), BF16),
            jax.ShapeDtypeStruct((tokens, LANES), F32),
            jax.ShapeDtypeStruct((tokens, LANES), F32),
            jax.ShapeDtypeStruct((N_EXPERTS, tokens), F32),
            jax.ShapeDtypeStruct((tokens // tm, 8, LANES), F32),
            jax.ShapeDtypeStruct((batch, 8, LANES), F32),
        ),
        grid=(tokens // tm,),
        in_specs=[
            pl.BlockSpec((tm, ATTN_WIDTH), row),
            pl.BlockSpec((tm, RET_WIDTH), row),
            pl.BlockSpec((tm, D_MODEL), row),
            pl.BlockSpec((None, 6, D_MODEL), lambda i: (i // per_seq, 0, 0)),
            pl.BlockSpec((None, D_MODEL, D_MODEL), lambda i: (layer, 0, 0)),
            pl.BlockSpec((1, D_MODEL), lambda i: (0, 0)),
            pl.BlockSpec((D_MODEL, LANES), lambda i: (0, 0)),
        ],
        out_specs=(
            pl.BlockSpec((tm, D_MODEL), row),
            pl.BlockSpec((tm, D_MODEL), row),
            pl.BlockSpec((tm, LANES), row),
            pl.BlockSpec((tm, LANES), row),
            pl.BlockSpec((N_EXPERTS, tm), lambda i: (0, i)),
            pl.BlockSpec((None, 8, LANES), lambda i: (i, 0, 0)),
            pl.BlockSpec((None, 8, LANES), lambda i: (i // per_seq, 0, 0)),
        ),
        scratch_shapes=[pltpu.VMEM((8, LANES), F32)],
        compiler_params=_cparams(("arbitrary",)),
        name="out_projection_router",
    )(attn, ret, x, mod, w_out, gain.reshape(1, D_MODEL), wr)


TOK_BLOCKS = SEQ // TR_MOE
TAIL_TILES = (64, 128, TR_MOE)
SPILL_ROWS = 16
SMALL_WINDOW = 128
BIG_TILE = 1024
SEG_ALIGN = 16
GROUP_ROWS = 3072
PACKED_ROWS = 2 * SEQ + N_EXPERTS * SEG_ALIGN + 2 * TR_MOE
BIG_CHUNK = 128
STAGE_ROWS = SEQ + TR_MOE
N_BOUNDS = TOK_BLOCKS + 1


def _moe_plan(start, cnt):
    batch = cnt.shape[0]
    counts = cnt[:, 0, :N_EXPERTS].astype(jnp.int32)
    bounds = jnp.concatenate([start[:, 0, :N_EXPERTS].reshape(batch, TOK_BLOCKS, N_EXPERTS).astype(jnp.int32),
                              counts[:, None, :]], axis=1)
    seg = (counts + (SEG_ALIGN - 1)) // SEG_ALIGN * SEG_ALIGN
    rows_e = jnp.sum(seg, axis=0)
    groups_e = (rows_e + (GROUP_ROWS - 1)) // GROUP_ROWS
    first_group = jnp.cumsum(groups_e) - groups_e
    seg_off = first_group[None, :] * GROUP_ROWS + jnp.cumsum(seg, axis=0) - seg
    max_rows = batch * (2 * SEQ + N_EXPERTS * (SEG_ALIGN - 1))
    n_groups = max_rows // GROUP_ROWS + N_EXPERTS
    g = jnp.arange(n_groups)
    g_expert = jnp.minimum(jnp.sum(g[:, None] >= jnp.cumsum(groups_e)[None, :], axis=1), N_EXPERTS - 1)
    g_rows = jnp.clip(rows_e[g_expert] - (g - first_group[g_expert]) * GROUP_ROWS, 0, GROUP_ROWS)
    g_rows = jnp.where(g < jnp.sum(groups_e), g_rows, 0)
    return (bounds.reshape(-1), seg.reshape(-1), seg_off.reshape(-1),
            g_expert.astype(jnp.int32), g_rows.astype(jnp.int32), n_groups)


def _segment_copies(seg, make_copy, action):
    n_big = lax.shift_right_logical(seg, BIG_CHUNK.bit_length() - 1)

    def big(i, carry):
        getattr(make_copy(pl.multiple_of(i * BIG_CHUNK, BIG_CHUNK), BIG_CHUNK), action)()
        return carry
    lax.fori_loop(0, n_big, big, 0)
    rest0 = n_big * BIG_CHUNK
    n_small = lax.shift_right_logical(seg - rest0, SEG_ALIGN.bit_length() - 1)

    def small(i, carry):
        getattr(make_copy(pl.multiple_of(rest0 + i * SEG_ALIGN, SEG_ALIGN), SEG_ALIGN), action)()
        return carry
    lax.fori_loop(0, n_small, small, 0)


def _block_windows(tbl_ref, b, e, align):
    base = b * N_BOUNDS * N_EXPERTS + e
    bounds = [tbl_ref[base + tb * N_EXPERTS] for tb in range(N_BOUNDS)]
    shift = align.bit_length() - 1
    win = [pl.multiple_of(lax.shift_left(lax.shift_right_logical(s, shift), shift), align) for s in bounds[:-1]]
    all_small = functools.reduce(jnp.logical_and,
                                 [bounds[tb + 1] - win[tb] <= SMALL_WINDOW for tb in range(TOK_BLOCKS)])
    return bounds, win, all_small


def _dispatch_kernel(tbl_ref, seg_ref, off_ref, h_ref, post_ref, rows_hbm, acc_ref, stage_ref, sem):
    b = pl.program_id(0)
    tr = TR_MOE

    def copies(e, action):
        slot = e % 2
        off = off_ref[b * N_EXPERTS + e]

        def make_copy(r0, rows):
            return pltpu.make_async_copy(stage_ref.at[slot, pl.ds(r0, rows), :],
                                         rows_hbm.at[pl.ds(pl.multiple_of(off + r0, SEG_ALIGN), rows), :],
                                         sem.at[slot])
        _segment_copies(seg_ref[b * N_EXPERTS + e], make_copy, action)

    def expert(e, carry):
        bounds, win, all_small = _block_windows(tbl_ref, b, e, SUBLANES)
        n_tiles = lax.shift_right_logical(bounds[-1] + (tr - 1), tr.bit_length() - 1)

        def clear(r, c):
            acc_ref[pl.ds(pl.multiple_of(r * tr, tr), tr), :] = jnp.zeros((tr, D_MODEL), F32)
            return c
        lax.fori_loop(0, n_tiles + 2, clear, 0)

        def gather(height):
            slot_id = lax.broadcasted_iota(jnp.int32, (height, tr), 0).astype(F32)
            for tb in range(TOK_BLOCKS):
                ts = slice(tb * tr, (tb + 1) * tr)
                local = post_ref[pl.ds(e, 1), ts] - win[tb].astype(F32)
                sel = jnp.where(local == slot_id, 1.0, 0.0).astype(BF16)
                acc_ref[pl.ds(win[tb], height), :] += _dot(sel, h_ref[ts, :])

        @pl.when(all_small)
        def _():
            gather(SMALL_WINDOW)

        @pl.when(jnp.logical_not(all_small))
        def _():
            gather(tr + SPILL_ROWS)

        @pl.when(e >= 2)
        def _():
            copies(e - 2, "wait")

        def to_stage(r, c):
            rows = pl.ds(pl.multiple_of(r * tr, tr), tr)
            stage_ref[e % 2, rows, :] = acc_ref[rows, :].astype(BF16)
            return c
        lax.fori_loop(0, n_tiles, to_stage, 0)
        copies(e, "start")
        return carry
    lax.fori_loop(0, N_EXPERTS, expert, 0)
    copies(N_EXPERTS - 2, "wait")
    copies(N_EXPERTS - 1, "wait")


def _dispatch(h, post, bounds, seg, seg_off, total_rows):
    tokens = h.shape[0]
    grid_spec = pltpu.PrefetchScalarGridSpec(
        num_scalar_prefetch=3,
        grid=(tokens // SEQ,),
        in_specs=[pl.BlockSpec((SEQ, D_MODEL), lambda b, *_: (b, 0)),
                  pl.BlockSpec((N_EXPERTS, SEQ), lambda b, *_: (0, b))],
        out_specs=pl.BlockSpec(memory_space=pl.ANY),
        scratch_shapes=[pltpu.VMEM((SEQ + 3 * TR_MOE, D_MODEL), F32),
                        pltpu.VMEM((2, STAGE_ROWS, D_MODEL), BF16),
                        pltpu.SemaphoreType.DMA((2,))],
    )
    return pl.pallas_call(
        _dispatch_kernel,
        out_shape=jax.ShapeDtypeStruct((total_rows, D_MODEL), BF16),
        grid_spec=grid_spec,
        compiler_params=_cparams(("arbitrary",)),
        name="expert_dispatch",
    )(bounds, seg, seg_off, h, post)


def _experts_kernel(ge_ref, rows_ref, hs_ref, w1_ref, w3_ref, w2_ref, ys_ref, acc_ref, wb1_ref, wb3_ref, wb2_ref):
    g, f = pl.program_id(0), pl.program_id(1)
    last_f = pl.num_programs(1) - 1
    tr = TR_MOE
    rows = rows_ref[g]
    n_big = lax.shift_right_logical(rows, BIG_TILE.bit_length() - 1)
    n_full = lax.shift_right_logical(rows, tr.bit_length() - 1)
    rest = rows - n_full * tr
    n_tiles = lax.shift_right_logical(rows + (tr - 1), tr.bit_length() - 1)
    tail0 = pl.multiple_of(n_full * tr, tr)

    @pl.when(rows > 0)
    def _():
        wb1_ref[...] = w1_ref[...].astype(BF16)
        wb3_ref[...] = w3_ref[...].astype(BF16)
        wb2_ref[...] = w2_ref[...].astype(BF16)

    @pl.when(f == 0)
    def _():
        def clear(r, carry):
            acc_ref[pl.ds(pl.multiple_of(r * tr, tr), tr), :] = jnp.zeros((tr, D_MODEL), F32)
            return carry
        lax.fori_loop(0, n_tiles, clear, 0)

    def swiglu(hr, r0, m):
        if m > tr:
            acc_ref[pl.ds(r0, m), :] += _swiglu_chunks(hr, wb1_ref, wb3_ref, wb2_ref)
        else:
            z = (_silu(_dot(hr, wb1_ref[...])) * _dot(hr, wb3_ref[...])).astype(BF16)
            acc_ref[pl.ds(r0, m), :] += _dot(z, wb2_ref[...])

    def big_tile(r, carry):
        r0 = pl.multiple_of(r * BIG_TILE, BIG_TILE)
        swiglu(hs_ref[pl.ds(r0, BIG_TILE), :], r0, BIG_TILE)
        return carry
    lax.fori_loop(0, n_big, big_tile, 0)

    done = n_big * BIG_TILE
    m = BIG_TILE // 2
    while m >= tr:
        take = (rows - done) >= m

        @pl.when(take)
        def _(done=done, m=m):
            r0 = pl.multiple_of(done, tr)
            swiglu(hs_ref[pl.ds(r0, m), :], r0, m)
        done = done + jnp.where(take, m, 0)
        m //= 2
    lo = 0
    for m in TAIL_TILES:
        @pl.when((rest > lo) & (rest <= m))
        def _():
            valid = lax.broadcasted_iota(jnp.int32, (m, D_MODEL), 0) < rest
            hr = hs_ref[pl.ds(tail0, m), :]
            swiglu(jnp.where(valid, hr, jnp.zeros_like(hr)), tail0, m)
        lo = m

    @pl.when(f == last_f)
    def _():
        ys_ref[...] = jnp.zeros_like(ys_ref)

        def store(r, carry):
            rr = pl.ds(pl.multiple_of(r * tr, tr), tr)
            ys_ref[rr, :] = acc_ref[rr, :].astype(BF16)
            return carry
        lax.fori_loop(0, n_tiles, store, 0)


def _experts(hs, g_expert, g_rows, n_groups, w1, w3, w2):
    d_ff = w1.shape[2]
    tf = TF_MOE
    n_f = d_ff // tf
    chunk = lambda g, f, gr: jnp.where(gr[g] > 0, f, n_f - 1)
    grid_spec = pltpu.PrefetchScalarGridSpec(
        num_scalar_prefetch=2,
        grid=(n_groups, n_f),
        in_specs=[
            pl.BlockSpec((GROUP_ROWS, D_MODEL), lambda g, f, ge, gr: (g, 0)),
            pl.BlockSpec((None, D_MODEL, tf), lambda g, f, ge, gr: (ge[g], 0, chunk(g, f, gr))),
            pl.BlockSpec((None, D_MODEL, tf), lambda g, f, ge, gr: (ge[g], 0, chunk(g, f, gr))),
            pl.BlockSpec((None, tf, D_MODEL), lambda g, f, ge, gr: (ge[g], chunk(g, f, gr), 0)),
        ],
        out_specs=pl.BlockSpec((GROUP_ROWS, D_MODEL), lambda g, f, ge, gr: (g, 0)),
        scratch_shapes=[pltpu.VMEM((GROUP_ROWS, D_MODEL), F32),
                        pltpu.VMEM((D_MODEL, tf), BF16), pltpu.VMEM((D_MODEL, tf), BF16),
                        pltpu.VMEM((tf, D_MODEL), BF16)],
    )
    return pl.pallas_call(
        _experts_kernel,
        out_shape=jax.ShapeDtypeStruct(hs.shape, BF16),
        grid_spec=grid_spec,
        compiler_params=_cparams(("arbitrary", "arbitrary")),
        name="expert_swiglu",
    )(g_expert, g_rows, hs, w1, w3, w2)


def _combine_kernel(tbl_ref, seg_ref, off_ref, rows_hbm, pos_ref, gates_ref, x_ref, mod_ref, out_ref, buf_ref, sem):
    b = pl.program_id(0)
    n_seq = pl.num_programs(0)
    tr = TR_MOE

    def all_segments(seq, action):
        def body(e, first):
            off = off_ref[seq * N_EXPERTS + e]
            seg = seg_ref[seq * N_EXPERTS + e]

            def make_copy(r0, rows):
                return pltpu.make_async_copy(
                    rows_hbm.at[pl.ds(pl.multiple_of(off + r0, SEG_ALIGN), rows), :],
                    buf_ref.at[seq % 2, pl.ds(pl.multiple_of(first + r0, SEG_ALIGN), rows), :],
                    sem.at[seq % 2, e])
            _segment_copies(seg, make_copy, action)
            return first + seg
        lax.fori_loop(0, N_EXPERTS, body, 0)

    @pl.when(b == 0)
    def _():
        buf_ref[...] = jnp.zeros_like(buf_ref)
        all_segments(b, "start")

    @pl.when(b + 1 < n_seq)
    def _():
        all_segments(b + 1, "start")

    out_ref[...] = x_ref[...]
    lane = lax.broadcasted_iota(jnp.int32, (tr, LANES), 1)
    layer_gate = mod_ref[5:6, :]
    all_segments(b, "wait")

    def expert(e, seg_start):
        bounds, win, all_small = _block_windows(tbl_ref, b, e, SEG_ALIGN)

        def scatter(tb, first, height):
            ts = slice(tb * tr, (tb + 1) * tr)
            pos_col = jnp.sum(jnp.where(lane == e, pos_ref[ts, :], 0.0), axis=-1, keepdims=True)
            gate_col = jnp.sum(jnp.where(lane == e, gates_ref[ts, :], 0.0), axis=-1, keepdims=True)
            slot_id = lax.broadcasted_iota(jnp.int32, (tr, height), 1).astype(F32) + float(first)
            sel = jnp.where(pos_col - win[tb].astype(F32) == slot_id, 1.0, 0.0).astype(BF16)
            rows = buf_ref[b % 2, pl.ds(pl.multiple_of(seg_start + win[tb] + first, SEG_ALIGN), height), :]
            out_ref[ts, :] += layer_gate * (gate_col * _dot(sel, rows))

        @pl.when(all_small)
        def _():
            for tb in range(TOK_BLOCKS):
                scatter(tb, 0, SMALL_WINDOW)

        @pl.when(jnp.logical_not(all_small))
        def _():
            for tb in range(TOK_BLOCKS):
                scatter(tb, 0, tr)
            for tb in range(TOK_BLOCKS):
                @pl.when(bounds[tb + 1] - win[tb] > tr)
                def _():
                    scatter(tb, tr, SPILL_ROWS)
        return seg_start + seg_ref[b * N_EXPERTS + e]
    lax.fori_loop(0, N_EXPERTS, expert, 0)


def _combine(ys, pos, gates, x, mod, bounds, seg, seg_off):
    tokens = x.shape[0]
    grid_spec = pltpu.PrefetchScalarGridSpec(
        num_scalar_prefetch=3,
        grid=(tokens // SEQ,),
        in_specs=[pl.BlockSpec(memory_space=pl.ANY),
                  pl.BlockSpec((SEQ, LANES), lambda b, *_: (b, 0)),
                  pl.BlockSpec((SEQ, LANES), lambda b, *_: (b, 0)),
                  pl.BlockSpec((SEQ, D_MODEL), lambda b, *_: (b, 0)),
                  pl.BlockSpec((None, 6, D_MODEL), lambda b, *_: (b, 0, 0))],
        out_specs=pl.BlockSpec((SEQ, D_MODEL), lambda b, *_: (b, 0)),
        scratch_shapes=[pltpu.VMEM((2, PACKED_ROWS, D_MODEL), BF16),
                        pltpu.SemaphoreType.DMA((2, N_EXPERTS))],
    )
    return pl.pallas_call(
        _combine_kernel,
        out_shape=jax.ShapeDtypeStruct((tokens, D_MODEL), F32),
        grid_spec=grid_spec,
        compiler_params=_cparams(("arbitrary",)),
        name="expert_combine",
    )(bounds, seg, seg_off, ys, pos, gates, x, mod)


def _token_mixer(x, mod, gain, w_in, layer, q_gain, k_gain, ret_gain, bias_masks):
    *qkv, ret_in = _in_proj(x, mod, gain, w_in, layer, q_gain, k_gain)
    return _dilated_attention(qkv, bias_masks), _retention(ret_in, ret_gain)


def _out_proj_moe_ffn(attn, ret, x, mod, w_out, layer, gain, w_router, w1, w3, w2):
    x, h, gates, pos, post, start, cnt = _out_proj_router(attn, ret, x, mod, w_out, layer, gain, w_router)
    bounds, seg, seg_off, g_expert, g_rows, n_groups = _moe_plan(start, cnt)
    hs = _dispatch(h, post, bounds, seg, seg_off, n_groups * GROUP_ROWS)
    ys = _experts(hs, g_expert, g_rows, n_groups, w1, w3, w2)
    return _combine(ys, pos, gates, x, mod, bounds, seg, seg_off)


def kernel(x, c, rel_bias_table, norm_mix, norm_ffn, w_mod, b_mod, w_in, q_gain, k_gain, ret_gain, w_out,
           ffn_w1, ffn_w3, ffn_w2, moe_router, moe_w1, moe_w3, moe_w2):
    batch, seq, d_model = x.shape
    assert (seq, d_model) == (SEQ, D_MODEL)
    depth = w_mod.shape[0]
    mods = _modulation(c, w_mod, b_mod).reshape(depth, batch, 6, D_MODEL)
    bias_masks = _bias_masks(rel_bias_table)
    xt = x.reshape(batch * seq, d_model)
    for layer in range(depth):
        mod = mods[layer]
        attn, ret = _token_mixer(xt, mod, norm_mix[layer], w_in, layer, q_gain[layer], k_gain[layer],
                                 ret_gain[layer], bias_masks)
        i = layer // 2
        if layer % 2 == 0:
            xt = _out_proj_dense_ffn(attn, ret, xt, mod, w_out, layer, norm_ffn[layer],
                                     ffn_w1, ffn_w3, ffn_w2, i)
        else:
            xt = _out_proj_moe_ffn(attn, ret, xt, mod, w_out, layer, norm_ffn[layer],
                                   moe_router[i], moe_w1[i], moe_w3[i], moe_w2[i])
    return xt.reshape(batch, seq, d_model)
```

```python
import functools
import math

import jax
import jax.numpy as jnp
import numpy as np
from jax import lax
from jax.experimental import pallas as pl
from jax.experimental.pallas import tpu as pltpu

D_MODEL = 1024
SEQ = 2048
ATTN_HEADS = 8
ATTN_HEAD_DIM = 64
ATTN_WIDTH = ATTN_HEADS * ATTN_HEAD_DIM
DILATED_PATTERNS = ((128, 1), (512, 4), (2048, 16))
BLOCK = 128
NUM_BUCKETS = 32
MAX_DISTANCE = 2048
RET_HEADS = 4
RET_KEY_DIM = 64
RET_VALUE_DIM = 128
RET_WIDTH = RET_HEADS * RET_VALUE_DIM
RET_QK_WIDTH = RET_HEADS * RET_KEY_DIM
RET_CHUNK = 128
ROPE_BASE = 10000.0
IN_WIDTH = 3 * ATTN_WIDTH + 2 * RET_QK_WIDTH + 2 * RET_WIDTH
RET_IN_WIDTH = IN_WIDTH - 3 * ATTN_WIDTH
N_EXPERTS = 8
EPS = 1e-6
NEG_INF = -1e30

LANES = 128
VMEM_LIMIT = 60 * 1024 * 1024

BF16 = jnp.bfloat16
F32 = jnp.float32

TM_PROJ = 256
TM_FFN = 512
MXU_COLS = 256
TR_MOE = 256
TF_MOE = 512
assert TM_PROJ == TR_MOE
RET_ROWS = 512
SUBLANES = 8


def _cparams(sem):
    return pltpu.CompilerParams(dimension_semantics=sem, vmem_limit_bytes=VMEM_LIMIT)


def _dot(a, b):
    return jnp.dot(a, b, preferred_element_type=F32)


def _dot_nt(a, b):
    return lax.dot_general(a, b, (((1,), (1,)), ((), ())), preferred_element_type=F32)


def _dot_tn(a, b):
    return lax.dot_general(a, b, (((0,), (0,)), ((), ())), preferred_element_type=F32)


def _split_bf16(v):
    hi = v.astype(BF16)
    lo = (v - hi.astype(F32)).astype(BF16)
    return hi, lo


def _silu(v):
    return v * (1.0 / (1.0 + jnp.exp(-v)))


def _modulated_norm(x, gain, scale, shift):
    ms = jnp.mean(x * x, axis=-1, keepdims=True)
    y = x * lax.rsqrt(ms + EPS) * gain
    return y * (1.0 + scale) + shift


def _mod_kernel(c_ref, w_ref, b_ref, o_ref):
    ca = _silu(c_ref[...]).astype(BF16)
    o_ref[...] = _dot(ca, w_ref[...].astype(BF16)) + b_ref[...]


def _modulation(c, w_mod, b_mod):
    depth, _, width = w_mod.shape
    batch = c.shape[0]
    tn = 1536
    return pl.pallas_call(
        _mod_kernel,
        out_shape=jax.ShapeDtypeStruct((depth, batch, width), F32),
        grid=(depth, width // tn),
        in_specs=[
            pl.BlockSpec((batch, D_MODEL), lambda l, n: (0, 0)),
            pl.BlockSpec((None, D_MODEL, tn), lambda l, n: (l, 0, n)),
            pl.BlockSpec((None, 1, tn), lambda l, n: (l, 0, n)),
        ],
        out_specs=pl.BlockSpec((None, batch, tn), lambda l, n: (l, 0, n)),
        compiler_params=_cparams(("arbitrary", "arbitrary")),
        name="adaln_modulation",
    )(c, w_mod, b_mod.reshape(depth, 1, width))


def _bias_kernel(table_ref, bucket_ref, o_ref):
    bucket = bucket_ref[...]
    acc = [jnp.full(bucket.shape, NEG_INF, F32) for _ in range(ATTN_HEADS)]
    for b in range(NUM_BUCKETS):
        hit = bucket == b
        for h in range(ATTN_HEADS):
            acc[h] = jnp.where(hit, table_ref[b, h], acc[h])
    for h in range(ATTN_HEADS):
        o_ref[h // 2, (h % 2) * BLOCK:(h % 2 + 1) * BLOCK, :] = acc[h]


BIAS_FULL = {1: 0, 4: 2}
BIAS_FIRST = {1: 1, 4: 3, 16: 4}
N_BIAS_SETS = 5


def _bias_masks(rel_bias_table):
    i = jnp.arange(BLOCK)[:, None]
    j = jnp.arange(2 * BLOCK)[None, :]
    max_exact = NUM_BUCKETS // 2

    def bucket_of(rel, dilation, w_sub, exists):
        n = jnp.maximum(rel * dilation, 0)
        nf = jnp.maximum(n.astype(F32), float(max_exact))
        large = max_exact + (jnp.log(nf / max_exact) / math.log(MAX_DISTANCE / max_exact)
                             * (NUM_BUCKETS - max_exact)).astype(jnp.int32)
        large = jnp.minimum(large, NUM_BUCKETS - 1)
        bucket = jnp.where(n < max_exact, n, large)
        allowed = (rel >= 0) & (rel <= w_sub) & exists
        return jnp.where(allowed, bucket, -1)

    sets = [None] * N_BIAS_SETS
    for window, dilation in DILATED_PATTERNS:
        w_sub = window // dilation
        if dilation in BIAS_FULL:
            sets[BIAS_FULL[dilation]] = bucket_of(i - j + BLOCK, dilation, w_sub, j >= 0)
        sets[BIAS_FIRST[dilation]] = bucket_of(i - j, dilation, w_sub, j < BLOCK)
    buckets = jnp.stack(sets).astype(jnp.int32)
    return pl.pallas_call(
        _bias_kernel,
        out_shape=jax.ShapeDtypeStruct((N_BIAS_SETS, ATTN_HEADS // 2, 2 * BLOCK, 2 * BLOCK), F32),
        grid=(N_BIAS_SETS,),
        in_specs=[
            pl.BlockSpec(memory_space=pltpu.SMEM),
            pl.BlockSpec((None, BLOCK, 2 * BLOCK), lambda p: (p, 0, 0)),
        ],
        out_specs=pl.BlockSpec((None, ATTN_HEADS // 2, 2 * BLOCK, 2 * BLOCK), lambda p: (p, 0, 0, 0)),
        compiler_params=_cparams(("arbitrary",)),
        name="relative_bias_masks",
    )(rel_bias_table, buckets)


HALF = ATTN_WIDTH // 2


def _in_proj_kernel(x_ref, mod_ref, gain_ref, w_ref, qg_ref, kg_ref, grp_ref,
                    q1_ref, k1_ref, v1_ref, q4_ref, k4_ref, v4_ref, q16_ref, k16_ref, v16_ref, r_ref,
                    perm_ref):
    mod = mod_ref[...]
    h = _modulated_norm(x_ref[...], gain_ref[...], mod[1:2], mod[0:1]).astype(BF16)
    proj = _dot(h, w_ref[...].astype(BF16))
    grp = grp_ref[...]
    tm = proj.shape[0]

    def head_norm(t, gain):
        hi, lo = _split_bf16(t * t)
        ss = _dot(hi, grp) + _dot(lo, grp)
        return t * lax.rsqrt(ss * (1.0 / ATTN_HEAD_DIM) + EPS) * gain

    def emit(t, o1_ref, o4_ref, o16_ref):
        o1_ref[...] = t.astype(BF16)
        for j in range(ATTN_WIDTH // LANES):
            perm_ref[j] = t[:, j * LANES:(j + 1) * LANES]
        for dil, o_ref in ((4, o4_ref), (16, o16_ref)):
            for hf in range(2):
                for r in range(dil):
                    for jj in range(HALF // LANES):
                        c0 = (hf * dil + r) * HALF + jj * LANES
                        o_ref[:, c0:c0 + LANES] = perm_ref[hf * (HALF // LANES) + jj,
                                                           pl.ds(r, tm // dil, stride=dil), :].astype(BF16)

    emit(head_norm(proj[:, :ATTN_WIDTH], qg_ref[...]) * (ATTN_HEAD_DIM ** -0.5), q1_ref, q4_ref, q16_ref)
    emit(head_norm(proj[:, ATTN_WIDTH:2 * ATTN_WIDTH], kg_ref[...]), k1_ref, k4_ref, k16_ref)
    emit(proj[:, 2 * ATTN_WIDTH:3 * ATTN_WIDTH], v1_ref, v4_ref, v16_ref)
    r_ref[...] = proj[:, 3 * ATTN_WIDTH:]


def _in_proj(x, mod, gain, w_in, layer, q_gain, k_gain):
    tokens = x.shape[0]
    tm = TM_PROJ
    per_seq = SEQ // tm
    grp = np.kron(np.eye(ATTN_HEADS), np.ones((ATTN_HEAD_DIM, ATTN_HEAD_DIM))).astype(np.float32)
    row = lambda i: (i, 0)
    const = lambda i: (0, 0)
    layouts = []
    for dil in (1, 4, 16):
        shape = jax.ShapeDtypeStruct((tokens // dil, dil * ATTN_WIDTH), BF16)
        spec = pl.BlockSpec((tm // dil, dil * ATTN_WIDTH), row)
        layouts.append(((shape,) * 3, (spec,) * 3))
    out_shape = sum((s for s, _ in layouts), ()) + (jax.ShapeDtypeStruct((tokens, RET_IN_WIDTH), F32),)
    out_specs = sum((s for _, s in layouts), ()) + (pl.BlockSpec((tm, RET_IN_WIDTH), row),)
    return pl.pallas_call(
        _in_proj_kernel,
        out_shape=out_shape,
        grid=(tokens // tm,),
        in_specs=[
            pl.BlockSpec((tm, D_MODEL), row),
            pl.BlockSpec((None, 6, D_MODEL), lambda i: (i // per_seq, 0, 0)),
            pl.BlockSpec((1, D_MODEL), const),
            pl.BlockSpec((None, D_MODEL, IN_WIDTH), lambda i: (layer, 0, 0), pipeline_mode=pl.Buffered(1)),
            pl.BlockSpec((1, ATTN_WIDTH), const),
            pl.BlockSpec((1, ATTN_WIDTH), const),
            pl.BlockSpec((ATTN_WIDTH, ATTN_WIDTH), const),
        ],
        out_specs=out_specs,
        scratch_shapes=[pltpu.VMEM((ATTN_WIDTH // LANES, tm, LANES), F32)],
        compiler_params=_cparams(("arbitrary",)),
        name="in_projection",
    )(x, mod, gain.reshape(1, D_MODEL), w_in,
      jnp.tile(q_gain, ATTN_HEADS).reshape(1, ATTN_WIDTH),
      jnp.tile(k_gain, ATTN_HEADS).reshape(1, ATTN_WIDTH),
      jnp.asarray(grp, BF16))


PAIRS_PER_HALF = ATTN_HEADS // 4
GROUP = 16


def _pair_scores(qp, kp, vp, bias2, masks, low):
    q2 = jnp.concatenate([qp * masks[0], qp * masks[1]], axis=0)
    s = _dot_nt(q2, kp) + bias2
    m = jnp.max(s, axis=-1, keepdims=True)
    p = jnp.exp(s - m)
    den = jnp.sum(p, axis=-1, keepdims=True)
    pv = _dot(p.astype(BF16), vp)
    pick = lambda t: jnp.where(low, t[:BLOCK], t[BLOCK:])
    return pick(pv), pick(m), pick(den)


def _attn_kernel(q1_ref, k1_ref, v1_ref, q4_ref, k4_ref, v4_ref, q16_ref, k16_ref, v16_ref, bm_ref,
                 o_ref, acc_ref, max_ref, den_ref):
    lane = lax.broadcasted_iota(jnp.int32, (BLOCK, LANES), 1)
    low = lane < ATTN_HEAD_DIM
    masks = (jnp.where(low, 1.0, 0.0).astype(BF16), jnp.where(low, 0.0, 1.0).astype(BF16))

    def block(q_ref, k_ref, v_ref, c0, q0, w0, width, bias_set, rows, first):
        for p in range(PAIRS_PER_HALF):
            cs = slice(c0 + p * LANES, c0 + (p + 1) * LANES)
            acc, m, den = _pair_scores(q_ref[pl.ds(q0, BLOCK), cs], k_ref[pl.ds(w0, width), cs],
                                       v_ref[pl.ds(w0, width), cs], bm_ref[bias_set, p, :, 0:width],
                                       masks, low)
            if not first:
                m_old = max_ref[p, rows, :]
                m_new = jnp.maximum(m_old, m)
                a, b = jnp.exp(m_old - m_new), jnp.exp(m - m_new)
                den = den_ref[p, rows, :] * a + den * b
                acc = acc_ref[p, rows, :] * a + acc * b
                m = m_new
            max_ref[p, rows, :] = m
            den_ref[p, rows, :] = den
            acc_ref[p, rows, :] = acc

    def d1_group(g, carry):
        for u in range(GROUP):
            n = g * GROUP + u
            q0 = pl.multiple_of(n * BLOCK, BLOCK)
            w0 = pl.multiple_of(jnp.maximum(n - 1, 0) * BLOCK, BLOCK)
            bias_set = jnp.where(n == 0, BIAS_FIRST[1], BIAS_FULL[1])
            block(q1_ref, k1_ref, v1_ref, 0, q0, w0, 2 * BLOCK, bias_set, pl.ds(q0, BLOCK), True)
        return carry
    lax.fori_loop(0, SEQ // BLOCK // GROUP, d1_group, 0)

    for r in range(4):
        for n in range(SEQ // 4 // BLOCK):
            block(q4_ref, k4_ref, v4_ref, r * HALF, n * BLOCK, max(n - 1, 0) * BLOCK, 2 * BLOCK,
                  BIAS_FIRST[4] if n == 0 else BIAS_FULL[4],
                  pl.ds(r + 4 * BLOCK * n, BLOCK, stride=4), False)

    for r in range(16):
        block(q16_ref, k16_ref, v16_ref, r * HALF, 0, 0, BLOCK, BIAS_FIRST[16],
              pl.ds(r, BLOCK, stride=16), False)

    for n in range(SEQ // BLOCK):
        rows = slice(n * BLOCK, (n + 1) * BLOCK)
        for p in range(PAIRS_PER_HALF):
            o_ref[rows, p * LANES:(p + 1) * LANES] = (acc_ref[p, rows, :] / den_ref[p, rows, :]).astype(BF16)


def _dilated_attention(qkv, bias_masks):
    tokens = qkv[0].shape[0]
    batch = tokens // SEQ
    specs = []
    for dil in (1, 4, 16):
        specs += [pl.BlockSpec((SEQ // dil, dil * HALF), lambda b, hf: (b, hf))] * 3
    state = pltpu.VMEM((PAIRS_PER_HALF, SEQ, LANES), F32)
    return pl.pallas_call(
        _attn_kernel,
        out_shape=jax.ShapeDtypeStruct((tokens, ATTN_WIDTH), BF16),
        grid=(batch, 2),
        in_specs=specs + [pl.BlockSpec((N_BIAS_SETS, PAIRS_PER_HALF, 2 * BLOCK, 2 * BLOCK),
                                       lambda b, hf: (0, hf, 0, 0))],
        out_specs=pl.BlockSpec((SEQ, HALF), lambda b, hf: (b, hf)),
        scratch_shapes=[state, state, state],
        compiler_params=_cparams(("arbitrary", "arbitrary")),
        name="dilated_attention",
    )(*qkv, bias_masks)


def _retention_kernel(r_ref, cos_ref, sin_ref, dmask_ref, qdec_ref, kdec_ref, cdec_ref, gain_ref,
                      o_ref, state_ref):
    @pl.when(pl.program_id(1) == 0)
    def _():
        state_ref[...] = jnp.zeros_like(state_ref)

    lane = lax.broadcasted_iota(jnp.int32, (RET_CHUNK, LANES), 1)
    low = lane < RET_KEY_DIM
    first_half = (lane % RET_KEY_DIM) < (RET_KEY_DIM // 2)

    def rotate(t, cos, sin):
        partner = jnp.where(first_half, pltpu.roll(t, LANES - RET_KEY_DIM // 2, 1),
                            pltpu.roll(t, RET_KEY_DIM // 2, 1))
        return t * cos + partner * sin

    for c in range(RET_ROWS // RET_CHUNK):
        rows = slice(c * RET_CHUNK, (c + 1) * RET_CHUNK)
        for hp in range(RET_HEADS // 2):
            qs = slice(hp * LANES, (hp + 1) * LANES)
            ks = slice(RET_QK_WIDTH + hp * LANES, RET_QK_WIDTH + (hp + 1) * LANES)
            cos, sin = cos_ref[rows, qs], sin_ref[rows, qs]
            q_pair = rotate(r_ref[rows, qs], cos, sin) * (RET_KEY_DIM ** -0.5)
            k_pair = rotate(r_ref[rows, ks], cos, sin)
            for hh in range(2):
                head = 2 * hp + hh
                vs = slice(2 * RET_QK_WIDTH + head * LANES, 2 * RET_QK_WIDTH + (head + 1) * LANES)
                gs = slice(2 * RET_QK_WIDTH + RET_WIDTH + head * LANES,
                           2 * RET_QK_WIDTH + RET_WIDTH + (head + 1) * LANES)
                keep = low if hh == 0 else jnp.logical_not(low)
                qm = jnp.where(keep, q_pair, 0.0)
                vb = r_ref[rows, vs].astype(BF16)
                state = state_ref[head]
                inner = _dot_nt(qm.astype(BF16), k_pair.astype(BF16)) * dmask_ref[head]
                y = _dot(inner.astype(BF16), vb)
                y = y + _dot((qm * qdec_ref[head]).astype(BF16), state.astype(BF16))
                state_ref[head] = state * cdec_ref[head] + _dot_tn((k_pair * kdec_ref[head]).astype(BF16), vb)
                mu = jnp.mean(y, axis=-1, keepdims=True)
                yc = y - mu
                var = jnp.mean(yc * yc, axis=-1, keepdims=True)
                yn = yc * lax.rsqrt(var + EPS) * gain_ref[:, head * LANES:(head + 1) * LANES]
                o_ref[rows, head * LANES:(head + 1) * LANES] = (_silu(r_ref[rows, gs]) * yn).astype(BF16)


def _retention_tables():
    half = RET_KEY_DIM // 2
    pos = jnp.arange(SEQ, dtype=F32)
    inv = ROPE_BASE ** (-jnp.arange(half, dtype=F32) / half)
    ang = pos[:, None] * inv[None, :]
    cos, sin = jnp.cos(ang), jnp.sin(ang)
    cos_full = jnp.tile(jnp.concatenate([cos, cos], axis=-1), (1, RET_HEADS))
    sin_signed = jnp.tile(jnp.concatenate([-sin, sin], axis=-1), (1, RET_HEADS))
    log_g = jnp.log(1.0 - 2.0 ** (-5.0 - jnp.arange(RET_HEADS, dtype=F32)))
    idx = jnp.arange(RET_CHUNK, dtype=F32)
    diff = idx[:, None] - idx[None, :]
    dmask = jnp.where(diff >= 0, jnp.exp(jnp.maximum(diff, 0.0)[None] * log_g[:, None, None]), 0.0)
    q_decay = jnp.exp((idx + 1.0)[None, :] * log_g[:, None])[..., None]
    k_decay = jnp.exp((RET_CHUNK - 1.0 - idx)[None, :] * log_g[:, None])[..., None]
    chunk_decay = jnp.exp(RET_CHUNK * log_g)[:, None, None]
    full = (RET_HEADS, RET_CHUNK, LANES)
    return (cos_full, sin_signed, dmask, jnp.broadcast_to(q_decay, full),
            jnp.broadcast_to(k_decay, full), jnp.broadcast_to(chunk_decay, full))


def _retention(ret_in, ret_gain):
    tokens = ret_in.shape[0]
    batch = tokens // SEQ
    per_seq = SEQ // RET_ROWS
    cos, sin, dmask, qdec, kdec, cdec = _retention_tables()
    tab = pl.BlockSpec((RET_ROWS, RET_QK_WIDTH), lambda b, j: (j, 0))
    const3 = pl.BlockSpec((RET_HEADS, RET_CHUNK, LANES), lambda b, j: (0, 0, 0))
    return pl.pallas_call(
        _retention_kernel,
        out_shape=jax.ShapeDtypeStruct((tokens, RET_WIDTH), BF16),
        grid=(batch, per_seq),
        in_specs=[
            pl.BlockSpec((RET_ROWS, RET_IN_WIDTH), lambda b, j: (b * per_seq + j, 0)),
            tab, tab, const3, const3, const3, const3,
            pl.BlockSpec((1, RET_WIDTH), lambda b, j: (0, 0)),
        ],
        out_specs=pl.BlockSpec((RET_ROWS, RET_WIDTH), lambda b, j: (b * per_seq + j, 0)),
        scratch_shapes=[pltpu.VMEM((RET_HEADS, LANES, RET_VALUE_DIM), F32)],
        compiler_params=_cparams(("arbitrary", "arbitrary")),
        name="retention",
    )(ret_in, cos, sin, dmask, qdec, kdec, cdec, ret_gain.reshape(1, RET_WIDTH))


def _swiglu_chunks(h, w1_ref, w3_ref, w2_ref):
    d_ff = w1_ref.shape[-1]
    total = None
    for c0 in range(0, d_ff, MXU_COLS):
        c1 = min(c0 + MXU_COLS, d_ff)
        z = (_silu(_dot(h, w1_ref[:, c0:c1].astype(BF16))) * _dot(h, w3_ref[:, c0:c1].astype(BF16))).astype(BF16)
        part = _dot(z, w2_ref[c0:c1, :].astype(BF16))
        total = part if total is None else total + part
    return total


def _out_proj_ffn_kernel(attn_ref, ret_ref, x_ref, mod_ref, wo_ref, gain_ref, w1_ref, w3_ref, w2_ref, out_ref):
    mod = mod_ref[...]
    mix = (_dot(attn_ref[...], wo_ref[:ATTN_WIDTH, :].astype(BF16))
           + _dot(ret_ref[...], wo_ref[ATTN_WIDTH:, :].astype(BF16)))
    x = x_ref[...] + mod[2:3, :] * mix
    h = _modulated_norm(x, gain_ref[...], mod[4:5], mod[3:4]).astype(BF16)
    out_ref[...] = x + mod[5:6, :] * _swiglu_chunks(h, w1_ref, w3_ref, w2_ref)


def _out_proj_dense_ffn(attn, ret, x, mod, w_out, layer, gain, w1, w3, w2, index):
    tokens = x.shape[0]
    d_ff = w1.shape[2]
    tm = TM_FFN
    per_seq = SEQ // tm
    row = lambda i: (i, 0)
    resident = pl.Buffered(1)
    return pl.pallas_call(
        _out_proj_ffn_kernel,
        out_shape=jax.ShapeDtypeStruct((tokens, D_MODEL), F32),
        grid=(tokens // tm,),
        in_specs=[
            pl.BlockSpec((tm, ATTN_WIDTH), row),
            pl.BlockSpec((tm, RET_WIDTH), row),
            pl.BlockSpec((tm, D_MODEL), row),
            pl.BlockSpec((None, 6, D_MODEL), lambda i: (i // per_seq, 0, 0)),
            pl.BlockSpec((None, D_MODEL, D_MODEL), lambda i: (layer, 0, 0), pipeline_mode=resident),
            pl.BlockSpec((1, D_MODEL), lambda i: (0, 0)),
            pl.BlockSpec((None, D_MODEL, d_ff), lambda i: (index, 0, 0), pipeline_mode=resident),
            pl.BlockSpec((None, D_MODEL, d_ff), lambda i: (index, 0, 0), pipeline_mode=resident),
            pl.BlockSpec((None, d_ff, D_MODEL), lambda i: (index, 0, 0), pipeline_mode=resident),
        ],
        out_specs=pl.BlockSpec((tm, D_MODEL), row),
        compiler_params=_cparams(("arbitrary",)),
        name="out_projection_dense_swiglu",
    )(attn, ret, x, mod, w_out, gain.reshape(1, D_MODEL), w1, w3, w2)


def _router_kernel(attn_ref, ret_ref, x_ref, mod_ref, wo_ref, gain_ref, wr_ref,
                   xo_ref, h_ref, gates_ref, pos_ref, post_ref, start_ref, cnt_ref, carry_ref, *, tiles_per_seq):
    i = pl.program_id(0)

    @pl.when(i % tiles_per_seq == 0)
    def _():
        carry_ref[...] = jnp.zeros_like(carry_ref)

    mod = mod_ref[...]
    mix = (_dot(attn_ref[...], wo_ref[:ATTN_WIDTH, :].astype(BF16))
           + _dot(ret_ref[...], wo_ref[ATTN_WIDTH:, :].astype(BF16)))
    x = x_ref[...] + mod[2:3, :] * mix
    xo_ref[...] = x
    h = _modulated_norm(x, gain_ref[...], mod[4:5], mod[3:4]).astype(BF16)
    h_ref[...] = h
    tm = h.shape[0]
    lane = lax.broadcasted_iota(jnp.int32, (tm, LANES), 1).astype(F32)
    logits = jnp.where(lane < N_EXPERTS, _dot(h, wr_ref[...]), -jnp.inf)
    m1 = jnp.max(logits, axis=-1, keepdims=True)
    i1 = jnp.min(jnp.where(logits == m1, lane, float(LANES)), axis=-1, keepdims=True)
    rest = jnp.where(lane == i1, -jnp.inf, logits)
    m2 = jnp.max(rest, axis=-1, keepdims=True)
    i2 = jnp.min(jnp.where(rest == m2, lane, float(LANES)), axis=-1, keepdims=True)
    e2 = jnp.exp(m2 - m1)
    g1 = 1.0 / (1.0 + e2)
    g2 = e2 / (1.0 + e2)
    gates_ref[...] = jnp.where(lane == i1, g1, 0.0) + jnp.where(lane == i2, g2, 0.0)
    chosen = (lane == i1) | (lane == i2)
    onehot = jnp.where(chosen, 1.0, 0.0)
    r = lax.broadcasted_iota(jnp.int32, (tm, tm), 0)
    c = lax.broadcasted_iota(jnp.int32, (tm, tm), 1)
    tril = jnp.where(c <= r, 1.0, 0.0).astype(BF16)
    incl = _dot(tril, onehot.astype(BF16))
    carry = carry_ref[0:1, :]
    start_ref[...] = carry_ref[...]
    pos = jnp.where(chosen, incl - 1.0 + carry, -1.0)
    pos_ref[...] = pos
    post_ref[...] = pos.T[:N_EXPERTS, :]
    total = carry + incl[tm - 1:tm, :]
    carry_ref[...] = jnp.broadcast_to(total, carry_ref.shape)
    cnt_ref[...] = jnp.broadcast_to(total, cnt_ref.shape)


def _out_proj_router(attn, ret, x, mod, w_out, layer, gain, w_router):
    tokens = x.shape[0]
    batch = tokens // SEQ
    tm = TM_PROJ
    per_seq = SEQ // tm
    wr = jnp.zeros((D_MODEL, LANES), BF16).at[:, :N_EXPERTS].set(w_router.astype(BF16))
    row = lambda i: (i, 0)
    return pl.pallas_call(
        functools.partial(_router_kernel, tiles_per_seq=per_seq),
        out_shape=(
            jax.ShapeDtypeStruct((tokens, D_MODEL), F32),
            jax.ShapeDtypeStruct((tokens, D_MODEL), BF16),
            jax.ShapeDtypeStruct((tokens, LANES), F32),
            jax.ShapeDtypeStruct((tokens, LANES), F32),
            jax.ShapeDtypeStruct((N_EXPERTS, tokens), F32),
            jax.ShapeDtypeStruct((tokens // tm, 8, LANES), F32),
            jax.ShapeDtypeStruct((batch, 8, LANES), F32),
        ),
        grid=(tokens // tm,),
        in_specs=[
            pl.BlockSpec((tm, ATTN_WIDTH), row),
            pl.BlockSpec((tm, RET_WIDTH), row),
            pl.BlockSpec((tm, D_MODEL), row),
            pl.BlockSpec((None, 6, D_MODEL), lambda i: (i // per_seq, 0, 0)),
            pl.BlockSpec((None, D_MODEL, D_MODEL), lambda i: (layer, 0, 0)),
            pl.BlockSpec((1, D_MODEL), lambda i: (0, 0)),
            pl.BlockSpec((D_MODEL, LANES), lambda i: (0, 0)),
        ],
        out_specs=(
            pl.BlockSpec((tm, D_MODEL), row),
            pl.BlockSpec((tm, D_MODEL), row),
            pl.BlockSpec((tm, LANES), row),
            pl.BlockSpec((tm, LANES), row),
            pl.BlockSpec((N_EXPERTS, tm), lambda i: (0, i)),
            pl.BlockSpec((None, 8, LANES), lambda i: (i, 0, 0)),
            pl.BlockSpec((None, 8, LANES), lambda i: (i // per_seq, 0, 0)),
        ),
        scratch_shapes=[pltpu.VMEM((8, LANES), F32)],
        compiler_params=_cparams(("arbitrary",)),
        name="out_projection_router",
    )(attn, ret, x, mod, w_out, gain.reshape(1, D_MODEL), wr)


TOK_BLOCKS = SEQ // TR_MOE
TAIL_TILES = (64, 128, TR_MOE)
SPILL_ROWS = 16
SMALL_WINDOW = 128
BIG_TILE = 1024
SEG_ALIGN = 16
GROUP_ROWS = 3072
PACKED_ROWS = 2 * SEQ + N_EXPERTS * SEG_ALIGN + 2 * TR_MOE
BIG_CHUNK = 128
STAGE_ROWS = SEQ + TR_MOE
N_BOUNDS = TOK_BLOCKS + 1


def _moe_plan(start, cnt):
    batch = cnt.shape[0]
    counts = cnt[:, 0, :N_EXPERTS].astype(jnp.int32)
    bounds = jnp.concatenate([start[:, 0, :N_EXPERTS].reshape(batch, TOK_BLOCKS, N_EXPERTS).astype(jnp.int32),
                              counts[:, None, :]], axis=1)
    seg = (counts + (SEG_ALIGN - 1)) // SEG_ALIGN * SEG_ALIGN
    rows_e = jnp.sum(seg, axis=0)
    groups_e = (rows_e + (GROUP_ROWS - 1)) // GROUP_ROWS
    first_group = jnp.cumsum(groups_e) - groups_e
    seg_off = first_group[None, :] * GROUP_ROWS + jnp.cumsum(seg, axis=0) - seg
    max_rows = batch * (2 * SEQ + N_EXPERTS * (SEG_ALIGN - 1))
    n_groups = max_rows // GROUP_ROWS + N_EXPERTS
    g = jnp.arange(n_groups)
    g_expert = jnp.minimum(jnp.sum(g[:, None] >= jnp.cumsum(groups_e)[None, :], axis=1), N_EXPERTS - 1)
    g_rows = jnp.clip(rows_e[g_expert] - (g - first_group[g_expert]) * GROUP_ROWS, 0, GROUP_ROWS)
    g_rows = jnp.where(g < jnp.sum(groups_e), g_rows, 0)
    return (bounds.reshape(-1), seg.reshape(-1), seg_off.reshape(-1),
            g_expert.astype(jnp.int32), g_rows.astype(jnp.int32), n_groups)


def _segment_copies(seg, make_copy, action):
    n_big = lax.shift_right_logical(seg, BIG_CHUNK.bit_length() - 1)

    def big(i, carry):
        getattr(make_copy(pl.multiple_of(i * BIG_CHUNK, BIG_CHUNK), BIG_CHUNK), action)()
        return carry
    lax.fori_loop(0, n_big, big, 0)
    rest0 = n_big * BIG_CHUNK
    n_small = lax.shift_right_logical(seg - rest0, SEG_ALIGN.bit_length() - 1)

    def small(i, carry):
        getattr(make_copy(pl.multiple_of(rest0 + i * SEG_ALIGN, SEG_ALIGN), SEG_ALIGN), action)()
        return carry
    lax.fori_loop(0, n_small, small, 0)


def _block_windows(tbl_ref, b, e, align):
    base = b * N_BOUNDS * N_EXPERTS + e
    bounds = [tbl_ref[base + tb * N_EXPERTS] for tb in range(N_BOUNDS)]
    shift = align.bit_length() - 1
    win = [pl.multiple_of(lax.shift_left(lax.shift_right_logical(s, shift), shift), align) for s in bounds[:-1]]
    all_small = functools.reduce(jnp.logical_and,
                                 [bounds[tb + 1] - win[tb] <= SMALL_WINDOW for tb in range(TOK_BLOCKS)])
    return bounds, win, all_small


def _dispatch_kernel(tbl_ref, seg_ref, off_ref, h_ref, post_ref, rows_hbm, acc_ref, stage_ref, sem):
    b = pl.program_id(0)
    tr = TR_MOE

    def copies(e, action):
        slot = e % 2
        off = off_ref[b * N_EXPERTS + e]

        def make_copy(r0, rows):
            return pltpu.make_async_copy(stage_ref.at[slot, pl.ds(r0, rows), :],
                                         rows_hbm.at[pl.ds(pl.multiple_of(off + r0, SEG_ALIGN), rows), :],
                                         sem.at[slot])
        _segment_copies(seg_ref[b * N_EXPERTS + e], make_copy, action)

    def expert(e, carry):
        bounds, win, all_small = _block_windows(tbl_ref, b, e, SUBLANES)
        n_tiles = lax.shift_right_logical(bounds[-1] + (tr - 1), tr.bit_length() - 1)

        def clear(r, c):
            acc_ref[pl.ds(pl.multiple_of(r * tr, tr), tr), :] = jnp.zeros((tr, D_MODEL), F32)
            return c
        lax.fori_loop(0, n_tiles + 2, clear, 0)

        def gather(height):
            slot_id = lax.broadcasted_iota(jnp.int32, (height, tr), 0).astype(F32)
            for tb in range(TOK_BLOCKS):
                ts = slice(tb * tr, (tb + 1) * tr)
                local = post_ref[pl.ds(e, 1), ts] - win[tb].astype(F32)
                sel = jnp.where(local == slot_id, 1.0, 0.0).astype(BF16)
                acc_ref[pl.ds(win[tb], height), :] += _dot(sel, h_ref[ts, :])

        @pl.when(all_small)
        def _():
            gather(SMALL_WINDOW)

        @pl.when(jnp.logical_not(all_small))
        def _():
            gather(tr + SPILL_ROWS)

        @pl.when(e >= 2)
        def _():
            copies(e - 2, "wait")

        def to_stage(r, c):
            rows = pl.ds(pl.multiple_of(r * tr, tr), tr)
            stage_ref[e % 2, rows, :] = acc_ref[rows, :].astype(BF16)
            return c
        lax.fori_loop(0, n_tiles, to_stage, 0)
        copies(e, "start")
        return carry
    lax.fori_loop(0, N_EXPERTS, expert, 0, unroll=2)
    copies(N_EXPERTS - 2, "wait")
    copies(N_EXPERTS - 1, "wait")


def _dispatch(h, post, bounds, seg, seg_off, total_rows):
    tokens = h.shape[0]
    grid_spec = pltpu.PrefetchScalarGridSpec(
        num_scalar_prefetch=3,
        grid=(tokens // SEQ,),
        in_specs=[pl.BlockSpec((SEQ, D_MODEL), lambda b, *_: (b, 0)),
                  pl.BlockSpec((N_EXPERTS, SEQ), lambda b, *_: (0, b))],
        out_specs=pl.BlockSpec(memory_space=pl.ANY),
        scratch_shapes=[pltpu.VMEM((SEQ + 3 * TR_MOE, D_MODEL), F32),
                        pltpu.VMEM((2, STAGE_ROWS, D_MODEL), BF16),
                        pltpu.SemaphoreType.DMA((2,))],
    )
    return pl.pallas_call(
        _dispatch_kernel,
        out_shape=jax.ShapeDtypeStruct((total_rows, D_MODEL), BF16),
        grid_spec=grid_spec,
        compiler_params=_cparams(("arbitrary",)),
        name="expert_dispatch",
    )(bounds, seg, seg_off, h, post)


def _experts_kernel(ge_ref, rows_ref, hs_ref, w1_ref, w3_ref, w2_ref, ys_ref, acc_ref, wb1_ref, wb3_ref, wb2_ref):
    g, f = pl.program_id(0), pl.program_id(1)
    last_f = pl.num_programs(1) - 1
    tr = TR_MOE
    rows = rows_ref[g]
    n_big = lax.shift_right_logical(rows, BIG_TILE.bit_length() - 1)
    n_full = lax.shift_right_logical(rows, tr.bit_length() - 1)
    rest = rows - n_full * tr
    n_tiles = lax.shift_right_logical(rows + (tr - 1), tr.bit_length() - 1)
    tail0 = pl.multiple_of(n_full * tr, tr)

    @pl.when(rows > 0)
    def _():
        wb1_ref[...] = w1_ref[...].astype(BF16)
        wb3_ref[...] = w3_ref[...].astype(BF16)
        wb2_ref[...] = w2_ref[...].astype(BF16)

    @pl.when(f == 0)
    def _():
        def clear(r, carry):
            acc_ref[pl.ds(pl.multiple_of(r * tr, tr), tr), :] = jnp.zeros((tr, D_MODEL), F32)
            return carry
        lax.fori_loop(0, n_tiles, clear, 0)

    def swiglu(hr, r0, m):
        if m > tr:
            acc_ref[pl.ds(r0, m), :] += _swiglu_chunks(hr, wb1_ref, wb3_ref, wb2_ref)
        else:
            z = (_silu(_dot(hr, wb1_ref[...])) * _dot(hr, wb3_ref[...])).astype(BF16)
            acc_ref[pl.ds(r0, m), :] += _dot(z, wb2_ref[...])

    def big_tile(r, carry):
        r0 = pl.multiple_of(r * BIG_TILE, BIG_TILE)
        swiglu(hs_ref[pl.ds(r0, BIG_TILE), :], r0, BIG_TILE)
        return carry
    lax.fori_loop(0, n_big, big_tile, 0)

    done = n_big * BIG_TILE
    m = BIG_TILE // 2
    while m >= tr:
        take = (rows - done) >= m

        @pl.when(take)
        def _(done=done, m=m):
            r0 = pl.multiple_of(done, tr)
            swiglu(hs_ref[pl.ds(r0, m), :], r0, m)
        done = done + jnp.where(take, m, 0)
        m //= 2
    lo = 0
    for m in TAIL_TILES:
        @pl.when((rest > lo) & (rest <= m))
        def _():
            valid = lax.broadcasted_iota(jnp.int32, (m, D_MODEL), 0) < rest
            hr = hs_ref[pl.ds(tail0, m), :]
            swiglu(jnp.where(valid, hr, jnp.zeros_like(hr)), tail0, m)
        lo = m

    @pl.when(f == last_f)
    def _():
        ys_ref[...] = jnp.zeros_like(ys_ref)

        def store(r, carry):
            rr = pl.ds(pl.multiple_of(r * tr, tr), tr)
            ys_ref[rr, :] = acc_ref[rr, :].astype(BF16)
            return carry
        lax.fori_loop(0, n_tiles, store, 0)


def _experts(hs, g_expert, g_rows, n_groups, w1, w3, w2):
    d_ff = w1.shape[2]
    tf = TF_MOE
    n_f = d_ff // tf
    chunk = lambda g, f, gr: jnp.where(gr[g] > 0, f, n_f - 1)
    grid_spec = pltpu.PrefetchScalarGridSpec(
        num_scalar_prefetch=2,
        grid=(n_groups, n_f),
        in_specs=[
            pl.BlockSpec((GROUP_ROWS, D_MODEL), lambda g, f, ge, gr: (g, 0)),
            pl.BlockSpec((None, D_MODEL, tf), lambda g, f, ge, gr: (ge[g], 0, chunk(g, f, gr))),
            pl.BlockSpec((None, D_MODEL, tf), lambda g, f, ge, gr: (ge[g], 0, chunk(g, f, gr))),
            pl.BlockSpec((None, tf, D_MODEL), lambda g, f, ge, gr: (ge[g], chunk(g, f, gr), 0)),
        ],
        out_specs=pl.BlockSpec((GROUP_ROWS, D_MODEL), lambda g, f, ge, gr: (g, 0)),
        scratch_shapes=[pltpu.VMEM((GROUP_ROWS, D_MODEL), F32),
                        pltpu.VMEM((D_MODEL, tf), BF16), pltpu.VMEM((D_MODEL, tf), BF16),
                        pltpu.VMEM((tf, D_MODEL), BF16)],
    )
    return pl.pallas_call(
        _experts_kernel,
        out_shape=jax.ShapeDtypeStruct(hs.shape, BF16),
        grid_spec=grid_spec,
        compiler_params=_cparams(("arbitrary", "arbitrary")),
        name="expert_swiglu",
    )(g_expert, g_rows, hs, w1, w3, w2)


def _combine_kernel(tbl_ref, seg_ref, off_ref, rows_hbm, pos_ref, gates_ref, x_ref, mod_ref, out_ref, buf_ref, sem):
    b = pl.program_id(0)
    n_seq = pl.num_programs(0)
    tr = TR_MOE

    def all_segments(seq, action):
        def body(e, first):
            off = off_ref[seq * N_EXPERTS + e]
            seg = seg_ref[seq * N_EXPERTS + e]

            def make_copy(r0, rows):
                return pltpu.make_async_copy(
                    rows_hbm.at[pl.ds(pl.multiple_of(off + r0, SEG_ALIGN), rows), :],
                    buf_ref.at[seq % 2, pl.ds(pl.multiple_of(first + r0, SEG_ALIGN), rows), :],
                    sem.at[seq % 2, e])
            _segment_copies(seg, make_copy, action)
            return first + seg
        lax.fori_loop(0, N_EXPERTS, body, 0)

    @pl.when(b == 0)
    def _():
        buf_ref[...] = jnp.zeros_like(buf_ref)
        all_segments(b, "start")

    @pl.when(b + 1 < n_seq)
    def _():
        all_segments(b + 1, "start")

    out_ref[...] = x_ref[...]
    lane = lax.broadcasted_iota(jnp.int32, (tr, LANES), 1)
    layer_gate = mod_ref[5:6, :]
    all_segments(b, "wait")

    def expert(e, seg_start):
        bounds, win, all_small = _block_windows(tbl_ref, b, e, SEG_ALIGN)

        def scatter(tb, first, height):
            ts = slice(tb * tr, (tb + 1) * tr)
            pos_col = jnp.sum(jnp.where(lane == e, pos_ref[ts, :], 0.0), axis=-1, keepdims=True)
            gate_col = jnp.sum(jnp.where(lane == e, gates_ref[ts, :], 0.0), axis=-1, keepdims=True)
            slot_id = lax.broadcasted_iota(jnp.int32, (tr, height), 1).astype(F32) + float(first)
            sel = jnp.where(pos_col - win[tb].astype(F32) == slot_id, 1.0, 0.0).astype(BF16)
            rows = buf_ref[b % 2, pl.ds(pl.multiple_of(seg_start + win[tb] + first, SEG_ALIGN), height), :]
            out_ref[ts, :] += layer_gate * (gate_col * _dot(sel, rows))

        @pl.when(all_small)
        def _():
            for tb in range(TOK_BLOCKS):
                scatter(tb, 0, SMALL_WINDOW)

        @pl.when(jnp.logical_not(all_small))
        def _():
            for tb in range(TOK_BLOCKS):
                scatter(tb, 0, tr)
            for tb in range(TOK_BLOCKS):
                @pl.when(bounds[tb + 1] - win[tb] > tr)
                def _():
                    scatter(tb, tr, SPILL_ROWS)
        return seg_start + seg_ref[b * N_EXPERTS + e]
    lax.fori_loop(0, N_EXPERTS, expert, 0, unroll=2)


def _combine(ys, pos, gates, x, mod, bounds, seg, seg_off):
    tokens = x.shape[0]
    grid_spec = pltpu.PrefetchScalarGridSpec(
        num_scalar_prefetch=3,
        grid=(tokens // SEQ,),
        in_specs=[pl.BlockSpec(memory_space=pl.ANY),
                  pl.BlockSpec((SEQ, LANES), lambda b, *_: (b, 0)),
                  pl.BlockSpec((SEQ, LANES), lambda b, *_: (b, 0)),
                  pl.BlockSpec((SEQ, D_MODEL), lambda b, *_: (b, 0)),
                  pl.BlockSpec((None, 6, D_MODEL), lambda b, *_: (b, 0, 0))],
        out_specs=pl.BlockSpec((SEQ, D_MODEL), lambda b, *_: (b, 0)),
        scratch_shapes=[pltpu.VMEM((2, PACKED_ROWS, D_MODEL), BF16),
                        pltpu.SemaphoreType.DMA((2, N_EXPERTS))],
    )
    return pl.pallas_call(
        _combine_kernel,
        out_shape=jax.ShapeDtypeStruct((tokens, D_MODEL), F32),
        grid_spec=grid_spec,
        compiler_params=_cparams(("arbitrary",)),
        name="expert_combine",
    )(bounds, seg, seg_off, ys, pos, gates, x, mod)


def _token_mixer(x, mod, gain, w_in, layer, q_gain, k_gain, ret_gain, bias_masks):
    *qkv, ret_in = _in_proj(x, mod, gain, w_in, layer, q_gain, k_gain)
    return _dilated_attention(qkv, bias_masks), _retention(ret_in, ret_gain)


def _out_proj_moe_ffn(attn, ret, x, mod, w_out, layer, gain, w_router, w1, w3, w2):
    x, h, gates, pos, post, start, cnt = _out_proj_router(attn, ret, x, mod, w_out, layer, gain, w_router)
    bounds, seg, seg_off, g_expert, g_rows, n_groups = _moe_plan(start, cnt)
    hs = _dispatch(h, post, bounds, seg, seg_off, n_groups * GROUP_ROWS)
    ys = _experts(hs, g_expert, g_rows, n_groups, w1, w3, w2)
    return _combine(ys, pos, gates, x, mod, bounds, seg, seg_off)


def kernel(x, c, rel_bias_table, norm_mix, norm_ffn, w_mod, b_mod, w_in, q_gain, k_gain, ret_gain, w_out,
           ffn_w1, ffn_w3, ffn_w2, moe_router, moe_w1, moe_w3, moe_w2):
    batch, seq, d_model = x.shape
    assert (seq, d_model) == (SEQ, D_MODEL)
    depth = w_mod.shape[0]
    mods = _modulation(c, w_mod, b_mod).reshape(depth, batch, 6, D_MODEL)
    bias_masks = _bias_masks(rel_bias_table)
    xt = x.reshape(batch * seq, d_model)
    for layer in range(depth):
        mod = mods[layer]
        attn, ret = _token_mixer(xt, mod, norm_mix[layer], w_in, layer, q_gain[layer], k_gain[layer],
                                 ret_gain[layer], bias_masks)
        i = layer // 2
        if layer % 2 == 0:
            xt = _out_proj_dense_ffn(attn, ret, xt, mod, w_out, layer, norm_ffn[layer],
                                     ffn_w1, ffn_w3, ffn_w2, i)
        else:
            xt = _out_proj_moe_ffn(attn, ret, xt, mod, w_out, layer, norm_ffn[layer],
                                   moe_router[i], moe_w1[i], moe_w3[i], moe_w2[i])
    return xt.reshape(batch, seq, d_model)
```

```python
import functools
import math

import jax
import jax.numpy as jnp
import numpy as np
from jax import lax
from jax.experimental import pallas as pl
from jax.experimental.pallas import tpu as pltpu

D_MODEL = 1024
SEQ = 2048
ATTN_HEADS = 8
ATTN_HEAD_DIM = 64
ATTN_WIDTH = ATTN_HEADS * ATTN_HEAD_DIM
DILATED_PATTERNS = ((128, 1), (512, 4), (2048, 16))
BLOCK = 128
NUM_BUCKETS = 32
MAX_DISTANCE = 2048
RET_HEADS = 4
RET_KEY_DIM = 64
RET_VALUE_DIM = 128
RET_WIDTH = RET_HEADS * RET_VALUE_DIM
RET_QK_WIDTH = RET_HEADS * RET_KEY_DIM
RET_CHUNK = 128
ROPE_BASE = 10000.0
IN_WIDTH = 3 * ATTN_WIDTH + 2 * RET_QK_WIDTH + 2 * RET_WIDTH
RET_IN_WIDTH = IN_WIDTH - 3 * ATTN_WIDTH
N_EXPERTS = 8
EPS = 1e-6
NEG_INF = -1e30

LANES = 128
VMEM_LIMIT = 60 * 1024 * 1024

BF16 = jnp.bfloat16
F32 = jnp.float32

TM_PROJ = 256
TM_FFN = 512
MXU_COLS = 256
TR_MOE = 256
TF_MOE = 512
assert TM_PROJ == TR_MOE
RET_ROWS = 512
SUBLANES = 8


def _cparams(sem):
    return pltpu.CompilerParams(dimension_semantics=sem, vmem_limit_bytes=VMEM_LIMIT)


def _dot(a, b):
    return jnp.dot(a, b, preferred_element_type=F32)


def _dot_nt(a, b):
    return lax.dot_general(a, b, (((1,), (1,)), ((), ())), preferred_element_type=F32)


def _dot_tn(a, b):
    return lax.dot_general(a, b, (((0,), (0,)), ((), ())), preferred_element_type=F32)


def _split_bf16(v):
    hi = v.astype(BF16)
    lo = (v - hi.astype(F32)).astype(BF16)
    return hi, lo


def _silu(v):
    return v * (1.0 / (1.0 + jnp.exp(-v)))


def _modulated_norm(x, gain, scale, shift):
    ms = jnp.mean(x * x, axis=-1, keepdims=True)
    y = x * lax.rsqrt(ms + EPS) * gain
    return y * (1.0 + scale) + shift


def _mod_kernel(c_ref, w_ref, b_ref, o_ref):
    ca = _silu(c_ref[...]).astype(BF16)
    o_ref[...] = _dot(ca, w_ref[...].astype(BF16)) + b_ref[...]


def _modulation(c, w_mod, b_mod):
    depth, _, width = w_mod.shape
    batch = c.shape[0]
    tn = 1536
    return pl.pallas_call(
        _mod_kernel,
        out_shape=jax.ShapeDtypeStruct((depth, batch, width), F32),
        grid=(depth, width // tn),
        in_specs=[
            pl.BlockSpec((batch, D_MODEL), lambda l, n: (0, 0)),
            pl.BlockSpec((None, D_MODEL, tn), lambda l, n: (l, 0, n)),
            pl.BlockSpec((None, 1, tn), lambda l, n: (l, 0, n)),
        ],
        out_specs=pl.BlockSpec((None, batch, tn), lambda l, n: (l, 0, n)),
        compiler_params=_cparams(("arbitrary", "arbitrary")),
        name="adaln_modulation",
    )(c, w_mod, b_mod.reshape(depth, 1, width))


def _bias_kernel(table_ref, bucket_ref, o_ref):
    bucket = bucket_ref[...]
    acc = [jnp.full(bucket.shape, NEG_INF, F32) for _ in range(ATTN_HEADS)]
    for b in range(NUM_BUCKETS):
        hit = bucket == b
        for h in range(ATTN_HEADS):
            acc[h] = jnp.where(hit, table_ref[b, h], acc[h])
    for h in range(ATTN_HEADS):
        o_ref[h // 2, (h % 2) * BLOCK:(h % 2 + 1) * BLOCK, :] = acc[h]


BIAS_FULL = {1: 0, 4: 2}
BIAS_FIRST = {1: 1, 4: 3, 16: 4}
N_BIAS_SETS = 5


def _bias_masks(rel_bias_table):
    i = jnp.arange(BLOCK)[:, None]
    j = jnp.arange(2 * BLOCK)[None, :]
    max_exact = NUM_BUCKETS // 2

    def bucket_of(rel, dilation, w_sub, exists):
        n = jnp.maximum(rel * dilation, 0)
        nf = jnp.maximum(n.astype(F32), float(max_exact))
        large = max_exact + (jnp.log(nf / max_exact) / math.log(MAX_DISTANCE / max_exact)
                             * (NUM_BUCKETS - max_exact)).astype(jnp.int32)
        large = jnp.minimum(large, NUM_BUCKETS - 1)
        bucket = jnp.where(n < max_exact, n, large)
        allowed = (rel >= 0) & (rel <= w_sub) & exists
        return jnp.where(allowed, bucket, -1)

    sets = [None] * N_BIAS_SETS
    for window, dilation in DILATED_PATTERNS:
        w_sub = window // dilation
        if dilation in BIAS_FULL:
            sets[BIAS_FULL[dilation]] = bucket_of(i - j + BLOCK, dilation, w_sub, j >= 0)
        sets[BIAS_FIRST[dilation]] = bucket_of(i - j, dilation, w_sub, j < BLOCK)
    buckets = jnp.stack(sets).astype(jnp.int32)
    return pl.pallas_call(
        _bias_kernel,
        out_shape=jax.ShapeDtypeStruct((N_BIAS_SETS, ATTN_HEADS // 2, 2 * BLOCK, 2 * BLOCK), F32),
        grid=(N_BIAS_SETS,),
        in_specs=[
            pl.BlockSpec(memory_space=pltpu.SMEM),
            pl.BlockSpec((None, BLOCK, 2 * BLOCK), lambda p: (p, 0, 0)),
        ],
        out_specs=pl.BlockSpec((None, ATTN_HEADS // 2, 2 * BLOCK, 2 * BLOCK), lambda p: (p, 0, 0, 0)),
        compiler_params=_cparams(("arbitrary",)),
        name="relative_bias_masks",
    )(rel_bias_table, buckets)


HALF = ATTN_WIDTH // 2


def _in_proj_kernel(x_ref, mod_ref, gain_ref, w_ref, qg_ref, kg_ref, grp_ref,
                    q1_ref, k1_ref, v1_ref, q4_ref, k4_ref, v4_ref, q16_ref, k16_ref, v16_ref, r_ref,
                    perm_ref):
    mod = mod_ref[...]
    h = _modulated_norm(x_ref[...], gain_ref[...], mod[1:2], mod[0:1]).astype(BF16)
    proj = _dot(h, w_ref[...].astype(BF16))
    grp = grp_ref[...]
    tm = proj.shape[0]

    def head_norm(t, gain):
        hi, lo = _split_bf16(t * t)
        ss = _dot(hi, grp) + _dot(lo, grp)
        return t * lax.rsqrt(ss * (1.0 / ATTN_HEAD_DIM) + EPS) * gain

    def emit(t, o1_ref, o4_ref, o16_ref):
        o1_ref[...] = t.astype(BF16)
        for j in range(ATTN_WIDTH // LANES):
            perm_ref[j] = t[:, j * LANES:(j + 1) * LANES]
        for dil, o_ref in ((4, o4_ref), (16, o16_ref)):
            for hf in range(2):
                for r in range(dil):
                    for jj in range(HALF // LANES):
                        c0 = (hf * dil + r) * HALF + jj * LANES
                        o_ref[:, c0:c0 + LANES] = perm_ref[hf * (HALF // LANES) + jj,
                                                           pl.ds(r, tm // dil, stride=dil), :].astype(BF16)

    emit(head_norm(proj[:, :ATTN_WIDTH], qg_ref[...]) * (ATTN_HEAD_DIM ** -0.5), q1_ref, q4_ref, q16_ref)
    emit(head_norm(proj[:, ATTN_WIDTH:2 * ATTN_WIDTH], kg_ref[...]), k1_ref, k4_ref, k16_ref)
    emit(proj[:, 2 * ATTN_WIDTH:3 * ATTN_WIDTH], v1_ref, v4_ref, v16_ref)
    r_ref[...] = proj[:, 3 * ATTN_WIDTH:]


def _in_proj(x, mod, gain, w_in, layer, q_gain, k_gain):
    tokens = x.shape[0]
    tm = TM_FFN
    per_seq = SEQ // tm
    grp = np.kron(np.eye(ATTN_HEADS), np.ones((ATTN_HEAD_DIM, ATTN_HEAD_DIM))).astype(np.float32)
    row = lambda i: (i, 0)
    const = lambda i: (0, 0)
    layouts = []
    for dil in (1, 4, 16):
        shape = jax.ShapeDtypeStruct((tokens // dil, dil * ATTN_WIDTH), BF16)
        spec = pl.BlockSpec((tm // dil, dil * ATTN_WIDTH), row)
        layouts.append(((shape,) * 3, (spec,) * 3))
    out_shape = sum((s for s, _ in layouts), ()) + (jax.ShapeDtypeStruct((tokens, RET_IN_WIDTH), F32),)
    out_specs = sum((s for _, s in layouts), ()) + (pl.BlockSpec((tm, RET_IN_WIDTH), row),)
    return pl.pallas_call(
        _in_proj_kernel,
        out_shape=out_shape,
        grid=(tokens // tm,),
        in_specs=[
            pl.BlockSpec((tm, D_MODEL), row),
            pl.BlockSpec((None, 6, D_MODEL), lambda i: (i // per_seq, 0, 0)),
            pl.BlockSpec((1, D_MODEL), const),
            pl.BlockSpec((None, D_MODEL, IN_WIDTH), lambda i: (layer, 0, 0), pipeline_mode=pl.Buffered(1)),
            pl.BlockSpec((1, ATTN_WIDTH), const),
            pl.BlockSpec((1, ATTN_WIDTH), const),
            pl.BlockSpec((ATTN_WIDTH, ATTN_WIDTH), const),
        ],
        out_specs=out_specs,
        scratch_shapes=[pltpu.VMEM((ATTN_WIDTH // LANES, tm, LANES), F32)],
        compiler_params=_cparams(("arbitrary",)),
        name="in_projection",
    )(x, mod, gain.reshape(1, D_MODEL), w_in,
      jnp.tile(q_gain, ATTN_HEADS).reshape(1, ATTN_WIDTH),
      jnp.tile(k_gain, ATTN_HEADS).reshape(1, ATTN_WIDTH),
      jnp.asarray(grp, BF16))


PAIRS_PER_HALF = ATTN_HEADS // 4
GROUP = 16


def _pair_scores(qp, kp, vp, bias2, masks, low):
    q2 = jnp.concatenate([qp * masks[0], qp * masks[1]], axis=0)
    s = _dot_nt(q2, kp) + bias2
    m = jnp.max(s, axis=-1, keepdims=True)
    p = jnp.exp(s - m)
    den = jnp.sum(p, axis=-1, keepdims=True)
    pv = _dot(p.astype(BF16), vp)
    pick = lambda t: jnp.where(low, t[:BLOCK], t[BLOCK:])
    return pick(pv), pick(m), pick(den)


def _attn_kernel(q1_ref, k1_ref, v1_ref, q4_ref, k4_ref, v4_ref, q16_ref, k16_ref, v16_ref, bm_ref,
                 o_ref, acc_ref, max_ref, den_ref):
    lane = lax.broadcasted_iota(jnp.int32, (BLOCK, LANES), 1)
    low = lane < ATTN_HEAD_DIM
    masks = (jnp.where(low, 1.0, 0.0).astype(BF16), jnp.where(low, 0.0, 1.0).astype(BF16))

    def block(q_ref, k_ref, v_ref, c0, q0, w0, width, bias_set, rows, first):
        for p in range(PAIRS_PER_HALF):
            cs = slice(c0 + p * LANES, c0 + (p + 1) * LANES)
            acc, m, den = _pair_scores(q_ref[pl.ds(q0, BLOCK), cs], k_ref[pl.ds(w0, width), cs],
                                       v_ref[pl.ds(w0, width), cs], bm_ref[bias_set, p, :, 0:width],
                                       masks, low)
            if not first:
                m_old = max_ref[p, rows, :]
                m_new = jnp.maximum(m_old, m)
                a, b = jnp.exp(m_old - m_new), jnp.exp(m - m_new)
                den = den_ref[p, rows, :] * a + den * b
                acc = acc_ref[p, rows, :] * a + acc * b
                m = m_new
            max_ref[p, rows, :] = m
            den_ref[p, rows, :] = den
            acc_ref[p, rows, :] = acc

    def d1_group(g, carry):
        for u in range(GROUP):
            n = g * GROUP + u
            q0 = pl.multiple_of(n * BLOCK, BLOCK)
            w0 = pl.multiple_of(jnp.maximum(n - 1, 0) * BLOCK, BLOCK)
            bias_set = jnp.where(n == 0, BIAS_FIRST[1], BIAS_FULL[1])
            block(q1_ref, k1_ref, v1_ref, 0, q0, w0, 2 * BLOCK, bias_set, pl.ds(q0, BLOCK), True)
        return carry
    lax.fori_loop(0, SEQ // BLOCK // GROUP, d1_group, 0)

    for r in range(4):
        for n in range(SEQ // 4 // BLOCK):
            block(q4_ref, k4_ref, v4_ref, r * HALF, n * BLOCK, max(n - 1, 0) * BLOCK, 2 * BLOCK,
                  BIAS_FIRST[4] if n == 0 else BIAS_FULL[4],
                  pl.ds(r + 4 * BLOCK * n, BLOCK, stride=4), False)

    for r in range(16):
        block(q16_ref, k16_ref, v16_ref, r * HALF, 0, 0, BLOCK, BIAS_FIRST[16],
              pl.ds(r, BLOCK, stride=16), False)

    for n in range(SEQ // BLOCK):
        rows = slice(n * BLOCK, (n + 1) * BLOCK)
        for p in range(PAIRS_PER_HALF):
            o_ref[rows, p * LANES:(p + 1) * LANES] = (acc_ref[p, rows, :] / den_ref[p, rows, :]).astype(BF16)


def _dilated_attention(qkv, bias_masks):
    tokens = qkv[0].shape[0]
    batch = tokens // SEQ
    specs = []
    for dil in (1, 4, 16):
        specs += [pl.BlockSpec((SEQ // dil, dil * HALF), lambda b, hf: (b, hf))] * 3
    state = pltpu.VMEM((PAIRS_PER_HALF, SEQ, LANES), F32)
    return pl.pallas_call(
        _attn_kernel,
        out_shape=jax.ShapeDtypeStruct((tokens, ATTN_WIDTH), BF16),
        grid=(batch, 2),
        in_specs=specs + [pl.BlockSpec((N_BIAS_SETS, PAIRS_PER_HALF, 2 * BLOCK, 2 * BLOCK),
                                       lambda b, hf: (0, hf, 0, 0))],
        out_specs=pl.BlockSpec((SEQ, HALF), lambda b, hf: (b, hf)),
        scratch_shapes=[state, state, state],
        compiler_params=_cparams(("arbitrary", "arbitrary")),
        name="dilated_attention",
    )(*qkv, bias_masks)


def _retention_kernel(r_ref, cos_ref, sin_ref, dmask_ref, qdec_ref, kdec_ref, cdec_ref, gain_ref,
                      o_ref, state_ref):
    @pl.when(pl.program_id(1) == 0)
    def _():
        state_ref[...] = jnp.zeros_like(state_ref)

    lane = lax.broadcasted_iota(jnp.int32, (RET_CHUNK, LANES), 1)
    low = lane < RET_KEY_DIM
    first_half = (lane % RET_KEY_DIM) < (RET_KEY_DIM // 2)

    def rotate(t, cos, sin):
        partner = jnp.where(first_half, pltpu.roll(t, LANES - RET_KEY_DIM // 2, 1),
                            pltpu.roll(t, RET_KEY_DIM // 2, 1))
        return t * cos + partner * sin

    for c in range(RET_ROWS // RET_CHUNK):
        rows = slice(c * RET_CHUNK, (c + 1) * RET_CHUNK)
        for hp in range(RET_HEADS // 2):
            qs = slice(hp * LANES, (hp + 1) * LANES)
            ks = slice(RET_QK_WIDTH + hp * LANES, RET_QK_WIDTH + (hp + 1) * LANES)
            cos, sin = cos_ref[rows, qs], sin_ref[rows, qs]
            q_pair = rotate(r_ref[rows, qs], cos, sin) * (RET_KEY_DIM ** -0.5)
            k_pair = rotate(r_ref[rows, ks], cos, sin)
            for hh in range(2):
                head = 2 * hp + hh
                vs = slice(2 * RET_QK_WIDTH + head * LANES, 2 * RET_QK_WIDTH + (head + 1) * LANES)
                gs = slice(2 * RET_QK_WIDTH + RET_WIDTH + head * LANES,
                           2 * RET_QK_WIDTH + RET_WIDTH + (head + 1) * LANES)
                keep = low if hh == 0 else jnp.logical_not(low)
                qm = jnp.where(keep, q_pair, 0.0)
                vb = r_ref[rows, vs].astype(BF16)
                state = state_ref[head]
                inner = _dot_nt(qm.astype(BF16), k_pair.astype(BF16)) * dmask_ref[head]
                y = _dot(inner.astype(BF16), vb)
                y = y + _dot((qm * qdec_ref[head]).astype(BF16), state.astype(BF16))
                state_ref[head] = state * cdec_ref[head] + _dot_tn((k_pair * kdec_ref[head]).astype(BF16), vb)
                mu = jnp.mean(y, axis=-1, keepdims=True)
                yc = y - mu
                var = jnp.mean(yc * yc, axis=-1, keepdims=True)
                yn = yc * lax.rsqrt(var + EPS) * gain_ref[:, head * LANES:(head + 1) * LANES]
                o_ref[rows, head * LANES:(head + 1) * LANES] = (_silu(r_ref[rows, gs]) * yn).astype(BF16)


def _retention_tables():
    half = RET_KEY_DIM // 2
    pos = jnp.arange(SEQ, dtype=F32)
    inv = ROPE_BASE ** (-jnp.arange(half, dtype=F32) / half)
    ang = pos[:, None] * inv[None, :]
    cos, sin = jnp.cos(ang), jnp.sin(ang)
    cos_full = jnp.tile(jnp.concatenate([cos, cos], axis=-1), (1, RET_HEADS))
    sin_signed = jnp.tile(jnp.concatenate([-sin, sin], axis=-1), (1, RET_HEADS))
    log_g = jnp.log(1.0 - 2.0 ** (-5.0 - jnp.arange(RET_HEADS, dtype=F32)))
    idx = jnp.arange(RET_CHUNK, dtype=F32)
    diff = idx[:, None] - idx[None, :]
    dmask = jnp.where(diff >= 0, jnp.exp(jnp.maximum(diff, 0.0)[None] * log_g[:, None, None]), 0.0)
    q_decay = jnp.exp((idx + 1.0)[None, :] * log_g[:, None])[..., None]
    k_decay = jnp.exp((RET_CHUNK - 1.0 - idx)[None, :] * log_g[:, None])[..., None]
    chunk_decay = jnp.exp(RET_CHUNK * log_g)[:, None, None]
    full = (RET_HEADS, RET_CHUNK, LANES)
    return (cos_full, sin_signed, dmask, jnp.broadcast_to(q_decay, full),
            jnp.broadcast_to(k_decay, full), jnp.broadcast_to(chunk_decay, full))


def _retention(ret_in, ret_gain):
    tokens = ret_in.shape[0]
    batch = tokens // SEQ
    per_seq = SEQ // RET_ROWS
    cos, sin, dmask, qdec, kdec, cdec = _retention_tables()
    tab = pl.BlockSpec((RET_ROWS, RET_QK_WIDTH), lambda b, j: (j, 0))
    const3 = pl.BlockSpec((RET_HEADS, RET_CHUNK, LANES), lambda b, j: (0, 0, 0))
    return pl.pallas_call(
        _retention_kernel,
        out_shape=jax.ShapeDtypeStruct((tokens, RET_WIDTH), BF16),
        grid=(batch, per_seq),
        in_specs=[
            pl.BlockSpec((RET_ROWS, RET_IN_WIDTH), lambda b, j: (b * per_seq + j, 0)),
            tab, tab, const3, const3, const3, const3,
            pl.BlockSpec((1, RET_WIDTH), lambda b, j: (0, 0)),
        ],
        out_specs=pl.BlockSpec((RET_ROWS, RET_WIDTH), lambda b, j: (b * per_seq + j, 0)),
        scratch_shapes=[pltpu.VMEM((RET_HEADS, LANES, RET_VALUE_DIM), F32)],
        compiler_params=_cparams(("arbitrary", "arbitrary")),
        name="retention",
    )(ret_in, cos, sin, dmask, qdec, kdec, cdec, ret_gain.reshape(1, RET_WIDTH))


def _swiglu_chunks(h, w1_ref, w3_ref, w2_ref):
    d_ff = w1_ref.shape[-1]
    total = None
    for c0 in range(0, d_ff, MXU_COLS):
        c1 = min(c0 + MXU_COLS, d_ff)
        z = (_silu(_dot(h, w1_ref[:, c0:c1].astype(BF16))) * _dot(h, w3_ref[:, c0:c1].astype(BF16))).astype(BF16)
        part = _dot(z, w2_ref[c0:c1, :].astype(BF16))
        total = part if total is None else total + part
    return total


def _out_proj_ffn_kernel(attn_ref, ret_ref, x_ref, mod_ref, wo_ref, gain_ref, w1_ref, w3_ref, w2_ref, out_ref):
    mod = mod_ref[...]
    mix = (_dot(attn_ref[...], wo_ref[:ATTN_WIDTH, :].astype(BF16))
           + _dot(ret_ref[...], wo_ref[ATTN_WIDTH:, :].astype(BF16)))
    x = x_ref[...] + mod[2:3, :] * mix
    h = _modulated_norm(x, gain_ref[...], mod[4:5], mod[3:4]).astype(BF16)
    out_ref[...] = x + mod[5:6, :] * _swiglu_chunks(h, w1_ref, w3_ref, w2_ref)


def _out_proj_dense_ffn(attn, ret, x, mod, w_out, layer, gain, w1, w3, w2, index):
    tokens = x.shape[0]
    d_ff = w1.shape[2]
    tm = TM_FFN
    per_seq = SEQ // tm
    row = lambda i: (i, 0)
    resident = pl.Buffered(1)
    return pl.pallas_call(
        _out_proj_ffn_kernel,
        out_shape=jax.ShapeDtypeStruct((tokens, D_MODEL), F32),
        grid=(tokens // tm,),
        in_specs=[
            pl.BlockSpec((tm, ATTN_WIDTH), row),
            pl.BlockSpec((tm, RET_WIDTH), row),
            pl.BlockSpec((tm, D_MODEL), row),
            pl.BlockSpec((None, 6, D_MODEL), lambda i: (i // per_seq, 0, 0)),
            pl.BlockSpec((None, D_MODEL, D_MODEL), lambda i: (layer, 0, 0), pipeline_mode=resident),
            pl.BlockSpec((1, D_MODEL), lambda i: (0, 0)),
            pl.BlockSpec((None, D_MODEL, d_ff), lambda i: (index, 0, 0), pipeline_mode=resident),
            pl.BlockSpec((None, D_MODEL, d_ff), lambda i: (index, 0, 0), pipeline_mode=resident),
            pl.BlockSpec((None, d_ff, D_MODEL), lambda i: (index, 0, 0), pipeline_mode=resident),
        ],
        out_specs=pl.BlockSpec((tm, D_MODEL), row),
        compiler_params=_cparams(("arbitrary",)),
        name="out_projection_dense_swiglu",
    )(attn, ret, x, mod, w_out, gain.reshape(1, D_MODEL), w1, w3, w2)


def _router_kernel(attn_ref, ret_ref, x_ref, mod_ref, wo_ref, gain_ref, wr_ref,
                   xo_ref, h_ref, gates_ref, pos_ref, post_ref, start_ref, cnt_ref, carry_ref, *, tiles_per_seq):
    i = pl.program_id(0)

    @pl.when(i % tiles_per_seq == 0)
    def _():
        carry_ref[...] = jnp.zeros_like(carry_ref)

    mod = mod_ref[...]
    mix = (_dot(attn_ref[...], wo_ref[:ATTN_WIDTH, :].astype(BF16))
           + _dot(ret_ref[...], wo_ref[ATTN_WIDTH:, :].astype(BF16)))
    x = x_ref[...] + mod[2:3, :] * mix
    xo_ref[...] = x
    h = _modulated_norm(x, gain_ref[...], mod[4:5], mod[3:4]).astype(BF16)
    h_ref[...] = h
    tm = h.shape[0]
    lane = lax.broadcasted_iota(jnp.int32, (tm, LANES), 1).astype(F32)
    logits = jnp.where(lane < N_EXPERTS, _dot(h, wr_ref[...]), -jnp.inf)
    m1 = jnp.max(logits, axis=-1, keepdims=True)
    i1 = jnp.min(jnp.where(logits == m1, lane, float(LANES)), axis=-1, keepdims=True)
    rest = jnp.where(lane == i1, -jnp.inf, logits)
    m2 = jnp.max(rest, axis=-1, keepdims=True)
    i2 = jnp.min(jnp.where(rest == m2, lane, float(LANES)), axis=-1, keepdims=True)
    e2 = jnp.exp(m2 - m1)
    g1 = 1.0 / (1.0 + e2)
    g2 = e2 / (1.0 + e2)
    gates_ref[...] = jnp.where(lane == i1, g1, 0.0) + jnp.where(lane == i2, g2, 0.0)
    chosen = (lane == i1) | (lane == i2)
    onehot = jnp.where(chosen, 1.0, 0.0)
    r = lax.broadcasted_iota(jnp.int32, (tm, tm), 0)
    c = lax.broadcasted_iota(jnp.int32, (tm, tm), 1)
    tril = jnp.where(c <= r, 1.0, 0.0).astype(BF16)
    incl = _dot(tril, onehot.astype(BF16))
    carry = carry_ref[0:1, :]
    start_ref[...] = carry_ref[...]
    pos = jnp.where(chosen, incl - 1.0 + carry, -1.0)
    pos_ref[...] = pos
    post_ref[...] = pos.T[:N_EXPERTS, :]
    total = carry + incl[tm - 1:tm, :]
    carry_ref[...] = jnp.broadcast_to(total, carry_ref.shape)
    cnt_ref[...] = jnp.broadcast_to(total, cnt_ref.shape)


def _out_proj_router(attn, ret, x, mod, w_out, layer, gain, w_router):
    tokens = x.shape[0]
    batch = tokens // SEQ
    tm = TM_PROJ
    per_seq = SEQ // tm
    wr = jnp.zeros((D_MODEL, LANES), BF16).at[:, :N_EXPERTS].set(w_router.astype(BF16))
    row = lambda i: (i, 0)
    return pl.pallas_call(
        functools.partial(_router_kernel, tiles_per_seq=per_seq),
        out_shape=(
            jax.ShapeDtypeStruct((tokens, D_MODEL), F32),
            jax.ShapeDtypeStruct((tokens, D_MODEL), BF16),
            jax.ShapeDtypeStruct((tokens, LANES), F32),
            jax.ShapeDtypeStruct((tokens, LANES), F32),
            jax.ShapeDtypeStruct((N_EXPERTS, tokens), F32),
            jax.ShapeDtypeStruct((tokens // tm, 8, LANES), F32),
            jax.ShapeDtypeStruct((batch, 8, LANES), F32),
        ),
        grid=(tokens // tm,),
        in_specs=[
            pl.BlockSpec((tm, ATTN_WIDTH), row),
            pl.BlockSpec((tm, RET_WIDTH), row),
            pl.BlockSpec((tm, D_MODEL), row),
            pl.BlockSpec((None, 6, D_MODEL), lambda i: (i // per_seq, 0, 0)),
            pl.BlockSpec((None, D_MODEL, D_MODEL), lambda i: (layer, 0, 0)),
            pl.BlockSpec((1, D_MODEL), lambda i: (0, 0)),
            pl.BlockSpec((D_MODEL, LANES), lambda i: (0, 0)),
        ],
        out_specs=(
            pl.BlockSpec((tm, D_MODEL), row),
            pl.BlockSpec((tm, D_MODEL), row),
            pl.BlockSpec((tm, LANES), row),
            pl.BlockSpec((tm, LANES), row),
            pl.BlockSpec((N_EXPERTS, tm), lambda i: (0, i)),
            pl.BlockSpec((None, 8, LANES), lambda i: (i, 0, 0)),
            pl.BlockSpec((None, 8, LANES), lambda i: (i // per_seq, 0, 0)),
        ),
        scratch_shapes=[pltpu.VMEM((8, LANES), F32)],
        compiler_params=_cparams(("arbitrary",)),
        name="out_projection_router",
    )(attn, ret, x, mod, w_out, gain.reshape(1, D_MODEL), wr)


TOK_BLOCKS = SEQ // TR_MOE
TAIL_TILES = (64, 128, TR_MOE)
SPILL_ROWS = 16
SMALL_WINDOW = 128
BIG_TILE = 1024
SEG_ALIGN = 16
GROUP_ROWS = 3072
PACKED_ROWS = 2 * SEQ + N_EXPERTS * SEG_ALIGN + 2 * TR_MOE
BIG_CHUNK = 128
STAGE_ROWS = SEQ + TR_MOE
N_BOUNDS = TOK_BLOCKS + 1


def _moe_plan(start, cnt):
    batch = cnt.shape[0]
    counts = cnt[:, 0, :N_EXPERTS].astype(jnp.int32)
    bounds = jnp.concatenate([start[:, 0, :N_EXPERTS].reshape(batch, TOK_BLOCKS, N_EXPERTS).astype(jnp.int32),
                              counts[:, None, :]], axis=1)
    seg = (counts + (SEG_ALIGN - 1)) // SEG_ALIGN * SEG_ALIGN
    rows_e = jnp.sum(seg, axis=0)
    groups_e = (rows_e + (GROUP_ROWS - 1)) // GROUP_ROWS
    first_group = jnp.cumsum(groups_e) - groups_e
    seg_off = first_group[None, :] * GROUP_ROWS + jnp.cumsum(seg, axis=0) - seg
    max_rows = batch * (2 * SEQ + N_EXPERTS * (SEG_ALIGN - 1))
    n_groups = max_rows // GROUP_ROWS + N_EXPERTS
    g = jnp.arange(n_groups)
    g_expert = jnp.minimum(jnp.sum(g[:, None] >= jnp.cumsum(groups_e)[None, :], axis=1), N_EXPERTS - 1)
    g_rows = jnp.clip(rows_e[g_expert] - (g - first_group[g_expert]) * GROUP_ROWS, 0, GROUP_ROWS)
    g_rows = jnp.where(g < jnp.sum(groups_e), g_rows, 0)
    return (bounds.reshape(-1), seg.reshape(-1), seg_off.reshape(-1),
            g_expert.astype(jnp.int32), g_rows.astype(jnp.int32), n_groups)


def _segment_copies(seg, make_copy, action):
    n_big = lax.shift_right_logical(seg, BIG_CHUNK.bit_length() - 1)

    def big(i, carry):
        getattr(make_copy(pl.multiple_of(i * BIG_CHUNK, BIG_CHUNK), BIG_CHUNK), action)()
        return carry
    lax.fori_loop(0, n_big, big, 0)
    rest0 = n_big * BIG_CHUNK
    n_small = lax.shift_right_logical(seg - rest0, SEG_ALIGN.bit_length() - 1)

    def small(i, carry):
        getattr(make_copy(pl.multiple_of(rest0 + i * SEG_ALIGN, SEG_ALIGN), SEG_ALIGN), action)()
        return carry
    lax.fori_loop(0, n_small, small, 0)


def _block_windows(tbl_ref, b, e, align):
    base = b * N_BOUNDS * N_EXPERTS + e
    bounds = [tbl_ref[base + tb * N_EXPERTS] for tb in range(N_BOUNDS)]
    shift = align.bit_length() - 1
    win = [pl.multiple_of(lax.shift_left(lax.shift_right_logical(s, shift), shift), align) for s in bounds[:-1]]
    all_small = functools.reduce(jnp.logical_and,
                                 [bounds[tb + 1] - win[tb] <= SMALL_WINDOW for tb in range(TOK_BLOCKS)])
    return bounds, win, all_small


def _dispatch_kernel(tbl_ref, seg_ref, off_ref, h_ref, post_ref, rows_hbm, acc_ref, stage_ref, sem):
    b = pl.program_id(0)
    tr = TR_MOE

    def copies(e, action):
        slot = e % 2
        off = off_ref[b * N_EXPERTS + e]

        def make_copy(r0, rows):
            return pltpu.make_async_copy(stage_ref.at[slot, pl.ds(r0, rows), :],
                                         rows_hbm.at[pl.ds(pl.multiple_of(off + r0, SEG_ALIGN), rows), :],
                                         sem.at[slot])
        _segment_copies(seg_ref[b * N_EXPERTS + e], make_copy, action)

    def expert(e, carry):
        bounds, win, all_small = _block_windows(tbl_ref, b, e, SUBLANES)
        n_tiles = lax.shift_right_logical(bounds[-1] + (tr - 1), tr.bit_length() - 1)

        def clear(r, c):
            acc_ref[pl.ds(pl.multiple_of(r * tr, tr), tr), :] = jnp.zeros((tr, D_MODEL), F32)
            return c
        lax.fori_loop(0, n_tiles + 2, clear, 0)

        def gather(height):
            slot_id = lax.broadcasted_iota(jnp.int32, (height, tr), 0).astype(F32)
            for tb in range(TOK_BLOCKS):
                ts = slice(tb * tr, (tb + 1) * tr)
                local = post_ref[pl.ds(e, 1), ts] - win[tb].astype(F32)
                sel = jnp.where(local == slot_id, 1.0, 0.0).astype(BF16)
                acc_ref[pl.ds(win[tb], height), :] += _dot(sel, h_ref[ts, :])

        @pl.when(all_small)
        def _():
            gather(SMALL_WINDOW)

        @pl.when(jnp.logical_not(all_small))
        def _():
            gather(tr + SPILL_ROWS)

        @pl.when(e >= 2)
        def _():
            copies(e - 2, "wait")

        def to_stage(r, c):
            rows = pl.ds(pl.multiple_of(r * tr, tr), tr)
            stage_ref[e % 2, rows, :] = acc_ref[rows, :].astype(BF16)
            return c
        lax.fori_loop(0, n_tiles, to_stage, 0)
        copies(e, "start")
        return carry
    lax.fori_loop(0, N_EXPERTS, expert, 0)
    copies(N_EXPERTS - 2, "wait")
    copies(N_EXPERTS - 1, "wait")


def _dispatch(h, post, bounds, seg, seg_off, total_rows):
    tokens = h.shape[0]
    grid_spec = pltpu.PrefetchScalarGridSpec(
        num_scalar_prefetch=3,
        grid=(tokens // SEQ,),
        in_specs=[pl.BlockSpec((SEQ, D_MODEL), lambda b, *_: (b, 0)),
                  pl.BlockSpec((N_EXPERTS, SEQ), lambda b, *_: (0, b))],
        out_specs=pl.BlockSpec(memory_space=pl.ANY),
        scratch_shapes=[pltpu.VMEM((SEQ + 3 * TR_MOE, D_MODEL), F32),
                        pltpu.VMEM((2, STAGE_ROWS, D_MODEL), BF16),
                        pltpu.SemaphoreType.DMA((2,))],
    )
    return pl.pallas_call(
        _dispatch_kernel,
        out_shape=jax.ShapeDtypeStruct((total_rows, D_MODEL), BF16),
        grid_spec=grid_spec,
        compiler_params=_cparams(("arbitrary",)),
        name="expert_dispatch",
    )(bounds, seg, seg_off, h, post)


def _experts_kernel(ge_ref, rows_ref, hs_ref, w1_ref, w3_ref, w2_ref, ys_ref, acc_ref, wb1_ref, wb3_ref, wb2_ref):
    g, f = pl.program_id(0), pl.program_id(1)
    last_f = pl.num_programs(1) - 1
    tr = TR_MOE
    rows = rows_ref[g]
    n_big = lax.shift_right_logical(rows, BIG_TILE.bit_length() - 1)
    n_full = lax.shift_right_logical(rows, tr.bit_length() - 1)
    rest = rows - n_full * tr
    n_tiles = lax.shift_right_logical(rows + (tr - 1), tr.bit_length() - 1)
    tail0 = pl.multiple_of(n_full * tr, tr)

    @pl.when(rows > 0)
    def _():
        wb1_ref[...] = w1_ref[...].astype(BF16)
        wb3_ref[...] = w3_ref[...].astype(BF16)
        wb2_ref[...] = w2_ref[...].astype(BF16)

    @pl.when(f == 0)
    def _():
        def clear(r, carry):
            acc_ref[pl.ds(pl.multiple_of(r * tr, tr), tr), :] = jnp.zeros((tr, D_MODEL), F32)
            return carry
        lax.fori_loop(0, n_tiles, clear, 0)

    def swiglu(hr, r0, m):
        if m > tr:
            acc_ref[pl.ds(r0, m), :] += _swiglu_chunks(hr, wb1_ref, wb3_ref, wb2_ref)
        else:
            z = (_silu(_dot(hr, wb1_ref[...])) * _dot(hr, wb3_ref[...])).astype(BF16)
            acc_ref[pl.ds(r0, m), :] += _dot(z, wb2_ref[...])

    def big_tile(r, carry):
        r0 = pl.multiple_of(r * BIG_TILE, BIG_TILE)
        swiglu(hs_ref[pl.ds(r0, BIG_TILE), :], r0, BIG_TILE)
        return carry
    lax.fori_loop(0, n_big, big_tile, 0)

    done = n_big * BIG_TILE
    m = BIG_TILE // 2
    while m >= tr:
        take = (rows - done) >= m

        @pl.when(take)
        def _(done=done, m=m):
            r0 = pl.multiple_of(done, tr)
            swiglu(hs_ref[pl.ds(r0, m), :], r0, m)
        done = done + jnp.where(take, m, 0)
        m //= 2
    lo = 0
    for m in TAIL_TILES:
        @pl.when((rest > lo) & (rest <= m))
        def _():
            valid = lax.broadcasted_iota(jnp.int32, (m, D_MODEL), 0) < rest
            hr = hs_ref[pl.ds(tail0, m), :]
            swiglu(jnp.where(valid, hr, jnp.zeros_like(hr)), tail0, m)
        lo = m

    @pl.when(f == last_f)
    def _():
        ys_ref[...] = jnp.zeros_like(ys_ref)

        def store(r, carry):
            rr = pl.ds(pl.multiple_of(r * tr, tr), tr)
            ys_ref[rr, :] = acc_ref[rr, :].astype(BF16)
            return carry
        lax.fori_loop(0, n_tiles, store, 0)


def _experts(hs, g_expert, g_rows, n_groups, w1, w3, w2):
    d_ff = w1.shape[2]
    tf = TF_MOE
    n_f = d_ff // tf
    chunk = lambda g, f, gr: jnp.where(gr[g] > 0, f, n_f - 1)
    grid_spec = pltpu.PrefetchScalarGridSpec(
        num_scalar_prefetch=2,
        grid=(n_groups, n_f),
        in_specs=[
            pl.BlockSpec((GROUP_ROWS, D_MODEL), lambda g, f, ge, gr: (g, 0)),
            pl.BlockSpec((None, D_MODEL, tf), lambda g, f, ge, gr: (ge[g], 0, chunk(g, f, gr))),
            pl.BlockSpec((None, D_MODEL, tf), lambda g, f, ge, gr: (ge[g], 0, chunk(g, f, gr))),
            pl.BlockSpec((None, tf, D_MODEL), lambda g, f, ge, gr: (ge[g], chunk(g, f, gr), 0)),
        ],
        out_specs=pl.BlockSpec((GROUP_ROWS, D_MODEL), lambda g, f, ge, gr: (g, 0)),
        scratch_shapes=[pltpu.VMEM((GROUP_ROWS, D_MODEL), F32),
                        pltpu.VMEM((D_MODEL, tf), BF16), pltpu.VMEM((D_MODEL, tf), BF16),
                        pltpu.VMEM((tf, D_MODEL), BF16)],
    )
    return pl.pallas_call(
        _experts_kernel,
        out_shape=jax.ShapeDtypeStruct(hs.shape, BF16),
        grid_spec=grid_spec,
        compiler_params=_cparams(("arbitrary", "arbitrary")),
        name="expert_swiglu",
    )(g_expert, g_rows, hs, w1, w3, w2)


def _combine_kernel(tbl_ref, seg_ref, off_ref, rows_hbm, pos_ref, gates_ref, x_ref, mod_ref, out_ref, buf_ref, sem):
    b = pl.program_id(0)
    n_seq = pl.num_programs(0)
    tr = TR_MOE

    def all_segments(seq, action):
        def body(e, first):
            off = off_ref[seq * N_EXPERTS + e]
            seg = seg_ref[seq * N_EXPERTS + e]

            def make_copy(r0, rows):
                return pltpu.make_async_copy(
                    rows_hbm.at[pl.ds(pl.multiple_of(off + r0, SEG_ALIGN), rows), :],
                    buf_ref.at[seq % 2, pl.ds(pl.multiple_of(first + r0, SEG_ALIGN), rows), :],
                    sem.at[seq % 2, e])
            _segment_copies(seg, make_copy, action)
            return first + seg
        lax.fori_loop(0, N_EXPERTS, body, 0)

    @pl.when(b == 0)
    def _():
        buf_ref[...] = jnp.zeros_like(buf_ref)
        all_segments(b, "start")

    @pl.when(b + 1 < n_seq)
    def _():
        all_segments(b + 1, "start")

    out_ref[...] = x_ref[...]
    lane = lax.broadcasted_iota(jnp.int32, (tr, LANES), 1)
    layer_gate = mod_ref[5:6, :]
    all_segments(b, "wait")

    def expert(e, seg_start):
        bounds, win, all_small = _block_windows(tbl_ref, b, e, SEG_ALIGN)

        def scatter(tb, first, height):
            ts = slice(tb * tr, (tb + 1) * tr)
            pos_col = jnp.sum(jnp.where(lane == e, pos_ref[ts, :], 0.0), axis=-1, keepdims=True)
            gate_col = jnp.sum(jnp.where(lane == e, gates_ref[ts, :], 0.0), axis=-1, keepdims=True)
            slot_id = lax.broadcasted_iota(jnp.int32, (tr, height), 1).astype(F32) + float(first)
            sel = jnp.where(pos_col - win[tb].astype(F32) == slot_id, 1.0, 0.0).astype(BF16)
            rows = buf_ref[b % 2, pl.ds(pl.multiple_of(seg_start + win[tb] + first, SEG_ALIGN), height), :]
            out_ref[ts, :] += layer_gate * (gate_col * _dot(sel, rows))

        @pl.when(all_small)
        def _():
            for tb in range(TOK_BLOCKS):
                scatter(tb, 0, SMALL_WINDOW)

        @pl.when(jnp.logical_not(all_small))
        def _():
            for tb in range(TOK_BLOCKS):
                scatter(tb, 0, tr)
            for tb in range(TOK_BLOCKS):
                @pl.when(bounds[tb + 1] - win[tb] > tr)
                def _():
                    scatter(tb, tr, SPILL_ROWS)
        return seg_start + seg_ref[b * N_EXPERTS + e]
    lax.fori_loop(0, N_EXPERTS, expert, 0)


def _combine(ys, pos, gates, x, mod, bounds, seg, seg_off):
    tokens = x.shape[0]
    grid_spec = pltpu.PrefetchScalarGridSpec(
        num_scalar_prefetch=3,
        grid=(tokens // SEQ,),
        in_specs=[pl.BlockSpec(memory_space=pl.ANY),
                  pl.BlockSpec((SEQ, LANES), lambda b, *_: (b, 0)),
                  pl.BlockSpec((SEQ, LANES), lambda b, *_: (b, 0)),
                  pl.BlockSpec((SEQ, D_MODEL), lambda b, *_: (b, 0)),
                  pl.BlockSpec((None, 6, D_MODEL), lambda b, *_: (b, 0, 0))],
        out_specs=pl.BlockSpec((SEQ, D_MODEL), lambda b, *_: (b, 0)),
        scratch_shapes=[pltpu.VMEM((2, PACKED_ROWS, D_MODEL), BF16),
                        pltpu.SemaphoreType.DMA((2, N_EXPERTS))],
    )
    return pl.pallas_call(
        _combine_kernel,
        out_shape=jax.ShapeDtypeStruct((tokens, D_MODEL), F32),
        grid_spec=grid_spec,
        compiler_params=_cparams(("arbitrary",)),
        name="expert_combine",
    )(bounds, seg, seg_off, ys, pos, gates, x, mod)


def _token_mixer(x, mod, gain, w_in, layer, q_gain, k_gain, ret_gain, bias_masks):
    *qkv, ret_in = _in_proj(x, mod, gain, w_in, layer, q_gain, k_gain)
    return _dilated_attention(qkv, bias_masks), _retention(ret_in, ret_gain)


def _out_proj_moe_ffn(attn, ret, x, mod, w_out, layer, gain, w_router, w1, w3, w2):
    x, h, gates, pos, post, start, cnt = _out_proj_router(attn, ret, x, mod, w_out, layer, gain, w_router)
    bounds, seg, seg_off, g_expert, g_rows, n_groups = _moe_plan(start, cnt)
    hs = _dispatch(h, post, bounds, seg, seg_off, n_groups * GROUP_ROWS)
    ys = _experts(hs, g_expert, g_rows, n_groups, w1, w3, w2)
    return _combine(ys, pos, gates, x, mod, bounds, seg, seg_off)


def kernel(x, c, rel_bias_table, norm_mix, norm_ffn, w_mod, b_mod, w_in, q_gain, k_gain, ret_gain, w_out,
           ffn_w1, ffn_w3, ffn_w2, moe_router, moe_w1, moe_w3, moe_w2):
    batch, seq, d_model = x.shape
    assert (seq, d_model) == (SEQ, D_MODEL)
    depth = w_mod.shape[0]
    mods = _modulation(c, w_mod, b_mod).reshape(depth, batch, 6, D_MODEL)
    bias_masks = _bias_masks(rel_bias_table)
    xt = x.reshape(batch * seq, d_model)
    for layer in range(depth):
        mod = mods[layer]
        attn, ret = _token_mixer(xt, mod, norm_mix[layer], w_in, layer, q_gain[layer], k_gain[layer],
                                 ret_gain[layer], bias_masks)
        i = layer // 2
        if layer % 2 == 0:
            xt = _out_proj_dense_ffn(attn, ret, xt, mod, w_out, layer, norm_ffn[layer],
                                     ffn_w1, ffn_w3, ffn_w2, i)
        else:
            xt = _out_proj_moe_ffn(attn, ret, xt, mod, w_out, layer, norm_ffn[layer],
                                   moe_router[i], moe_w1[i], moe_w3[i], moe_w2[i])
    return xt.reshape(batch, seq, d_model)
```

```python
import functools
import math

import jax
import jax.numpy as jnp
import numpy as np
from jax import lax
from jax.experimental import pallas as pl
from jax.experimental.pallas import tpu as pltpu

D_MODEL = 1024
SEQ = 2048
ATTN_HEADS = 8
ATTN_HEAD_DIM = 64
ATTN_WIDTH = ATTN_HEADS * ATTN_HEAD_DIM
DILATED_PATTERNS = ((128, 1), (512, 4), (2048, 16))
BLOCK = 128
NUM_BUCKETS = 32
MAX_DISTANCE = 2048
RET_HEADS = 4
RET_KEY_DIM = 64
RET_VALUE_DIM = 128
RET_WIDTH = RET_HEADS * RET_VALUE_DIM
RET_QK_WIDTH = RET_HEADS * RET_KEY_DIM
RET_CHUNK = 128
ROPE_BASE = 10000.0
IN_WIDTH = 3 * ATTN_WIDTH + 2 * RET_QK_WIDTH + 2 * RET_WIDTH
RET_IN_WIDTH = IN_WIDTH - 3 * ATTN_WIDTH
N_EXPERTS = 8
EPS = 1e-6
NEG_INF = -1e30

LANES = 128
VMEM_LIMIT = 60 * 1024 * 1024

BF16 = jnp.bfloat16
F32 = jnp.float32

TM_PROJ = 256
TM_FFN = 512
MXU_COLS = 256
TR_MOE = 256
TF_MOE = 512
assert TM_PROJ == TR_MOE
RET_ROWS = 512


def _cparams(sem):
    return pltpu.CompilerParams(dimension_semantics=sem, vmem_limit_bytes=VMEM_LIMIT)


def _dot(a, b):
    return jnp.dot(a, b, preferred_element_type=F32)


def _dot_nt(a, b):
    return lax.dot_general(a, b, (((1,), (1,)), ((), ())), preferred_element_type=F32)


def _dot_tn(a, b):
    return lax.dot_general(a, b, (((0,), (0,)), ((), ())), preferred_element_type=F32)


def _split_bf16(v):
    hi = v.astype(BF16)
    lo = (v - hi.astype(F32)).astype(BF16)
    return hi, lo


def _silu(v):
    return v * (1.0 / (1.0 + jnp.exp(-v)))


def _modulated_norm(x, gain, scale, shift):
    ms = jnp.mean(x * x, axis=-1, keepdims=True)
    y = x * lax.rsqrt(ms + EPS) * gain
    return y * (1.0 + scale) + shift


def _mod_kernel(c_ref, w_ref, b_ref, o_ref):
    ca = _silu(c_ref[...]).astype(BF16)
    o_ref[...] = _dot(ca, w_ref[...].astype(BF16)) + b_ref[...]


def _modulation(c, w_mod, b_mod):
    depth, _, width = w_mod.shape
    batch = c.shape[0]
    tn = 1536
    return pl.pallas_call(
        _mod_kernel,
        out_shape=jax.ShapeDtypeStruct((depth, batch, width), F32),
        grid=(depth, width // tn),
        in_specs=[
            pl.BlockSpec((batch, D_MODEL), lambda l, n: (0, 0)),
            pl.BlockSpec((None, D_MODEL, tn), lambda l, n: (l, 0, n)),
            pl.BlockSpec((None, 1, tn), lambda l, n: (l, 0, n)),
        ],
        out_specs=pl.BlockSpec((None, batch, tn), lambda l, n: (l, 0, n)),
        compiler_params=_cparams(("arbitrary", "arbitrary")),
        name="adaln_modulation",
    )(c, w_mod, b_mod.reshape(depth, 1, width))


def _bias_kernel(table_ref, bucket_ref, o_ref):
    bucket = bucket_ref[...]
    acc = [jnp.full(bucket.shape, NEG_INF, F32) for _ in range(ATTN_HEADS)]
    for b in range(NUM_BUCKETS):
        hit = bucket == b
        for h in range(ATTN_HEADS):
            acc[h] = jnp.where(hit, table_ref[b, h], acc[h])
    for h in range(ATTN_HEADS):
        o_ref[h // 2, (h % 2) * BLOCK:(h % 2 + 1) * BLOCK, :] = acc[h]


BIAS_FULL = {1: 0, 4: 2}
BIAS_FIRST = {1: 1, 4: 3, 16: 4}
N_BIAS_SETS = 5


def _bias_masks(rel_bias_table):
    i = jnp.arange(BLOCK)[:, None]
    j = jnp.arange(2 * BLOCK)[None, :]
    max_exact = NUM_BUCKETS // 2

    def bucket_of(rel, dilation, w_sub, exists):
        n = jnp.maximum(rel * dilation, 0)
        nf = jnp.maximum(n.astype(F32), float(max_exact))
        large = max_exact + (jnp.log(nf / max_exact) / math.log(MAX_DISTANCE / max_exact)
                             * (NUM_BUCKETS - max_exact)).astype(jnp.int32)
        large = jnp.minimum(large, NUM_BUCKETS - 1)
        bucket = jnp.where(n < max_exact, n, large)
        allowed = (rel >= 0) & (rel <= w_sub) & exists
        return jnp.where(allowed, bucket, -1)

    sets = [None] * N_BIAS_SETS
    for window, dilation in DILATED_PATTERNS:
        w_sub = window // dilation
        if dilation in BIAS_FULL:
            sets[BIAS_FULL[dilation]] = bucket_of(i - j + BLOCK, dilation, w_sub, j >= 0)
        sets[BIAS_FIRST[dilation]] = bucket_of(i - j, dilation, w_sub, j < BLOCK)
    buckets = jnp.stack(sets).astype(jnp.int32)
    return pl.pallas_call(
        _bias_kernel,
        out_shape=jax.ShapeDtypeStruct((N_BIAS_SETS, ATTN_HEADS // 2, 2 * BLOCK, 2 * BLOCK), F32),
        grid=(N_BIAS_SETS,),
        in_specs=[
            pl.BlockSpec(memory_space=pltpu.SMEM),
            pl.BlockSpec((None, BLOCK, 2 * BLOCK), lambda p: (p, 0, 0)),
        ],
        out_specs=pl.BlockSpec((None, ATTN_HEADS // 2, 2 * BLOCK, 2 * BLOCK), lambda p: (p, 0, 0, 0)),
        compiler_params=_cparams(("arbitrary",)),
        name="relative_bias_masks",
    )(rel_bias_table, buckets)


HALF = ATTN_WIDTH // 2


def _in_proj_kernel(x_ref, mod_ref, gain_ref, w_ref, qg_ref, kg_ref, grp_ref,
                    q1_ref, k1_ref, v1_ref, q4_ref, k4_ref, v4_ref, q16_ref, k16_ref, v16_ref, r_ref,
                    perm_ref):
    mod = mod_ref[...]
    h = _modulated_norm(x_ref[...], gain_ref[...], mod[1:2], mod[0:1]).astype(BF16)
    proj = _dot(h, w_ref[...].astype(BF16))
    grp = grp_ref[...]
    tm = proj.shape[0]

    def head_norm(t, gain):
        hi, lo = _split_bf16(t * t)
        ss = _dot(hi, grp) + _dot(lo, grp)
        return t * lax.rsqrt(ss * (1.0 / ATTN_HEAD_DIM) + EPS) * gain

    def emit(t, o1_ref, o4_ref, o16_ref):
        o1_ref[...] = t.astype(BF16)
        for j in range(ATTN_WIDTH // LANES):
            perm_ref[j] = t[:, j * LANES:(j + 1) * LANES]
        for dil, o_ref in ((4, o4_ref), (16, o16_ref)):
            for hf in range(2):
                for r in range(dil):
                    for jj in range(HALF // LANES):
                        c0 = (hf * dil + r) * HALF + jj * LANES
                        o_ref[:, c0:c0 + LANES] = perm_ref[hf * (HALF // LANES) + jj,
                                                           pl.ds(r, tm // dil, stride=dil), :].astype(BF16)

    emit(head_norm(proj[:, :ATTN_WIDTH], qg_ref[...]) * (ATTN_HEAD_DIM ** -0.5), q1_ref, q4_ref, q16_ref)
    emit(head_norm(proj[:, ATTN_WIDTH:2 * ATTN_WIDTH], kg_ref[...]), k1_ref, k4_ref, k16_ref)
    emit(proj[:, 2 * ATTN_WIDTH:3 * ATTN_WIDTH], v1_ref, v4_ref, v16_ref)
    r_ref[...] = proj[:, 3 * ATTN_WIDTH:]


def _in_proj(x, mod, gain, w_in, layer, q_gain, k_gain):
    tokens = x.shape[0]
    tm = TM_FFN
    per_seq = SEQ // tm
    grp = np.kron(np.eye(ATTN_HEADS), np.ones((ATTN_HEAD_DIM, ATTN_HEAD_DIM))).astype(np.float32)
    row = lambda i: (i, 0)
    const = lambda i: (0, 0)
    layouts = []
    for dil in (1, 4, 16):
        shape = jax.ShapeDtypeStruct((tokens // dil, dil * ATTN_WIDTH), BF16)
        spec = pl.BlockSpec((tm // dil, dil * ATTN_WIDTH), row)
        layouts.append(((shape,) * 3, (spec,) * 3))
    out_shape = sum((s for s, _ in layouts), ()) + (jax.ShapeDtypeStruct((tokens, RET_IN_WIDTH), F32),)
    out_specs = sum((s for _, s in layouts), ()) + (pl.BlockSpec((tm, RET_IN_WIDTH), row),)
    return pl.pallas_call(
        _in_proj_kernel,
        out_shape=out_shape,
        grid=(tokens // tm,),
        in_specs=[
            pl.BlockSpec((tm, D_MODEL), row),
            pl.BlockSpec((None, 6, D_MODEL), lambda i: (i // per_seq, 0, 0)),
            pl.BlockSpec((1, D_MODEL), const),
            pl.BlockSpec((None, D_MODEL, IN_WIDTH), lambda i: (layer, 0, 0), pipeline_mode=pl.Buffered(1)),
            pl.BlockSpec((1, ATTN_WIDTH), const),
            pl.BlockSpec((1, ATTN_WIDTH), const),
            pl.BlockSpec((ATTN_WIDTH, ATTN_WIDTH), const),
        ],
        out_specs=out_specs,
        scratch_shapes=[pltpu.VMEM((ATTN_WIDTH // LANES, tm, LANES), F32)],
        compiler_params=_cparams(("arbitrary",)),
        name="in_projection",
    )(x, mod, gain.reshape(1, D_MODEL), w_in,
      jnp.tile(q_gain, ATTN_HEADS).reshape(1, ATTN_WIDTH),
      jnp.tile(k_gain, ATTN_HEADS).reshape(1, ATTN_WIDTH),
      jnp.asarray(grp, BF16))


PAIRS_PER_HALF = ATTN_HEADS // 4
GROUP = 16


def _pair_scores(qp, kp, vp, bias2, masks, low):
    q2 = jnp.concatenate([qp * masks[0], qp * masks[1]], axis=0)
    s = _dot_nt(q2, kp) + bias2
    m = jnp.max(s, axis=-1, keepdims=True)
    p = jnp.exp(s - m)
    den = jnp.sum(p, axis=-1, keepdims=True)
    pv = _dot(p.astype(BF16), vp)
    pick = lambda t: jnp.where(low, t[:BLOCK], t[BLOCK:])
    return pick(pv), pick(m), pick(den)


def _attn_kernel(q1_ref, k1_ref, v1_ref, q4_ref, k4_ref, v4_ref, q16_ref, k16_ref, v16_ref, bm_ref,
                 o_ref, acc_ref, max_ref, den_ref):
    lane = lax.broadcasted_iota(jnp.int32, (BLOCK, LANES), 1)
    low = lane < ATTN_HEAD_DIM
    masks = (jnp.where(low, 1.0, 0.0).astype(BF16), jnp.where(low, 0.0, 1.0).astype(BF16))

    def block(q_ref, k_ref, v_ref, c0, q0, w0, width, bias_set, rows, first):
        for p in range(PAIRS_PER_HALF):
            cs = slice(c0 + p * LANES, c0 + (p + 1) * LANES)
            acc, m, den = _pair_scores(q_ref[pl.ds(q0, BLOCK), cs], k_ref[pl.ds(w0, width), cs],
                                       v_ref[pl.ds(w0, width), cs], bm_ref[bias_set, p, :, 0:width],
                                       masks, low)
            if not first:
                m_old = max_ref[p, rows, :]
                m_new = jnp.maximum(m_old, m)
                a, b = jnp.exp(m_old - m_new), jnp.exp(m - m_new)
                den = den_ref[p, rows, :] * a + den * b
                acc = acc_ref[p, rows, :] * a + acc * b
                m = m_new
            max_ref[p, rows, :] = m
            den_ref[p, rows, :] = den
            acc_ref[p, rows, :] = acc

    def d1_group(g, carry):
        for u in range(GROUP):
            n = g * GROUP + u
            q0 = pl.multiple_of(n * BLOCK, BLOCK)
            w0 = pl.multiple_of(jnp.maximum(n - 1, 0) * BLOCK, BLOCK)
            bias_set = jnp.where(n == 0, BIAS_FIRST[1], BIAS_FULL[1])
            block(q1_ref, k1_ref, v1_ref, 0, q0, w0, 2 * BLOCK, bias_set, pl.ds(q0, BLOCK), True)
        return carry
    lax.fori_loop(0, SEQ // BLOCK // GROUP, d1_group, 0)

    for r in range(4):
        for n in range(SEQ // 4 // BLOCK):
            block(q4_ref, k4_ref, v4_ref, r * HALF, n * BLOCK, max(n - 1, 0) * BLOCK, 2 * BLOCK,
                  BIAS_FIRST[4] if n == 0 else BIAS_FULL[4],
                  pl.ds(r + 4 * BLOCK * n, BLOCK, stride=4), False)

    for r in range(16):
        block(q16_ref, k16_ref, v16_ref, r * HALF, 0, 0, BLOCK, BIAS_FIRST[16],
              pl.ds(r, BLOCK, stride=16), False)

    for n in range(SEQ // BLOCK):
        rows = slice(n * BLOCK, (n + 1) * BLOCK)
        for p in range(PAIRS_PER_HALF):
            o_ref[rows, p * LANES:(p + 1) * LANES] = (acc_ref[p, rows, :] / den_ref[p, rows, :]).astype(BF16)


def _dilated_attention(qkv, bias_masks):
    tokens = qkv[0].shape[0]
    batch = tokens // SEQ
    specs = []
    for dil in (1, 4, 16):
        specs += [pl.BlockSpec((SEQ // dil, dil * HALF), lambda b, hf: (b, hf))] * 3
    state = pltpu.VMEM((PAIRS_PER_HALF, SEQ, LANES), F32)
    return pl.pallas_call(
        _attn_kernel,
        out_shape=jax.ShapeDtypeStruct((tokens, ATTN_WIDTH), BF16),
        grid=(batch, 2),
        in_specs=specs + [pl.BlockSpec((N_BIAS_SETS, PAIRS_PER_HALF, 2 * BLOCK, 2 * BLOCK),
                                       lambda b, hf: (0, hf, 0, 0))],
        out_specs=pl.BlockSpec((SEQ, HALF), lambda b, hf: (b, hf)),
        scratch_shapes=[state, state, state],
        compiler_params=_cparams(("arbitrary", "arbitrary")),
        name="dilated_attention",
    )(*qkv, bias_masks)


def _retention_kernel(r_ref, cos_ref, sin_ref, dmask_ref, qdec_ref, kdec_ref, cdec_ref, gain_ref,
                      o_ref, state_ref):
    @pl.when(pl.program_id(1) == 0)
    def _():
        state_ref[...] = jnp.zeros_like(state_ref)

    lane = lax.broadcasted_iota(jnp.int32, (RET_CHUNK, LANES), 1)
    low = lane < RET_KEY_DIM
    first_half = (lane % RET_KEY_DIM) < (RET_KEY_DIM // 2)

    def rotate(t, cos, sin):
        partner = jnp.where(first_half, pltpu.roll(t, LANES - RET_KEY_DIM // 2, 1),
                            pltpu.roll(t, RET_KEY_DIM // 2, 1))
        return t * cos + partner * sin

    for c in range(RET_ROWS // RET_CHUNK):
        rows = slice(c * RET_CHUNK, (c + 1) * RET_CHUNK)
        for hp in range(RET_HEADS // 2):
            qs = slice(hp * LANES, (hp + 1) * LANES)
            ks = slice(RET_QK_WIDTH + hp * LANES, RET_QK_WIDTH + (hp + 1) * LANES)
            cos, sin = cos_ref[rows, qs], sin_ref[rows, qs]
            q_pair = rotate(r_ref[rows, qs], cos, sin) * (RET_KEY_DIM ** -0.5)
            k_pair = rotate(r_ref[rows, ks], cos, sin)
            for hh in range(2):
                head = 2 * hp + hh
                vs = slice(2 * RET_QK_WIDTH + head * LANES, 2 * RET_QK_WIDTH + (head + 1) * LANES)
                gs = slice(2 * RET_QK_WIDTH + RET_WIDTH + head * LANES,
                           2 * RET_QK_WIDTH + RET_WIDTH + (head + 1) * LANES)
                keep = low if hh == 0 else jnp.logical_not(low)
                qm = jnp.where(keep, q_pair, 0.0)
                vb = r_ref[rows, vs].astype(BF16)
                state = state_ref[head]
                inner = _dot_nt(qm.astype(BF16), k_pair.astype(BF16)) * dmask_ref[head]
                y = _dot(inner.astype(BF16), vb)
                y = y + _dot((qm * qdec_ref[head]).astype(BF16), state.astype(BF16))
                state_ref[head] = state * cdec_ref[head] + _dot_tn((k_pair * kdec_ref[head]).astype(BF16), vb)
                mu = jnp.mean(y, axis=-1, keepdims=True)
                yc = y - mu
                var = jnp.mean(yc * yc, axis=-1, keepdims=True)
                yn = yc * lax.rsqrt(var + EPS) * gain_ref[:, head * LANES:(head + 1) * LANES]
                o_ref[rows, head * LANES:(head + 1) * LANES] = (_silu(r_ref[rows, gs]) * yn).astype(BF16)


def _retention_tables():
    half = RET_KEY_DIM // 2
    pos = jnp.arange(SEQ, dtype=F32)
    inv = ROPE_BASE ** (-jnp.arange(half, dtype=F32) / half)
    ang = pos[:, None] * inv[None, :]
    cos, sin = jnp.cos(ang), jnp.sin(ang)
    cos_full = jnp.tile(jnp.concatenate([cos, cos], axis=-1), (1, RET_HEADS))
    sin_signed = jnp.tile(jnp.concatenate([-sin, sin], axis=-1), (1, RET_HEADS))
    log_g = jnp.log(1.0 - 2.0 ** (-5.0 - jnp.arange(RET_HEADS, dtype=F32)))
    idx = jnp.arange(RET_CHUNK, dtype=F32)
    diff = idx[:, None] - idx[None, :]
    dmask = jnp.where(diff >= 0, jnp.exp(jnp.maximum(diff, 0.0)[None] * log_g[:, None, None]), 0.0)
    q_decay = jnp.exp((idx + 1.0)[None, :] * log_g[:, None])[..., None]
    k_decay = jnp.exp((RET_CHUNK - 1.0 - idx)[None, :] * log_g[:, None])[..., None]
    chunk_decay = jnp.exp(RET_CHUNK * log_g)[:, None, None]
    full = (RET_HEADS, RET_CHUNK, LANES)
    return (cos_full, sin_signed, dmask, jnp.broadcast_to(q_decay, full),
            jnp.broadcast_to(k_decay, full), jnp.broadcast_to(chunk_decay, full))


def _retention(ret_in, ret_gain):
    tokens = ret_in.shape[0]
    batch = tokens // SEQ
    per_seq = SEQ // RET_ROWS
    cos, sin, dmask, qdec, kdec, cdec = _retention_tables()
    tab = pl.BlockSpec((RET_ROWS, RET_QK_WIDTH), lambda b, j: (j, 0))
    const3 = pl.BlockSpec((RET_HEADS, RET_CHUNK, LANES), lambda b, j: (0, 0, 0))
    return pl.pallas_call(
        _retention_kernel,
        out_shape=jax.ShapeDtypeStruct((tokens, RET_WIDTH), BF16),
        grid=(batch, per_seq),
        in_specs=[
            pl.BlockSpec((RET_ROWS, RET_IN_WIDTH), lambda b, j: (b * per_seq + j, 0)),
            tab, tab, const3, const3, const3, const3,
            pl.BlockSpec((1, RET_WIDTH), lambda b, j: (0, 0)),
        ],
        out_specs=pl.BlockSpec((RET_ROWS, RET_WIDTH), lambda b, j: (b * per_seq + j, 0)),
        scratch_shapes=[pltpu.VMEM((RET_HEADS, LANES, RET_VALUE_DIM), F32)],
        compiler_params=_cparams(("arbitrary", "arbitrary")),
        name="retention",
    )(ret_in, cos, sin, dmask, qdec, kdec, cdec, ret_gain.reshape(1, RET_WIDTH))


def _swiglu_chunks(h, w1_ref, w3_ref, w2_ref):
    d_ff = w1_ref.shape[-1]
    total = None
    for c0 in range(0, d_ff, MXU_COLS):
        c1 = min(c0 + MXU_COLS, d_ff)
        z = (_silu(_dot(h, w1_ref[:, c0:c1].astype(BF16))) * _dot(h, w3_ref[:, c0:c1].astype(BF16))).astype(BF16)
        part = _dot(z, w2_ref[c0:c1, :].astype(BF16))
        total = part if total is None else total + part
    return total


def _out_proj_ffn_kernel(attn_ref, ret_ref, x_ref, mod_ref, wo_ref, gain_ref, w1_ref, w3_ref, w2_ref, out_ref):
    mod = mod_ref[...]
    mix = (_dot(attn_ref[...], wo_ref[:ATTN_WIDTH, :].astype(BF16))
           + _dot(ret_ref[...], wo_ref[ATTN_WIDTH:, :].astype(BF16)))
    x = x_ref[...] + mod[2:3, :] * mix
    h = _modulated_norm(x, gain_ref[...], mod[4:5], mod[3:4]).astype(BF16)
    out_ref[...] = x + mod[5:6, :] * _swiglu_chunks(h, w1_ref, w3_ref, w2_ref)


def _out_proj_dense_ffn(attn, ret, x, mod, w_out, layer, gain, w1, w3, w2, index):
    tokens = x.shape[0]
    d_ff = w1.shape[2]
    tm = TM_FFN
    per_seq = SEQ // tm
    row = lambda i: (i, 0)
    resident = pl.Buffered(1)
    return pl.pallas_call(
        _out_proj_ffn_kernel,
        out_shape=jax.ShapeDtypeStruct((tokens, D_MODEL), F32),
        grid=(tokens // tm,),
        in_specs=[
            pl.BlockSpec((tm, ATTN_WIDTH), row),
            pl.BlockSpec((tm, RET_WIDTH), row),
            pl.BlockSpec((tm, D_MODEL), row),
            pl.BlockSpec((None, 6, D_MODEL), lambda i: (i // per_seq, 0, 0)),
            pl.BlockSpec((None, D_MODEL, D_MODEL), lambda i: (layer, 0, 0), pipeline_mode=resident),
            pl.BlockSpec((1, D_MODEL), lambda i: (0, 0)),
            pl.BlockSpec((None, D_MODEL, d_ff), lambda i: (index, 0, 0), pipeline_mode=resident),
            pl.BlockSpec((None, D_MODEL, d_ff), lambda i: (index, 0, 0), pipeline_mode=resident),
            pl.BlockSpec((None, d_ff, D_MODEL), lambda i: (index, 0, 0), pipeline_mode=resident),
        ],
        out_specs=pl.BlockSpec((tm, D_MODEL), row),
        compiler_params=_cparams(("arbitrary",)),
        name="out_projection_dense_swiglu",
    )(attn, ret, x, mod, w_out, gain.reshape(1, D_MODEL), w1, w3, w2)


def _router_kernel(attn_ref, ret_ref, x_ref, mod_ref, wo_ref, gain_ref, wr_ref,
                   xo_ref, h_ref, gates_ref, pos_ref, post_ref, start_ref, cnt_ref, carry_ref, *, tiles_per_seq):
    i = pl.program_id(0)

    @pl.when(i % tiles_per_seq == 0)
    def _():
        carry_ref[...] = jnp.zeros_like(carry_ref)

    mod = mod_ref[...]
    mix = (_dot(attn_ref[...], wo_ref[:ATTN_WIDTH, :].astype(BF16))
           + _dot(ret_ref[...], wo_ref[ATTN_WIDTH:, :].astype(BF16)))
    x = x_ref[...] + mod[2:3, :] * mix
    xo_ref[...] = x
    h = _modulated_norm(x, gain_ref[...], mod[4:5], mod[3:4]).astype(BF16)
    h_ref[...] = h
    tm = h.shape[0]
    lane = lax.broadcasted_iota(jnp.int32, (tm, LANES), 1).astype(F32)
    logits = jnp.where(lane < N_EXPERTS, _dot(h, wr_ref[...]), -jnp.inf)
    m1 = jnp.max(logits, axis=-1, keepdims=True)
    i1 = jnp.min(jnp.where(logits == m1, lane, float(LANES)), axis=-1, keepdims=True)
    rest = jnp.where(lane == i1, -jnp.inf, logits)
    m2 = jnp.max(rest, axis=-1, keepdims=True)
    i2 = jnp.min(jnp.where(rest == m2, lane, float(LANES)), axis=-1, keepdims=True)
    e2 = jnp.exp(m2 - m1)
    g1 = 1.0 / (1.0 + e2)
    g2 = e2 / (1.0 + e2)
    gates_ref[...] = jnp.where(lane == i1, g1, 0.0) + jnp.where(lane == i2, g2, 0.0)
    chosen = (lane == i1) | (lane == i2)
    onehot = jnp.where(chosen, 1.0, 0.0)
    r = lax.broadcasted_iota(jnp.int32, (tm, tm), 0)
    c = lax.broadcasted_iota(jnp.int32, (tm, tm), 1)
    tril = jnp.where(c <= r, 1.0, 0.0).astype(BF16)
    incl = _dot(tril, onehot.astype(BF16))
    carry = carry_ref[0:1, :]
    start_ref[...] = carry_ref[...]
    pos = jnp.where(chosen, incl - 1.0 + carry, -1.0)
    pos_ref[...] = pos
    post_ref[...] = pos.T[:N_EXPERTS, :]
    total = carry + incl[tm - 1:tm, :]
    carry_ref[...] = jnp.broadcast_to(total, carry_ref.shape)
    cnt_ref[...] = jnp.broadcast_to(total, cnt_ref.shape)


def _out_proj_router(attn, ret, x, mod, w_out, layer, gain, w_router):
    tokens = x.shape[0]
    batch = tokens // SEQ
    tm = TM_PROJ
    per_seq = SEQ // tm
    wr = jnp.zeros((D_MODEL, LANES), BF16).at[:, :N_EXPERTS].set(w_router.astype(BF16))
    row = lambda i: (i, 0)
    return pl.pallas_call(
        functools.partial(_router_kernel, tiles_per_seq=per_seq),
        out_shape=(
            jax.ShapeDtypeStruct((tokens, D_MODEL), F32),
            jax.ShapeDtypeStruct((tokens, D_MODEL), BF16),
            jax.ShapeDtypeStruct((tokens, LANES), F32),
            jax.ShapeDtypeStruct((tokens, LANES), F32),
            jax.ShapeDtypeStruct((N_EXPERTS, tokens), F32),
            jax.ShapeDtypeStruct((tokens // tm, 8, LANES), F32),
            jax.ShapeDtypeStruct((batch, 8, LANES), F32),
        ),
        grid=(tokens // tm,),
        in_specs=[
            pl.BlockSpec((tm, ATTN_WIDTH), row),
            pl.BlockSpec((tm, RET_WIDTH), row),
            pl.BlockSpec((tm, D_MODEL), row),
            pl.BlockSpec((None, 6, D_MODEL), lambda i: (i // per_seq, 0, 0)),
            pl.BlockSpec((None, D_MODEL, D_MODEL), lambda i: (layer, 0, 0)),
            pl.BlockSpec((1, D_MODEL), lambda i: (0, 0)),
            pl.BlockSpec((D_MODEL, LANES), lambda i: (0, 0)),
        ],
        out_specs=(
            pl.BlockSpec((tm, D_MODEL), row),
            pl.BlockSpec((tm, D_MODEL), row),
            pl.BlockSpec((tm, LANES), row),
            pl.BlockSpec((tm, LANES), row),
            pl.BlockSpec((N_EXPERTS, tm), lambda i: (0, i)),
            pl.BlockSpec((None, 8, LANES), lambda i: (i, 0, 0)),
            pl.BlockSpec((None, 8, LANES), lambda i: (i // per_seq, 0, 0)),
        ),
        scratch_shapes=[pltpu.VMEM((8, LANES), F32)],
        compiler_params=_cparams(("arbitrary",)),
        name="out_projection_router",
    )(attn, ret, x, mod, w_out, gain.reshape(1, D_MODEL), wr)


TOK_BLOCKS = SEQ // TR_MOE
TAIL_TILES = (64, 128, TR_MOE)
SPILL_ROWS = 16
SMALL_WINDOW = 128
BIG_TILE = 1024
SEG_ALIGN = 16
GROUP_ROWS = 3072
PACKED_ROWS = 2 * SEQ + N_EXPERTS * SEG_ALIGN + 2 * TR_MOE
BIG_CHUNK = 128
N_BOUNDS = TOK_BLOCKS + 1


def _moe_plan(start, cnt):
    batch = cnt.shape[0]
    counts = cnt[:, 0, :N_EXPERTS].astype(jnp.int32)
    bounds = jnp.concatenate([start[:, 0, :N_EXPERTS].reshape(batch, TOK_BLOCKS, N_EXPERTS).astype(jnp.int32),
                              counts[:, None, :]], axis=1)
    seg = (counts + (SEG_ALIGN - 1)) // SEG_ALIGN * SEG_ALIGN
    rows_e = jnp.sum(seg, axis=0)
    groups_e = (rows_e + (GROUP_ROWS - 1)) // GROUP_ROWS
    first_group = jnp.cumsum(groups_e) - groups_e
    seg_off = first_group[None, :] * GROUP_ROWS + jnp.cumsum(seg, axis=0) - seg
    max_rows = batch * (2 * SEQ + N_EXPERTS * (SEG_ALIGN - 1))
    n_groups = max_rows // GROUP_ROWS + N_EXPERTS
    g = jnp.arange(n_groups)
    g_expert = jnp.minimum(jnp.sum(g[:, None] >= jnp.cumsum(groups_e)[None, :], axis=1), N_EXPERTS - 1)
    g_rows = jnp.clip(rows_e[g_expert] - (g - first_group[g_expert]) * GROUP_ROWS, 0, GROUP_ROWS)
    g_rows = jnp.where(g < jnp.sum(groups_e), g_rows, 0)
    return (bounds.reshape(-1), seg.reshape(-1), seg_off.reshape(-1),
            g_expert.astype(jnp.int32), g_rows.astype(jnp.int32), n_groups)


def _segment_copies(seg, make_copy, action):
    n_big = lax.shift_right_logical(seg, BIG_CHUNK.bit_length() - 1)

    def big(i, carry):
        getattr(make_copy(pl.multiple_of(i * BIG_CHUNK, BIG_CHUNK), BIG_CHUNK), action)()
        return carry
    lax.fori_loop(0, n_big, big, 0)
    rest0 = n_big * BIG_CHUNK
    n_small = lax.shift_right_logical(seg - rest0, SEG_ALIGN.bit_length() - 1)

    def small(i, carry):
        copy = make_copy(pl.multiple_of(rest0 + i * SEG_ALIGN, SEG_ALIGN), SEG_ALIGN)
        if action == "start":
            copy.start(priority=1)
        else:
            copy.wait()
        return carry
    lax.fori_loop(0, n_small, small, 0)


def _block_windows(tbl_ref, b, e, align):
    base = b * N_BOUNDS * N_EXPERTS + e
    bounds = [tbl_ref[base + tb * N_EXPERTS] for tb in range(N_BOUNDS)]
    shift = align.bit_length() - 1
    win = [pl.multiple_of(lax.shift_left(lax.shift_right_logical(s, shift), shift), align) for s in bounds[:-1]]
    all_small = functools.reduce(jnp.logical_and,
                                 [bounds[tb + 1] - win[tb] <= SMALL_WINDOW for tb in range(TOK_BLOCKS)])
    return bounds, win, all_small


def _dispatch_kernel(tbl_ref, seg_ref, off_ref, h_ref, post_ref, rows_hbm, stage_ref, sem):
    b = pl.program_id(0)
    n_seq = pl.num_programs(0)
    tr = TR_MOE

    def all_segments(seq, action):
        def body(e, first):
            off = off_ref[seq * N_EXPERTS + e]
            seg = seg_ref[seq * N_EXPERTS + e]

            def make_copy(r0, rows):
                return pltpu.make_async_copy(
                    stage_ref.at[seq % 2, pl.ds(pl.multiple_of(first + r0, SEG_ALIGN), rows), :],
                    rows_hbm.at[pl.ds(pl.multiple_of(off + r0, SEG_ALIGN), rows), :],
                    sem.at[seq % 2])
            _segment_copies(seg, make_copy, action)
            return first + seg
        lax.fori_loop(0, N_EXPERTS, body, 0)

    @pl.when(b >= 1)
    def _():
        all_segments(b - 1, "wait")

    stage_ref[b % 2] = jnp.zeros(stage_ref.shape[1:], BF16)

    def expert(e, first):
        bounds, win, all_small = _block_windows(tbl_ref, b, e, SEG_ALIGN)

        def gather(height):
            slot_id = lax.broadcasted_iota(jnp.int32, (height, tr), 0).astype(F32)
            for tb in range(TOK_BLOCKS):
                ts = slice(tb * tr, (tb + 1) * tr)
                local = post_ref[pl.ds(e, 1), ts] - win[tb].astype(F32)
                sel = jnp.where(local == slot_id, 1.0, 0.0).astype(BF16)
                rows = pl.ds(pl.multiple_of(first + win[tb], SEG_ALIGN), height)
                stage_ref[b % 2, rows, :] += _dot(sel, h_ref[ts, :]).astype(BF16)

        @pl.when(all_small)
        def _():
            gather(SMALL_WINDOW)

        @pl.when(jnp.logical_not(all_small))
        def _():
            gather(tr + SPILL_ROWS)
        return first + seg_ref[b * N_EXPERTS + e]
    lax.fori_loop(0, N_EXPERTS, expert, 0)
    all_segments(b, "start")

    @pl.when(b == n_seq - 1)
    def _():
        all_segments(b, "wait")


def _dispatch(h, post, bounds, seg, seg_off, total_rows):
    tokens = h.shape[0]
    grid_spec = pltpu.PrefetchScalarGridSpec(
        num_scalar_prefetch=3,
        grid=(tokens // SEQ,),
        in_specs=[pl.BlockSpec((SEQ, D_MODEL), lambda b, *_: (b, 0)),
                  pl.BlockSpec((N_EXPERTS, SEQ), lambda b, *_: (0, b))],
        out_specs=pl.BlockSpec(memory_space=pl.ANY),
        scratch_shapes=[pltpu.VMEM((2, PACKED_ROWS, D_MODEL), BF16),
                        pltpu.SemaphoreType.DMA((2,))],
    )
    return pl.pallas_call(
        _dispatch_kernel,
        out_shape=jax.ShapeDtypeStruct((total_rows, D_MODEL), BF16),
        grid_spec=grid_spec,
        compiler_params=_cparams(("arbitrary",)),
        name="expert_dispatch",
    )(bounds, seg, seg_off, h, post)


def _experts_kernel(ge_ref, rows_ref, hs_ref, w1_ref, w3_ref, w2_ref, ys_ref, acc_ref, wb1_ref, wb3_ref, wb2_ref):
    g, f = pl.program_id(0), pl.program_id(1)
    last_f = pl.num_programs(1) - 1
    tr = TR_MOE
    rows = rows_ref[g]
    n_big = lax.shift_right_logical(rows, BIG_TILE.bit_length() - 1)
    n_full = lax.shift_right_logical(rows, tr.bit_length() - 1)
    rest = rows - n_full * tr
    n_tiles = lax.shift_right_logical(rows + (tr - 1), tr.bit_length() - 1)
    tail0 = pl.multiple_of(n_full * tr, tr)

    @pl.when(rows > 0)
    def _():
        wb1_ref[...] = w1_ref[...].astype(BF16)
        wb3_ref[...] = w3_ref[...].astype(BF16)
        wb2_ref[...] = w2_ref[...].astype(BF16)

    @pl.when(f == 0)
    def _():
        def clear(r, carry):
            acc_ref[pl.ds(pl.multiple_of(r * tr, tr), tr), :] = jnp.zeros((tr, D_MODEL), F32)
            return carry
        lax.fori_loop(0, n_tiles, clear, 0)

    def swiglu(hr, r0, m):
        if m > tr:
            acc_ref[pl.ds(r0, m), :] += _swiglu_chunks(hr, wb1_ref, wb3_ref, wb2_ref)
        else:
            z = (_silu(_dot(hr, wb1_ref[...])) * _dot(hr, wb3_ref[...])).astype(BF16)
            acc_ref[pl.ds(r0, m), :] += _dot(z, wb2_ref[...])

    def big_tile(r, carry):
        r0 = pl.multiple_of(r * BIG_TILE, BIG_TILE)
        swiglu(hs_ref[pl.ds(r0, BIG_TILE), :], r0, BIG_TILE)
        return carry
    lax.fori_loop(0, n_big, big_tile, 0)

    done = n_big * BIG_TILE
    m = BIG_TILE // 2
    while m >= tr:
        take = (rows - done) >= m

        @pl.when(take)
        def _(done=done, m=m):
            r0 = pl.multiple_of(done, tr)
            swiglu(hs_ref[pl.ds(r0, m), :], r0, m)
        done = done + jnp.where(take, m, 0)
        m //= 2
    lo = 0
    for m in TAIL_TILES:
        @pl.when((rest > lo) & (rest <= m))
        def _():
            valid = lax.broadcasted_iota(jnp.int32, (m, D_MODEL), 0) < rest
            hr = hs_ref[pl.ds(tail0, m), :]
            swiglu(jnp.where(valid, hr, jnp.zeros_like(hr)), tail0, m)
        lo = m

    @pl.when(f == last_f)
    def _():
        ys_ref[...] = jnp.zeros_like(ys_ref)

        def store(r, carry):
            rr = pl.ds(pl.multiple_of(r * tr, tr), tr)
            ys_ref[rr, :] = acc_ref[rr, :].astype(BF16)
            return carry
        lax.fori_loop(0, n_tiles, store, 0)


def _experts(hs, g_expert, g_rows, n_groups, w1, w3, w2):
    d_ff = w1.shape[2]
    tf = TF_MOE
    n_f = d_ff // tf
    chunk = lambda g, f, gr: jnp.where(gr[g] > 0, f, n_f - 1)
    grid_spec = pltpu.PrefetchScalarGridSpec(
        num_scalar_prefetch=2,
        grid=(n_groups, n_f),
        in_specs=[
            pl.BlockSpec((GROUP_ROWS, D_MODEL), lambda g, f, ge, gr: (g, 0)),
            pl.BlockSpec((None, D_MODEL, tf), lambda g, f, ge, gr: (ge[g], 0, chunk(g, f, gr))),
            pl.BlockSpec((None, D_MODEL, tf), lambda g, f, ge, gr: (ge[g], 0, chunk(g, f, gr))),
            pl.BlockSpec((None, tf, D_MODEL), lambda g, f, ge, gr: (ge[g], chunk(g, f, gr), 0)),
        ],
        out_specs=pl.BlockSpec((GROUP_ROWS, D_MODEL), lambda g, f, ge, gr: (g, 0)),
        scratch_shapes=[pltpu.VMEM((GROUP_ROWS, D_MODEL), F32),
                        pltpu.VMEM((D_MODEL, tf), BF16), pltpu.VMEM((D_MODEL, tf), BF16),
                        pltpu.VMEM((tf, D_MODEL), BF16)],
    )
    return pl.pallas_call(
        _experts_kernel,
        out_shape=jax.ShapeDtypeStruct(hs.shape, BF16),
        grid_spec=grid_spec,
        compiler_params=_cparams(("arbitrary", "arbitrary")),
        name="expert_swiglu",
    )(g_expert, g_rows, hs, w1, w3, w2)


def _combine_kernel(tbl_ref, seg_ref, off_ref, rows_hbm, pos_ref, gates_ref, x_ref, mod_ref, out_ref, buf_ref, sem):
    b = pl.program_id(0)
    n_seq = pl.num_programs(0)
    tr = TR_MOE

    def all_segments(seq, action):
        def body(e, first):
            off = off_ref[seq * N_EXPERTS + e]
            seg = seg_ref[seq * N_EXPERTS + e]

            def make_copy(r0, rows):
                return pltpu.make_async_copy(
                    rows_hbm.at[pl.ds(pl.multiple_of(off + r0, SEG_ALIGN), rows), :],
                    buf_ref.at[seq % 2, pl.ds(pl.multiple_of(first + r0, SEG_ALIGN), rows), :],
                    sem.at[seq % 2, e])
            _segment_copies(seg, make_copy, action)
            return first + seg
        lax.fori_loop(0, N_EXPERTS, body, 0)

    @pl.when(b == 0)
    def _():
        buf_ref[...] = jnp.zeros_like(buf_ref)
        all_segments(b, "start")

    @pl.when(b + 1 < n_seq)
    def _():
        all_segments(b + 1, "start")

    out_ref[...] = x_ref[...]
    lane = lax.broadcasted_iota(jnp.int32, (tr, LANES), 1)
    layer_gate = mod_ref[5:6, :]
    all_segments(b, "wait")

    def expert(e, seg_start):
        bounds, win, all_small = _block_windows(tbl_ref, b, e, SEG_ALIGN)

        def scatter(tb, first, height):
            ts = slice(tb * tr, (tb + 1) * tr)
            pos_col = jnp.sum(jnp.where(lane == e, pos_ref[ts, :], 0.0), axis=-1, keepdims=True)
            gate_col = jnp.sum(jnp.where(lane == e, gates_ref[ts, :], 0.0), axis=-1, keepdims=True)
            slot_id = lax.broadcasted_iota(jnp.int32, (tr, height), 1).astype(F32) + float(first)
            sel = jnp.where(pos_col - win[tb].astype(F32) == slot_id, 1.0, 0.0).astype(BF16)
            rows = buf_ref[b % 2, pl.ds(pl.multiple_of(seg_start + win[tb] + first, SEG_ALIGN), height), :]
            out_ref[ts, :] += layer_gate * (gate_col * _dot(sel, rows))

        @pl.when(all_small)
        def _():
            for tb in range(TOK_BLOCKS):
                scatter(tb, 0, SMALL_WINDOW)

        @pl.when(jnp.logical_not(all_small))
        def _():
            for tb in range(TOK_BLOCKS):
                scatter(tb, 0, tr)
            for tb in range(TOK_BLOCKS):
                @pl.when(bounds[tb + 1] - win[tb] > tr)
                def _():
                    scatter(tb, tr, SPILL_ROWS)
        return seg_start + seg_ref[b * N_EXPERTS + e]
    lax.fori_loop(0, N_EXPERTS, expert, 0)


def _combine(ys, pos, gates, x, mod, bounds, seg, seg_off):
    tokens = x.shape[0]
    grid_spec = pltpu.PrefetchScalarGridSpec(
        num_scalar_prefetch=3,
        grid=(tokens // SEQ,),
        in_specs=[pl.BlockSpec(memory_space=pl.ANY),
                  pl.BlockSpec((SEQ, LANES), lambda b, *_: (b, 0)),
                  pl.BlockSpec((SEQ, LANES), lambda b, *_: (b, 0)),
                  pl.BlockSpec((SEQ, D_MODEL), lambda b, *_: (b, 0)),
                  pl.BlockSpec((None, 6, D_MODEL), lambda b, *_: (b, 0, 0))],
        out_specs=pl.BlockSpec((SEQ, D_MODEL), lambda b, *_: (b, 0)),
        scratch_shapes=[pltpu.VMEM((2, PACKED_ROWS, D_MODEL), BF16),
                        pltpu.SemaphoreType.DMA((2, N_EXPERTS))],
    )
    return pl.pallas_call(
        _combine_kernel,
        out_shape=jax.ShapeDtypeStruct((tokens, D_MODEL), F32),
        grid_spec=grid_spec,
        compiler_params=_cparams(("arbitrary",)),
        name="expert_combine",
    )(bounds, seg, seg_off, ys, pos, gates, x, mod)


def _token_mixer(x, mod, gain, w_in, layer, q_gain, k_gain, ret_gain, bias_masks):
    *qkv, ret_in = _in_proj(x, mod, gain, w_in, layer, q_gain, k_gain)
    return _dilated_attention(qkv, bias_masks), _retention(ret_in, ret_gain)


def _out_proj_moe_ffn(attn, ret, x, mod, w_out, layer, gain, w_router, w1, w3, w2):
    x, h, gates, pos, post, start, cnt = _out_proj_router(attn, ret, x, mod, w_out, layer, gain, w_router)
    bounds, seg, seg_off, g_expert, g_rows, n_groups = _moe_plan(start, cnt)
    hs = _dispatch(h, post, bounds, seg, seg_off, n_groups * GROUP_ROWS)
    ys = _experts(hs, g_expert, g_rows, n_groups, w1, w3, w2)
    return _combine(ys, pos, gates, x, mod, bounds, seg, seg_off)


def kernel(x, c, rel_bias_table, norm_mix, norm_ffn, w_mod, b_mod, w_in, q_gain, k_gain, ret_gain, w_out,
           ffn_w1, ffn_w3, ffn_w2, moe_router, moe_w1, moe_w3, moe_w2):
    batch, seq, d_model = x.shape
    assert (seq, d_model) == (SEQ, D_MODEL)
    depth = w_mod.shape[0]
    mods = _modulation(c, w_mod, b_mod).reshape(depth, batch, 6, D_MODEL)
    bias_masks = _bias_masks(rel_bias_table)
    xt = x.reshape(batch * seq, d_model)
    for layer in range(depth):
        mod = mods[layer]
        attn, ret = _token_mixer(xt, mod, norm_mix[layer], w_in, layer, q_gain[layer], k_gain[layer],
                                 ret_gain[layer], bias_masks)
        i = layer // 2
        if layer % 2 == 0:
            xt = _out_proj_dense_ffn(attn, ret, xt, mod, w_out, layer, norm_ffn[layer],
                                     ffn_w1, ffn_w3, ffn_w2, i)
        else:
            xt = _out_proj_moe_ffn(attn, ret, xt, mod, w_out, layer, norm_ffn[layer],
                                   moe_router[i], moe_w1[i], moe_w3[i], moe_w2[i])
    return xt.reshape(batch, seq, d_model)
```
